```python
import jax, jax.numpy as jnp
from jax import lax
import numpy as np

D_MODEL = 1024
BATCH = 8
SEQ = 4096
DEPTH = 2

ML_HEADS = 4
ML_DIM = 128
ML_W = ML_HEADS * ML_DIM
ML_CONV = 4
ML_CHUNK = 64
GLA_HEADS = 4
GLA_DK = 64
GLA_DV = 128
GLA_K_W = GLA_HEADS * GLA_DK
GLA_V_W = GLA_HEADS * GLA_DV
GLA_RANK = 16
GLA_TAU = 16.0
GLA_CHUNK = 32
RET_HEADS = 4
RET_DIM = 128
RET_W = RET_HEADS * RET_DIM
RET_CHUNK = 64
ROPE_BASE = 10000.0
RW_HEADS = 8
RW_DIM = 64
RW_W = RW_HEADS * RW_DIM
RW_DECAY_RANK = 64
RW_A_RANK = 64
RW_GATE_RANK = 128
RW_LN_EPS = 64e-5
D_FF = 3584
N_EXPERTS = 8
TOP_K = 2
EPS = 1e-6

E_SPLITS = (2 * ML_W, ML_W, ML_W, 2 * ML_HEADS, GLA_K_W, GLA_K_W, GLA_V_W, GLA_V_W, GLA_RANK)
E_COLS = sum(E_SPLITS)
E_MIX_W = ML_W + GLA_V_W
RET_SPLITS = (RET_W, RET_W, RET_W, RET_W)
RET_COLS = sum(RET_SPLITS)
RW_SPLITS = (RW_W, RW_W, RW_W, RW_DECAY_RANK, RW_A_RANK, RW_GATE_RANK)
RW_COLS = sum(RW_SPLITS)
O_COLS = RET_COLS + RW_COLS
O_MIX_W = RET_W + RW_W

kernel_name = 'hybrid_mlstm_gla_retnet_rwkv7_moe'


def rms_norm(x, g):
    xf = x.astype(jnp.float32)
    y = xf * lax.rsqrt(jnp.mean(xf * xf, axis=-1, keepdims=True) + EPS)
    return (y * g).astype(x.dtype)


def head_layer_norm(x, eps):
    xf = x.astype(jnp.float32)
    xc = xf - jnp.mean(xf, axis=-1, keepdims=True)
    return xc * lax.rsqrt(jnp.mean(xc * xc, axis=-1, keepdims=True) + eps)


def head_rms_norm(x, eps):
    xf = x.astype(jnp.float32)
    return xf * lax.rsqrt(jnp.mean(xf * xf, axis=-1, keepdims=True) + eps)


def split_cols(p, sizes):
    idx = np.cumsum(sizes)[:-1].tolist()
    return jnp.split(p, idx, axis=-1)


def heads(a, n):
    return a.reshape(a.shape[:-1] + (n, -1)).astype(jnp.float32)


def token_shift(x):
    return jnp.pad(x[:, :-1], ((0, 0), (1, 0), (0, 0)))


def causal_depthwise_conv(x, w, b):
    c = x.shape[-1]
    y = lax.conv_general_dilated(x, w[:, None, :].astype(x.dtype), window_strides=(1,),
                                 padding=[(w.shape[0] - 1, 0)],
                                 dimension_numbers=('NWC', 'WIO', 'NWC'), feature_group_count=c)
    return y + b


def apply_rotary(x, positions):
    d = x.shape[-1]
    inv = ROPE_BASE ** (-jnp.arange(0, d, 2, dtype=jnp.float32) / d)
    ang = positions.astype(jnp.float32)[:, None] * inv[None, :]
    cos = jnp.cos(ang)[None, :, None, :]
    sin = jnp.sin(ang)[None, :, None, :]
    x1, x2 = x[..., : d // 2], x[..., d // 2:]
    return jnp.concatenate([x1 * cos - x2 * sin, x1 * sin + x2 * cos], axis=-1)


def to_chunks(a, L):
    b, t = a.shape[:2]
    a = a.reshape((b, t // L, L) + a.shape[2:])
    return jnp.moveaxis(a, (1, 3), (0, 2))


def from_chunks(a):
    a = jnp.moveaxis(a, (0, 2), (1, 3))
    return a.reshape((a.shape[0], a.shape[1] * a.shape[2]) + a.shape[3:])


def mlstm_chunkwise(q, k, v, i_pre, f_pre):
    bsz, t, nh, d = q.shape
    L = ML_CHUNK
    causal = jnp.tril(jnp.ones((L, L), dtype=bool))
    log_f = jax.nn.log_sigmoid(f_pre)
    xs = (to_chunks(q * d ** -0.5, L), to_chunks(k, L), to_chunks(v, L),
          to_chunks(i_pre, L), to_chunks(log_f, L))

    def step(carry, inp):
        c_st, n_st, m_st = carry
        qc, kc, vc, ic, fc = inp
        b = jnp.cumsum(fc, axis=-1)
        d_intra = jnp.where(causal, b[..., :, None] - b[..., None, :] + ic[..., None, :], -jnp.inf)
        d_inter = b + m_st[..., None]
        m_row = jnp.maximum(d_inter, jnp.max(d_intra, axis=-1))
        w_intra = jnp.exp(d_intra - m_row[..., None])
        w_inter = jnp.exp(d_inter - m_row)
        s = jnp.einsum('bhid,bhjd->bhij', qc, kc) * w_intra
        num = jnp.einsum('bhij,bhje->bhie', s, vc) + w_inter[..., None] * jnp.einsum('bhid,bhde->bhie', qc, c_st)
        den = jnp.sum(s, axis=-1) + w_inter * jnp.einsum('bhid,bhd->bhi', qc, n_st)
        h = num / jnp.maximum(jnp.abs(den), jnp.exp(-m_row))[..., None]
        g = b[..., -1]
        d_state = g[..., None] - b + ic
        m_new = jnp.maximum(g + m_st, jnp.max(d_state, axis=-1))
        w_state = jnp.exp(d_state - m_new[..., None])
        carry_decay = jnp.exp(g + m_st - m_new)
        c_st = carry_decay[..., None, None] * c_st + jnp.einsum('bhj,bhjd,bhje->bhde', w_state, kc, vc)
        n_st = carry_decay[..., None] * n_st + jnp.einsum('bhj,bhjd->bhd', w_state, kc)
        return (c_st, n_st, m_new), h

    init = (jnp.zeros((bsz, nh, d, v.shape[-1]), jnp.float32), jnp.zeros((bsz, nh, d), jnp.float32),
            jnp.zeros((bsz, nh), jnp.float32))
    _, h = lax.scan(step, init, xs)
    return from_chunks(h)


def gla_chunkwise(q, k, v, log_a):
    bsz, t, nh, dk = q.shape
    L = GLA_CHUNK
    causal = jnp.tril(jnp.ones((L, L), dtype=bool))[:, :, None]
    xs = (to_chunks(q, L), to_chunks(k, L), to_chunks(v, L), to_chunks(log_a, L))

    def step(s_st, inp):
        qc, kc, vc, ac = inp
        b = jnp.cumsum(ac, axis=2)
        diff = jnp.where(causal, b[:, :, :, None, :] - b[:, :, None, :, :], -jnp.inf)
        att = jnp.einsum('bhid,bhjd,bhijd->bhij', qc, kc, jnp.exp(diff))
        o = jnp.einsum('bhij,bhje->bhie', att, vc) + jnp.einsum('bhid,bhde->bhie', qc * jnp.exp(b), s_st)
        g = b[:, :, -1]
        s_st = jnp.exp(g)[..., None] * s_st + jnp.einsum('bhjd,bhje->bhde', kc * jnp.exp(g[:, :, None] - b), vc)
        return s_st, o

    init = jnp.zeros((bsz, nh, dk, v.shape[-1]), jnp.float32)
    _, o = lax.scan(step, init, xs)
    return from_chunks(o)


def retention_chunkwise(q, k, v, log_gamma):
    bsz, t, nh, d = q.shape
    L = RET_CHUNK
    idx = jnp.arange(L, dtype=jnp.float32)
    causal = idx[:, None] >= idx[None, :]
    rel = jnp.where(causal, idx[:, None] - idx[None, :], 0.0)
    intra_decay = jnp.where(causal, jnp.exp(log_gamma[:, None, None] * rel), 0.0)
    inter_decay = jnp.exp(log_gamma[:, None] * (idx + 1.0))[:, :, None]
    state_decay = jnp.exp(log_gamma[:, None] * (L - 1.0 - idx))[:, :, None]
    chunk_decay = jnp.exp(log_gamma * L)[:, None, None]
    xs = (to_chunks(q, L), to_chunks(k, L), to_chunks(v, L))

    def step(s_st, inp):
        qc, kc, vc = inp
        s = jnp.einsum('bhid,bhjd->bhij', qc, kc) * intra_decay
        o = jnp.einsum('bhij,bhje->bhie', s, vc) + inter_decay * jnp.einsum('bhid,bhde->bhie', qc, s_st)
        s_st = chunk_decay * s_st + jnp.einsum('bhjd,bhje->bhde', kc * state_decay, vc)
        return s_st, o

    init = jnp.zeros((bsz, nh, d, v.shape[-1]), jnp.float32)
    _, o = lax.scan(step, init, xs)
    return from_chunks(o)


def rwkv7_recurrence(r, w, k, v, a, b):
    bsz, t, nh, n = r.shape

    def step(s_st, inp):
        rt, wt, kt, vt, at, bt = inp
        sa = jnp.einsum('bhij,bhj->bhi', s_st, at)
        s_st = s_st * wt[:, :, None, :] + sa[..., None] * bt[:, :, None, :] + vt[..., None] * kt[:, :, None, :]
        return s_st, jnp.einsum('bhij,bhj->bhi', s_st, rt)

    xs = tuple(jnp.moveaxis(z, 1, 0) for z in (r, w, k, v, a, b))
    _, y = lax.scan(step, jnp.zeros((bsz, nh, n, n), jnp.float32), xs)
    return jnp.moveaxis(y, 0, 1)


def swiglu(h, w_gate, w_up, w_down):
    return (jax.nn.silu(h @ w_gate) * (h @ w_up)) @ w_down


def moe_swiglu(h, router, w_gate, w_up, w_down):
    logits = (h @ router).astype(jnp.float32)
    top_val, top_idx = lax.top_k(logits, TOP_K)
    top_w = jax.nn.softmax(top_val, axis=-1)
    gate = jnp.sum(jax.nn.one_hot(top_idx, N_EXPERTS, dtype=jnp.float32) * top_w[..., None], axis=-2)
    out = jnp.zeros(h.shape, jnp.float32)
    for e in range(N_EXPERTS):
        out = out + gate[..., e:e + 1] * swiglu(h, w_gate[e], w_up[e], w_down[e]).astype(jnp.float32)
    return out.astype(h.dtype)


def even_mixer(h, w_in, ml_conv_w, ml_conv_b, ml_gate_b, ml_norm_g, gla_gate_up, gla_gate_b, gla_norm_g, w_out):
    bsz, t, _ = h.shape
    p = h @ w_in
    ml_qk, ml_v, ml_o, ml_if, g_q, g_k, g_v, g_r, g_low = split_cols(p, E_SPLITS)
    ml_qk = jax.nn.silu(causal_depthwise_conv(ml_qk, ml_conv_w, ml_conv_b))
    q_ml, k_ml = jnp.split(ml_qk, 2, axis=-1)
    gates = (ml_if + ml_gate_b).astype(jnp.float32)
    h_ml = mlstm_chunkwise(heads(q_ml, ML_HEADS), heads(k_ml, ML_HEADS), heads(ml_v, ML_HEADS),
                           gates[..., :ML_HEADS], gates[..., ML_HEADS:])
    h_ml = jax.nn.sigmoid(heads(ml_o, ML_HEADS)) * h_ml
    h_ml = head_layer_norm(h_ml, EPS).reshape(bsz, t, ML_W) * ml_norm_g
    log_a = jax.nn.log_sigmoid((g_low @ gla_gate_up + gla_gate_b).astype(jnp.float32)) / GLA_TAU
    o_gla = gla_chunkwise(heads(g_q, GLA_HEADS) * GLA_DK ** -0.5, heads(g_k, GLA_HEADS), heads(g_v, GLA_HEADS),
                          log_a.reshape(bsz, t, GLA_HEADS, GLA_DK))
    o_gla = head_rms_norm(o_gla, EPS).reshape(bsz, t, GLA_V_W) * gla_norm_g * jax.nn.silu(g_r.astype(jnp.float32))
    mixed = jnp.concatenate([h_ml, o_gla], axis=-1).astype(h.dtype)
    return mixed @ w_out


def odd_mixer(h, w_in, ret_norm_g, rw_mu, rw_w_up, rw_w0, rw_a_up, rw_a0, rw_g_up, rw_k_k, rw_k_a, rw_r_k,
              rw_ln_g, rw_ln_b, w_out):
    bsz, t, _ = h.shape
    p = h @ w_in
    p_ret, p_rw = p[..., :RET_COLS], p[..., RET_COLS:]
    r_q, r_k, r_v, r_g = split_cols(p_ret, RET_SPLITS)
    positions = jnp.arange(t)
    log_gamma = jnp.log1p(-jnp.exp2(-5.0 - jnp.arange(RET_HEADS, dtype=jnp.float32)))
    q_ret = apply_rotary(heads(r_q, RET_HEADS), positions) * RET_DIM ** -0.5
    k_ret = apply_rotary(heads(r_k, RET_HEADS), positions)
    y_ret = retention_chunkwise(q_ret, k_ret, heads(r_v, RET_HEADS), log_gamma)
    y_ret = head_layer_norm(y_ret, EPS).reshape(bsz, t, RET_W) * ret_norm_g * jax.nn.silu(r_g.astype(jnp.float32))
    p_rw = (p_rw + (token_shift(p_rw) - p_rw) * rw_mu).astype(jnp.float32)
    x_r, x_k, x_v, x_dl, x_al, x_gl = split_cols(p_rw, RW_SPLITS)
    w_log = -jax.nn.softplus(-(rw_w0 + jnp.tanh(x_dl) @ rw_w_up)) - 0.5
    decay = jnp.exp(-jnp.exp(w_log))
    a = jax.nn.sigmoid(rw_a0 + x_al @ rw_a_up)
    g_out = jax.nn.sigmoid(x_gl) @ rw_g_up
    kk = heads(x_k * rw_k_k, RW_HEADS)
    kk = kk / jnp.maximum(jnp.linalg.norm(kk, axis=-1, keepdims=True), 1e-12)
    k_h = heads(x_k * (1.0 + (a - 1.0) * rw_k_a), RW_HEADS)
    r_h, v_h, a_h = heads(x_r, RW_HEADS), heads(x_v, RW_HEADS), heads(a, RW_HEADS)
    y_rw = rwkv7_recurrence(r_h, heads(decay, RW_HEADS), k_h, v_h, -kk, kk * a_h)
    y_rw = head_layer_norm(y_rw, RW_LN_EPS).reshape(bsz, t, RW_W) * rw_ln_g + rw_ln_b
    bonus = jnp.sum(r_h * k_h * rw_r_k, axis=-1, keepdims=True) * v_h
    y_rw = (y_rw + bonus.reshape(bsz, t, RW_W)) * g_out
    mixed = jnp.concatenate([y_ret, y_rw], axis=-1).astype(h.dtype)
    return mixed @ w_out


def setup_inputs(seed: int = 0) -> dict:
    key = jax.random.key(seed)
    ks = iter(jax.random.split(key, 64))
    f32 = jnp.float32
    ne, no = (DEPTH + 1) // 2, DEPTH // 2

    def normal(shape, scale):
        return jax.random.normal(next(ks), shape, f32) * scale

    def gain(n, dim):
        return 1.0 + normal((n, dim), 0.02)

    d = D_MODEL
    inp = {}
    inp['x'] = normal((BATCH, SEQ, d), 1.0)
    inp['e_norm1_g'] = gain(ne, d)
    inp['e_w_in'] = normal((ne, d, E_COLS), d ** -0.5)
    inp['e_ml_conv_w'] = normal((ne, ML_CONV, 2 * ML_W), 0.5)
    inp['e_ml_conv_b'] = normal((ne, 2 * ML_W), 0.02)
    inp['e_ml_gate_b'] = jnp.concatenate(
        [normal((ne, ML_HEADS), 0.1),
         jnp.linspace(3.0, 6.0, ML_HEADS, dtype=f32)[None, :] + normal((ne, ML_HEADS), 0.1)], axis=-1)
    inp['e_ml_norm_g'] = gain(ne, ML_W)
    inp['e_gla_gate_up'] = normal((ne, GLA_RANK, GLA_K_W), GLA_RANK ** -0.5)
    inp['e_gla_gate_b'] = normal((ne, GLA_K_W), 0.1)
    inp['e_gla_norm_g'] = gain(ne, GLA_V_W)
    inp['e_w_out'] = normal((ne, E_MIX_W, d), E_MIX_W ** -0.5)
    inp['e_norm2_g'] = gain(ne, d)
    inp['e_ffn_w_gate'] = normal((ne, d, D_FF), d ** -0.5)
    inp['e_ffn_w_up'] = normal((ne, d, D_FF), d ** -0.5)
    inp['e_ffn_w_down'] = normal((ne, D_FF, d), D_FF ** -0.5)
    inp['o_norm1_g'] = gain(no, d)
    inp['o_w_in'] = normal((no, d, O_COLS), d ** -0.5)
    inp['o_ret_norm_g'] = gain(no, RET_W)
    inp['o_rw_mu'] = jax.random.uniform(next(ks), (no, RW_COLS), f32)
    inp['o_rw_w_up'] = normal((no, RW_DECAY_RANK, RW_W), 0.1 * RW_DECAY_RANK ** -0.5)
    w0_ramp = jnp.broadcast_to(jnp.linspace(-6.0, -1.0, RW_DIM, dtype=f32), (RW_HEADS, RW_DIM)).reshape(-1)
    inp['o_rw_w0'] = w0_ramp[None, :] + normal((no, RW_W), 0.1)
    inp['o_rw_a_up'] = normal((no, RW_A_RANK, RW_W), 0.5 * RW_A_RANK ** -0.5)
    inp['o_rw_a0'] = normal((no, RW_W), 0.1)
    inp['o_rw_g_up'] = normal((no, RW_GATE_RANK, RW_W), RW_GATE_RANK ** -0.5)
    inp['o_rw_k_k'] = 0.85 + normal((no, RW_W), 0.02)
    inp['o_rw_k_a'] = 1.0 + normal((no, RW_W), 0.02)
    inp['o_rw_r_k'] = normal((no, RW_HEADS, RW_DIM), 0.1)
    inp['o_rw_ln_g'] = gain(no, RW_W)
    inp['o_rw_ln_b'] = normal((no, RW_W), 0.02)
    inp['o_w_out'] = normal((no, O_MIX_W, d), O_MIX_W ** -0.5)
    inp['o_norm2_g'] = gain(no, d)
    inp['o_moe_router'] = normal((no, d, N_EXPERTS), d ** -0.5)
    inp['o_moe_w_gate'] = normal((no, N_EXPERTS, d, D_FF), d ** -0.5)
    inp['o_moe_w_up'] = normal((no, N_EXPERTS, d, D_FF), d ** -0.5)
    inp['o_moe_w_down'] = normal((no, N_EXPERTS, D_FF, d), D_FF ** -0.5)
    inp['final_norm_g'] = 1.0 + normal((d,), 0.02)
    return inp


def reference(x, e_norm1_g, e_w_in, e_ml_conv_w, e_ml_conv_b, e_ml_gate_b, e_ml_norm_g, e_gla_gate_up,
              e_gla_gate_b, e_gla_norm_g, e_w_out, e_norm2_g, e_ffn_w_gate, e_ffn_w_up, e_ffn_w_down,
              o_norm1_g, o_w_in, o_ret_norm_g, o_rw_mu, o_rw_w_up, o_rw_w0, o_rw_a_up, o_rw_a0, o_rw_g_up,
              o_rw_k_k, o_rw_k_a, o_rw_r_k, o_rw_ln_g, o_rw_ln_b, o_w_out, o_norm2_g, o_moe_router,
              o_moe_w_gate, o_moe_w_up, o_moe_w_down, final_norm_g):
    h = x
    for layer in range(DEPTH):
        j = layer // 2
        if layer % 2 == 0:
            h = h + even_mixer(rms_norm(h, e_norm1_g[j]), e_w_in[j], e_ml_conv_w[j], e_ml_conv_b[j],
                               e_ml_gate_b[j], e_ml_norm_g[j], e_gla_gate_up[j], e_gla_gate_b[j],
                               e_gla_norm_g[j], e_w_out[j])
            h = h + swiglu(rms_norm(h, e_norm2_g[j]), e_ffn_w_gate[j], e_ffn_w_up[j], e_ffn_w_down[j])
        else:
            h = h + odd_mixer(rms_norm(h, o_norm1_g[j]), o_w_in[j], o_ret_norm_g[j], o_rw_mu[j], o_rw_w_up[j],
                              o_rw_w0[j], o_rw_a_up[j], o_rw_a0[j], o_rw_g_up[j], o_rw_k_k[j], o_rw_k_a[j],
                              o_rw_r_k[j], o_rw_ln_g[j], o_rw_ln_b[j], o_w_out[j])
            h = h + moe_swiglu(rms_norm(h, o_norm2_g[j]), o_moe_router[j], o_moe_w_gate[j], o_moe_w_up[j],
                               o_moe_w_down[j])
    return rms_norm(h, final_norm_g)
```

```python
import functools

import numpy as np
import jax
import jax.numpy as jnp
from jax import lax
from jax.experimental import pallas as pl
from jax.experimental.pallas import tpu as pltpu

F32 = jnp.float32
BF16 = jnp.bfloat16

D_MODEL = 1024
EPS = 1e-6
ML_HEADS, ML_DIM, ML_W, ML_CONV = 4, 128, 512, 4
GLA_HEADS, GLA_DK, GLA_DV, GLA_RANK, GLA_TAU = 4, 64, 128, 16, 16.0
E_COLS_PAD = 3840
RET_HEADS, RET_DIM, RET_W = 4, 128, 512
ROPE_BASE = 10000.0
RW_HEADS, RW_DIM, RW_W = 8, 64, 512
RW_COLS = 1792
RW_LN_EPS = 64e-5
D_FF = 3584
N_EXPERTS = 8
TOP_K = 2

VMEM_LIMIT = 48 * 1024 * 1024
NEG = -1e30


def _cp(*sem):
    return pltpu.CompilerParams(dimension_semantics=sem, vmem_limit_bytes=VMEM_LIMIT)


def _sigmoid(x):
    return 1.0 / (1.0 + jnp.exp(-x))


def _silu(x):
    return x * _sigmoid(x)


def _log_sigmoid(x):
    return jnp.minimum(x, 0.0) - jnp.log(1.0 + jnp.exp(-jnp.abs(x)))


def _dot(a, b):
    return jnp.dot(a.astype(BF16), b.astype(BF16), preferred_element_type=F32)


def _dot_nt(a, b):
    return lax.dot_general(a.astype(BF16), b.astype(BF16), (((1,), (1,)), ((), ())), preferred_element_type=F32)


def _dot_tn(a, b):
    return jnp.dot(a.T.astype(BF16), b.astype(BF16), preferred_element_type=F32)


def _rms_rows(x, g):
    ms = jnp.mean(x * x, axis=-1, keepdims=True)
    return x * lax.rsqrt(ms + EPS) * g


def _norm_matmul(x2d, g, w_bf16, tm=512, tn=768):
    n, d = x2d.shape
    c = w_bf16.shape[1]

    def body(x_ref, g_ref, w_ref, o_ref, xn_ref):
        @pl.when(pl.program_id(1) == 0)
        def _():
            xn_ref[...] = _rms_rows(x_ref[...], g_ref[...]).astype(BF16)

        o_ref[...] = jnp.dot(xn_ref[...], w_ref[...], preferred_element_type=F32)

    return pl.pallas_call(
        body,
        grid=(n // tm, c // tn),
        in_specs=[pl.BlockSpec((tm, d), lambda i, j: (i, 0)),
                  pl.BlockSpec((1, d), lambda i, j: (0, 0)),
                  pl.BlockSpec((d, tn), lambda i, j: (0, j))],
        out_specs=pl.BlockSpec((tm, tn), lambda i, j: (i, j)),
        out_shape=jax.ShapeDtypeStruct((n, c), F32),
        scratch_shapes=[pltpu.VMEM((tm, d), BF16)],
        compiler_params=_cp("parallel", "arbitrary"),
        name="norm_matmul",
    )(x2d, g.reshape(1, d), w_bf16)


def _mix_out(a, b, w_bf16, resid, tm=512):
    n, wa = a.shape
    wb = b.shape[1]
    d = w_bf16.shape[1]

    def body(a_ref, b_ref, w_ref, r_ref, o_ref):
        acc = jnp.dot(a_ref[...], w_ref[0:wa, :], preferred_element_type=F32)
        acc = acc + jnp.dot(b_ref[...], w_ref[wa:wa + wb, :], preferred_element_type=F32)
        o_ref[...] = r_ref[...] + acc

    return pl.pallas_call(
        body,
        grid=(n // tm,),
        in_specs=[pl.BlockSpec((tm, wa), lambda i: (i, 0)),
                  pl.BlockSpec((tm, wb), lambda i: (i, 0)),
                  pl.BlockSpec((wa + wb, d), lambda i: (0, 0)),
                  pl.BlockSpec((tm, d), lambda i: (i, 0))],
        out_specs=pl.BlockSpec((tm, d), lambda i: (i, 0)),
        out_shape=jax.ShapeDtypeStruct((n, d), F32),
        compiler_params=_cp("parallel"),
        name="mix_out",
    )(a, b, w_bf16, resid)


def _ffn(x2d, g, wg, wu, wd, tm=512, tf=512):
    n, d = x2d.shape
    f = wg.shape[1]
    nf = f // tf

    def body(x_ref, g_ref, wg_ref, wu_ref, wd_ref, o_ref, xn_ref, acc_ref):
        j = pl.program_id(1)

        @pl.when(j == 0)
        def _():
            xn_ref[...] = _rms_rows(x_ref[...], g_ref[...]).astype(BF16)
            acc_ref[...] = jnp.zeros_like(acc_ref)

        xn = xn_ref[...]
        gg = jnp.dot(xn, wg_ref[...], preferred_element_type=F32)
        uu = jnp.dot(xn, wu_ref[...], preferred_element_type=F32)
        act = (_silu(gg) * uu).astype(BF16)
        acc_ref[...] += jnp.dot(act, wd_ref[...], preferred_element_type=F32)

        @pl.when(j == nf - 1)
        def _():
            o_ref[...] = x_ref[...] + acc_ref[...]

    return pl.pallas_call(
        body,
        grid=(n // tm, nf),
        in_specs=[pl.BlockSpec((tm, d), lambda i, j: (i, 0)),
                  pl.BlockSpec((1, d), lambda i, j: (0, 0)),
                  pl.BlockSpec((d, tf), lambda i, j: (0, j)),
                  pl.BlockSpec((d, tf), lambda i, j: (0, j)),
                  pl.BlockSpec((tf, d), lambda i, j: (j, 0))],
        out_specs=pl.BlockSpec((tm, d), lambda i, j: (i, 0)),
        out_shape=jax.ShapeDtypeStruct((n, d), F32),
        scratch_shapes=[pltpu.VMEM((tm, d), BF16), pltpu.VMEM((tm, d), F32)],
        compiler_params=_cp("parallel", "arbitrary"),
        name="ffn",
    )(x2d, g.reshape(1, d), wg, wu, wd)


ML_TC = 256
ML_L = 64


def _mlstm(p0, gates_col, gates_row, gate_b, conv_w, conv_b, norm_g):
    bsz, t, _ = p0.shape
    tc, L, D = ML_TC, ML_L, ML_DIM
    nck = tc // L

    def body(q_ref, k_ref, v_ref, og_ref, gc_ref, gr_ref, gb_ref, cwq_ref, cwk_ref, cbq_ref, cbk_ref, ng_ref,
             out_ref, qext, kext, c_ref, n_ref, m_ref):
        c = pl.program_id(2)

        @pl.when(c == 0)
        def _():
            qext[0:8, :] = jnp.zeros((8, D), F32)
            kext[0:8, :] = jnp.zeros((8, D), F32)
            c_ref[...] = jnp.zeros_like(c_ref)
            n_ref[...] = jnp.zeros_like(n_ref)
            m_ref[...] = jnp.zeros_like(m_ref)

        qext[8:, :] = q_ref[0]
        kext[8:, :] = k_ref[0]

        def conv(ext, cw_ref, cb_ref):
            acc = cb_ref[...] + cw_ref[0:1, :] * ext[pl.ds(8 - ML_CONV + 1, tc), :]
            for kk in range(1, ML_CONV):
                acc = acc + cw_ref[kk:kk + 1, :] * ext[pl.ds(8 - ML_CONV + 1 + kk, tc), :]
            return _silu(acc)

        q_all = conv(qext, cwq_ref, cbq_ref) * (D ** -0.5)
        k_all = conv(kext, cwk_ref, cbk_ref)
        qext[0:8, :] = qext[tc:tc + 8, :]
        kext[0:8, :] = kext[tc:tc + 8, :]

        bi = gb_ref[0, :, 0:1]
        bf = gb_ref[0, :, 1:2]
        ri = lax.broadcasted_iota(jnp.int32, (L, L), 0)
        ci = lax.broadcasted_iota(jnp.int32, (L, L), 1)
        causal = ri >= ci

        c_st = c_ref[...]
        n_st = n_ref[...]
        m_st = m_ref[...]
        for j in range(nck):
            sl = slice(j * L, (j + 1) * L)
            q = q_all[sl]
            k = k_all[sl]
            v = v_ref[0, sl, :]
            i_col = gc_ref[0, 0, sl, 0:1] + bi
            f_col = _log_sigmoid(gc_ref[0, 0, sl, 1:2] + bf)
            i_row = gr_ref[0, 0, 0:1, sl] + bi
            f_row = _log_sigmoid(gr_ref[0, 0, 1:2, sl] + bf)
            b_col = jnp.sum(jnp.where(causal, f_row, 0.0), axis=1, keepdims=True)
            b_row = jnp.sum(jnp.where(causal, 0.0, f_col) + jnp.where(ri == ci, f_col, 0.0),
                            axis=0, keepdims=True)
            g_tot = jnp.sum(f_row, axis=1, keepdims=True)
            d_intra = jnp.where(causal, b_col - b_row + i_row, NEG)
            d_inter = b_col + m_st
            m_row = jnp.maximum(d_inter, jnp.max(d_intra, axis=1, keepdims=True))
            w_intra = jnp.exp(d_intra - m_row)
            w_inter = jnp.exp(d_inter - m_row)
            s = _dot_nt(q, k) * w_intra
            num = _dot(s, v) + w_inter * _dot(q, c_st)
            den = jnp.sum(s, axis=1, keepdims=True) + w_inter * jnp.sum(q * n_st, axis=1, keepdims=True)
            h = num / jnp.maximum(jnp.abs(den), jnp.exp(-m_row))
            d_state = g_tot - b_col + i_col
            m_new = jnp.maximum(g_tot + m_st, jnp.max(d_state, axis=0, keepdims=True))
            w_state = jnp.exp(d_state - m_new)
            carry = jnp.exp(g_tot + m_st - m_new)
            kw = k * w_state
            c_st = carry * c_st + _dot_tn(kw, v)
            n_st = carry * n_st + jnp.sum(kw, axis=0, keepdims=True)
            m_st = m_new
            hg = _sigmoid(og_ref[0, sl, :]) * h
            hc = hg - jnp.mean(hg, axis=1, keepdims=True)
            hn = hc * lax.rsqrt(jnp.mean(hc * hc, axis=1, keepdims=True) + EPS)
            out_ref[0, sl, :] = (hn * ng_ref[...]).astype(BF16)
        c_ref[...] = c_st
        n_ref[...] = n_st
        m_ref[...] = m_st

    hq = ML_HEADS
    col = lambda off: (lambda b, h, c: (b, c, off + h))
    return pl.pallas_call(
        body,
        grid=(bsz, ML_HEADS, t // tc),
        in_specs=[pl.BlockSpec((1, tc, D), col(0)),
                  pl.BlockSpec((1, tc, D), col(hq)),
                  pl.BlockSpec((1, tc, D), col(2 * hq)),
                  pl.BlockSpec((1, tc, D), col(3 * hq)),
                  pl.BlockSpec((1, 1, tc, 2), lambda b, h, c: (b, h, c, 0)),
                  pl.BlockSpec((1, 1, 2, tc), lambda b, h, c: (b, h, 0, c)),
                  pl.BlockSpec((1, 1, 2), lambda b, h, c: (h, 0, 0)),
                  pl.BlockSpec((ML_CONV, D), lambda b, h, c: (0, h)),
                  pl.BlockSpec((ML_CONV, D), lambda b, h, c: (0, hq + h)),
                  pl.BlockSpec((1, D), lambda b, h, c: (0, h)),
                  pl.BlockSpec((1, D), lambda b, h, c: (0, hq + h)),
                  pl.BlockSpec((1, D), lambda b, h, c: (0, h))],
        out_specs=pl.BlockSpec((1, tc, D), lambda b, h, c: (b, c, h)),
        out_shape=jax.ShapeDtypeStruct((bsz, t, ML_W), BF16),
        scratch_shapes=[pltpu.VMEM((tc + 8, D), F32), pltpu.VMEM((tc + 8, D), F32),
                        pltpu.VMEM((D, D), F32), pltpu.VMEM((1, D), F32), pltpu.VMEM((1, 1), F32)],
        compiler_params=_cp("parallel", "parallel", "arbitrary"),
        name="mlstm",
    )(p0, p0, p0, p0, gates_col, gates_row, gate_b, conv_w, conv_w, conv_b, conv_b, norm_g)


GLA_TC = 256
GLA_SUB = 16


def _gla(p0, g_low, gate_up, gate_b, norm_g):
    bsz, t, _ = p0.shape
    tc, S = GLA_TC, GLA_SUB
    nsub = tc // S
    dk, dv = GLA_DK, GLA_DV

    def body(q_ref, k_ref, v_ref, gr_ref, gl_ref, gu_ref, gbias_ref, ng_ref, out_ref,
             st_ref, qt_scr, kt_scr, eg_scr, o_scr):
        c = pl.program_id(2)

        @pl.when(c == 0)
        def _():
            st_ref[...] = jnp.zeros_like(st_ref)

        z = _dot(gl_ref[0], gu_ref[...]) + gbias_ref[...]
        la = _log_sigmoid(z) / GLA_TAU
        rmod = lax.broadcasted_iota(jnp.int32, (tc, 1), 0) & (S - 1)
        bcum = la
        rsum = la
        for s in (1, 2, 4, 8):
            bcum = bcum + jnp.where(rmod >= s, pltpu.roll(bcum, s, 0), 0.0)
            rsum = rsum + jnp.where(rmod < S - s, pltpu.roll(rsum, tc - s, 0), 0.0)
        q = q_ref[0] * (dk ** -0.5)
        k = k_ref[0]
        v = v_ref[0]
        qt_scr[...] = q * jnp.exp(bcum)
        kt_scr[...] = k * jnp.exp(rsum - la)
        eg_scr[...] = jnp.exp(bcum + rsum - la)

        o0 = jnp.zeros((tc, dv), F32)
        o1 = jnp.zeros((tc, dv), F32)
        for d in range(S):
            if d == 0:
                kd, bd, vd = k, bcum, v
            else:
                kd = pltpu.roll(k, d, 0)
                bd = pltpu.roll(bcum, d, 0)
                vd = pltpu.roll(v, d, 0)
            valid = rmod >= d
            e = jnp.exp(jnp.where(valid, bcum - bd, 0.0))
            prod = q * kd * e
            w0 = jnp.where(valid, jnp.sum(prod[:, 0:dk], axis=1, keepdims=True), 0.0)
            w1 = jnp.where(valid, jnp.sum(prod[:, dk:2 * dk], axis=1, keepdims=True), 0.0)
            o0 = o0 + w0 * vd[:, 0:dv]
            o1 = o1 + w1 * vd[:, dv:2 * dv]
        o_scr[:, 0:dv] = o0
        o_scr[:, dv:2 * dv] = o1

        def sub(si, carry):
            st0, st1 = carry
            r0 = pl.multiple_of(si * S, S)
            qs = qt_scr[pl.ds(r0, S), :]
            ks = kt_scr[pl.ds(r0, S), :]
            vs = v_ref[0, pl.ds(r0, S), :]
            eg = eg_scr[pl.ds(r0, 1), :]
            new = []
            for hh, st in ((0, st0), (1, st1)):
                lk = slice(hh * dk, (hh + 1) * dk)
                lv = slice(hh * dv, (hh + 1) * dv)
                o_scr[pl.ds(r0, S), lv] += _dot_nt(qs[:, lk], st)
                new.append(st * eg[:, lk] + _dot_tn(vs[:, lv], ks[:, lk]))
            return tuple(new)

        st0, st1 = lax.fori_loop(0, nsub, sub, (st_ref[0], st_ref[1]))
        st_ref[0] = st0
        st_ref[1] = st1

        for hh in range(2):
            lv = slice(hh * dv, (hh + 1) * dv)
            o = o_scr[:, lv]
            on = o * lax.rsqrt(jnp.mean(o * o, axis=1, keepdims=True) + EPS)
            out_ref[0, :, lv] = (on * ng_ref[:, lv] * _silu(gr_ref[0, :, lv])).astype(BF16)

    return pl.pallas_call(
        body,
        grid=(bsz, GLA_HEADS // 2, t // tc),
        in_specs=[pl.BlockSpec((1, tc, 2 * dk), lambda b, h, c: (b, c, 16 + h)),
                  pl.BlockSpec((1, tc, 2 * dk), lambda b, h, c: (b, c, 18 + h)),
                  pl.BlockSpec((1, tc, 2 * dv), lambda b, h, c: (b, c, 10 + h)),
                  pl.BlockSpec((1, tc, 2 * dv), lambda b, h, c: (b, c, 12 + h)),
                  pl.BlockSpec((1, tc, GLA_RANK), lambda b, h, c: (b, c, 0)),
                  pl.BlockSpec((GLA_RANK, 2 * dk), lambda b, h, c: (0, h)),
                  pl.BlockSpec((1, 2 * dk), lambda b, h, c: (0, h)),
                  pl.BlockSpec((1, 2 * dv), lambda b, h, c: (0, h))],
        out_specs=pl.BlockSpec((1, tc, 2 * dv), lambda b, h, c: (b, c, h)),
        out_shape=jax.ShapeDtypeStruct((bsz, t, GLA_HEADS * dv), BF16),
        scratch_shapes=[pltpu.VMEM((2, dv, dk), F32), pltpu.VMEM((tc, 2 * dk), F32), pltpu.VMEM((tc, 2 * dk), F32),
                        pltpu.VMEM((tc, 2 * dk), F32), pltpu.VMEM((tc, 2 * dv), F32)],
        compiler_params=_cp("parallel", "parallel", "arbitrary"),
        name="gla",
    )(p0, p0, p0, p0, g_low, gate_up, gate_b, norm_g)


RET_L = 128


def _retention(p1, cos_t, sin_t, intra, inter, sdec, cdec, norm_g):
    bsz, t, _ = p1.shape
    L, D = RET_L, RET_DIM
    base = RW_COLS // D

    def body(q_ref, k_ref, v_ref, g_ref, cos_ref, sin_ref, intra_ref, inter_ref, sdec_ref, cdec_ref, ng_ref,
             out_ref, s_ref):
        c = pl.program_id(2)

        @pl.when(c == 0)
        def _():
            s_ref[...] = jnp.zeros_like(s_ref)

        cs = cos_ref[...]
        sn = sin_ref[...]

        def rot(x):
            return x * cs + pltpu.roll(x, D // 2, 1) * sn

        q = rot(q_ref[0]) * (D ** -0.5)
        k = rot(k_ref[0])
        v = v_ref[0]
        s_st = s_ref[...]
        s = _dot_nt(q, k) * intra_ref[0]
        o = _dot(s, v) + inter_ref[0] * _dot(q, s_st)
        s_ref[...] = cdec_ref[0] * s_st + _dot_tn(k * sdec_ref[0], v)
        oc = o - jnp.mean(o, axis=1, keepdims=True)
        on = oc * lax.rsqrt(jnp.mean(oc * oc, axis=1, keepdims=True) + EPS)
        out_ref[0] = (on * ng_ref[...] * _silu(g_ref[0])).astype(BF16)

    hq = RET_HEADS
    col = lambda off: (lambda b, h, c: (b, c, base + off + h))
    return pl.pallas_call(
        body,
        grid=(bsz, RET_HEADS, t // L),
        in_specs=[pl.BlockSpec((1, L, D), col(0)),
                  pl.BlockSpec((1, L, D), col(hq)),
                  pl.BlockSpec((1, L, D), col(2 * hq)),
                  pl.BlockSpec((1, L, D), col(3 * hq)),
                  pl.BlockSpec((L, D), lambda b, h, c: (c, 0)),
                  pl.BlockSpec((L, D), lambda b, h, c: (c, 0)),
                  pl.BlockSpec((1, L, L), lambda b, h, c: (h, 0, 0)),
                  pl.BlockSpec((1, L, 1), lambda b, h, c: (h, 0, 0)),
                  pl.BlockSpec((1, L, 1), lambda b, h, c: (h, 0, 0)),
                  pl.BlockSpec((1, 1, 1), lambda b, h, c: (h, 0, 0)),
                  pl.BlockSpec((1, D), lambda b, h, c: (0, h))],
        out_specs=pl.BlockSpec((1, L, D), lambda b, h, c: (b, c, h)),
        out_shape=jax.ShapeDtypeStruct((bsz, t, RET_W), BF16),
        scratch_shapes=[pltpu.VMEM((D, D), F32)],
        compiler_params=_cp("parallel", "parallel", "arbitrary"),
        name="retention",
    )(p1, p1, p1, p1, cos_t, sin_t, intra, inter, sdec, cdec, norm_g)


def _retention_tables(t):
    L, D = RET_L, RET_DIM
    inv = ROPE_BASE ** (-jnp.arange(0, D, 2, dtype=F32) / D)
    ang = jnp.arange(t).astype(F32)[:, None] * inv[None, :]
    cos = jnp.cos(ang)
    sin = jnp.sin(ang)
    cos_t = jnp.concatenate([cos, cos], axis=-1)
    sin_t = jnp.concatenate([-sin, sin], axis=-1)
    log_gamma = jnp.log1p(-jnp.exp2(-5.0 - jnp.arange(RET_HEADS, dtype=F32)))
    idx = jnp.arange(L, dtype=F32)
    causal = idx[:, None] >= idx[None, :]
    rel = jnp.where(causal, idx[:, None] - idx[None, :], 0.0)
    intra = jnp.where(causal, jnp.exp(log_gamma[:, None, None] * rel), 0.0)
    inter = jnp.exp(log_gamma[:, None] * (idx + 1.0))[:, :, None]
    sdec = jnp.exp(log_gamma[:, None] * (L - 1.0 - idx))[:, :, None]
    cdec = jnp.exp(log_gamma * L)[:, None, None]
    return cos_t, sin_t, intra, inter, sdec, cdec


RWP_TM = 256


def _seg_sum(x, bd):
    hi = x.astype(BF16)
    lo = (x - hi.astype(F32)).astype(BF16)
    return jnp.dot(hi, bd, preferred_element_type=F32) + jnp.dot(lo, bd, preferred_element_type=F32)


def _rwkv_prep(p1, mu, w_up, w0, a_up, a0, g_up, k_k, k_a, r_k, bd):
    bsz, t, _ = p1.shape
    tm, W = RWP_TM, RW_W

    def body(x_ref, xp_ref, mu_ref, wup_ref, w0_ref, aup_ref, a0_ref, gup_ref, kk_ref, ka_ref, rk_ref, bd_ref,
             r_out, lw_out, k_out, v_out, a_out, b_out, g_out, bonus_out):
        c = pl.program_id(1)
        cur = x_ref[0]
        prev = jnp.where(c == 0, 0.0, xp_ref[0, 7:8, :])
        rowi = lax.broadcasted_iota(jnp.int32, (tm, 1), 0)
        sh = jnp.where(rowi == 0, prev, pltpu.roll(cur, 1, 0))
        xm = cur + (sh - cur) * mu_ref[...]
        x_r = xm[:, 0:W]
        x_k = xm[:, W:2 * W]
        x_v = xm[:, 2 * W:3 * W]
        x_dl = xm[:, 3 * W:3 * W + 64]
        x_al = xm[:, 3 * W + 64:3 * W + 128]
        x_gl = xm[:, 3 * W + 128:3 * W + 256]
        wl = w0_ref[...] + _dot(jnp.tanh(x_dl), wup_ref[...])
        sp = jnp.maximum(-wl, 0.0) + jnp.log(1.0 + jnp.exp(-jnp.abs(wl)))
        lw_out[0] = -jnp.exp(-sp - 0.5)
        a = _sigmoid(a0_ref[...] + _dot(x_al, aup_ref[...]))
        g_out[0] = _dot(_sigmoid(x_gl), gup_ref[...])
        kk0 = x_k * kk_ref[...]
        nrm = jnp.sqrt(_seg_sum(kk0 * kk0, bd_ref[...]))
        kk = kk0 / jnp.maximum(nrm, 1e-12)
        k_h = x_k * (1.0 + (a - 1.0) * ka_ref[...])
        r_out[0] = x_r
        k_out[0] = k_h
        v_out[0] = x_v
        a_out[0] = -kk
        b_out[0] = kk * a
        bonus_out[0] = _seg_sum(x_r * k_h * rk_ref[...], bd_ref[...]) * x_v

    row = lambda b, c: (0, 0)
    blk = pl.BlockSpec((1, tm, W), lambda b, c: (b, c, 0))
    shp = jax.ShapeDtypeStruct((bsz, t, W), F32)
    return pl.pallas_call(
        body,
        grid=(bsz, t // tm),
        in_specs=[pl.BlockSpec((1, tm, RW_COLS), lambda b, c: (b, c, 0)),
                  pl.BlockSpec((1, 8, RW_COLS), lambda b, c: (b, jnp.maximum(c * (tm // 8) - 1, 0), 0)),
                  pl.BlockSpec((1, RW_COLS), row),
                  pl.BlockSpec((64, W), row), pl.BlockSpec((1, W), row),
                  pl.BlockSpec((64, W), row), pl.BlockSpec((1, W), row),
                  pl.BlockSpec((128, W), row),
                  pl.BlockSpec((1, W), row), pl.BlockSpec((1, W), row), pl.BlockSpec((1, W), row),
                  pl.BlockSpec((W, W), row)],
        out_specs=[blk] * 8,
        out_shape=[shp] * 8,
        compiler_params=_cp("parallel", "arbitrary"),
        name="rwkv_prep",
    )(p1, p1, mu, w_up, w0, a_up, a0, g_up, k_k, k_a, r_k, bd)


RW_L = 64


def _rwkv_scan(r, lw, k, v, aa, bb, g_out, bonus, ln_g, ln_b):
    bsz, t, W = r.shape
    L, N = RW_L, RW_DIM

    def body(r_ref, lw_ref, k_ref, v_ref, a_ref, b_ref, g_ref, bonus_ref, lng_ref, lnb_ref, out_ref, h_ref):
        c = pl.program_id(1)

        @pl.when(c == 0)
        def _():
            h_ref[...] = jnp.zeros_like(h_ref)

        lw_all = lw_ref[0]
        rowi = lax.broadcasted_iota(jnp.int32, (L, 1), 0)
        cl_all = lw_all
        for s in (1, 2, 4, 8, 16, 32):
            cl_all = cl_all + jnp.where(rowi >= s, pltpu.roll(cl_all, s, 0), 0.0)
        ri = lax.broadcasted_iota(jnp.int32, (L, L), 0)
        ci = lax.broadcasted_iota(jnp.int32, (L, L), 1)
        strict = ri > ci
        lower = ri >= ci
        eye = ri == ci
        outs = []
        for h in range(RW_HEADS):
            ls = slice(h * N, (h + 1) * N)
            cl = cl_all[:, ls]
            cl_last = cl[L - 1:L, :]
            e_in = jnp.exp(cl)
            e_ex = jnp.exp(cl - lw_all[:, ls])
            e_inv = jnp.exp(-cl)
            e_end = jnp.exp(cl_last - cl)
            rr, kk, vv, ah, bh = r_ref[0, :, ls], k_ref[0, :, ls], v_ref[0, :, ls], a_ref[0, :, ls], b_ref[0, :, ls]
            at = ah * e_ex
            rt = rr * e_in
            bt = bh * e_inv
            kt = kk * e_inv
            m = _dot_nt(jnp.concatenate([at, rt], axis=0), jnp.concatenate([bt, kt], axis=0))
            a_ab = jnp.where(strict, m[0:L, 0:L], 0.0)
            a_ak = jnp.where(strict, m[0:L, L:2 * L], 0.0)
            a_rb = jnp.where(lower, m[L:2 * L, 0:L], 0.0)
            a_rk = jnp.where(lower, m[L:2 * L, L:2 * L], 0.0)
            x = jnp.concatenate([at, _dot(a_ak, vv)], axis=1)
            ap = a_ab
            for it in range(6):
                x = x + _dot(ap, x)
                if it < 5:
                    ap = _dot(ap, ap)
            a_hat = x[:, 0:N]
            uv = x[:, N:2 * N]
            q_hat = rt + _dot(a_rb, a_hat)
            y0 = _dot(a_rb, uv) + _dot(a_rk, vv)
            b_end = bh * e_end
            k_end = kk * e_end
            gmat = jnp.where(eye, jnp.exp(cl_last), 0.0) + _dot_tn(b_end, a_hat)
            h_add = _dot_tn(b_end, uv) + _dot_tn(k_end, vv)
            h_st = h_ref[h]
            y = _dot(q_hat, h_st) + y0
            h_ref[h] = _dot(gmat, h_st) + h_add
            yc = y - jnp.mean(y, axis=1, keepdims=True)
            outs.append(yc * lax.rsqrt(jnp.mean(yc * yc, axis=1, keepdims=True) + RW_LN_EPS))
        yn = jnp.concatenate(outs, axis=1)
        out_ref[0] = ((yn * lng_ref[...] + lnb_ref[...] + bonus_ref[0]) * g_ref[0]).astype(BF16)

    blk = pl.BlockSpec((1, L, W), lambda b, c: (b, c, 0))
    vec = pl.BlockSpec((1, W), lambda b, c: (0, 0))
    return pl.pallas_call(
        body,
        grid=(bsz, t // L),
        in_specs=[blk] * 8 + [vec, vec],
        out_specs=blk,
        out_shape=jax.ShapeDtypeStruct((bsz, t, W), BF16),
        scratch_shapes=[pltpu.VMEM((RW_HEADS, N, N), F32)],
        compiler_params=_cp("parallel", "arbitrary"),
        name="rwkv_scan",
    )(r, lw, k, v, aa, bb, g_out, bonus, ln_g, ln_b)


def _router(x2d, g, router_pad, tm=512):
    n, d = x2d.shape
    e = router_pad.shape[1]

    def body(x_ref, g_ref, r_ref, xn_ref, lg_ref):
        xn = _rms_rows(x_ref[...], g_ref[...])
        xn_ref[...] = xn.astype(BF16)
        lg_ref[...] = jnp.dot(xn, r_ref[...], preferred_element_type=F32, precision=lax.Precision.HIGHEST)

    return pl.pallas_call(
        body,
        grid=(n // tm,),
        in_specs=[pl.BlockSpec((tm, d), lambda i: (i, 0)),
                  pl.BlockSpec((1, d), lambda i: (0, 0)),
                  pl.BlockSpec((d, e), lambda i: (0, 0))],
        out_specs=[pl.BlockSpec((tm, d), lambda i: (i, 0)), pl.BlockSpec((tm, e), lambda i: (i, 0))],
        out_shape=[jax.ShapeDtypeStruct((n, d), BF16), jax.ShapeDtypeStruct((n, e), F32)],
        compiler_params=_cp("parallel"),
        name="moe_router",
    )(x2d, g.reshape(1, d), router_pad)


MOE_TM = 512
MOE_TF = 512


def _experts(xs, row_w, tile_expert, n_used, wg, wu, wd):
    npad, d = xs.shape
    tm, tf = MOE_TM, MOE_TF
    nf = D_FF // tf
    n_tiles = npad // tm

    def body(te_ref, nu_ref, x_ref, w_ref, wg_ref, wu_ref, wd_ref, o_ref):
        i = pl.program_id(0)
        j = pl.program_id(1)

        @pl.when(j == 0)
        def _():
            o_ref[...] = jnp.zeros_like(o_ref)

        @pl.when(i < nu_ref[0])
        def _():
            x = x_ref[...]
            gg = jnp.dot(x, wg_ref[0], preferred_element_type=F32)
            uu = jnp.dot(x, wu_ref[0], preferred_element_type=F32)
            act = (_silu(gg) * uu).astype(BF16)
            o_ref[...] += jnp.dot(act, wd_ref[0], preferred_element_type=F32)

        @pl.when(j == nf - 1)
        def _():
            o_ref[...] = o_ref[...] * w_ref[...]

    grid_spec = pltpu.PrefetchScalarGridSpec(
        num_scalar_prefetch=2,
        grid=(n_tiles, nf),
        in_specs=[pl.BlockSpec((tm, d), lambda i, j, te, nu: (i, 0)),
                  pl.BlockSpec((tm, 1), lambda i, j, te, nu: (i, 0)),
                  pl.BlockSpec((1, d, tf), lambda i, j, te, nu: (te[i], 0, j)),
                  pl.BlockSpec((1, d, tf), lambda i, j, te, nu: (te[i], 0, j)),
                  pl.BlockSpec((1, tf, d), lambda i, j, te, nu: (te[i], j, 0))],
        out_specs=pl.BlockSpec((tm, d), lambda i, j, te, nu: (i, 0)),
    )
    return pl.pallas_call(
        body,
        grid_spec=grid_spec,
        out_shape=jax.ShapeDtypeStruct((npad, d), F32),
        compiler_params=_cp("parallel", "arbitrary"),
        name="moe_experts",
    )(tile_expert, n_used, xs, row_w, wg, wu, wd)


def _combine_norm(h, y0, y1, g, tm=512):
    n, d = h.shape

    def body(h_ref, a_ref, b_ref, g_ref, o_ref):
        o_ref[...] = _rms_rows(h_ref[...] + (a_ref[...] + b_ref[...]), g_ref[...])

    blk = pl.BlockSpec((tm, d), lambda i: (i, 0))
    return pl.pallas_call(
        body,
        grid=(n // tm,),
        in_specs=[blk, blk, blk, pl.BlockSpec((1, d), lambda i: (0, 0))],
        out_specs=blk,
        out_shape=jax.ShapeDtypeStruct((n, d), F32),
        compiler_params=_cp("parallel"),
        name="combine_norm",
    )(h, y0, y1, g.reshape(1, d))


def _route(logits, n):
    tm = MOE_TM
    top_val, top_idx = lax.top_k(logits, TOP_K)
    top_w = jax.nn.softmax(top_val, axis=-1)
    e_flat = top_idx.reshape(-1).astype(jnp.int32)
    w_flat = top_w.reshape(-1)
    onehot = (e_flat[:, None] == jnp.arange(N_EXPERTS, dtype=jnp.int32)[None, :]).astype(jnp.int32)
    rank = jnp.take_along_axis(jnp.cumsum(onehot, axis=0), e_flat[:, None], axis=1)[:, 0] - 1
    counts = jnp.sum(onehot, axis=0)
    padded = ((counts + tm - 1) // tm) * tm
    ends = jnp.cumsum(padded)
    starts = ends - padded
    pos = starts[e_flat] + rank
    npad = n * TOP_K + N_EXPERTS * tm
    tok = jnp.arange(n * TOP_K, dtype=jnp.int32) // TOP_K
    sorted_tok = jnp.zeros((npad,), jnp.int32).at[pos].set(tok)
    sorted_w = jnp.zeros((npad,), F32).at[pos].set(w_flat)
    tile_start = jnp.arange(npad // tm, dtype=jnp.int32) * tm
    tile_expert = jnp.minimum(jnp.searchsorted(ends, tile_start, side="right"), N_EXPERTS - 1).astype(jnp.int32)
    n_used = (ends[-1] // tm).astype(jnp.int32).reshape(1)
    return sorted_tok, sorted_w, tile_expert, n_used, pos.reshape(n, TOP_K)


def kernel(x, e_norm1_g, e_w_in, e_ml_conv_w, e_ml_conv_b, e_ml_gate_b, e_ml_norm_g, e_gla_gate_up, e_gla_gate_b,
           e_gla_norm_g, e_w_out, e_norm2_g, e_ffn_w_gate, e_ffn_w_up, e_ffn_w_down, o_norm1_g, o_w_in,
           o_ret_norm_g, o_rw_mu, o_rw_w_up, o_rw_w0, o_rw_a_up, o_rw_a0, o_rw_g_up, o_rw_k_k, o_rw_k_a, o_rw_r_k,
           o_rw_ln_g, o_rw_ln_b, o_w_out, o_norm2_g, o_moe_router, o_moe_w_gate, o_moe_w_up, o_moe_w_down,
           final_norm_g):
    bsz, t, d = x.shape
    n = bsz * t
    h0 = x.reshape(n, d)

    w = e_w_in[0]
    w0 = jnp.concatenate([w[:, :2048], w[:, 2056:3592], w[:, 2048:2056], w[:, 3592:3608],
                          jnp.zeros((d, E_COLS_PAD - 3608), F32)], axis=1).astype(BF16)
    p0 = _norm_matmul(h0, e_norm1_g[0], w0).reshape(bsz, t, E_COLS_PAD)
    gates = p0[:, :, 3584:3592]
    gates_h = gates.reshape(bsz, t, 2, ML_HEADS)
    gates_col = jnp.transpose(gates_h, (0, 3, 1, 2))
    gates_row = jnp.transpose(gates_h, (0, 3, 2, 1))
    gate_b = jnp.transpose(e_ml_gate_b[0].reshape(2, ML_HEADS), (1, 0)).reshape(ML_HEADS, 1, 2)
    h_ml = _mlstm(p0, gates_col, gates_row, gate_b, e_ml_conv_w[0], e_ml_conv_b[0].reshape(1, -1),
                  e_ml_norm_g[0].reshape(1, -1))
    g_low = p0[:, :, 3592:3608]
    o_gla = _gla(p0, g_low, e_gla_gate_up[0].astype(BF16), e_gla_gate_b[0].reshape(1, -1),
                 e_gla_norm_g[0].reshape(1, -1))
    h1 = _mix_out(h_ml.reshape(n, -1), o_gla.reshape(n, -1), e_w_out[0].astype(BF16), h0)
    h2 = _ffn(h1, e_norm2_g[0], e_ffn_w_gate[0].astype(BF16), e_ffn_w_up[0].astype(BF16),
              e_ffn_w_down[0].astype(BF16))

    w = o_w_in[0]
    w1 = jnp.concatenate([w[:, 2048:], w[:, :2048]], axis=1).astype(BF16)
    p1 = _norm_matmul(h2, o_norm1_g[0], w1).reshape(bsz, t, -1)
    y_ret = _retention(p1, *_retention_tables(t), o_ret_norm_g[0].reshape(1, -1))
    head_of = jnp.arange(RW_W) // RW_DIM
    bd = (head_of[:, None] == head_of[None, :]).astype(BF16)
    row = lambda a: a.reshape(1, -1)
    r, lw, k, v, aa, bb, g_out, bonus = _rwkv_prep(
        p1, row(o_rw_mu[0]), o_rw_w_up[0].astype(BF16), row(o_rw_w0[0]), o_rw_a_up[0].astype(BF16),
        row(o_rw_a0[0]), o_rw_g_up[0].astype(BF16), row(o_rw_k_k[0]), row(o_rw_k_a[0]), row(o_rw_r_k[0]), bd)
    y_rw = _rwkv_scan(r, lw, k, v, aa, bb, g_out, bonus, row(o_rw_ln_g[0]), row(o_rw_ln_b[0]))
    h3 = _mix_out(y_ret.reshape(n, -1), y_rw.reshape(n, -1), o_w_out[0].astype(BF16), h2)

    router_pad = jnp.zeros((d, 128), F32).at[:, :N_EXPERTS].set(o_moe_router[0])
    xn, logits = _router(h3, o_norm2_g[0], router_pad)
    sorted_tok, sorted_w, tile_expert, n_used, pos = _route(logits[:, :N_EXPERTS], n)
    xs = jnp.take(xn, sorted_tok, axis=0)
    ys = _experts(xs, sorted_w.reshape(-1, 1), tile_expert, n_used, o_moe_w_gate[0].astype(BF16),
                  o_moe_w_up[0].astype(BF16), o_moe_w_down[0].astype(BF16))
    y0 = jnp.take(ys, pos[:, 0], axis=0)
    y1 = jnp.take(ys, pos[:, 1], axis=0)
    out = _combine_norm(h3, y0, y1, final_norm_g)
    return out.reshape(bsz, t, d)
```

```python
import functools

import numpy as np
import jax
import jax.numpy as jnp
from jax import lax
from jax.experimental import pallas as pl
from jax.experimental.pallas import tpu as pltpu

F32 = jnp.float32
BF16 = jnp.bfloat16

D_MODEL = 1024
EPS = 1e-6
ML_HEADS, ML_DIM, ML_W, ML_CONV = 4, 128, 512, 4
GLA_HEADS, GLA_DK, GLA_DV, GLA_RANK, GLA_TAU = 4, 64, 128, 16, 16.0
E_COLS_PAD = 3840
RET_HEADS, RET_DIM, RET_W = 4, 128, 512
ROPE_BASE = 10000.0
RW_HEADS, RW_DIM, RW_W = 8, 64, 512
RW_COLS = 1792
RW_LN_EPS = 64e-5
D_FF = 3584
N_EXPERTS = 8
TOP_K = 2

VMEM_LIMIT = 48 * 1024 * 1024
NEG = -1e30


def _cp(*sem):
    return pltpu.CompilerParams(dimension_semantics=sem, vmem_limit_bytes=VMEM_LIMIT)


def _sigmoid(x):
    return 1.0 / (1.0 + jnp.exp(-x))


def _silu(x):
    return x * _sigmoid(x)


def _log_sigmoid(x):
    return jnp.minimum(x, 0.0) - jnp.log(1.0 + jnp.exp(-jnp.abs(x)))


def _dot(a, b):
    return jnp.dot(a.astype(BF16), b.astype(BF16), preferred_element_type=F32)


def _dot_nt(a, b):
    return lax.dot_general(a.astype(BF16), b.astype(BF16), (((1,), (1,)), ((), ())), preferred_element_type=F32)


def _dot_tn(a, b):
    return jnp.dot(a.T.astype(BF16), b.astype(BF16), preferred_element_type=F32)


def _rms_rows(x, g):
    ms = jnp.mean(x * x, axis=-1, keepdims=True)
    return x * lax.rsqrt(ms + EPS) * g


def _norm_matmul(x2d, g, w_bf16, tm=512, tn=768):
    n, d = x2d.shape
    c = w_bf16.shape[1]

    def body(x_ref, g_ref, w_ref, o_ref, xn_ref):
        @pl.when(pl.program_id(1) == 0)
        def _():
            xn_ref[...] = _rms_rows(x_ref[...], g_ref[...]).astype(BF16)

        o_ref[...] = jnp.dot(xn_ref[...], w_ref[...], preferred_element_type=F32)

    return pl.pallas_call(
        body,
        grid=(n // tm, c // tn),
        in_specs=[pl.BlockSpec((tm, d), lambda i, j: (i, 0)),
                  pl.BlockSpec((1, d), lambda i, j: (0, 0)),
                  pl.BlockSpec((d, tn), lambda i, j: (0, j))],
        out_specs=pl.BlockSpec((tm, tn), lambda i, j: (i, j)),
        out_shape=jax.ShapeDtypeStruct((n, c), F32),
        scratch_shapes=[pltpu.VMEM((tm, d), BF16)],
        compiler_params=_cp("parallel", "arbitrary"),
        name="norm_matmul",
    )(x2d, g.reshape(1, d), w_bf16)


def _mix_out(a, b, w_bf16, resid, tm=512):
    n, wa = a.shape
    wb = b.shape[1]
    d = w_bf16.shape[1]

    def body(a_ref, b_ref, w_ref, r_ref, o_ref):
        acc = jnp.dot(a_ref[...], w_ref[0:wa, :], preferred_element_type=F32)
        acc = acc + jnp.dot(b_ref[...], w_ref[wa:wa + wb, :], preferred_element_type=F32)
        o_ref[...] = r_ref[...] + acc

    return pl.pallas_call(
        body,
        grid=(n // tm,),
        in_specs=[pl.BlockSpec((tm, wa), lambda i: (i, 0)),
                  pl.BlockSpec((tm, wb), lambda i: (i, 0)),
                  pl.BlockSpec((wa + wb, d), lambda i: (0, 0)),
                  pl.BlockSpec((tm, d), lambda i: (i, 0))],
        out_specs=pl.BlockSpec((tm, d), lambda i: (i, 0)),
        out_shape=jax.ShapeDtypeStruct((n, d), F32),
        compiler_params=_cp("parallel"),
        name="mix_out",
    )(a, b, w_bf16, resid)


def _ffn(x2d, g, wg, wu, wd, tm=512, tf=896):
    n, d = x2d.shape
    f = wg.shape[1]
    nf = f // tf

    def body(x_ref, g_ref, wg_ref, wu_ref, wd_ref, o_ref, xn_ref, acc_ref):
        j = pl.program_id(1)

        @pl.when(j == 0)
        def _():
            xn_ref[...] = _rms_rows(x_ref[...], g_ref[...]).astype(BF16)
            acc_ref[...] = jnp.zeros_like(acc_ref)

        xn = xn_ref[...]
        gg = jnp.dot(xn, wg_ref[...], preferred_element_type=F32)
        uu = jnp.dot(xn, wu_ref[...], preferred_element_type=F32)
        act = (_silu(gg) * uu).astype(BF16)
        acc_ref[...] += jnp.dot(act, wd_ref[...], preferred_element_type=F32)

        @pl.when(j == nf - 1)
        def _():
            o_ref[...] = x_ref[...] + acc_ref[...]

    return pl.pallas_call(
        body,
        grid=(n // tm, nf),
        in_specs=[pl.BlockSpec((tm, d), lambda i, j: (i, 0)),
                  pl.BlockSpec((1, d), lambda i, j: (0, 0)),
                  pl.BlockSpec((d, tf), lambda i, j: (0, j)),
                  pl.BlockSpec((d, tf), lambda i, j: (0, j)),
                  pl.BlockSpec((tf, d), lambda i, j: (j, 0))],
        out_specs=pl.BlockSpec((tm, d), lambda i, j: (i, 0)),
        out_shape=jax.ShapeDtypeStruct((n, d), F32),
        scratch_shapes=[pltpu.VMEM((tm, d), BF16), pltpu.VMEM((tm, d), F32)],
        compiler_params=_cp("parallel", "arbitrary"),
        name="ffn",
    )(x2d, g.reshape(1, d), wg, wu, wd)


ML_L = 256


def _mlstm(p0, gates_col, gates_row, gate_b_row, gate_b_col, conv_w, conv_b, norm_g):
    bsz, t, _ = p0.shape
    L, D, H, W = ML_L, ML_DIM, ML_HEADS, ML_W

    def body(q_ref, k_ref, v_ref, og_ref, gc_ref, gr_ref, gbr_ref, gbc_ref, cwq_ref, cwk_ref, cbq_ref, cbk_ref,
             ng_ref, out_ref, qext, kext, c_ref, n_ref, m_ref):
        c = pl.program_id(1)

        @pl.when(c == 0)
        def _():
            qext[0:8, :] = jnp.zeros((8, W), F32)
            kext[0:8, :] = jnp.zeros((8, W), F32)
            c_ref[...] = jnp.zeros_like(c_ref)
            n_ref[...] = jnp.zeros_like(n_ref)
            m_ref[...] = jnp.zeros_like(m_ref)

        qext[8:, :] = q_ref[0]
        kext[8:, :] = k_ref[0]

        def conv(ext, cw_ref, cb_ref):
            acc = cb_ref[...] + cw_ref[0:1, :] * ext[pl.ds(8 - ML_CONV + 1, L), :]
            for kk in range(1, ML_CONV):
                acc = acc + cw_ref[kk:kk + 1, :] * ext[pl.ds(8 - ML_CONV + 1 + kk, L), :]
            return _silu(acc)

        q_all = conv(qext, cwq_ref, cbq_ref) * (D ** -0.5)
        k_all = conv(kext, cwk_ref, cbk_ref)
        qext[0:8, :] = qext[L:L + 8, :]
        kext[0:8, :] = kext[L:L + 8, :]

        gcol = gc_ref[0] + gbr_ref[...]
        grow = gr_ref[0] + gbc_ref[...]
        fcol = _log_sigmoid(gcol[:, H:2 * H])
        frow = _log_sigmoid(grow[H:2 * H, :])
        ri = lax.broadcasted_iota(jnp.int32, (L, L), 0)
        ci = lax.broadcasted_iota(jnp.int32, (L, L), 1)
        causal = ri >= ci
        heads = range(H)
        hs = lambda z, h: z[:, h * D:(h + 1) * D]
        w_intra, w_inter, w_state, carry, m_row, m_new = [], [], [], [], [], []
        for h in heads:
            f_row = frow[h:h + 1, :]
            i_row = grow[h:h + 1, :]
            f_col = fcol[:, h:h + 1]
            i_col = gcol[:, h:h + 1]
            m_st = m_ref[h]
            b_col = jnp.sum(jnp.where(causal, f_row, 0.0), axis=1, keepdims=True)
            b_row = jnp.sum(jnp.where(ri <= ci, f_col, 0.0), axis=0, keepdims=True)
            g_tot = jnp.sum(f_row, axis=1, keepdims=True)
            d_intra = jnp.where(causal, b_col - b_row + i_row, NEG)
            d_inter = b_col + m_st
            mr = jnp.maximum(d_inter, jnp.max(d_intra, axis=1, keepdims=True))
            w_intra.append(jnp.exp(d_intra - mr))
            w_inter.append(jnp.exp(d_inter - mr))
            m_row.append(mr)
            d_state = g_tot - b_col + i_col
            mn = jnp.maximum(g_tot + m_st, jnp.max(d_state, axis=0, keepdims=True))
            w_state.append(jnp.exp(d_state - mn))
            carry.append(jnp.exp(g_tot + m_st - mn))
            m_new.append(mn)
        qh = [hs(q_all, h) for h in heads]
        kh = [hs(k_all, h) for h in heads]
        vh = [v_ref[0, :, h * D:(h + 1) * D] for h in heads]
        c_st = [c_ref[h] for h in heads]
        n_st = [n_ref[h] for h in heads]
        s = [_dot_nt(qh[h], kh[h]) * w_intra[h] for h in heads]
        qc = [_dot(qh[h], c_st[h]) for h in heads]
        kw = [kh[h] * w_state[h] for h in heads]
        kv = [_dot_tn(kw[h], vh[h]) for h in heads]
        sv = [_dot(s[h], vh[h]) for h in heads]
        for h in heads:
            num = sv[h] + w_inter[h] * qc[h]
            den = jnp.sum(s[h], axis=1, keepdims=True) + w_inter[h] * jnp.sum(qh[h] * n_st[h], axis=1, keepdims=True)
            hval = num / jnp.maximum(jnp.abs(den), jnp.exp(-m_row[h]))
            c_ref[h] = carry[h] * c_st[h] + kv[h]
            n_ref[h] = carry[h] * n_st[h] + jnp.sum(kw[h], axis=0, keepdims=True)
            m_ref[h] = m_new[h]
            hg = _sigmoid(og_ref[0, :, h * D:(h + 1) * D]) * hval
            hc = hg - jnp.mean(hg, axis=1, keepdims=True)
            hn = hc * lax.rsqrt(jnp.mean(hc * hc, axis=1, keepdims=True) + EPS)
            out_ref[0, :, h * D:(h + 1) * D] = (hn * ng_ref[:, h * D:(h + 1) * D]).astype(BF16)

    col = lambda j: (lambda b, c: (b, c, j))
    fix = lambda j: (lambda b, c: (0, j))
    return pl.pallas_call(
        body,
        grid=(bsz, t // L),
        in_specs=[pl.BlockSpec((1, L, W), col(0)),
                  pl.BlockSpec((1, L, W), col(1)),
                  pl.BlockSpec((1, L, W), col(2)),
                  pl.BlockSpec((1, L, W), col(3)),
                  pl.BlockSpec((1, L, 2 * H), lambda b, c: (b, c, 0)),
                  pl.BlockSpec((1, 2 * H, L), lambda b, c: (b, 0, c)),
                  pl.BlockSpec((1, 2 * H), fix(0)),
                  pl.BlockSpec((2 * H, 1), fix(0)),
                  pl.BlockSpec((ML_CONV, W), fix(0)),
                  pl.BlockSpec((ML_CONV, W), fix(1)),
                  pl.BlockSpec((1, W), fix(0)),
                  pl.BlockSpec((1, W), fix(1)),
                  pl.BlockSpec((1, W), fix(0))],
        out_specs=pl.BlockSpec((1, L, W), lambda b, c: (b, c, 0)),
        out_shape=jax.ShapeDtypeStruct((bsz, t, W), BF16),
        scratch_shapes=[pltpu.VMEM((L + 8, W), F32), pltpu.VMEM((L + 8, W), F32),
                        pltpu.VMEM((H, D, D), F32), pltpu.VMEM((H, 1, D), F32), pltpu.VMEM((H, 1, 1), F32)],
        compiler_params=_cp("parallel", "arbitrary"),
        name="mlstm",
    )(p0, p0, p0, p0, gates_col, gates_row, gate_b_row, gate_b_col, conv_w, conv_w, conv_b, conv_b, norm_g)


GLA_TC = 256
GLA_SUB = 16


def _gla(p0, g_low, gate_up, gate_b, norm_g):
    bsz, t, _ = p0.shape
    tc, S = GLA_TC, GLA_SUB
    nsub = tc // S
    dk, dv = GLA_DK, GLA_DV

    def body(q_ref, k_ref, v_ref, gr_ref, gl_ref, gu_ref, gbias_ref, ng_ref, out_ref,
             st_ref, qt_scr, kt_scr, eg_scr, o_scr):
        c = pl.program_id(2)

        @pl.when(c == 0)
        def _():
            st_ref[...] = jnp.zeros_like(st_ref)

        z = _dot(gl_ref[0], gu_ref[...]) + gbias_ref[...]
        la = _log_sigmoid(z) / GLA_TAU
        rmod = lax.broadcasted_iota(jnp.int32, (tc, 1), 0) & (S - 1)
        bcum = la
        rsum = la
        for s in (1, 2, 4, 8):
            bcum = bcum + jnp.where(rmod >= s, pltpu.roll(bcum, s, 0), 0.0)
            rsum = rsum + jnp.where(rmod < S - s, pltpu.roll(rsum, tc - s, 0), 0.0)
        q = q_ref[0] * (dk ** -0.5)
        k = k_ref[0]
        v = v_ref[0]
        qt_scr[...] = q * jnp.exp(bcum)
        kt_scr[...] = k * jnp.exp(rsum - la)
        eg_scr[...] = jnp.exp(bcum + rsum - la)

        o0 = jnp.zeros((tc, dv), F32)
        o1 = jnp.zeros((tc, dv), F32)
        for d in range(S):
            if d == 0:
                kd, bd, vd = k, bcum, v
            else:
                kd = pltpu.roll(k, d, 0)
                bd = pltpu.roll(bcum, d, 0)
                vd = pltpu.roll(v, d, 0)
            valid = rmod >= d
            e = jnp.exp(jnp.where(valid, bcum - bd, 0.0))
            prod = q * kd * e
            w0 = jnp.where(valid, jnp.sum(prod[:, 0:dk], axis=1, keepdims=True), 0.0)
            w1 = jnp.where(valid, jnp.sum(prod[:, dk:2 * dk], axis=1, keepdims=True), 0.0)
            o0 = o0 + w0 * vd[:, 0:dv]
            o1 = o1 + w1 * vd[:, dv:2 * dv]
        o_scr[:, 0:dv] = o0
        o_scr[:, dv:2 * dv] = o1

        def sub(si, carry):
            st0, st1 = carry
            r0 = pl.multiple_of(si * S, S)
            qs = qt_scr[pl.ds(r0, S), :]
            ks = kt_scr[pl.ds(r0, S), :]
            vs = v_ref[0, pl.ds(r0, S), :]
            eg = eg_scr[pl.ds(r0, 1), :]
            new = []
            for hh, st in ((0, st0), (1, st1)):
                lk = slice(hh * dk, (hh + 1) * dk)
                lv = slice(hh * dv, (hh + 1) * dv)
                o_scr[pl.ds(r0, S), lv] += _dot_nt(qs[:, lk], st)
                new.append(st * eg[:, lk] + _dot_tn(vs[:, lv], ks[:, lk]))
            return tuple(new)

        st0, st1 = lax.fori_loop(0, nsub, sub, (st_ref[0], st_ref[1]))
        st_ref[0] = st0
        st_ref[1] = st1

        for hh in range(2):
            lv = slice(hh * dv, (hh + 1) * dv)
            o = o_scr[:, lv]
            on = o * lax.rsqrt(jnp.mean(o * o, axis=1, keepdims=True) + EPS)
            out_ref[0, :, lv] = (on * ng_ref[:, lv] * _silu(gr_ref[0, :, lv])).astype(BF16)

    return pl.pallas_call(
        body,
        grid=(bsz, GLA_HEADS // 2, t // tc),
        in_specs=[pl.BlockSpec((1, tc, 2 * dk), lambda b, h, c: (b, c, 16 + h)),
                  pl.BlockSpec((1, tc, 2 * dk), lambda b, h, c: (b, c, 18 + h)),
                  pl.BlockSpec((1, tc, 2 * dv), lambda b, h, c: (b, c, 10 + h)),
                  pl.BlockSpec((1, tc, 2 * dv), lambda b, h, c: (b, c, 12 + h)),
                  pl.BlockSpec((1, tc, GLA_RANK), lambda b, h, c: (b, c, 0)),
                  pl.BlockSpec((GLA_RANK, 2 * dk), lambda b, h, c: (0, h)),
                  pl.BlockSpec((1, 2 * dk), lambda b, h, c: (0, h)),
                  pl.BlockSpec((1, 2 * dv), lambda b, h, c: (0, h))],
        out_specs=pl.BlockSpec((1, tc, 2 * dv), lambda b, h, c: (b, c, h)),
        out_shape=jax.ShapeDtypeStruct((bsz, t, GLA_HEADS * dv), BF16),
        scratch_shapes=[pltpu.VMEM((2, dv, dk), F32), pltpu.VMEM((tc, 2 * dk), F32), pltpu.VMEM((tc, 2 * dk), F32),
                        pltpu.VMEM((tc, 2 * dk), F32), pltpu.VMEM((tc, 2 * dv), F32)],
        compiler_params=_cp("parallel", "parallel", "arbitrary"),
        name="gla",
    )(p0, p0, p0, p0, g_low, gate_up, gate_b, norm_g)


RET_L = 256


def _retention(p_ret, cos_t, sin_t, intra, inter, sdec, cdec, norm_g):
    bsz, t, _ = p_ret.shape
    L, D, H, W = RET_L, RET_DIM, RET_HEADS, RET_W

    def body(q_ref, k_ref, v_ref, g_ref, cos_ref, sin_ref, intra_ref, inter_ref, sdec_ref, cdec_ref, ng_ref,
             out_ref, s_ref):
        c = pl.program_id(1)

        @pl.when(c == 0)
        def _():
            s_ref[...] = jnp.zeros_like(s_ref)

        cs = cos_ref[...]
        sn = sin_ref[...]

        def rot(x):
            return x * cs + pltpu.roll(x, D // 2, 1) * sn

        heads = range(H)
        q = [rot(q_ref[0, :, h * D:(h + 1) * D]) * (D ** -0.5) for h in heads]
        k = [rot(k_ref[0, :, h * D:(h + 1) * D]) for h in heads]
        v = [v_ref[0, :, h * D:(h + 1) * D] for h in heads]
        s_st = [s_ref[h] for h in heads]
        s = [_dot_nt(q[h], k[h]) * intra_ref[h] for h in heads]
        qs = [_dot(q[h], s_st[h]) for h in heads]
        kv = [_dot_tn(k[h] * sdec_ref[h], v[h]) for h in heads]
        sv = [_dot(s[h], v[h]) for h in heads]
        for h in heads:
            o = sv[h] + inter_ref[h] * qs[h]
            s_ref[h] = cdec_ref[h] * s_st[h] + kv[h]
            oc = o - jnp.mean(o, axis=1, keepdims=True)
            on = oc * lax.rsqrt(jnp.mean(oc * oc, axis=1, keepdims=True) + EPS)
            hl = slice(h * D, (h + 1) * D)
            out_ref[0, :, hl] = (on * ng_ref[:, hl] * _silu(g_ref[0, :, hl])).astype(BF16)

    col = lambda j: (lambda b, c: (b, c, j))
    fix3 = lambda b, c: (0, 0, 0)
    return pl.pallas_call(
        body,
        grid=(bsz, t // L),
        in_specs=[pl.BlockSpec((1, L, W), col(0)),
                  pl.BlockSpec((1, L, W), col(1)),
                  pl.BlockSpec((1, L, W), col(2)),
                  pl.BlockSpec((1, L, W), col(3)),
                  pl.BlockSpec((L, D), lambda b, c: (c, 0)),
                  pl.BlockSpec((L, D), lambda b, c: (c, 0)),
                  pl.BlockSpec((H, L, L), fix3),
                  pl.BlockSpec((H, L, 1), fix3),
                  pl.BlockSpec((H, L, 1), fix3),
                  pl.BlockSpec((H, 1, 1), fix3),
                  pl.BlockSpec((1, W), lambda b, c: (0, 0))],
        out_specs=pl.BlockSpec((1, L, W), lambda b, c: (b, c, 0)),
        out_shape=jax.ShapeDtypeStruct((bsz, t, W), BF16),
        scratch_shapes=[pltpu.VMEM((H, D, D), F32)],
        compiler_params=_cp("parallel", "arbitrary"),
        name="retention",
    )(p_ret, p_ret, p_ret, p_ret, cos_t, sin_t, intra, inter, sdec, cdec, norm_g)


def _retention_tables(t):
    L, D = RET_L, RET_DIM
    inv = ROPE_BASE ** (-jnp.arange(0, D, 2, dtype=F32) / D)
    ang = jnp.arange(t).astype(F32)[:, None] * inv[None, :]
    cos = jnp.cos(ang)
    sin = jnp.sin(ang)
    cos_t = jnp.concatenate([cos, cos], axis=-1)
    sin_t = jnp.concatenate([-sin, sin], axis=-1)
    log_gamma = jnp.log1p(-jnp.exp2(-5.0 - jnp.arange(RET_HEADS, dtype=F32)))
    idx = jnp.arange(L, dtype=F32)
    causal = idx[:, None] >= idx[None, :]
    rel = jnp.where(causal, idx[:, None] - idx[None, :], 0.0)
    intra = jnp.where(causal, jnp.exp(log_gamma[:, None, None] * rel), 0.0)
    inter = jnp.exp(log_gamma[:, None] * (idx + 1.0))[:, :, None]
    sdec = jnp.exp(log_gamma[:, None] * (L - 1.0 - idx))[:, :, None]
    cdec = jnp.exp(log_gamma * L)[:, None, None]
    return cos_t, sin_t, intra, inter, sdec, cdec


RWP_TM = 256


def _seg_sum(x, bd):
    hi = x.astype(BF16)
    lo = (x - hi.astype(F32)).astype(BF16)
    return jnp.dot(hi, bd, preferred_element_type=F32) + jnp.dot(lo, bd, preferred_element_type=F32)


def _rwkv_prep(p1, mu, w_up, w0, a_up, a0, g_up, k_k, k_a, r_k, bd):
    bsz, t, _ = p1.shape
    tm, W = RWP_TM, RW_W

    def body(x_ref, xp_ref, mu_ref, wup_ref, w0_ref, aup_ref, a0_ref, gup_ref, kk_ref, ka_ref, rk_ref, bd_ref,
             r_out, lw_out, k_out, v_out, a_out, b_out, g_out, bonus_out):
        c = pl.program_id(1)
        cur = x_ref[0]
        prev = jnp.where(c == 0, 0.0, xp_ref[0, 7:8, :])
        rowi = lax.broadcasted_iota(jnp.int32, (tm, 1), 0)
        sh = jnp.where(rowi == 0, prev, pltpu.roll(cur, 1, 0))
        xm = cur + (sh - cur) * mu_ref[...]
        x_r = xm[:, 0:W]
        x_k = xm[:, W:2 * W]
        x_v = xm[:, 2 * W:3 * W]
        x_dl = xm[:, 3 * W:3 * W + 64]
        x_al = xm[:, 3 * W + 64:3 * W + 128]
        x_gl = xm[:, 3 * W + 128:3 * W + 256]
        wl = w0_ref[...] + _dot(jnp.tanh(x_dl), wup_ref[...])
        sp = jnp.maximum(-wl, 0.0) + jnp.log(1.0 + jnp.exp(-jnp.abs(wl)))
        lw_out[0] = -jnp.exp(-sp - 0.5)
        a = _sigmoid(a0_ref[...] + _dot(x_al, aup_ref[...]))
        g_out[0] = _dot(_sigmoid(x_gl), gup_ref[...])
        kk0 = x_k * kk_ref[...]
        nrm = jnp.sqrt(_seg_sum(kk0 * kk0, bd_ref[...]))
        kk = kk0 / jnp.maximum(nrm, 1e-12)
        k_h = x_k * (1.0 + (a - 1.0) * ka_ref[...])
        r_out[0] = x_r
        k_out[0] = k_h
        v_out[0] = x_v
        a_out[0] = -kk
        b_out[0] = kk * a
        bonus_out[0] = _seg_sum(x_r * k_h * rk_ref[...], bd_ref[...]) * x_v

    row = lambda b, c: (0, 0)
    blk = pl.BlockSpec((1, tm, W), lambda b, c: (b, c, 0))
    shp = jax.ShapeDtypeStruct((bsz, t, W), F32)
    return pl.pallas_call(
        body,
        grid=(bsz, t // tm),
        in_specs=[pl.BlockSpec((1, tm, RW_COLS), lambda b, c: (b, c, 0)),
                  pl.BlockSpec((1, 8, RW_COLS), lambda b, c: (b, jnp.maximum(c * (tm // 8) - 1, 0), 0)),
                  pl.BlockSpec((1, RW_COLS), row),
                  pl.BlockSpec((64, W), row), pl.BlockSpec((1, W), row),
                  pl.BlockSpec((64, W), row), pl.BlockSpec((1, W), row),
                  pl.BlockSpec((128, W), row),
                  pl.BlockSpec((1, W), row), pl.BlockSpec((1, W), row), pl.BlockSpec((1, W), row),
                  pl.BlockSpec((W, W), row)],
        out_specs=[blk] * 8,
        out_shape=[shp] * 8,
        compiler_params=_cp("parallel", "arbitrary"),
        name="rwkv_prep",
    )(p1, p1, mu, w_up, w0, a_up, a0, g_up, k_k, k_a, r_k, bd)


RW_L = 64
RW_TB = 128


def _rwkv_scan(r, lw, k, v, aa, bb, g_out, bonus, ln_g, ln_b):
    bsz, t, W = r.shape
    L, N, tb = RW_L, RW_DIM, RW_TB
    nck = tb // L

    def body(r_ref, lw_ref, k_ref, v_ref, a_ref, b_ref, g_ref, bonus_ref, lng_ref, lnb_ref, out_ref, h_ref):
        c = pl.program_id(1)

        @pl.when(c == 0)
        def _():
            h_ref[...] = jnp.zeros_like(h_ref)

        lw_all = lw_ref[0]
        rowi = lax.broadcasted_iota(jnp.int32, (tb, 1), 0) & (L - 1)
        cl = lw_all
        for s in (1, 2, 4, 8, 16, 32):
            cl = cl + jnp.where(rowi >= s, pltpu.roll(cl, s, 0), 0.0)
        cl_last = jnp.concatenate([jnp.broadcast_to(cl[(cc + 1) * L - 1:(cc + 1) * L, :], (L, W))
                                   for cc in range(nck)], axis=0)
        e_inv = jnp.exp(-cl)
        e_end = jnp.exp(cl_last - cl)
        p_end = jnp.exp(cl_last)
        at = a_ref[0] * jnp.exp(cl - lw_all)
        rt = r_ref[0] * jnp.exp(cl)
        bt = b_ref[0] * e_inv
        kt = k_ref[0] * e_inv
        b_end = b_ref[0] * e_end
        k_end = k_ref[0] * e_end
        v_all = v_ref[0]
        ri = lax.broadcasted_iota(jnp.int32, (L, L), 0)
        ci = lax.broadcasted_iota(jnp.int32, (L, L), 1)
        strict = ri > ci
        lower = ri >= ci
        eye = ri == ci
        heads = range(nck * RW_HEADS)

        def hs(z, u):
            cc, hh = divmod(u, RW_HEADS)
            return z[cc * L:(cc + 1) * L, hh * N:(hh + 1) * N]

        m = [_dot_nt(jnp.concatenate([hs(at, h), hs(rt, h)], axis=0),
                     jnp.concatenate([hs(bt, h), hs(kt, h)], axis=0)) for h in heads]
        a_ab = [jnp.where(strict, m[h][0:L, 0:L], 0.0) for h in heads]
        a_ak = [jnp.where(strict, m[h][0:L, L:2 * L], 0.0) for h in heads]
        a_rb = [jnp.where(lower, m[h][L:2 * L, 0:L], 0.0) for h in heads]
        a_rk = [jnp.where(lower, m[h][L:2 * L, L:2 * L], 0.0) for h in heads]
        end_t = [jnp.concatenate([hs(b_end, h), hs(k_end, h)], axis=1).T for h in heads]
        vp = [_dot(jnp.concatenate([a_ak[h], a_rk[h], end_t[h][N:2 * N]], axis=0), hs(v_all, h)) for h in heads]
        x = [jnp.concatenate([hs(at, h), vp[h][0:L]], axis=1) for h in heads]
        ap = a_ab
        for it in range(6):
            x = [x[h] + _dot(ap[h], x[h]) for h in heads]
            if it < 5:
                ap = [_dot(ap[h], ap[h]) for h in heads]
        post = [_dot(jnp.concatenate([a_rb[h], end_t[h][0:N]], axis=0), x[h]) for h in heads]
        lhs = []
        y0s = []
        h_adds = []
        for u in heads:
            q_hat = hs(rt, u) + post[u][0:L, 0:N]
            gmat = jnp.where(eye, hs(p_end, u), 0.0) + post[u][L:L + N, 0:N]
            lhs.append(jnp.concatenate([q_hat, gmat], axis=0))
            y0s.append(post[u][0:L, N:2 * N] + vp[u][L:2 * L])
            h_adds.append(post[u][L:L + N, N:2 * N] + vp[u][2 * L:2 * L + N])
        h_st = [h_ref[hh] for hh in range(RW_HEADS)]
        for cc in range(nck):
            outs = []
            for hh in range(RW_HEADS):
                u = cc * RW_HEADS + hh
                res = _dot(lhs[u], h_st[hh])
                y = res[0:L] + y0s[u]
                h_st[hh] = res[L:L + N] + h_adds[u]
                yc = y - jnp.mean(y, axis=1, keepdims=True)
                outs.append(yc * lax.rsqrt(jnp.mean(yc * yc, axis=1, keepdims=True) + RW_LN_EPS))
            yn = jnp.concatenate(outs, axis=1)
            rows = slice(cc * L, (cc + 1) * L)
            out_ref[0, rows, :] = ((yn * lng_ref[...] + lnb_ref[...] + bonus_ref[0, rows, :])
                                   * g_ref[0, rows, :]).astype(BF16)
        for hh in range(RW_HEADS):
            h_ref[hh] = h_st[hh]

    blk = pl.BlockSpec((1, tb, W), lambda b, c: (b, c, 0))
    vec = pl.BlockSpec((1, W), lambda b, c: (0, 0))
    return pl.pallas_call(
        body,
        grid=(bsz, t // tb),
        in_specs=[blk] * 8 + [vec, vec],
        out_specs=blk,
        out_shape=jax.ShapeDtypeStruct((bsz, t, W), BF16),
        scratch_shapes=[pltpu.VMEM((RW_HEADS, N, N), F32)],
        compiler_params=_cp("parallel", "arbitrary"),
        name="rwkv_scan",
    )(r, lw, k, v, aa, bb, g_out, bonus, ln_g, ln_b)


def _router(x2d, g, router_pad, tm=512):
    n, d = x2d.shape
    e = router_pad.shape[1]

    def body(x_ref, g_ref, r_ref, xn_ref, lg_ref):
        xn = _rms_rows(x_ref[...], g_ref[...])
        xn_ref[...] = xn.astype(BF16)
        lg_ref[...] = jnp.dot(xn, r_ref[...], preferred_element_type=F32, precision=lax.Precision.HIGHEST)

    return pl.pallas_call(
        body,
        grid=(n // tm,),
        in_specs=[pl.BlockSpec((tm, d), lambda i: (i, 0)),
                  pl.BlockSpec((1, d), lambda i: (0, 0)),
                  pl.BlockSpec((d, e), lambda i: (0, 0))],
        out_specs=[pl.BlockSpec((tm, d), lambda i: (i, 0)), pl.BlockSpec((tm, e), lambda i: (i, 0))],
        out_shape=[jax.ShapeDtypeStruct((n, d), BF16), jax.ShapeDtypeStruct((n, e), F32)],
        compiler_params=_cp("parallel"),
        name="moe_router",
    )(x2d, g.reshape(1, d), router_pad)


MOE_TM = 512
MOE_TF = 896


def _experts(xs, row_w, tile_expert, n_used, wg, wu, wd):
    npad, d = xs.shape
    tm, tf = MOE_TM, MOE_TF
    nf = D_FF // tf
    n_tiles = npad // tm

    def body(te_ref, nu_ref, x_ref, w_ref, wg_ref, wu_ref, wd_ref, o_ref):
        i = pl.program_id(0)
        j = pl.program_id(1)

        @pl.when(j == 0)
        def _():
            o_ref[...] = jnp.zeros_like(o_ref)

        @pl.when(i < nu_ref[0])
        def _():
            x = x_ref[...]
            gg = jnp.dot(x, wg_ref[0], preferred_element_type=F32)
            uu = jnp.dot(x, wu_ref[0], preferred_element_type=F32)
            act = (_silu(gg) * uu).astype(BF16)
            o_ref[...] += jnp.dot(act, wd_ref[0], preferred_element_type=F32)

        @pl.when(j == nf - 1)
        def _():
            o_ref[...] = o_ref[...] * w_ref[...]

    grid_spec = pltpu.PrefetchScalarGridSpec(
        num_scalar_prefetch=2,
        grid=(n_tiles, nf),
        in_specs=[pl.BlockSpec((tm, d), lambda i, j, te, nu: (i, 0)),
                  pl.BlockSpec((tm, 1), lambda i, j, te, nu: (i, 0)),
                  pl.BlockSpec((1, d, tf), lambda i, j, te, nu: (te[i], 0, j)),
                  pl.BlockSpec((1, d, tf), lambda i, j, te, nu: (te[i], 0, j)),
                  pl.BlockSpec((1, tf, d), lambda i, j, te, nu: (te[i], j, 0))],
        out_specs=pl.BlockSpec((tm, d), lambda i, j, te, nu: (i, 0)),
    )
    return pl.pallas_call(
        body,
        grid_spec=grid_spec,
        out_shape=jax.ShapeDtypeStruct((npad, d), F32),
        compiler_params=_cp("parallel", "arbitrary"),
        name="moe_experts",
    )(tile_expert, n_used, xs, row_w, wg, wu, wd)


def _combine_norm(h, y0, y1, g, tm=512):
    n, d = h.shape

    def body(h_ref, a_ref, b_ref, g_ref, o_ref):
        o_ref[...] = _rms_rows(h_ref[...] + (a_ref[...] + b_ref[...]), g_ref[...])

    blk = pl.BlockSpec((tm, d), lambda i: (i, 0))
    return pl.pallas_call(
        body,
        grid=(n // tm,),
        in_specs=[blk, blk, blk, pl.BlockSpec((1, d), lambda i: (0, 0))],
        out_specs=blk,
        out_shape=jax.ShapeDtypeStruct((n, d), F32),
        compiler_params=_cp("parallel"),
        name="combine_norm",
    )(h, y0, y1, g.reshape(1, d))


def _route(logits, n):
    tm = MOE_TM
    top_val, top_idx = lax.top_k(logits, TOP_K)
    top_w = jax.nn.softmax(top_val, axis=-1)
    e_flat = top_idx.reshape(-1).astype(jnp.int32)
    w_flat = top_w.reshape(-1)
    onehot = (e_flat[:, None] == jnp.arange(N_EXPERTS, dtype=jnp.int32)[None, :]).astype(jnp.int32)
    rank = jnp.take_along_axis(jnp.cumsum(onehot, axis=0), e_flat[:, None], axis=1)[:, 0] - 1
    counts = jnp.sum(onehot, axis=0)
    padded = ((counts + tm - 1) // tm) * tm
    ends = jnp.cumsum(padded)
    starts = ends - padded
    pos = starts[e_flat] + rank
    npad = n * TOP_K + N_EXPERTS * tm
    tok = jnp.arange(n * TOP_K, dtype=jnp.int32) // TOP_K
    sorted_tok = jnp.zeros((npad,), jnp.int32).at[pos].set(tok)
    sorted_w = jnp.zeros((npad,), F32).at[pos].set(w_flat)
    tile_start = jnp.arange(npad // tm, dtype=jnp.int32) * tm
    tile_expert = jnp.minimum(jnp.searchsorted(ends, tile_start, side="right"), N_EXPERTS - 1).astype(jnp.int32)
    n_used = (ends[-1] // tm).astype(jnp.int32).reshape(1)
    return sorted_tok, sorted_w, tile_expert, n_used, pos.reshape(n, TOP_K)


def kernel(x, e_norm1_g, e_w_in, e_ml_conv_w, e_ml_conv_b, e_ml_gate_b, e_ml_norm_g, e_gla_gate_up, e_gla_gate_b,
           e_gla_norm_g, e_w_out, e_norm2_g, e_ffn_w_gate, e_ffn_w_up, e_ffn_w_down, o_norm1_g, o_w_in,
           o_ret_norm_g, o_rw_mu, o_rw_w_up, o_rw_w0, o_rw_a_up, o_rw_a0, o_rw_g_up, o_rw_k_k, o_rw_k_a, o_rw_r_k,
           o_rw_ln_g, o_rw_ln_b, o_w_out, o_norm2_g, o_moe_router, o_moe_w_gate, o_moe_w_up, o_moe_w_down,
           final_norm_g):
    bsz, t, d = x.shape
    n = bsz * t
    h0 = x.reshape(n, d)

    w = e_w_in[0]
    w0 = jnp.concatenate([w[:, :2048], w[:, 2056:3592], w[:, 2048:2056], w[:, 3592:3608],
                          jnp.zeros((d, E_COLS_PAD - 3608), F32)], axis=1).astype(BF16)
    p0 = _norm_matmul(h0, e_norm1_g[0], w0).reshape(bsz, t, E_COLS_PAD)
    gates_col = p0[:, :, 3584:3592]
    gates_row = jnp.transpose(gates_col, (0, 2, 1))
    h_ml = _mlstm(p0, gates_col, gates_row, e_ml_gate_b[0].reshape(1, -1), e_ml_gate_b[0].reshape(-1, 1),
                  e_ml_conv_w[0], e_ml_conv_b[0].reshape(1, -1), e_ml_norm_g[0].reshape(1, -1))
    g_low = p0[:, :, 3592:3608]
    o_gla = _gla(p0, g_low, e_gla_gate_up[0].astype(BF16), e_gla_gate_b[0].reshape(1, -1),
                 e_gla_norm_g[0].reshape(1, -1))
    h1 = _mix_out(h_ml.reshape(n, -1), o_gla.reshape(n, -1), e_w_out[0].astype(BF16), h0)
    h2 = _ffn(h1, e_norm2_g[0], e_ffn_w_gate[0].astype(BF16), e_ffn_w_up[0].astype(BF16),
              e_ffn_w_down[0].astype(BF16))

    w = o_w_in[0]
    p_ret = _norm_matmul(h2, o_norm1_g[0], w[:, :4 * RET_W].astype(BF16), tn=1024).reshape(bsz, t, -1)
    p1 = _norm_matmul(h2, o_norm1_g[0], w[:, 4 * RET_W:].astype(BF16), tn=896).reshape(bsz, t, -1)
    y_ret = _retention(p_ret, *_retention_tables(t), o_ret_norm_g[0].reshape(1, -1))
    head_of = jnp.arange(RW_W) // RW_DIM
    bd = (head_of[:, None] == head_of[None, :]).astype(BF16)
    row = lambda a: a.reshape(1, -1)
    r, lw, k, v, aa, bb, g_out, bonus = _rwkv_prep(
        p1, row(o_rw_mu[0]), o_rw_w_up[0].astype(BF16), row(o_rw_w0[0]), o_rw_a_up[0].astype(BF16),
        row(o_rw_a0[0]), o_rw_g_up[0].astype(BF16), row(o_rw_k_k[0]), row(o_rw_k_a[0]), row(o_rw_r_k[0]), bd)
    y_rw = _rwkv_scan(r, lw, k, v, aa, bb, g_out, bonus, row(o_rw_ln_g[0]), row(o_rw_ln_b[0]))
    h3 = _mix_out(y_ret.reshape(n, -1), y_rw.reshape(n, -1), o_w_out[0].astype(BF16), h2)

    router_pad = jnp.zeros((d, 128), F32).at[:, :N_EXPERTS].set(o_moe_router[0])
    xn, logits = _router(h3, o_norm2_g[0], router_pad)
    sorted_tok, sorted_w, tile_expert, n_used, pos = _route(logits[:, :N_EXPERTS], n)
    xs = jnp.take(xn, sorted_tok, axis=0)
    ys = _experts(xs, sorted_w.reshape(-1, 1), tile_expert, n_used, o_moe_w_gate[0].astype(BF16),
                  o_moe_w_up[0].astype(BF16), o_moe_w_down[0].astype(BF16))
    y0 = jnp.take(ys, pos[:, 0], axis=0)
    y1 = jnp.take(ys, pos[:, 1], axis=0)
    out = _combine_norm(h3, y0, y1, final_norm_g)
    return out.reshape(bsz, t, d)
```

```python
import functools

import numpy as np
import jax
import jax.numpy as jnp
from jax import lax
from jax.experimental import pallas as pl
from jax.experimental.pallas import tpu as pltpu

F32 = jnp.float32
BF16 = jnp.bfloat16

D_MODEL = 1024
EPS = 1e-6
ML_HEADS, ML_DIM, ML_W, ML_CONV = 4, 128, 512, 4
GLA_HEADS, GLA_DK, GLA_DV, GLA_RANK, GLA_TAU = 4, 64, 128, 16, 16.0
E_COLS_PAD = 3840
RET_HEADS, RET_DIM, RET_W = 4, 128, 512
ROPE_BASE = 10000.0
RW_HEADS, RW_DIM, RW_W = 8, 64, 512
RW_COLS = 1792
RW_LN_EPS = 64e-5
D_FF = 3584
N_EXPERTS = 8
TOP_K = 2

VMEM_LIMIT = 48 * 1024 * 1024
NEG = -1e30


def _cp(*sem):
    return pltpu.CompilerParams(dimension_semantics=sem, vmem_limit_bytes=VMEM_LIMIT)


def _sigmoid(x):
    return 1.0 / (1.0 + jnp.exp(-x))


def _silu(x):
    return x * _sigmoid(x)


def _log_sigmoid(x):
    return jnp.minimum(x, 0.0) - jnp.log(1.0 + jnp.exp(-jnp.abs(x)))


def _dot(a, b):
    return jnp.dot(a.astype(BF16), b.astype(BF16), preferred_element_type=F32)


def _dot_nt(a, b):
    return lax.dot_general(a.astype(BF16), b.astype(BF16), (((1,), (1,)), ((), ())), preferred_element_type=F32)


def _dot_tn(a, b):
    return jnp.dot(a.T.astype(BF16), b.astype(BF16), preferred_element_type=F32)


def _rms_rows(x, g):
    ms = jnp.mean(x * x, axis=-1, keepdims=True)
    return x * lax.rsqrt(ms + EPS) * g


def _norm_matmul(x2d, g, w_bf16, tm=512, tn=768):
    n, d = x2d.shape
    c = w_bf16.shape[1]

    def body(x_ref, g_ref, w_ref, o_ref, xn_ref):
        @pl.when(pl.program_id(1) == 0)
        def _():
            xn_ref[...] = _rms_rows(x_ref[...], g_ref[...]).astype(BF16)

        o_ref[...] = jnp.dot(xn_ref[...], w_ref[...], preferred_element_type=F32)

    return pl.pallas_call(
        body,
        grid=(n // tm, c // tn),
        in_specs=[pl.BlockSpec((tm, d), lambda i, j: (i, 0)),
                  pl.BlockSpec((1, d), lambda i, j: (0, 0)),
                  pl.BlockSpec((d, tn), lambda i, j: (0, j))],
        out_specs=pl.BlockSpec((tm, tn), lambda i, j: (i, j)),
        out_shape=jax.ShapeDtypeStruct((n, c), F32),
        scratch_shapes=[pltpu.VMEM((tm, d), BF16)],
        compiler_params=_cp("parallel", "arbitrary"),
        name="norm_matmul",
    )(x2d, g.reshape(1, d), w_bf16)


def _mix_out(a, b, w_bf16, resid, tm=512):
    n, wa = a.shape
    wb = b.shape[1]
    d = w_bf16.shape[1]

    def body(a_ref, b_ref, w_ref, r_ref, o_ref):
        acc = jnp.dot(a_ref[...], w_ref[0:wa, :], preferred_element_type=F32)
        acc = acc + jnp.dot(b_ref[...], w_ref[wa:wa + wb, :], preferred_element_type=F32)
        o_ref[...] = r_ref[...] + acc

    return pl.pallas_call(
        body,
        grid=(n // tm,),
        in_specs=[pl.BlockSpec((tm, wa), lambda i: (i, 0)),
                  pl.BlockSpec((tm, wb), lambda i: (i, 0)),
                  pl.BlockSpec((wa + wb, d), lambda i: (0, 0)),
                  pl.BlockSpec((tm, d), lambda i: (i, 0))],
        out_specs=pl.BlockSpec((tm, d), lambda i: (i, 0)),
        out_shape=jax.ShapeDtypeStruct((n, d), F32),
        compiler_params=_cp("parallel"),
        name="mix_out",
    )(a, b, w_bf16, resid)


def _ffn(x2d, g, wg, wu, wd, tm=512, tf=896):
    n, d = x2d.shape
    f = wg.shape[1]
    nf = f // tf

    def body(x_ref, g_ref, wg_ref, wu_ref, wd_ref, o_ref, xn_ref, acc_ref):
        j = pl.program_id(1)

        @pl.when(j == 0)
        def _():
            xn_ref[...] = _rms_rows(x_ref[...], g_ref[...]).astype(BF16)
            acc_ref[...] = jnp.zeros_like(acc_ref)

        xn = xn_ref[...]
        gg = jnp.dot(xn, wg_ref[...], preferred_element_type=F32)
        uu = jnp.dot(xn, wu_ref[...], preferred_element_type=F32)
        act = (_silu(gg) * uu).astype(BF16)
        acc_ref[...] += jnp.dot(act, wd_ref[...], preferred_element_type=F32)

        @pl.when(j == nf - 1)
        def _():
            o_ref[...] = x_ref[...] + acc_ref[...]

    return pl.pallas_call(
        body,
        grid=(n // tm, nf),
        in_specs=[pl.BlockSpec((tm, d), lambda i, j: (i, 0)),
                  pl.BlockSpec((1, d), lambda i, j: (0, 0)),
                  pl.BlockSpec((d, tf), lambda i, j: (0, j)),
                  pl.BlockSpec((d, tf), lambda i, j: (0, j)),
                  pl.BlockSpec((tf, d), lambda i, j: (j, 0))],
        out_specs=pl.BlockSpec((tm, d), lambda i, j: (i, 0)),
        out_shape=jax.ShapeDtypeStruct((n, d), F32),
        scratch_shapes=[pltpu.VMEM((tm, d), BF16), pltpu.VMEM((tm, d), F32)],
        compiler_params=_cp("parallel", "arbitrary"),
        name="ffn",
    )(x2d, g.reshape(1, d), wg, wu, wd)


ML_L = 256


def _mlstm(p0, gates_col, gates_row, gate_b_row, gate_b_col, conv_w, conv_b, norm_g):
    bsz, t, _ = p0.shape
    L, D, H, W = ML_L, ML_DIM, ML_HEADS, ML_W

    def body(q_ref, k_ref, v_ref, og_ref, gc_ref, gr_ref, gbr_ref, gbc_ref, cwq_ref, cwk_ref, cbq_ref, cbk_ref,
             ng_ref, out_ref, qext, kext, c_ref, n_ref, m_ref):
        c = pl.program_id(1)

        @pl.when(c == 0)
        def _():
            qext[0:8, :] = jnp.zeros((8, W), F32)
            kext[0:8, :] = jnp.zeros((8, W), F32)
            c_ref[...] = jnp.zeros_like(c_ref)
            n_ref[...] = jnp.zeros_like(n_ref)
            m_ref[...] = jnp.zeros_like(m_ref)

        qext[8:, :] = q_ref[0]
        kext[8:, :] = k_ref[0]

        def conv(ext, cw_ref, cb_ref):
            acc = cb_ref[...] + cw_ref[0:1, :] * ext[pl.ds(8 - ML_CONV + 1, L), :]
            for kk in range(1, ML_CONV):
                acc = acc + cw_ref[kk:kk + 1, :] * ext[pl.ds(8 - ML_CONV + 1 + kk, L), :]
            return _silu(acc)

        q_all = conv(qext, cwq_ref, cbq_ref) * (D ** -0.5)
        k_all = conv(kext, cwk_ref, cbk_ref)
        qext[0:8, :] = qext[L:L + 8, :]
        kext[0:8, :] = kext[L:L + 8, :]

        gcol = gc_ref[0] + gbr_ref[...]
        grow = gr_ref[0] + gbc_ref[...]
        fcol = _log_sigmoid(gcol[:, H:2 * H])
        frow = _log_sigmoid(grow[H:2 * H, :])
        ri = lax.broadcasted_iota(jnp.int32, (L, L), 0)
        ci = lax.broadcasted_iota(jnp.int32, (L, L), 1)
        causal = ri >= ci
        heads = range(H)
        hs = lambda z, h: z[:, h * D:(h + 1) * D]
        w_intra, w_inter, w_state, carry, m_row, m_new = [], [], [], [], [], []
        for h in heads:
            f_row = frow[h:h + 1, :]
            i_row = grow[h:h + 1, :]
            f_col = fcol[:, h:h + 1]
            i_col = gcol[:, h:h + 1]
            m_st = m_ref[h]
            b_col = jnp.sum(jnp.where(causal, f_row, 0.0), axis=1, keepdims=True)
            b_row = jnp.sum(jnp.where(ri <= ci, f_col, 0.0), axis=0, keepdims=True)
            g_tot = jnp.sum(f_row, axis=1, keepdims=True)
            d_intra = jnp.where(causal, b_col - b_row + i_row, NEG)
            d_inter = b_col + m_st
            mr = jnp.maximum(d_inter, jnp.max(d_intra, axis=1, keepdims=True))
            w_intra.append(jnp.exp(d_intra - mr))
            w_inter.append(jnp.exp(d_inter - mr))
            m_row.append(mr)
            d_state = g_tot - b_col + i_col
            mn = jnp.maximum(g_tot + m_st, jnp.max(d_state, axis=0, keepdims=True))
            w_state.append(jnp.exp(d_state - mn))
            carry.append(jnp.exp(g_tot + m_st - mn))
            m_new.append(mn)
        qh = [hs(q_all, h) for h in heads]
        kh = [hs(k_all, h) for h in heads]
        vh = [v_ref[0, :, h * D:(h + 1) * D] for h in heads]
        c_st = [c_ref[h] for h in heads]
        n_st = [n_ref[h] for h in heads]
        s = [_dot_nt(qh[h], kh[h]) * w_intra[h] for h in heads]
        qc = [_dot(qh[h], c_st[h]) for h in heads]
        kw = [kh[h] * w_state[h] for h in heads]
        kv = [_dot_tn(kw[h], vh[h]) for h in heads]
        sv = [_dot(s[h], vh[h]) for h in heads]
        for h in heads:
            num = sv[h] + w_inter[h] * qc[h]
            den = jnp.sum(s[h], axis=1, keepdims=True) + w_inter[h] * jnp.sum(qh[h] * n_st[h], axis=1, keepdims=True)
            hval = num / jnp.maximum(jnp.abs(den), jnp.exp(-m_row[h]))
            c_ref[h] = carry[h] * c_st[h] + kv[h]
            n_ref[h] = carry[h] * n_st[h] + jnp.sum(kw[h], axis=0, keepdims=True)
            m_ref[h] = m_new[h]
            hg = _sigmoid(og_ref[0, :, h * D:(h + 1) * D]) * hval
            hc = hg - jnp.mean(hg, axis=1, keepdims=True)
            hn = hc * lax.rsqrt(jnp.mean(hc * hc, axis=1, keepdims=True) + EPS)
            out_ref[0, :, h * D:(h + 1) * D] = (hn * ng_ref[:, h * D:(h + 1) * D]).astype(BF16)

    col = lambda j: (lambda b, c: (b, c, j))
    fix = lambda j: (lambda b, c: (0, j))
    return pl.pallas_call(
        body,
        grid=(bsz, t // L),
        in_specs=[pl.BlockSpec((1, L, W), col(0)),
                  pl.BlockSpec((1, L, W), col(1)),
                  pl.BlockSpec((1, L, W), col(2)),
                  pl.BlockSpec((1, L, W), col(3)),
                  pl.BlockSpec((1, L, 2 * H), lambda b, c: (b, c, 0)),
                  pl.BlockSpec((1, 2 * H, L), lambda b, c: (b, 0, c)),
                  pl.BlockSpec((1, 2 * H), fix(0)),
                  pl.BlockSpec((2 * H, 1), fix(0)),
                  pl.BlockSpec((ML_CONV, W), fix(0)),
                  pl.BlockSpec((ML_CONV, W), fix(1)),
                  pl.BlockSpec((1, W), fix(0)),
                  pl.BlockSpec((1, W), fix(1)),
                  pl.BlockSpec((1, W), fix(0))],
        out_specs=pl.BlockSpec((1, L, W), lambda b, c: (b, c, 0)),
        out_shape=jax.ShapeDtypeStruct((bsz, t, W), BF16),
        scratch_shapes=[pltpu.VMEM((L + 8, W), F32), pltpu.VMEM((L + 8, W), F32),
                        pltpu.VMEM((H, D, D), F32), pltpu.VMEM((H, 1, D), F32), pltpu.VMEM((H, 1, 1), F32)],
        compiler_params=_cp("parallel", "arbitrary"),
        name="mlstm",
    )(p0, p0, p0, p0, gates_col, gates_row, gate_b_row, gate_b_col, conv_w, conv_w, conv_b, conv_b, norm_g)


GLA_TC = 256
GLA_SUB = 16
GLA_GROUP = 128


def _gla(p0, g_low, gate_up, gate_b, norm_g):
    bsz, t, _ = p0.shape
    tc, S, GB = GLA_TC, GLA_SUB, GLA_GROUP
    nsub = tc // S
    dk, dv = GLA_DK, GLA_DV

    def body(q_ref, k_ref, v_ref, gr_ref, gl_ref, gu_ref, gbias_ref, ng_ref, out_ref, st_ref):
        c = pl.program_id(2)

        @pl.when(c == 0)
        def _():
            st_ref[...] = jnp.zeros_like(st_ref)

        z = _dot(gl_ref[0], gu_ref[...]) + gbias_ref[...]
        la = _log_sigmoid(z) / GLA_TAU
        rowi = lax.broadcasted_iota(jnp.int32, (tc, 1), 0)
        rmod = rowi & (S - 1)
        bcum = la
        rsum = la
        for s in (1, 2, 4, 8):
            bcum = bcum + jnp.where(rmod >= s, pltpu.roll(bcum, s, 0), 0.0)
            rsum = rsum + jnp.where(rmod < S - s, pltpu.roll(rsum, tc - s, 0), 0.0)
        q = q_ref[0] * (dk ** -0.5)
        k = k_ref[0]
        v = v_ref[0]
        qt = q * jnp.exp(bcum)
        kt = k * jnp.exp(rsum - la)
        eg = jnp.exp(bcum + rsum - la)

        coli = lax.broadcasted_iota(jnp.int32, (tc, GB), 1)
        rgrp = rowi & (GB - 1)
        att0 = jnp.zeros((tc, GB), F32)
        att1 = jnp.zeros((tc, GB), F32)
        for d in range(S):
            kd = k if d == 0 else pltpu.roll(k, d, 0)
            bd = bcum if d == 0 else pltpu.roll(bcum, d, 0)
            valid = rmod >= d
            e = jnp.exp(jnp.where(valid, bcum - bd, 0.0))
            prod = q * kd * e
            w0 = jnp.where(valid, jnp.sum(prod[:, 0:dk], axis=1, keepdims=True), 0.0)
            w1 = jnp.where(valid, jnp.sum(prod[:, dk:2 * dk], axis=1, keepdims=True), 0.0)
            here = coli == rgrp - d
            att0 = jnp.where(here, w0, att0)
            att1 = jnp.where(here, w1, att1)

        heads = range(2)
        lk = [slice(hh * dk, (hh + 1) * dk) for hh in heads]
        lv = [slice(hh * dv, (hh + 1) * dv) for hh in heads]
        kv = [[_dot_tn(v[si * S:(si + 1) * S, lv[hh]], kt[si * S:(si + 1) * S, lk[hh]]) for hh in heads]
              for si in range(nsub)]
        st = [st_ref[hh] for hh in heads]
        inter = [[], []]
        for si in range(nsub):
            rows = slice(si * S, (si + 1) * S)
            for hh in heads:
                inter[hh].append(_dot_nt(qt[rows, lk[hh]], st[hh]))
                st[hh] = st[hh] * eg[si * S:si * S + 1, lk[hh]] + kv[si][hh]
        for hh in heads:
            st_ref[hh] = st[hh]

        for hh, att in ((0, att0), (1, att1)):
            diag = jnp.concatenate([_dot(att[g * GB:(g + 1) * GB], v[g * GB:(g + 1) * GB, lv[hh]])
                                    for g in range(tc // GB)], axis=0)
            o = diag + jnp.concatenate(inter[hh], axis=0)
            on = o * lax.rsqrt(jnp.mean(o * o, axis=1, keepdims=True) + EPS)
            out_ref[0, :, lv[hh]] = (on * ng_ref[:, lv[hh]] * _silu(gr_ref[0, :, lv[hh]])).astype(BF16)

    return pl.pallas_call(
        body,
        grid=(bsz, GLA_HEADS // 2, t // tc),
        in_specs=[pl.BlockSpec((1, tc, 2 * dk), lambda b, h, c: (b, c, 16 + h)),
                  pl.BlockSpec((1, tc, 2 * dk), lambda b, h, c: (b, c, 18 + h)),
                  pl.BlockSpec((1, tc, 2 * dv), lambda b, h, c: (b, c, 10 + h)),
                  pl.BlockSpec((1, tc, 2 * dv), lambda b, h, c: (b, c, 12 + h)),
                  pl.BlockSpec((1, tc, GLA_RANK), lambda b, h, c: (b, c, 0)),
                  pl.BlockSpec((GLA_RANK, 2 * dk), lambda b, h, c: (0, h)),
                  pl.BlockSpec((1, 2 * dk), lambda b, h, c: (0, h)),
                  pl.BlockSpec((1, 2 * dv), lambda b, h, c: (0, h))],
        out_specs=pl.BlockSpec((1, tc, 2 * dv), lambda b, h, c: (b, c, h)),
        out_shape=jax.ShapeDtypeStruct((bsz, t, GLA_HEADS * dv), BF16),
        scratch_shapes=[pltpu.VMEM((2, dv, dk), F32)],
        compiler_params=_cp("parallel", "parallel", "arbitrary"),
        name="gla",
    )(p0, p0, p0, p0, g_low, gate_up, gate_b, norm_g)


RET_L = 256


def _retention(p_ret, cos_t, sin_t, intra, inter, sdec, cdec, norm_g):
    bsz, t, _ = p_ret.shape
    L, D, H, W = RET_L, RET_DIM, RET_HEADS, RET_W

    def body(q_ref, k_ref, v_ref, g_ref, cos_ref, sin_ref, intra_ref, inter_ref, sdec_ref, cdec_ref, ng_ref,
             out_ref, s_ref):
        c = pl.program_id(1)

        @pl.when(c == 0)
        def _():
            s_ref[...] = jnp.zeros_like(s_ref)

        cs = cos_ref[...]
        sn = sin_ref[...]

        def rot(x):
            return x * cs + pltpu.roll(x, D // 2, 1) * sn

        heads = range(H)
        q = [rot(q_ref[0, :, h * D:(h + 1) * D]) * (D ** -0.5) for h in heads]
        k = [rot(k_ref[0, :, h * D:(h + 1) * D]) for h in heads]
        v = [v_ref[0, :, h * D:(h + 1) * D] for h in heads]
        s_st = [s_ref[h] for h in heads]
        s = [_dot_nt(q[h], k[h]) * intra_ref[h] for h in heads]
        qs = [_dot(q[h], s_st[h]) for h in heads]
        kv = [_dot_tn(k[h] * sdec_ref[h], v[h]) for h in heads]
        sv = [_dot(s[h], v[h]) for h in heads]
        for h in heads:
            o = sv[h] + inter_ref[h] * qs[h]
            s_ref[h] = cdec_ref[h] * s_st[h] + kv[h]
            oc = o - jnp.mean(o, axis=1, keepdims=True)
            on = oc * lax.rsqrt(jnp.mean(oc * oc, axis=1, keepdims=True) + EPS)
            hl = slice(h * D, (h + 1) * D)
            out_ref[0, :, hl] = (on * ng_ref[:, hl] * _silu(g_ref[0, :, hl])).astype(BF16)

    col = lambda j: (lambda b, c: (b, c, j))
    fix3 = lambda b, c: (0, 0, 0)
    return pl.pallas_call(
        body,
        grid=(bsz, t // L),
        in_specs=[pl.BlockSpec((1, L, W), col(0)),
                  pl.BlockSpec((1, L, W), col(1)),
                  pl.BlockSpec((1, L, W), col(2)),
                  pl.BlockSpec((1, L, W), col(3)),
                  pl.BlockSpec((L, D), lambda b, c: (c, 0)),
                  pl.BlockSpec((L, D), lambda b, c: (c, 0)),
                  pl.BlockSpec((H, L, L), fix3),
                  pl.BlockSpec((H, L, 1), fix3),
                  pl.BlockSpec((H, L, 1), fix3),
                  pl.BlockSpec((H, 1, 1), fix3),
                  pl.BlockSpec((1, W), lambda b, c: (0, 0))],
        out_specs=pl.BlockSpec((1, L, W), lambda b, c: (b, c, 0)),
        out_shape=jax.ShapeDtypeStruct((bsz, t, W), BF16),
        scratch_shapes=[pltpu.VMEM((H, D, D), F32)],
        compiler_params=_cp("parallel", "arbitrary"),
        name="retention",
    )(p_ret, p_ret, p_ret, p_ret, cos_t, sin_t, intra, inter, sdec, cdec, norm_g)


def _retention_tables(t):
    L, D = RET_L, RET_DIM
    inv = ROPE_BASE ** (-jnp.arange(0, D, 2, dtype=F32) / D)
    ang = jnp.arange(t).astype(F32)[:, None] * inv[None, :]
    cos = jnp.cos(ang)
    sin = jnp.sin(ang)
    cos_t = jnp.concatenate([cos, cos], axis=-1)
    sin_t = jnp.concatenate([-sin, sin], axis=-1)
    log_gamma = jnp.log1p(-jnp.exp2(-5.0 - jnp.arange(RET_HEADS, dtype=F32)))
    idx = jnp.arange(L, dtype=F32)
    causal = idx[:, None] >= idx[None, :]
    rel = jnp.where(causal, idx[:, None] - idx[None, :], 0.0)
    intra = jnp.where(causal, jnp.exp(log_gamma[:, None, None] * rel), 0.0)
    inter = jnp.exp(log_gamma[:, None] * (idx + 1.0))[:, :, None]
    sdec = jnp.exp(log_gamma[:, None] * (L - 1.0 - idx))[:, :, None]
    cdec = jnp.exp(log_gamma * L)[:, None, None]
    return cos_t, sin_t, intra, inter, sdec, cdec


RWP_TM = 256


def _seg_sum(x, bd):
    hi = x.astype(BF16)
    lo = (x - hi.astype(F32)).astype(BF16)
    return jnp.dot(hi, bd, preferred_element_type=F32) + jnp.dot(lo, bd, preferred_element_type=F32)


def _rwkv_prep(p1, mu, w_up, w0, a_up, a0, g_up, k_k, k_a, r_k, bd):
    bsz, t, _ = p1.shape
    tm, W = RWP_TM, RW_W

    def body(x_ref, xp_ref, mu_ref, wup_ref, w0_ref, aup_ref, a0_ref, gup_ref, kk_ref, ka_ref, rk_ref, bd_ref,
             r_out, lw_out, k_out, v_out, a_out, b_out, g_out, bonus_out):
        c = pl.program_id(1)
        cur = x_ref[0]
        prev = jnp.where(c == 0, 0.0, xp_ref[0, 7:8, :])
        rowi = lax.broadcasted_iota(jnp.int32, (tm, 1), 0)
        sh = jnp.where(rowi == 0, prev, pltpu.roll(cur, 1, 0))
        xm = cur + (sh - cur) * mu_ref[...]
        x_r = xm[:, 0:W]
        x_k = xm[:, W:2 * W]
        x_v = xm[:, 2 * W:3 * W]
        x_dl = xm[:, 3 * W:3 * W + 64]
        x_al = xm[:, 3 * W + 64:3 * W + 128]
        x_gl = xm[:, 3 * W + 128:3 * W + 256]
        wl = w0_ref[...] + _dot(jnp.tanh(x_dl), wup_ref[...])
        sp = jnp.maximum(-wl, 0.0) + jnp.log(1.0 + jnp.exp(-jnp.abs(wl)))
        lw_out[0] = -jnp.exp(-sp - 0.5)
        a = _sigmoid(a0_ref[...] + _dot(x_al, aup_ref[...]))
        g_out[0] = _dot(_sigmoid(x_gl), gup_ref[...])
        kk0 = x_k * kk_ref[...]
        nrm = jnp.sqrt(_seg_sum(kk0 * kk0, bd_ref[...]))
        kk = kk0 / jnp.maximum(nrm, 1e-12)
        k_h = x_k * (1.0 + (a - 1.0) * ka_ref[...])
        r_out[0] = x_r
        k_out[0] = k_h
        v_out[0] = x_v
        a_out[0] = -kk
        b_out[0] = kk * a
        bonus_out[0] = _seg_sum(x_r * k_h * rk_ref[...], bd_ref[...]) * x_v

    row = lambda b, c: (0, 0)
    blk = pl.BlockSpec((1, tm, W), lambda b, c: (b, c, 0))
    shp = jax.ShapeDtypeStruct((bsz, t, W), F32)
    return pl.pallas_call(
        body,
        grid=(bsz, t // tm),
        in_specs=[pl.BlockSpec((1, tm, RW_COLS), lambda b, c: (b, c, 0)),
                  pl.BlockSpec((1, 8, RW_COLS), lambda b, c: (b, jnp.maximum(c * (tm // 8) - 1, 0), 0)),
                  pl.BlockSpec((1, RW_COLS), row),
                  pl.BlockSpec((64, W), row), pl.BlockSpec((1, W), row),
                  pl.BlockSpec((64, W), row), pl.BlockSpec((1, W), row),
                  pl.BlockSpec((128, W), row),
                  pl.BlockSpec((1, W), row), pl.BlockSpec((1, W), row), pl.BlockSpec((1, W), row),
                  pl.BlockSpec((W, W), row)],
        out_specs=[blk] * 8,
        out_shape=[shp] * 8,
        compiler_params=_cp("parallel", "arbitrary"),
        name="rwkv_prep",
    )(p1, p1, mu, w_up, w0, a_up, a0, g_up, k_k, k_a, r_k, bd)


RW_L = 64
RW_TB = 128


def _rwkv_scan(r, lw, k, v, aa, bb, g_out, bonus, ln_g, ln_b):
    bsz, t, W = r.shape
    L, N, tb = RW_L, RW_DIM, RW_TB
    nck = tb // L

    def body(r_ref, lw_ref, k_ref, v_ref, a_ref, b_ref, g_ref, bonus_ref, lng_ref, lnb_ref, out_ref, h_ref):
        c = pl.program_id(1)

        @pl.when(c == 0)
        def _():
            h_ref[...] = jnp.zeros_like(h_ref)

        lw_all = lw_ref[0]
        rowi = lax.broadcasted_iota(jnp.int32, (tb, 1), 0) & (L - 1)
        cl = lw_all
        for s in (1, 2, 4, 8, 16, 32):
            cl = cl + jnp.where(rowi >= s, pltpu.roll(cl, s, 0), 0.0)
        cl_last = jnp.concatenate([jnp.broadcast_to(cl[(cc + 1) * L - 1:(cc + 1) * L, :], (L, W))
                                   for cc in range(nck)], axis=0)
        e_inv = jnp.exp(-cl)
        e_end = jnp.exp(cl_last - cl)
        p_end = jnp.exp(cl_last)
        at = a_ref[0] * jnp.exp(cl - lw_all)
        rt = r_ref[0] * jnp.exp(cl)
        bt = b_ref[0] * e_inv
        kt = k_ref[0] * e_inv
        b_end = b_ref[0] * e_end
        k_end = k_ref[0] * e_end
        v_all = v_ref[0]
        ri = lax.broadcasted_iota(jnp.int32, (L, L), 0)
        ci = lax.broadcasted_iota(jnp.int32, (L, L), 1)
        strict = ri > ci
        lower = ri >= ci
        eye = ri == ci
        heads = range(nck * RW_HEADS)

        def hs(z, u):
            cc, hh = divmod(u, RW_HEADS)
            return z[cc * L:(cc + 1) * L, hh * N:(hh + 1) * N]

        m = [_dot_nt(jnp.concatenate([hs(at, h), hs(rt, h)], axis=0),
                     jnp.concatenate([hs(bt, h), hs(kt, h)], axis=0)) for h in heads]
        a_ab = [jnp.where(strict, m[h][0:L, 0:L], 0.0) for h in heads]
        a_ak = [jnp.where(strict, m[h][0:L, L:2 * L], 0.0) for h in heads]
        a_rb = [jnp.where(lower, m[h][L:2 * L, 0:L], 0.0) for h in heads]
        a_rk = [jnp.where(lower, m[h][L:2 * L, L:2 * L], 0.0) for h in heads]
        end_t = [jnp.concatenate([hs(b_end, h), hs(k_end, h)], axis=1).T for h in heads]
        vp = [_dot(jnp.concatenate([a_ak[h], a_rk[h], end_t[h][N:2 * N]], axis=0), hs(v_all, h)) for h in heads]
        x = [jnp.concatenate([hs(at, h), vp[h][0:L]], axis=1) for h in heads]
        ap = a_ab
        for it in range(6):
            x = [x[h] + _dot(ap[h], x[h]) for h in heads]
            if it < 5:
                ap = [_dot(ap[h], ap[h]) for h in heads]
        post = [_dot(jnp.concatenate([a_rb[h], end_t[h][0:N]], axis=0), x[h]) for h in heads]
        lhs = []
        y0s = []
        h_adds = []
        for u in heads:
            q_hat = hs(rt, u) + post[u][0:L, 0:N]
            gmat = jnp.where(eye, hs(p_end, u), 0.0) + post[u][L:L + N, 0:N]
            lhs.append(jnp.concatenate([q_hat, gmat], axis=0))
            y0s.append(post[u][0:L, N:2 * N] + vp[u][L:2 * L])
            h_adds.append(post[u][L:L + N, N:2 * N] + vp[u][2 * L:2 * L + N])
        h_st = [h_ref[hh] for hh in range(RW_HEADS)]
        for cc in range(nck):
            outs = []
            for hh in range(RW_HEADS):
                u = cc * RW_HEADS + hh
                res = _dot(lhs[u], h_st[hh])
                y = res[0:L] + y0s[u]
                h_st[hh] = res[L:L + N] + h_adds[u]
                yc = y - jnp.mean(y, axis=1, keepdims=True)
                outs.append(yc * lax.rsqrt(jnp.mean(yc * yc, axis=1, keepdims=True) + RW_LN_EPS))
            yn = jnp.concatenate(outs, axis=1)
            rows = slice(cc * L, (cc + 1) * L)
            out_ref[0, rows, :] = ((yn * lng_ref[...] + lnb_ref[...] + bonus_ref[0, rows, :])
                                   * g_ref[0, rows, :]).astype(BF16)
        for hh in range(RW_HEADS):
            h_ref[hh] = h_st[hh]

    blk = pl.BlockSpec((1, tb, W), lambda b, c: (b, c, 0))
    vec = pl.BlockSpec((1, W), lambda b, c: (0, 0))
    return pl.pallas_call(
        body,
        grid=(bsz, t // tb),
        in_specs=[blk] * 8 + [vec, vec],
        out_specs=blk,
        out_shape=jax.ShapeDtypeStruct((bsz, t, W), BF16),
        scratch_shapes=[pltpu.VMEM((RW_HEADS, N, N), F32)],
        compiler_params=_cp("parallel", "arbitrary"),
        name="rwkv_scan",
    )(r, lw, k, v, aa, bb, g_out, bonus, ln_g, ln_b)


def _router(x2d, g, router_pad, tm=512):
    n, d = x2d.shape
    e = router_pad.shape[1]

    def body(x_ref, g_ref, r_ref, xn_ref, lg_ref):
        xn = _rms_rows(x_ref[...], g_ref[...])
        xn_ref[...] = xn.astype(BF16)
        lg_ref[...] = jnp.dot(xn, r_ref[...], preferred_element_type=F32, precision=lax.Precision.HIGHEST)

    return pl.pallas_call(
        body,
        grid=(n // tm,),
        in_specs=[pl.BlockSpec((tm, d), lambda i: (i, 0)),
                  pl.BlockSpec((1, d), lambda i: (0, 0)),
                  pl.BlockSpec((d, e), lambda i: (0, 0))],
        out_specs=[pl.BlockSpec((tm, d), lambda i: (i, 0)), pl.BlockSpec((tm, e), lambda i: (i, 0))],
        out_shape=[jax.ShapeDtypeStruct((n, d), BF16), jax.ShapeDtypeStruct((n, e), F32)],
        compiler_params=_cp("parallel"),
        name="moe_router",
    )(x2d, g.reshape(1, d), router_pad)


MOE_TM = 512
MOE_TF = 896


def _experts(xs, row_w, item_tile, item_exp, item_lo, item_hi, wg, wu, wd):
    nrows, d = xs.shape
    tm, tf = MOE_TM, MOE_TF
    nf = D_FF // tf
    n_items = item_tile.shape[0]

    def body(it_ref, ie_ref, lo_ref, hi_ref, x_ref, w_ref, wg_ref, wu_ref, wd_ref, o_ref, acc_ref):
        i = pl.program_id(0)
        j = pl.program_id(1)
        first = jnp.logical_or(i == 0, it_ref[i] != it_ref[jnp.maximum(i - 1, 0)])

        @pl.when(jnp.logical_and(first, j == 0))
        def _():
            o_ref[...] = jnp.zeros_like(o_ref)

        lo = lo_ref[i]
        hi = hi_ref[i]

        @pl.when(lo < hi)
        def _():
            x = x_ref[...]
            gg = jnp.dot(x, wg_ref[0], preferred_element_type=F32)
            uu = jnp.dot(x, wu_ref[0], preferred_element_type=F32)
            act = (_silu(gg) * uu).astype(BF16)
            part = jnp.dot(act, wd_ref[0], preferred_element_type=F32)

            @pl.when(j == 0)
            def _():
                acc_ref[...] = part

            @pl.when(j > 0)
            def _():
                acc_ref[...] += part

            @pl.when(j == nf - 1)
            def _():
                rowi = lax.broadcasted_iota(jnp.int32, (tm, 1), 0)
                mine = jnp.logical_and(rowi >= lo, rowi < hi)
                o_ref[...] += acc_ref[...] * jnp.where(mine, w_ref[...], 0.0)

    grid_spec = pltpu.PrefetchScalarGridSpec(
        num_scalar_prefetch=4,
        grid=(n_items, nf),
        in_specs=[pl.BlockSpec((tm, d), lambda i, j, it, ie, lo, hi: (it[i], 0)),
                  pl.BlockSpec((tm, 1), lambda i, j, it, ie, lo, hi: (it[i], 0)),
                  pl.BlockSpec((1, d, tf), lambda i, j, it, ie, lo, hi: (ie[i], 0, j)),
                  pl.BlockSpec((1, d, tf), lambda i, j, it, ie, lo, hi: (ie[i], 0, j)),
                  pl.BlockSpec((1, tf, d), lambda i, j, it, ie, lo, hi: (ie[i], j, 0))],
        out_specs=pl.BlockSpec((tm, d), lambda i, j, it, ie, lo, hi: (it[i], 0)),
        scratch_shapes=[pltpu.VMEM((tm, d), F32)],
    )
    return pl.pallas_call(
        body,
        grid_spec=grid_spec,
        out_shape=jax.ShapeDtypeStruct((nrows, d), F32),
        compiler_params=_cp("arbitrary", "arbitrary"),
        name="moe_experts",
    )(item_tile, item_exp, item_lo, item_hi, xs, row_w, wg, wu, wd)


def _combine_norm(h, y0, y1, g, tm=512):
    n, d = h.shape

    def body(h_ref, a_ref, b_ref, g_ref, o_ref):
        o_ref[...] = _rms_rows(h_ref[...] + (a_ref[...] + b_ref[...]), g_ref[...])

    blk = pl.BlockSpec((tm, d), lambda i: (i, 0))
    return pl.pallas_call(
        body,
        grid=(n // tm,),
        in_specs=[blk, blk, blk, pl.BlockSpec((1, d), lambda i: (0, 0))],
        out_specs=blk,
        out_shape=jax.ShapeDtypeStruct((n, d), F32),
        compiler_params=_cp("parallel"),
        name="combine_norm",
    )(h, y0, y1, g.reshape(1, d))


def _route(logits, n):
    tm = MOE_TM
    na = n * TOP_K
    n_tiles = na // tm
    top_val, top_idx = lax.top_k(logits, TOP_K)
    top_w = jax.nn.softmax(top_val, axis=-1)
    e_flat = top_idx.reshape(-1).astype(jnp.int32)
    w_flat = top_w.reshape(-1)
    tok = jnp.arange(na, dtype=jnp.int32) // TOP_K
    _, sorted_tok, sorted_w = lax.sort((e_flat, tok, w_flat), num_keys=1, is_stable=True)
    onehot = (e_flat[:, None] == jnp.arange(N_EXPERTS, dtype=jnp.int32)[None, :]).astype(jnp.int32)
    rank = jnp.take_along_axis(jnp.cumsum(onehot, axis=0), e_flat[:, None], axis=1)[:, 0] - 1
    counts = jnp.sum(onehot, axis=0)
    ends = jnp.cumsum(counts)
    pos = (ends - counts)[e_flat] + rank
    cuts = jnp.sort(jnp.concatenate([jnp.arange(n_tiles, dtype=jnp.int32) * tm, ends[:-1].astype(jnp.int32)]))
    nxt = jnp.concatenate([cuts[1:], jnp.full((1,), na, jnp.int32)])
    item_tile = jnp.minimum(cuts // tm, n_tiles - 1)
    item_exp = jnp.minimum(jnp.searchsorted(ends, cuts, side="right"), N_EXPERTS - 1).astype(jnp.int32)
    item_lo = cuts - item_tile * tm
    item_hi = nxt - item_tile * tm
    return sorted_tok, sorted_w, (item_tile, item_exp, item_lo, item_hi), pos.reshape(n, TOP_K)


def kernel(x, e_norm1_g, e_w_in, e_ml_conv_w, e_ml_conv_b, e_ml_gate_b, e_ml_norm_g, e_gla_gate_up, e_gla_gate_b,
           e_gla_norm_g, e_w_out, e_norm2_g, e_ffn_w_gate, e_ffn_w_up, e_ffn_w_down, o_norm1_g, o_w_in,
           o_ret_norm_g, o_rw_mu, o_rw_w_up, o_rw_w0, o_rw_a_up, o_rw_a0, o_rw_g_up, o_rw_k_k, o_rw_k_a, o_rw_r_k,
           o_rw_ln_g, o_rw_ln_b, o_w_out, o_norm2_g, o_moe_router, o_moe_w_gate, o_moe_w_up, o_moe_w_down,
           final_norm_g):
    bsz, t, d = x.shape
    n = bsz * t
    h0 = x.reshape(n, d)

    w = e_w_in[0]
    w0 = jnp.concatenate([w[:, :2048], w[:, 2056:3592], w[:, 2048:2056], w[:, 3592:3608],
                          jnp.zeros((d, E_COLS_PAD - 3608), F32)], axis=1).astype(BF16)
    p0 = _norm_matmul(h0, e_norm1_g[0], w0).reshape(bsz, t, E_COLS_PAD)
    gates_col = p0[:, :, 3584:3592]
    gates_row = jnp.transpose(gates_col, (0, 2, 1))
    h_ml = _mlstm(p0, gates_col, gates_row, e_ml_gate_b[0].reshape(1, -1), e_ml_gate_b[0].reshape(-1, 1),
                  e_ml_conv_w[0], e_ml_conv_b[0].reshape(1, -1), e_ml_norm_g[0].reshape(1, -1))
    g_low = p0[:, :, 3592:3608]
    o_gla = _gla(p0, g_low, e_gla_gate_up[0].astype(BF16), e_gla_gate_b[0].reshape(1, -1),
                 e_gla_norm_g[0].reshape(1, -1))
    h1 = _mix_out(h_ml.reshape(n, -1), o_gla.reshape(n, -1), e_w_out[0].astype(BF16), h0)
    h2 = _ffn(h1, e_norm2_g[0], e_ffn_w_gate[0].astype(BF16), e_ffn_w_up[0].astype(BF16),
              e_ffn_w_down[0].astype(BF16))

    w = o_w_in[0]
    p_ret = _norm_matmul(h2, o_norm1_g[0], w[:, :4 * RET_W].astype(BF16), tn=1024).reshape(bsz, t, -1)
    p1 = _norm_matmul(h2, o_norm1_g[0], w[:, 4 * RET_W:].astype(BF16), tn=896).reshape(bsz, t, -1)
    y_ret = _retention(p_ret, *_retention_tables(t), o_ret_norm_g[0].reshape(1, -1))
    head_of = jnp.arange(RW_W) // RW_DIM
    bd = (head_of[:, None] == head_of[None, :]).astype(BF16)
    row = lambda a: a.reshape(1, -1)
    r, lw, k, v, aa, bb, g_out, bonus = _rwkv_prep(
        p1, row(o_rw_mu[0]), o_rw_w_up[0].astype(BF16), row(o_rw_w0[0]), o_rw_a_up[0].astype(BF16),
        row(o_rw_a0[0]), o_rw_g_up[0].astype(BF16), row(o_rw_k_k[0]), row(o_rw_k_a[0]), row(o_rw_r_k[0]), bd)
    y_rw = _rwkv_scan(r, lw, k, v, aa, bb, g_out, bonus, row(o_rw_ln_g[0]), row(o_rw_ln_b[0]))
    h3 = _mix_out(y_ret.reshape(n, -1), y_rw.reshape(n, -1), o_w_out[0].astype(BF16), h2)

    router_pad = jnp.zeros((d, 128), F32).at[:, :N_EXPERTS].set(o_moe_router[0])
    xn, logits = _router(h3, o_norm2_g[0], router_pad)
    sorted_tok, sorted_w, items, pos = _route(logits[:, :N_EXPERTS], n)
    xs = xn.at[sorted_tok].get(mode="promise_in_bounds")
    ys = _experts(xs, sorted_w.reshape(-1, 1), *items, o_moe_w_gate[0].astype(BF16),
                  o_moe_w_up[0].astype(BF16), o_moe_w_down[0].astype(BF16))
    y0 = ys.at[pos[:, 0]].get(mode="promise_in_bounds")
    y1 = ys.at[pos[:, 1]].get(mode="promise_in_bounds")
    out = _combine_norm(h3, y0, y1, final_norm_g)
    return out.reshape(bsz, t, d)
```

```python
import functools

import numpy as np
import jax
import jax.numpy as jnp
from jax import lax
from jax.experimental import pallas as pl
from jax.experimental.pallas import tpu as pltpu

F32 = jnp.float32
BF16 = jnp.bfloat16

D_MODEL = 1024
EPS = 1e-6
ML_HEADS, ML_DIM, ML_W, ML_CONV = 4, 128, 512, 4
GLA_HEADS, GLA_DK, GLA_DV, GLA_RANK, GLA_TAU = 4, 64, 128, 16, 16.0
E_COLS_PAD = 3840
RET_HEADS, RET_DIM, RET_W = 4, 128, 512
ROPE_BASE = 10000.0
RW_HEADS, RW_DIM, RW_W = 8, 64, 512
RW_COLS = 1792
RW_LN_EPS = 64e-5
D_FF = 3584
N_EXPERTS = 8
TOP_K = 2

VMEM_LIMIT = 48 * 1024 * 1024
NEG = -1e30


def _cp(*sem):
    return pltpu.CompilerParams(dimension_semantics=sem, vmem_limit_bytes=VMEM_LIMIT)


def _sigmoid(x):
    return 1.0 / (1.0 + jnp.exp(-x))


def _silu(x):
    return x * _sigmoid(x)


def _log_sigmoid(x):
    return jnp.minimum(x, 0.0) - jnp.log(1.0 + jnp.exp(-jnp.abs(x)))


def _dot(a, b):
    return jnp.dot(a.astype(BF16), b.astype(BF16), preferred_element_type=F32)


def _dot_nt(a, b):
    return lax.dot_general(a.astype(BF16), b.astype(BF16), (((1,), (1,)), ((), ())), preferred_element_type=F32)


def _dot_tn(a, b):
    return jnp.dot(a.T.astype(BF16), b.astype(BF16), preferred_element_type=F32)


def _rms_rows(x, g):
    ms = jnp.mean(x * x, axis=-1, keepdims=True)
    return x * lax.rsqrt(ms + EPS) * g


def _norm_matmul(x2d, g, w_bf16, tm=1024, tn=1280):
    n, d = x2d.shape
    c = w_bf16.shape[1]

    def body(x_ref, g_ref, w_ref, o_ref, xn_ref):
        @pl.when(pl.program_id(1) == 0)
        def _():
            xn_ref[...] = _rms_rows(x_ref[...], g_ref[...]).astype(BF16)

        o_ref[...] = jnp.dot(xn_ref[...], w_ref[...], preferred_element_type=F32)

    return pl.pallas_call(
        body,
        grid=(n // tm, c // tn),
        in_specs=[pl.BlockSpec((tm, d), lambda i, j: (i, 0)),
                  pl.BlockSpec((1, d), lambda i, j: (0, 0)),
                  pl.BlockSpec((d, tn), lambda i, j: (0, j))],
        out_specs=pl.BlockSpec((tm, tn), lambda i, j: (i, j)),
        out_shape=jax.ShapeDtypeStruct((n, c), F32),
        scratch_shapes=[pltpu.VMEM((tm, d), BF16)],
        compiler_params=_cp("parallel", "arbitrary"),
        name="norm_matmul",
    )(x2d, g.reshape(1, d), w_bf16)


def _mix_out(a, b, w_bf16, resid, tm=1024):
    n, wa = a.shape
    wb = b.shape[1]
    d = w_bf16.shape[1]

    def body(a_ref, b_ref, w_ref, r_ref, o_ref):
        acc = jnp.dot(a_ref[...], w_ref[0:wa, :], preferred_element_type=F32)
        acc = acc + jnp.dot(b_ref[...], w_ref[wa:wa + wb, :], preferred_element_type=F32)
        o_ref[...] = r_ref[...] + acc

    return pl.pallas_call(
        body,
        grid=(n // tm,),
        in_specs=[pl.BlockSpec((tm, wa), lambda i: (i, 0)),
                  pl.BlockSpec((tm, wb), lambda i: (i, 0)),
                  pl.BlockSpec((wa + wb, d), lambda i: (0, 0)),
                  pl.BlockSpec((tm, d), lambda i: (i, 0))],
        out_specs=pl.BlockSpec((tm, d), lambda i: (i, 0)),
        out_shape=jax.ShapeDtypeStruct((n, d), F32),
        compiler_params=_cp("parallel"),
        name="mix_out",
    )(a, b, w_bf16, resid)


def _ffn(x2d, g, wg, wu, wd, tm=512, tf=1792):
    n, d = x2d.shape
    f = wg.shape[1]
    nf = f // tf

    def body(x_ref, g_ref, wg_ref, wu_ref, wd_ref, o_ref, xn_ref, acc_ref):
        j = pl.program_id(1)

        @pl.when(j == 0)
        def _():
            xn_ref[...] = _rms_rows(x_ref[...], g_ref[...]).astype(BF16)
            acc_ref[...] = jnp.zeros_like(acc_ref)

        xn = xn_ref[...]
        gg = jnp.dot(xn, wg_ref[...], preferred_element_type=F32)
        uu = jnp.dot(xn, wu_ref[...], preferred_element_type=F32)
        act = (_silu(gg) * uu).astype(BF16)
        acc_ref[...] += jnp.dot(act, wd_ref[...], preferred_element_type=F32)

        @pl.when(j == nf - 1)
        def _():
            o_ref[...] = x_ref[...] + acc_ref[...]

    return pl.pallas_call(
        body,
        grid=(n // tm, nf),
        in_specs=[pl.BlockSpec((tm, d), lambda i, j: (i, 0)),
                  pl.BlockSpec((1, d), lambda i, j: (0, 0)),
                  pl.BlockSpec((d, tf), lambda i, j: (0, j)),
                  pl.BlockSpec((d, tf), lambda i, j: (0, j)),
                  pl.BlockSpec((tf, d), lambda i, j: (j, 0))],
        out_specs=pl.BlockSpec((tm, d), lambda i, j: (i, 0)),
        out_shape=jax.ShapeDtypeStruct((n, d), F32),
        scratch_shapes=[pltpu.VMEM((tm, d), BF16), pltpu.VMEM((tm, d), F32)],
        compiler_params=_cp("parallel", "arbitrary"),
        name="ffn",
    )(x2d, g.reshape(1, d), wg, wu, wd)


ML_L = 256


def _mlstm(p0, gates_col, gates_row, gate_b_row, gate_b_col, conv_w, conv_b, norm_g):
    bsz, t, _ = p0.shape
    L, D, H, W = ML_L, ML_DIM, ML_HEADS, ML_W

    def body(q_ref, k_ref, v_ref, og_ref, gc_ref, gr_ref, gbr_ref, gbc_ref, cwq_ref, cwk_ref, cbq_ref, cbk_ref,
             ng_ref, out_ref, qext, kext, c_ref, n_ref, m_ref):
        c = pl.program_id(1)

        @pl.when(c == 0)
        def _():
            qext[0:8, :] = jnp.zeros((8, W), F32)
            kext[0:8, :] = jnp.zeros((8, W), F32)
            c_ref[...] = jnp.zeros_like(c_ref)
            n_ref[...] = jnp.zeros_like(n_ref)
            m_ref[...] = jnp.zeros_like(m_ref)

        qext[8:, :] = q_ref[0]
        kext[8:, :] = k_ref[0]

        def conv(ext, cw_ref, cb_ref):
            acc = cb_ref[...] + cw_ref[0:1, :] * ext[pl.ds(8 - ML_CONV + 1, L), :]
            for kk in range(1, ML_CONV):
                acc = acc + cw_ref[kk:kk + 1, :] * ext[pl.ds(8 - ML_CONV + 1 + kk, L), :]
            return _silu(acc)

        q_all = conv(qext, cwq_ref, cbq_ref) * (D ** -0.5)
        k_all = conv(kext, cwk_ref, cbk_ref)
        qext[0:8, :] = qext[L:L + 8, :]
        kext[0:8, :] = kext[L:L + 8, :]

        gcol = gc_ref[0] + gbr_ref[...]
        grow = gr_ref[0] + gbc_ref[...]
        fcol = _log_sigmoid(gcol[:, H:2 * H])
        frow = _log_sigmoid(grow[H:2 * H, :])
        ri = lax.broadcasted_iota(jnp.int32, (L, L), 0)
        ci = lax.broadcasted_iota(jnp.int32, (L, L), 1)
        causal = ri >= ci
        heads = range(H)
        hs = lambda z, h: z[:, h * D:(h + 1) * D]
        w_intra, w_inter, w_state, carry, m_row, m_new = [], [], [], [], [], []
        for h in heads:
            f_row = frow[h:h + 1, :]
            i_row = grow[h:h + 1, :]
            f_col = fcol[:, h:h + 1]
            i_col = gcol[:, h:h + 1]
            m_st = m_ref[h]
            b_col = jnp.sum(jnp.where(causal, f_row, 0.0), axis=1, keepdims=True)
            b_row = jnp.sum(jnp.where(ri <= ci, f_col, 0.0), axis=0, keepdims=True)
            g_tot = jnp.sum(f_row, axis=1, keepdims=True)
            d_intra = jnp.where(causal, b_col - b_row + i_row, NEG)
            d_inter = b_col + m_st
            mr = jnp.maximum(d_inter, jnp.max(d_intra, axis=1, keepdims=True))
            w_intra.append(jnp.exp(d_intra - mr))
            w_inter.append(jnp.exp(d_inter - mr))
            m_row.append(mr)
            d_state = g_tot - b_col + i_col
            mn = jnp.maximum(g_tot + m_st, jnp.max(d_state, axis=0, keepdims=True))
            w_state.append(jnp.exp(d_state - mn))
            carry.append(jnp.exp(g_tot + m_st - mn))
            m_new.append(mn)
        qh = [hs(q_all, h) for h in heads]
        kh = [hs(k_all, h) for h in heads]
        vh = [v_ref[0, :, h * D:(h + 1) * D] for h in heads]
        c_st = [c_ref[h] for h in heads]
        n_st = [n_ref[h] for h in heads]
        s = [_dot_nt(qh[h], kh[h]) * w_intra[h] for h in heads]
        qc = [_dot(qh[h], c_st[h]) for h in heads]
        kw = [kh[h] * w_state[h] for h in heads]
        kv = [_dot_tn(kw[h], vh[h]) for h in heads]
        sv = [_dot(s[h], vh[h]) for h in heads]
        for h in heads:
            num = sv[h] + w_inter[h] * qc[h]
            den = jnp.sum(s[h], axis=1, keepdims=True) + w_inter[h] * jnp.sum(qh[h] * n_st[h], axis=1, keepdims=True)
            hval = num / jnp.maximum(jnp.abs(den), jnp.exp(-m_row[h]))
            c_ref[h] = carry[h] * c_st[h] + kv[h]
            n_ref[h] = carry[h] * n_st[h] + jnp.sum(kw[h], axis=0, keepdims=True)
            m_ref[h] = m_new[h]
            hg = _sigmoid(og_ref[0, :, h * D:(h + 1) * D]) * hval
            hc = hg - jnp.mean(hg, axis=1, keepdims=True)
            hn = hc * lax.rsqrt(jnp.mean(hc * hc, axis=1, keepdims=True) + EPS)
            out_ref[0, :, h * D:(h + 1) * D] = (hn * ng_ref[:, h * D:(h + 1) * D]).astype(BF16)

    col = lambda j: (lambda b, c: (b, c, j))
    fix = lambda j: (lambda b, c: (0, j))
    return pl.pallas_call(
        body,
        grid=(bsz, t // L),
        in_specs=[pl.BlockSpec((1, L, W), col(0)),
                  pl.BlockSpec((1, L, W), col(1)),
                  pl.BlockSpec((1, L, W), col(2)),
                  pl.BlockSpec((1, L, W), col(3)),
                  pl.BlockSpec((1, L, 2 * H), lambda b, c: (b, c, 0)),
                  pl.BlockSpec((1, 2 * H, L), lambda b, c: (b, 0, c)),
                  pl.BlockSpec((1, 2 * H), fix(0)),
                  pl.BlockSpec((2 * H, 1), fix(0)),
                  pl.BlockSpec((ML_CONV, W), fix(0)),
                  pl.BlockSpec((ML_CONV, W), fix(1)),
                  pl.BlockSpec((1, W), fix(0)),
                  pl.BlockSpec((1, W), fix(1)),
                  pl.BlockSpec((1, W), fix(0))],
        out_specs=pl.BlockSpec((1, L, W), lambda b, c: (b, c, 0)),
        out_shape=jax.ShapeDtypeStruct((bsz, t, W), BF16),
        scratch_shapes=[pltpu.VMEM((L + 8, W), F32), pltpu.VMEM((L + 8, W), F32),
                        pltpu.VMEM((H, D, D), F32), pltpu.VMEM((H, 1, D), F32), pltpu.VMEM((H, 1, 1), F32)],
        compiler_params=_cp("parallel", "arbitrary"),
        name="mlstm",
    )(p0, p0, p0, p0, gates_col, gates_row, gate_b_row, gate_b_col, conv_w, conv_w, conv_b, conv_b, norm_g)


GLA_TC = 256
GLA_SUB = 16
GLA_GROUP = 128


def _gla(p0, g_low, gate_up, gate_b, norm_g):
    bsz, t, _ = p0.shape
    tc, S, GB = GLA_TC, GLA_SUB, GLA_GROUP
    nsub = tc // S
    dk, dv = GLA_DK, GLA_DV

    def body(q_ref, k_ref, v_ref, gr_ref, gl_ref, gu_ref, gbias_ref, ng_ref, out_ref, st_ref):
        c = pl.program_id(2)

        @pl.when(c == 0)
        def _():
            st_ref[...] = jnp.zeros_like(st_ref)

        z = _dot(gl_ref[0], gu_ref[...]) + gbias_ref[...]
        la = _log_sigmoid(z) / GLA_TAU
        rowi = lax.broadcasted_iota(jnp.int32, (tc, 1), 0)
        rmod = rowi & (S - 1)
        bcum = la
        rsum = la
        for s in (1, 2, 4, 8):
            bcum = bcum + jnp.where(rmod >= s, pltpu.roll(bcum, s, 0), 0.0)
            rsum = rsum + jnp.where(rmod < S - s, pltpu.roll(rsum, tc - s, 0), 0.0)
        q = q_ref[0] * (dk ** -0.5)
        k = k_ref[0]
        v = v_ref[0]
        qt = q * jnp.exp(bcum)
        kt = k * jnp.exp(rsum - la)
        eg = jnp.exp(bcum + rsum - la)

        coli = lax.broadcasted_iota(jnp.int32, (tc, GB), 1)
        rgrp = rowi & (GB - 1)
        att0 = jnp.zeros((tc, GB), F32)
        att1 = jnp.zeros((tc, GB), F32)
        for d in range(S):
            kd = k if d == 0 else pltpu.roll(k, d, 0)
            bd = bcum if d == 0 else pltpu.roll(bcum, d, 0)
            valid = rmod >= d
            e = jnp.exp(jnp.where(valid, bcum - bd, 0.0))
            prod = q * kd * e
            w0 = jnp.where(valid, jnp.sum(prod[:, 0:dk], axis=1, keepdims=True), 0.0)
            w1 = jnp.where(valid, jnp.sum(prod[:, dk:2 * dk], axis=1, keepdims=True), 0.0)
            here = coli == rgrp - d
            att0 = jnp.where(here, w0, att0)
            att1 = jnp.where(here, w1, att1)

        heads = range(2)
        lk = [slice(hh * dk, (hh + 1) * dk) for hh in heads]
        lv = [slice(hh * dv, (hh + 1) * dv) for hh in heads]
        kv = [[_dot_tn(v[si * S:(si + 1) * S, lv[hh]], kt[si * S:(si + 1) * S, lk[hh]]) for hh in heads]
              for si in range(nsub)]
        st = [st_ref[hh] for hh in heads]
        inter = [[], []]
        for si in range(nsub):
            rows = slice(si * S, (si + 1) * S)
            for hh in heads:
                inter[hh].append(_dot_nt(qt[rows, lk[hh]], st[hh]))
                st[hh] = st[hh] * eg[si * S:si * S + 1, lk[hh]] + kv[si][hh]
        for hh in heads:
            st_ref[hh] = st[hh]

        for hh, att in ((0, att0), (1, att1)):
            diag = jnp.concatenate([_dot(att[g * GB:(g + 1) * GB], v[g * GB:(g + 1) * GB, lv[hh]])
                                    for g in range(tc // GB)], axis=0)
            o = diag + jnp.concatenate(inter[hh], axis=0)
            on = o * lax.rsqrt(jnp.mean(o * o, axis=1, keepdims=True) + EPS)
            out_ref[0, :, lv[hh]] = (on * ng_ref[:, lv[hh]] * _silu(gr_ref[0, :, lv[hh]])).astype(BF16)

    return pl.pallas_call(
        body,
        grid=(bsz, GLA_HEADS // 2, t // tc),
        in_specs=[pl.BlockSpec((1, tc, 2 * dk), lambda b, h, c: (b, c, 16 + h)),
                  pl.BlockSpec((1, tc, 2 * dk), lambda b, h, c: (b, c, 18 + h)),
                  pl.BlockSpec((1, tc, 2 * dv), lambda b, h, c: (b, c, 10 + h)),
                  pl.BlockSpec((1, tc, 2 * dv), lambda b, h, c: (b, c, 12 + h)),
                  pl.BlockSpec((1, tc, GLA_RANK), lambda b, h, c: (b, c, 0)),
                  pl.BlockSpec((GLA_RANK, 2 * dk), lambda b, h, c: (0, h)),
                  pl.BlockSpec((1, 2 * dk), lambda b, h, c: (0, h)),
                  pl.BlockSpec((1, 2 * dv), lambda b, h, c: (0, h))],
        out_specs=pl.BlockSpec((1, tc, 2 * dv), lambda b, h, c: (b, c, h)),
        out_shape=jax.ShapeDtypeStruct((bsz, t, GLA_HEADS * dv), BF16),
        scratch_shapes=[pltpu.VMEM((2, dv, dk), F32)],
        compiler_params=_cp("parallel", "parallel", "arbitrary"),
        name="gla",
    )(p0, p0, p0, p0, g_low, gate_up, gate_b, norm_g)


RET_L = 256


def _retention(p_ret, cos_t, sin_t, intra, inter, sdec, cdec, norm_g):
    bsz, t, _ = p_ret.shape
    L, D, H, W = RET_L, RET_DIM, RET_HEADS, RET_W

    def body(q_ref, k_ref, v_ref, g_ref, cos_ref, sin_ref, intra_ref, inter_ref, sdec_ref, cdec_ref, ng_ref,
             out_ref, s_ref):
        c = pl.program_id(1)

        @pl.when(c == 0)
        def _():
            s_ref[...] = jnp.zeros_like(s_ref)

        cs = cos_ref[...]
        sn = sin_ref[...]

        def rot(x):
            return x * cs + pltpu.roll(x, D // 2, 1) * sn

        heads = range(H)
        q = [rot(q_ref[0, :, h * D:(h + 1) * D]) * (D ** -0.5) for h in heads]
        k = [rot(k_ref[0, :, h * D:(h + 1) * D]) for h in heads]
        v = [v_ref[0, :, h * D:(h + 1) * D] for h in heads]
        s_st = [s_ref[h] for h in heads]
        s = [_dot_nt(q[h], k[h]) * intra_ref[h] for h in heads]
        qs = [_dot(q[h], s_st[h]) for h in heads]
        kv = [_dot_tn(k[h] * sdec_ref[h], v[h]) for h in heads]
        sv = [_dot(s[h], v[h]) for h in heads]
        for h in heads:
            o = sv[h] + inter_ref[h] * qs[h]
            s_ref[h] = cdec_ref[h] * s_st[h] + kv[h]
            oc = o - jnp.mean(o, axis=1, keepdims=True)
            on = oc * lax.rsqrt(jnp.mean(oc * oc, axis=1, keepdims=True) + EPS)
            hl = slice(h * D, (h + 1) * D)
            out_ref[0, :, hl] = (on * ng_ref[:, hl] * _silu(g_ref[0, :, hl])).astype(BF16)

    col = lambda j: (lambda b, c: (b, c, j))
    fix3 = lambda b, c: (0, 0, 0)
    return pl.pallas_call(
        body,
        grid=(bsz, t // L),
        in_specs=[pl.BlockSpec((1, L, W), col(0)),
                  pl.BlockSpec((1, L, W), col(1)),
                  pl.BlockSpec((1, L, W), col(2)),
                  pl.BlockSpec((1, L, W), col(3)),
                  pl.BlockSpec((L, D), lambda b, c: (c, 0)),
                  pl.BlockSpec((L, D), lambda b, c: (c, 0)),
                  pl.BlockSpec((H, L, L), fix3),
                  pl.BlockSpec((H, L, 1), fix3),
                  pl.BlockSpec((H, L, 1), fix3),
                  pl.BlockSpec((H, 1, 1), fix3),
                  pl.BlockSpec((1, W), lambda b, c: (0, 0))],
        out_specs=pl.BlockSpec((1, L, W), lambda b, c: (b, c, 0)),
        out_shape=jax.ShapeDtypeStruct((bsz, t, W), BF16),
        scratch_shapes=[pltpu.VMEM((H, D, D), F32)],
        compiler_params=_cp("parallel", "arbitrary"),
        name="retention",
    )(p_ret, p_ret, p_ret, p_ret, cos_t, sin_t, intra, inter, sdec, cdec, norm_g)


def _retention_tables(t):
    L, D = RET_L, RET_DIM
    inv = ROPE_BASE ** (-jnp.arange(0, D, 2, dtype=F32) / D)
    ang = jnp.arange(t).astype(F32)[:, None] * inv[None, :]
    cos = jnp.cos(ang)
    sin = jnp.sin(ang)
    cos_t = jnp.concatenate([cos, cos], axis=-1)
    sin_t = jnp.concatenate([-sin, sin], axis=-1)
    log_gamma = jnp.log1p(-jnp.exp2(-5.0 - jnp.arange(RET_HEADS, dtype=F32)))
    idx = jnp.arange(L, dtype=F32)
    causal = idx[:, None] >= idx[None, :]
    rel = jnp.where(causal, idx[:, None] - idx[None, :], 0.0)
    intra = jnp.where(causal, jnp.exp(log_gamma[:, None, None] * rel), 0.0)
    inter = jnp.exp(log_gamma[:, None] * (idx + 1.0))[:, :, None]
    sdec = jnp.exp(log_gamma[:, None] * (L - 1.0 - idx))[:, :, None]
    cdec = jnp.exp(log_gamma * L)[:, None, None]
    return cos_t, sin_t, intra, inter, sdec, cdec


RWP_TM = 256


def _seg_sum(x, bd):
    hi = x.astype(BF16)
    lo = (x - hi.astype(F32)).astype(BF16)
    return jnp.dot(hi, bd, preferred_element_type=F32) + jnp.dot(lo, bd, preferred_element_type=F32)


def _rwkv_prep(p1, mu, w_up, w0, a_up, a0, g_up, k_k, k_a, r_k, bd):
    bsz, t, _ = p1.shape
    tm, W = RWP_TM, RW_W

    def body(x_ref, xp_ref, mu_ref, wup_ref, w0_ref, aup_ref, a0_ref, gup_ref, kk_ref, ka_ref, rk_ref, bd_ref,
             r_out, lw_out, k_out, v_out, a_out, b_out, g_out, bonus_out):
        c = pl.program_id(1)
        cur = x_ref[0]
        prev = jnp.where(c == 0, 0.0, xp_ref[0, 7:8, :])
        rowi = lax.broadcasted_iota(jnp.int32, (tm, 1), 0)
        sh = jnp.where(rowi == 0, prev, pltpu.roll(cur, 1, 0))
        xm = cur + (sh - cur) * mu_ref[...]
        x_r = xm[:, 0:W]
        x_k = xm[:, W:2 * W]
        x_v = xm[:, 2 * W:3 * W]
        x_dl = xm[:, 3 * W:3 * W + 64]
        x_al = xm[:, 3 * W + 64:3 * W + 128]
        x_gl = xm[:, 3 * W + 128:3 * W + 256]
        wl = w0_ref[...] + _dot(jnp.tanh(x_dl), wup_ref[...])
        sp = jnp.maximum(-wl, 0.0) + jnp.log(1.0 + jnp.exp(-jnp.abs(wl)))
        lw_out[0] = -jnp.exp(-sp - 0.5)
        a = _sigmoid(a0_ref[...] + _dot(x_al, aup_ref[...]))
        g_out[0] = _dot(_sigmoid(x_gl), gup_ref[...])
        kk0 = x_k * kk_ref[...]
        nrm = jnp.sqrt(_seg_sum(kk0 * kk0, bd_ref[...]))
        kk = kk0 / jnp.maximum(nrm, 1e-12)
        k_h = x_k * (1.0 + (a - 1.0) * ka_ref[...])
        r_out[0] = x_r
        k_out[0] = k_h
        v_out[0] = x_v
        a_out[0] = -kk
        b_out[0] = kk * a
        bonus_out[0] = _seg_sum(x_r * k_h * rk_ref[...], bd_ref[...]) * x_v

    row = lambda b, c: (0, 0)
    blk = pl.BlockSpec((1, tm, W), lambda b, c: (b, c, 0))
    shp = jax.ShapeDtypeStruct((bsz, t, W), F32)
    return pl.pallas_call(
        body,
        grid=(bsz, t // tm),
        in_specs=[pl.BlockSpec((1, tm, RW_COLS), lambda b, c: (b, c, 0)),
                  pl.BlockSpec((1, 8, RW_COLS), lambda b, c: (b, jnp.maximum(c * (tm // 8) - 1, 0), 0)),
                  pl.BlockSpec((1, RW_COLS), row),
                  pl.BlockSpec((64, W), row), pl.BlockSpec((1, W), row),
                  pl.BlockSpec((64, W), row), pl.BlockSpec((1, W), row),
                  pl.BlockSpec((128, W), row),
                  pl.BlockSpec((1, W), row), pl.BlockSpec((1, W), row), pl.BlockSpec((1, W), row),
                  pl.BlockSpec((W, W), row)],
        out_specs=[blk] * 8,
        out_shape=[shp] * 8,
        compiler_params=_cp("parallel", "arbitrary"),
        name="rwkv_prep",
    )(p1, p1, mu, w_up, w0, a_up, a0, g_up, k_k, k_a, r_k, bd)


RW_L = 64
RW_TB = 128


def _rwkv_scan(r, lw, k, v, aa, bb, g_out, bonus, ln_g, ln_b):
    bsz, t, W = r.shape
    L, N, tb = RW_L, RW_DIM, RW_TB
    nck = tb // L

    def body(r_ref, lw_ref, k_ref, v_ref, a_ref, b_ref, g_ref, bonus_ref, lng_ref, lnb_ref, out_ref, h_ref):
        c = pl.program_id(1)

        @pl.when(c == 0)
        def _():
            h_ref[...] = jnp.zeros_like(h_ref)

        lw_all = lw_ref[0]
        rowi = lax.broadcasted_iota(jnp.int32, (tb, 1), 0) & (L - 1)
        cl = lw_all
        for s in (1, 2, 4, 8, 16, 32):
            cl = cl + jnp.where(rowi >= s, pltpu.roll(cl, s, 0), 0.0)
        cl_last = jnp.concatenate([jnp.broadcast_to(cl[(cc + 1) * L - 1:(cc + 1) * L, :], (L, W))
                                   for cc in range(nck)], axis=0)
        e_inv = jnp.exp(-cl)
        e_end = jnp.exp(cl_last - cl)
        p_end = jnp.exp(cl_last)
        at = a_ref[0] * jnp.exp(cl - lw_all)
        rt = r_ref[0] * jnp.exp(cl)
        bt = b_ref[0] * e_inv
        kt = k_ref[0] * e_inv
        b_end = b_ref[0] * e_end
        k_end = k_ref[0] * e_end
        v_all = v_ref[0]
        ri = lax.broadcasted_iota(jnp.int32, (L, L), 0)
        ci = lax.broadcasted_iota(jnp.int32, (L, L), 1)
        strict = ri > ci
        lower = ri >= ci
        eye = ri == ci
        heads = range(nck * RW_HEADS)

        def hs(z, u):
            cc, hh = divmod(u, RW_HEADS)
            return z[cc * L:(cc + 1) * L, hh * N:(hh + 1) * N]

        m = [_dot_nt(jnp.concatenate([hs(at, h), hs(rt, h)], axis=0),
                     jnp.concatenate([hs(bt, h), hs(kt, h)], axis=0)) for h in heads]
        a_ab = [jnp.where(strict, m[h][0:L, 0:L], 0.0) for h in heads]
        a_ak = [jnp.where(strict, m[h][0:L, L:2 * L], 0.0) for h in heads]
        a_rb = [jnp.where(lower, m[h][L:2 * L, 0:L], 0.0) for h in heads]
        a_rk = [jnp.where(lower, m[h][L:2 * L, L:2 * L], 0.0) for h in heads]
        end_t = [jnp.concatenate([hs(b_end, h), hs(k_end, h)], axis=1).T for h in heads]
        vp = [_dot(jnp.concatenate([a_ak[h], a_rk[h], end_t[h][N:2 * N]], axis=0), hs(v_all, h)) for h in heads]
        x = [jnp.concatenate([hs(at, h), vp[h][0:L]], axis=1) for h in heads]
        ap = a_ab
        for it in range(6):
            x = [x[h] + _dot(ap[h], x[h]) for h in heads]
            if it < 5:
                ap = [_dot(ap[h], ap[h]) for h in heads]
        post = [_dot(jnp.concatenate([a_rb[h], end_t[h][0:N]], axis=0), x[h]) for h in heads]
        lhs = []
        y0s = []
        h_adds = []
        for u in heads:
            q_hat = hs(rt, u) + post[u][0:L, 0:N]
            gmat = jnp.where(eye, hs(p_end, u), 0.0) + post[u][L:L + N, 0:N]
            lhs.append(jnp.concatenate([q_hat, gmat], axis=0))
            y0s.append(post[u][0:L, N:2 * N] + vp[u][L:2 * L])
            h_adds.append(post[u][L:L + N, N:2 * N] + vp[u][2 * L:2 * L + N])
        h_st = [h_ref[hh] for hh in range(RW_HEADS)]
        for cc in range(nck):
            outs = []
            for hh in range(RW_HEADS):
                u = cc * RW_HEADS + hh
                res = _dot(lhs[u], h_st[hh])
                y = res[0:L] + y0s[u]
                h_st[hh] = res[L:L + N] + h_adds[u]
                yc = y - jnp.mean(y, axis=1, keepdims=True)
                outs.append(yc * lax.rsqrt(jnp.mean(yc * yc, axis=1, keepdims=True) + RW_LN_EPS))
            yn = jnp.concatenate(outs, axis=1)
            rows = slice(cc * L, (cc + 1) * L)
            out_ref[0, rows, :] = ((yn * lng_ref[...] + lnb_ref[...] + bonus_ref[0, rows, :])
                                   * g_ref[0, rows, :]).astype(BF16)
        for hh in range(RW_HEADS):
            h_ref[hh] = h_st[hh]

    blk = pl.BlockSpec((1, tb, W), lambda b, c: (b, c, 0))
    vec = pl.BlockSpec((1, W), lambda b, c: (0, 0))
    return pl.pallas_call(
        body,
        grid=(bsz, t // tb),
        in_specs=[blk] * 8 + [vec, vec],
        out_specs=blk,
        out_shape=jax.ShapeDtypeStruct((bsz, t, W), BF16),
        scratch_shapes=[pltpu.VMEM((RW_HEADS, N, N), F32)],
        compiler_params=_cp("parallel", "arbitrary"),
        name="rwkv_scan",
    )(r, lw, k, v, aa, bb, g_out, bonus, ln_g, ln_b)


def _router(x2d, g, router_pad, tm=512):
    n, d = x2d.shape
    e = router_pad.shape[1]

    def body(x_ref, g_ref, r_ref, xn_ref, lg_ref):
        xn = _rms_rows(x_ref[...], g_ref[...])
        xn_ref[...] = xn.astype(BF16)
        lg_ref[...] = jnp.dot(xn, r_ref[...], preferred_element_type=F32, precision=lax.Precision.HIGHEST)

    return pl.pallas_call(
        body,
        grid=(n // tm,),
        in_specs=[pl.BlockSpec((tm, d), lambda i: (i, 0)),
                  pl.BlockSpec((1, d), lambda i: (0, 0)),
                  pl.BlockSpec((d, e), lambda i: (0, 0))],
        out_specs=[pl.BlockSpec((tm, d), lambda i: (i, 0)), pl.BlockSpec((tm, e), lambda i: (i, 0))],
        out_shape=[jax.ShapeDtypeStruct((n, d), BF16), jax.ShapeDtypeStruct((n, e), F32)],
        compiler_params=_cp("parallel"),
        name="moe_router",
    )(x2d, g.reshape(1, d), router_pad)


MOE_TM = 512
MOE_TF = 1792


def _experts(xs, row_w, item_tile, item_exp, item_lo, item_hi, wg, wu, wd):
    nrows, d = xs.shape
    tm, tf = MOE_TM, MOE_TF
    nf = D_FF // tf
    n_items = item_tile.shape[0]

    def body(it_ref, ie_ref, lo_ref, hi_ref, x_ref, w_ref, wg_ref, wu_ref, wd_ref, o_ref):
        i = pl.program_id(0)
        j = pl.program_id(1)
        first = jnp.logical_or(i == 0, it_ref[i] != it_ref[jnp.maximum(i - 1, 0)])

        @pl.when(jnp.logical_and(first, j == 0))
        def _():
            o_ref[...] = jnp.zeros_like(o_ref)

        lo = lo_ref[i]
        hi = hi_ref[i]

        @pl.when(lo < hi)
        def _():
            x = x_ref[...]
            gg = jnp.dot(x, wg_ref[0], preferred_element_type=F32)
            uu = jnp.dot(x, wu_ref[0], preferred_element_type=F32)
            act = (_silu(gg) * uu).astype(BF16)
            part = jnp.dot(act, wd_ref[0], preferred_element_type=F32)
            rowi = lax.broadcasted_iota(jnp.int32, (tm, 1), 0)
            mine = jnp.logical_and(rowi >= lo, rowi < hi)
            o_ref[...] += part * jnp.where(mine, w_ref[...], 0.0)

    grid_spec = pltpu.PrefetchScalarGridSpec(
        num_scalar_prefetch=4,
        grid=(n_items, nf),
        in_specs=[pl.BlockSpec((tm, d), lambda i, j, it, ie, lo, hi: (it[i], 0)),
                  pl.BlockSpec((tm, 1), lambda i, j, it, ie, lo, hi: (it[i], 0)),
                  pl.BlockSpec((1, d, tf), lambda i, j, it, ie, lo, hi: (ie[i], 0, j)),
                  pl.BlockSpec((1, d, tf), lambda i, j, it, ie, lo, hi: (ie[i], 0, j)),
                  pl.BlockSpec((1, tf, d), lambda i, j, it, ie, lo, hi: (ie[i], j, 0))],
        out_specs=pl.BlockSpec((tm, d), lambda i, j, it, ie, lo, hi: (it[i], 0)),
    )
    return pl.pallas_call(
        body,
        grid_spec=grid_spec,
        out_shape=jax.ShapeDtypeStruct((nrows, d), F32),
        compiler_params=_cp("arbitrary", "arbitrary"),
        name="moe_experts",
    )(item_tile, item_exp, item_lo, item_hi, xs, row_w, wg, wu, wd)


def _combine_norm(h, y0, y1, g, tm=512):
    n, d = h.shape

    def body(h_ref, a_ref, b_ref, g_ref, o_ref):
        o_ref[...] = _rms_rows(h_ref[...] + (a_ref[...] + b_ref[...]), g_ref[...])

    blk = pl.BlockSpec((tm, d), lambda i: (i, 0))
    return pl.pallas_call(
        body,
        grid=(n // tm,),
        in_specs=[blk, blk, blk, pl.BlockSpec((1, d), lambda i: (0, 0))],
        out_specs=blk,
        out_shape=jax.ShapeDtypeStruct((n, d), F32),
        compiler_params=_cp("parallel"),
        name="combine_norm",
    )(h, y0, y1, g.reshape(1, d))


def _route(logits, n):
    tm = MOE_TM
    na = n * TOP_K
    n_tiles = na // tm
    top_val, top_idx = lax.top_k(logits, TOP_K)
    top_w = jax.nn.softmax(top_val, axis=-1)
    e_flat = top_idx.reshape(-1).astype(jnp.int32)
    w_flat = top_w.reshape(-1)
    tok = jnp.arange(na, dtype=jnp.int32) // TOP_K
    _, sorted_tok, sorted_w = lax.sort((e_flat, tok, w_flat), num_keys=1, is_stable=True)
    onehot = (e_flat[:, None] == jnp.arange(N_EXPERTS, dtype=jnp.int32)[None, :]).astype(jnp.int32)
    rank = jnp.take_along_axis(jnp.cumsum(onehot, axis=0), e_flat[:, None], axis=1)[:, 0] - 1
    counts = jnp.sum(onehot, axis=0)
    ends = jnp.cumsum(counts)
    pos = (ends - counts)[e_flat] + rank
    cuts = jnp.sort(jnp.concatenate([jnp.arange(n_tiles, dtype=jnp.int32) * tm, ends[:-1].astype(jnp.int32)]))
    nxt = jnp.concatenate([cuts[1:], jnp.full((1,), na, jnp.int32)])
    item_tile = jnp.minimum(cuts // tm, n_tiles - 1)
    item_exp = jnp.minimum(jnp.searchsorted(ends, cuts, side="right"), N_EXPERTS - 1).astype(jnp.int32)
    item_lo = cuts - item_tile * tm
    item_hi = nxt - item_tile * tm
    return sorted_tok, sorted_w, (item_tile, item_exp, item_lo, item_hi), pos.reshape(n, TOP_K)


def kernel(x, e_norm1_g, e_w_in, e_ml_conv_w, e_ml_conv_b, e_ml_gate_b, e_ml_norm_g, e_gla_gate_up, e_gla_gate_b,
           e_gla_norm_g, e_w_out, e_norm2_g, e_ffn_w_gate, e_ffn_w_up, e_ffn_w_down, o_norm1_g, o_w_in,
           o_ret_norm_g, o_rw_mu, o_rw_w_up, o_rw_w0, o_rw_a_up, o_rw_a0, o_rw_g_up, o_rw_k_k, o_rw_k_a, o_rw_r_k,
           o_rw_ln_g, o_rw_ln_b, o_w_out, o_norm2_g, o_moe_router, o_moe_w_gate, o_moe_w_up, o_moe_w_down,
           final_norm_g):
    bsz, t, d = x.shape
    n = bsz * t
    h0 = x.reshape(n, d)

    w = e_w_in[0]
    w0 = jnp.concatenate([w[:, :2048], w[:, 2056:3592], w[:, 2048:2056], w[:, 3592:3608],
                          jnp.zeros((d, E_COLS_PAD - 3608), F32)], axis=1).astype(BF16)
    p0 = _norm_matmul(h0, e_norm1_g[0], w0).reshape(bsz, t, E_COLS_PAD)
    gates_col = p0[:, :, 3584:3592]
    gates_row = jnp.transpose(gates_col, (0, 2, 1))
    h_ml = _mlstm(p0, gates_col, gates_row, e_ml_gate_b[0].reshape(1, -1), e_ml_gate_b[0].reshape(-1, 1),
                  e_ml_conv_w[0], e_ml_conv_b[0].reshape(1, -1), e_ml_norm_g[0].reshape(1, -1))
    g_low = p0[:, :, 3592:3608]
    o_gla = _gla(p0, g_low, e_gla_gate_up[0].astype(BF16), e_gla_gate_b[0].reshape(1, -1),
                 e_gla_norm_g[0].reshape(1, -1))
    h1 = _mix_out(h_ml.reshape(n, -1), o_gla.reshape(n, -1), e_w_out[0].astype(BF16), h0)
    h2 = _ffn(h1, e_norm2_g[0], e_ffn_w_gate[0].astype(BF16), e_ffn_w_up[0].astype(BF16),
              e_ffn_w_down[0].astype(BF16))

    w = o_w_in[0]
    p_ret = _norm_matmul(h2, o_norm1_g[0], w[:, :4 * RET_W].astype(BF16), tn=1024).reshape(bsz, t, -1)
    p1 = _norm_matmul(h2, o_norm1_g[0], w[:, 4 * RET_W:].astype(BF16), tn=896).reshape(bsz, t, -1)
    y_ret = _retention(p_ret, *_retention_tables(t), o_ret_norm_g[0].reshape(1, -1))
    head_of = jnp.arange(RW_W) // RW_DIM
    bd = (head_of[:, None] == head_of[None, :]).astype(BF16)
    row = lambda a: a.reshape(1, -1)
    r, lw, k, v, aa, bb, g_out, bonus = _rwkv_prep(
        p1, row(o_rw_mu[0]), o_rw_w_up[0].astype(BF16), row(o_rw_w0[0]), o_rw_a_up[0].astype(BF16),
        row(o_rw_a0[0]), o_rw_g_up[0].astype(BF16), row(o_rw_k_k[0]), row(o_rw_k_a[0]), row(o_rw_r_k[0]), bd)
    y_rw = _rwkv_scan(r, lw, k, v, aa, bb, g_out, bonus, row(o_rw_ln_g[0]), row(o_rw_ln_b[0]))
    h3 = _mix_out(y_ret.reshape(n, -1), y_rw.reshape(n, -1), o_w_out[0].astype(BF16), h2)

    router_pad = jnp.zeros((d, 128), F32).at[:, :N_EXPERTS].set(o_moe_router[0])
    xn, logits = _router(h3, o_norm2_g[0], router_pad)
    sorted_tok, sorted_w, items, pos = _route(logits[:, :N_EXPERTS], n)
    xs = xn.at[sorted_tok].get(mode="promise_in_bounds")
    ys = _experts(xs, sorted_w.reshape(-1, 1), *items, o_moe_w_gate[0].astype(BF16),
                  o_moe_w_up[0].astype(BF16), o_moe_w_down[0].astype(BF16))
    y0 = ys.at[pos[:, 0]].get(mode="promise_in_bounds")
    y1 = ys.at[pos[:, 1]].get(mode="promise_in_bounds")
    out = _combine_norm(h3, y0, y1, final_norm_g)
    return out.reshape(bsz, t, d)
```

```python
import functools

import numpy as np
import jax
import jax.numpy as jnp
from jax import lax
from jax.experimental import pallas as pl
from jax.experimental.pallas import tpu as pltpu

F32 = jnp.float32
BF16 = jnp.bfloat16

D_MODEL = 1024
EPS = 1e-6
ML_HEADS, ML_DIM, ML_W, ML_CONV = 4, 128, 512, 4
GLA_HEADS, GLA_DK, GLA_DV, GLA_RANK, GLA_TAU = 4, 64, 128, 16, 16.0
E_COLS_PAD = 3840
RET_HEADS, RET_DIM, RET_W = 4, 128, 512
ROPE_BASE = 10000.0
RW_HEADS, RW_DIM, RW_W = 8, 64, 512
RW_COLS = 1792
RW_LN_EPS = 64e-5
D_FF = 3584
N_EXPERTS = 8
TOP_K = 2

VMEM_LIMIT = 48 * 1024 * 1024
NEG = -1e30


def _cp(*sem):
    return pltpu.CompilerParams(dimension_semantics=sem, vmem_limit_bytes=VMEM_LIMIT)


def _sigmoid(x):
    return 1.0 / (1.0 + jnp.exp(-x))


def _silu(x):
    return x * _sigmoid(x)


def _log_sigmoid(x):
    return jnp.minimum(x, 0.0) - jnp.log(1.0 + jnp.exp(-jnp.abs(x)))


def _dot(a, b):
    return jnp.dot(a.astype(BF16), b.astype(BF16), preferred_element_type=F32)


def _dot_nt(a, b):
    return lax.dot_general(a.astype(BF16), b.astype(BF16), (((1,), (1,)), ((), ())), preferred_element_type=F32)


def _dot_tn(a, b):
    return jnp.dot(a.T.astype(BF16), b.astype(BF16), preferred_element_type=F32)


def _rms_rows(x, g):
    ms = jnp.mean(x * x, axis=-1, keepdims=True)
    return x * lax.rsqrt(ms + EPS) * g


def _norm_matmul(x2d, g, w_bf16, tm=1024, tn=1280):
    n, d = x2d.shape
    c = w_bf16.shape[1]

    def body(x_ref, g_ref, w_ref, o_ref, xn_ref):
        @pl.when(pl.program_id(1) == 0)
        def _():
            xn_ref[...] = _rms_rows(x_ref[...], g_ref[...]).astype(BF16)

        o_ref[...] = jnp.dot(xn_ref[...], w_ref[...], preferred_element_type=F32)

    return pl.pallas_call(
        body,
        grid=(n // tm, c // tn),
        in_specs=[pl.BlockSpec((tm, d), lambda i, j: (i, 0)),
                  pl.BlockSpec((1, d), lambda i, j: (0, 0)),
                  pl.BlockSpec((d, tn), lambda i, j: (0, j))],
        out_specs=pl.BlockSpec((tm, tn), lambda i, j: (i, j)),
        out_shape=jax.ShapeDtypeStruct((n, c), F32),
        scratch_shapes=[pltpu.VMEM((tm, d), BF16)],
        compiler_params=_cp("parallel", "arbitrary"),
        name="norm_matmul",
    )(x2d, g.reshape(1, d), w_bf16)


def _mix_out(a, b, w_bf16, resid, tm=1024):
    n, wa = a.shape
    wb = b.shape[1]
    d = w_bf16.shape[1]

    def body(a_ref, b_ref, w_ref, r_ref, o_ref):
        acc = jnp.dot(a_ref[...], w_ref[0:wa, :], preferred_element_type=F32)
        acc = acc + jnp.dot(b_ref[...], w_ref[wa:wa + wb, :], preferred_element_type=F32)
        o_ref[...] = r_ref[...] + acc

    return pl.pallas_call(
        body,
        grid=(n // tm,),
        in_specs=[pl.BlockSpec((tm, wa), lambda i: (i, 0)),
                  pl.BlockSpec((tm, wb), lambda i: (i, 0)),
                  pl.BlockSpec((wa + wb, d), lambda i: (0, 0)),
                  pl.BlockSpec((tm, d), lambda i: (i, 0))],
        out_specs=pl.BlockSpec((tm, d), lambda i: (i, 0)),
        out_shape=jax.ShapeDtypeStruct((n, d), F32),
        compiler_params=_cp("parallel"),
        name="mix_out",
    )(a, b, w_bf16, resid)


def _ffn(x2d, g, wg, wu, wd, tm=512, tf=1792):
    n, d = x2d.shape
    f = wg.shape[1]
    nf = f // tf

    def body(x_ref, g_ref, wg_ref, wu_ref, wd_ref, o_ref, xn_ref, acc_ref):
        j = pl.program_id(1)

        @pl.when(j == 0)
        def _():
            xn_ref[...] = _rms_rows(x_ref[...], g_ref[...]).astype(BF16)
            acc_ref[...] = jnp.zeros_like(acc_ref)

        xn = xn_ref[...]
        gg = jnp.dot(xn, wg_ref[...], preferred_element_type=F32)
        uu = jnp.dot(xn, wu_ref[...], preferred_element_type=F32)
        act = (_silu(gg) * uu).astype(BF16)
        acc_ref[...] += jnp.dot(act, wd_ref[...], preferred_element_type=F32)

        @pl.when(j == nf - 1)
        def _():
            o_ref[...] = x_ref[...] + acc_ref[...]

    return pl.pallas_call(
        body,
        grid=(n // tm, nf),
        in_specs=[pl.BlockSpec((tm, d), lambda i, j: (i, 0)),
                  pl.BlockSpec((1, d), lambda i, j: (0, 0)),
                  pl.BlockSpec((d, tf), lambda i, j: (0, j)),
                  pl.BlockSpec((d, tf), lambda i, j: (0, j)),
                  pl.BlockSpec((tf, d), lambda i, j: (j, 0))],
        out_specs=pl.BlockSpec((tm, d), lambda i, j: (i, 0)),
        out_shape=jax.ShapeDtypeStruct((n, d), F32),
        scratch_shapes=[pltpu.VMEM((tm, d), BF16), pltpu.VMEM((tm, d), F32)],
        compiler_params=_cp("parallel", "arbitrary"),
        name="ffn",
    )(x2d, g.reshape(1, d), wg, wu, wd)


ML_L = 256


def _mlstm(p0, gates_col, gates_row, gate_b_row, gate_b_col, conv_w, conv_b, norm_g):
    bsz, t, _ = p0.shape
    L, D, H, W = ML_L, ML_DIM, ML_HEADS, ML_W

    def body(q_ref, k_ref, v_ref, og_ref, gc_ref, gr_ref, gbr_ref, gbc_ref, cwq_ref, cwk_ref, cbq_ref, cbk_ref,
             ng_ref, out_ref, qext, kext, c_ref, n_ref, m_ref):
        c = pl.program_id(1)

        @pl.when(c == 0)
        def _():
            qext[0:8, :] = jnp.zeros((8, W), F32)
            kext[0:8, :] = jnp.zeros((8, W), F32)
            c_ref[...] = jnp.zeros_like(c_ref)
            n_ref[...] = jnp.zeros_like(n_ref)
            m_ref[...] = jnp.zeros_like(m_ref)

        qext[8:, :] = q_ref[0]
        kext[8:, :] = k_ref[0]

        def conv(ext, cw_ref, cb_ref):
            acc = cb_ref[...] + cw_ref[0:1, :] * ext[pl.ds(8 - ML_CONV + 1, L), :]
            for kk in range(1, ML_CONV):
                acc = acc + cw_ref[kk:kk + 1, :] * ext[pl.ds(8 - ML_CONV + 1 + kk, L), :]
            return _silu(acc)

        q_all = conv(qext, cwq_ref, cbq_ref) * (D ** -0.5)
        k_all = conv(kext, cwk_ref, cbk_ref)
        qext[0:8, :] = qext[L:L + 8, :]
        kext[0:8, :] = kext[L:L + 8, :]

        gcol = gc_ref[0] + gbr_ref[...]
        grow = gr_ref[0] + gbc_ref[...]
        fcol = _log_sigmoid(gcol[:, H:2 * H])
        frow = _log_sigmoid(grow[H:2 * H, :])
        ri = lax.broadcasted_iota(jnp.int32, (L, L), 0)
        ci = lax.broadcasted_iota(jnp.int32, (L, L), 1)
        causal = ri >= ci
        heads = range(H)
        hs = lambda z, h: z[:, h * D:(h + 1) * D]
        w_intra, w_inter, w_state, carry, m_row, m_new = [], [], [], [], [], []
        for h in heads:
            f_row = frow[h:h + 1, :]
            i_row = grow[h:h + 1, :]
            f_col = fcol[:, h:h + 1]
            i_col = gcol[:, h:h + 1]
            m_st = m_ref[h]
            b_col = jnp.sum(jnp.where(causal, f_row, 0.0), axis=1, keepdims=True)
            b_row = jnp.sum(jnp.where(ri <= ci, f_col, 0.0), axis=0, keepdims=True)
            g_tot = jnp.sum(f_row, axis=1, keepdims=True)
            d_intra = jnp.where(causal, b_col - b_row + i_row, NEG)
            d_inter = b_col + m_st
            mr = jnp.maximum(d_inter, jnp.max(d_intra, axis=1, keepdims=True))
            w_intra.append(jnp.exp(d_intra - mr))
            w_inter.append(jnp.exp(d_inter - mr))
            m_row.append(mr)
            d_state = g_tot - b_col + i_col
            mn = jnp.maximum(g_tot + m_st, jnp.max(d_state, axis=0, keepdims=True))
            w_state.append(jnp.exp(d_state - mn))
            carry.append(jnp.exp(g_tot + m_st - mn))
            m_new.append(mn)
        qh = [hs(q_all, h) for h in heads]
        kh = [hs(k_all, h) for h in heads]
        vh = [v_ref[0, :, h * D:(h + 1) * D] for h in heads]
        c_st = [c_ref[h] for h in heads]
        n_st = [n_ref[h] for h in heads]
        s = [_dot_nt(qh[h], kh[h]) * w_intra[h] for h in heads]
        qc = [_dot(qh[h], c_st[h]) for h in heads]
        kw = [kh[h] * w_state[h] for h in heads]
        kv = [_dot_tn(kw[h], vh[h]) for h in heads]
        sv = [_dot(s[h], vh[h]) for h in heads]
        for h in heads:
            num = sv[h] + w_inter[h] * qc[h]
            den = jnp.sum(s[h], axis=1, keepdims=True) + w_inter[h] * jnp.sum(qh[h] * n_st[h], axis=1, keepdims=True)
            hval = num / jnp.maximum(jnp.abs(den), jnp.exp(-m_row[h]))
            c_ref[h] = carry[h] * c_st[h] + kv[h]
            n_ref[h] = carry[h] * n_st[h] + jnp.sum(kw[h], axis=0, keepdims=True)
            m_ref[h] = m_new[h]
            hg = _sigmoid(og_ref[0, :, h * D:(h + 1) * D]) * hval
            hc = hg - jnp.mean(hg, axis=1, keepdims=True)
            hn = hc * lax.rsqrt(jnp.mean(hc * hc, axis=1, keepdims=True) + EPS)
            out_ref[0, :, h * D:(h + 1) * D] = (hn * ng_ref[:, h * D:(h + 1) * D]).astype(BF16)

    col = lambda j: (lambda b, c: (b, c, j))
    fix = lambda j: (lambda b, c: (0, j))
    return pl.pallas_call(
        body,
        grid=(bsz, t // L),
        in_specs=[pl.BlockSpec((1, L, W), col(0)),
                  pl.BlockSpec((1, L, W), col(1)),
                  pl.BlockSpec((1, L, W), col(2)),
                  pl.BlockSpec((1, L, W), col(3)),
                  pl.BlockSpec((1, L, 2 * H), lambda b, c: (b, c, 0)),
                  pl.BlockSpec((1, 2 * H, L), lambda b, c: (b, 0, c)),
                  pl.BlockSpec((1, 2 * H), fix(0)),
                  pl.BlockSpec((2 * H, 1), fix(0)),
                  pl.BlockSpec((ML_CONV, W), fix(0)),
                  pl.BlockSpec((ML_CONV, W), fix(1)),
                  pl.BlockSpec((1, W), fix(0)),
                  pl.BlockSpec((1, W), fix(1)),
                  pl.BlockSpec((1, W), fix(0))],
        out_specs=pl.BlockSpec((1, L, W), lambda b, c: (b, c, 0)),
        out_shape=jax.ShapeDtypeStruct((bsz, t, W), BF16),
        scratch_shapes=[pltpu.VMEM((L + 8, W), F32), pltpu.VMEM((L + 8, W), F32),
                        pltpu.VMEM((H, D, D), F32), pltpu.VMEM((H, 1, D), F32), pltpu.VMEM((H, 1, 1), F32)],
        compiler_params=_cp("parallel", "arbitrary"),
        name="mlstm",
    )(p0, p0, p0, p0, gates_col, gates_row, gate_b_row, gate_b_col, conv_w, conv_w, conv_b, conv_b, norm_g)


GLA_TC = 256
GLA_SUB = 16
GLA_GROUP = 128


def _gla(p0, g_low, gate_up, gate_b, norm_g):
    bsz, t, _ = p0.shape
    tc, S, GB = GLA_TC, GLA_SUB, GLA_GROUP
    head_ones = (jnp.arange(2 * GLA_DK)[:, None] // GLA_DK == jnp.arange(2 * GB)[None, :] // GB).astype(BF16)
    nsub = tc // S
    dk, dv = GLA_DK, GLA_DV

    def body(q_ref, k_ref, v_ref, gr_ref, gl_ref, gu_ref, gbias_ref, ng_ref, ones_ref, out_ref, st_ref, ksh, bsh):
        c = pl.program_id(2)

        @pl.when(c == 0)
        def _():
            st_ref[...] = jnp.zeros_like(st_ref)
            ksh[0:S, :] = jnp.zeros((S, 2 * dk), F32)
            bsh[0:S, :] = jnp.zeros((S, 2 * dk), F32)

        z = _dot(gl_ref[0], gu_ref[...]) + gbias_ref[...]
        la = _log_sigmoid(z) / GLA_TAU
        rowi = lax.broadcasted_iota(jnp.int32, (tc, 1), 0)
        rmod = rowi & (S - 1)
        bcum = la
        rsum = la
        for s in (1, 2, 4, 8):
            bcum = bcum + jnp.where(rmod >= s, pltpu.roll(bcum, s, 0), 0.0)
            rsum = rsum + jnp.where(rmod < S - s, pltpu.roll(rsum, tc - s, 0), 0.0)
        q = q_ref[0] * (dk ** -0.5)
        k = k_ref[0]
        v = v_ref[0]
        qt = q * jnp.exp(bcum)
        kt = k * jnp.exp(rsum - la)
        eg = jnp.exp(bcum + rsum - la)

        ksh[S:, :] = k
        bsh[S:, :] = bcum
        prods = []
        for d in range(S):
            kd = k if d == 0 else ksh[pl.ds(S - d, tc), :]
            bd = bcum if d == 0 else bsh[pl.ds(S - d, tc), :]
            e = jnp.exp(jnp.where(rmod >= d, bcum - bd, 0.0))
            prods.append((q * kd * e).astype(BF16))
        ws = [jnp.dot(p, ones_ref[...], preferred_element_type=F32) for p in prods]
        coli = lax.broadcasted_iota(jnp.int32, (tc, GB), 1)
        rgrp = rowi & (GB - 1)
        att0 = jnp.zeros((tc, GB), F32)
        att1 = jnp.zeros((tc, GB), F32)
        for d in range(S):
            here = jnp.logical_and(coli == rgrp - d, rmod >= d)
            att0 = jnp.where(here, ws[d][:, 0:GB], att0)
            att1 = jnp.where(here, ws[d][:, GB:2 * GB], att1)

        heads = range(2)
        lk = [slice(hh * dk, (hh + 1) * dk) for hh in heads]
        lv = [slice(hh * dv, (hh + 1) * dv) for hh in heads]
        kv = [[_dot_tn(v[si * S:(si + 1) * S, lv[hh]], kt[si * S:(si + 1) * S, lk[hh]]) for hh in heads]
              for si in range(nsub)]
        st = [st_ref[hh] for hh in heads]
        inter = [[], []]
        for si in range(nsub):
            rows = slice(si * S, (si + 1) * S)
            for hh in heads:
                inter[hh].append(_dot_nt(qt[rows, lk[hh]], st[hh]))
                st[hh] = st[hh] * eg[si * S:si * S + 1, lk[hh]] + kv[si][hh]
        for hh in heads:
            st_ref[hh] = st[hh]

        for hh, att in ((0, att0), (1, att1)):
            diag = jnp.concatenate([_dot(att[g * GB:(g + 1) * GB], v[g * GB:(g + 1) * GB, lv[hh]])
                                    for g in range(tc // GB)], axis=0)
            o = diag + jnp.concatenate(inter[hh], axis=0)
            on = o * lax.rsqrt(jnp.mean(o * o, axis=1, keepdims=True) + EPS)
            out_ref[0, :, lv[hh]] = (on * ng_ref[:, lv[hh]] * _silu(gr_ref[0, :, lv[hh]])).astype(BF16)

    return pl.pallas_call(
        body,
        grid=(bsz, GLA_HEADS // 2, t // tc),
        in_specs=[pl.BlockSpec((1, tc, 2 * dk), lambda b, h, c: (b, c, 16 + h)),
                  pl.BlockSpec((1, tc, 2 * dk), lambda b, h, c: (b, c, 18 + h)),
                  pl.BlockSpec((1, tc, 2 * dv), lambda b, h, c: (b, c, 10 + h)),
                  pl.BlockSpec((1, tc, 2 * dv), lambda b, h, c: (b, c, 12 + h)),
                  pl.BlockSpec((1, tc, GLA_RANK), lambda b, h, c: (b, c, 0)),
                  pl.BlockSpec((GLA_RANK, 2 * dk), lambda b, h, c: (0, h)),
                  pl.BlockSpec((1, 2 * dk), lambda b, h, c: (0, h)),
                  pl.BlockSpec((1, 2 * dv), lambda b, h, c: (0, h)),
                  pl.BlockSpec((2 * dk, 2 * GB), lambda b, h, c: (0, 0))],
        out_specs=pl.BlockSpec((1, tc, 2 * dv), lambda b, h, c: (b, c, h)),
        out_shape=jax.ShapeDtypeStruct((bsz, t, GLA_HEADS * dv), BF16),
        scratch_shapes=[pltpu.VMEM((2, dv, dk), F32), pltpu.VMEM((tc + S, 2 * dk), F32),
                        pltpu.VMEM((tc + S, 2 * dk), F32)],
        compiler_params=_cp("parallel", "parallel", "arbitrary"),
        name="gla",
    )(p0, p0, p0, p0, g_low, gate_up, gate_b, norm_g, head_ones)


RET_L = 256


def _retention(p_ret, cos_t, sin_t, intra, inter, sdec, cdec, norm_g):
    bsz, t, _ = p_ret.shape
    L, D, H, W = RET_L, RET_DIM, RET_HEADS, RET_W

    def body(q_ref, k_ref, v_ref, g_ref, cos_ref, sin_ref, intra_ref, inter_ref, sdec_ref, cdec_ref, ng_ref,
             out_ref, s_ref):
        c = pl.program_id(1)

        @pl.when(c == 0)
        def _():
            s_ref[...] = jnp.zeros_like(s_ref)

        cs = cos_ref[...]
        sn = sin_ref[...]

        def rot(x):
            return x * cs + pltpu.roll(x, D // 2, 1) * sn

        heads = range(H)
        q = [rot(q_ref[0, :, h * D:(h + 1) * D]) * (D ** -0.5) for h in heads]
        k = [rot(k_ref[0, :, h * D:(h + 1) * D]) for h in heads]
        v = [v_ref[0, :, h * D:(h + 1) * D] for h in heads]
        s_st = [s_ref[h] for h in heads]
        s = [_dot_nt(q[h], k[h]) * intra_ref[h] for h in heads]
        qs = [_dot(q[h], s_st[h]) for h in heads]
        kv = [_dot_tn(k[h] * sdec_ref[h], v[h]) for h in heads]
        sv = [_dot(s[h], v[h]) for h in heads]
        for h in heads:
            o = sv[h] + inter_ref[h] * qs[h]
            s_ref[h] = cdec_ref[h] * s_st[h] + kv[h]
            oc = o - jnp.mean(o, axis=1, keepdims=True)
            on = oc * lax.rsqrt(jnp.mean(oc * oc, axis=1, keepdims=True) + EPS)
            hl = slice(h * D, (h + 1) * D)
            out_ref[0, :, hl] = (on * ng_ref[:, hl] * _silu(g_ref[0, :, hl])).astype(BF16)

    col = lambda j: (lambda b, c: (b, c, j))
    fix3 = lambda b, c: (0, 0, 0)
    return pl.pallas_call(
        body,
        grid=(bsz, t // L),
        in_specs=[pl.BlockSpec((1, L, W), col(0)),
                  pl.BlockSpec((1, L, W), col(1)),
                  pl.BlockSpec((1, L, W), col(2)),
                  pl.BlockSpec((1, L, W), col(3)),
                  pl.BlockSpec((L, D), lambda b, c: (c, 0)),
                  pl.BlockSpec((L, D), lambda b, c: (c, 0)),
                  pl.BlockSpec((H, L, L), fix3),
                  pl.BlockSpec((H, L, 1), fix3),
                  pl.BlockSpec((H, L, 1), fix3),
                  pl.BlockSpec((H, 1, 1), fix3),
                  pl.BlockSpec((1, W), lambda b, c: (0, 0))],
        out_specs=pl.BlockSpec((1, L, W), lambda b, c: (b, c, 0)),
        out_shape=jax.ShapeDtypeStruct((bsz, t, W), BF16),
        scratch_shapes=[pltpu.VMEM((H, D, D), F32)],
        compiler_params=_cp("parallel", "arbitrary"),
        name="retention",
    )(p_ret, p_ret, p_ret, p_ret, cos_t, sin_t, intra, inter, sdec, cdec, norm_g)


def _retention_tables(t):
    L, D = RET_L, RET_DIM
    inv = ROPE_BASE ** (-jnp.arange(0, D, 2, dtype=F32) / D)
    ang = jnp.arange(t).astype(F32)[:, None] * inv[None, :]
    cos = jnp.cos(ang)
    sin = jnp.sin(ang)
    cos_t = jnp.concatenate([cos, cos], axis=-1)
    sin_t = jnp.concatenate([-sin, sin], axis=-1)
    log_gamma = jnp.log1p(-jnp.exp2(-5.0 - jnp.arange(RET_HEADS, dtype=F32)))
    idx = jnp.arange(L, dtype=F32)
    causal = idx[:, None] >= idx[None, :]
    rel = jnp.where(causal, idx[:, None] - idx[None, :], 0.0)
    intra = jnp.where(causal, jnp.exp(log_gamma[:, None, None] * rel), 0.0)
    inter = jnp.exp(log_gamma[:, None] * (idx + 1.0))[:, :, None]
    sdec = jnp.exp(log_gamma[:, None] * (L - 1.0 - idx))[:, :, None]
    cdec = jnp.exp(log_gamma * L)[:, None, None]
    return cos_t, sin_t, intra, inter, sdec, cdec


RWP_TM = 256


def _seg_sum(x, bd):
    hi = x.astype(BF16)
    lo = (x - hi.astype(F32)).astype(BF16)
    return jnp.dot(hi, bd, preferred_element_type=F32) + jnp.dot(lo, bd, preferred_element_type=F32)


def _rwkv_prep(p1, mu, w_up, w0, a_up, a0, g_up, k_k, k_a, r_k, bd):
    bsz, t, _ = p1.shape
    tm, W = RWP_TM, RW_W

    def body(x_ref, xp_ref, mu_ref, wup_ref, w0_ref, aup_ref, a0_ref, gup_ref, kk_ref, ka_ref, rk_ref, bd_ref,
             r_out, lw_out, k_out, v_out, a_out, b_out, g_out, bonus_out):
        c = pl.program_id(1)
        cur = x_ref[0]
        prev = jnp.where(c == 0, 0.0, xp_ref[0, 7:8, :])
        rowi = lax.broadcasted_iota(jnp.int32, (tm, 1), 0)
        sh = jnp.where(rowi == 0, prev, pltpu.roll(cur, 1, 0))
        xm = cur + (sh - cur) * mu_ref[...]
        x_r = xm[:, 0:W]
        x_k = xm[:, W:2 * W]
        x_v = xm[:, 2 * W:3 * W]
        x_dl = xm[:, 3 * W:3 * W + 64]
        x_al = xm[:, 3 * W + 64:3 * W + 128]
        x_gl = xm[:, 3 * W + 128:3 * W + 256]
        wl = w0_ref[...] + _dot(jnp.tanh(x_dl), wup_ref[...])
        sp = jnp.maximum(-wl, 0.0) + jnp.log(1.0 + jnp.exp(-jnp.abs(wl)))
        lw_out[0] = -jnp.exp(-sp - 0.5)
        a = _sigmoid(a0_ref[...] + _dot(x_al, aup_ref[...]))
        g_out[0] = _dot(_sigmoid(x_gl), gup_ref[...])
        kk0 = x_k * kk_ref[...]
        nrm = jnp.sqrt(_seg_sum(kk0 * kk0, bd_ref[...]))
        kk = kk0 / jnp.maximum(nrm, 1e-12)
        k_h = x_k * (1.0 + (a - 1.0) * ka_ref[...])
        r_out[0] = x_r
        k_out[0] = k_h
        v_out[0] = x_v
        a_out[0] = -kk
        b_out[0] = kk * a
        bonus_out[0] = _seg_sum(x_r * k_h * rk_ref[...], bd_ref[...]) * x_v

    row = lambda b, c: (0, 0)
    blk = pl.BlockSpec((1, tm, W), lambda b, c: (b, c, 0))
    shp = jax.ShapeDtypeStruct((bsz, t, W), F32)
    return pl.pallas_call(
        body,
        grid=(bsz, t // tm),
        in_specs=[pl.BlockSpec((1, tm, RW_COLS), lambda b, c: (b, c, 0)),
                  pl.BlockSpec((1, 8, RW_COLS), lambda b, c: (b, jnp.maximum(c * (tm // 8) - 1, 0), 0)),
                  pl.BlockSpec((1, RW_COLS), row),
                  pl.BlockSpec((64, W), row), pl.BlockSpec((1, W), row),
                  pl.BlockSpec((64, W), row), pl.BlockSpec((1, W), row),
                  pl.BlockSpec((128, W), row),
                  pl.BlockSpec((1, W), row), pl.BlockSpec((1, W), row), pl.BlockSpec((1, W), row),
                  pl.BlockSpec((W, W), row)],
        out_specs=[blk] * 8,
        out_shape=[shp] * 8,
        compiler_params=_cp("parallel", "arbitrary"),
        name="rwkv_prep",
    )(p1, p1, mu, w_up, w0, a_up, a0, g_up, k_k, k_a, r_k, bd)


RW_L = 64
RW_TB = 128


def _rwkv_scan(r, lw, k, v, aa, bb, g_out, bonus, ln_g, ln_b):
    bsz, t, W = r.shape
    L, N, tb = RW_L, RW_DIM, RW_TB
    nck = tb // L

    def body(r_ref, lw_ref, k_ref, v_ref, a_ref, b_ref, g_ref, bonus_ref, lng_ref, lnb_ref, out_ref, h_ref):
        c = pl.program_id(1)

        @pl.when(c == 0)
        def _():
            h_ref[...] = jnp.zeros_like(h_ref)

        lw_all = lw_ref[0]
        rowi = lax.broadcasted_iota(jnp.int32, (tb, 1), 0) & (L - 1)
        cl = lw_all
        for s in (1, 2, 4, 8, 16, 32):
            cl = cl + jnp.where(rowi >= s, pltpu.roll(cl, s, 0), 0.0)
        cl_last = jnp.concatenate([jnp.broadcast_to(cl[(cc + 1) * L - 1:(cc + 1) * L, :], (L, W))
                                   for cc in range(nck)], axis=0)
        e_inv = jnp.exp(-cl)
        e_end = jnp.exp(cl_last - cl)
        p_end = jnp.exp(cl_last)
        at = a_ref[0] * jnp.exp(cl - lw_all)
        rt = r_ref[0] * jnp.exp(cl)
        bt = b_ref[0] * e_inv
        kt = k_ref[0] * e_inv
        b_end = b_ref[0] * e_end
        k_end = k_ref[0] * e_end
        v_all = v_ref[0]
        ri = lax.broadcasted_iota(jnp.int32, (L, L), 0)
        ci = lax.broadcasted_iota(jnp.int32, (L, L), 1)
        strict = ri > ci
        lower = ri >= ci
        eye = ri == ci
        heads = range(nck * RW_HEADS)

        def hs(z, u):
            cc, hh = divmod(u, RW_HEADS)
            return z[cc * L:(cc + 1) * L, hh * N:(hh + 1) * N]

        m = [_dot_nt(jnp.concatenate([hs(at, h), hs(rt, h)], axis=0),
                     jnp.concatenate([hs(bt, h), hs(kt, h)], axis=0)) for h in heads]
        a_ab = [jnp.where(strict, m[h][0:L, 0:L], 0.0) for h in heads]
        a_ak = [jnp.where(strict, m[h][0:L, L:2 * L], 0.0) for h in heads]
        a_rb = [jnp.where(lower, m[h][L:2 * L, 0:L], 0.0) for h in heads]
        a_rk = [jnp.where(lower, m[h][L:2 * L, L:2 * L], 0.0) for h in heads]
        end_t = [jnp.concatenate([hs(b_end, h), hs(k_end, h)], axis=1).T for h in heads]
        vp = [_dot(jnp.concatenate([a_ak[h], a_rk[h], end_t[h][N:2 * N]], axis=0), hs(v_all, h)) for h in heads]
        x = [jnp.concatenate([hs(at, h), vp[h][0:L]], axis=1) for h in heads]
        ap = a_ab
        for it in range(6):
            x = [x[h] + _dot(ap[h], x[h]) for h in heads]
            if it < 5:
                ap = [_dot(ap[h], ap[h]) for h in heads]
        post = [_dot(jnp.concatenate([a_rb[h], end_t[h][0:N]], axis=0), x[h]) for h in heads]
        lhs = []
        y0s = []
        h_adds = []
        for u in heads:
            q_hat = hs(rt, u) + post[u][0:L, 0:N]
            gmat = jnp.where(eye, hs(p_end, u), 0.0) + post[u][L:L + N, 0:N]
            lhs.append(jnp.concatenate([q_hat, gmat], axis=0))
            y0s.append(post[u][0:L, N:2 * N] + vp[u][L:2 * L])
            h_adds.append(post[u][L:L + N, N:2 * N] + vp[u][2 * L:2 * L + N])
        h_st = [h_ref[hh] for hh in range(RW_HEADS)]
        for cc in range(nck):
            outs = []
            for hh in range(RW_HEADS):
                u = cc * RW_HEADS + hh
                res = _dot(lhs[u], h_st[hh])
                y = res[0:L] + y0s[u]
                h_st[hh] = res[L:L + N] + h_adds[u]
                yc = y - jnp.mean(y, axis=1, keepdims=True)
                outs.append(yc * lax.rsqrt(jnp.mean(yc * yc, axis=1, keepdims=True) + RW_LN_EPS))
            yn = jnp.concatenate(outs, axis=1)
            rows = slice(cc * L, (cc + 1) * L)
            out_ref[0, rows, :] = ((yn * lng_ref[...] + lnb_ref[...] + bonus_ref[0, rows, :])
                                   * g_ref[0, rows, :]).astype(BF16)
        for hh in range(RW_HEADS):
            h_ref[hh] = h_st[hh]

    blk = pl.BlockSpec((1, tb, W), lambda b, c: (b, c, 0))
    vec = pl.BlockSpec((1, W), lambda b, c: (0, 0))
    return pl.pallas_call(
        body,
        grid=(bsz, t // tb),
        in_specs=[blk] * 8 + [vec, vec],
        out_specs=blk,
        out_shape=jax.ShapeDtypeStruct((bsz, t, W), BF16),
        scratch_shapes=[pltpu.VMEM((RW_HEADS, N, N), F32)],
        compiler_params=_cp("parallel", "arbitrary"),
        name="rwkv_scan",
    )(r, lw, k, v, aa, bb, g_out, bonus, ln_g, ln_b)


def _router(x2d, g, router_pad, tm=512):
    n, d = x2d.shape
    e = router_pad.shape[1]

    def body(x_ref, g_ref, r_ref, xn_ref, lg_ref):
        xn = _rms_rows(x_ref[...], g_ref[...])
        xn_ref[...] = xn.astype(BF16)
        lg_ref[...] = jnp.dot(xn, r_ref[...], preferred_element_type=F32, precision=lax.Precision.HIGHEST)

    return pl.pallas_call(
        body,
        grid=(n // tm,),
        in_specs=[pl.BlockSpec((tm, d), lambda i: (i, 0)),
                  pl.BlockSpec((1, d), lambda i: (0, 0)),
                  pl.BlockSpec((d, e), lambda i: (0, 0))],
        out_specs=[pl.BlockSpec((tm, d), lambda i: (i, 0)), pl.BlockSpec((tm, e), lambda i: (i, 0))],
        out_shape=[jax.ShapeDtypeStruct((n, d), BF16), jax.ShapeDtypeStruct((n, e), F32)],
        compiler_params=_cp("parallel"),
        name="moe_router",
    )(x2d, g.reshape(1, d), router_pad)


MOE_TM = 512
MOE_TF = 1792


def _experts(xs, row_w, item_tile, item_exp, item_lo, item_hi, wg, wu, wd):
    nrows, d = xs.shape
    tm, tf = MOE_TM, MOE_TF
    nf = D_FF // tf
    n_items = item_tile.shape[0]

    def body(it_ref, ie_ref, lo_ref, hi_ref, x_ref, w_ref, wg_ref, wu_ref, wd_ref, o_ref):
        i = pl.program_id(0)
        j = pl.program_id(1)
        first = jnp.logical_or(i == 0, it_ref[i] != it_ref[jnp.maximum(i - 1, 0)])

        @pl.when(jnp.logical_and(first, j == 0))
        def _():
            o_ref[...] = jnp.zeros_like(o_ref)

        lo = lo_ref[i]
        hi = hi_ref[i]

        @pl.when(lo < hi)
        def _():
            x = x_ref[...]
            gg = jnp.dot(x, wg_ref[0], preferred_element_type=F32)
            uu = jnp.dot(x, wu_ref[0], preferred_element_type=F32)
            act = (_silu(gg) * uu).astype(BF16)
            part = jnp.dot(act, wd_ref[0], preferred_element_type=F32)
            rowi = lax.broadcasted_iota(jnp.int32, (tm, 1), 0)
            mine = jnp.logical_and(rowi >= lo, rowi < hi)
            o_ref[...] += part * jnp.where(mine, w_ref[...], 0.0)

    grid_spec = pltpu.PrefetchScalarGridSpec(
        num_scalar_prefetch=4,
        grid=(n_items, nf),
        in_specs=[pl.BlockSpec((tm, d), lambda i, j, it, ie, lo, hi: (it[i], 0)),
                  pl.BlockSpec((tm, 1), lambda i, j, it, ie, lo, hi: (it[i], 0)),
                  pl.BlockSpec((1, d, tf), lambda i, j, it, ie, lo, hi: (ie[i], 0, j)),
                  pl.BlockSpec((1, d, tf), lambda i, j, it, ie, lo, hi: (ie[i], 0, j)),
                  pl.BlockSpec((1, tf, d), lambda i, j, it, ie, lo, hi: (ie[i], j, 0))],
        out_specs=pl.BlockSpec((tm, d), lambda i, j, it, ie, lo, hi: (it[i], 0)),
    )
    return pl.pallas_call(
        body,
        grid_spec=grid_spec,
        out_shape=jax.ShapeDtypeStruct((nrows, d), F32),
        compiler_params=_cp("arbitrary", "arbitrary"),
        name="moe_experts",
    )(item_tile, item_exp, item_lo, item_hi, xs, row_w, wg, wu, wd)


def _combine_norm(h, y0, y1, g, tm=512):
    n, d = h.shape

    def body(h_ref, a_ref, b_ref, g_ref, o_ref):
        o_ref[...] = _rms_rows(h_ref[...] + (a_ref[...] + b_ref[...]), g_ref[...])

    blk = pl.BlockSpec((tm, d), lambda i: (i, 0))
    return pl.pallas_call(
        body,
        grid=(n // tm,),
        in_specs=[blk, blk, blk, pl.BlockSpec((1, d), lambda i: (0, 0))],
        out_specs=blk,
        out_shape=jax.ShapeDtypeStruct((n, d), F32),
        compiler_params=_cp("parallel"),
        name="combine_norm",
    )(h, y0, y1, g.reshape(1, d))


def _route(logits, n):
    tm = MOE_TM
    na = n * TOP_K
    n_tiles = na // tm
    top_val, top_idx = lax.top_k(logits, TOP_K)
    top_w = jax.nn.softmax(top_val, axis=-1)
    e_flat = top_idx.reshape(-1).astype(jnp.int32)
    w_flat = top_w.reshape(-1)
    tok = jnp.arange(na, dtype=jnp.int32) // TOP_K
    _, sorted_tok, sorted_w = lax.sort((e_flat, tok, w_flat), num_keys=1, is_stable=True)
    onehot = (e_flat[:, None] == jnp.arange(N_EXPERTS, dtype=jnp.int32)[None, :]).astype(jnp.int32)
    rank = jnp.take_along_axis(jnp.cumsum(onehot, axis=0), e_flat[:, None], axis=1)[:, 0] - 1
    counts = jnp.sum(onehot, axis=0)
    ends = jnp.cumsum(counts)
    pos = (ends - counts)[e_flat] + rank
    cuts = jnp.sort(jnp.concatenate([jnp.arange(n_tiles, dtype=jnp.int32) * tm, ends[:-1].astype(jnp.int32)]))
    nxt = jnp.concatenate([cuts[1:], jnp.full((1,), na, jnp.int32)])
    item_tile = jnp.minimum(cuts // tm, n_tiles - 1)
    item_exp = jnp.minimum(jnp.searchsorted(ends, cuts, side="right"), N_EXPERTS - 1).astype(jnp.int32)
    item_lo = cuts - item_tile * tm
    item_hi = nxt - item_tile * tm
    return sorted_tok, sorted_w, (item_tile, item_exp, item_lo, item_hi), pos.reshape(n, TOP_K)


def kernel(x, e_norm1_g, e_w_in, e_ml_conv_w, e_ml_conv_b, e_ml_gate_b, e_ml_norm_g, e_gla_gate_up, e_gla_gate_b,
           e_gla_norm_g, e_w_out, e_norm2_g, e_ffn_w_gate, e_ffn_w_up, e_ffn_w_down, o_norm1_g, o_w_in,
           o_ret_norm_g, o_rw_mu, o_rw_w_up, o_rw_w0, o_rw_a_up, o_rw_a0, o_rw_g_up, o_rw_k_k, o_rw_k_a, o_rw_r_k,
           o_rw_ln_g, o_rw_ln_b, o_w_out, o_norm2_g, o_moe_router, o_moe_w_gate, o_moe_w_up, o_moe_w_down,
           final_norm_g):
    bsz, t, d = x.shape
    n = bsz * t
    h0 = x.reshape(n, d)

    w = e_w_in[0]
    w0 = jnp.concatenate([w[:, :2048], w[:, 2056:3592], w[:, 2048:2056], w[:, 3592:3608],
                          jnp.zeros((d, E_COLS_PAD - 3608), F32)], axis=1).astype(BF16)
    p0 = _norm_matmul(h0, e_norm1_g[0], w0).reshape(bsz, t, E_COLS_PAD)
    gates_col = p0[:, :, 3584:3592]
    gates_row = jnp.transpose(gates_col, (0, 2, 1))
    h_ml = _mlstm(p0, gates_col, gates_row, e_ml_gate_b[0].reshape(1, -1), e_ml_gate_b[0].reshape(-1, 1),
                  e_ml_conv_w[0], e_ml_conv_b[0].reshape(1, -1), e_ml_norm_g[0].reshape(1, -1))
    g_low = p0[:, :, 3592:3608]
    o_gla = _gla(p0, g_low, e_gla_gate_up[0].astype(BF16), e_gla_gate_b[0].reshape(1, -1),
                 e_gla_norm_g[0].reshape(1, -1))
    h1 = _mix_out(h_ml.reshape(n, -1), o_gla.reshape(n, -1), e_w_out[0].astype(BF16), h0)
    h2 = _ffn(h1, e_norm2_g[0], e_ffn_w_gate[0].astype(BF16), e_ffn_w_up[0].astype(BF16),
              e_ffn_w_down[0].astype(BF16))

    w = o_w_in[0]
    p_ret = _norm_matmul(h2, o_norm1_g[0], w[:, :4 * RET_W].astype(BF16), tn=1024).reshape(bsz, t, -1)
    p1 = _norm_matmul(h2, o_norm1_g[0], w[:, 4 * RET_W:].astype(BF16), tn=896).reshape(bsz, t, -1)
    y_ret = _retention(p_ret, *_retention_tables(t), o_ret_norm_g[0].reshape(1, -1))
    head_of = jnp.arange(RW_W) // RW_DIM
    bd = (head_of[:, None] == head_of[None, :]).astype(BF16)
    row = lambda a: a.reshape(1, -1)
    r, lw, k, v, aa, bb, g_out, bonus = _rwkv_prep(
        p1, row(o_rw_mu[0]), o_rw_w_up[0].astype(BF16), row(o_rw_w0[0]), o_rw_a_up[0].astype(BF16),
        row(o_rw_a0[0]), o_rw_g_up[0].astype(BF16), row(o_rw_k_k[0]), row(o_rw_k_a[0]), row(o_rw_r_k[0]), bd)
    y_rw = _rwkv_scan(r, lw, k, v, aa, bb, g_out, bonus, row(o_rw_ln_g[0]), row(o_rw_ln_b[0]))
    h3 = _mix_out(y_ret.reshape(n, -1), y_rw.reshape(n, -1), o_w_out[0].astype(BF16), h2)

    router_pad = jnp.zeros((d, 128), F32).at[:, :N_EXPERTS].set(o_moe_router[0])
    xn, logits = _router(h3, o_norm2_g[0], router_pad)
    sorted_tok, sorted_w, items, pos = _route(logits[:, :N_EXPERTS], n)
    xs = xn.at[sorted_tok].get(mode="promise_in_bounds")
    ys = _experts(xs, sorted_w.reshape(-1, 1), *items, o_moe_w_gate[0].astype(BF16),
                  o_moe_w_up[0].astype(BF16), o_moe_w_down[0].astype(BF16))
    y0 = ys.at[pos[:, 0]].get(mode="promise_in_bounds")
    y1 = ys.at[pos[:, 1]].get(mode="promise_in_bounds")
    out = _combine_norm(h3, y0, y1, final_norm_g)
    return out.reshape(bsz, t, d)
```

```python
import functools

import numpy as np
import jax
import jax.numpy as jnp
from jax import lax
from jax.experimental import pallas as pl
from jax.experimental.pallas import tpu as pltpu

F32 = jnp.float32
BF16 = jnp.bfloat16

D_MODEL = 1024
EPS = 1e-6
ML_HEADS, ML_DIM, ML_W, ML_CONV = 4, 128, 512, 4
GLA_HEADS, GLA_DK, GLA_DV, GLA_RANK, GLA_TAU = 4, 64, 128, 16, 16.0
E_COLS_PAD = 3840
RET_HEADS, RET_DIM, RET_W = 4, 128, 512
ROPE_BASE = 10000.0
RW_HEADS, RW_DIM, RW_W = 8, 64, 512
RW_COLS = 1792
RW_LN_EPS = 64e-5
D_FF = 3584
N_EXPERTS = 8
TOP_K = 2

VMEM_LIMIT = 48 * 1024 * 1024
NEG = -1e30


def _cp(*sem):
    return pltpu.CompilerParams(dimension_semantics=sem, vmem_limit_bytes=VMEM_LIMIT)


def _sigmoid(x):
    return 1.0 / (1.0 + jnp.exp(-x))


def _silu(x):
    return x * _sigmoid(x)


def _log_sigmoid(x):
    return jnp.minimum(x, 0.0) - jnp.log(1.0 + jnp.exp(-jnp.abs(x)))


def _dot(a, b):
    return jnp.dot(a.astype(BF16), b.astype(BF16), preferred_element_type=F32)


def _dot_nt(a, b):
    return lax.dot_general(a.astype(BF16), b.astype(BF16), (((1,), (1,)), ((), ())), preferred_element_type=F32)


def _dot_tn(a, b):
    return jnp.dot(a.T.astype(BF16), b.astype(BF16), preferred_element_type=F32)


def _rms_rows(x, g):
    ms = jnp.mean(x * x, axis=-1, keepdims=True)
    return x * lax.rsqrt(ms + EPS) * g


def _norm_matmul(x2d, g, w_bf16, tm=1024, tn=1280):
    n, d = x2d.shape
    c = w_bf16.shape[1]

    def body(x_ref, g_ref, w_ref, o_ref, xn_ref):
        @pl.when(pl.program_id(1) == 0)
        def _():
            xn_ref[...] = _rms_rows(x_ref[...], g_ref[...]).astype(BF16)

        o_ref[...] = jnp.dot(xn_ref[...], w_ref[...], preferred_element_type=F32)

    return pl.pallas_call(
        body,
        grid=(n // tm, c // tn),
        in_specs=[pl.BlockSpec((tm, d), lambda i, j: (i, 0)),
                  pl.BlockSpec((1, d), lambda i, j: (0, 0)),
                  pl.BlockSpec((d, tn), lambda i, j: (0, j))],
        out_specs=pl.BlockSpec((tm, tn), lambda i, j: (i, j)),
        out_shape=jax.ShapeDtypeStruct((n, c), F32),
        scratch_shapes=[pltpu.VMEM((tm, d), BF16)],
        compiler_params=_cp("parallel", "arbitrary"),
        name="norm_matmul",
    )(x2d, g.reshape(1, d), w_bf16)


def _mix_out(a, b, w_bf16, resid, tm=1024):
    n, wa = a.shape
    wb = b.shape[1]
    d = w_bf16.shape[1]

    def body(a_ref, b_ref, w_ref, r_ref, o_ref):
        acc = jnp.dot(a_ref[...], w_ref[0:wa, :], preferred_element_type=F32)
        acc = acc + jnp.dot(b_ref[...], w_ref[wa:wa + wb, :], preferred_element_type=F32)
        o_ref[...] = r_ref[...] + acc

    return pl.pallas_call(
        body,
        grid=(n // tm,),
        in_specs=[pl.BlockSpec((tm, wa), lambda i: (i, 0)),
                  pl.BlockSpec((tm, wb), lambda i: (i, 0)),
                  pl.BlockSpec((wa + wb, d), lambda i: (0, 0)),
                  pl.BlockSpec((tm, d), lambda i: (i, 0))],
        out_specs=pl.BlockSpec((tm, d), lambda i: (i, 0)),
        out_shape=jax.ShapeDtypeStruct((n, d), F32),
        compiler_params=_cp("parallel"),
        name="mix_out",
    )(a, b, w_bf16, resid)


def _ffn(x2d, g, wg, wu, wd, tm=512, tf=1792):
    n, d = x2d.shape
    f = wg.shape[1]
    nf = f // tf

    def body(x_ref, g_ref, wg_ref, wu_ref, wd_ref, o_ref, xn_ref, acc_ref):
        j = pl.program_id(1)

        @pl.when(j == 0)
        def _():
            xn_ref[...] = _rms_rows(x_ref[...], g_ref[...]).astype(BF16)
            acc_ref[...] = jnp.zeros_like(acc_ref)

        xn = xn_ref[...]
        gg = jnp.dot(xn, wg_ref[...], preferred_element_type=F32)
        uu = jnp.dot(xn, wu_ref[...], preferred_element_type=F32)
        act = (_silu(gg) * uu).astype(BF16)
        acc_ref[...] += jnp.dot(act, wd_ref[...], preferred_element_type=F32)

        @pl.when(j == nf - 1)
        def _():
            o_ref[...] = x_ref[...] + acc_ref[...]

    return pl.pallas_call(
        body,
        grid=(n // tm, nf),
        in_specs=[pl.BlockSpec((tm, d), lambda i, j: (i, 0)),
                  pl.BlockSpec((1, d), lambda i, j: (0, 0)),
                  pl.BlockSpec((d, tf), lambda i, j: (0, j)),
                  pl.BlockSpec((d, tf), lambda i, j: (0, j)),
                  pl.BlockSpec((tf, d), lambda i, j: (j, 0))],
        out_specs=pl.BlockSpec((tm, d), lambda i, j: (i, 0)),
        out_shape=jax.ShapeDtypeStruct((n, d), F32),
        scratch_shapes=[pltpu.VMEM((tm, d), BF16), pltpu.VMEM((tm, d), F32)],
        compiler_params=_cp("parallel", "arbitrary"),
        name="ffn",
    )(x2d, g.reshape(1, d), wg, wu, wd)


ML_L = 256


def _mlstm(p0, gates_col, gates_row, gate_b_row, gate_b_col, conv_w, conv_b, norm_g):
    bsz, t, _ = p0.shape
    L, D, H, W = ML_L, ML_DIM, ML_HEADS, ML_W

    def body(q_ref, k_ref, v_ref, og_ref, gc_ref, gr_ref, gbr_ref, gbc_ref, cwq_ref, cwk_ref, cbq_ref, cbk_ref,
             ng_ref, out_ref, qext, kext, c_ref, n_ref, m_ref):
        c = pl.program_id(1)

        @pl.when(c == 0)
        def _():
            qext[0:8, :] = jnp.zeros((8, W), F32)
            kext[0:8, :] = jnp.zeros((8, W), F32)
            c_ref[...] = jnp.zeros_like(c_ref)
            n_ref[...] = jnp.zeros_like(n_ref)
            m_ref[...] = jnp.zeros_like(m_ref)

        qext[8:, :] = q_ref[0]
        kext[8:, :] = k_ref[0]

        def conv(ext, cw_ref, cb_ref):
            acc = cb_ref[...] + cw_ref[0:1, :] * ext[pl.ds(8 - ML_CONV + 1, L), :]
            for kk in range(1, ML_CONV):
                acc = acc + cw_ref[kk:kk + 1, :] * ext[pl.ds(8 - ML_CONV + 1 + kk, L), :]
            return _silu(acc)

        q_all = conv(qext, cwq_ref, cbq_ref) * (D ** -0.5)
        k_all = conv(kext, cwk_ref, cbk_ref)
        qext[0:8, :] = qext[L:L + 8, :]
        kext[0:8, :] = kext[L:L + 8, :]

        gcol = gc_ref[0] + gbr_ref[...]
        grow = gr_ref[0] + gbc_ref[...]
        fcol = _log_sigmoid(gcol[:, H:2 * H])
        frow = _log_sigmoid(grow[H:2 * H, :])
        ri = lax.broadcasted_iota(jnp.int32, (L, L), 0)
        ci = lax.broadcasted_iota(jnp.int32, (L, L), 1)
        causal = ri >= ci
        heads = range(H)
        hs = lambda z, h: z[:, h * D:(h + 1) * D]
        w_intra, w_inter, w_state, carry, m_row, m_new = [], [], [], [], [], []
        for h in heads:
            f_row = frow[h:h + 1, :]
            i_row = grow[h:h + 1, :]
            f_col = fcol[:, h:h + 1]
            i_col = gcol[:, h:h + 1]
            m_st = m_ref[h]
            b_col = jnp.sum(jnp.where(causal, f_row, 0.0), axis=1, keepdims=True)
            b_row = jnp.sum(jnp.where(ri <= ci, f_col, 0.0), axis=0, keepdims=True)
            g_tot = jnp.sum(f_row, axis=1, keepdims=True)
            d_intra = jnp.where(causal, b_col - b_row + i_row, NEG)
            d_inter = b_col + m_st
            mr = jnp.maximum(d_inter, jnp.max(d_intra, axis=1, keepdims=True))
            w_intra.append(jnp.exp(d_intra - mr))
            w_inter.append(jnp.exp(d_inter - mr))
            m_row.append(mr)
            d_state = g_tot - b_col + i_col
            mn = jnp.maximum(g_tot + m_st, jnp.max(d_state, axis=0, keepdims=True))
            w_state.append(jnp.exp(d_state - mn))
            carry.append(jnp.exp(g_tot + m_st - mn))
            m_new.append(mn)
        qh = [hs(q_all, h) for h in heads]
        kh = [hs(k_all, h) for h in heads]
        vh = [v_ref[0, :, h * D:(h + 1) * D] for h in heads]
        c_st = [c_ref[h] for h in heads]
        n_st = [n_ref[h] for h in heads]
        s = [_dot_nt(qh[h], kh[h]) * w_intra[h] for h in heads]
        qc = [_dot(qh[h], c_st[h]) for h in heads]
        kw = [kh[h] * w_state[h] for h in heads]
        kv = [_dot_tn(kw[h], vh[h]) for h in heads]
        sv = [_dot(s[h], vh[h]) for h in heads]
        for h in heads:
            num = sv[h] + w_inter[h] * qc[h]
            den = jnp.sum(s[h], axis=1, keepdims=True) + w_inter[h] * jnp.sum(qh[h] * n_st[h], axis=1, keepdims=True)
            hval = num / jnp.maximum(jnp.abs(den), jnp.exp(-m_row[h]))
            c_ref[h] = carry[h] * c_st[h] + kv[h]
            n_ref[h] = carry[h] * n_st[h] + jnp.sum(kw[h], axis=0, keepdims=True)
            m_ref[h] = m_new[h]
            hg = _sigmoid(og_ref[0, :, h * D:(h + 1) * D]) * hval
            hc = hg - jnp.mean(hg, axis=1, keepdims=True)
            hn = hc * lax.rsqrt(jnp.mean(hc * hc, axis=1, keepdims=True) + EPS)
            out_ref[0, :, h * D:(h + 1) * D] = (hn * ng_ref[:, h * D:(h + 1) * D]).astype(BF16)

    col = lambda j: (lambda b, c: (b, c, j))
    fix = lambda j: (lambda b, c: (0, j))
    return pl.pallas_call(
        body,
        grid=(bsz, t // L),
        in_specs=[pl.BlockSpec((1, L, W), col(0)),
                  pl.BlockSpec((1, L, W), col(1)),
                  pl.BlockSpec((1, L, W), col(2)),
                  pl.BlockSpec((1, L, W), col(3)),
                  pl.BlockSpec((1, L, 2 * H), lambda b, c: (b, c, 0)),
                  pl.BlockSpec((1, 2 * H, L), lambda b, c: (b, 0, c)),
                  pl.BlockSpec((1, 2 * H), fix(0)),
                  pl.BlockSpec((2 * H, 1), fix(0)),
                  pl.BlockSpec((ML_CONV, W), fix(0)),
                  pl.BlockSpec((ML_CONV, W), fix(1)),
                  pl.BlockSpec((1, W), fix(0)),
                  pl.BlockSpec((1, W), fix(1)),
                  pl.BlockSpec((1, W), fix(0))],
        out_specs=pl.BlockSpec((1, L, W), lambda b, c: (b, c, 0)),
        out_shape=jax.ShapeDtypeStruct((bsz, t, W), BF16),
        scratch_shapes=[pltpu.VMEM((L + 8, W), F32), pltpu.VMEM((L + 8, W), F32),
                        pltpu.VMEM((H, D, D), F32), pltpu.VMEM((H, 1, D), F32), pltpu.VMEM((H, 1, 1), F32)],
        compiler_params=_cp("parallel", "arbitrary"),
        name="mlstm",
    )(p0, p0, p0, p0, gates_col, gates_row, gate_b_row, gate_b_col, conv_w, conv_w, conv_b, conv_b, norm_g)


GLA_TC = 256
GLA_SUB = 16
GLA_GROUP = 128


def _gla(p0, g_low, gate_up, gate_b, norm_g):
    bsz, t, _ = p0.shape
    tc, S, GB = GLA_TC, GLA_SUB, GLA_GROUP
    head_ones = (jnp.arange(2 * GLA_DK)[:, None] // GLA_DK == jnp.arange(2 * GB)[None, :] // GB).astype(BF16)
    nsub = tc // S
    dk, dv = GLA_DK, GLA_DV

    def body(q_ref, k_ref, v_ref, gr_ref, gl_ref, gu_ref, gbias_ref, ng_ref, ones_ref, out_ref, st_ref, ksh, bsh):
        c = pl.program_id(2)

        @pl.when(c == 0)
        def _():
            st_ref[...] = jnp.zeros_like(st_ref)
            ksh[0:S, :] = jnp.zeros((S, 2 * dk), F32)
            bsh[0:S, :] = jnp.zeros((S, 2 * dk), F32)

        z = _dot(gl_ref[0], gu_ref[...]) + gbias_ref[...]
        la = _log_sigmoid(z) / GLA_TAU
        rowi = lax.broadcasted_iota(jnp.int32, (tc, 1), 0)
        rmod = rowi & (S - 1)
        bcum = la
        rsum = la
        for s in (1, 2, 4, 8):
            bcum = bcum + jnp.where(rmod >= s, pltpu.roll(bcum, s, 0), 0.0)
            rsum = rsum + jnp.where(rmod < S - s, pltpu.roll(rsum, tc - s, 0), 0.0)
        q = q_ref[0] * (dk ** -0.5)
        k = k_ref[0]
        v = v_ref[0]
        qt = q * jnp.exp(bcum)
        kt = k * jnp.exp(rsum - la)
        eg = jnp.exp(bcum + rsum - la)

        ksh[S:, :] = k
        bsh[S:, :] = bcum
        prods = []
        for d in range(S):
            kd = k if d == 0 else ksh[pl.ds(S - d, tc), :]
            bd = bcum if d == 0 else bsh[pl.ds(S - d, tc), :]
            e = jnp.exp(jnp.where(rmod >= d, bcum - bd, 0.0))
            prods.append((q * kd * e).astype(BF16))
        ws = [jnp.dot(p, ones_ref[...], preferred_element_type=F32) for p in prods]
        coli = lax.broadcasted_iota(jnp.int32, (tc, GB), 1)
        rgrp = rowi & (GB - 1)
        att0 = jnp.zeros((tc, GB), F32)
        att1 = jnp.zeros((tc, GB), F32)
        for d in range(S):
            here = jnp.logical_and(coli == rgrp - d, rmod >= d)
            att0 = jnp.where(here, ws[d][:, 0:GB], att0)
            att1 = jnp.where(here, ws[d][:, GB:2 * GB], att1)

        heads = range(2)
        lk = [slice(hh * dk, (hh + 1) * dk) for hh in heads]
        lv = [slice(hh * dv, (hh + 1) * dv) for hh in heads]
        kv = [[_dot_tn(v[si * S:(si + 1) * S, lv[hh]], kt[si * S:(si + 1) * S, lk[hh]]) for hh in heads]
              for si in range(nsub)]
        st = [st_ref[hh] for hh in heads]
        inter = [[], []]
        for si in range(nsub):
            rows = slice(si * S, (si + 1) * S)
            for hh in heads:
                inter[hh].append(_dot_nt(qt[rows, lk[hh]], st[hh]))
                st[hh] = st[hh] * eg[si * S:si * S + 1, lk[hh]] + kv[si][hh]
        for hh in heads:
            st_ref[hh] = st[hh]

        for hh, att in ((0, att0), (1, att1)):
            diag = jnp.concatenate([_dot(att[g * GB:(g + 1) * GB], v[g * GB:(g + 1) * GB, lv[hh]])
                                    for g in range(tc // GB)], axis=0)
            o = diag + jnp.concatenate(inter[hh], axis=0)
            on = o * lax.rsqrt(jnp.mean(o * o, axis=1, keepdims=True) + EPS)
            out_ref[0, :, lv[hh]] = (on * ng_ref[:, lv[hh]] * _silu(gr_ref[0, :, lv[hh]])).astype(BF16)

    return pl.pallas_call(
        body,
        grid=(bsz, GLA_HEADS // 2, t // tc),
        in_specs=[pl.BlockSpec((1, tc, 2 * dk), lambda b, h, c: (b, c, 16 + h)),
                  pl.BlockSpec((1, tc, 2 * dk), lambda b, h, c: (b, c, 18 + h)),
                  pl.BlockSpec((1, tc, 2 * dv), lambda b, h, c: (b, c, 10 + h)),
                  pl.BlockSpec((1, tc, 2 * dv), lambda b, h, c: (b, c, 12 + h)),
                  pl.BlockSpec((1, tc, GLA_RANK), lambda b, h, c: (b, c, 0)),
                  pl.BlockSpec((GLA_RANK, 2 * dk), lambda b, h, c: (0, h)),
                  pl.BlockSpec((1, 2 * dk), lambda b, h, c: (0, h)),
                  pl.BlockSpec((1, 2 * dv), lambda b, h, c: (0, h)),
                  pl.BlockSpec((2 * dk, 2 * GB), lambda b, h, c: (0, 0))],
        out_specs=pl.BlockSpec((1, tc, 2 * dv), lambda b, h, c: (b, c, h)),
        out_shape=jax.ShapeDtypeStruct((bsz, t, GLA_HEADS * dv), BF16),
        scratch_shapes=[pltpu.VMEM((2, dv, dk), F32), pltpu.VMEM((tc + S, 2 * dk), F32),
                        pltpu.VMEM((tc + S, 2 * dk), F32)],
        compiler_params=_cp("parallel", "parallel", "arbitrary"),
        name="gla",
    )(p0, p0, p0, p0, g_low, gate_up, gate_b, norm_g, head_ones)


RET_L = 256


def _retention(p_ret, cos_t, sin_t, intra, inter, sdec, cdec, norm_g):
    bsz, t, _ = p_ret.shape
    L, D, H, W = RET_L, RET_DIM, RET_HEADS, RET_W

    def body(q_ref, k_ref, v_ref, g_ref, cos_ref, sin_ref, intra_ref, inter_ref, sdec_ref, cdec_ref, ng_ref,
             out_ref, s_ref):
        c = pl.program_id(1)

        @pl.when(c == 0)
        def _():
            s_ref[...] = jnp.zeros_like(s_ref)

        cs = cos_ref[...]
        sn = sin_ref[...]

        def rot(x):
            return x * cs + pltpu.roll(x, D // 2, 1) * sn

        heads = range(H)
        q = [rot(q_ref[0, :, h * D:(h + 1) * D]) * (D ** -0.5) for h in heads]
        k = [rot(k_ref[0, :, h * D:(h + 1) * D]) for h in heads]
        v = [v_ref[0, :, h * D:(h + 1) * D] for h in heads]
        s_st = [s_ref[h] for h in heads]
        s = [_dot_nt(q[h], k[h]) * intra_ref[h] for h in heads]
        qs = [_dot(q[h], s_st[h]) for h in heads]
        kv = [_dot_tn(k[h] * sdec_ref[h], v[h]) for h in heads]
        sv = [_dot(s[h], v[h]) for h in heads]
        for h in heads:
            o = sv[h] + inter_ref[h] * qs[h]
            s_ref[h] = cdec_ref[h] * s_st[h] + kv[h]
            oc = o - jnp.mean(o, axis=1, keepdims=True)
            on = oc * lax.rsqrt(jnp.mean(oc * oc, axis=1, keepdims=True) + EPS)
            hl = slice(h * D, (h + 1) * D)
            out_ref[0, :, hl] = (on * ng_ref[:, hl] * _silu(g_ref[0, :, hl])).astype(BF16)

    col = lambda j: (lambda b, c: (b, c, j))
    fix3 = lambda b, c: (0, 0, 0)
    return pl.pallas_call(
        body,
        grid=(bsz, t // L),
        in_specs=[pl.BlockSpec((1, L, W), col(0)),
                  pl.BlockSpec((1, L, W), col(1)),
                  pl.BlockSpec((1, L, W), col(2)),
                  pl.BlockSpec((1, L, W), col(3)),
                  pl.BlockSpec((L, D), lambda b, c: (c, 0)),
                  pl.BlockSpec((L, D), lambda b, c: (c, 0)),
                  pl.BlockSpec((H, L, L), fix3),
                  pl.BlockSpec((H, L, 1), fix3),
                  pl.BlockSpec((H, L, 1), fix3),
                  pl.BlockSpec((H, 1, 1), fix3),
                  pl.BlockSpec((1, W), lambda b, c: (0, 0))],
        out_specs=pl.BlockSpec((1, L, W), lambda b, c: (b, c, 0)),
        out_shape=jax.ShapeDtypeStruct((bsz, t, W), BF16),
        scratch_shapes=[pltpu.VMEM((H, D, D), F32)],
        compiler_params=_cp("parallel", "arbitrary"),
        name="retention",
    )(p_ret, p_ret, p_ret, p_ret, cos_t, sin_t, intra, inter, sdec, cdec, norm_g)


def _retention_tables(t):
    L, D = RET_L, RET_DIM
    inv = ROPE_BASE ** (-jnp.arange(0, D, 2, dtype=F32) / D)
    ang = jnp.arange(t).astype(F32)[:, None] * inv[None, :]
    cos = jnp.cos(ang)
    sin = jnp.sin(ang)
    cos_t = jnp.concatenate([cos, cos], axis=-1)
    sin_t = jnp.concatenate([-sin, sin], axis=-1)
    log_gamma = jnp.log1p(-jnp.exp2(-5.0 - jnp.arange(RET_HEADS, dtype=F32)))
    idx = jnp.arange(L, dtype=F32)
    causal = idx[:, None] >= idx[None, :]
    rel = jnp.where(causal, idx[:, None] - idx[None, :], 0.0)
    intra = jnp.where(causal, jnp.exp(log_gamma[:, None, None] * rel), 0.0)
    inter = jnp.exp(log_gamma[:, None] * (idx + 1.0))[:, :, None]
    sdec = jnp.exp(log_gamma[:, None] * (L - 1.0 - idx))[:, :, None]
    cdec = jnp.exp(log_gamma * L)[:, None, None]
    return cos_t, sin_t, intra, inter, sdec, cdec


RWP_TM = 256


def _seg_sum(x, bd):
    hi = x.astype(BF16)
    lo = (x - hi.astype(F32)).astype(BF16)
    return jnp.dot(hi, bd, preferred_element_type=F32) + jnp.dot(lo, bd, preferred_element_type=F32)


def _rwkv_prep(p1, mu, w_up, w0, a_up, a0, g_up, k_k, k_a, r_k, bd):
    bsz, t, _ = p1.shape
    tm, W = RWP_TM, RW_W

    def body(x_ref, xp_ref, mu_ref, wup_ref, w0_ref, aup_ref, a0_ref, gup_ref, kk_ref, ka_ref, rk_ref, bd_ref,
             r_out, lw_out, k_out, v_out, a_out, b_out, g_out, bonus_out):
        c = pl.program_id(1)
        cur = x_ref[0]
        prev = jnp.where(c == 0, 0.0, xp_ref[0, 7:8, :])
        rowi = lax.broadcasted_iota(jnp.int32, (tm, 1), 0)
        sh = jnp.where(rowi == 0, prev, pltpu.roll(cur, 1, 0))
        xm = cur + (sh - cur) * mu_ref[...]
        x_r = xm[:, 0:W]
        x_k = xm[:, W:2 * W]
        x_v = xm[:, 2 * W:3 * W]
        x_dl = xm[:, 3 * W:3 * W + 64]
        x_al = xm[:, 3 * W + 64:3 * W + 128]
        x_gl = xm[:, 3 * W + 128:3 * W + 256]
        wl = w0_ref[...] + _dot(jnp.tanh(x_dl), wup_ref[...])
        sp = jnp.maximum(-wl, 0.0) + jnp.log(1.0 + jnp.exp(-jnp.abs(wl)))
        lw_out[0] = -jnp.exp(-sp - 0.5)
        a = _sigmoid(a0_ref[...] + _dot(x_al, aup_ref[...]))
        g_out[0] = _dot(_sigmoid(x_gl), gup_ref[...]).astype(BF16)
        kk0 = x_k * kk_ref[...]
        nrm = jnp.sqrt(_seg_sum(kk0 * kk0, bd_ref[...]))
        kk = kk0 / jnp.maximum(nrm, 1e-12)
        k_h = x_k * (1.0 + (a - 1.0) * ka_ref[...])
        r_out[0] = x_r.astype(BF16)
        k_out[0] = k_h.astype(BF16)
        v_out[0] = x_v.astype(BF16)
        a_out[0] = (-kk).astype(BF16)
        b_out[0] = (kk * a).astype(BF16)
        bonus_out[0] = (_seg_sum(x_r * k_h * rk_ref[...], bd_ref[...]) * x_v).astype(BF16)

    row = lambda b, c: (0, 0)
    blk = pl.BlockSpec((1, tm, W), lambda b, c: (b, c, 0))
    shp = [jax.ShapeDtypeStruct((bsz, t, W), F32 if i == 1 else BF16) for i in range(8)]
    return pl.pallas_call(
        body,
        grid=(bsz, t // tm),
        in_specs=[pl.BlockSpec((1, tm, RW_COLS), lambda b, c: (b, c, 0)),
                  pl.BlockSpec((1, 8, RW_COLS), lambda b, c: (b, jnp.maximum(c * (tm // 8) - 1, 0), 0)),
                  pl.BlockSpec((1, RW_COLS), row),
                  pl.BlockSpec((64, W), row), pl.BlockSpec((1, W), row),
                  pl.BlockSpec((64, W), row), pl.BlockSpec((1, W), row),
                  pl.BlockSpec((128, W), row),
                  pl.BlockSpec((1, W), row), pl.BlockSpec((1, W), row), pl.BlockSpec((1, W), row),
                  pl.BlockSpec((W, W), row)],
        out_specs=[blk] * 8,
        out_shape=shp,
        compiler_params=_cp("parallel", "arbitrary"),
        name="rwkv_prep",
    )(p1, p1, mu, w_up, w0, a_up, a0, g_up, k_k, k_a, r_k, bd)


RW_L = 64
RW_TB = 256


def _rwkv_scan(r, lw, k, v, aa, bb, g_out, bonus, ln_g, ln_b):
    bsz, t, W = r.shape
    L, N, tb = RW_L, RW_DIM, RW_TB
    nck = tb // L

    def body(r_ref, lw_ref, k_ref, v_ref, a_ref, b_ref, g_ref, bonus_ref, lng_ref, lnb_ref, out_ref, h_ref):
        c = pl.program_id(1)

        @pl.when(c == 0)
        def _():
            h_ref[...] = jnp.zeros_like(h_ref)

        lw_all = lw_ref[0]
        rowi = lax.broadcasted_iota(jnp.int32, (tb, 1), 0) & (L - 1)
        cl = lw_all
        for s in (1, 2, 4, 8, 16, 32):
            cl = cl + jnp.where(rowi >= s, pltpu.roll(cl, s, 0), 0.0)
        cl_last = jnp.concatenate([jnp.broadcast_to(cl[(cc + 1) * L - 1:(cc + 1) * L, :], (L, W))
                                   for cc in range(nck)], axis=0)
        e_inv = jnp.exp(-cl)
        e_end = jnp.exp(cl_last - cl)
        p_end = jnp.exp(cl_last)
        at = a_ref[0] * jnp.exp(cl - lw_all)
        rt = r_ref[0] * jnp.exp(cl)
        bt = b_ref[0] * e_inv
        kt = k_ref[0] * e_inv
        b_end = b_ref[0] * e_end
        k_end = k_ref[0] * e_end
        v_all = v_ref[0]
        ri = lax.broadcasted_iota(jnp.int32, (L, L), 0)
        ci = lax.broadcasted_iota(jnp.int32, (L, L), 1)
        strict = ri > ci
        lower = ri >= ci
        eye = ri == ci
        heads = range(nck * RW_HEADS)

        def hs(z, u):
            cc, hh = divmod(u, RW_HEADS)
            return z[cc * L:(cc + 1) * L, hh * N:(hh + 1) * N]

        m = [_dot_nt(jnp.concatenate([hs(at, h), hs(rt, h)], axis=0),
                     jnp.concatenate([hs(bt, h), hs(kt, h)], axis=0)) for h in heads]
        a_ab = [jnp.where(strict, m[h][0:L, 0:L], 0.0) for h in heads]
        a_ak = [jnp.where(strict, m[h][0:L, L:2 * L], 0.0) for h in heads]
        a_rb = [jnp.where(lower, m[h][L:2 * L, 0:L], 0.0) for h in heads]
        a_rk = [jnp.where(lower, m[h][L:2 * L, L:2 * L], 0.0) for h in heads]
        end_t = [jnp.concatenate([hs(b_end, h), hs(k_end, h)], axis=1).T for h in heads]
        vp = [_dot(jnp.concatenate([a_ak[h], a_rk[h], end_t[h][N:2 * N]], axis=0), hs(v_all, h)) for h in heads]
        x = [jnp.concatenate([hs(at, h), vp[h][0:L]], axis=1) for h in heads]
        ap = a_ab
        for it in range(6):
            x = [x[h] + _dot(ap[h], x[h]) for h in heads]
            if it < 5:
                ap = [_dot(ap[h], ap[h]) for h in heads]
        post = [_dot(jnp.concatenate([a_rb[h], end_t[h][0:N]], axis=0), x[h]) for h in heads]
        lhs = []
        y0s = []
        h_adds = []
        for u in heads:
            q_hat = hs(rt, u) + post[u][0:L, 0:N]
            gmat = jnp.where(eye, hs(p_end, u), 0.0) + post[u][L:L + N, 0:N]
            lhs.append(jnp.concatenate([q_hat, gmat], axis=0))
            y0s.append(post[u][0:L, N:2 * N] + vp[u][L:2 * L])
            h_adds.append(post[u][L:L + N, N:2 * N] + vp[u][2 * L:2 * L + N])
        h_st = [h_ref[hh] for hh in range(RW_HEADS)]
        for cc in range(nck):
            outs = []
            for hh in range(RW_HEADS):
                u = cc * RW_HEADS + hh
                res = _dot(lhs[u], h_st[hh])
                y = res[0:L] + y0s[u]
                h_st[hh] = res[L:L + N] + h_adds[u]
                yc = y - jnp.mean(y, axis=1, keepdims=True)
                outs.append(yc * lax.rsqrt(jnp.mean(yc * yc, axis=1, keepdims=True) + RW_LN_EPS))
            yn = jnp.concatenate(outs, axis=1)
            rows = slice(cc * L, (cc + 1) * L)
            out_ref[0, rows, :] = ((yn * lng_ref[...] + lnb_ref[...] + bonus_ref[0, rows, :])
                                   * g_ref[0, rows, :]).astype(BF16)
        for hh in range(RW_HEADS):
            h_ref[hh] = h_st[hh]

    blk = pl.BlockSpec((1, tb, W), lambda b, c: (b, c, 0))
    vec = pl.BlockSpec((1, W), lambda b, c: (0, 0))
    return pl.pallas_call(
        body,
        grid=(bsz, t // tb),
        in_specs=[blk] * 8 + [vec, vec],
        out_specs=blk,
        out_shape=jax.ShapeDtypeStruct((bsz, t, W), BF16),
        scratch_shapes=[pltpu.VMEM((RW_HEADS, N, N), F32)],
        compiler_params=_cp("parallel", "arbitrary"),
        name="rwkv_scan",
    )(r, lw, k, v, aa, bb, g_out, bonus, ln_g, ln_b)


def _router(x2d, g, router_pad, tm=512):
    n, d = x2d.shape
    e = router_pad.shape[1]

    def body(x_ref, g_ref, r_ref, xn_ref, lg_ref):
        xn = _rms_rows(x_ref[...], g_ref[...])
        xn_ref[...] = xn.astype(BF16)
        lg_ref[...] = jnp.dot(xn, r_ref[...], preferred_element_type=F32, precision=lax.Precision.HIGHEST)

    return pl.pallas_call(
        body,
        grid=(n // tm,),
        in_specs=[pl.BlockSpec((tm, d), lambda i: (i, 0)),
                  pl.BlockSpec((1, d), lambda i: (0, 0)),
                  pl.BlockSpec((d, e), lambda i: (0, 0))],
        out_specs=[pl.BlockSpec((tm, d), lambda i: (i, 0)), pl.BlockSpec((tm, e), lambda i: (i, 0))],
        out_shape=[jax.ShapeDtypeStruct((n, d), BF16), jax.ShapeDtypeStruct((n, e), F32)],
        compiler_params=_cp("parallel"),
        name="moe_router",
    )(x2d, g.reshape(1, d), router_pad)


MOE_TM = 512
MOE_TF = 1792


def _experts(xs, row_w, item_tile, item_exp, item_lo, item_hi, wg, wu, wd):
    nrows, d = xs.shape
    tm, tf = MOE_TM, MOE_TF
    nf = D_FF // tf
    n_items = item_tile.shape[0]

    def body(it_ref, ie_ref, lo_ref, hi_ref, x_ref, w_ref, wg_ref, wu_ref, wd_ref, o_ref):
        i = pl.program_id(0)
        j = pl.program_id(1)
        first = jnp.logical_or(i == 0, it_ref[i] != it_ref[jnp.maximum(i - 1, 0)])

        @pl.when(jnp.logical_and(first, j == 0))
        def _():
            o_ref[...] = jnp.zeros_like(o_ref)

        lo = lo_ref[i]
        hi = hi_ref[i]

        @pl.when(lo < hi)
        def _():
            x = x_ref[...]
            gg = jnp.dot(x, wg_ref[0], preferred_element_type=F32)
            uu = jnp.dot(x, wu_ref[0], preferred_element_type=F32)
            act = (_silu(gg) * uu).astype(BF16)
            part = jnp.dot(act, wd_ref[0], preferred_element_type=F32)
            rowi = lax.broadcasted_iota(jnp.int32, (tm, 1), 0)
            mine = jnp.logical_and(rowi >= lo, rowi < hi)
            o_ref[...] += part * jnp.where(mine, w_ref[...], 0.0)

    grid_spec = pltpu.PrefetchScalarGridSpec(
        num_scalar_prefetch=4,
        grid=(n_items, nf),
        in_specs=[pl.BlockSpec((tm, d), lambda i, j, it, ie, lo, hi: (it[i], 0)),
                  pl.BlockSpec((tm, 1), lambda i, j, it, ie, lo, hi: (it[i], 0)),
                  pl.BlockSpec((1, d, tf), lambda i, j, it, ie, lo, hi: (ie[i], 0, j)),
                  pl.BlockSpec((1, d, tf), lambda i, j, it, ie, lo, hi: (ie[i], 0, j)),
                  pl.BlockSpec((1, tf, d), lambda i, j, it, ie, lo, hi: (ie[i], j, 0))],
        out_specs=pl.BlockSpec((tm, d), lambda i, j, it, ie, lo, hi: (it[i], 0)),
    )
    return pl.pallas_call(
        body,
        grid_spec=grid_spec,
        out_shape=jax.ShapeDtypeStruct((nrows, d), F32),
        compiler_params=_cp("arbitrary", "arbitrary"),
        name="moe_experts",
    )(item_tile, item_exp, item_lo, item_hi, xs, row_w, wg, wu, wd)


def _combine_norm(h, y0, y1, g, tm=512):
    n, d = h.shape

    def body(h_ref, a_ref, b_ref, g_ref, o_ref):
        o_ref[...] = _rms_rows(h_ref[...] + (a_ref[...] + b_ref[...]), g_ref[...])

    blk = pl.BlockSpec((tm, d), lambda i: (i, 0))
    return pl.pallas_call(
        body,
        grid=(n // tm,),
        in_specs=[blk, blk, blk, pl.BlockSpec((1, d), lambda i: (0, 0))],
        out_specs=blk,
        out_shape=jax.ShapeDtypeStruct((n, d), F32),
        compiler_params=_cp("parallel"),
        name="combine_norm",
    )(h, y0, y1, g.reshape(1, d))


def _route(logits, n):
    tm = MOE_TM
    na = n * TOP_K
    n_tiles = na // tm
    top_val, top_idx = lax.top_k(logits, TOP_K)
    top_w = jax.nn.softmax(top_val, axis=-1)
    e_flat = top_idx.reshape(-1).astype(jnp.int32)
    w_flat = top_w.reshape(-1)
    tok = jnp.arange(na, dtype=jnp.int32) // TOP_K
    _, sorted_tok, sorted_w = lax.sort((e_flat, tok, w_flat), num_keys=1, is_stable=True)
    onehot = (e_flat[:, None] == jnp.arange(N_EXPERTS, dtype=jnp.int32)[None, :]).astype(jnp.int32)
    rank = jnp.take_along_axis(jnp.cumsum(onehot, axis=0), e_flat[:, None], axis=1)[:, 0] - 1
    counts = jnp.sum(onehot, axis=0)
    ends = jnp.cumsum(counts)
    pos = (ends - counts)[e_flat] + rank
    cuts = jnp.sort(jnp.concatenate([jnp.arange(n_tiles, dtype=jnp.int32) * tm, ends[:-1].astype(jnp.int32)]))
    nxt = jnp.concatenate([cuts[1:], jnp.full((1,), na, jnp.int32)])
    item_tile = jnp.minimum(cuts // tm, n_tiles - 1)
    item_exp = jnp.minimum(jnp.searchsorted(ends, cuts, side="right"), N_EXPERTS - 1).astype(jnp.int32)
    item_lo = cuts - item_tile * tm
    item_hi = nxt - item_tile * tm
    return sorted_tok, sorted_w, (item_tile, item_exp, item_lo, item_hi), pos.reshape(n, TOP_K)


def kernel(x, e_norm1_g, e_w_in, e_ml_conv_w, e_ml_conv_b, e_ml_gate_b, e_ml_norm_g, e_gla_gate_up, e_gla_gate_b,
           e_gla_norm_g, e_w_out, e_norm2_g, e_ffn_w_gate, e_ffn_w_up, e_ffn_w_down, o_norm1_g, o_w_in,
           o_ret_norm_g, o_rw_mu, o_rw_w_up, o_rw_w0, o_rw_a_up, o_rw_a0, o_rw_g_up, o_rw_k_k, o_rw_k_a, o_rw_r_k,
           o_rw_ln_g, o_rw_ln_b, o_w_out, o_norm2_g, o_moe_router, o_moe_w_gate, o_moe_w_up, o_moe_w_down,
           final_norm_g):
    bsz, t, d = x.shape
    n = bsz * t
    h0 = x.reshape(n, d)

    w = e_w_in[0]
    w0 = jnp.concatenate([w[:, :2048], w[:, 2056:3592], w[:, 2048:2056], w[:, 3592:3608],
                          jnp.zeros((d, E_COLS_PAD - 3608), F32)], axis=1).astype(BF16)
    p0 = _norm_matmul(h0, e_norm1_g[0], w0).reshape(bsz, t, E_COLS_PAD)
    gates_col = p0[:, :, 3584:3592]
    gates_row = jnp.transpose(gates_col, (0, 2, 1))
    h_ml = _mlstm(p0, gates_col, gates_row, e_ml_gate_b[0].reshape(1, -1), e_ml_gate_b[0].reshape(-1, 1),
                  e_ml_conv_w[0], e_ml_conv_b[0].reshape(1, -1), e_ml_norm_g[0].reshape(1, -1))
    g_low = p0[:, :, 3592:3608]
    o_gla = _gla(p0, g_low, e_gla_gate_up[0].astype(BF16), e_gla_gate_b[0].reshape(1, -1),
                 e_gla_norm_g[0].reshape(1, -1))
    h1 = _mix_out(h_ml.reshape(n, -1), o_gla.reshape(n, -1), e_w_out[0].astype(BF16), h0)
    h2 = _ffn(h1, e_norm2_g[0], e_ffn_w_gate[0].astype(BF16), e_ffn_w_up[0].astype(BF16),
              e_ffn_w_down[0].astype(BF16))

    w = o_w_in[0]
    p_ret = _norm_matmul(h2, o_norm1_g[0], w[:, :4 * RET_W].astype(BF16), tn=1024).reshape(bsz, t, -1)
    p1 = _norm_matmul(h2, o_norm1_g[0], w[:, 4 * RET_W:].astype(BF16), tn=896).reshape(bsz, t, -1)
    y_ret = _retention(p_ret, *_retention_tables(t), o_ret_norm_g[0].reshape(1, -1))
    head_of = jnp.arange(RW_W) // RW_DIM
    bd = (head_of[:, None] == head_of[None, :]).astype(BF16)
    row = lambda a: a.reshape(1, -1)
    r, lw, k, v, aa, bb, g_out, bonus = _rwkv_prep(
        p1, row(o_rw_mu[0]), o_rw_w_up[0].astype(BF16), row(o_rw_w0[0]), o_rw_a_up[0].astype(BF16),
        row(o_rw_a0[0]), o_rw_g_up[0].astype(BF16), row(o_rw_k_k[0]), row(o_rw_k_a[0]), row(o_rw_r_k[0]), bd)
    y_rw = _rwkv_scan(r, lw, k, v, aa, bb, g_out, bonus, row(o_rw_ln_g[0]), row(o_rw_ln_b[0]))
    h3 = _mix_out(y_ret.reshape(n, -1), y_rw.reshape(n, -1), o_w_out[0].astype(BF16), h2)

    router_pad = jnp.zeros((d, 128), F32).at[:, :N_EXPERTS].set(o_moe_router[0])
    xn, logits = _router(h3, o_norm2_g[0], router_pad)
    sorted_tok, sorted_w, items, pos = _route(logits[:, :N_EXPERTS], n)
    xs = xn.at[sorted_tok].get(mode="promise_in_bounds")
    ys = _experts(xs, sorted_w.reshape(-1, 1), *items, o_moe_w_gate[0].astype(BF16),
                  o_moe_w_up[0].astype(BF16), o_moe_w_down[0].astype(BF16))
    y0 = ys.at[pos[:, 0]].get(mode="promise_in_bounds")
    y1 = ys.at[pos[:, 1]].get(mode="promise_in_bounds")
    out = _combine_norm(h3, y0, y1, final_norm_g)
    return out.reshape(bsz, t, d)
```

```python
import functools

import numpy as np
import jax
import jax.numpy as jnp
from jax import lax
from jax.experimental import pallas as pl
from jax.experimental.pallas import tpu as pltpu

F32 = jnp.float32
BF16 = jnp.bfloat16

D_MODEL = 1024
EPS = 1e-6
ML_HEADS, ML_DIM, ML_W, ML_CONV = 4, 128, 512, 4
GLA_HEADS, GLA_DK, GLA_DV, GLA_RANK, GLA_TAU = 4, 64, 128, 16, 16.0
RET_HEADS, RET_DIM, RET_W = 4, 128, 512
ROPE_BASE = 10000.0
RW_HEADS, RW_DIM, RW_W = 8, 64, 512
RW_COLS = 1792
RW_LN_EPS = 64e-5
D_FF = 3584
N_EXPERTS = 8
TOP_K = 2

VMEM_LIMIT = 48 * 1024 * 1024
NEG = -1e30


def _cp(*sem):
    return pltpu.CompilerParams(dimension_semantics=sem, vmem_limit_bytes=VMEM_LIMIT)


def _sigmoid(x):
    return 1.0 / (1.0 + jnp.exp(-x))


def _silu(x):
    return x * _sigmoid(x)


def _log_sigmoid(x):
    return jnp.minimum(x, 0.0) - jnp.log(1.0 + jnp.exp(-jnp.abs(x)))


def _dot(a, b):
    return jnp.dot(a.astype(BF16), b.astype(BF16), preferred_element_type=F32)


def _dot_nt(a, b):
    return lax.dot_general(a.astype(BF16), b.astype(BF16), (((1,), (1,)), ((), ())), preferred_element_type=F32)


def _dot_tn(a, b):
    return jnp.dot(a.T.astype(BF16), b.astype(BF16), preferred_element_type=F32)


def _rms_rows(x, g):
    ms = jnp.mean(x * x, axis=-1, keepdims=True)
    return x * lax.rsqrt(ms + EPS) * g


def _mix_out(a, b, w_bf16, resid, tm=1024):
    n, wa = a.shape
    wb = b.shape[1]
    d = w_bf16.shape[1]

    def body(a_ref, b_ref, w_ref, r_ref, o_ref):
        acc = jnp.dot(a_ref[...], w_ref[0:wa, :], preferred_element_type=F32)
        acc = acc + jnp.dot(b_ref[...], w_ref[wa:wa + wb, :], preferred_element_type=F32)
        o_ref[...] = r_ref[...] + acc

    return pl.pallas_call(
        body,
        grid=(n // tm,),
        in_specs=[pl.BlockSpec((tm, wa), lambda i: (i, 0)),
                  pl.BlockSpec((tm, wb), lambda i: (i, 0)),
                  pl.BlockSpec((wa + wb, d), lambda i: (0, 0)),
                  pl.BlockSpec((tm, d), lambda i: (i, 0))],
        out_specs=pl.BlockSpec((tm, d), lambda i: (i, 0)),
        out_shape=jax.ShapeDtypeStruct((n, d), F32),
        compiler_params=_cp("parallel"),
        name="mix_out",
    )(a, b, w_bf16, resid)


def _ffn(x2d, g, wg, wu, wd, tm=512, tf=1792):
    n, d = x2d.shape
    f = wg.shape[1]
    nf = f // tf

    def body(x_ref, g_ref, wg_ref, wu_ref, wd_ref, o_ref, xn_ref, acc_ref):
        j = pl.program_id(1)

        @pl.when(j == 0)
        def _():
            xn_ref[...] = _rms_rows(x_ref[...], g_ref[...]).astype(BF16)
            acc_ref[...] = jnp.zeros_like(acc_ref)

        xn = xn_ref[...]
        gg = jnp.dot(xn, wg_ref[...], preferred_element_type=F32)
        uu = jnp.dot(xn, wu_ref[...], preferred_element_type=F32)
        act = (_silu(gg) * uu).astype(BF16)
        acc_ref[...] += jnp.dot(act, wd_ref[...], preferred_element_type=F32)

        @pl.when(j == nf - 1)
        def _():
            o_ref[...] = x_ref[...] + acc_ref[...]

    return pl.pallas_call(
        body,
        grid=(n // tm, nf),
        in_specs=[pl.BlockSpec((tm, d), lambda i, j: (i, 0)),
                  pl.BlockSpec((1, d), lambda i, j: (0, 0)),
                  pl.BlockSpec((d, tf), lambda i, j: (0, j)),
                  pl.BlockSpec((d, tf), lambda i, j: (0, j)),
                  pl.BlockSpec((tf, d), lambda i, j: (j, 0))],
        out_specs=pl.BlockSpec((tm, d), lambda i, j: (i, 0)),
        out_shape=jax.ShapeDtypeStruct((n, d), F32),
        scratch_shapes=[pltpu.VMEM((tm, d), BF16), pltpu.VMEM((tm, d), F32)],
        compiler_params=_cp("parallel", "arbitrary"),
        name="ffn",
    )(x2d, g.reshape(1, d), wg, wu, wd)


ML_L = 256


def _mlstm(x, in_g, w_in, w_gate, w_gate_t, gate_b_row, gate_b_col, conv_w, conv_b, norm_g):
    bsz, t, d = x.shape
    L, D, H, W = ML_L, ML_DIM, ML_HEADS, ML_W

    def body(x_ref, ing_ref, w_ref, wg_ref, wgt_ref, gbr_ref, gbc_ref, cwq_ref, cwk_ref, cbq_ref, cbk_ref,
             ng_ref, out_ref, qext, kext, c_ref, n_ref, m_ref):
        c = pl.program_id(1)

        @pl.when(c == 0)
        def _():
            qext[0:8, :] = jnp.zeros((8, W), F32)
            kext[0:8, :] = jnp.zeros((8, W), F32)
            c_ref[...] = jnp.zeros_like(c_ref)
            n_ref[...] = jnp.zeros_like(n_ref)
            m_ref[...] = jnp.zeros_like(m_ref)

        xn = _rms_rows(x_ref[0], ing_ref[...]).astype(BF16)
        p = jnp.dot(xn, w_ref[...], preferred_element_type=F32)
        g_cols = jnp.dot(xn, wg_ref[...], preferred_element_type=F32)[:, 0:2 * H]
        g_rows = lax.dot_general(wgt_ref[...], xn, (((1,), (1,)), ((), ())),
                                 preferred_element_type=F32)[0:2 * H, :]
        qext[8:, :] = p[:, 0:W]
        kext[8:, :] = p[:, W:2 * W]

        def conv(ext, cw_ref, cb_ref):
            acc = cb_ref[...] + cw_ref[0:1, :] * ext[pl.ds(8 - ML_CONV + 1, L), :]
            for kk in range(1, ML_CONV):
                acc = acc + cw_ref[kk:kk + 1, :] * ext[pl.ds(8 - ML_CONV + 1 + kk, L), :]
            return _silu(acc)

        q_all = conv(qext, cwq_ref, cbq_ref) * (D ** -0.5)
        k_all = conv(kext, cwk_ref, cbk_ref)
        qext[0:8, :] = qext[L:L + 8, :]
        kext[0:8, :] = kext[L:L + 8, :]

        gcol = g_cols + gbr_ref[...]
        grow = g_rows + gbc_ref[...]
        fcol = _log_sigmoid(gcol[:, H:2 * H])
        frow = _log_sigmoid(grow[H:2 * H, :])
        ri = lax.broadcasted_iota(jnp.int32, (L, L), 0)
        ci = lax.broadcasted_iota(jnp.int32, (L, L), 1)
        causal = ri >= ci
        heads = range(H)
        hs = lambda z, h: z[:, h * D:(h + 1) * D]
        w_intra, w_inter, w_state, carry, m_row, m_new = [], [], [], [], [], []
        for h in heads:
            f_row = frow[h:h + 1, :]
            i_row = grow[h:h + 1, :]
            f_col = fcol[:, h:h + 1]
            i_col = gcol[:, h:h + 1]
            m_st = m_ref[h]
            b_col = jnp.sum(jnp.where(causal, f_row, 0.0), axis=1, keepdims=True)
            b_row = jnp.sum(jnp.where(ri <= ci, f_col, 0.0), axis=0, keepdims=True)
            g_tot = jnp.sum(f_row, axis=1, keepdims=True)
            d_intra = jnp.where(causal, b_col - b_row + i_row, NEG)
            d_inter = b_col + m_st
            mr = jnp.maximum(d_inter, jnp.max(d_intra, axis=1, keepdims=True))
            w_intra.append(jnp.exp(d_intra - mr))
            w_inter.append(jnp.exp(d_inter - mr))
            m_row.append(mr)
            d_state = g_tot - b_col + i_col
            mn = jnp.maximum(g_tot + m_st, jnp.max(d_state, axis=0, keepdims=True))
            w_state.append(jnp.exp(d_state - mn))
            carry.append(jnp.exp(g_tot + m_st - mn))
            m_new.append(mn)
        qh = [hs(q_all, h) for h in heads]
        kh = [hs(k_all, h) for h in heads]
        vh = [p[:, 2 * W + h * D:2 * W + (h + 1) * D] for h in heads]
        c_st = [c_ref[h] for h in heads]
        n_st = [n_ref[h] for h in heads]
        s = [_dot_nt(qh[h], kh[h]) * w_intra[h] for h in heads]
        qc = [_dot(qh[h], c_st[h]) for h in heads]
        kw = [kh[h] * w_state[h] for h in heads]
        kv = [_dot_tn(kw[h], vh[h]) for h in heads]
        sv = [_dot(s[h], vh[h]) for h in heads]
        for h in heads:
            num = sv[h] + w_inter[h] * qc[h]
            den = jnp.sum(s[h], axis=1, keepdims=True) + w_inter[h] * jnp.sum(qh[h] * n_st[h], axis=1, keepdims=True)
            hval = num / jnp.maximum(jnp.abs(den), jnp.exp(-m_row[h]))
            c_ref[h] = carry[h] * c_st[h] + kv[h]
            n_ref[h] = carry[h] * n_st[h] + jnp.sum(kw[h], axis=0, keepdims=True)
            m_ref[h] = m_new[h]
            hg = _sigmoid(p[:, 3 * W + h * D:3 * W + (h + 1) * D]) * hval
            hc = hg - jnp.mean(hg, axis=1, keepdims=True)
            hn = hc * lax.rsqrt(jnp.mean(hc * hc, axis=1, keepdims=True) + EPS)
            out_ref[0, :, h * D:(h + 1) * D] = (hn * ng_ref[:, h * D:(h + 1) * D]).astype(BF16)

    fix = lambda j: (lambda b, c: (0, j))
    return pl.pallas_call(
        body,
        grid=(bsz, t // L),
        in_specs=[pl.BlockSpec((1, L, d), lambda b, c: (b, c, 0)),
                  pl.BlockSpec((1, d), fix(0)),
                  pl.BlockSpec((d, 4 * W), fix(0)),
                  pl.BlockSpec((d, 128), fix(0)),
                  pl.BlockSpec((16, d), fix(0)),
                  pl.BlockSpec((1, 2 * H), fix(0)),
                  pl.BlockSpec((2 * H, 1), fix(0)),
                  pl.BlockSpec((ML_CONV, W), fix(0)),
                  pl.BlockSpec((ML_CONV, W), fix(1)),
                  pl.BlockSpec((1, W), fix(0)),
                  pl.BlockSpec((1, W), fix(1)),
                  pl.BlockSpec((1, W), fix(0))],
        out_specs=pl.BlockSpec((1, L, W), lambda b, c: (b, c, 0)),
        out_shape=jax.ShapeDtypeStruct((bsz, t, W), BF16),
        scratch_shapes=[pltpu.VMEM((L + 8, W), F32), pltpu.VMEM((L + 8, W), F32),
                        pltpu.VMEM((H, D, D), F32), pltpu.VMEM((H, 1, D), F32), pltpu.VMEM((H, 1, 1), F32)],
        compiler_params=_cp("parallel", "arbitrary"),
        name="mlstm",
    )(x, in_g, w_in, w_gate, w_gate_t, gate_b_row, gate_b_col, conv_w, conv_w, conv_b, conv_b, norm_g)


GLA_TC = 256
GLA_SUB = 16
GLA_GROUP = 128


def _gla(x, in_g, w_pairs, w_low, gate_up, gate_b, norm_g):
    bsz, t, d = x.shape
    tc, S, GB = GLA_TC, GLA_SUB, GLA_GROUP
    head_ones = (jnp.arange(2 * GLA_DK)[:, None] // GLA_DK == jnp.arange(2 * GB)[None, :] // GB).astype(BF16)
    nsub = tc // S
    dk, dv = GLA_DK, GLA_DV

    def body(x_ref, ing_ref, w_ref, wl_ref, gu_ref, gbias_ref, ng_ref, ones_ref, out_ref, st_ref, ksh, bsh):
        c = pl.program_id(2)

        @pl.when(c == 0)
        def _():
            st_ref[...] = jnp.zeros_like(st_ref)
            ksh[0:S, :] = jnp.zeros((S, 2 * dk), F32)
            bsh[0:S, :] = jnp.zeros((S, 2 * dk), F32)

        xn = _rms_rows(x_ref[0], ing_ref[...]).astype(BF16)
        p = jnp.dot(xn, w_ref[0], preferred_element_type=F32)
        g_low = jnp.dot(xn, wl_ref[...], preferred_element_type=F32)[:, 0:GLA_RANK]
        z = _dot(g_low, gu_ref[...]) + gbias_ref[...]
        la = _log_sigmoid(z) / GLA_TAU
        rowi = lax.broadcasted_iota(jnp.int32, (tc, 1), 0)
        rmod = rowi & (S - 1)
        bcum = la
        rsum = la
        for s in (1, 2, 4, 8):
            bcum = bcum + jnp.where(rmod >= s, pltpu.roll(bcum, s, 0), 0.0)
            rsum = rsum + jnp.where(rmod < S - s, pltpu.roll(rsum, tc - s, 0), 0.0)
        q = p[:, 0:2 * dk] * (dk ** -0.5)
        k = p[:, 2 * dk:4 * dk]
        v = p[:, 4 * dk:4 * dk + 2 * dv]
        gate = p[:, 4 * dk + 2 * dv:4 * dk + 4 * dv]
        qt = q * jnp.exp(bcum)
        kt = k * jnp.exp(rsum - la)
        eg = jnp.exp(bcum + rsum - la)

        ksh[S:, :] = k
        bsh[S:, :] = bcum
        prods = []
        for d in range(S):
            kd = k if d == 0 else ksh[pl.ds(S - d, tc), :]
            bd = bcum if d == 0 else bsh[pl.ds(S - d, tc), :]
            e = jnp.exp(jnp.where(rmod >= d, bcum - bd, 0.0))
            prods.append((q * kd * e).astype(BF16))
        ws = [jnp.dot(p, ones_ref[...], preferred_element_type=F32) for p in prods]
        coli = lax.broadcasted_iota(jnp.int32, (tc, GB), 1)
        rgrp = rowi & (GB - 1)
        att0 = jnp.zeros((tc, GB), F32)
        att1 = jnp.zeros((tc, GB), F32)
        for d in range(S):
            here = jnp.logical_and(coli == rgrp - d, rmod >= d)
            att0 = jnp.where(here, ws[d][:, 0:GB], att0)
            att1 = jnp.where(here, ws[d][:, GB:2 * GB], att1)

        heads = range(2)
        lk = [slice(hh * dk, (hh + 1) * dk) for hh in heads]
        lv = [slice(hh * dv, (hh + 1) * dv) for hh in heads]
        kv = [[_dot_tn(v[si * S:(si + 1) * S, lv[hh]], kt[si * S:(si + 1) * S, lk[hh]]) for hh in heads]
              for si in range(nsub)]
        st = [st_ref[hh] for hh in heads]
        inter = [[], []]
        for si in range(nsub):
            rows = slice(si * S, (si + 1) * S)
            for hh in heads:
                inter[hh].append(_dot_nt(qt[rows, lk[hh]], st[hh]))
                st[hh] = st[hh] * eg[si * S:si * S + 1, lk[hh]] + kv[si][hh]
        for hh in heads:
            st_ref[hh] = st[hh]

        for hh, att in ((0, att0), (1, att1)):
            diag = jnp.concatenate([_dot(att[g * GB:(g + 1) * GB], v[g * GB:(g + 1) * GB, lv[hh]])
                                    for g in range(tc // GB)], axis=0)
            o = diag + jnp.concatenate(inter[hh], axis=0)
            on = o * lax.rsqrt(jnp.mean(o * o, axis=1, keepdims=True) + EPS)
            out_ref[0, :, lv[hh]] = (on * ng_ref[:, lv[hh]] * _silu(gate[:, lv[hh]])).astype(BF16)

    pw = 4 * dk + 4 * dv
    return pl.pallas_call(
        body,
        grid=(bsz, GLA_HEADS // 2, t // tc),
        in_specs=[pl.BlockSpec((1, tc, d), lambda b, h, c: (b, c, 0)),
                  pl.BlockSpec((1, d), lambda b, h, c: (0, 0)),
                  pl.BlockSpec((1, d, pw), lambda b, h, c: (h, 0, 0)),
                  pl.BlockSpec((d, 128), lambda b, h, c: (0, 0)),
                  pl.BlockSpec((GLA_RANK, 2 * dk), lambda b, h, c: (0, h)),
                  pl.BlockSpec((1, 2 * dk), lambda b, h, c: (0, h)),
                  pl.BlockSpec((1, 2 * dv), lambda b, h, c: (0, h)),
                  pl.BlockSpec((2 * dk, 2 * GB), lambda b, h, c: (0, 0))],
        out_specs=pl.BlockSpec((1, tc, 2 * dv), lambda b, h, c: (b, c, h)),
        out_shape=jax.ShapeDtypeStruct((bsz, t, GLA_HEADS * dv), BF16),
        scratch_shapes=[pltpu.VMEM((2, dv, dk), F32), pltpu.VMEM((tc + S, 2 * dk), F32),
                        pltpu.VMEM((tc + S, 2 * dk), F32)],
        compiler_params=_cp("parallel", "parallel", "arbitrary"),
        name="gla",
    )(x, in_g, w_pairs, w_low, gate_up, gate_b, norm_g, head_ones)


RET_L = 256


def _retention(x, in_g, w_in, cos_t, sin_t, intra, inter, sdec, cdec, norm_g):
    bsz, t, d = x.shape
    L, D, H, W = RET_L, RET_DIM, RET_HEADS, RET_W

    def body(x_ref, ing_ref, w_ref, cos_ref, sin_ref, intra_ref, inter_ref, sdec_ref, cdec_ref, ng_ref,
             out_ref, s_ref):
        c = pl.program_id(1)

        @pl.when(c == 0)
        def _():
            s_ref[...] = jnp.zeros_like(s_ref)

        p = jnp.dot(_rms_rows(x_ref[0], ing_ref[...]).astype(BF16), w_ref[...], preferred_element_type=F32)
        cs = cos_ref[...]
        sn = sin_ref[...]

        def rot(z):
            return z * cs + pltpu.roll(z, D // 2, 1) * sn

        heads = range(H)
        q = [rot(p[:, h * D:(h + 1) * D]) * (D ** -0.5) for h in heads]
        k = [rot(p[:, W + h * D:W + (h + 1) * D]) for h in heads]
        v = [p[:, 2 * W + h * D:2 * W + (h + 1) * D] for h in heads]
        s_st = [s_ref[h] for h in heads]
        s = [_dot_nt(q[h], k[h]) * intra_ref[h] for h in heads]
        qs = [_dot(q[h], s_st[h]) for h in heads]
        kv = [_dot_tn(k[h] * sdec_ref[h], v[h]) for h in heads]
        sv = [_dot(s[h], v[h]) for h in heads]
        for h in heads:
            o = sv[h] + inter_ref[h] * qs[h]
            s_ref[h] = cdec_ref[h] * s_st[h] + kv[h]
            oc = o - jnp.mean(o, axis=1, keepdims=True)
            on = oc * lax.rsqrt(jnp.mean(oc * oc, axis=1, keepdims=True) + EPS)
            hl = slice(h * D, (h + 1) * D)
            gate = p[:, 3 * W + h * D:3 * W + (h + 1) * D]
            out_ref[0, :, hl] = (on * ng_ref[:, hl] * _silu(gate)).astype(BF16)

    fix2 = lambda b, c: (0, 0)
    fix3 = lambda b, c: (0, 0, 0)
    return pl.pallas_call(
        body,
        grid=(bsz, t // L),
        in_specs=[pl.BlockSpec((1, L, d), lambda b, c: (b, c, 0)),
                  pl.BlockSpec((1, d), fix2),
                  pl.BlockSpec((d, 4 * W), fix2),
                  pl.BlockSpec((L, D), lambda b, c: (c, 0)),
                  pl.BlockSpec((L, D), lambda b, c: (c, 0)),
                  pl.BlockSpec((H, L, L), fix3),
                  pl.BlockSpec((H, L, 1), fix3),
                  pl.BlockSpec((H, L, 1), fix3),
                  pl.BlockSpec((H, 1, 1), fix3),
                  pl.BlockSpec((1, W), lambda b, c: (0, 0))],
        out_specs=pl.BlockSpec((1, L, W), lambda b, c: (b, c, 0)),
        out_shape=jax.ShapeDtypeStruct((bsz, t, W), BF16),
        scratch_shapes=[pltpu.VMEM((H, D, D), F32)],
        compiler_params=_cp("parallel", "arbitrary"),
        name="retention",
    )(x, in_g, w_in, cos_t, sin_t, intra, inter, sdec, cdec, norm_g)


def _retention_tables(t):
    L, D = RET_L, RET_DIM
    inv = ROPE_BASE ** (-jnp.arange(0, D, 2, dtype=F32) / D)
    ang = jnp.arange(t).astype(F32)[:, None] * inv[None, :]
    cos = jnp.cos(ang)
    sin = jnp.sin(ang)
    cos_t = jnp.concatenate([cos, cos], axis=-1)
    sin_t = jnp.concatenate([-sin, sin], axis=-1)
    log_gamma = jnp.log1p(-jnp.exp2(-5.0 - jnp.arange(RET_HEADS, dtype=F32)))
    idx = jnp.arange(L, dtype=F32)
    causal = idx[:, None] >= idx[None, :]
    rel = jnp.where(causal, idx[:, None] - idx[None, :], 0.0)
    intra = jnp.where(causal, jnp.exp(log_gamma[:, None, None] * rel), 0.0)
    inter = jnp.exp(log_gamma[:, None] * (idx + 1.0))[:, :, None]
    sdec = jnp.exp(log_gamma[:, None] * (L - 1.0 - idx))[:, :, None]
    cdec = jnp.exp(log_gamma * L)[:, None, None]
    return cos_t, sin_t, intra, inter, sdec, cdec


RWP_TM = 256


def _seg_sum(x, bd):
    hi = x.astype(BF16)
    lo = (x - hi.astype(F32)).astype(BF16)
    return jnp.dot(hi, bd, preferred_element_type=F32) + jnp.dot(lo, bd, preferred_element_type=F32)


def _rwkv_prep(x, in_g, w_in, mu, w_up, w0, a_up, a0, g_up, k_k, k_a, r_k, bd):
    bsz, t, d = x.shape
    tm, W = RWP_TM, RW_W

    def body(x_ref, ing_ref, win_ref, mu_ref, wup_ref, w0_ref, aup_ref, a0_ref, gup_ref, kk_ref, ka_ref, rk_ref,
             bd_ref, r_out, lw_out, k_out, v_out, a_out, b_out, g_out, bonus_out, last_ref):
        c = pl.program_id(1)

        @pl.when(c == 0)
        def _():
            last_ref[...] = jnp.zeros_like(last_ref)

        cur = jnp.dot(_rms_rows(x_ref[0], ing_ref[...]).astype(BF16), win_ref[...], preferred_element_type=F32)
        prev = last_ref[...]
        last_ref[...] = cur[tm - 1:tm, :]
        rowi = lax.broadcasted_iota(jnp.int32, (tm, 1), 0)
        sh = jnp.where(rowi == 0, prev, pltpu.roll(cur, 1, 0))
        xm = cur + (sh - cur) * mu_ref[...]
        x_r = xm[:, 0:W]
        x_k = xm[:, W:2 * W]
        x_v = xm[:, 2 * W:3 * W]
        x_dl = xm[:, 3 * W:3 * W + 64]
        x_al = xm[:, 3 * W + 64:3 * W + 128]
        x_gl = xm[:, 3 * W + 128:3 * W + 256]
        wl = w0_ref[...] + _dot(jnp.tanh(x_dl), wup_ref[...])
        sp = jnp.maximum(-wl, 0.0) + jnp.log(1.0 + jnp.exp(-jnp.abs(wl)))
        lw_out[0] = -jnp.exp(-sp - 0.5)
        a = _sigmoid(a0_ref[...] + _dot(x_al, aup_ref[...]))
        g_out[0] = _dot(_sigmoid(x_gl), gup_ref[...]).astype(BF16)
        kk0 = x_k * kk_ref[...]
        nrm = jnp.sqrt(_seg_sum(kk0 * kk0, bd_ref[...]))
        kk = kk0 / jnp.maximum(nrm, 1e-12)
        k_h = x_k * (1.0 + (a - 1.0) * ka_ref[...])
        r_out[0] = x_r.astype(BF16)
        k_out[0] = k_h.astype(BF16)
        v_out[0] = x_v.astype(BF16)
        a_out[0] = (-kk).astype(BF16)
        b_out[0] = (kk * a).astype(BF16)
        bonus_out[0] = (_seg_sum(x_r * k_h * rk_ref[...], bd_ref[...]) * x_v).astype(BF16)

    row = lambda b, c: (0, 0)
    blk = pl.BlockSpec((1, tm, W), lambda b, c: (b, c, 0))
    shp = [jax.ShapeDtypeStruct((bsz, t, W), F32 if i == 1 else BF16) for i in range(8)]
    return pl.pallas_call(
        body,
        grid=(bsz, t // tm),
        in_specs=[pl.BlockSpec((1, tm, d), lambda b, c: (b, c, 0)),
                  pl.BlockSpec((1, d), row),
                  pl.BlockSpec((d, RW_COLS), row),
                  pl.BlockSpec((1, RW_COLS), row),
                  pl.BlockSpec((64, W), row), pl.BlockSpec((1, W), row),
                  pl.BlockSpec((64, W), row), pl.BlockSpec((1, W), row),
                  pl.BlockSpec((128, W), row),
                  pl.BlockSpec((1, W), row), pl.BlockSpec((1, W), row), pl.BlockSpec((1, W), row),
                  pl.BlockSpec((W, W), row)],
        out_specs=[blk] * 8,
        out_shape=shp,
        scratch_shapes=[pltpu.VMEM((1, RW_COLS), F32)],
        compiler_params=_cp("parallel", "arbitrary"),
        name="rwkv_prep",
    )(x, in_g, w_in, mu, w_up, w0, a_up, a0, g_up, k_k, k_a, r_k, bd)


RW_L = 64
RW_TB = 128


def _rwkv_scan(r, lw, k, v, aa, bb, g_out, bonus, ln_g, ln_b):
    bsz, t, W = r.shape
    L, N, tb = RW_L, RW_DIM, RW_TB
    nck = tb // L

    def body(r_ref, lw_ref, k_ref, v_ref, a_ref, b_ref, g_ref, bonus_ref, lng_ref, lnb_ref, out_ref, h_ref):
        c = pl.program_id(1)

        @pl.when(c == 0)
        def _():
            h_ref[...] = jnp.zeros_like(h_ref)

        lw_all = lw_ref[0]
        rowi = lax.broadcasted_iota(jnp.int32, (tb, 1), 0) & (L - 1)
        cl = lw_all
        for s in (1, 2, 4, 8, 16, 32):
            cl = cl + jnp.where(rowi >= s, pltpu.roll(cl, s, 0), 0.0)
        cl_last = jnp.concatenate([jnp.broadcast_to(cl[(cc + 1) * L - 1:(cc + 1) * L, :], (L, W))
                                   for cc in range(nck)], axis=0)
        e_inv = jnp.exp(-cl)
        e_end = jnp.exp(cl_last - cl)
        p_end = jnp.exp(cl_last)
        at = a_ref[0] * jnp.exp(cl - lw_all)
        rt = r_ref[0] * jnp.exp(cl)
        bt = b_ref[0] * e_inv
        kt = k_ref[0] * e_inv
        b_end = b_ref[0] * e_end
        k_end = k_ref[0] * e_end
        v_all = v_ref[0]
        ri = lax.broadcasted_iota(jnp.int32, (L, L), 0)
        ci = lax.broadcasted_iota(jnp.int32, (L, L), 1)
        strict = ri > ci
        lower = ri >= ci
        eye = ri == ci
        heads = range(nck * RW_HEADS)

        def hs(z, u):
            cc, hh = divmod(u, RW_HEADS)
            return z[cc * L:(cc + 1) * L, hh * N:(hh + 1) * N]

        m = [_dot_nt(jnp.concatenate([hs(at, h), hs(rt, h)], axis=0),
                     jnp.concatenate([hs(bt, h), hs(kt, h)], axis=0)) for h in heads]
        a_ab = [jnp.where(strict, m[h][0:L, 0:L], 0.0) for h in heads]
        a_ak = [jnp.where(strict, m[h][0:L, L:2 * L], 0.0) for h in heads]
        a_rb = [jnp.where(lower, m[h][L:2 * L, 0:L], 0.0) for h in heads]
        a_rk = [jnp.where(lower, m[h][L:2 * L, L:2 * L], 0.0) for h in heads]
        end_t = [jnp.concatenate([hs(b_end, h), hs(k_end, h)], axis=1).T for h in heads]
        vp = [_dot(jnp.concatenate([a_ak[h], a_rk[h], end_t[h][N:2 * N]], axis=0), hs(v_all, h)) for h in heads]
        x = [jnp.concatenate([hs(at, h), vp[h][0:L]], axis=1) for h in heads]
        ap = a_ab
        for it in range(6):
            x = [x[h] + _dot(ap[h], x[h]) for h in heads]
            if it < 5:
                ap = [_dot(ap[h], ap[h]) for h in heads]
        post = [_dot(jnp.concatenate([a_rb[h], end_t[h][0:N]], axis=0), x[h]) for h in heads]
        lhs = []
        y0s = []
        h_adds = []
        for u in heads:
            q_hat = hs(rt, u) + post[u][0:L, 0:N]
            gmat = jnp.where(eye, hs(p_end, u), 0.0) + post[u][L:L + N, 0:N]
            lhs.append(jnp.concatenate([q_hat, gmat], axis=0))
            y0s.append(post[u][0:L, N:2 * N] + vp[u][L:2 * L])
            h_adds.append(post[u][L:L + N, N:2 * N] + vp[u][2 * L:2 * L + N])
        h_st = [h_ref[hh] for hh in range(RW_HEADS)]
        for cc in range(nck):
            outs = []
            for hh in range(RW_HEADS):
                u = cc * RW_HEADS + hh
                res = _dot(lhs[u], h_st[hh])
                y = res[0:L] + y0s[u]
                h_st[hh] = res[L:L + N] + h_adds[u]
                yc = y - jnp.mean(y, axis=1, keepdims=True)
                outs.append(yc * lax.rsqrt(jnp.mean(yc * yc, axis=1, keepdims=True) + RW_LN_EPS))
            yn = jnp.concatenate(outs, axis=1)
            rows = slice(cc * L, (cc + 1) * L)
            out_ref[0, rows, :] = ((yn * lng_ref[...] + lnb_ref[...] + bonus_ref[0, rows, :])
                                   * g_ref[0, rows, :]).astype(BF16)
        for hh in range(RW_HEADS):
            h_ref[hh] = h_st[hh]

    blk = pl.BlockSpec((1, tb, W), lambda b, c: (b, c, 0))
    vec = pl.BlockSpec((1, W), lambda b, c: (0, 0))
    return pl.pallas_call(
        body,
        grid=(bsz, t // tb),
        in_specs=[blk] * 8 + [vec, vec],
        out_specs=blk,
        out_shape=jax.ShapeDtypeStruct((bsz, t, W), BF16),
        scratch_shapes=[pltpu.VMEM((RW_HEADS, N, N), F32)],
        compiler_params=_cp("parallel", "arbitrary"),
        name="rwkv_scan",
    )(r, lw, k, v, aa, bb, g_out, bonus, ln_g, ln_b)


def _router(x2d, g, router_pad, tm=512):
    n, d = x2d.shape
    e = router_pad.shape[1]

    def body(x_ref, g_ref, r_ref, xn_ref, lg_ref):
        xn = _rms_rows(x_ref[...], g_ref[...])
        xn_ref[...] = xn.astype(BF16)
        lg_ref[...] = jnp.dot(xn, r_ref[...], preferred_element_type=F32, precision=lax.Precision.HIGHEST)

    return pl.pallas_call(
        body,
        grid=(n // tm,),
        in_specs=[pl.BlockSpec((tm, d), lambda i: (i, 0)),
                  pl.BlockSpec((1, d), lambda i: (0, 0)),
                  pl.BlockSpec((d, e), lambda i: (0, 0))],
        out_specs=[pl.BlockSpec((tm, d), lambda i: (i, 0)), pl.BlockSpec((tm, e), lambda i: (i, 0))],
        out_shape=[jax.ShapeDtypeStruct((n, d), BF16), jax.ShapeDtypeStruct((n, e), F32)],
        compiler_params=_cp("parallel"),
        name="moe_router",
    )(x2d, g.reshape(1, d), router_pad)


MOE_TM = 512
MOE_TF = 1792


def _experts(xs, row_w, item_tile, item_exp, item_lo, item_hi, wg, wu, wd):
    nrows, d = xs.shape
    tm, tf = MOE_TM, MOE_TF
    nf = D_FF // tf
    n_items = item_tile.shape[0]

    def body(it_ref, ie_ref, lo_ref, hi_ref, x_ref, w_ref, wg_ref, wu_ref, wd_ref, o_ref):
        i = pl.program_id(0)
        j = pl.program_id(1)
        first = jnp.logical_or(i == 0, it_ref[i] != it_ref[jnp.maximum(i - 1, 0)])

        @pl.when(jnp.logical_and(first, j == 0))
        def _():
            o_ref[...] = jnp.zeros_like(o_ref)

        lo = lo_ref[i]
        hi = hi_ref[i]

        @pl.when(lo < hi)
        def _():
            x = x_ref[...]
            gg = jnp.dot(x, wg_ref[0], preferred_element_type=F32)
            uu = jnp.dot(x, wu_ref[0], preferred_element_type=F32)
            act = (_silu(gg) * uu).astype(BF16)
            part = jnp.dot(act, wd_ref[0], preferred_element_type=F32)
            rowi = lax.broadcasted_iota(jnp.int32, (tm, 1), 0)
            mine = jnp.logical_and(rowi >= lo, rowi < hi)
            o_ref[...] += part * jnp.where(mine, w_ref[...], 0.0)

    grid_spec = pltpu.PrefetchScalarGridSpec(
        num_scalar_prefetch=4,
        grid=(n_items, nf),
        in_specs=[pl.BlockSpec((tm, d), lambda i, j, it, ie, lo, hi: (it[i], 0)),
                  pl.BlockSpec((tm, 1), lambda i, j, it, ie, lo, hi: (it[i], 0)),
                  pl.BlockSpec((1, d, tf), lambda i, j, it, ie, lo, hi: (ie[i], 0, j)),
                  pl.BlockSpec((1, d, tf), lambda i, j, it, ie, lo, hi: (ie[i], 0, j)),
                  pl.BlockSpec((1, tf, d), lambda i, j, it, ie, lo, hi: (ie[i], j, 0))],
        out_specs=pl.BlockSpec((tm, d), lambda i, j, it, ie, lo, hi: (it[i], 0)),
    )
    return pl.pallas_call(
        body,
        grid_spec=grid_spec,
        out_shape=jax.ShapeDtypeStruct((nrows, d), F32),
        compiler_params=_cp("arbitrary", "arbitrary"),
        name="moe_experts",
    )(item_tile, item_exp, item_lo, item_hi, xs, row_w, wg, wu, wd)


def _combine_norm(h, y0, y1, g, tm=512):
    n, d = h.shape

    def body(h_ref, a_ref, b_ref, g_ref, o_ref):
        o_ref[...] = _rms_rows(h_ref[...] + (a_ref[...] + b_ref[...]), g_ref[...])

    blk = pl.BlockSpec((tm, d), lambda i: (i, 0))
    return pl.pallas_call(
        body,
        grid=(n // tm,),
        in_specs=[blk, blk, blk, pl.BlockSpec((1, d), lambda i: (0, 0))],
        out_specs=blk,
        out_shape=jax.ShapeDtypeStruct((n, d), F32),
        compiler_params=_cp("parallel"),
        name="combine_norm",
    )(h, y0, y1, g.reshape(1, d))


def _route(logits, n):
    tm = MOE_TM
    na = n * TOP_K
    n_tiles = na // tm
    top_val, top_idx = lax.top_k(logits, TOP_K)
    top_w = jax.nn.softmax(top_val, axis=-1)
    e_flat = top_idx.reshape(-1).astype(jnp.int32)
    w_flat = top_w.reshape(-1)
    tok = jnp.arange(na, dtype=jnp.int32) // TOP_K
    _, sorted_tok, sorted_w = lax.sort((e_flat, tok, w_flat), num_keys=1, is_stable=True)
    onehot = (e_flat[:, None] == jnp.arange(N_EXPERTS, dtype=jnp.int32)[None, :]).astype(jnp.int32)
    rank = jnp.take_along_axis(jnp.cumsum(onehot, axis=0), e_flat[:, None], axis=1)[:, 0] - 1
    counts = jnp.sum(onehot, axis=0)
    ends = jnp.cumsum(counts)
    pos = (ends - counts)[e_flat] + rank
    cuts = jnp.sort(jnp.concatenate([jnp.arange(n_tiles, dtype=jnp.int32) * tm, ends[:-1].astype(jnp.int32)]))
    nxt = jnp.concatenate([cuts[1:], jnp.full((1,), na, jnp.int32)])
    item_tile = jnp.minimum(cuts // tm, n_tiles - 1)
    item_exp = jnp.minimum(jnp.searchsorted(ends, cuts, side="right"), N_EXPERTS - 1).astype(jnp.int32)
    item_lo = cuts - item_tile * tm
    item_hi = nxt - item_tile * tm
    return sorted_tok, sorted_w, (item_tile, item_exp, item_lo, item_hi), pos.reshape(n, TOP_K)


def kernel(x, e_norm1_g, e_w_in, e_ml_conv_w, e_ml_conv_b, e_ml_gate_b, e_ml_norm_g, e_gla_gate_up, e_gla_gate_b,
           e_gla_norm_g, e_w_out, e_norm2_g, e_ffn_w_gate, e_ffn_w_up, e_ffn_w_down, o_norm1_g, o_w_in,
           o_ret_norm_g, o_rw_mu, o_rw_w_up, o_rw_w0, o_rw_a_up, o_rw_a0, o_rw_g_up, o_rw_k_k, o_rw_k_a, o_rw_r_k,
           o_rw_ln_g, o_rw_ln_b, o_w_out, o_norm2_g, o_moe_router, o_moe_w_gate, o_moe_w_up, o_moe_w_down,
           final_norm_g):
    bsz, t, d = x.shape
    n = bsz * t
    h0 = x.reshape(n, d)

    w = e_w_in[0]
    row = lambda a: a.reshape(1, -1)
    x3 = x
    ng1 = row(e_norm1_g[0])
    w_if = w[:, 2048:2056]
    w_gate = jnp.zeros((d, 128), F32).at[:, :2 * ML_HEADS].set(w_if).astype(BF16)
    w_gate_t = jnp.zeros((16, d), F32).at[:2 * ML_HEADS, :].set(w_if.T).astype(BF16)
    h_ml = _mlstm(x3, ng1, w[:, :4 * ML_W].astype(BF16), w_gate, w_gate_t, row(e_ml_gate_b[0]),
                  e_ml_gate_b[0].reshape(-1, 1), e_ml_conv_w[0], row(e_ml_conv_b[0]), row(e_ml_norm_g[0]))
    gq, gk, gv, gr = 2056, 2312, 2568, 3080
    w_pairs = jnp.stack([jnp.concatenate([w[:, gq + 128 * hp:gq + 128 * (hp + 1)],
                                          w[:, gk + 128 * hp:gk + 128 * (hp + 1)],
                                          w[:, gv + 256 * hp:gv + 256 * (hp + 1)],
                                          w[:, gr + 256 * hp:gr + 256 * (hp + 1)]], axis=1)
                         for hp in range(GLA_HEADS // 2)]).astype(BF16)
    w_low = jnp.zeros((d, 128), F32).at[:, :GLA_RANK].set(w[:, 3592:3608]).astype(BF16)
    o_gla = _gla(x3, ng1, w_pairs, w_low, e_gla_gate_up[0].astype(BF16), row(e_gla_gate_b[0]),
                 row(e_gla_norm_g[0]))
    h1 = _mix_out(h_ml.reshape(n, -1), o_gla.reshape(n, -1), e_w_out[0].astype(BF16), h0)
    h2 = _ffn(h1, e_norm2_g[0], e_ffn_w_gate[0].astype(BF16), e_ffn_w_up[0].astype(BF16),
              e_ffn_w_down[0].astype(BF16))

    w = o_w_in[0]
    h2_3 = h2.reshape(bsz, t, d)
    ng2 = row(o_norm1_g[0])
    y_ret = _retention(h2_3, ng2, w[:, :4 * RET_W].astype(BF16), *_retention_tables(t), row(o_ret_norm_g[0]))
    head_of = jnp.arange(RW_W) // RW_DIM
    bd = (head_of[:, None] == head_of[None, :]).astype(BF16)
    r, lw, k, v, aa, bb, g_out, bonus = _rwkv_prep(
        h2_3, ng2, w[:, 4 * RET_W:].astype(BF16), row(o_rw_mu[0]), o_rw_w_up[0].astype(BF16), row(o_rw_w0[0]),
        o_rw_a_up[0].astype(BF16), row(o_rw_a0[0]), o_rw_g_up[0].astype(BF16), row(o_rw_k_k[0]),
        row(o_rw_k_a[0]), row(o_rw_r_k[0]), bd)
    y_rw = _rwkv_scan(r, lw, k, v, aa, bb, g_out, bonus, row(o_rw_ln_g[0]), row(o_rw_ln_b[0]))
    h3 = _mix_out(y_ret.reshape(n, -1), y_rw.reshape(n, -1), o_w_out[0].astype(BF16), h2)

    router_pad = jnp.zeros((d, 128), F32).at[:, :N_EXPERTS].set(o_moe_router[0])
    xn, logits = _router(h3, o_norm2_g[0], router_pad)
    sorted_tok, sorted_w, items, pos = _route(logits[:, :N_EXPERTS], n)
    xs = xn.at[sorted_tok].get(mode="promise_in_bounds")
    ys = _experts(xs, sorted_w.reshape(-1, 1), *items, o_moe_w_gate[0].astype(BF16),
                  o_moe_w_up[0].astype(BF16), o_moe_w_down[0].astype(BF16))
    y0 = ys.at[pos[:, 0]].get(mode="promise_in_bounds")
    y1 = ys.at[pos[:, 1]].get(mode="promise_in_bounds")
    out = _combine_norm(h3, y0, y1, final_norm_g)
    return out.reshape(bsz, t, d)
```

```python
import functools

import numpy as np
import jax
import jax.numpy as jnp
from jax import lax
from jax.experimental import pallas as pl
from jax.experimental.pallas import tpu as pltpu

F32 = jnp.float32
BF16 = jnp.bfloat16

D_MODEL = 1024
EPS = 1e-6
ML_HEADS, ML_DIM, ML_W, ML_CONV = 4, 128, 512, 4
GLA_HEADS, GLA_DK, GLA_DV, GLA_RANK, GLA_TAU = 4, 64, 128, 16, 16.0
RET_HEADS, RET_DIM, RET_W = 4, 128, 512
ROPE_BASE = 10000.0
RW_HEADS, RW_DIM, RW_W = 8, 64, 512
RW_COLS = 1792
RW_LN_EPS = 64e-5
D_FF = 3584
N_EXPERTS = 8
TOP_K = 2

VMEM_LIMIT = 48 * 1024 * 1024
NEG = -1e30


def _cp(*sem):
    return pltpu.CompilerParams(dimension_semantics=sem, vmem_limit_bytes=VMEM_LIMIT)


def _sigmoid(x):
    return 1.0 / (1.0 + jnp.exp(-x))


def _silu(x):
    return x * _sigmoid(x)


def _log_sigmoid(x):
    return jnp.minimum(x, 0.0) - jnp.log(1.0 + jnp.exp(-jnp.abs(x)))


def _dot(a, b):
    return jnp.dot(a.astype(BF16), b.astype(BF16), preferred_element_type=F32)


def _dot_nt(a, b):
    return lax.dot_general(a.astype(BF16), b.astype(BF16), (((1,), (1,)), ((), ())), preferred_element_type=F32)


def _dot_tn(a, b):
    return jnp.dot(a.T.astype(BF16), b.astype(BF16), preferred_element_type=F32)


def _rms_rows(x, g):
    ms = jnp.mean(x * x, axis=-1, keepdims=True)
    return x * lax.rsqrt(ms + EPS) * g


def _mix_out_ffn(a, b, w_out, resid, g, wg, wu, wd, tm=512, tf=1792):
    n, d = resid.shape
    wa = a.shape[1]
    wb = b.shape[1]
    f = wg.shape[1]
    nf = f // tf

    def body(a_ref, b_ref, wo_ref, r_ref, g_ref, wg_ref, wu_ref, wd_ref, o_ref, xn_ref, acc_ref):
        j = pl.program_id(1)

        @pl.when(j == 0)
        def _():
            h = r_ref[...] + jnp.dot(a_ref[...], wo_ref[0:wa, :], preferred_element_type=F32)
            h = h + jnp.dot(b_ref[...], wo_ref[wa:wa + wb, :], preferred_element_type=F32)
            xn_ref[...] = _rms_rows(h, g_ref[...]).astype(BF16)
            acc_ref[...] = h

        xn = xn_ref[...]
        gg = jnp.dot(xn, wg_ref[...], preferred_element_type=F32)
        uu = jnp.dot(xn, wu_ref[...], preferred_element_type=F32)
        act = (_silu(gg) * uu).astype(BF16)
        acc_ref[...] += jnp.dot(act, wd_ref[...], preferred_element_type=F32)

        @pl.when(j == nf - 1)
        def _():
            o_ref[...] = acc_ref[...]

    return pl.pallas_call(
        body,
        grid=(n // tm, nf),
        in_specs=[pl.BlockSpec((tm, wa), lambda i, j: (i, 0)),
                  pl.BlockSpec((tm, wb), lambda i, j: (i, 0)),
                  pl.BlockSpec((wa + wb, d), lambda i, j: (0, 0)),
                  pl.BlockSpec((tm, d), lambda i, j: (i, 0)),
                  pl.BlockSpec((1, d), lambda i, j: (0, 0)),
                  pl.BlockSpec((d, tf), lambda i, j: (0, j)),
                  pl.BlockSpec((d, tf), lambda i, j: (0, j)),
                  pl.BlockSpec((tf, d), lambda i, j: (j, 0))],
        out_specs=pl.BlockSpec((tm, d), lambda i, j: (i, 0)),
        out_shape=jax.ShapeDtypeStruct((n, d), F32),
        scratch_shapes=[pltpu.VMEM((tm, d), BF16), pltpu.VMEM((tm, d), F32)],
        compiler_params=_cp("parallel", "arbitrary"),
        name="mix_out_ffn",
    )(a, b, w_out, resid, g.reshape(1, d), wg, wu, wd)


ML_L = 256


def _mlstm(x, in_g, w_in, w_gate, w_gate_t, gate_b_row, gate_b_col, conv_w, conv_b, norm_g):
    bsz, t, d = x.shape
    L, D, H, W = ML_L, ML_DIM, ML_HEADS, ML_W

    def body(x_ref, ing_ref, w_ref, wg_ref, wgt_ref, gbr_ref, gbc_ref, cwq_ref, cwk_ref, cbq_ref, cbk_ref,
             ng_ref, out_ref, qext, kext, c_ref, n_ref, m_ref):
        c = pl.program_id(1)

        @pl.when(c == 0)
        def _():
            qext[0:8, :] = jnp.zeros((8, W), F32)
            kext[0:8, :] = jnp.zeros((8, W), F32)
            c_ref[...] = jnp.zeros_like(c_ref)
            n_ref[...] = jnp.zeros_like(n_ref)
            m_ref[...] = jnp.zeros_like(m_ref)

        xn = _rms_rows(x_ref[0], ing_ref[...]).astype(BF16)
        p = jnp.dot(xn, w_ref[...], preferred_element_type=F32)
        g_cols = jnp.dot(xn, wg_ref[...], preferred_element_type=F32)[:, 0:2 * H]
        g_rows = lax.dot_general(wgt_ref[...], xn, (((1,), (1,)), ((), ())),
                                 preferred_element_type=F32)[0:2 * H, :]
        qext[8:, :] = p[:, 0:W]
        kext[8:, :] = p[:, W:2 * W]

        def conv(ext, cw_ref, cb_ref):
            acc = cb_ref[...] + cw_ref[0:1, :] * ext[pl.ds(8 - ML_CONV + 1, L), :]
            for kk in range(1, ML_CONV):
                acc = acc + cw_ref[kk:kk + 1, :] * ext[pl.ds(8 - ML_CONV + 1 + kk, L), :]
            return _silu(acc)

        q_all = conv(qext, cwq_ref, cbq_ref) * (D ** -0.5)
        k_all = conv(kext, cwk_ref, cbk_ref)
        qext[0:8, :] = qext[L:L + 8, :]
        kext[0:8, :] = kext[L:L + 8, :]

        gcol = g_cols + gbr_ref[...]
        grow = g_rows + gbc_ref[...]
        fcol = _log_sigmoid(gcol[:, H:2 * H])
        frow = _log_sigmoid(grow[H:2 * H, :])
        ri = lax.broadcasted_iota(jnp.int32, (L, L), 0)
        ci = lax.broadcasted_iota(jnp.int32, (L, L), 1)
        causal = ri >= ci
        heads = range(H)
        hs = lambda z, h: z[:, h * D:(h + 1) * D]
        w_intra, w_inter, w_state, carry, m_row, m_new = [], [], [], [], [], []
        for h in heads:
            f_row = frow[h:h + 1, :]
            i_row = grow[h:h + 1, :]
            f_col = fcol[:, h:h + 1]
            i_col = gcol[:, h:h + 1]
            m_st = m_ref[h]
            b_col = jnp.sum(jnp.where(causal, f_row, 0.0), axis=1, keepdims=True)
            b_row = jnp.sum(jnp.where(ri <= ci, f_col, 0.0), axis=0, keepdims=True)
            g_tot = jnp.sum(f_row, axis=1, keepdims=True)
            d_intra = jnp.where(causal, b_col - b_row + i_row, NEG)
            d_inter = b_col + m_st
            mr = jnp.maximum(d_inter, jnp.max(d_intra, axis=1, keepdims=True))
            w_intra.append(jnp.exp(d_intra - mr))
            w_inter.append(jnp.exp(d_inter - mr))
            m_row.append(mr)
            d_state = g_tot - b_col + i_col
            mn = jnp.maximum(g_tot + m_st, jnp.max(d_state, axis=0, keepdims=True))
            w_state.append(jnp.exp(d_state - mn))
            carry.append(jnp.exp(g_tot + m_st - mn))
            m_new.append(mn)
        qh = [hs(q_all, h) for h in heads]
        kh = [hs(k_all, h) for h in heads]
        vh = [p[:, 2 * W + h * D:2 * W + (h + 1) * D] for h in heads]
        c_st = [c_ref[h] for h in heads]
        n_st = [n_ref[h] for h in heads]
        s = [_dot_nt(qh[h], kh[h]) * w_intra[h] for h in heads]
        qc = [_dot(qh[h], c_st[h]) for h in heads]
        kw = [kh[h] * w_state[h] for h in heads]
        kv = [_dot_tn(kw[h], vh[h]) for h in heads]
        sv = [_dot(s[h], vh[h]) for h in heads]
        for h in heads:
            num = sv[h] + w_inter[h] * qc[h]
            den = jnp.sum(s[h], axis=1, keepdims=True) + w_inter[h] * jnp.sum(qh[h] * n_st[h], axis=1, keepdims=True)
            hval = num / jnp.maximum(jnp.abs(den), jnp.exp(-m_row[h]))
            c_ref[h] = carry[h] * c_st[h] + kv[h]
            n_ref[h] = carry[h] * n_st[h] + jnp.sum(kw[h], axis=0, keepdims=True)
            m_ref[h] = m_new[h]
            hg = _sigmoid(p[:, 3 * W + h * D:3 * W + (h + 1) * D]) * hval
            hc = hg - jnp.mean(hg, axis=1, keepdims=True)
            hn = hc * lax.rsqrt(jnp.mean(hc * hc, axis=1, keepdims=True) + EPS)
            out_ref[0, :, h * D:(h + 1) * D] = (hn * ng_ref[:, h * D:(h + 1) * D]).astype(BF16)

    fix = lambda j: (lambda b, c: (0, j))
    return pl.pallas_call(
        body,
        grid=(bsz, t // L),
        in_specs=[pl.BlockSpec((1, L, d), lambda b, c: (b, c, 0)),
                  pl.BlockSpec((1, d), fix(0)),
                  pl.BlockSpec((d, 4 * W), fix(0)),
                  pl.BlockSpec((d, 128), fix(0)),
                  pl.BlockSpec((16, d), fix(0)),
                  pl.BlockSpec((1, 2 * H), fix(0)),
                  pl.BlockSpec((2 * H, 1), fix(0)),
                  pl.BlockSpec((ML_CONV, W), fix(0)),
                  pl.BlockSpec((ML_CONV, W), fix(1)),
                  pl.BlockSpec((1, W), fix(0)),
                  pl.BlockSpec((1, W), fix(1)),
                  pl.BlockSpec((1, W), fix(0))],
        out_specs=pl.BlockSpec((1, L, W), lambda b, c: (b, c, 0)),
        out_shape=jax.ShapeDtypeStruct((bsz, t, W), BF16),
        scratch_shapes=[pltpu.VMEM((L + 8, W), F32), pltpu.VMEM((L + 8, W), F32),
                        pltpu.VMEM((H, D, D), F32), pltpu.VMEM((H, 1, D), F32), pltpu.VMEM((H, 1, 1), F32)],
        compiler_params=_cp("parallel", "arbitrary"),
        name="mlstm",
    )(x, in_g, w_in, w_gate, w_gate_t, gate_b_row, gate_b_col, conv_w, conv_w, conv_b, conv_b, norm_g)


GLA_TC = 256
GLA_SUB = 16
GLA_GROUP = 128


def _gla(x, in_g, w_pairs, w_low, gate_up, gate_b, norm_g):
    bsz, t, d = x.shape
    tc, S, GB = GLA_TC, GLA_SUB, GLA_GROUP
    head_ones = (jnp.arange(2 * GLA_DK)[:, None] // GLA_DK == jnp.arange(2 * GB)[None, :] // GB).astype(BF16)
    nsub = tc // S
    dk, dv = GLA_DK, GLA_DV

    def body(x_ref, ing_ref, w_ref, wl_ref, gu_ref, gbias_ref, ng_ref, ones_ref, out_ref, st_ref, ksh, bsh):
        c = pl.program_id(2)

        @pl.when(c == 0)
        def _():
            st_ref[...] = jnp.zeros_like(st_ref)
            ksh[0:S, :] = jnp.zeros((S, 2 * dk), F32)
            bsh[0:S, :] = jnp.zeros((S, 2 * dk), F32)

        xn = _rms_rows(x_ref[0], ing_ref[...]).astype(BF16)
        p = jnp.dot(xn, w_ref[0], preferred_element_type=F32)
        g_low = jnp.dot(xn, wl_ref[...], preferred_element_type=F32)[:, 0:GLA_RANK]
        z = _dot(g_low, gu_ref[...]) + gbias_ref[...]
        la = _log_sigmoid(z) / GLA_TAU
        rowi = lax.broadcasted_iota(jnp.int32, (tc, 1), 0)
        rmod = rowi & (S - 1)
        bcum = la
        rsum = la
        for s in (1, 2, 4, 8):
            bcum = bcum + jnp.where(rmod >= s, pltpu.roll(bcum, s, 0), 0.0)
            rsum = rsum + jnp.where(rmod < S - s, pltpu.roll(rsum, tc - s, 0), 0.0)
        q = p[:, 0:2 * dk] * (dk ** -0.5)
        k = p[:, 2 * dk:4 * dk]
        v = p[:, 4 * dk:4 * dk + 2 * dv]
        gate = p[:, 4 * dk + 2 * dv:4 * dk + 4 * dv]
        qt = q * jnp.exp(bcum)
        kt = k * jnp.exp(rsum - la)
        eg = jnp.exp(bcum + rsum - la)

        ksh[S:, :] = k
        bsh[S:, :] = bcum
        prods = []
        for d in range(S):
            kd = k if d == 0 else ksh[pl.ds(S - d, tc), :]
            bd = bcum if d == 0 else bsh[pl.ds(S - d, tc), :]
            e = jnp.exp(jnp.where(rmod >= d, bcum - bd, 0.0))
            prods.append((q * kd * e).astype(BF16))
        ws = [jnp.dot(p, ones_ref[...], preferred_element_type=F32) for p in prods]
        coli = lax.broadcasted_iota(jnp.int32, (tc, GB), 1)
        rgrp = rowi & (GB - 1)
        att0 = jnp.zeros((tc, GB), F32)
        att1 = jnp.zeros((tc, GB), F32)
        for d in range(S):
            here = jnp.logical_and(coli == rgrp - d, rmod >= d)
            att0 = jnp.where(here, ws[d][:, 0:GB], att0)
            att1 = jnp.where(here, ws[d][:, GB:2 * GB], att1)

        heads = range(2)
        lk = [slice(hh * dk, (hh + 1) * dk) for hh in heads]
        lv = [slice(hh * dv, (hh + 1) * dv) for hh in heads]
        kv = [[_dot_tn(v[si * S:(si + 1) * S, lv[hh]], kt[si * S:(si + 1) * S, lk[hh]]) for hh in heads]
              for si in range(nsub)]
        st = [st_ref[hh] for hh in heads]
        inter = [[], []]
        for si in range(nsub):
            rows = slice(si * S, (si + 1) * S)
            for hh in heads:
                inter[hh].append(_dot_nt(qt[rows, lk[hh]], st[hh]))
                st[hh] = st[hh] * eg[si * S:si * S + 1, lk[hh]] + kv[si][hh]
        for hh in heads:
            st_ref[hh] = st[hh]

        for hh, att in ((0, att0), (1, att1)):
            diag = jnp.concatenate([_dot(att[g * GB:(g + 1) * GB], v[g * GB:(g + 1) * GB, lv[hh]])
                                    for g in range(tc // GB)], axis=0)
            o = diag + jnp.concatenate(inter[hh], axis=0)
            on = o * lax.rsqrt(jnp.mean(o * o, axis=1, keepdims=True) + EPS)
            out_ref[0, :, lv[hh]] = (on * ng_ref[:, lv[hh]] * _silu(gate[:, lv[hh]])).astype(BF16)

    pw = 4 * dk + 4 * dv
    return pl.pallas_call(
        body,
        grid=(bsz, GLA_HEADS // 2, t // tc),
        in_specs=[pl.BlockSpec((1, tc, d), lambda b, h, c: (b, c, 0)),
                  pl.BlockSpec((1, d), lambda b, h, c: (0, 0)),
                  pl.BlockSpec((1, d, pw), lambda b, h, c: (h, 0, 0)),
                  pl.BlockSpec((d, 128), lambda b, h, c: (0, 0)),
                  pl.BlockSpec((GLA_RANK, 2 * dk), lambda b, h, c: (0, h)),
                  pl.BlockSpec((1, 2 * dk), lambda b, h, c: (0, h)),
                  pl.BlockSpec((1, 2 * dv), lambda b, h, c: (0, h)),
                  pl.BlockSpec((2 * dk, 2 * GB), lambda b, h, c: (0, 0))],
        out_specs=pl.BlockSpec((1, tc, 2 * dv), lambda b, h, c: (b, c, h)),
        out_shape=jax.ShapeDtypeStruct((bsz, t, GLA_HEADS * dv), BF16),
        scratch_shapes=[pltpu.VMEM((2, dv, dk), F32), pltpu.VMEM((tc + S, 2 * dk), F32),
                        pltpu.VMEM((tc + S, 2 * dk), F32)],
        compiler_params=_cp("parallel", "parallel", "arbitrary"),
        name="gla",
    )(x, in_g, w_pairs, w_low, gate_up, gate_b, norm_g, head_ones)


RET_L = 256


def _retention(x, in_g, w_in, cos_t, sin_t, intra, inter, sdec, cdec, norm_g):
    bsz, t, d = x.shape
    L, D, H, W = RET_L, RET_DIM, RET_HEADS, RET_W

    def body(x_ref, ing_ref, w_ref, cos_ref, sin_ref, intra_ref, inter_ref, sdec_ref, cdec_ref, ng_ref,
             out_ref, s_ref):
        c = pl.program_id(1)

        @pl.when(c == 0)
        def _():
            s_ref[...] = jnp.zeros_like(s_ref)

        p = jnp.dot(_rms_rows(x_ref[0], ing_ref[...]).astype(BF16), w_ref[...], preferred_element_type=F32)
        cs = cos_ref[...]
        sn = sin_ref[...]

        def rot(z):
            return z * cs + pltpu.roll(z, D // 2, 1) * sn

        heads = range(H)
        q = [rot(p[:, h * D:(h + 1) * D]) * (D ** -0.5) for h in heads]
        k = [rot(p[:, W + h * D:W + (h + 1) * D]) for h in heads]
        v = [p[:, 2 * W + h * D:2 * W + (h + 1) * D] for h in heads]
        s_st = [s_ref[h] for h in heads]
        s = [_dot_nt(q[h], k[h]) * intra_ref[h] for h in heads]
        qs = [_dot(q[h], s_st[h]) for h in heads]
        kv = [_dot_tn(k[h] * sdec_ref[h], v[h]) for h in heads]
        sv = [_dot(s[h], v[h]) for h in heads]
        for h in heads:
            o = sv[h] + inter_ref[h] * qs[h]
            s_ref[h] = cdec_ref[h] * s_st[h] + kv[h]
            oc = o - jnp.mean(o, axis=1, keepdims=True)
            on = oc * lax.rsqrt(jnp.mean(oc * oc, axis=1, keepdims=True) + EPS)
            hl = slice(h * D, (h + 1) * D)
            gate = p[:, 3 * W + h * D:3 * W + (h + 1) * D]
            out_ref[0, :, hl] = (on * ng_ref[:, hl] * _silu(gate)).astype(BF16)

    fix2 = lambda b, c: (0, 0)
    fix3 = lambda b, c: (0, 0, 0)
    return pl.pallas_call(
        body,
        grid=(bsz, t // L),
        in_specs=[pl.BlockSpec((1, L, d), lambda b, c: (b, c, 0)),
                  pl.BlockSpec((1, d), fix2),
                  pl.BlockSpec((d, 4 * W), fix2),
                  pl.BlockSpec((L, D), lambda b, c: (c, 0)),
                  pl.BlockSpec((L, D), lambda b, c: (c, 0)),
                  pl.BlockSpec((H, L, L), fix3),
                  pl.BlockSpec((H, L, 1), fix3),
                  pl.BlockSpec((H, L, 1), fix3),
                  pl.BlockSpec((H, 1, 1), fix3),
                  pl.BlockSpec((1, W), lambda b, c: (0, 0))],
        out_specs=pl.BlockSpec((1, L, W), lambda b, c: (b, c, 0)),
        out_shape=jax.ShapeDtypeStruct((bsz, t, W), BF16),
        scratch_shapes=[pltpu.VMEM((H, D, D), F32)],
        compiler_params=_cp("parallel", "arbitrary"),
        name="retention",
    )(x, in_g, w_in, cos_t, sin_t, intra, inter, sdec, cdec, norm_g)


def _retention_tables(t):
    L, D = RET_L, RET_DIM
    inv = ROPE_BASE ** (-jnp.arange(0, D, 2, dtype=F32) / D)
    ang = jnp.arange(t).astype(F32)[:, None] * inv[None, :]
    cos = jnp.cos(ang)
    sin = jnp.sin(ang)
    cos_t = jnp.concatenate([cos, cos], axis=-1)
    sin_t = jnp.concatenate([-sin, sin], axis=-1)
    log_gamma = jnp.log1p(-jnp.exp2(-5.0 - jnp.arange(RET_HEADS, dtype=F32)))
    idx = jnp.arange(L, dtype=F32)
    causal = idx[:, None] >= idx[None, :]
    rel = jnp.where(causal, idx[:, None] - idx[None, :], 0.0)
    intra = jnp.where(causal, jnp.exp(log_gamma[:, None, None] * rel), 0.0)
    inter = jnp.exp(log_gamma[:, None] * (idx + 1.0))[:, :, None]
    sdec = jnp.exp(log_gamma[:, None] * (L - 1.0 - idx))[:, :, None]
    cdec = jnp.exp(log_gamma * L)[:, None, None]
    return cos_t, sin_t, intra, inter, sdec, cdec


RWP_TM = 256


def _seg_sum(x, bd):
    hi = x.astype(BF16)
    lo = (x - hi.astype(F32)).astype(BF16)
    return jnp.dot(hi, bd, preferred_element_type=F32) + jnp.dot(lo, bd, preferred_element_type=F32)


def _rwkv_prep(x, in_g, w_in, mu, w_up, w0, a_up, a0, g_up, k_k, k_a, r_k, bd):
    bsz, t, d = x.shape
    tm, W = RWP_TM, RW_W

    def body(x_ref, ing_ref, win_ref, mu_ref, wup_ref, w0_ref, aup_ref, a0_ref, gup_ref, kk_ref, ka_ref, rk_ref,
             bd_ref, r_out, lw_out, k_out, v_out, a_out, b_out, g_out, bonus_out, last_ref):
        c = pl.program_id(1)

        @pl.when(c == 0)
        def _():
            last_ref[...] = jnp.zeros_like(last_ref)

        cur = jnp.dot(_rms_rows(x_ref[0], ing_ref[...]).astype(BF16), win_ref[...], preferred_element_type=F32)
        prev = last_ref[...]
        last_ref[...] = cur[tm - 1:tm, :]
        rowi = lax.broadcasted_iota(jnp.int32, (tm, 1), 0)
        sh = jnp.where(rowi == 0, prev, pltpu.roll(cur, 1, 0))
        xm = cur + (sh - cur) * mu_ref[...]
        x_r = xm[:, 0:W]
        x_k = xm[:, W:2 * W]
        x_v = xm[:, 2 * W:3 * W]
        x_dl = xm[:, 3 * W:3 * W + 64]
        x_al = xm[:, 3 * W + 64:3 * W + 128]
        x_gl = xm[:, 3 * W + 128:3 * W + 256]
        wl = w0_ref[...] + _dot(jnp.tanh(x_dl), wup_ref[...])
        sp = jnp.maximum(-wl, 0.0) + jnp.log(1.0 + jnp.exp(-jnp.abs(wl)))
        lw_out[0] = -jnp.exp(-sp - 0.5)
        a = _sigmoid(a0_ref[...] + _dot(x_al, aup_ref[...]))
        g_out[0] = _dot(_sigmoid(x_gl), gup_ref[...]).astype(BF16)
        kk0 = x_k * kk_ref[...]
        nrm = jnp.sqrt(_seg_sum(kk0 * kk0, bd_ref[...]))
        kk = kk0 / jnp.maximum(nrm, 1e-12)
        k_h = x_k * (1.0 + (a - 1.0) * ka_ref[...])
        r_out[0] = x_r.astype(BF16)
        k_out[0] = k_h.astype(BF16)
        v_out[0] = x_v.astype(BF16)
        a_out[0] = (-kk).astype(BF16)
        b_out[0] = (kk * a).astype(BF16)
        bonus_out[0] = (_seg_sum(x_r * k_h * rk_ref[...], bd_ref[...]) * x_v).astype(BF16)

    row = lambda b, c: (0, 0)
    blk = pl.BlockSpec((1, tm, W), lambda b, c: (b, c, 0))
    shp = [jax.ShapeDtypeStruct((bsz, t, W), F32 if i == 1 else BF16) for i in range(8)]
    return pl.pallas_call(
        body,
        grid=(bsz, t // tm),
        in_specs=[pl.BlockSpec((1, tm, d), lambda b, c: (b, c, 0)),
                  pl.BlockSpec((1, d), row),
                  pl.BlockSpec((d, RW_COLS), row),
                  pl.BlockSpec((1, RW_COLS), row),
                  pl.BlockSpec((64, W), row), pl.BlockSpec((1, W), row),
                  pl.BlockSpec((64, W), row), pl.BlockSpec((1, W), row),
                  pl.BlockSpec((128, W), row),
                  pl.BlockSpec((1, W), row), pl.BlockSpec((1, W), row), pl.BlockSpec((1, W), row),
                  pl.BlockSpec((W, W), row)],
        out_specs=[blk] * 8,
        out_shape=shp,
        scratch_shapes=[pltpu.VMEM((1, RW_COLS), F32)],
        compiler_params=_cp("parallel", "arbitrary"),
        name="rwkv_prep",
    )(x, in_g, w_in, mu, w_up, w0, a_up, a0, g_up, k_k, k_a, r_k, bd)


RW_L = 64
RW_TB = 128


def _rwkv_scan(r, lw, k, v, aa, bb, g_out, bonus, ln_g, ln_b):
    bsz, t, W = r.shape
    L, N, tb = RW_L, RW_DIM, RW_TB
    nck = tb // L

    def body(r_ref, lw_ref, k_ref, v_ref, a_ref, b_ref, g_ref, bonus_ref, lng_ref, lnb_ref, out_ref, h_ref):
        c = pl.program_id(1)

        @pl.when(c == 0)
        def _():
            h_ref[...] = jnp.zeros_like(h_ref)

        lw_all = lw_ref[0]
        rowi = lax.broadcasted_iota(jnp.int32, (tb, 1), 0) & (L - 1)
        cl = lw_all
        for s in (1, 2, 4, 8, 16, 32):
            cl = cl + jnp.where(rowi >= s, pltpu.roll(cl, s, 0), 0.0)
        cl_last = jnp.concatenate([jnp.broadcast_to(cl[(cc + 1) * L - 1:(cc + 1) * L, :], (L, W))
                                   for cc in range(nck)], axis=0)
        e_inv = jnp.exp(-cl)
        e_end = jnp.exp(cl_last - cl)
        p_end = jnp.exp(cl_last)
        at = a_ref[0] * jnp.exp(cl - lw_all)
        rt = r_ref[0] * jnp.exp(cl)
        bt = b_ref[0] * e_inv
        kt = k_ref[0] * e_inv
        b_end = b_ref[0] * e_end
        k_end = k_ref[0] * e_end
        v_all = v_ref[0]
        ri = lax.broadcasted_iota(jnp.int32, (L, L), 0)
        ci = lax.broadcasted_iota(jnp.int32, (L, L), 1)
        strict = ri > ci
        lower = ri >= ci
        eye = ri == ci
        heads = range(nck * RW_HEADS)

        def hs(z, u):
            cc, hh = divmod(u, RW_HEADS)
            return z[cc * L:(cc + 1) * L, hh * N:(hh + 1) * N]

        m = [_dot_nt(jnp.concatenate([hs(at, h), hs(rt, h)], axis=0),
                     jnp.concatenate([hs(bt, h), hs(kt, h)], axis=0)) for h in heads]
        a_ab = [jnp.where(strict, m[h][0:L, 0:L], 0.0) for h in heads]
        a_ak = [jnp.where(strict, m[h][0:L, L:2 * L], 0.0) for h in heads]
        a_rb = [jnp.where(lower, m[h][L:2 * L, 0:L], 0.0) for h in heads]
        a_rk = [jnp.where(lower, m[h][L:2 * L, L:2 * L], 0.0) for h in heads]
        end_t = [jnp.concatenate([hs(b_end, h), hs(k_end, h)], axis=1).T for h in heads]
        vp = [_dot(jnp.concatenate([a_ak[h], a_rk[h], end_t[h][N:2 * N]], axis=0), hs(v_all, h)) for h in heads]
        x = [jnp.concatenate([hs(at, h), vp[h][0:L]], axis=1) for h in heads]
        ap = a_ab
        for it in range(6):
            x = [x[h] + _dot(ap[h], x[h]) for h in heads]
            if it < 5:
                ap = [_dot(ap[h], ap[h]) for h in heads]
        post = [_dot(jnp.concatenate([a_rb[h], end_t[h][0:N]], axis=0), x[h]) for h in heads]
        lhs = []
        y0s = []
        h_adds = []
        for u in heads:
            q_hat = hs(rt, u) + post[u][0:L, 0:N]
            gmat = jnp.where(eye, hs(p_end, u), 0.0) + post[u][L:L + N, 0:N]
            lhs.append(jnp.concatenate([q_hat, gmat], axis=0))
            y0s.append(post[u][0:L, N:2 * N] + vp[u][L:2 * L])
            h_adds.append(post[u][L:L + N, N:2 * N] + vp[u][2 * L:2 * L + N])
        h_st = [h_ref[hh] for hh in range(RW_HEADS)]
        for cc in range(nck):
            outs = []
            for hh in range(RW_HEADS):
                u = cc * RW_HEADS + hh
                res = _dot(lhs[u], h_st[hh])
                y = res[0:L] + y0s[u]
                h_st[hh] = res[L:L + N] + h_adds[u]
                yc = y - jnp.mean(y, axis=1, keepdims=True)
                outs.append(yc * lax.rsqrt(jnp.mean(yc * yc, axis=1, keepdims=True) + RW_LN_EPS))
            yn = jnp.concatenate(outs, axis=1)
            rows = slice(cc * L, (cc + 1) * L)
            out_ref[0, rows, :] = ((yn * lng_ref[...] + lnb_ref[...] + bonus_ref[0, rows, :])
                                   * g_ref[0, rows, :]).astype(BF16)
        for hh in range(RW_HEADS):
            h_ref[hh] = h_st[hh]

    blk = pl.BlockSpec((1, tb, W), lambda b, c: (b, c, 0))
    vec = pl.BlockSpec((1, W), lambda b, c: (0, 0))
    return pl.pallas_call(
        body,
        grid=(bsz, t // tb),
        in_specs=[blk] * 8 + [vec, vec],
        out_specs=blk,
        out_shape=jax.ShapeDtypeStruct((bsz, t, W), BF16),
        scratch_shapes=[pltpu.VMEM((RW_HEADS, N, N), F32)],
        compiler_params=_cp("parallel", "arbitrary"),
        name="rwkv_scan",
    )(r, lw, k, v, aa, bb, g_out, bonus, ln_g, ln_b)


def _mix_out_route(a, b, w_bf16, resid, g, router_pad, tm=512):
    n, wa = a.shape
    wb = b.shape[1]
    d = w_bf16.shape[1]
    e = router_pad.shape[1]

    def body(a_ref, b_ref, w_ref, r_ref, g_ref, rt_ref, h_ref, xn_ref, lg_ref):
        acc = jnp.dot(a_ref[...], w_ref[0:wa, :], preferred_element_type=F32)
        acc = acc + jnp.dot(b_ref[...], w_ref[wa:wa + wb, :], preferred_element_type=F32)
        h = r_ref[...] + acc
        h_ref[...] = h
        xn = _rms_rows(h, g_ref[...])
        xn_ref[...] = xn.astype(BF16)
        lg_ref[...] = jnp.dot(xn, rt_ref[...], preferred_element_type=F32, precision=lax.Precision.HIGHEST)

    return pl.pallas_call(
        body,
        grid=(n // tm,),
        in_specs=[pl.BlockSpec((tm, wa), lambda i: (i, 0)),
                  pl.BlockSpec((tm, wb), lambda i: (i, 0)),
                  pl.BlockSpec((wa + wb, d), lambda i: (0, 0)),
                  pl.BlockSpec((tm, d), lambda i: (i, 0)),
                  pl.BlockSpec((1, d), lambda i: (0, 0)),
                  pl.BlockSpec((d, e), lambda i: (0, 0))],
        out_specs=[pl.BlockSpec((tm, d), lambda i: (i, 0)), pl.BlockSpec((tm, d), lambda i: (i, 0)),
                   pl.BlockSpec((tm, e), lambda i: (i, 0))],
        out_shape=[jax.ShapeDtypeStruct((n, d), F32), jax.ShapeDtypeStruct((n, d), BF16),
                   jax.ShapeDtypeStruct((n, e), F32)],
        compiler_params=_cp("parallel"),
        name="mix_out_route",
    )(a, b, w_bf16, resid, g.reshape(1, d), router_pad)


MOE_TM = 512
MOE_TF = 1792


def _experts(xs, row_w, item_tile, item_exp, item_lo, item_hi, wg, wu, wd):
    nrows, d = xs.shape
    tm, tf = MOE_TM, MOE_TF
    nf = D_FF // tf
    n_items = item_tile.shape[0]

    def body(it_ref, ie_ref, lo_ref, hi_ref, x_ref, w_ref, wg_ref, wu_ref, wd_ref, o_ref, acc_ref):
        i = pl.program_id(0)
        j = pl.program_id(1)
        tile = it_ref[i]
        first = jnp.logical_or(i == 0, tile != it_ref[jnp.maximum(i - 1, 0)])
        last = jnp.logical_or(i == n_items - 1, tile != it_ref[jnp.minimum(i + 1, n_items - 1)])

        @pl.when(jnp.logical_and(first, j == 0))
        def _():
            acc_ref[...] = jnp.zeros_like(acc_ref)

        lo = lo_ref[i]
        hi = hi_ref[i]

        @pl.when(lo < hi)
        def _():
            x = x_ref[...]
            gg = jnp.dot(x, wg_ref[0], preferred_element_type=F32)
            uu = jnp.dot(x, wu_ref[0], preferred_element_type=F32)
            act = (_silu(gg) * uu).astype(BF16)
            part = jnp.dot(act, wd_ref[0], preferred_element_type=F32)
            rowi = lax.broadcasted_iota(jnp.int32, (tm, 1), 0)
            mine = jnp.logical_and(rowi >= lo, rowi < hi)
            acc_ref[...] += part * jnp.where(mine, w_ref[...], 0.0)

        @pl.when(jnp.logical_and(last, j == nf - 1))
        def _():
            o_ref[...] = acc_ref[...].astype(o_ref.dtype)

    grid_spec = pltpu.PrefetchScalarGridSpec(
        num_scalar_prefetch=4,
        grid=(n_items, nf),
        in_specs=[pl.BlockSpec((tm, d), lambda i, j, it, ie, lo, hi: (it[i], 0)),
                  pl.BlockSpec((tm, 1), lambda i, j, it, ie, lo, hi: (it[i], 0)),
                  pl.BlockSpec((1, d, tf), lambda i, j, it, ie, lo, hi: (ie[i], 0, j)),
                  pl.BlockSpec((1, d, tf), lambda i, j, it, ie, lo, hi: (ie[i], 0, j)),
                  pl.BlockSpec((1, tf, d), lambda i, j, it, ie, lo, hi: (ie[i], j, 0))],
        out_specs=pl.BlockSpec((tm, d), lambda i, j, it, ie, lo, hi: (it[i], 0)),
        scratch_shapes=[pltpu.VMEM((tm, d), F32)],
    )
    return pl.pallas_call(
        body,
        grid_spec=grid_spec,
        out_shape=jax.ShapeDtypeStruct((nrows, d), BF16),
        compiler_params=_cp("arbitrary", "arbitrary"),
        name="moe_experts",
    )(item_tile, item_exp, item_lo, item_hi, xs, row_w, wg, wu, wd)


def _combine_norm(h, y0, y1, g, tm=512):
    n, d = h.shape

    def body(h_ref, a_ref, b_ref, g_ref, o_ref):
        o_ref[...] = _rms_rows(h_ref[...] + (a_ref[...].astype(F32) + b_ref[...].astype(F32)), g_ref[...])

    blk = pl.BlockSpec((tm, d), lambda i: (i, 0))
    return pl.pallas_call(
        body,
        grid=(n // tm,),
        in_specs=[blk, blk, blk, pl.BlockSpec((1, d), lambda i: (0, 0))],
        out_specs=blk,
        out_shape=jax.ShapeDtypeStruct((n, d), F32),
        compiler_params=_cp("parallel"),
        name="combine_norm",
    )(h, y0, y1, g.reshape(1, d))


def _route(logits, n):
    tm = MOE_TM
    na = n * TOP_K
    n_tiles = na // tm
    top_val, top_idx = lax.top_k(logits, TOP_K)
    top_w = jax.nn.softmax(top_val, axis=-1)
    e_flat = top_idx.reshape(-1).astype(jnp.int32)
    w_flat = top_w.reshape(-1)
    tok = jnp.arange(na, dtype=jnp.int32) // TOP_K
    _, sorted_tok, sorted_w = lax.sort((e_flat, tok, w_flat), num_keys=1, is_stable=True)
    onehot = (e_flat[:, None] == jnp.arange(N_EXPERTS, dtype=jnp.int32)[None, :]).astype(jnp.int32)
    rank = jnp.take_along_axis(jnp.cumsum(onehot, axis=0), e_flat[:, None], axis=1)[:, 0] - 1
    counts = jnp.sum(onehot, axis=0)
    ends = jnp.cumsum(counts)
    pos = (ends - counts)[e_flat] + rank
    cuts = jnp.sort(jnp.concatenate([jnp.arange(n_tiles, dtype=jnp.int32) * tm, ends[:-1].astype(jnp.int32)]))
    nxt = jnp.concatenate([cuts[1:], jnp.full((1,), na, jnp.int32)])
    item_tile = jnp.minimum(cuts // tm, n_tiles - 1)
    item_exp = jnp.minimum(jnp.searchsorted(ends, cuts, side="right"), N_EXPERTS - 1).astype(jnp.int32)
    item_lo = cuts - item_tile * tm
    item_hi = nxt - item_tile * tm
    return sorted_tok, sorted_w, (item_tile, item_exp, item_lo, item_hi), pos.reshape(n, TOP_K)


def kernel(x, e_norm1_g, e_w_in, e_ml_conv_w, e_ml_conv_b, e_ml_gate_b, e_ml_norm_g, e_gla_gate_up, e_gla_gate_b,
           e_gla_norm_g, e_w_out, e_norm2_g, e_ffn_w_gate, e_ffn_w_up, e_ffn_w_down, o_norm1_g, o_w_in,
           o_ret_norm_g, o_rw_mu, o_rw_w_up, o_rw_w0, o_rw_a_up, o_rw_a0, o_rw_g_up, o_rw_k_k, o_rw_k_a, o_rw_r_k,
           o_rw_ln_g, o_rw_ln_b, o_w_out, o_norm2_g, o_moe_router, o_moe_w_gate, o_moe_w_up, o_moe_w_down,
           final_norm_g):
    bsz, t, d = x.shape
    n = bsz * t
    h0 = x.reshape(n, d)

    w = e_w_in[0]
    row = lambda a: a.reshape(1, -1)
    x3 = x
    ng1 = row(e_norm1_g[0])
    w_if = w[:, 2048:2056]
    w_gate = jnp.zeros((d, 128), F32).at[:, :2 * ML_HEADS].set(w_if).astype(BF16)
    w_gate_t = jnp.zeros((16, d), F32).at[:2 * ML_HEADS, :].set(w_if.T).astype(BF16)
    h_ml = _mlstm(x3, ng1, w[:, :4 * ML_W].astype(BF16), w_gate, w_gate_t, row(e_ml_gate_b[0]),
                  e_ml_gate_b[0].reshape(-1, 1), e_ml_conv_w[0], row(e_ml_conv_b[0]), row(e_ml_norm_g[0]))
    gq, gk, gv, gr = 2056, 2312, 2568, 3080
    w_pairs = jnp.stack([jnp.concatenate([w[:, gq + 128 * hp:gq + 128 * (hp + 1)],
                                          w[:, gk + 128 * hp:gk + 128 * (hp + 1)],
                                          w[:, gv + 256 * hp:gv + 256 * (hp + 1)],
                                          w[:, gr + 256 * hp:gr + 256 * (hp + 1)]], axis=1)
                         for hp in range(GLA_HEADS // 2)]).astype(BF16)
    w_low = jnp.zeros((d, 128), F32).at[:, :GLA_RANK].set(w[:, 3592:3608]).astype(BF16)
    o_gla = _gla(x3, ng1, w_pairs, w_low, e_gla_gate_up[0].astype(BF16), row(e_gla_gate_b[0]),
                 row(e_gla_norm_g[0]))
    h2 = _mix_out_ffn(h_ml.reshape(n, -1), o_gla.reshape(n, -1), e_w_out[0].astype(BF16), h0, e_norm2_g[0],
                      e_ffn_w_gate[0].astype(BF16), e_ffn_w_up[0].astype(BF16), e_ffn_w_down[0].astype(BF16))

    w = o_w_in[0]
    h2_3 = h2.reshape(bsz, t, d)
    ng2 = row(o_norm1_g[0])
    y_ret = _retention(h2_3, ng2, w[:, :4 * RET_W].astype(BF16), *_retention_tables(t), row(o_ret_norm_g[0]))
    head_of = jnp.arange(RW_W) // RW_DIM
    bd = (head_of[:, None] == head_of[None, :]).astype(BF16)
    r, lw, k, v, aa, bb, g_out, bonus = _rwkv_prep(
        h2_3, ng2, w[:, 4 * RET_W:].astype(BF16), row(o_rw_mu[0]), o_rw_w_up[0].astype(BF16), row(o_rw_w0[0]),
        o_rw_a_up[0].astype(BF16), row(o_rw_a0[0]), o_rw_g_up[0].astype(BF16), row(o_rw_k_k[0]),
        row(o_rw_k_a[0]), row(o_rw_r_k[0]), bd)
    y_rw = _rwkv_scan(r, lw, k, v, aa, bb, g_out, bonus, row(o_rw_ln_g[0]), row(o_rw_ln_b[0]))
    router_pad = jnp.zeros((d, 128), F32).at[:, :N_EXPERTS].set(o_moe_router[0])
    h3, xn, logits = _mix_out_route(y_ret.reshape(n, -1), y_rw.reshape(n, -1), o_w_out[0].astype(BF16), h2,
                                    o_norm2_g[0], router_pad)
    sorted_tok, sorted_w, items, pos = _route(logits[:, :N_EXPERTS], n)
    xs = xn.at[sorted_tok].get(mode="promise_in_bounds")
    ys = _experts(xs, sorted_w.reshape(-1, 1), *items, o_moe_w_gate[0].astype(BF16),
                  o_moe_w_up[0].astype(BF16), o_moe_w_down[0].astype(BF16))
    y0 = ys.at[pos[:, 0]].get(mode="promise_in_bounds")
    y1 = ys.at[pos[:, 1]].get(mode="promise_in_bounds")
    out = _combine_norm(h3, y0, y1, final_norm_g)
    return out.reshape(bsz, t, d)
```

```python
import functools

import numpy as np
import jax
import jax.numpy as jnp
from jax import lax
from jax.experimental import pallas as pl
from jax.experimental.pallas import tpu as pltpu

F32 = jnp.float32
BF16 = jnp.bfloat16

D_MODEL = 1024
EPS = 1e-6
ML_HEADS, ML_DIM, ML_W, ML_CONV = 4, 128, 512, 4
GLA_HEADS, GLA_DK, GLA_DV, GLA_RANK, GLA_TAU = 4, 64, 128, 16, 16.0
RET_HEADS, RET_DIM, RET_W = 4, 128, 512
ROPE_BASE = 10000.0
RW_HEADS, RW_DIM, RW_W = 8, 64, 512
RW_COLS = 1792
RW_LN_EPS = 64e-5
D_FF = 3584
N_EXPERTS = 8
TOP_K = 2

VMEM_LIMIT = 48 * 1024 * 1024
NEG = -1e30


def _cp(*sem):
    return pltpu.CompilerParams(dimension_semantics=sem, vmem_limit_bytes=VMEM_LIMIT)


def _sigmoid(x):
    return 1.0 / (1.0 + jnp.exp(-x))


def _silu(x):
    return x * _sigmoid(x)


def _log_sigmoid(x):
    return jnp.minimum(x, 0.0) - jnp.log(1.0 + jnp.exp(-jnp.abs(x)))


def _dot(a, b):
    return jnp.dot(a.astype(BF16), b.astype(BF16), preferred_element_type=F32)


def _dot_nt(a, b):
    return lax.dot_general(a.astype(BF16), b.astype(BF16), (((1,), (1,)), ((), ())), preferred_element_type=F32)


def _dot_tn(a, b):
    return jnp.dot(a.T.astype(BF16), b.astype(BF16), preferred_element_type=F32)


def _rms_rows(x, g):
    ms = jnp.mean(x * x, axis=-1, keepdims=True)
    return x * lax.rsqrt(ms + EPS) * g


def _mix_out_ffn(a, b, w_out, resid, g, wg, wu, wd, tm=512, tf=1792):
    n, d = resid.shape
    wa = a.shape[1]
    wb = b.shape[1]
    f = wg.shape[1]
    nf = f // tf

    def body(a_ref, b_ref, wo_ref, r_ref, g_ref, wg_ref, wu_ref, wd_ref, o_ref, xn_ref, acc_ref):
        j = pl.program_id(1)

        @pl.when(j == 0)
        def _():
            h = r_ref[...] + jnp.dot(a_ref[...], wo_ref[0:wa, :], preferred_element_type=F32)
            h = h + jnp.dot(b_ref[...], wo_ref[wa:wa + wb, :], preferred_element_type=F32)
            xn_ref[...] = _rms_rows(h, g_ref[...]).astype(BF16)
            acc_ref[...] = h

        xn = xn_ref[...]
        gg = jnp.dot(xn, wg_ref[...], preferred_element_type=F32)
        uu = jnp.dot(xn, wu_ref[...], preferred_element_type=F32)
        act = (_silu(gg) * uu).astype(BF16)
        acc_ref[...] += jnp.dot(act, wd_ref[...], preferred_element_type=F32)

        @pl.when(j == nf - 1)
        def _():
            o_ref[...] = acc_ref[...]

    return pl.pallas_call(
        body,
        grid=(n // tm, nf),
        in_specs=[pl.BlockSpec((tm, wa), lambda i, j: (i, 0)),
                  pl.BlockSpec((tm, wb), lambda i, j: (i, 0)),
                  pl.BlockSpec((wa + wb, d), lambda i, j: (0, 0)),
                  pl.BlockSpec((tm, d), lambda i, j: (i, 0)),
                  pl.BlockSpec((1, d), lambda i, j: (0, 0)),
                  pl.BlockSpec((d, tf), lambda i, j: (0, j)),
                  pl.BlockSpec((d, tf), lambda i, j: (0, j)),
                  pl.BlockSpec((tf, d), lambda i, j: (j, 0))],
        out_specs=pl.BlockSpec((tm, d), lambda i, j: (i, 0)),
        out_shape=jax.ShapeDtypeStruct((n, d), F32),
        scratch_shapes=[pltpu.VMEM((tm, d), BF16), pltpu.VMEM((tm, d), F32)],
        compiler_params=_cp("parallel", "arbitrary"),
        name="mix_out_ffn",
    )(a, b, w_out, resid, g.reshape(1, d), wg, wu, wd)


ML_L = 256


def _mlstm(x, in_g, w_in, w_gate, w_gate_t, gate_b_row, gate_b_col, conv_w, conv_b, norm_g):
    bsz, t, d = x.shape
    L, D, H, W = ML_L, ML_DIM, ML_HEADS, ML_W

    def body(x_ref, ing_ref, w_ref, wg_ref, wgt_ref, gbr_ref, gbc_ref, cwq_ref, cwk_ref, cbq_ref, cbk_ref,
             ng_ref, out_ref, qext, kext, c_ref, n_ref, m_ref):
        c = pl.program_id(1)

        @pl.when(c == 0)
        def _():
            qext[0:8, :] = jnp.zeros((8, W), F32)
            kext[0:8, :] = jnp.zeros((8, W), F32)
            c_ref[...] = jnp.zeros_like(c_ref)
            n_ref[...] = jnp.zeros_like(n_ref)
            m_ref[...] = jnp.zeros_like(m_ref)

        xn = _rms_rows(x_ref[0], ing_ref[...]).astype(BF16)
        p = jnp.dot(xn, w_ref[...], preferred_element_type=F32)
        g_cols = jnp.dot(xn, wg_ref[...], preferred_element_type=F32)[:, 0:2 * H]
        g_rows = lax.dot_general(wgt_ref[...], xn, (((1,), (1,)), ((), ())),
                                 preferred_element_type=F32)[0:2 * H, :]
        qext[8:, :] = p[:, 0:W]
        kext[8:, :] = p[:, W:2 * W]

        def conv(ext, cw_ref, cb_ref):
            acc = cb_ref[...] + cw_ref[0:1, :] * ext[pl.ds(8 - ML_CONV + 1, L), :]
            for kk in range(1, ML_CONV):
                acc = acc + cw_ref[kk:kk + 1, :] * ext[pl.ds(8 - ML_CONV + 1 + kk, L), :]
            return _silu(acc)

        q_all = conv(qext, cwq_ref, cbq_ref) * (D ** -0.5)
        k_all = conv(kext, cwk_ref, cbk_ref)
        qext[0:8, :] = qext[L:L + 8, :]
        kext[0:8, :] = kext[L:L + 8, :]

        gcol = g_cols + gbr_ref[...]
        grow = g_rows + gbc_ref[...]
        icol = gcol[:, 0:H]
        irow = grow[0:H, :]
        rowi = lax.broadcasted_iota(jnp.int32, (L, 1), 0)
        lanei = lax.broadcasted_iota(jnp.int32, (1, L), 1)
        b_col = _log_sigmoid(gcol[:, H:2 * H])
        b_row = _log_sigmoid(grow[H:2 * H, :])
        steps = [1 << e for e in range(L.bit_length() - 1)]
        for s in steps:
            b_col = b_col + jnp.where(rowi >= s, pltpu.roll(b_col, s, 0), 0.0)
            b_row = b_row + jnp.where(lanei >= s, pltpu.roll(b_row, s, 1), 0.0)
        pmax = icol - b_col
        for s in steps:
            pmax = jnp.maximum(pmax, jnp.where(rowi >= s, pltpu.roll(pmax, s, 0), NEG))
        m_st = m_ref[...]
        g_tot = b_col[L - 1:L, :]
        d_inter = b_col + m_st
        m_rows = jnp.maximum(d_inter, b_col + pmax)
        w_inters = jnp.exp(d_inter - m_rows)
        d_state = g_tot - b_col + icol
        m_news = jnp.maximum(g_tot + m_st, jnp.max(d_state, axis=0, keepdims=True))
        w_states = jnp.exp(d_state - m_news)
        carries = jnp.exp(g_tot + m_st - m_news)
        m_ref[...] = m_news
        e_neg_m = jnp.exp(-m_rows)
        ri = lax.broadcasted_iota(jnp.int32, (L, L), 0)
        ci = lax.broadcasted_iota(jnp.int32, (L, L), 1)
        causal = ri >= ci
        heads = range(H)
        hs = lambda z, h: z[:, h * D:(h + 1) * D]
        cs = lambda z, h: z[:, h:h + 1]
        w_intra = [jnp.exp(jnp.where(causal, cs(b_col, h) - b_row[h:h + 1, :] + irow[h:h + 1, :], NEG)
                           - cs(m_rows, h)) for h in heads]
        w_inter = [cs(w_inters, h) for h in heads]
        w_state = [cs(w_states, h) for h in heads]
        carry = [cs(carries, h) for h in heads]
        qh = [hs(q_all, h) for h in heads]
        kh = [hs(k_all, h) for h in heads]
        vh = [p[:, 2 * W + h * D:2 * W + (h + 1) * D] for h in heads]
        c_st = [c_ref[h] for h in heads]
        n_st = [n_ref[h] for h in heads]
        s = [_dot_nt(qh[h], kh[h]) * w_intra[h] for h in heads]
        qc = [_dot(qh[h], c_st[h]) for h in heads]
        kw = [kh[h] * w_state[h] for h in heads]
        kv = [_dot_tn(kw[h], vh[h]) for h in heads]
        sv = [_dot(s[h], vh[h]) for h in heads]
        ones = jnp.ones((L, 128), BF16)
        ssum = [_dot(s[h], ones)[:, 0:1] for h in heads]
        for h in heads:
            num = sv[h] + w_inter[h] * qc[h]
            den = ssum[h] + w_inter[h] * jnp.sum(qh[h] * n_st[h], axis=1, keepdims=True)
            hval = num / jnp.maximum(jnp.abs(den), cs(e_neg_m, h))
            c_ref[h] = carry[h] * c_st[h] + kv[h]
            n_ref[h] = carry[h] * n_st[h] + jnp.sum(kw[h], axis=0, keepdims=True)
            hg = _sigmoid(p[:, 3 * W + h * D:3 * W + (h + 1) * D]) * hval
            hc = hg - jnp.mean(hg, axis=1, keepdims=True)
            hn = hc * lax.rsqrt(jnp.mean(hc * hc, axis=1, keepdims=True) + EPS)
            out_ref[0, :, h * D:(h + 1) * D] = (hn * ng_ref[:, h * D:(h + 1) * D]).astype(BF16)

    fix = lambda j: (lambda b, c: (0, j))
    return pl.pallas_call(
        body,
        grid=(bsz, t // L),
        in_specs=[pl.BlockSpec((1, L, d), lambda b, c: (b, c, 0)),
                  pl.BlockSpec((1, d), fix(0)),
                  pl.BlockSpec((d, 4 * W), fix(0)),
                  pl.BlockSpec((d, 128), fix(0)),
                  pl.BlockSpec((16, d), fix(0)),
                  pl.BlockSpec((1, 2 * H), fix(0)),
                  pl.BlockSpec((2 * H, 1), fix(0)),
                  pl.BlockSpec((ML_CONV, W), fix(0)),
                  pl.BlockSpec((ML_CONV, W), fix(1)),
                  pl.BlockSpec((1, W), fix(0)),
                  pl.BlockSpec((1, W), fix(1)),
                  pl.BlockSpec((1, W), fix(0))],
        out_specs=pl.BlockSpec((1, L, W), lambda b, c: (b, c, 0)),
        out_shape=jax.ShapeDtypeStruct((bsz, t, W), BF16),
        scratch_shapes=[pltpu.VMEM((L + 8, W), F32), pltpu.VMEM((L + 8, W), F32),
                        pltpu.VMEM((H, D, D), F32), pltpu.VMEM((H, 1, D), F32), pltpu.VMEM((1, H), F32)],
        compiler_params=_cp("parallel", "arbitrary"),
        name="mlstm",
    )(x, in_g, w_in, w_gate, w_gate_t, gate_b_row, gate_b_col, conv_w, conv_w, conv_b, conv_b, norm_g)


GLA_TC = 256
GLA_SUB = 16
GLA_GROUP = 128


def _gla(x, in_g, w_pairs, w_low, gate_up, gate_b, norm_g):
    bsz, t, d = x.shape
    tc, S, GB = GLA_TC, GLA_SUB, GLA_GROUP
    head_ones = (jnp.arange(2 * GLA_DK)[:, None] // GLA_DK == jnp.arange(2 * GB)[None, :] // GB).astype(BF16)
    nsub = tc // S
    dk, dv = GLA_DK, GLA_DV

    def body(x_ref, ing_ref, w_ref, wl_ref, gu_ref, gbias_ref, ng_ref, ones_ref, out_ref, st_ref, ksh, bsh):
        c = pl.program_id(2)

        @pl.when(c == 0)
        def _():
            st_ref[...] = jnp.zeros_like(st_ref)
            ksh[0:S, :] = jnp.zeros((S, 2 * dk), F32)
            bsh[0:S, :] = jnp.zeros((S, 2 * dk), F32)

        xn = _rms_rows(x_ref[0], ing_ref[...]).astype(BF16)
        p = jnp.dot(xn, w_ref[0], preferred_element_type=F32)
        g_low = jnp.dot(xn, wl_ref[...], preferred_element_type=F32)[:, 0:GLA_RANK]
        z = _dot(g_low, gu_ref[...]) + gbias_ref[...]
        la = _log_sigmoid(z) / GLA_TAU
        rowi = lax.broadcasted_iota(jnp.int32, (tc, 1), 0)
        rmod = rowi & (S - 1)
        bcum = la
        rsum = la
        for s in (1, 2, 4, 8):
            bcum = bcum + jnp.where(rmod >= s, pltpu.roll(bcum, s, 0), 0.0)
            rsum = rsum + jnp.where(rmod < S - s, pltpu.roll(rsum, tc - s, 0), 0.0)
        q = p[:, 0:2 * dk] * (dk ** -0.5)
        k = p[:, 2 * dk:4 * dk]
        v = p[:, 4 * dk:4 * dk + 2 * dv]
        gate = p[:, 4 * dk + 2 * dv:4 * dk + 4 * dv]
        qt = q * jnp.exp(bcum)
        kt = k * jnp.exp(rsum - la)
        eg = jnp.exp(bcum + rsum - la)

        ksh[S:, :] = k
        bsh[S:, :] = bcum
        prods = []
        for d in range(S):
            kd = k if d == 0 else ksh[pl.ds(S - d, tc), :]
            bd = bcum if d == 0 else bsh[pl.ds(S - d, tc), :]
            e = jnp.exp(jnp.where(rmod >= d, bcum - bd, 0.0))
            prods.append((q * kd * e).astype(BF16))
        ws = [jnp.dot(p, ones_ref[...], preferred_element_type=F32) for p in prods]
        coli = lax.broadcasted_iota(jnp.int32, (tc, GB), 1)
        rgrp = rowi & (GB - 1)
        att0 = jnp.zeros((tc, GB), F32)
        att1 = jnp.zeros((tc, GB), F32)
        for d in range(S):
            here = jnp.logical_and(coli == rgrp - d, rmod >= d)
            att0 = jnp.where(here, ws[d][:, 0:GB], att0)
            att1 = jnp.where(here, ws[d][:, GB:2 * GB], att1)

        heads = range(2)
        lk = [slice(hh * dk, (hh + 1) * dk) for hh in heads]
        lv = [slice(hh * dv, (hh + 1) * dv) for hh in heads]
        kv = [[_dot_tn(v[si * S:(si + 1) * S, lv[hh]], kt[si * S:(si + 1) * S, lk[hh]]) for hh in heads]
              for si in range(nsub)]
        st = [st_ref[hh] for hh in heads]
        inter = [[], []]
        for si in range(nsub):
            rows = slice(si * S, (si + 1) * S)
            for hh in heads:
                inter[hh].append(_dot_nt(qt[rows, lk[hh]], st[hh]))
                st[hh] = st[hh] * eg[si * S:si * S + 1, lk[hh]] + kv[si][hh]
        for hh in heads:
            st_ref[hh] = st[hh]

        for hh, att in ((0, att0), (1, att1)):
            diag = jnp.concatenate([_dot(att[g * GB:(g + 1) * GB], v[g * GB:(g + 1) * GB, lv[hh]])
                                    for g in range(tc // GB)], axis=0)
            o = diag + jnp.concatenate(inter[hh], axis=0)
            on = o * lax.rsqrt(jnp.mean(o * o, axis=1, keepdims=True) + EPS)
            out_ref[0, :, lv[hh]] = (on * ng_ref[:, lv[hh]] * _silu(gate[:, lv[hh]])).astype(BF16)

    pw = 4 * dk + 4 * dv
    return pl.pallas_call(
        body,
        grid=(bsz, GLA_HEADS // 2, t // tc),
        in_specs=[pl.BlockSpec((1, tc, d), lambda b, h, c: (b, c, 0)),
                  pl.BlockSpec((1, d), lambda b, h, c: (0, 0)),
                  pl.BlockSpec((1, d, pw), lambda b, h, c: (h, 0, 0)),
                  pl.BlockSpec((d, 128), lambda b, h, c: (0, 0)),
                  pl.BlockSpec((GLA_RANK, 2 * dk), lambda b, h, c: (0, h)),
                  pl.BlockSpec((1, 2 * dk), lambda b, h, c: (0, h)),
                  pl.BlockSpec((1, 2 * dv), lambda b, h, c: (0, h)),
                  pl.BlockSpec((2 * dk, 2 * GB), lambda b, h, c: (0, 0))],
        out_specs=pl.BlockSpec((1, tc, 2 * dv), lambda b, h, c: (b, c, h)),
        out_shape=jax.ShapeDtypeStruct((bsz, t, GLA_HEADS * dv), BF16),
        scratch_shapes=[pltpu.VMEM((2, dv, dk), F32), pltpu.VMEM((tc + S, 2 * dk), F32),
                        pltpu.VMEM((tc + S, 2 * dk), F32)],
        compiler_params=_cp("parallel", "parallel", "arbitrary"),
        name="gla",
    )(x, in_g, w_pairs, w_low, gate_up, gate_b, norm_g, head_ones)


RET_L = 256


def _retention(x, in_g, w_in, cos_t, sin_t, intra, inter, sdec, cdec, norm_g):
    bsz, t, d = x.shape
    L, D, H, W = RET_L, RET_DIM, RET_HEADS, RET_W

    def body(x_ref, ing_ref, w_ref, cos_ref, sin_ref, intra_ref, inter_ref, sdec_ref, cdec_ref, ng_ref,
             out_ref, s_ref):
        c = pl.program_id(1)

        @pl.when(c == 0)
        def _():
            s_ref[...] = jnp.zeros_like(s_ref)

        p = jnp.dot(_rms_rows(x_ref[0], ing_ref[...]).astype(BF16), w_ref[...], preferred_element_type=F32)
        cs = cos_ref[...]
        sn = sin_ref[...]

        def rot(z):
            return z * cs + pltpu.roll(z, D // 2, 1) * sn

        heads = range(H)
        q = [rot(p[:, h * D:(h + 1) * D]) * (D ** -0.5) for h in heads]
        k = [rot(p[:, W + h * D:W + (h + 1) * D]) for h in heads]
        v = [p[:, 2 * W + h * D:2 * W + (h + 1) * D] for h in heads]
        s_st = [s_ref[h] for h in heads]
        s = [_dot_nt(q[h], k[h]) * intra_ref[h] for h in heads]
        qs = [_dot(q[h], s_st[h]) for h in heads]
        kv = [_dot_tn(k[h] * sdec_ref[h], v[h]) for h in heads]
        sv = [_dot(s[h], v[h]) for h in heads]
        for h in heads:
            o = sv[h] + inter_ref[h] * qs[h]
            s_ref[h] = cdec_ref[h] * s_st[h] + kv[h]
            oc = o - jnp.mean(o, axis=1, keepdims=True)
            on = oc * lax.rsqrt(jnp.mean(oc * oc, axis=1, keepdims=True) + EPS)
            hl = slice(h * D, (h + 1) * D)
            gate = p[:, 3 * W + h * D:3 * W + (h + 1) * D]
            out_ref[0, :, hl] = (on * ng_ref[:, hl] * _silu(gate)).astype(BF16)

    fix2 = lambda b, c: (0, 0)
    fix3 = lambda b, c: (0, 0, 0)
    return pl.pallas_call(
        body,
        grid=(bsz, t // L),
        in_specs=[pl.BlockSpec((1, L, d), lambda b, c: (b, c, 0)),
                  pl.BlockSpec((1, d), fix2),
                  pl.BlockSpec((d, 4 * W), fix2),
                  pl.BlockSpec((L, D), lambda b, c: (c, 0)),
                  pl.BlockSpec((L, D), lambda b, c: (c, 0)),
                  pl.BlockSpec((H, L, L), fix3),
                  pl.BlockSpec((H, L, 1), fix3),
                  pl.BlockSpec((H, L, 1), fix3),
                  pl.BlockSpec((H, 1, 1), fix3),
                  pl.BlockSpec((1, W), lambda b, c: (0, 0))],
        out_specs=pl.BlockSpec((1, L, W), lambda b, c: (b, c, 0)),
        out_shape=jax.ShapeDtypeStruct((bsz, t, W), BF16),
        scratch_shapes=[pltpu.VMEM((H, D, D), F32)],
        compiler_params=_cp("parallel", "arbitrary"),
        name="retention",
    )(x, in_g, w_in, cos_t, sin_t, intra, inter, sdec, cdec, norm_g)


def _retention_tables(t):
    L, D = RET_L, RET_DIM
    inv = ROPE_BASE ** (-jnp.arange(0, D, 2, dtype=F32) / D)
    ang = jnp.arange(t).astype(F32)[:, None] * inv[None, :]
    cos = jnp.cos(ang)
    sin = jnp.sin(ang)
    cos_t = jnp.concatenate([cos, cos], axis=-1)
    sin_t = jnp.concatenate([-sin, sin], axis=-1)
    log_gamma = jnp.log1p(-jnp.exp2(-5.0 - jnp.arange(RET_HEADS, dtype=F32)))
    idx = jnp.arange(L, dtype=F32)
    causal = idx[:, None] >= idx[None, :]
    rel = jnp.where(causal, idx[:, None] - idx[None, :], 0.0)
    intra = jnp.where(causal, jnp.exp(log_gamma[:, None, None] * rel), 0.0)
    inter = jnp.exp(log_gamma[:, None] * (idx + 1.0))[:, :, None]
    sdec = jnp.exp(log_gamma[:, None] * (L - 1.0 - idx))[:, :, None]
    cdec = jnp.exp(log_gamma * L)[:, None, None]
    return cos_t, sin_t, intra, inter, sdec, cdec


RWP_TM = 256


def _seg_sum(x, bd):
    hi = x.astype(BF16)
    lo = (x - hi.astype(F32)).astype(BF16)
    return jnp.dot(hi, bd, preferred_element_type=F32) + jnp.dot(lo, bd, preferred_element_type=F32)


def _rwkv_prep(x, in_g, w_in, mu, w_up, w0, a_up, a0, g_up, k_k, k_a, r_k, bd):
    bsz, t, d = x.shape
    tm, W = RWP_TM, RW_W

    def body(x_ref, ing_ref, win_ref, mu_ref, wup_ref, w0_ref, aup_ref, a0_ref, gup_ref, kk_ref, ka_ref, rk_ref,
             bd_ref, r_out, lw_out, k_out, v_out, a_out, b_out, g_out, bonus_out, last_ref):
        c = pl.program_id(1)

        @pl.when(c == 0)
        def _():
            last_ref[...] = jnp.zeros_like(last_ref)

        cur = jnp.dot(_rms_rows(x_ref[0], ing_ref[...]).astype(BF16), win_ref[...], preferred_element_type=F32)
        prev = last_ref[...]
        last_ref[...] = cur[tm - 1:tm, :]
        rowi = lax.broadcasted_iota(jnp.int32, (tm, 1), 0)
        sh = jnp.where(rowi == 0, prev, pltpu.roll(cur, 1, 0))
        xm = cur + (sh - cur) * mu_ref[...]
        x_r = xm[:, 0:W]
        x_k = xm[:, W:2 * W]
        x_v = xm[:, 2 * W:3 * W]
        x_dl = xm[:, 3 * W:3 * W + 64]
        x_al = xm[:, 3 * W + 64:3 * W + 128]
        x_gl = xm[:, 3 * W + 128:3 * W + 256]
        wl = w0_ref[...] + _dot(jnp.tanh(x_dl), wup_ref[...])
        sp = jnp.maximum(-wl, 0.0) + jnp.log(1.0 + jnp.exp(-jnp.abs(wl)))
        lw_out[0] = -jnp.exp(-sp - 0.5)
        a = _sigmoid(a0_ref[...] + _dot(x_al, aup_ref[...]))
        g_out[0] = _dot(_sigmoid(x_gl), gup_ref[...]).astype(BF16)
        kk0 = x_k * kk_ref[...]
        nrm = jnp.sqrt(_seg_sum(kk0 * kk0, bd_ref[...]))
        kk = kk0 / jnp.maximum(nrm, 1e-12)
        k_h = x_k * (1.0 + (a - 1.0) * ka_ref[...])
        r_out[0] = x_r.astype(BF16)
        k_out[0] = k_h.astype(BF16)
        v_out[0] = x_v.astype(BF16)
        a_out[0] = (-kk).astype(BF16)
        b_out[0] = (kk * a).astype(BF16)
        bonus_out[0] = (_seg_sum(x_r * k_h * rk_ref[...], bd_ref[...]) * x_v).astype(BF16)

    row = lambda b, c: (0, 0)
    blk = pl.BlockSpec((1, tm, W), lambda b, c: (b, c, 0))
    shp = [jax.ShapeDtypeStruct((bsz, t, W), F32 if i == 1 else BF16) for i in range(8)]
    return pl.pallas_call(
        body,
        grid=(bsz, t // tm),
        in_specs=[pl.BlockSpec((1, tm, d), lambda b, c: (b, c, 0)),
                  pl.BlockSpec((1, d), row),
                  pl.BlockSpec((d, RW_COLS), row),
                  pl.BlockSpec((1, RW_COLS), row),
                  pl.BlockSpec((64, W), row), pl.BlockSpec((1, W), row),
                  pl.BlockSpec((64, W), row), pl.BlockSpec((1, W), row),
                  pl.BlockSpec((128, W), row),
                  pl.BlockSpec((1, W), row), pl.BlockSpec((1, W), row), pl.BlockSpec((1, W), row),
                  pl.BlockSpec((W, W), row)],
        out_specs=[blk] * 8,
        out_shape=shp,
        scratch_shapes=[pltpu.VMEM((1, RW_COLS), F32)],
        compiler_params=_cp("parallel", "arbitrary"),
        name="rwkv_prep",
    )(x, in_g, w_in, mu, w_up, w0, a_up, a0, g_up, k_k, k_a, r_k, bd)


RW_L = 64
RW_TB = 128


def _rwkv_scan(r, lw, k, v, aa, bb, g_out, bonus, ln_g, ln_b):
    bsz, t, W = r.shape
    L, N, tb = RW_L, RW_DIM, RW_TB
    nck = tb // L

    def body(r_ref, lw_ref, k_ref, v_ref, a_ref, b_ref, g_ref, bonus_ref, lng_ref, lnb_ref, out_ref, h_ref):
        c = pl.program_id(1)

        @pl.when(c == 0)
        def _():
            h_ref[...] = jnp.zeros_like(h_ref)

        lw_all = lw_ref[0]
        rowi = lax.broadcasted_iota(jnp.int32, (tb, 1), 0) & (L - 1)
        cl = lw_all
        for s in (1, 2, 4, 8, 16, 32):
            cl = cl + jnp.where(rowi >= s, pltpu.roll(cl, s, 0), 0.0)
        cl_last = jnp.concatenate([jnp.broadcast_to(cl[(cc + 1) * L - 1:(cc + 1) * L, :], (L, W))
                                   for cc in range(nck)], axis=0)
        e_inv = jnp.exp(-cl)
        e_end = jnp.exp(cl_last - cl)
        p_end = jnp.exp(cl_last)
        at = a_ref[0] * jnp.exp(cl - lw_all)
        rt = r_ref[0] * jnp.exp(cl)
        bt = b_ref[0] * e_inv
        kt = k_ref[0] * e_inv
        b_end = b_ref[0] * e_end
        k_end = k_ref[0] * e_end
        v_all = v_ref[0]
        ri = lax.broadcasted_iota(jnp.int32, (L, L), 0)
        ci = lax.broadcasted_iota(jnp.int32, (L, L), 1)
        strict = ri > ci
        lower = ri >= ci
        eye = ri == ci
        heads = range(nck * RW_HEADS)

        def hs(z, u):
            cc, hh = divmod(u, RW_HEADS)
            return z[cc * L:(cc + 1) * L, hh * N:(hh + 1) * N]

        m = [_dot_nt(jnp.concatenate([hs(at, h), hs(rt, h)], axis=0),
                     jnp.concatenate([hs(bt, h), hs(kt, h)], axis=0)) for h in heads]
        a_ab = [jnp.where(strict, m[h][0:L, 0:L], 0.0) for h in heads]
        a_ak = [jnp.where(strict, m[h][0:L, L:2 * L], 0.0) for h in heads]
        a_rb = [jnp.where(lower, m[h][L:2 * L, 0:L], 0.0) for h in heads]
        a_rk = [jnp.where(lower, m[h][L:2 * L, L:2 * L], 0.0) for h in heads]
        end_t = [jnp.concatenate([hs(b_end, h), hs(k_end, h)], axis=1).T for h in heads]
        vp = [_dot(jnp.concatenate([a_ak[h], a_rk[h], end_t[h][N:2 * N]], axis=0), hs(v_all, h)) for h in heads]
        x = [jnp.concatenate([hs(at, h), vp[h][0:L]], axis=1) for h in heads]
        ap = a_ab
        for it in range(6):
            x = [x[h] + _dot(ap[h], x[h]) for h in heads]
            if it < 5:
                ap = [_dot(ap[h], ap[h]) for h in heads]
        post = [_dot(jnp.concatenate([a_rb[h], end_t[h][0:N]], axis=0), x[h]) for h in heads]
        lhs = []
        y0s = []
        h_adds = []
        for u in heads:
            q_hat = hs(rt, u) + post[u][0:L, 0:N]
            gmat = jnp.where(eye, hs(p_end, u), 0.0) + post[u][L:L + N, 0:N]
            lhs.append(jnp.concatenate([q_hat, gmat], axis=0))
            y0s.append(post[u][0:L, N:2 * N] + vp[u][L:2 * L])
            h_adds.append(post[u][L:L + N, N:2 * N] + vp[u][2 * L:2 * L + N])
        h_st = [h_ref[hh] for hh in range(RW_HEADS)]
        for cc in range(nck):
            outs = []
            for hh in range(RW_HEADS):
                u = cc * RW_HEADS + hh
                res = _dot(lhs[u], h_st[hh])
                y = res[0:L] + y0s[u]
                h_st[hh] = res[L:L + N] + h_adds[u]
                yc = y - jnp.mean(y, axis=1, keepdims=True)
                outs.append(yc * lax.rsqrt(jnp.mean(yc * yc, axis=1, keepdims=True) + RW_LN_EPS))
            yn = jnp.concatenate(outs, axis=1)
            rows = slice(cc * L, (cc + 1) * L)
            out_ref[0, rows, :] = ((yn * lng_ref[...] + lnb_ref[...] + bonus_ref[0, rows, :])
                                   * g_ref[0, rows, :]).astype(BF16)
        for hh in range(RW_HEADS):
            h_ref[hh] = h_st[hh]

    blk = pl.BlockSpec((1, tb, W), lambda b, c: (b, c, 0))
    vec = pl.BlockSpec((1, W), lambda b, c: (0, 0))
    return pl.pallas_call(
        body,
        grid=(bsz, t // tb),
        in_specs=[blk] * 8 + [vec, vec],
        out_specs=blk,
        out_shape=jax.ShapeDtypeStruct((bsz, t, W), BF16),
        scratch_shapes=[pltpu.VMEM((RW_HEADS, N, N), F32)],
        compiler_params=_cp("parallel", "arbitrary"),
        name="rwkv_scan",
    )(r, lw, k, v, aa, bb, g_out, bonus, ln_g, ln_b)


def _mix_out_route(a, b, w_bf16, resid, g, router_split, tm=1024):
    n, wa = a.shape
    wb = b.shape[1]
    d = w_bf16.shape[1]
    e = router_split.shape[2]

    def body(a_ref, b_ref, w_ref, r_ref, g_ref, rt_ref, h_ref, xn_ref, lg_ref):
        acc = jnp.dot(a_ref[...], w_ref[0:wa, :], preferred_element_type=F32)
        acc = acc + jnp.dot(b_ref[...], w_ref[wa:wa + wb, :], preferred_element_type=F32)
        h = r_ref[...] + acc
        h_ref[...] = h
        xn = _rms_rows(h, g_ref[...])
        x_hi = xn.astype(BF16)
        xn_ref[...] = x_hi
        x_lo = (xn - x_hi.astype(F32)).astype(BF16)
        r_hi = rt_ref[0]
        r_lo = rt_ref[1]
        lg_ref[...] = (jnp.dot(x_hi, r_hi, preferred_element_type=F32)
                       + (jnp.dot(x_lo, r_hi, preferred_element_type=F32)
                          + jnp.dot(x_hi, r_lo, preferred_element_type=F32)))

    return pl.pallas_call(
        body,
        grid=(n // tm,),
        in_specs=[pl.BlockSpec((tm, wa), lambda i: (i, 0)),
                  pl.BlockSpec((tm, wb), lambda i: (i, 0)),
                  pl.BlockSpec((wa + wb, d), lambda i: (0, 0)),
                  pl.BlockSpec((tm, d), lambda i: (i, 0)),
                  pl.BlockSpec((1, d), lambda i: (0, 0)),
                  pl.BlockSpec((2, d, e), lambda i: (0, 0, 0))],
        out_specs=[pl.BlockSpec((tm, d), lambda i: (i, 0)), pl.BlockSpec((tm, d), lambda i: (i, 0)),
                   pl.BlockSpec((tm, e), lambda i: (i, 0))],
        out_shape=[jax.ShapeDtypeStruct((n, d), F32), jax.ShapeDtypeStruct((n, d), BF16),
                   jax.ShapeDtypeStruct((n, e), F32)],
        compiler_params=_cp("parallel"),
        name="mix_out_route",
    )(a, b, w_bf16, resid, g.reshape(1, d), router_split)


MOE_TM = 512
MOE_TF = 1792


def _experts(xs, row_w, item_tile, item_exp, item_lo, item_hi, wg, wu, wd):
    nrows, d = xs.shape
    tm, tf = MOE_TM, MOE_TF
    nf = D_FF // tf
    n_items = item_tile.shape[0]

    def body(it_ref, ie_ref, lo_ref, hi_ref, x_ref, w_ref, wg_ref, wu_ref, wd_ref, o_ref, acc_ref):
        i = pl.program_id(0)
        j = pl.program_id(1)
        tile = it_ref[i]
        first = jnp.logical_or(i == 0, tile != it_ref[jnp.maximum(i - 1, 0)])
        last = jnp.logical_or(i == n_items - 1, tile != it_ref[jnp.minimum(i + 1, n_items - 1)])

        @pl.when(jnp.logical_and(first, j == 0))
        def _():
            acc_ref[...] = jnp.zeros_like(acc_ref)

        lo = lo_ref[i]
        hi = hi_ref[i]

        @pl.when(lo < hi)
        def _():
            x = x_ref[...]
            gg = jnp.dot(x, wg_ref[0], preferred_element_type=F32)
            uu = jnp.dot(x, wu_ref[0], preferred_element_type=F32)
            act = (_silu(gg) * uu).astype(BF16)
            part = jnp.dot(act, wd_ref[0], preferred_element_type=F32)
            rowi = lax.broadcasted_iota(jnp.int32, (tm, 1), 0)
            mine = jnp.logical_and(rowi >= lo, rowi < hi)
            acc_ref[...] += part * jnp.where(mine, w_ref[...], 0.0)

        @pl.when(jnp.logical_and(last, j == nf - 1))
        def _():
            o_ref[...] = acc_ref[...].astype(o_ref.dtype)

    grid_spec = pltpu.PrefetchScalarGridSpec(
        num_scalar_prefetch=4,
        grid=(n_items, nf),
        in_specs=[pl.BlockSpec((tm, d), lambda i, j, it, ie, lo, hi: (it[i], 0)),
                  pl.BlockSpec((tm, 1), lambda i, j, it, ie, lo, hi: (it[i], 0)),
                  pl.BlockSpec((1, d, tf), lambda i, j, it, ie, lo, hi: (ie[i], 0, j)),
                  pl.BlockSpec((1, d, tf), lambda i, j, it, ie, lo, hi: (ie[i], 0, j)),
                  pl.BlockSpec((1, tf, d), lambda i, j, it, ie, lo, hi: (ie[i], j, 0))],
        out_specs=pl.BlockSpec((tm, d), lambda i, j, it, ie, lo, hi: (it[i], 0)),
        scratch_shapes=[pltpu.VMEM((tm, d), F32)],
    )
    return pl.pallas_call(
        body,
        grid_spec=grid_spec,
        out_shape=jax.ShapeDtypeStruct((nrows, d), BF16),
        compiler_params=_cp("arbitrary", "arbitrary"),
        name="moe_experts",
    )(item_tile, item_exp, item_lo, item_hi, xs, row_w, wg, wu, wd)


def _combine_norm(h, y0, y1, g, tm=512):
    n, d = h.shape

    def body(h_ref, a_ref, b_ref, g_ref, o_ref):
        o_ref[...] = _rms_rows(h_ref[...] + (a_ref[...].astype(F32) + b_ref[...].astype(F32)), g_ref[...])

    blk = pl.BlockSpec((tm, d), lambda i: (i, 0))
    return pl.pallas_call(
        body,
        grid=(n // tm,),
        in_specs=[blk, blk, blk, pl.BlockSpec((1, d), lambda i: (0, 0))],
        out_specs=blk,
        out_shape=jax.ShapeDtypeStruct((n, d), F32),
        compiler_params=_cp("parallel"),
        name="combine_norm",
    )(h, y0, y1, g.reshape(1, d))


def _route(logits, n):
    tm = MOE_TM
    na = n * TOP_K
    n_tiles = na // tm
    top_val, top_idx = lax.top_k(logits, TOP_K)
    top_w = jax.nn.softmax(top_val, axis=-1)
    e_flat = top_idx.reshape(-1).astype(jnp.int32)
    w_flat = top_w.reshape(-1)
    tok = jnp.arange(na, dtype=jnp.int32) // TOP_K
    _, sorted_tok, sorted_w = lax.sort((e_flat, tok, w_flat), num_keys=1, is_stable=True)
    onehot = (e_flat[:, None] == jnp.arange(N_EXPERTS, dtype=jnp.int32)[None, :]).astype(jnp.int32)
    rank = jnp.take_along_axis(jnp.cumsum(onehot, axis=0), e_flat[:, None], axis=1)[:, 0] - 1
    counts = jnp.sum(onehot, axis=0)
    ends = jnp.cumsum(counts)
    pos = (ends - counts)[e_flat] + rank
    cuts = jnp.sort(jnp.concatenate([jnp.arange(n_tiles, dtype=jnp.int32) * tm, ends[:-1].astype(jnp.int32)]))
    nxt = jnp.concatenate([cuts[1:], jnp.full((1,), na, jnp.int32)])
    item_tile = jnp.minimum(cuts // tm, n_tiles - 1)
    item_exp = jnp.minimum(jnp.searchsorted(ends, cuts, side="right"), N_EXPERTS - 1).astype(jnp.int32)
    item_lo = cuts - item_tile * tm
    item_hi = nxt - item_tile * tm
    return sorted_tok, sorted_w, (item_tile, item_exp, item_lo, item_hi), pos.reshape(n, TOP_K)


def kernel(x, e_norm1_g, e_w_in, e_ml_conv_w, e_ml_conv_b, e_ml_gate_b, e_ml_norm_g, e_gla_gate_up, e_gla_gate_b,
           e_gla_norm_g, e_w_out, e_norm2_g, e_ffn_w_gate, e_ffn_w_up, e_ffn_w_down, o_norm1_g, o_w_in,
           o_ret_norm_g, o_rw_mu, o_rw_w_up, o_rw_w0, o_rw_a_up, o_rw_a0, o_rw_g_up, o_rw_k_k, o_rw_k_a, o_rw_r_k,
           o_rw_ln_g, o_rw_ln_b, o_w_out, o_norm2_g, o_moe_router, o_moe_w_gate, o_moe_w_up, o_moe_w_down,
           final_norm_g):
    bsz, t, d = x.shape
    n = bsz * t
    h0 = x.reshape(n, d)

    w = e_w_in[0]
    row = lambda a: a.reshape(1, -1)
    x3 = x
    ng1 = row(e_norm1_g[0])
    w_if = w[:, 2048:2056]
    w_gate = jnp.zeros((d, 128), F32).at[:, :2 * ML_HEADS].set(w_if).astype(BF16)
    w_gate_t = jnp.zeros((16, d), F32).at[:2 * ML_HEADS, :].set(w_if.T).astype(BF16)
    h_ml = _mlstm(x3, ng1, w[:, :4 * ML_W].astype(BF16), w_gate, w_gate_t, row(e_ml_gate_b[0]),
                  e_ml_gate_b[0].reshape(-1, 1), e_ml_conv_w[0], row(e_ml_conv_b[0]), row(e_ml_norm_g[0]))
    gq, gk, gv, gr = 2056, 2312, 2568, 3080
    w_pairs = jnp.stack([jnp.concatenate([w[:, gq + 128 * hp:gq + 128 * (hp + 1)],
                                          w[:, gk + 128 * hp:gk + 128 * (hp + 1)],
                                          w[:, gv + 256 * hp:gv + 256 * (hp + 1)],
                                          w[:, gr + 256 * hp:gr + 256 * (hp + 1)]], axis=1)
                         for hp in range(GLA_HEADS // 2)]).astype(BF16)
    w_low = jnp.zeros((d, 128), F32).at[:, :GLA_RANK].set(w[:, 3592:3608]).astype(BF16)
    o_gla = _gla(x3, ng1, w_pairs, w_low, e_gla_gate_up[0].astype(BF16), row(e_gla_gate_b[0]),
                 row(e_gla_norm_g[0]))
    h2 = _mix_out_ffn(h_ml.reshape(n, -1), o_gla.reshape(n, -1), e_w_out[0].astype(BF16), h0, e_norm2_g[0],
                      e_ffn_w_gate[0].astype(BF16), e_ffn_w_up[0].astype(BF16), e_ffn_w_down[0].astype(BF16))

    w = o_w_in[0]
    h2_3 = h2.reshape(bsz, t, d)
    ng2 = row(o_norm1_g[0])
    y_ret = _retention(h2_3, ng2, w[:, :4 * RET_W].astype(BF16), *_retention_tables(t), row(o_ret_norm_g[0]))
    head_of = jnp.arange(RW_W) // RW_DIM
    bd = (head_of[:, None] == head_of[None, :]).astype(BF16)
    r, lw, k, v, aa, bb, g_out, bonus = _rwkv_prep(
        h2_3, ng2, w[:, 4 * RET_W:].astype(BF16), row(o_rw_mu[0]), o_rw_w_up[0].astype(BF16), row(o_rw_w0[0]),
        o_rw_a_up[0].astype(BF16), row(o_rw_a0[0]), o_rw_g_up[0].astype(BF16), row(o_rw_k_k[0]),
        row(o_rw_k_a[0]), row(o_rw_r_k[0]), bd)
    y_rw = _rwkv_scan(r, lw, k, v, aa, bb, g_out, bonus, row(o_rw_ln_g[0]), row(o_rw_ln_b[0]))
    router_pad = jnp.zeros((d, 128), F32).at[:, :N_EXPERTS].set(o_moe_router[0])
    router_hi = router_pad.astype(BF16)
    router_split = jnp.stack([router_hi, (router_pad - router_hi.astype(F32)).astype(BF16)])
    h3, xn, logits = _mix_out_route(y_ret.reshape(n, -1), y_rw.reshape(n, -1), o_w_out[0].astype(BF16), h2,
                                    o_norm2_g[0], router_split)
    sorted_tok, sorted_w, items, pos = _route(logits[:, :N_EXPERTS], n)
    xs = xn.at[sorted_tok].get(mode="promise_in_bounds")
    ys = _experts(xs, sorted_w.reshape(-1, 1), *items, o_moe_w_gate[0].astype(BF16),
                  o_moe_w_up[0].astype(BF16), o_moe_w_down[0].astype(BF16))
    y0 = ys.at[pos[:, 0]].get(mode="promise_in_bounds")
    y1 = ys.at[pos[:, 1]].get(mode="promise_in_bounds")
    out = _combine_norm(h3, y0, y1, final_norm_g)
    return out.reshape(bsz, t, d)
```

```python
import functools

import numpy as np
import jax
import jax.numpy as jnp
from jax import lax
from jax.experimental import pallas as pl
from jax.experimental.pallas import tpu as pltpu

F32 = jnp.float32
BF16 = jnp.bfloat16

D_MODEL = 1024
EPS = 1e-6
ML_HEADS, ML_DIM, ML_W, ML_CONV = 4, 128, 512, 4
GLA_HEADS, GLA_DK, GLA_DV, GLA_RANK, GLA_TAU = 4, 64, 128, 16, 16.0
RET_HEADS, RET_DIM, RET_W = 4, 128, 512
ROPE_BASE = 10000.0
RW_HEADS, RW_DIM, RW_W = 8, 64, 512
RW_COLS = 1792
RW_LN_EPS = 64e-5
D_FF = 3584
N_EXPERTS = 8
TOP_K = 2

VMEM_LIMIT = 48 * 1024 * 1024
NEG = -1e30


def _cp(*sem):
    return pltpu.CompilerParams(dimension_semantics=sem, vmem_limit_bytes=VMEM_LIMIT)


def _sigmoid(x):
    return 1.0 / (1.0 + jnp.exp(-x))


def _silu(x):
    return x * _sigmoid(x)


def _log_sigmoid(x):
    return jnp.minimum(x, 0.0) - jnp.log(1.0 + jnp.exp(-jnp.abs(x)))


def _dot(a, b):
    return jnp.dot(a.astype(BF16), b.astype(BF16), preferred_element_type=F32)


def _dot_nt(a, b):
    return lax.dot_general(a.astype(BF16), b.astype(BF16), (((1,), (1,)), ((), ())), preferred_element_type=F32)


def _dot_tn(a, b):
    return jnp.dot(a.T.astype(BF16), b.astype(BF16), preferred_element_type=F32)


def _rms_rows(x, g):
    ms = jnp.mean(x * x, axis=-1, keepdims=True)
    return x * lax.rsqrt(ms + EPS) * g


def _mix_out_ffn(a, b, w_out, resid, g, wg, wu, wd, tm=512, tf=1792):
    n, d = resid.shape
    wa = a.shape[1]
    wb = b.shape[1]
    f = wg.shape[1]
    nf = f // tf

    def body(a_ref, b_ref, wo_ref, r_ref, g_ref, wg_ref, wu_ref, wd_ref, o_ref, xn_ref, acc_ref):
        j = pl.program_id(1)

        @pl.when(j == 0)
        def _():
            h = r_ref[...] + jnp.dot(a_ref[...], wo_ref[0:wa, :], preferred_element_type=F32)
            h = h + jnp.dot(b_ref[...], wo_ref[wa:wa + wb, :], preferred_element_type=F32)
            xn_ref[...] = _rms_rows(h, g_ref[...]).astype(BF16)
            acc_ref[...] = h

        xn = xn_ref[...]
        gg = jnp.dot(xn, wg_ref[...], preferred_element_type=F32)
        uu = jnp.dot(xn, wu_ref[...], preferred_element_type=F32)
        act = (_silu(gg) * uu).astype(BF16)
        acc_ref[...] += jnp.dot(act, wd_ref[...], preferred_element_type=F32)

        @pl.when(j == nf - 1)
        def _():
            o_ref[...] = acc_ref[...]

    return pl.pallas_call(
        body,
        grid=(n // tm, nf),
        in_specs=[pl.BlockSpec((tm, wa), lambda i, j: (i, 0)),
                  pl.BlockSpec((tm, wb), lambda i, j: (i, 0)),
                  pl.BlockSpec((wa + wb, d), lambda i, j: (0, 0)),
                  pl.BlockSpec((tm, d), lambda i, j: (i, 0)),
                  pl.BlockSpec((1, d), lambda i, j: (0, 0)),
                  pl.BlockSpec((d, tf), lambda i, j: (0, j)),
                  pl.BlockSpec((d, tf), lambda i, j: (0, j)),
                  pl.BlockSpec((tf, d), lambda i, j: (j, 0))],
        out_specs=pl.BlockSpec((tm, d), lambda i, j: (i, 0)),
        out_shape=jax.ShapeDtypeStruct((n, d), F32),
        scratch_shapes=[pltpu.VMEM((tm, d), BF16), pltpu.VMEM((tm, d), F32)],
        compiler_params=_cp("parallel", "arbitrary"),
        name="mix_out_ffn",
    )(a, b, w_out, resid, g.reshape(1, d), wg, wu, wd)


ML_L = 256


def _mlstm(x, in_g, w_in, w_gate, w_gate_t, gate_b_row, gate_b_col, conv_w, conv_b, norm_g):
    bsz, t, d = x.shape
    L, D, H, W = ML_L, ML_DIM, ML_HEADS, ML_W

    def body(x_ref, ing_ref, w_ref, wg_ref, wgt_ref, gbr_ref, gbc_ref, cwq_ref, cwk_ref, cbq_ref, cbk_ref,
             ng_ref, out_ref, qext, kext, c_ref, n_ref, m_ref):
        c = pl.program_id(1)

        @pl.when(c == 0)
        def _():
            qext[0:8, :] = jnp.zeros((8, W), F32)
            kext[0:8, :] = jnp.zeros((8, W), F32)
            c_ref[...] = jnp.zeros_like(c_ref)
            n_ref[...] = jnp.zeros_like(n_ref)
            m_ref[...] = jnp.zeros_like(m_ref)

        xn = _rms_rows(x_ref[0], ing_ref[...]).astype(BF16)
        p = jnp.dot(xn, w_ref[...], preferred_element_type=F32)
        g_cols = jnp.dot(xn, wg_ref[...], preferred_element_type=F32)[:, 0:2 * H]
        g_rows = lax.dot_general(wgt_ref[...], xn, (((1,), (1,)), ((), ())),
                                 preferred_element_type=F32)[0:2 * H, :]
        qext[8:, :] = p[:, 0:W]
        kext[8:, :] = p[:, W:2 * W]

        def conv(ext, cw_ref, cb_ref):
            acc = cb_ref[...] + cw_ref[0:1, :] * ext[pl.ds(8 - ML_CONV + 1, L), :]
            for kk in range(1, ML_CONV):
                acc = acc + cw_ref[kk:kk + 1, :] * ext[pl.ds(8 - ML_CONV + 1 + kk, L), :]
            return _silu(acc)

        q_all = conv(qext, cwq_ref, cbq_ref) * (D ** -0.5)
        k_all = conv(kext, cwk_ref, cbk_ref)
        qext[0:8, :] = qext[L:L + 8, :]
        kext[0:8, :] = kext[L:L + 8, :]

        gcol = g_cols + gbr_ref[...]
        grow = g_rows + gbc_ref[...]
        fcol = _log_sigmoid(gcol[:, H:2 * H])
        frow = _log_sigmoid(grow[H:2 * H, :])
        ri = lax.broadcasted_iota(jnp.int32, (L, L), 0)
        ci = lax.broadcasted_iota(jnp.int32, (L, L), 1)
        causal = ri >= ci
        heads = range(H)
        hs = lambda z, h: z[:, h * D:(h + 1) * D]
        w_intra, w_inter, w_state, carry, m_row, m_new = [], [], [], [], [], []
        for h in heads:
            f_row = frow[h:h + 1, :]
            i_row = grow[h:h + 1, :]
            f_col = fcol[:, h:h + 1]
            i_col = gcol[:, h:h + 1]
            m_st = m_ref[h]
            b_col = jnp.sum(jnp.where(causal, f_row, 0.0), axis=1, keepdims=True)
            b_row = jnp.sum(jnp.where(ri <= ci, f_col, 0.0), axis=0, keepdims=True)
            g_tot = jnp.sum(f_row, axis=1, keepdims=True)
            d_intra = jnp.where(causal, b_col - b_row + i_row, NEG)
            d_inter = b_col + m_st
            mr = jnp.maximum(d_inter, jnp.max(d_intra, axis=1, keepdims=True))
            w_intra.append(jnp.exp(d_intra - mr))
            w_inter.append(jnp.exp(d_inter - mr))
            m_row.append(mr)
            d_state = g_tot - b_col + i_col
            mn = jnp.maximum(g_tot + m_st, jnp.max(d_state, axis=0, keepdims=True))
            w_state.append(jnp.exp(d_state - mn))
            carry.append(jnp.exp(g_tot + m_st - mn))
            m_new.append(mn)
        qh = [hs(q_all, h) for h in heads]
        kh = [hs(k_all, h) for h in heads]
        vh = [p[:, 2 * W + h * D:2 * W + (h + 1) * D] for h in heads]
        c_st = [c_ref[h] for h in heads]
        n_st = [n_ref[h] for h in heads]
        s = [_dot_nt(qh[h], kh[h]) * w_intra[h] for h in heads]
        qc = [_dot(qh[h], c_st[h]) for h in heads]
        kw = [kh[h] * w_state[h] for h in heads]
        kv = [_dot_tn(kw[h], vh[h]) for h in heads]
        sv = [_dot(s[h], vh[h]) for h in heads]
        for h in heads:
            num = sv[h] + w_inter[h] * qc[h]
            den = jnp.sum(s[h], axis=1, keepdims=True) + w_inter[h] * jnp.sum(qh[h] * n_st[h], axis=1, keepdims=True)
            hval = num / jnp.maximum(jnp.abs(den), jnp.exp(-m_row[h]))
            c_ref[h] = carry[h] * c_st[h] + kv[h]
            n_ref[h] = carry[h] * n_st[h] + jnp.sum(kw[h], axis=0, keepdims=True)
            m_ref[h] = m_new[h]
            hg = _sigmoid(p[:, 3 * W + h * D:3 * W + (h + 1) * D]) * hval
            hc = hg - jnp.mean(hg, axis=1, keepdims=True)
            hn = hc * lax.rsqrt(jnp.mean(hc * hc, axis=1, keepdims=True) + EPS)
            out_ref[0, :, h * D:(h + 1) * D] = (hn * ng_ref[:, h * D:(h + 1) * D]).astype(BF16)

    fix = lambda j: (lambda b, c: (0, j))
    return pl.pallas_call(
        body,
        grid=(bsz, t // L),
        in_specs=[pl.BlockSpec((1, L, d), lambda b, c: (b, c, 0)),
                  pl.BlockSpec((1, d), fix(0)),
                  pl.BlockSpec((d, 4 * W), fix(0)),
                  pl.BlockSpec((d, 128), fix(0)),
                  pl.BlockSpec((16, d), fix(0)),
                  pl.BlockSpec((1, 2 * H), fix(0)),
                  pl.BlockSpec((2 * H, 1), fix(0)),
                  pl.BlockSpec((ML_CONV, W), fix(0)),
                  pl.BlockSpec((ML_CONV, W), fix(1)),
                  pl.BlockSpec((1, W), fix(0)),
                  pl.BlockSpec((1, W), fix(1)),
                  pl.BlockSpec((1, W), fix(0))],
        out_specs=pl.BlockSpec((1, L, W), lambda b, c: (b, c, 0)),
        out_shape=jax.ShapeDtypeStruct((bsz, t, W), BF16),
        scratch_shapes=[pltpu.VMEM((L + 8, W), F32), pltpu.VMEM((L + 8, W), F32),
                        pltpu.VMEM((H, D, D), F32), pltpu.VMEM((H, 1, D), F32), pltpu.VMEM((H, 1, 1), F32)],
        compiler_params=_cp("parallel", "arbitrary"),
        name="mlstm",
    )(x, in_g, w_in, w_gate, w_gate_t, gate_b_row, gate_b_col, conv_w, conv_w, conv_b, conv_b, norm_g)


GLA_TC = 256
GLA_SUB = 16
GLA_GROUP = 128


def _gla(x, in_g, w_pairs, w_low, gate_up, gate_b, norm_g):
    bsz, t, d = x.shape
    tc, S, GB = GLA_TC, GLA_SUB, GLA_GROUP
    head_ones = (jnp.arange(2 * GLA_DK)[:, None] // GLA_DK == jnp.arange(2 * GB)[None, :] // GB).astype(BF16)
    nsub = tc // S
    dk, dv = GLA_DK, GLA_DV

    def body(x_ref, ing_ref, w_ref, wl_ref, gu_ref, gbias_ref, ng_ref, ones_ref, out_ref, st_ref, ksh, bsh):
        c = pl.program_id(2)

        @pl.when(c == 0)
        def _():
            st_ref[...] = jnp.zeros_like(st_ref)
            ksh[0:S, :] = jnp.zeros((S, 2 * dk), F32)
            bsh[0:S, :] = jnp.zeros((S, 2 * dk), F32)

        xn = _rms_rows(x_ref[0], ing_ref[...]).astype(BF16)
        p = jnp.dot(xn, w_ref[0], preferred_element_type=F32)
        g_low = jnp.dot(xn, wl_ref[...], preferred_element_type=F32)[:, 0:GLA_RANK]
        z = _dot(g_low, gu_ref[...]) + gbias_ref[...]
        la = _log_sigmoid(z) / GLA_TAU
        rowi = lax.broadcasted_iota(jnp.int32, (tc, 1), 0)
        rmod = rowi & (S - 1)
        bcum = la
        rsum = la
        for s in (1, 2, 4, 8):
            bcum = bcum + jnp.where(rmod >= s, pltpu.roll(bcum, s, 0), 0.0)
            rsum = rsum + jnp.where(rmod < S - s, pltpu.roll(rsum, tc - s, 0), 0.0)
        q = p[:, 0:2 * dk] * (dk ** -0.5)
        k = p[:, 2 * dk:4 * dk]
        v = p[:, 4 * dk:4 * dk + 2 * dv]
        gate = p[:, 4 * dk + 2 * dv:4 * dk + 4 * dv]
        qt = q * jnp.exp(bcum)
        kt = k * jnp.exp(rsum - la)
        eg = jnp.exp(bcum + rsum - la)

        ksh[S:, :] = k
        bsh[S:, :] = bcum
        prods = []
        for d in range(S):
            kd = k if d == 0 else ksh[pl.ds(S - d, tc), :]
            bd = bcum if d == 0 else bsh[pl.ds(S - d, tc), :]
            e = jnp.exp(jnp.where(rmod >= d, bcum - bd, 0.0))
            prods.append((q * kd * e).astype(BF16))
        ws = [jnp.dot(p, ones_ref[...], preferred_element_type=F32) for p in prods]
        coli = lax.broadcasted_iota(jnp.int32, (tc, GB), 1)
        rgrp = rowi & (GB - 1)
        att0 = jnp.zeros((tc, GB), F32)
        att1 = jnp.zeros((tc, GB), F32)
        for d in range(S):
            here = jnp.logical_and(coli == rgrp - d, rmod >= d)
            att0 = jnp.where(here, ws[d][:, 0:GB], att0)
            att1 = jnp.where(here, ws[d][:, GB:2 * GB], att1)

        heads = range(2)
        lk = [slice(hh * dk, (hh + 1) * dk) for hh in heads]
        lv = [slice(hh * dv, (hh + 1) * dv) for hh in heads]
        kv = [[_dot_tn(v[si * S:(si + 1) * S, lv[hh]], kt[si * S:(si + 1) * S, lk[hh]]) for hh in heads]
              for si in range(nsub)]
        st = [st_ref[hh] for hh in heads]
        inter = [[], []]
        for si in range(nsub):
            rows = slice(si * S, (si + 1) * S)
            for hh in heads:
                inter[hh].append(_dot_nt(qt[rows, lk[hh]], st[hh]))
                st[hh] = st[hh] * eg[si * S:si * S + 1, lk[hh]] + kv[si][hh]
        for hh in heads:
            st_ref[hh] = st[hh]

        for hh, att in ((0, att0), (1, att1)):
            diag = jnp.concatenate([_dot(att[g * GB:(g + 1) * GB], v[g * GB:(g + 1) * GB, lv[hh]])
                                    for g in range(tc // GB)], axis=0)
            o = diag + jnp.concatenate(inter[hh], axis=0)
            on = o * lax.rsqrt(jnp.mean(o * o, axis=1, keepdims=True) + EPS)
            out_ref[0, :, lv[hh]] = (on * ng_ref[:, lv[hh]] * _silu(gate[:, lv[hh]])).astype(BF16)

    pw = 4 * dk + 4 * dv
    return pl.pallas_call(
        body,
        grid=(bsz, GLA_HEADS // 2, t // tc),
        in_specs=[pl.BlockSpec((1, tc, d), lambda b, h, c: (b, c, 0)),
                  pl.BlockSpec((1, d), lambda b, h, c: (0, 0)),
                  pl.BlockSpec((1, d, pw), lambda b, h, c: (h, 0, 0)),
                  pl.BlockSpec((d, 128), lambda b, h, c: (0, 0)),
                  pl.BlockSpec((GLA_RANK, 2 * dk), lambda b, h, c: (0, h)),
                  pl.BlockSpec((1, 2 * dk), lambda b, h, c: (0, h)),
                  pl.BlockSpec((1, 2 * dv), lambda b, h, c: (0, h)),
                  pl.BlockSpec((2 * dk, 2 * GB), lambda b, h, c: (0, 0))],
        out_specs=pl.BlockSpec((1, tc, 2 * dv), lambda b, h, c: (b, c, h)),
        out_shape=jax.ShapeDtypeStruct((bsz, t, GLA_HEADS * dv), BF16),
        scratch_shapes=[pltpu.VMEM((2, dv, dk), F32), pltpu.VMEM((tc + S, 2 * dk), F32),
                        pltpu.VMEM((tc + S, 2 * dk), F32)],
        compiler_params=_cp("parallel", "parallel", "arbitrary"),
        name="gla",
    )(x, in_g, w_pairs, w_low, gate_up, gate_b, norm_g, head_ones)


RET_L = 256


def _retention(x, in_g, w_in, cos_t, sin_t, intra, inter, sdec, cdec, norm_g):
    bsz, t, d = x.shape
    L, D, H, W = RET_L, RET_DIM, RET_HEADS, RET_W

    def body(x_ref, ing_ref, w_ref, cos_ref, sin_ref, intra_ref, inter_ref, sdec_ref, cdec_ref, ng_ref,
             out_ref, s_ref):
        c = pl.program_id(1)

        @pl.when(c == 0)
        def _():
            s_ref[...] = jnp.zeros_like(s_ref)

        p = jnp.dot(_rms_rows(x_ref[0], ing_ref[...]).astype(BF16), w_ref[...], preferred_element_type=F32)
        cs = cos_ref[...]
        sn = sin_ref[...]

        def rot(z):
            return z * cs + pltpu.roll(z, D // 2, 1) * sn

        heads = range(H)
        q = [rot(p[:, h * D:(h + 1) * D]) * (D ** -0.5) for h in heads]
        k = [rot(p[:, W + h * D:W + (h + 1) * D]) for h in heads]
        v = [p[:, 2 * W + h * D:2 * W + (h + 1) * D] for h in heads]
        s_st = [s_ref[h] for h in heads]
        s = [_dot_nt(q[h], k[h]) * intra_ref[h] for h in heads]
        qs = [_dot(q[h], s_st[h]) for h in heads]
        kv = [_dot_tn(k[h] * sdec_ref[h], v[h]) for h in heads]
        sv = [_dot(s[h], v[h]) for h in heads]
        for h in heads:
            o = sv[h] + inter_ref[h] * qs[h]
            s_ref[h] = cdec_ref[h] * s_st[h] + kv[h]
            oc = o - jnp.mean(o, axis=1, keepdims=True)
            on = oc * lax.rsqrt(jnp.mean(oc * oc, axis=1, keepdims=True) + EPS)
            hl = slice(h * D, (h + 1) * D)
            gate = p[:, 3 * W + h * D:3 * W + (h + 1) * D]
            out_ref[0, :, hl] = (on * ng_ref[:, hl] * _silu(gate)).astype(BF16)

    fix2 = lambda b, c: (0, 0)
    fix3 = lambda b, c: (0, 0, 0)
    return pl.pallas_call(
        body,
        grid=(bsz, t // L),
        in_specs=[pl.BlockSpec((1, L, d), lambda b, c: (b, c, 0)),
                  pl.BlockSpec((1, d), fix2),
                  pl.BlockSpec((d, 4 * W), fix2),
                  pl.BlockSpec((L, D), lambda b, c: (c, 0)),
                  pl.BlockSpec((L, D), lambda b, c: (c, 0)),
                  pl.BlockSpec((H, L, L), fix3),
                  pl.BlockSpec((H, L, 1), fix3),
                  pl.BlockSpec((H, L, 1), fix3),
                  pl.BlockSpec((H, 1, 1), fix3),
                  pl.BlockSpec((1, W), lambda b, c: (0, 0))],
        out_specs=pl.BlockSpec((1, L, W), lambda b, c: (b, c, 0)),
        out_shape=jax.ShapeDtypeStruct((bsz, t, W), BF16),
        scratch_shapes=[pltpu.VMEM((H, D, D), F32)],
        compiler_params=_cp("parallel", "arbitrary"),
        name="retention",
    )(x, in_g, w_in, cos_t, sin_t, intra, inter, sdec, cdec, norm_g)


def _retention_tables(t):
    L, D = RET_L, RET_DIM
    inv = ROPE_BASE ** (-jnp.arange(0, D, 2, dtype=F32) / D)
    ang = jnp.arange(t).astype(F32)[:, None] * inv[None, :]
    cos = jnp.cos(ang)
    sin = jnp.sin(ang)
    cos_t = jnp.concatenate([cos, cos], axis=-1)
    sin_t = jnp.concatenate([-sin, sin], axis=-1)
    log_gamma = jnp.log1p(-jnp.exp2(-5.0 - jnp.arange(RET_HEADS, dtype=F32)))
    idx = jnp.arange(L, dtype=F32)
    causal = idx[:, None] >= idx[None, :]
    rel = jnp.where(causal, idx[:, None] - idx[None, :], 0.0)
    intra = jnp.where(causal, jnp.exp(log_gamma[:, None, None] * rel), 0.0)
    inter = jnp.exp(log_gamma[:, None] * (idx + 1.0))[:, :, None]
    sdec = jnp.exp(log_gamma[:, None] * (L - 1.0 - idx))[:, :, None]
    cdec = jnp.exp(log_gamma * L)[:, None, None]
    return cos_t, sin_t, intra, inter, sdec, cdec


RWP_TM = 512


def _seg_sum(x, bd):
    hi = x.astype(BF16)
    lo = (x - hi.astype(F32)).astype(BF16)
    return jnp.dot(hi, bd, preferred_element_type=F32) + jnp.dot(lo, bd, preferred_element_type=F32)


def _rwkv_prep(x, in_g, w_in, mu, w_up, w0, a_up, a0, g_up, k_k, k_a, r_k, bd):
    bsz, t, d = x.shape
    tm, W = RWP_TM, RW_W

    def body(x_ref, ing_ref, win_ref, mu_ref, wup_ref, w0_ref, aup_ref, a0_ref, gup_ref, kk_ref, ka_ref, rk_ref,
             bd_ref, r_out, lw_out, k_out, v_out, a_out, b_out, g_out, bonus_out, last_ref):
        c = pl.program_id(1)

        @pl.when(c == 0)
        def _():
            last_ref[...] = jnp.zeros_like(last_ref)

        cur = jnp.dot(_rms_rows(x_ref[0], ing_ref[...]).astype(BF16), win_ref[...], preferred_element_type=F32)
        prev = last_ref[...]
        last_ref[...] = cur[tm - 1:tm, :]
        rowi = lax.broadcasted_iota(jnp.int32, (tm, 1), 0)
        sh = jnp.where(rowi == 0, prev, pltpu.roll(cur, 1, 0))
        xm = cur + (sh - cur) * mu_ref[...]
        x_r = xm[:, 0:W]
        x_k = xm[:, W:2 * W]
        x_v = xm[:, 2 * W:3 * W]
        x_dl = xm[:, 3 * W:3 * W + 64]
        x_al = xm[:, 3 * W + 64:3 * W + 128]
        x_gl = xm[:, 3 * W + 128:3 * W + 256]
        wl = w0_ref[...] + _dot(jnp.tanh(x_dl), wup_ref[...])
        sp = jnp.maximum(-wl, 0.0) + jnp.log(1.0 + jnp.exp(-jnp.abs(wl)))
        lw_out[0] = -jnp.exp(-sp - 0.5)
        a = _sigmoid(a0_ref[...] + _dot(x_al, aup_ref[...]))
        g_out[0] = _dot(_sigmoid(x_gl), gup_ref[...]).astype(BF16)
        kk0 = x_k * kk_ref[...]
        nrm = jnp.sqrt(_seg_sum(kk0 * kk0, bd_ref[...]))
        kk = kk0 / jnp.maximum(nrm, 1e-12)
        k_h = x_k * (1.0 + (a - 1.0) * ka_ref[...])
        r_out[0] = x_r.astype(BF16)
        k_out[0] = k_h.astype(BF16)
        v_out[0] = x_v.astype(BF16)
        a_out[0] = (-kk).astype(BF16)
        b_out[0] = (kk * a).astype(BF16)
        bonus_out[0] = (_seg_sum(x_r * k_h * rk_ref[...], bd_ref[...]) * x_v).astype(BF16)

    row = lambda b, c: (0, 0)
    blk = pl.BlockSpec((1, tm, W), lambda b, c: (b, c, 0))
    shp = [jax.ShapeDtypeStruct((bsz, t, W), F32 if i == 1 else BF16) for i in range(8)]
    return pl.pallas_call(
        body,
        grid=(bsz, t // tm),
        in_specs=[pl.BlockSpec((1, tm, d), lambda b, c: (b, c, 0)),
                  pl.BlockSpec((1, d), row),
                  pl.BlockSpec((d, RW_COLS), row),
                  pl.BlockSpec((1, RW_COLS), row),
                  pl.BlockSpec((64, W), row), pl.BlockSpec((1, W), row),
                  pl.BlockSpec((64, W), row), pl.BlockSpec((1, W), row),
                  pl.BlockSpec((128, W), row),
                  pl.BlockSpec((1, W), row), pl.BlockSpec((1, W), row), pl.BlockSpec((1, W), row),
                  pl.BlockSpec((W, W), row)],
        out_specs=[blk] * 8,
        out_shape=shp,
        scratch_shapes=[pltpu.VMEM((1, RW_COLS), F32)],
        compiler_params=_cp("parallel", "arbitrary"),
        name="rwkv_prep",
    )(x, in_g, w_in, mu, w_up, w0, a_up, a0, g_up, k_k, k_a, r_k, bd)


RW_L = 64
RW_TB = 128


def _rwkv_scan(r, lw, k, v, aa, bb, g_out, bonus, ln_g, ln_b):
    bsz, t, W = r.shape
    L, N, tb = RW_L, RW_DIM, RW_TB
    nck = tb // L

    def body(r_ref, lw_ref, k_ref, v_ref, a_ref, b_ref, g_ref, bonus_ref, lng_ref, lnb_ref, out_ref, h_ref):
        c = pl.program_id(1)

        @pl.when(c == 0)
        def _():
            h_ref[...] = jnp.zeros_like(h_ref)

        lw_all = lw_ref[0]
        rowi = lax.broadcasted_iota(jnp.int32, (tb, 1), 0) & (L - 1)
        cl = lw_all
        for s in (1, 2, 4, 8, 16, 32):
            cl = cl + jnp.where(rowi >= s, pltpu.roll(cl, s, 0), 0.0)
        cl_last = jnp.concatenate([jnp.broadcast_to(cl[(cc + 1) * L - 1:(cc + 1) * L, :], (L, W))
                                   for cc in range(nck)], axis=0)
        e_inv = jnp.exp(-cl)
        e_end = jnp.exp(cl_last - cl)
        p_end = jnp.exp(cl_last)
        at = a_ref[0] * jnp.exp(cl - lw_all)
        rt = r_ref[0] * jnp.exp(cl)
        bt = b_ref[0] * e_inv
        kt = k_ref[0] * e_inv
        b_end = b_ref[0] * e_end
        k_end = k_ref[0] * e_end
        v_all = v_ref[0]
        ri = lax.broadcasted_iota(jnp.int32, (L, L), 0)
        ci = lax.broadcasted_iota(jnp.int32, (L, L), 1)
        strict = ri > ci
        lower = ri >= ci
        P2 = 2 * N
        lane = lax.broadcasted_iota(jnp.int32, (1, P2), 1)
        half = [lane < N, lane >= N]
        pairs = [(cc, p) for cc in range(nck) for p in range(RW_HEADS // 2)]
        units = [(pi, hh) for pi in range(len(pairs)) for hh in range(2)]
        un = range(len(units))

        def pb(z, pi):
            cc, p = pairs[pi]
            return z[cc * L:(cc + 1) * L, p * P2:(p + 1) * P2]

        x_pair = [jnp.concatenate([pb(at, pi), pb(rt, pi)], axis=0) for pi in range(len(pairs))]
        bend_t = [pb(b_end, pi).T for pi in range(len(pairs))]
        kend_t = [pb(k_end, pi).T for pi in range(len(pairs))]
        xm = [jnp.where(half[hh], x_pair[pi], 0.0) for pi, hh in units]
        m_b = [_dot_nt(xm[u], pb(bt, units[u][0])) for u in un]
        m_k = [_dot_nt(xm[u], pb(kt, units[u][0])) for u in un]
        a_ab = [jnp.where(strict, m_b[u][0:L], 0.0) for u in un]
        a_rb = [jnp.where(lower, m_b[u][L:2 * L], 0.0) for u in un]
        a_ak = [jnp.where(strict, m_k[u][0:L], 0.0) for u in un]
        a_rk = [jnp.where(lower, m_k[u][L:2 * L], 0.0) for u in un]
        vp = [_dot(jnp.concatenate([a_ak[u], a_rk[u], kend_t[pi][hh * N:(hh + 1) * N]], axis=0), pb(v_all, pi))
              for u, (pi, hh) in enumerate(units)]
        x = [jnp.concatenate([pb(at, pi), vp[u][0:L]], axis=1) for u, (pi, hh) in enumerate(units)]
        ap = a_ab
        for it in range(6):
            x = [x[u] + _dot(ap[u], x[u]) for u in un]
            if it < 5:
                ap = [_dot(ap[u], ap[u]) for u in un]
        post = [_dot(jnp.concatenate([a_rb[u], bend_t[pi][hh * N:(hh + 1) * N]], axis=0), x[u])
                for u, (pi, hh) in enumerate(units)]
        d_idx = lax.broadcasted_iota(jnp.int32, (N, P2), 0)
        l_idx = lax.broadcasted_iota(jnp.int32, (N, P2), 1)
        eyeh = [l_idx == d_idx + hh * N for hh in range(2)]
        lhs = []
        y0s = []
        h_adds = []
        for u, (pi, hh) in enumerate(units):
            q_hat = pb(rt, pi) + post[u][0:L, 0:P2]
            gmat = jnp.where(eyeh[hh], pb(p_end, pi)[0:N], 0.0) + post[u][L:L + N, 0:P2]
            lhs.append(jnp.concatenate([q_hat, gmat], axis=0))
            y0s.append(post[u][0:L, P2:2 * P2] + vp[u][L:2 * L])
            h_adds.append(post[u][L:L + N, P2:2 * P2] + vp[u][2 * L:2 * L + N])
        h_st = [h_ref[h] for h in range(RW_HEADS)]
        seg = (lax.broadcasted_iota(jnp.int32, (P2, P2), 0) // N
               == lax.broadcasted_iota(jnp.int32, (P2, P2), 1) // N).astype(BF16)
        zer = jnp.zeros((N, P2), F32)
        for cc in range(nck):
            rows = slice(cc * L, (cc + 1) * L)
            for p in range(RW_HEADS // 2):
                pi = cc * (RW_HEADS // 2) + p
                ys = []
                for hh in range(2):
                    u = 2 * pi + hh
                    h = 2 * p + hh
                    res = _dot(lhs[u], h_st[h])
                    ys.append(res[0:L] + y0s[u])
                    hnew = jnp.where(half[hh], res[L:L + N] + h_adds[u], 0.0)
                    h_st[h] = jnp.concatenate([hnew, zer] if hh == 0 else [zer, hnew], axis=0)
                y = jnp.where(half[0], ys[0], ys[1])
                yc = y - _seg_sum(y, seg) * (1.0 / N)
                yn = yc * lax.rsqrt(_seg_sum(yc * yc, seg) * (1.0 / N) + RW_LN_EPS)
                cols = slice(p * P2, (p + 1) * P2)
                out_ref[0, rows, cols] = ((yn * lng_ref[:, cols] + lnb_ref[:, cols] + bonus_ref[0, rows, cols])
                                          * g_ref[0, rows, cols]).astype(BF16)
        for h in range(RW_HEADS):
            h_ref[h] = h_st[h]

    blk = pl.BlockSpec((1, tb, W), lambda b, c: (b, c, 0))
    vec = pl.BlockSpec((1, W), lambda b, c: (0, 0))
    return pl.pallas_call(
        body,
        grid=(bsz, t // tb),
        in_specs=[blk] * 8 + [vec, vec],
        out_specs=blk,
        out_shape=jax.ShapeDtypeStruct((bsz, t, W), BF16),
        scratch_shapes=[pltpu.VMEM((RW_HEADS, 2 * N, 2 * N), F32)],
        compiler_params=_cp("parallel", "arbitrary"),
        name="rwkv_scan",
    )(r, lw, k, v, aa, bb, g_out, bonus, ln_g, ln_b)


def _mix_out_route(a, b, w_bf16, resid, g, router_split, tm=1024):
    n, wa = a.shape
    wb = b.shape[1]
    d = w_bf16.shape[1]
    e = router_split.shape[2]

    def body(a_ref, b_ref, w_ref, r_ref, g_ref, rt_ref, h_ref, xn_ref, lg_ref):
        acc = jnp.dot(a_ref[...], w_ref[0:wa, :], preferred_element_type=F32)
        acc = acc + jnp.dot(b_ref[...], w_ref[wa:wa + wb, :], preferred_element_type=F32)
        h = r_ref[...] + acc
        h_ref[...] = h
        xn = _rms_rows(h, g_ref[...])
        x_hi = xn.astype(BF16)
        xn_ref[...] = x_hi
        x_lo = (xn - x_hi.astype(F32)).astype(BF16)
        r_hi = rt_ref[0]
        r_lo = rt_ref[1]
        lg_ref[...] = (jnp.dot(x_hi, r_hi, preferred_element_type=F32)
                       + (jnp.dot(x_lo, r_hi, preferred_element_type=F32)
                          + jnp.dot(x_hi, r_lo, preferred_element_type=F32)))

    return pl.pallas_call(
        body,
        grid=(n // tm,),
        in_specs=[pl.BlockSpec((tm, wa), lambda i: (i, 0)),
                  pl.BlockSpec((tm, wb), lambda i: (i, 0)),
                  pl.BlockSpec((wa + wb, d), lambda i: (0, 0)),
                  pl.BlockSpec((tm, d), lambda i: (i, 0)),
                  pl.BlockSpec((1, d), lambda i: (0, 0)),
                  pl.BlockSpec((2, d, e), lambda i: (0, 0, 0))],
        out_specs=[pl.BlockSpec((tm, d), lambda i: (i, 0)), pl.BlockSpec((tm, d), lambda i: (i, 0)),
                   pl.BlockSpec((tm, e), lambda i: (i, 0))],
        out_shape=[jax.ShapeDtypeStruct((n, d), F32), jax.ShapeDtypeStruct((n, d), BF16),
                   jax.ShapeDtypeStruct((n, e), F32)],
        compiler_params=_cp("parallel"),
        name="mix_out_route",
    )(a, b, w_bf16, resid, g.reshape(1, d), router_split)


MOE_TM = 512
MOE_TF = 1792


def _experts(xs, row_w, item_tile, item_exp, item_lo, item_hi, wg, wu, wd):
    nrows, d = xs.shape
    tm, tf = MOE_TM, MOE_TF
    nf = D_FF // tf
    n_items = item_tile.shape[0]

    def body(it_ref, ie_ref, lo_ref, hi_ref, x_ref, w_ref, wg_ref, wu_ref, wd_ref, o_ref, acc_ref):
        i = pl.program_id(0)
        j = pl.program_id(1)
        tile = it_ref[i]
        first = jnp.logical_or(i == 0, tile != it_ref[jnp.maximum(i - 1, 0)])
        last = jnp.logical_or(i == n_items - 1, tile != it_ref[jnp.minimum(i + 1, n_items - 1)])

        @pl.when(jnp.logical_and(first, j == 0))
        def _():
            acc_ref[...] = jnp.zeros_like(acc_ref)

        lo = lo_ref[i]
        hi = hi_ref[i]

        @pl.when(lo < hi)
        def _():
            x = x_ref[...]
            gg = jnp.dot(x, wg_ref[0], preferred_element_type=F32)
            uu = jnp.dot(x, wu_ref[0], preferred_element_type=F32)
            act = (_silu(gg) * uu).astype(BF16)
            part = jnp.dot(act, wd_ref[0], preferred_element_type=F32)
            rowi = lax.broadcasted_iota(jnp.int32, (tm, 1), 0)
            mine = jnp.logical_and(rowi >= lo, rowi < hi)
            acc_ref[...] += part * jnp.where(mine, w_ref[...], 0.0)

        @pl.when(jnp.logical_and(last, j == nf - 1))
        def _():
            o_ref[...] = acc_ref[...].astype(o_ref.dtype)

    grid_spec = pltpu.PrefetchScalarGridSpec(
        num_scalar_prefetch=4,
        grid=(n_items, nf),
        in_specs=[pl.BlockSpec((tm, d), lambda i, j, it, ie, lo, hi: (it[i], 0)),
                  pl.BlockSpec((tm, 1), lambda i, j, it, ie, lo, hi: (it[i], 0)),
                  pl.BlockSpec((1, d, tf), lambda i, j, it, ie, lo, hi: (ie[i], 0, j)),
                  pl.BlockSpec((1, d, tf), lambda i, j, it, ie, lo, hi: (ie[i], 0, j)),
                  pl.BlockSpec((1, tf, d), lambda i, j, it, ie, lo, hi: (ie[i], j, 0))],
        out_specs=pl.BlockSpec((tm, d), lambda i, j, it, ie, lo, hi: (it[i], 0)),
        scratch_shapes=[pltpu.VMEM((tm, d), F32)],
    )
    return pl.pallas_call(
        body,
        grid_spec=grid_spec,
        out_shape=jax.ShapeDtypeStruct((nrows, d), BF16),
        compiler_params=_cp("arbitrary", "arbitrary"),
        name="moe_experts",
    )(item_tile, item_exp, item_lo, item_hi, xs, row_w, wg, wu, wd)


def _combine_norm(h, y0, y1, g, tm=512):
    n, d = h.shape

    def body(h_ref, a_ref, b_ref, g_ref, o_ref):
        o_ref[...] = _rms_rows(h_ref[...] + (a_ref[...].astype(F32) + b_ref[...].astype(F32)), g_ref[...])

    blk = pl.BlockSpec((tm, d), lambda i: (i, 0))
    return pl.pallas_call(
        body,
        grid=(n // tm,),
        in_specs=[blk, blk, blk, pl.BlockSpec((1, d), lambda i: (0, 0))],
        out_specs=blk,
        out_shape=jax.ShapeDtypeStruct((n, d), F32),
        compiler_params=_cp("parallel"),
        name="combine_norm",
    )(h, y0, y1, g.reshape(1, d))


def _route(logits, n):
    tm = MOE_TM
    na = n * TOP_K
    n_tiles = na // tm
    top_val, top_idx = lax.top_k(logits, TOP_K)
    top_w = jax.nn.softmax(top_val, axis=-1)
    e_flat = top_idx.reshape(-1).astype(jnp.int32)
    w_flat = top_w.reshape(-1)
    tok = jnp.arange(na, dtype=jnp.int32) // TOP_K
    _, sorted_tok, sorted_w = lax.sort((e_flat, tok, w_flat), num_keys=1, is_stable=True)
    onehot = (e_flat[:, None] == jnp.arange(N_EXPERTS, dtype=jnp.int32)[None, :]).astype(jnp.int32)
    rank = jnp.take_along_axis(jnp.cumsum(onehot, axis=0), e_flat[:, None], axis=1)[:, 0] - 1
    counts = jnp.sum(onehot, axis=0)
    ends = jnp.cumsum(counts)
    pos = (ends - counts)[e_flat] + rank
    cuts = jnp.sort(jnp.concatenate([jnp.arange(n_tiles, dtype=jnp.int32) * tm, ends[:-1].astype(jnp.int32)]))
    nxt = jnp.concatenate([cuts[1:], jnp.full((1,), na, jnp.int32)])
    item_tile = jnp.minimum(cuts // tm, n_tiles - 1)
    item_exp = jnp.minimum(jnp.searchsorted(ends, cuts, side="right"), N_EXPERTS - 1).astype(jnp.int32)
    item_lo = cuts - item_tile * tm
    item_hi = nxt - item_tile * tm
    return sorted_tok, sorted_w, (item_tile, item_exp, item_lo, item_hi), pos.reshape(n, TOP_K)


def kernel(x, e_norm1_g, e_w_in, e_ml_conv_w, e_ml_conv_b, e_ml_gate_b, e_ml_norm_g, e_gla_gate_up, e_gla_gate_b,
           e_gla_norm_g, e_w_out, e_norm2_g, e_ffn_w_gate, e_ffn_w_up, e_ffn_w_down, o_norm1_g, o_w_in,
           o_ret_norm_g, o_rw_mu, o_rw_w_up, o_rw_w0, o_rw_a_up, o_rw_a0, o_rw_g_up, o_rw_k_k, o_rw_k_a, o_rw_r_k,
           o_rw_ln_g, o_rw_ln_b, o_w_out, o_norm2_g, o_moe_router, o_moe_w_gate, o_moe_w_up, o_moe_w_down,
           final_norm_g):
    bsz, t, d = x.shape
    n = bsz * t
    h0 = x.reshape(n, d)

    w = e_w_in[0]
    row = lambda a: a.reshape(1, -1)
    x3 = x
    ng1 = row(e_norm1_g[0])
    w_if = w[:, 2048:2056]
    w_gate = jnp.zeros((d, 128), F32).at[:, :2 * ML_HEADS].set(w_if).astype(BF16)
    w_gate_t = jnp.zeros((16, d), F32).at[:2 * ML_HEADS, :].set(w_if.T).astype(BF16)
    h_ml = _mlstm(x3, ng1, w[:, :4 * ML_W].astype(BF16), w_gate, w_gate_t, row(e_ml_gate_b[0]),
                  e_ml_gate_b[0].reshape(-1, 1), e_ml_conv_w[0], row(e_ml_conv_b[0]), row(e_ml_norm_g[0]))
    gq, gk, gv, gr = 2056, 2312, 2568, 3080
    w_pairs = jnp.stack([jnp.concatenate([w[:, gq + 128 * hp:gq + 128 * (hp + 1)],
                                          w[:, gk + 128 * hp:gk + 128 * (hp + 1)],
                                          w[:, gv + 256 * hp:gv + 256 * (hp + 1)],
                                          w[:, gr + 256 * hp:gr + 256 * (hp + 1)]], axis=1)
                         for hp in range(GLA_HEADS // 2)]).astype(BF16)
    w_low = jnp.zeros((d, 128), F32).at[:, :GLA_RANK].set(w[:, 3592:3608]).astype(BF16)
    o_gla = _gla(x3, ng1, w_pairs, w_low, e_gla_gate_up[0].astype(BF16), row(e_gla_gate_b[0]),
                 row(e_gla_norm_g[0]))
    h2 = _mix_out_ffn(h_ml.reshape(n, -1), o_gla.reshape(n, -1), e_w_out[0].astype(BF16), h0, e_norm2_g[0],
                      e_ffn_w_gate[0].astype(BF16), e_ffn_w_up[0].astype(BF16), e_ffn_w_down[0].astype(BF16))

    w = o_w_in[0]
    h2_3 = h2.reshape(bsz, t, d)
    ng2 = row(o_norm1_g[0])
    y_ret = _retention(h2_3, ng2, w[:, :4 * RET_W].astype(BF16), *_retention_tables(t), row(o_ret_norm_g[0]))
    head_of = jnp.arange(RW_W) // RW_DIM
    bd = (head_of[:, None] == head_of[None, :]).astype(BF16)
    r, lw, k, v, aa, bb, g_out, bonus = _rwkv_prep(
        h2_3, ng2, w[:, 4 * RET_W:].astype(BF16), row(o_rw_mu[0]), o_rw_w_up[0].astype(BF16), row(o_rw_w0[0]),
        o_rw_a_up[0].astype(BF16), row(o_rw_a0[0]), o_rw_g_up[0].astype(BF16), row(o_rw_k_k[0]),
        row(o_rw_k_a[0]), row(o_rw_r_k[0]), bd)
    y_rw = _rwkv_scan(r, lw, k, v, aa, bb, g_out, bonus, row(o_rw_ln_g[0]), row(o_rw_ln_b[0]))
    router_pad = jnp.zeros((d, 128), F32).at[:, :N_EXPERTS].set(o_moe_router[0])
    router_hi = router_pad.astype(BF16)
    router_split = jnp.stack([router_hi, (router_pad - router_hi.astype(F32)).astype(BF16)])
    h3, xn, logits = _mix_out_route(y_ret.reshape(n, -1), y_rw.reshape(n, -1), o_w_out[0].astype(BF16), h2,
                                    o_norm2_g[0], router_split)
    sorted_tok, sorted_w, items, pos = _route(logits[:, :N_EXPERTS], n)
    xs = xn.at[sorted_tok].get(mode="promise_in_bounds")
    ys = _experts(xs, sorted_w.reshape(-1, 1), *items, o_moe_w_gate[0].astype(BF16),
                  o_moe_w_up[0].astype(BF16), o_moe_w_down[0].astype(BF16))
    y0 = ys.at[pos[:, 0]].get(mode="promise_in_bounds")
    y1 = ys.at[pos[:, 1]].get(mode="promise_in_bounds")
    out = _combine_norm(h3, y0, y1, final_norm_g)
    return out.reshape(bsz, t, d)
```

```python
import functools

import numpy as np
import jax
import jax.numpy as jnp
from jax import lax
from jax.experimental import pallas as pl
from jax.experimental.pallas import tpu as pltpu

F32 = jnp.float32
BF16 = jnp.bfloat16

D_MODEL = 1024
EPS = 1e-6
ML_HEADS, ML_DIM, ML_W, ML_CONV = 4, 128, 512, 4
GLA_HEADS, GLA_DK, GLA_DV, GLA_RANK, GLA_TAU = 4, 64, 128, 16, 16.0
RET_HEADS, RET_DIM, RET_W = 4, 128, 512
ROPE_BASE = 10000.0
RW_HEADS, RW_DIM, RW_W = 8, 64, 512
RW_COLS = 1792
RW_LN_EPS = 64e-5
D_FF = 3584
N_EXPERTS = 8
TOP_K = 2

VMEM_LIMIT = 48 * 1024 * 1024
NEG = -1e30


def _cp(*sem):
    return pltpu.CompilerParams(dimension_semantics=sem, vmem_limit_bytes=VMEM_LIMIT)


def _sigmoid(x):
    return 1.0 / (1.0 + jnp.exp(-x))


def _silu(x):
    return x * _sigmoid(x)


def _log_sigmoid(x):
    return jnp.minimum(x, 0.0) - jnp.log(1.0 + jnp.exp(-jnp.abs(x)))


def _dot(a, b):
    return jnp.dot(a.astype(BF16), b.astype(BF16), preferred_element_type=F32)


def _dot_nt(a, b):
    return lax.dot_general(a.astype(BF16), b.astype(BF16), (((1,), (1,)), ((), ())), preferred_element_type=F32)


def _dot_tn(a, b):
    return jnp.dot(a.T.astype(BF16), b.astype(BF16), preferred_element_type=F32)


def _rms_rows(x, g):
    ms = jnp.mean(x * x, axis=-1, keepdims=True)
    return x * lax.rsqrt(ms + EPS) * g


def _mix_out_ffn(a, b, w_out, resid, g, wg, wu, wd, tm=512, tf=1792):
    n, d = resid.shape
    wa = a.shape[1]
    wb = b.shape[1]
    f = wg.shape[1]
    nf = f // tf

    def body(a_ref, b_ref, wo_ref, r_ref, g_ref, wg_ref, wu_ref, wd_ref, o_ref, xn_ref, acc_ref):
        j = pl.program_id(1)

        @pl.when(j == 0)
        def _():
            h = r_ref[...] + jnp.dot(a_ref[...], wo_ref[0:wa, :], preferred_element_type=F32)
            h = h + jnp.dot(b_ref[...], wo_ref[wa:wa + wb, :], preferred_element_type=F32)
            xn_ref[...] = _rms_rows(h, g_ref[...]).astype(BF16)
            acc_ref[...] = h

        xn = xn_ref[...]
        gg = jnp.dot(xn, wg_ref[...], preferred_element_type=F32)
        uu = jnp.dot(xn, wu_ref[...], preferred_element_type=F32)
        act = (_silu(gg) * uu).astype(BF16)
        acc_ref[...] += jnp.dot(act, wd_ref[...], preferred_element_type=F32)

        @pl.when(j == nf - 1)
        def _():
            o_ref[...] = acc_ref[...]

    return pl.pallas_call(
        body,
        grid=(n // tm, nf),
        in_specs=[pl.BlockSpec((tm, wa), lambda i, j: (i, 0)),
                  pl.BlockSpec((tm, wb), lambda i, j: (i, 0)),
                  pl.BlockSpec((wa + wb, d), lambda i, j: (0, 0)),
                  pl.BlockSpec((tm, d), lambda i, j: (i, 0)),
                  pl.BlockSpec((1, d), lambda i, j: (0, 0)),
                  pl.BlockSpec((d, tf), lambda i, j: (0, j)),
                  pl.BlockSpec((d, tf), lambda i, j: (0, j)),
                  pl.BlockSpec((tf, d), lambda i, j: (j, 0))],
        out_specs=pl.BlockSpec((tm, d), lambda i, j: (i, 0)),
        out_shape=jax.ShapeDtypeStruct((n, d), F32),
        scratch_shapes=[pltpu.VMEM((tm, d), BF16), pltpu.VMEM((tm, d), F32)],
        compiler_params=_cp("parallel", "arbitrary"),
        name="mix_out_ffn",
    )(a, b, w_out, resid, g.reshape(1, d), wg, wu, wd)


ML_L = 256


def _mlstm(x, in_g, w_in, w_gate, w_gate_t, gate_b_row, gate_b_col, conv_w, conv_b, norm_g):
    bsz, t, d = x.shape
    L, D, H, W = ML_L, ML_DIM, ML_HEADS, ML_W

    def body(x_ref, ing_ref, w_ref, wg_ref, wgt_ref, gbr_ref, gbc_ref, cwq_ref, cwk_ref, cbq_ref, cbk_ref,
             ng_ref, out_ref, qext, kext, c_ref, n_ref, m_ref):
        c = pl.program_id(1)

        @pl.when(c == 0)
        def _():
            qext[0:8, :] = jnp.zeros((8, W), F32)
            kext[0:8, :] = jnp.zeros((8, W), F32)
            c_ref[...] = jnp.zeros_like(c_ref)
            n_ref[...] = jnp.zeros_like(n_ref)
            m_ref[...] = jnp.zeros_like(m_ref)

        xn = _rms_rows(x_ref[0], ing_ref[...]).astype(BF16)
        p = jnp.dot(xn, w_ref[...], preferred_element_type=F32)
        g_cols = jnp.dot(xn, wg_ref[...], preferred_element_type=F32)[:, 0:2 * H]
        g_rows = lax.dot_general(wgt_ref[...], xn, (((1,), (1,)), ((), ())),
                                 preferred_element_type=F32)[0:2 * H, :]
        qext[8:, :] = p[:, 0:W]
        kext[8:, :] = p[:, W:2 * W]

        def conv(ext, cw_ref, cb_ref):
            acc = cb_ref[...] + cw_ref[0:1, :] * ext[pl.ds(8 - ML_CONV + 1, L), :]
            for kk in range(1, ML_CONV):
                acc = acc + cw_ref[kk:kk + 1, :] * ext[pl.ds(8 - ML_CONV + 1 + kk, L), :]
            return _silu(acc)

        q_all = conv(qext, cwq_ref, cbq_ref) * (D ** -0.5)
        k_all = conv(kext, cwk_ref, cbk_ref)
        qext[0:8, :] = qext[L:L + 8, :]
        kext[0:8, :] = kext[L:L + 8, :]

        gcol = g_cols + gbr_ref[...]
        grow = g_rows + gbc_ref[...]
        fcol = _log_sigmoid(gcol[:, H:2 * H])
        frow = _log_sigmoid(grow[H:2 * H, :])
        ri = lax.broadcasted_iota(jnp.int32, (L, L), 0)
        ci = lax.broadcasted_iota(jnp.int32, (L, L), 1)
        causal = ri >= ci
        heads = range(H)
        hs = lambda z, h: z[:, h * D:(h + 1) * D]
        w_intra, w_inter, w_state, carry, m_row, m_new = [], [], [], [], [], []
        for h in heads:
            f_row = frow[h:h + 1, :]
            i_row = grow[h:h + 1, :]
            f_col = fcol[:, h:h + 1]
            i_col = gcol[:, h:h + 1]
            m_st = m_ref[h]
            b_col = jnp.sum(jnp.where(causal, f_row, 0.0), axis=1, keepdims=True)
            b_row = jnp.sum(jnp.where(ri <= ci, f_col, 0.0), axis=0, keepdims=True)
            g_tot = jnp.sum(f_row, axis=1, keepdims=True)
            d_intra = jnp.where(causal, b_col - b_row + i_row, NEG)
            d_inter = b_col + m_st
            mr = jnp.maximum(d_inter, jnp.max(d_intra, axis=1, keepdims=True))
            w_intra.append(jnp.exp(d_intra - mr))
            w_inter.append(jnp.exp(d_inter - mr))
            m_row.append(mr)
            d_state = g_tot - b_col + i_col
            mn = jnp.maximum(g_tot + m_st, jnp.max(d_state, axis=0, keepdims=True))
            w_state.append(jnp.exp(d_state - mn))
            carry.append(jnp.exp(g_tot + m_st - mn))
            m_new.append(mn)
        qh = [hs(q_all, h) for h in heads]
        kh = [hs(k_all, h) for h in heads]
        vh = [p[:, 2 * W + h * D:2 * W + (h + 1) * D] for h in heads]
        c_st = [c_ref[h] for h in heads]
        n_st = [n_ref[h] for h in heads]
        s = [_dot_nt(qh[h], kh[h]) * w_intra[h] for h in heads]
        qc = [_dot(qh[h], c_st[h]) for h in heads]
        kw = [kh[h] * w_state[h] for h in heads]
        kv = [_dot_tn(kw[h], vh[h]) for h in heads]
        sv = [_dot(s[h], vh[h]) for h in heads]
        for h in heads:
            num = sv[h] + w_inter[h] * qc[h]
            den = jnp.sum(s[h], axis=1, keepdims=True) + w_inter[h] * jnp.sum(qh[h] * n_st[h], axis=1, keepdims=True)
            hval = num / jnp.maximum(jnp.abs(den), jnp.exp(-m_row[h]))
            c_ref[h] = carry[h] * c_st[h] + kv[h]
            n_ref[h] = carry[h] * n_st[h] + jnp.sum(kw[h], axis=0, keepdims=True)
            m_ref[h] = m_new[h]
            hg = _sigmoid(p[:, 3 * W + h * D:3 * W + (h + 1) * D]) * hval
            hc = hg - jnp.mean(hg, axis=1, keepdims=True)
            hn = hc * lax.rsqrt(jnp.mean(hc * hc, axis=1, keepdims=True) + EPS)
            out_ref[0, :, h * D:(h + 1) * D] = (hn * ng_ref[:, h * D:(h + 1) * D]).astype(BF16)

    fix = lambda j: (lambda b, c: (0, j))
    return pl.pallas_call(
        body,
        grid=(bsz, t // L),
        in_specs=[pl.BlockSpec((1, L, d), lambda b, c: (b, c, 0)),
                  pl.BlockSpec((1, d), fix(0)),
                  pl.BlockSpec((d, 4 * W), fix(0)),
                  pl.BlockSpec((d, 128), fix(0)),
                  pl.BlockSpec((16, d), fix(0)),
                  pl.BlockSpec((1, 2 * H), fix(0)),
                  pl.BlockSpec((2 * H, 1), fix(0)),
                  pl.BlockSpec((ML_CONV, W), fix(0)),
                  pl.BlockSpec((ML_CONV, W), fix(1)),
                  pl.BlockSpec((1, W), fix(0)),
                  pl.BlockSpec((1, W), fix(1)),
                  pl.BlockSpec((1, W), fix(0))],
        out_specs=pl.BlockSpec((1, L, W), lambda b, c: (b, c, 0)),
        out_shape=jax.ShapeDtypeStruct((bsz, t, W), BF16),
        scratch_shapes=[pltpu.VMEM((L + 8, W), F32), pltpu.VMEM((L + 8, W), F32),
                        pltpu.VMEM((H, D, D), F32), pltpu.VMEM((H, 1, D), F32), pltpu.VMEM((H, 1, 1), F32)],
        compiler_params=_cp("parallel", "arbitrary"),
        name="mlstm",
    )(x, in_g, w_in, w_gate, w_gate_t, gate_b_row, gate_b_col, conv_w, conv_w, conv_b, conv_b, norm_g)


GLA_TC = 256
GLA_SUB = 16
GLA_GROUP = 128


def _gla(x, in_g, w_pairs, w_low, gate_up, gate_b, norm_g):
    bsz, t, d = x.shape
    tc, S, GB = GLA_TC, GLA_SUB, GLA_GROUP
    head_ones = (jnp.arange(2 * GLA_DK)[:, None] // GLA_DK == jnp.arange(2 * GB)[None, :] // GB).astype(BF16)
    nsub = tc // S
    dk, dv = GLA_DK, GLA_DV

    def body(x_ref, ing_ref, w_ref, wl_ref, gu_ref, gbias_ref, ng_ref, ones_ref, out_ref, st_ref, ksh, bsh):
        c = pl.program_id(2)

        @pl.when(c == 0)
        def _():
            st_ref[...] = jnp.zeros_like(st_ref)
            ksh[0:S, :] = jnp.zeros((S, 2 * dk), F32)
            bsh[0:S, :] = jnp.zeros((S, 2 * dk), F32)

        xn = _rms_rows(x_ref[0], ing_ref[...]).astype(BF16)
        p = jnp.dot(xn, w_ref[0], preferred_element_type=F32)
        g_low = jnp.dot(xn, wl_ref[...], preferred_element_type=F32)[:, 0:GLA_RANK]
        z = _dot(g_low, gu_ref[...]) + gbias_ref[...]
        la = _log_sigmoid(z) / GLA_TAU
        rowi = lax.broadcasted_iota(jnp.int32, (tc, 1), 0)
        rmod = rowi & (S - 1)
        bcum = la
        rsum = la
        for s in (1, 2, 4, 8):
            bcum = bcum + jnp.where(rmod >= s, pltpu.roll(bcum, s, 0), 0.0)
            rsum = rsum + jnp.where(rmod < S - s, pltpu.roll(rsum, tc - s, 0), 0.0)
        q = p[:, 0:2 * dk] * (dk ** -0.5)
        k = p[:, 2 * dk:4 * dk]
        v = p[:, 4 * dk:4 * dk + 2 * dv]
        gate = p[:, 4 * dk + 2 * dv:4 * dk + 4 * dv]
        qt = q * jnp.exp(bcum)
        kt = k * jnp.exp(rsum - la)
        eg = jnp.exp(bcum + rsum - la)

        ksh[S:, :] = k
        bsh[S:, :] = bcum
        prods = []
        for d in range(S):
            kd = k if d == 0 else ksh[pl.ds(S - d, tc), :]
            bd = bcum if d == 0 else bsh[pl.ds(S - d, tc), :]
            e = jnp.exp(jnp.where(rmod >= d, bcum - bd, 0.0))
            prods.append((q * kd * e).astype(BF16))
        ws = [jnp.dot(p, ones_ref[...], preferred_element_type=F32) for p in prods]
        coli = lax.broadcasted_iota(jnp.int32, (tc, GB), 1)
        rgrp = rowi & (GB - 1)
        att0 = jnp.zeros((tc, GB), F32)
        att1 = jnp.zeros((tc, GB), F32)
        for d in range(S):
            here = jnp.logical_and(coli == rgrp - d, rmod >= d)
            att0 = jnp.where(here, ws[d][:, 0:GB], att0)
            att1 = jnp.where(here, ws[d][:, GB:2 * GB], att1)

        heads = range(2)
        lk = [slice(hh * dk, (hh + 1) * dk) for hh in heads]
        lv = [slice(hh * dv, (hh + 1) * dv) for hh in heads]
        kv = [[_dot_tn(v[si * S:(si + 1) * S, lv[hh]], kt[si * S:(si + 1) * S, lk[hh]]) for hh in heads]
              for si in range(nsub)]
        st = [st_ref[hh] for hh in heads]
        inter = [[], []]
        for si in range(nsub):
            rows = slice(si * S, (si + 1) * S)
            for hh in heads:
                inter[hh].append(_dot_nt(qt[rows, lk[hh]], st[hh]))
                st[hh] = st[hh] * eg[si * S:si * S + 1, lk[hh]] + kv[si][hh]
        for hh in heads:
            st_ref[hh] = st[hh]

        for hh, att in ((0, att0), (1, att1)):
            diag = jnp.concatenate([_dot(att[g * GB:(g + 1) * GB], v[g * GB:(g + 1) * GB, lv[hh]])
                                    for g in range(tc // GB)], axis=0)
            o = diag + jnp.concatenate(inter[hh], axis=0)
            on = o * lax.rsqrt(jnp.mean(o * o, axis=1, keepdims=True) + EPS)
            out_ref[0, :, lv[hh]] = (on * ng_ref[:, lv[hh]] * _silu(gate[:, lv[hh]])).astype(BF16)

    pw = 4 * dk + 4 * dv
    return pl.pallas_call(
        body,
        grid=(bsz, GLA_HEADS // 2, t // tc),
        in_specs=[pl.BlockSpec((1, tc, d), lambda b, h, c: (b, c, 0)),
                  pl.BlockSpec((1, d), lambda b, h, c: (0, 0)),
                  pl.BlockSpec((1, d, pw), lambda b, h, c: (h, 0, 0)),
                  pl.BlockSpec((d, 128), lambda b, h, c: (0, 0)),
                  pl.BlockSpec((GLA_RANK, 2 * dk), lambda b, h, c: (0, h)),
                  pl.BlockSpec((1, 2 * dk), lambda b, h, c: (0, h)),
                  pl.BlockSpec((1, 2 * dv), lambda b, h, c: (0, h)),
                  pl.BlockSpec((2 * dk, 2 * GB), lambda b, h, c: (0, 0))],
        out_specs=pl.BlockSpec((1, tc, 2 * dv), lambda b, h, c: (b, c, h)),
        out_shape=jax.ShapeDtypeStruct((bsz, t, GLA_HEADS * dv), BF16),
        scratch_shapes=[pltpu.VMEM((2, dv, dk), F32), pltpu.VMEM((tc + S, 2 * dk), F32),
                        pltpu.VMEM((tc + S, 2 * dk), F32)],
        compiler_params=_cp("parallel", "parallel", "arbitrary"),
        name="gla",
    )(x, in_g, w_pairs, w_low, gate_up, gate_b, norm_g, head_ones)


RET_L = 256


def _retention(x, in_g, w_in, cos_t, sin_t, intra, inter, sdec, cdec, norm_g):
    bsz, t, d = x.shape
    L, D, H, W = RET_L, RET_DIM, RET_HEADS, RET_W

    def body(x_ref, ing_ref, w_ref, cos_ref, sin_ref, intra_ref, inter_ref, sdec_ref, cdec_ref, ng_ref,
             out_ref, s_ref):
        c = pl.program_id(1)

        @pl.when(c == 0)
        def _():
            s_ref[...] = jnp.zeros_like(s_ref)

        p = jnp.dot(_rms_rows(x_ref[0], ing_ref[...]).astype(BF16), w_ref[...], preferred_element_type=F32)
        cs = cos_ref[...]
        sn = sin_ref[...]

        def rot(z):
            return z * cs + pltpu.roll(z, D // 2, 1) * sn

        heads = range(H)
        q = [rot(p[:, h * D:(h + 1) * D]) * (D ** -0.5) for h in heads]
        k = [rot(p[:, W + h * D:W + (h + 1) * D]) for h in heads]
        v = [p[:, 2 * W + h * D:2 * W + (h + 1) * D] for h in heads]
        s_st = [s_ref[h] for h in heads]
        s = [_dot_nt(q[h], k[h]) * intra_ref[h] for h in heads]
        qs = [_dot(q[h], s_st[h]) for h in heads]
        kv = [_dot_tn(k[h] * sdec_ref[h], v[h]) for h in heads]
        sv = [_dot(s[h], v[h]) for h in heads]
        for h in heads:
            o = sv[h] + inter_ref[h] * qs[h]
            s_ref[h] = cdec_ref[h] * s_st[h] + kv[h]
            oc = o - jnp.mean(o, axis=1, keepdims=True)
            on = oc * lax.rsqrt(jnp.mean(oc * oc, axis=1, keepdims=True) + EPS)
            hl = slice(h * D, (h + 1) * D)
            gate = p[:, 3 * W + h * D:3 * W + (h + 1) * D]
            out_ref[0, :, hl] = (on * ng_ref[:, hl] * _silu(gate)).astype(BF16)

    fix2 = lambda b, c: (0, 0)
    fix3 = lambda b, c: (0, 0, 0)
    return pl.pallas_call(
        body,
        grid=(bsz, t // L),
        in_specs=[pl.BlockSpec((1, L, d), lambda b, c: (b, c, 0)),
                  pl.BlockSpec((1, d), fix2),
                  pl.BlockSpec((d, 4 * W), fix2),
                  pl.BlockSpec((L, D), lambda b, c: (c, 0)),
                  pl.BlockSpec((L, D), lambda b, c: (c, 0)),
                  pl.BlockSpec((H, L, L), fix3),
                  pl.BlockSpec((H, L, 1), fix3),
                  pl.BlockSpec((H, L, 1), fix3),
                  pl.BlockSpec((H, 1, 1), fix3),
                  pl.BlockSpec((1, W), lambda b, c: (0, 0))],
        out_specs=pl.BlockSpec((1, L, W), lambda b, c: (b, c, 0)),
        out_shape=jax.ShapeDtypeStruct((bsz, t, W), BF16),
        scratch_shapes=[pltpu.VMEM((H, D, D), F32)],
        compiler_params=_cp("parallel", "arbitrary"),
        name="retention",
    )(x, in_g, w_in, cos_t, sin_t, intra, inter, sdec, cdec, norm_g)


def _retention_tables(t):
    L, D = RET_L, RET_DIM
    inv = ROPE_BASE ** (-jnp.arange(0, D, 2, dtype=F32) / D)
    ang = jnp.arange(t).astype(F32)[:, None] * inv[None, :]
    cos = jnp.cos(ang)
    sin = jnp.sin(ang)
    cos_t = jnp.concatenate([cos, cos], axis=-1)
    sin_t = jnp.concatenate([-sin, sin], axis=-1)
    log_gamma = jnp.log1p(-jnp.exp2(-5.0 - jnp.arange(RET_HEADS, dtype=F32)))
    idx = jnp.arange(L, dtype=F32)
    causal = idx[:, None] >= idx[None, :]
    rel = jnp.where(causal, idx[:, None] - idx[None, :], 0.0)
    intra = jnp.where(causal, jnp.exp(log_gamma[:, None, None] * rel), 0.0)
    inter = jnp.exp(log_gamma[:, None] * (idx + 1.0))[:, :, None]
    sdec = jnp.exp(log_gamma[:, None] * (L - 1.0 - idx))[:, :, None]
    cdec = jnp.exp(log_gamma * L)[:, None, None]
    return cos_t, sin_t, intra, inter, sdec, cdec


RWP_TM = 512


def _seg_sum(x, bd):
    hi = x.astype(BF16)
    lo = (x - hi.astype(F32)).astype(BF16)
    return jnp.dot(hi, bd, preferred_element_type=F32) + jnp.dot(lo, bd, preferred_element_type=F32)


def _rwkv_prep(x, in_g, w_in, mu, w_up, w0, a_up, a0, g_up, k_k, k_a, r_k, bd):
    bsz, t, d = x.shape
    tm, W = RWP_TM, RW_W

    def body(x_ref, ing_ref, win_ref, mu_ref, wup_ref, w0_ref, aup_ref, a0_ref, gup_ref, kk_ref, ka_ref, rk_ref,
             bd_ref, r_out, lw_out, k_out, v_out, a_out, b_out, g_out, bonus_out, last_ref):
        c = pl.program_id(1)

        @pl.when(c == 0)
        def _():
            last_ref[...] = jnp.zeros_like(last_ref)

        cur = jnp.dot(_rms_rows(x_ref[0], ing_ref[...]).astype(BF16), win_ref[...], preferred_element_type=F32)
        prev = last_ref[...]
        last_ref[...] = cur[tm - 1:tm, :]
        rowi = lax.broadcasted_iota(jnp.int32, (tm, 1), 0)
        sh = jnp.where(rowi == 0, prev, pltpu.roll(cur, 1, 0))
        xm = cur + (sh - cur) * mu_ref[...]
        x_r = xm[:, 0:W]
        x_k = xm[:, W:2 * W]
        x_v = xm[:, 2 * W:3 * W]
        x_dl = xm[:, 3 * W:3 * W + 64]
        x_al = xm[:, 3 * W + 64:3 * W + 128]
        x_gl = xm[:, 3 * W + 128:3 * W + 256]
        wl = w0_ref[...] + _dot(jnp.tanh(x_dl), wup_ref[...])
        sp = jnp.maximum(-wl, 0.0) + jnp.log(1.0 + jnp.exp(-jnp.abs(wl)))
        lw_out[0] = -jnp.exp(-sp - 0.5)
        a = _sigmoid(a0_ref[...] + _dot(x_al, aup_ref[...]))
        g_out[0] = _dot(_sigmoid(x_gl), gup_ref[...]).astype(BF16)
        kk0 = x_k * kk_ref[...]
        nrm = jnp.sqrt(_seg_sum(kk0 * kk0, bd_ref[...]))
        kk = kk0 / jnp.maximum(nrm, 1e-12)
        k_h = x_k * (1.0 + (a - 1.0) * ka_ref[...])
        r_out[0] = x_r.astype(BF16)
        k_out[0] = k_h.astype(BF16)
        v_out[0] = x_v.astype(BF16)
        a_out[0] = (-kk).astype(BF16)
        b_out[0] = (kk * a).astype(BF16)
        bonus_out[0] = (_seg_sum(x_r * k_h * rk_ref[...], bd_ref[...]) * x_v).astype(BF16)

    row = lambda b, c: (0, 0)
    blk = pl.BlockSpec((1, tm, W), lambda b, c: (b, c, 0))
    shp = [jax.ShapeDtypeStruct((bsz, t, W), F32 if i == 1 else BF16) for i in range(8)]
    return pl.pallas_call(
        body,
        grid=(bsz, t // tm),
        in_specs=[pl.BlockSpec((1, tm, d), lambda b, c: (b, c, 0)),
                  pl.BlockSpec((1, d), row),
                  pl.BlockSpec((d, RW_COLS), row),
                  pl.BlockSpec((1, RW_COLS), row),
                  pl.BlockSpec((64, W), row), pl.BlockSpec((1, W), row),
                  pl.BlockSpec((64, W), row), pl.BlockSpec((1, W), row),
                  pl.BlockSpec((128, W), row),
                  pl.BlockSpec((1, W), row), pl.BlockSpec((1, W), row), pl.BlockSpec((1, W), row),
                  pl.BlockSpec((W, W), row)],
        out_specs=[blk] * 8,
        out_shape=shp,
        scratch_shapes=[pltpu.VMEM((1, RW_COLS), F32)],
        compiler_params=_cp("parallel", "arbitrary"),
        name="rwkv_prep",
    )(x, in_g, w_in, mu, w_up, w0, a_up, a0, g_up, k_k, k_a, r_k, bd)


RW_L = 64
RW_TB = 128


def _rwkv_scan(r, lw, k, v, aa, bb, g_out, bonus, ln_g, ln_b):
    bsz, t, W = r.shape
    L, N, tb = RW_L, RW_DIM, RW_TB
    nck = tb // L

    def body(r_ref, lw_ref, k_ref, v_ref, a_ref, b_ref, g_ref, bonus_ref, lng_ref, lnb_ref, out_ref, h_ref):
        c = pl.program_id(1)

        @pl.when(c == 0)
        def _():
            h_ref[...] = jnp.zeros_like(h_ref)

        lw_all = lw_ref[0]
        rowi = lax.broadcasted_iota(jnp.int32, (tb, 1), 0) & (L - 1)
        cl = lw_all
        for s in (1, 2, 4, 8, 16, 32):
            cl = cl + jnp.where(rowi >= s, pltpu.roll(cl, s, 0), 0.0)
        cl_last = jnp.concatenate([jnp.broadcast_to(cl[(cc + 1) * L - 1:(cc + 1) * L, :], (L, W))
                                   for cc in range(nck)], axis=0)
        e_inv = jnp.exp(-cl)
        e_end = jnp.exp(cl_last - cl)
        p_end = jnp.exp(cl_last)
        at = a_ref[0] * jnp.exp(cl - lw_all)
        rt = r_ref[0] * jnp.exp(cl)
        bt = b_ref[0] * e_inv
        kt = k_ref[0] * e_inv
        b_end = b_ref[0] * e_end
        k_end = k_ref[0] * e_end
        v_all = v_ref[0]
        P2 = 2 * N
        pairs = [(cc, p) for cc in range(nck) for p in range(RW_HEADS // 2)]
        pr = range(len(pairs))

        def pb(z, pi):
            cc, p = pairs[pi]
            return z[cc * L:(cc + 1) * L, p * P2:(p + 1) * P2]

        def stack2(z):
            lane = lax.broadcasted_iota(jnp.int32, (1, z.shape[1]), 1) & (P2 - 1)
            return jnp.concatenate([jnp.where(lane < N, z, 0.0), jnp.where(lane >= N, z, 0.0)], axis=0)

        tcol = lax.broadcasted_iota(jnp.int32, (L, P2), 1) & (N - 1)
        trow = lax.broadcasted_iota(jnp.int32, (L, P2), 0)
        strict = trow > tcol
        lower = trow >= tcol
        r2 = lax.broadcasted_iota(jnp.int32, (P2, P2), 0)
        c2 = lax.broadcasted_iota(jnp.int32, (P2, P2), 1)
        same_head = (r2 // N) == (c2 // N)
        eye2 = r2 == c2
        x2 = [jnp.concatenate([pb(at, pi), pb(rt, pi)], axis=0) for pi in pr]
        m_b = [_dot_nt(x2[pi], stack2(pb(bt, pi))) for pi in pr]
        m_k = [_dot_nt(x2[pi], stack2(pb(kt, pi))) for pi in pr]
        ap = [jnp.where(strict, m_b[pi][0:L], 0.0) for pi in pr]
        a_rb = [jnp.where(lower, m_b[pi][L:2 * L], 0.0) for pi in pr]
        a_ak = [jnp.where(strict, m_k[pi][0:L], 0.0) for pi in pr]
        a_rk = [jnp.where(lower, m_k[pi][L:2 * L], 0.0) for pi in pr]
        vp = [_dot(jnp.concatenate([a_ak[pi], a_rk[pi]], axis=0), stack2(pb(v_all, pi))) for pi in pr]
        kv = [_dot(pb(k_end, pi).T, pb(v_all, pi)) for pi in pr]
        x = [jnp.concatenate([pb(at, pi), vp[pi][0:L]], axis=1) for pi in pr]
        for it in range(6):
            x = [x[pi] + _dot(ap[pi], stack2(x[pi])) for pi in pr]
            if it < 5:
                ap = [_dot(ap[pi], stack2(ap[pi])) for pi in pr]
        post1 = [_dot(a_rb[pi], stack2(x[pi])) for pi in pr]
        post2 = [_dot(pb(b_end, pi).T, x[pi]) for pi in pr]
        lhs = []
        y0s = []
        h_adds = []
        for pi in pr:
            q_hat = pb(rt, pi) + post1[pi][:, 0:P2]
            gmat = (jnp.where(same_head, post2[pi][:, 0:P2], 0.0)
                    + jnp.where(eye2, pb(p_end, pi)[0:1, :], 0.0))
            lhs.append(jnp.concatenate([q_hat, gmat], axis=0))
            y0s.append(post1[pi][:, P2:2 * P2] + vp[pi][L:2 * L])
            h_adds.append(jnp.where(same_head, post2[pi][:, P2:2 * P2] + kv[pi], 0.0))
        npair = RW_HEADS // 2
        h_st = [h_ref[p] for p in range(npair)]
        ys = [None] * len(pairs)
        for cc in range(nck):
            res = [_dot(lhs[cc * npair + p], h_st[p]) for p in range(npair)]
            for p in range(npair):
                pi = cc * npair + p
                ys[pi] = res[p][0:L] + y0s[pi]
                h_st[p] = res[p][L:L + P2] + h_adds[pi]
        for p in range(npair):
            h_ref[p] = h_st[p]
        seg = same_head.astype(BF16)
        mean = [_seg_sum(ys[pi], seg) * (1.0 / N) for pi in pr]
        yc = [ys[pi] - mean[pi] for pi in pr]
        var = [_seg_sum(yc[pi] * yc[pi], seg) * (1.0 / N) for pi in pr]
        for pi, (cc, p) in enumerate(pairs):
            yn = yc[pi] * lax.rsqrt(var[pi] + RW_LN_EPS)
            rows = slice(cc * L, (cc + 1) * L)
            cols = slice(p * P2, (p + 1) * P2)
            out_ref[0, rows, cols] = ((yn * lng_ref[:, cols] + lnb_ref[:, cols] + bonus_ref[0, rows, cols])
                                      * g_ref[0, rows, cols]).astype(BF16)

    blk = pl.BlockSpec((1, tb, W), lambda b, c: (b, c, 0))
    vec = pl.BlockSpec((1, W), lambda b, c: (0, 0))
    return pl.pallas_call(
        body,
        grid=(bsz, t // tb),
        in_specs=[blk] * 8 + [vec, vec],
        out_specs=blk,
        out_shape=jax.ShapeDtypeStruct((bsz, t, W), BF16),
        scratch_shapes=[pltpu.VMEM((RW_HEADS // 2, 2 * N, 2 * N), F32)],
        compiler_params=_cp("parallel", "arbitrary"),
        name="rwkv_scan",
    )(r, lw, k, v, aa, bb, g_out, bonus, ln_g, ln_b)


def _mix_out_route(a, b, w_bf16, resid, g, router_split, tm=1024):
    n, wa = a.shape
    wb = b.shape[1]
    d = w_bf16.shape[1]
    e = router_split.shape[2]

    def body(a_ref, b_ref, w_ref, r_ref, g_ref, rt_ref, h_ref, xn_ref, lg_ref):
        acc = jnp.dot(a_ref[...], w_ref[0:wa, :], preferred_element_type=F32)
        acc = acc + jnp.dot(b_ref[...], w_ref[wa:wa + wb, :], preferred_element_type=F32)
        h = r_ref[...] + acc
        h_ref[...] = h
        xn = _rms_rows(h, g_ref[...])
        x_hi = xn.astype(BF16)
        xn_ref[...] = x_hi
        x_lo = (xn - x_hi.astype(F32)).astype(BF16)
        r_hi = rt_ref[0]
        r_lo = rt_ref[1]
        lg_ref[...] = (jnp.dot(x_hi, r_hi, preferred_element_type=F32)
                       + (jnp.dot(x_lo, r_hi, preferred_element_type=F32)
                          + jnp.dot(x_hi, r_lo, preferred_element_type=F32)))

    return pl.pallas_call(
        body,
        grid=(n // tm,),
        in_specs=[pl.BlockSpec((tm, wa), lambda i: (i, 0)),
                  pl.BlockSpec((tm, wb), lambda i: (i, 0)),
                  pl.BlockSpec((wa + wb, d), lambda i: (0, 0)),
                  pl.BlockSpec((tm, d), lambda i: (i, 0)),
                  pl.BlockSpec((1, d), lambda i: (0, 0)),
                  pl.BlockSpec((2, d, e), lambda i: (0, 0, 0))],
        out_specs=[pl.BlockSpec((tm, d), lambda i: (i, 0)), pl.BlockSpec((tm, d), lambda i: (i, 0)),
                   pl.BlockSpec((tm, e), lambda i: (i, 0))],
        out_shape=[jax.ShapeDtypeStruct((n, d), F32), jax.ShapeDtypeStruct((n, d), BF16),
                   jax.ShapeDtypeStruct((n, e), F32)],
        compiler_params=_cp("parallel"),
        name="mix_out_route",
    )(a, b, w_bf16, resid, g.reshape(1, d), router_split)


MOE_TM = 512
MOE_TF = 1792


def _experts(xs, row_w, item_tile, item_exp, item_lo, item_hi, wg, wu, wd):
    nrows, d = xs.shape
    tm, tf = MOE_TM, MOE_TF
    nf = D_FF // tf
    n_items = item_tile.shape[0]

    def body(it_ref, ie_ref, lo_ref, hi_ref, x_ref, w_ref, wg_ref, wu_ref, wd_ref, o_ref, acc_ref):
        i = pl.program_id(0)
        j = pl.program_id(1)
        tile = it_ref[i]
        first = jnp.logical_or(i == 0, tile != it_ref[jnp.maximum(i - 1, 0)])
        last = jnp.logical_or(i == n_items - 1, tile != it_ref[jnp.minimum(i + 1, n_items - 1)])

        @pl.when(jnp.logical_and(first, j == 0))
        def _():
            acc_ref[...] = jnp.zeros_like(acc_ref)

        lo = lo_ref[i]
        hi = hi_ref[i]

        @pl.when(lo < hi)
        def _():
            x = x_ref[...]
            gg = jnp.dot(x, wg_ref[0], preferred_element_type=F32)
            uu = jnp.dot(x, wu_ref[0], preferred_element_type=F32)
            act = (_silu(gg) * uu).astype(BF16)
            part = jnp.dot(act, wd_ref[0], preferred_element_type=F32)
            rowi = lax.broadcasted_iota(jnp.int32, (tm, 1), 0)
            mine = jnp.logical_and(rowi >= lo, rowi < hi)
            acc_ref[...] += part * jnp.where(mine, w_ref[...], 0.0)

        @pl.when(jnp.logical_and(last, j == nf - 1))
        def _():
            o_ref[...] = acc_ref[...].astype(o_ref.dtype)

    grid_spec = pltpu.PrefetchScalarGridSpec(
        num_scalar_prefetch=4,
        grid=(n_items, nf),
        in_specs=[pl.BlockSpec((tm, d), lambda i, j, it, ie, lo, hi: (it[i], 0)),
                  pl.BlockSpec((tm, 1), lambda i, j, it, ie, lo, hi: (it[i], 0)),
                  pl.BlockSpec((1, d, tf), lambda i, j, it, ie, lo, hi: (ie[i], 0, j)),
                  pl.BlockSpec((1, d, tf), lambda i, j, it, ie, lo, hi: (ie[i], 0, j)),
                  pl.BlockSpec((1, tf, d), lambda i, j, it, ie, lo, hi: (ie[i], j, 0))],
        out_specs=pl.BlockSpec((tm, d), lambda i, j, it, ie, lo, hi: (it[i], 0)),
        scratch_shapes=[pltpu.VMEM((tm, d), F32)],
    )
    return pl.pallas_call(
        body,
        grid_spec=grid_spec,
        out_shape=jax.ShapeDtypeStruct((nrows, d), BF16),
        compiler_params=_cp("arbitrary", "arbitrary"),
        name="moe_experts",
    )(item_tile, item_exp, item_lo, item_hi, xs, row_w, wg, wu, wd)


def _combine_norm(h, y0, y1, g, tm=512):
    n, d = h.shape

    def body(h_ref, a_ref, b_ref, g_ref, o_ref):
        o_ref[...] = _rms_rows(h_ref[...] + (a_ref[...].astype(F32) + b_ref[...].astype(F32)), g_ref[...])

    blk = pl.BlockSpec((tm, d), lambda i: (i, 0))
    return pl.pallas_call(
        body,
        grid=(n // tm,),
        in_specs=[blk, blk, blk, pl.BlockSpec((1, d), lambda i: (0, 0))],
        out_specs=blk,
        out_shape=jax.ShapeDtypeStruct((n, d), F32),
        compiler_params=_cp("parallel"),
        name="combine_norm",
    )(h, y0, y1, g.reshape(1, d))


def _route(logits, n):
    tm = MOE_TM
    na = n * TOP_K
    n_tiles = na // tm
    top_val, top_idx = lax.top_k(logits, TOP_K)
    top_w = jax.nn.softmax(top_val, axis=-1)
    e_flat = top_idx.reshape(-1).astype(jnp.int32)
    w_flat = top_w.reshape(-1)
    tok = jnp.arange(na, dtype=jnp.int32) // TOP_K
    _, sorted_tok, sorted_w = lax.sort((e_flat, tok, w_flat), num_keys=1, is_stable=True)
    onehot = (e_flat[:, None] == jnp.arange(N_EXPERTS, dtype=jnp.int32)[None, :]).astype(jnp.int32)
    rank = jnp.take_along_axis(jnp.cumsum(onehot, axis=0), e_flat[:, None], axis=1)[:, 0] - 1
    counts = jnp.sum(onehot, axis=0)
    ends = jnp.cumsum(counts)
    pos = (ends - counts)[e_flat] + rank
    cuts = jnp.sort(jnp.concatenate([jnp.arange(n_tiles, dtype=jnp.int32) * tm, ends[:-1].astype(jnp.int32)]))
    nxt = jnp.concatenate([cuts[1:], jnp.full((1,), na, jnp.int32)])
    item_tile = jnp.minimum(cuts // tm, n_tiles - 1)
    item_exp = jnp.minimum(jnp.searchsorted(ends, cuts, side="right"), N_EXPERTS - 1).astype(jnp.int32)
    item_lo = cuts - item_tile * tm
    item_hi = nxt - item_tile * tm
    return sorted_tok, sorted_w, (item_tile, item_exp, item_lo, item_hi), pos.reshape(n, TOP_K)


def kernel(x, e_norm1_g, e_w_in, e_ml_conv_w, e_ml_conv_b, e_ml_gate_b, e_ml_norm_g, e_gla_gate_up, e_gla_gate_b,
           e_gla_norm_g, e_w_out, e_norm2_g, e_ffn_w_gate, e_ffn_w_up, e_ffn_w_down, o_norm1_g, o_w_in,
           o_ret_norm_g, o_rw_mu, o_rw_w_up, o_rw_w0, o_rw_a_up, o_rw_a0, o_rw_g_up, o_rw_k_k, o_rw_k_a, o_rw_r_k,
           o_rw_ln_g, o_rw_ln_b, o_w_out, o_norm2_g, o_moe_router, o_moe_w_gate, o_moe_w_up, o_moe_w_down,
           final_norm_g):
    bsz, t, d = x.shape
    n = bsz * t
    h0 = x.reshape(n, d)

    w = e_w_in[0]
    row = lambda a: a.reshape(1, -1)
    x3 = x
    ng1 = row(e_norm1_g[0])
    w_if = w[:, 2048:2056]
    w_gate = jnp.zeros((d, 128), F32).at[:, :2 * ML_HEADS].set(w_if).astype(BF16)
    w_gate_t = jnp.zeros((16, d), F32).at[:2 * ML_HEADS, :].set(w_if.T).astype(BF16)
    h_ml = _mlstm(x3, ng1, w[:, :4 * ML_W].astype(BF16), w_gate, w_gate_t, row(e_ml_gate_b[0]),
                  e_ml_gate_b[0].reshape(-1, 1), e_ml_conv_w[0], row(e_ml_conv_b[0]), row(e_ml_norm_g[0]))
    gq, gk, gv, gr = 2056, 2312, 2568, 3080
    w_pairs = jnp.stack([jnp.concatenate([w[:, gq + 128 * hp:gq + 128 * (hp + 1)],
                                          w[:, gk + 128 * hp:gk + 128 * (hp + 1)],
                                          w[:, gv + 256 * hp:gv + 256 * (hp + 1)],
                                          w[:, gr + 256 * hp:gr + 256 * (hp + 1)]], axis=1)
                         for hp in range(GLA_HEADS // 2)]).astype(BF16)
    w_low = jnp.zeros((d, 128), F32).at[:, :GLA_RANK].set(w[:, 3592:3608]).astype(BF16)
    o_gla = _gla(x3, ng1, w_pairs, w_low, e_gla_gate_up[0].astype(BF16), row(e_gla_gate_b[0]),
                 row(e_gla_norm_g[0]))
    h2 = _mix_out_ffn(h_ml.reshape(n, -1), o_gla.reshape(n, -1), e_w_out[0].astype(BF16), h0, e_norm2_g[0],
                      e_ffn_w_gate[0].astype(BF16), e_ffn_w_up[0].astype(BF16), e_ffn_w_down[0].astype(BF16))

    w = o_w_in[0]
    h2_3 = h2.reshape(bsz, t, d)
    ng2 = row(o_norm1_g[0])
    y_ret = _retention(h2_3, ng2, w[:, :4 * RET_W].astype(BF16), *_retention_tables(t), row(o_ret_norm_g[0]))
    head_of = jnp.arange(RW_W) // RW_DIM
    bd = (head_of[:, None] == head_of[None, :]).astype(BF16)
    r, lw, k, v, aa, bb, g_out, bonus = _rwkv_prep(
        h2_3, ng2, w[:, 4 * RET_W:].astype(BF16), row(o_rw_mu[0]), o_rw_w_up[0].astype(BF16), row(o_rw_w0[0]),
        o_rw_a_up[0].astype(BF16), row(o_rw_a0[0]), o_rw_g_up[0].astype(BF16), row(o_rw_k_k[0]),
        row(o_rw_k_a[0]), row(o_rw_r_k[0]), bd)
    y_rw = _rwkv_scan(r, lw, k, v, aa, bb, g_out, bonus, row(o_rw_ln_g[0]), row(o_rw_ln_b[0]))
    router_pad = jnp.zeros((d, 128), F32).at[:, :N_EXPERTS].set(o_moe_router[0])
    router_hi = router_pad.astype(BF16)
    router_split = jnp.stack([router_hi, (router_pad - router_hi.astype(F32)).astype(BF16)])
    h3, xn, logits = _mix_out_route(y_ret.reshape(n, -1), y_rw.reshape(n, -1), o_w_out[0].astype(BF16), h2,
                                    o_norm2_g[0], router_split)
    sorted_tok, sorted_w, items, pos = _route(logits[:, :N_EXPERTS], n)
    xs = xn.at[sorted_tok].get(mode="promise_in_bounds")
    ys = _experts(xs, sorted_w.reshape(-1, 1), *items, o_moe_w_gate[0].astype(BF16),
                  o_moe_w_up[0].astype(BF16), o_moe_w_down[0].astype(BF16))
    y0 = ys.at[pos[:, 0]].get(mode="promise_in_bounds")
    y1 = ys.at[pos[:, 1]].get(mode="promise_in_bounds")
    out = _combine_norm(h3, y0, y1, final_norm_g)
    return out.reshape(bsz, t, d)
```

```python
import functools

import numpy as np
import jax
import jax.numpy as jnp
from jax import lax
from jax.experimental import pallas as pl
from jax.experimental.pallas import tpu as pltpu

F32 = jnp.float32
BF16 = jnp.bfloat16

D_MODEL = 1024
EPS = 1e-6
ML_HEADS, ML_DIM, ML_W, ML_CONV = 4, 128, 512, 4
GLA_HEADS, GLA_DK, GLA_DV, GLA_RANK, GLA_TAU = 4, 64, 128, 16, 16.0
RET_HEADS, RET_DIM, RET_W = 4, 128, 512
ROPE_BASE = 10000.0
RW_HEADS, RW_DIM, RW_W = 8, 64, 512
RW_COLS = 1792
RW_LN_EPS = 64e-5
D_FF = 3584
N_EXPERTS = 8
TOP_K = 2

VMEM_LIMIT = 48 * 1024 * 1024
NEG = -1e30


def _cp(*sem):
    return pltpu.CompilerParams(dimension_semantics=sem, vmem_limit_bytes=VMEM_LIMIT)


def _sigmoid(x):
    return 1.0 / (1.0 + jnp.exp(-x))


def _silu(x):
    return x * _sigmoid(x)


def _log_sigmoid(x):
    return jnp.minimum(x, 0.0) - jnp.log(1.0 + jnp.exp(-jnp.abs(x)))


def _dot(a, b):
    return jnp.dot(a.astype(BF16), b.astype(BF16), preferred_element_type=F32)


def _dot_nt(a, b):
    return lax.dot_general(a.astype(BF16), b.astype(BF16), (((1,), (1,)), ((), ())), preferred_element_type=F32)


def _dot_tn(a, b):
    return jnp.dot(a.T.astype(BF16), b.astype(BF16), preferred_element_type=F32)


def _rms_rows(x, g):
    ms = jnp.mean(x * x, axis=-1, keepdims=True)
    return x * lax.rsqrt(ms + EPS) * g


def _mix_out_ffn(a, b, w_out, resid, g, wg, wu, wd, tm=512, tf=1792):
    n, d = resid.shape
    wa = a.shape[1]
    wb = b.shape[1]
    f = wg.shape[1]
    nf = f // tf

    def body(a_ref, b_ref, wo_ref, r_ref, g_ref, wg_ref, wu_ref, wd_ref, o_ref, xn_ref, acc_ref):
        j = pl.program_id(1)

        @pl.when(j == 0)
        def _():
            h = r_ref[...] + jnp.dot(a_ref[...], wo_ref[0:wa, :], preferred_element_type=F32)
            h = h + jnp.dot(b_ref[...], wo_ref[wa:wa + wb, :], preferred_element_type=F32)
            xn_ref[...] = _rms_rows(h, g_ref[...]).astype(BF16)
            acc_ref[...] = h

        xn = xn_ref[...]
        gg = jnp.dot(xn, wg_ref[...], preferred_element_type=F32)
        uu = jnp.dot(xn, wu_ref[...], preferred_element_type=F32)
        act = (_silu(gg) * uu).astype(BF16)
        acc_ref[...] += jnp.dot(act, wd_ref[...], preferred_element_type=F32)

        @pl.when(j == nf - 1)
        def _():
            o_ref[...] = acc_ref[...]

    return pl.pallas_call(
        body,
        grid=(n // tm, nf),
        in_specs=[pl.BlockSpec((tm, wa), lambda i, j: (i, 0)),
                  pl.BlockSpec((tm, wb), lambda i, j: (i, 0)),
                  pl.BlockSpec((wa + wb, d), lambda i, j: (0, 0)),
                  pl.BlockSpec((tm, d), lambda i, j: (i, 0)),
                  pl.BlockSpec((1, d), lambda i, j: (0, 0)),
                  pl.BlockSpec((d, tf), lambda i, j: (0, j)),
                  pl.BlockSpec((d, tf), lambda i, j: (0, j)),
                  pl.BlockSpec((tf, d), lambda i, j: (j, 0))],
        out_specs=pl.BlockSpec((tm, d), lambda i, j: (i, 0)),
        out_shape=jax.ShapeDtypeStruct((n, d), F32),
        scratch_shapes=[pltpu.VMEM((tm, d), BF16), pltpu.VMEM((tm, d), F32)],
        compiler_params=_cp("parallel", "arbitrary"),
        name="mix_out_ffn",
    )(a, b, w_out, resid, g.reshape(1, d), wg, wu, wd)


ML_TB = 256
ML_L = 128


def _mlstm(x, in_g, w_in, w_gate, w_gate_t, gate_b_row, gate_b_col, conv_w, conv_b, norm_g):
    bsz, t, d = x.shape
    TB, L, D, H, W = ML_TB, ML_L, ML_DIM, ML_HEADS, ML_W

    def body(x_ref, ing_ref, w_ref, wg_ref, wgt_ref, gbr_ref, gbc_ref, cwq_ref, cwk_ref, cbq_ref, cbk_ref,
             ng_ref, out_ref, qext, kext, c_ref, n_ref, m_ref):
        c = pl.program_id(1)

        @pl.when(c == 0)
        def _():
            qext[0:8, :] = jnp.zeros((8, W), F32)
            kext[0:8, :] = jnp.zeros((8, W), F32)
            c_ref[...] = jnp.zeros_like(c_ref)
            n_ref[...] = jnp.zeros_like(n_ref)
            m_ref[...] = jnp.zeros_like(m_ref)

        xn = _rms_rows(x_ref[0], ing_ref[...]).astype(BF16)
        p = jnp.dot(xn, w_ref[...], preferred_element_type=F32)
        g_cols = jnp.dot(xn, wg_ref[...], preferred_element_type=F32)[:, 0:2 * H]
        g_rows = lax.dot_general(wgt_ref[...], xn, (((1,), (1,)), ((), ())),
                                 preferred_element_type=F32)[0:2 * H, :]
        qext[8:, :] = p[:, 0:W]
        kext[8:, :] = p[:, W:2 * W]

        def conv(ext, cw_ref, cb_ref):
            acc = cb_ref[...] + cw_ref[0:1, :] * ext[pl.ds(8 - ML_CONV + 1, TB), :]
            for kk in range(1, ML_CONV):
                acc = acc + cw_ref[kk:kk + 1, :] * ext[pl.ds(8 - ML_CONV + 1 + kk, TB), :]
            return _silu(acc)

        q_all = conv(qext, cwq_ref, cbq_ref) * (D ** -0.5)
        k_all = conv(kext, cwk_ref, cbk_ref)
        qext[0:8, :] = qext[TB:TB + 8, :]
        kext[0:8, :] = kext[TB:TB + 8, :]

        gcol = g_cols + gbr_ref[...]
        grow = g_rows + gbc_ref[...]
        fcol = _log_sigmoid(gcol[:, H:2 * H])
        frow = _log_sigmoid(grow[H:2 * H, :])
        ri = lax.broadcasted_iota(jnp.int32, (L, L), 0)
        ci = lax.broadcasted_iota(jnp.int32, (L, L), 1)
        causal = ri >= ci
        heads = range(H)
        c_st = [c_ref[h] for h in heads]
        n_st = [n_ref[h] for h in heads]
        m_st = [m_ref[h] for h in heads]
        for cc in range(TB // L):
            rows = slice(cc * L, (cc + 1) * L)
            hs = lambda z, h: z[rows, h * D:(h + 1) * D]
            w_intra, w_inter, w_state, carry, m_row = [], [], [], [], []
            for h in heads:
                f_row = frow[h:h + 1, rows]
                i_row = grow[h:h + 1, rows]
                f_col = fcol[rows, h:h + 1]
                i_col = gcol[rows, h:h + 1]
                b_col = jnp.sum(jnp.where(causal, f_row, 0.0), axis=1, keepdims=True)
                b_row = jnp.sum(jnp.where(ri <= ci, f_col, 0.0), axis=0, keepdims=True)
                g_tot = jnp.sum(f_row, axis=1, keepdims=True)
                d_intra = jnp.where(causal, b_col - b_row + i_row, NEG)
                d_inter = b_col + m_st[h]
                mr = jnp.maximum(d_inter, jnp.max(d_intra, axis=1, keepdims=True))
                w_intra.append(jnp.exp(d_intra - mr))
                w_inter.append(jnp.exp(d_inter - mr))
                m_row.append(mr)
                d_state = g_tot - b_col + i_col
                mn = jnp.maximum(g_tot + m_st[h], jnp.max(d_state, axis=0, keepdims=True))
                w_state.append(jnp.exp(d_state - mn))
                carry.append(jnp.exp(g_tot + m_st[h] - mn))
                m_st[h] = mn
            qh = [hs(q_all, h) for h in heads]
            kh = [hs(k_all, h) for h in heads]
            vh = [p[rows, 2 * W + h * D:2 * W + (h + 1) * D] for h in heads]
            s = [_dot_nt(qh[h], kh[h]) * w_intra[h] for h in heads]
            qc = [_dot(qh[h], c_st[h]) for h in heads]
            kw = [kh[h] * w_state[h] for h in heads]
            kv = [_dot_tn(kw[h], vh[h]) for h in heads]
            sv = [_dot(s[h], vh[h]) for h in heads]
            for h in heads:
                num = sv[h] + w_inter[h] * qc[h]
                den = (jnp.sum(s[h], axis=1, keepdims=True)
                       + w_inter[h] * jnp.sum(qh[h] * n_st[h], axis=1, keepdims=True))
                hval = num / jnp.maximum(jnp.abs(den), jnp.exp(-m_row[h]))
                c_st[h] = carry[h] * c_st[h] + kv[h]
                n_st[h] = carry[h] * n_st[h] + jnp.sum(kw[h], axis=0, keepdims=True)
                hg = _sigmoid(p[rows, 3 * W + h * D:3 * W + (h + 1) * D]) * hval
                hc = hg - jnp.mean(hg, axis=1, keepdims=True)
                hn = hc * lax.rsqrt(jnp.mean(hc * hc, axis=1, keepdims=True) + EPS)
                out_ref[0, rows, h * D:(h + 1) * D] = (hn * ng_ref[:, h * D:(h + 1) * D]).astype(BF16)
        for h in heads:
            c_ref[h] = c_st[h]
            n_ref[h] = n_st[h]
            m_ref[h] = m_st[h]

    fix = lambda j: (lambda b, c: (0, j))
    return pl.pallas_call(
        body,
        grid=(bsz, t // TB),
        in_specs=[pl.BlockSpec((1, TB, d), lambda b, c: (b, c, 0)),
                  pl.BlockSpec((1, d), fix(0)),
                  pl.BlockSpec((d, 4 * W), fix(0)),
                  pl.BlockSpec((d, 128), fix(0)),
                  pl.BlockSpec((16, d), fix(0)),
                  pl.BlockSpec((1, 2 * H), fix(0)),
                  pl.BlockSpec((2 * H, 1), fix(0)),
                  pl.BlockSpec((ML_CONV, W), fix(0)),
                  pl.BlockSpec((ML_CONV, W), fix(1)),
                  pl.BlockSpec((1, W), fix(0)),
                  pl.BlockSpec((1, W), fix(1)),
                  pl.BlockSpec((1, W), fix(0))],
        out_specs=pl.BlockSpec((1, TB, W), lambda b, c: (b, c, 0)),
        out_shape=jax.ShapeDtypeStruct((bsz, t, W), BF16),
        scratch_shapes=[pltpu.VMEM((TB + 8, W), F32), pltpu.VMEM((TB + 8, W), F32),
                        pltpu.VMEM((H, D, D), F32), pltpu.VMEM((H, 1, D), F32), pltpu.VMEM((H, 1, 1), F32)],
        compiler_params=_cp("parallel", "arbitrary"),
        name="mlstm",
    )(x, in_g, w_in, w_gate, w_gate_t, gate_b_row, gate_b_col, conv_w, conv_w, conv_b, conv_b, norm_g)


GLA_TC = 256
GLA_SUB = 16
GLA_GROUP = 128


def _gla(x, in_g, w_pairs, w_low, gate_up, gate_b, norm_g):
    bsz, t, d = x.shape
    tc, S, GB = GLA_TC, GLA_SUB, GLA_GROUP
    head_ones = (jnp.arange(2 * GLA_DK)[:, None] // GLA_DK == jnp.arange(2 * GB)[None, :] // GB).astype(BF16)
    nsub = tc // S
    dk, dv = GLA_DK, GLA_DV

    def body(x_ref, ing_ref, w_ref, wl_ref, gu_ref, gbias_ref, ng_ref, ones_ref, out_ref, st_ref, ksh, bsh):
        c = pl.program_id(2)

        @pl.when(c == 0)
        def _():
            st_ref[...] = jnp.zeros_like(st_ref)
            ksh[0:S, :] = jnp.zeros((S, 2 * dk), F32)
            bsh[0:S, :] = jnp.zeros((S, 2 * dk), F32)

        xn = _rms_rows(x_ref[0], ing_ref[...]).astype(BF16)
        p = jnp.dot(xn, w_ref[0], preferred_element_type=F32)
        g_low = jnp.dot(xn, wl_ref[...], preferred_element_type=F32)[:, 0:GLA_RANK]
        z = _dot(g_low, gu_ref[...]) + gbias_ref[...]
        la = _log_sigmoid(z) / GLA_TAU
        rowi = lax.broadcasted_iota(jnp.int32, (tc, 1), 0)
        rmod = rowi & (S - 1)
        bcum = la
        rsum = la
        for s in (1, 2, 4, 8):
            bcum = bcum + jnp.where(rmod >= s, pltpu.roll(bcum, s, 0), 0.0)
            rsum = rsum + jnp.where(rmod < S - s, pltpu.roll(rsum, tc - s, 0), 0.0)
        q = p[:, 0:2 * dk] * (dk ** -0.5)
        k = p[:, 2 * dk:4 * dk]
        v = p[:, 4 * dk:4 * dk + 2 * dv]
        gate = p[:, 4 * dk + 2 * dv:4 * dk + 4 * dv]
        qt = q * jnp.exp(bcum)
        kt = k * jnp.exp(rsum - la)
        eg = jnp.exp(bcum + rsum - la)

        ksh[S:, :] = k
        bsh[S:, :] = bcum
        prods = []
        for d in range(S):
            kd = k if d == 0 else ksh[pl.ds(S - d, tc), :]
            bd = bcum if d == 0 else bsh[pl.ds(S - d, tc), :]
            e = jnp.exp(jnp.where(rmod >= d, bcum - bd, 0.0))
            prods.append((q * kd * e).astype(BF16))
        ws = [jnp.dot(p, ones_ref[...], preferred_element_type=F32) for p in prods]
        coli = lax.broadcasted_iota(jnp.int32, (tc, GB), 1)
        rgrp = rowi & (GB - 1)
        att0 = jnp.zeros((tc, GB), F32)
        att1 = jnp.zeros((tc, GB), F32)
        offs = jnp.where((coli // S) == (rgrp // S), rgrp - coli, -1)
        for d in range(S):
            here = offs == d
            att0 = jnp.where(here, ws[d][:, 0:GB], att0)
            att1 = jnp.where(here, ws[d][:, GB:2 * GB], att1)

        heads = range(2)
        lk = [slice(hh * dk, (hh + 1) * dk) for hh in heads]
        lv = [slice(hh * dv, (hh + 1) * dv) for hh in heads]
        kv = [[_dot_tn(v[si * S:(si + 1) * S, lv[hh]], kt[si * S:(si + 1) * S, lk[hh]]) for hh in heads]
              for si in range(nsub)]
        st = [st_ref[hh] for hh in heads]
        inter = [[], []]
        for si in range(nsub):
            rows = slice(si * S, (si + 1) * S)
            for hh in heads:
                inter[hh].append(_dot_nt(qt[rows, lk[hh]], st[hh]))
                st[hh] = st[hh] * eg[si * S:si * S + 1, lk[hh]] + kv[si][hh]
        for hh in heads:
            st_ref[hh] = st[hh]

        for hh, att in ((0, att0), (1, att1)):
            diag = jnp.concatenate([_dot(att[g * GB:(g + 1) * GB], v[g * GB:(g + 1) * GB, lv[hh]])
                                    for g in range(tc // GB)], axis=0)
            o = diag + jnp.concatenate(inter[hh], axis=0)
            on = o * lax.rsqrt(jnp.mean(o * o, axis=1, keepdims=True) + EPS)
            out_ref[0, :, lv[hh]] = (on * ng_ref[:, lv[hh]] * _silu(gate[:, lv[hh]])).astype(BF16)

    pw = 4 * dk + 4 * dv
    return pl.pallas_call(
        body,
        grid=(bsz, GLA_HEADS // 2, t // tc),
        in_specs=[pl.BlockSpec((1, tc, d), lambda b, h, c: (b, c, 0)),
                  pl.BlockSpec((1, d), lambda b, h, c: (0, 0)),
                  pl.BlockSpec((1, d, pw), lambda b, h, c: (h, 0, 0)),
                  pl.BlockSpec((d, 128), lambda b, h, c: (0, 0)),
                  pl.BlockSpec((GLA_RANK, 2 * dk), lambda b, h, c: (0, h)),
                  pl.BlockSpec((1, 2 * dk), lambda b, h, c: (0, h)),
                  pl.BlockSpec((1, 2 * dv), lambda b, h, c: (0, h)),
                  pl.BlockSpec((2 * dk, 2 * GB), lambda b, h, c: (0, 0))],
        out_specs=pl.BlockSpec((1, tc, 2 * dv), lambda b, h, c: (b, c, h)),
        out_shape=jax.ShapeDtypeStruct((bsz, t, GLA_HEADS * dv), BF16),
        scratch_shapes=[pltpu.VMEM((2, dv, dk), F32), pltpu.VMEM((tc + S, 2 * dk), F32),
                        pltpu.VMEM((tc + S, 2 * dk), F32)],
        compiler_params=_cp("parallel", "parallel", "arbitrary"),
        name="gla",
    )(x, in_g, w_pairs, w_low, gate_up, gate_b, norm_g, head_ones)


RET_L = 256


def _retention(x, in_g, w_in, cos_t, sin_t, intra, inter, sdec, cdec, norm_g):
    bsz, t, d = x.shape
    L, D, H, W = RET_L, RET_DIM, RET_HEADS, RET_W

    def body(x_ref, ing_ref, w_ref, cos_ref, sin_ref, intra_ref, inter_ref, sdec_ref, cdec_ref, ng_ref,
             out_ref, s_ref):
        c = pl.program_id(1)

        @pl.when(c == 0)
        def _():
            s_ref[...] = jnp.zeros_like(s_ref)

        p = jnp.dot(_rms_rows(x_ref[0], ing_ref[...]).astype(BF16), w_ref[...], preferred_element_type=F32)
        cs = cos_ref[...]
        sn = sin_ref[...]

        def rot(z):
            return z * cs + pltpu.roll(z, D // 2, 1) * sn

        heads = range(H)
        q = [rot(p[:, h * D:(h + 1) * D]) * (D ** -0.5) for h in heads]
        k = [rot(p[:, W + h * D:W + (h + 1) * D]) for h in heads]
        v = [p[:, 2 * W + h * D:2 * W + (h + 1) * D] for h in heads]
        s_st = [s_ref[h] for h in heads]
        s = [_dot_nt(q[h], k[h]) * intra_ref[h] for h in heads]
        qs = [_dot(q[h], s_st[h]) for h in heads]
        kv = [_dot_tn(k[h] * sdec_ref[h], v[h]) for h in heads]
        sv = [_dot(s[h], v[h]) for h in heads]
        for h in heads:
            o = sv[h] + inter_ref[h] * qs[h]
            s_ref[h] = cdec_ref[h] * s_st[h] + kv[h]
            oc = o - jnp.mean(o, axis=1, keepdims=True)
            on = oc * lax.rsqrt(jnp.mean(oc * oc, axis=1, keepdims=True) + EPS)
            hl = slice(h * D, (h + 1) * D)
            gate = p[:, 3 * W + h * D:3 * W + (h + 1) * D]
            out_ref[0, :, hl] = (on * ng_ref[:, hl] * _silu(gate)).astype(BF16)

    fix2 = lambda b, c: (0, 0)
    fix3 = lambda b, c: (0, 0, 0)
    return pl.pallas_call(
        body,
        grid=(bsz, t // L),
        in_specs=[pl.BlockSpec((1, L, d), lambda b, c: (b, c, 0)),
                  pl.BlockSpec((1, d), fix2),
                  pl.BlockSpec((d, 4 * W), fix2),
                  pl.BlockSpec((L, D), lambda b, c: (c, 0)),
                  pl.BlockSpec((L, D), lambda b, c: (c, 0)),
                  pl.BlockSpec((H, L, L), fix3),
                  pl.BlockSpec((H, L, 1), fix3),
                  pl.BlockSpec((H, L, 1), fix3),
                  pl.BlockSpec((H, 1, 1), fix3),
                  pl.BlockSpec((1, W), lambda b, c: (0, 0))],
        out_specs=pl.BlockSpec((1, L, W), lambda b, c: (b, c, 0)),
        out_shape=jax.ShapeDtypeStruct((bsz, t, W), BF16),
        scratch_shapes=[pltpu.VMEM((H, D, D), F32)],
        compiler_params=_cp("parallel", "arbitrary"),
        name="retention",
    )(x, in_g, w_in, cos_t, sin_t, intra, inter, sdec, cdec, norm_g)


def _retention_tables(t):
    L, D = RET_L, RET_DIM
    inv = ROPE_BASE ** (-jnp.arange(0, D, 2, dtype=F32) / D)
    ang = jnp.arange(t).astype(F32)[:, None] * inv[None, :]
    cos = jnp.cos(ang)
    sin = jnp.sin(ang)
    cos_t = jnp.concatenate([cos, cos], axis=-1)
    sin_t = jnp.concatenate([-sin, sin], axis=-1)
    log_gamma = jnp.log1p(-jnp.exp2(-5.0 - jnp.arange(RET_HEADS, dtype=F32)))
    idx = jnp.arange(L, dtype=F32)
    causal = idx[:, None] >= idx[None, :]
    rel = jnp.where(causal, idx[:, None] - idx[None, :], 0.0)
    intra = jnp.where(causal, jnp.exp(log_gamma[:, None, None] * rel), 0.0)
    inter = jnp.exp(log_gamma[:, None] * (idx + 1.0))[:, :, None]
    sdec = jnp.exp(log_gamma[:, None] * (L - 1.0 - idx))[:, :, None]
    cdec = jnp.exp(log_gamma * L)[:, None, None]
    return cos_t, sin_t, intra, inter, sdec, cdec


RWP_TM = 512


def _seg_sum(x, bd):
    hi = x.astype(BF16)
    lo = (x - hi.astype(F32)).astype(BF16)
    return jnp.dot(hi, bd, preferred_element_type=F32) + jnp.dot(lo, bd, preferred_element_type=F32)


def _rwkv_prep(x, in_g, w_in, mu, w_up, w0, a_up, a0, g_up, k_k, k_a, r_k, bd):
    bsz, t, d = x.shape
    tm, W = RWP_TM, RW_W

    def body(x_ref, ing_ref, win_ref, mu_ref, wup_ref, w0_ref, aup_ref, a0_ref, gup_ref, kk_ref, ka_ref, rk_ref,
             bd_ref, r_out, lw_out, k_out, v_out, a_out, b_out, g_out, bonus_out, last_ref):
        c = pl.program_id(1)

        @pl.when(c == 0)
        def _():
            last_ref[...] = jnp.zeros_like(last_ref)

        cur = jnp.dot(_rms_rows(x_ref[0], ing_ref[...]).astype(BF16), win_ref[...], preferred_element_type=F32)
        prev = last_ref[...]
        last_ref[...] = cur[tm - 1:tm, :]
        rowi = lax.broadcasted_iota(jnp.int32, (tm, 1), 0)
        sh = jnp.where(rowi == 0, prev, pltpu.roll(cur, 1, 0))
        xm = cur + (sh - cur) * mu_ref[...]
        x_r = xm[:, 0:W]
        x_k = xm[:, W:2 * W]
        x_v = xm[:, 2 * W:3 * W]
        x_dl = xm[:, 3 * W:3 * W + 64]
        x_al = xm[:, 3 * W + 64:3 * W + 128]
        x_gl = xm[:, 3 * W + 128:3 * W + 256]
        wl = w0_ref[...] + _dot(jnp.tanh(x_dl), wup_ref[...])
        sp = jnp.maximum(-wl, 0.0) + jnp.log(1.0 + jnp.exp(-jnp.abs(wl)))
        lw_out[0] = -jnp.exp(-sp - 0.5)
        a = _sigmoid(a0_ref[...] + _dot(x_al, aup_ref[...]))
        g_out[0] = _dot(_sigmoid(x_gl), gup_ref[...]).astype(BF16)
        kk0 = x_k * kk_ref[...]
        nrm = jnp.sqrt(_seg_sum(kk0 * kk0, bd_ref[...]))
        kk = kk0 / jnp.maximum(nrm, 1e-12)
        k_h = x_k * (1.0 + (a - 1.0) * ka_ref[...])
        r_out[0] = x_r.astype(BF16)
        k_out[0] = k_h.astype(BF16)
        v_out[0] = x_v.astype(BF16)
        a_out[0] = (-kk).astype(BF16)
        b_out[0] = (kk * a).astype(BF16)
        bonus_out[0] = (_seg_sum(x_r * k_h * rk_ref[...], bd_ref[...]) * x_v).astype(BF16)

    row = lambda b, c: (0, 0)
    blk = pl.BlockSpec((1, tm, W), lambda b, c: (b, c, 0))
    shp = [jax.ShapeDtypeStruct((bsz, t, W), F32 if i == 1 else BF16) for i in range(8)]
    return pl.pallas_call(
        body,
        grid=(bsz, t // tm),
        in_specs=[pl.BlockSpec((1, tm, d), lambda b, c: (b, c, 0)),
                  pl.BlockSpec((1, d), row),
                  pl.BlockSpec((d, RW_COLS), row),
                  pl.BlockSpec((1, RW_COLS), row),
                  pl.BlockSpec((64, W), row), pl.BlockSpec((1, W), row),
                  pl.BlockSpec((64, W), row), pl.BlockSpec((1, W), row),
                  pl.BlockSpec((128, W), row),
                  pl.BlockSpec((1, W), row), pl.BlockSpec((1, W), row), pl.BlockSpec((1, W), row),
                  pl.BlockSpec((W, W), row)],
        out_specs=[blk] * 8,
        out_shape=shp,
        scratch_shapes=[pltpu.VMEM((1, RW_COLS), F32)],
        compiler_params=_cp("parallel", "arbitrary"),
        name="rwkv_prep",
    )(x, in_g, w_in, mu, w_up, w0, a_up, a0, g_up, k_k, k_a, r_k, bd)


RW_L = 64
RW_TB = 256


def _rwkv_scan(r, lw, k, v, aa, bb, g_out, bonus, ln_g, ln_b):
    bsz, t, W = r.shape
    L, N, tb = RW_L, RW_DIM, RW_TB
    nck = tb // L

    def body(r_ref, lw_ref, k_ref, v_ref, a_ref, b_ref, g_ref, bonus_ref, lng_ref, lnb_ref, out_ref, h_ref):
        c = pl.program_id(1)

        @pl.when(c == 0)
        def _():
            h_ref[...] = jnp.zeros_like(h_ref)

        lw_all = lw_ref[0]
        rowi = lax.broadcasted_iota(jnp.int32, (tb, 1), 0) & (L - 1)
        cl = lw_all
        for s in (1, 2, 4, 8, 16, 32):
            cl = cl + jnp.where(rowi >= s, pltpu.roll(cl, s, 0), 0.0)
        cl_last = jnp.concatenate([jnp.broadcast_to(cl[(cc + 1) * L - 1:(cc + 1) * L, :], (L, W))
                                   for cc in range(nck)], axis=0)
        e_inv = jnp.exp(-cl)
        e_end = jnp.exp(cl_last - cl)
        p_end = jnp.exp(cl_last)
        at = a_ref[0] * jnp.exp(cl - lw_all)
        rt = r_ref[0] * jnp.exp(cl)
        bt = b_ref[0] * e_inv
        kt = k_ref[0] * e_inv
        b_end = b_ref[0] * e_end
        k_end = k_ref[0] * e_end
        v_all = v_ref[0]
        P2 = 2 * N
        pairs = [(cc, p) for cc in range(nck) for p in range(RW_HEADS // 2)]
        pr = range(len(pairs))

        def pb(z, pi):
            cc, p = pairs[pi]
            return z[cc * L:(cc + 1) * L, p * P2:(p + 1) * P2]

        def stack2(z):
            lane = lax.broadcasted_iota(jnp.int32, (1, z.shape[1]), 1) & (P2 - 1)
            return jnp.concatenate([jnp.where(lane < N, z, 0.0), jnp.where(lane >= N, z, 0.0)], axis=0)

        tcol = lax.broadcasted_iota(jnp.int32, (L, P2), 1) & (N - 1)
        trow = lax.broadcasted_iota(jnp.int32, (L, P2), 0)
        strict = trow > tcol
        lower = trow >= tcol
        r2 = lax.broadcasted_iota(jnp.int32, (P2, P2), 0)
        c2 = lax.broadcasted_iota(jnp.int32, (P2, P2), 1)
        same_head = (r2 // N) == (c2 // N)
        eye2 = r2 == c2
        x2 = [jnp.concatenate([pb(at, pi), pb(rt, pi)], axis=0) for pi in pr]
        m_b = [_dot_nt(x2[pi], stack2(pb(bt, pi))) for pi in pr]
        m_k = [_dot_nt(x2[pi], stack2(pb(kt, pi))) for pi in pr]
        ap = [jnp.where(strict, m_b[pi][0:L], 0.0) for pi in pr]
        a_rb = [jnp.where(lower, m_b[pi][L:2 * L], 0.0) for pi in pr]
        a_ak = [jnp.where(strict, m_k[pi][0:L], 0.0) for pi in pr]
        a_rk = [jnp.where(lower, m_k[pi][L:2 * L], 0.0) for pi in pr]
        vp = [_dot(jnp.concatenate([a_ak[pi], a_rk[pi]], axis=0), stack2(pb(v_all, pi))) for pi in pr]
        kv = [_dot(pb(k_end, pi).T, pb(v_all, pi)) for pi in pr]
        x = [jnp.concatenate([pb(at, pi), vp[pi][0:L]], axis=1) for pi in pr]
        for it in range(6):
            x = [x[pi] + _dot(ap[pi], stack2(x[pi])) for pi in pr]
            if it < 5:
                ap = [_dot(ap[pi], stack2(ap[pi])) for pi in pr]
        post1 = [_dot(a_rb[pi], stack2(x[pi])) for pi in pr]
        post2 = [_dot(pb(b_end, pi).T, x[pi]) for pi in pr]
        lhs = []
        y0s = []
        h_adds = []
        for pi in pr:
            q_hat = pb(rt, pi) + post1[pi][:, 0:P2]
            gmat = (jnp.where(same_head, post2[pi][:, 0:P2], 0.0)
                    + jnp.where(eye2, pb(p_end, pi)[0:1, :], 0.0))
            lhs.append(jnp.concatenate([q_hat, gmat], axis=0))
            y0s.append(post1[pi][:, P2:2 * P2] + vp[pi][L:2 * L])
            h_adds.append(jnp.where(same_head, post2[pi][:, P2:2 * P2] + kv[pi], 0.0))
        npair = RW_HEADS // 2
        h_st = [h_ref[p] for p in range(npair)]
        ys = [None] * len(pairs)
        for cc in range(nck):
            res = [_dot(lhs[cc * npair + p], h_st[p]) for p in range(npair)]
            for p in range(npair):
                pi = cc * npair + p
                ys[pi] = res[p][0:L] + y0s[pi]
                h_st[p] = res[p][L:L + P2] + h_adds[pi]
        for p in range(npair):
            h_ref[p] = h_st[p]
        seg = same_head.astype(BF16)
        mean = [_seg_sum(ys[pi], seg) * (1.0 / N) for pi in pr]
        yc = [ys[pi] - mean[pi] for pi in pr]
        var = [_seg_sum(yc[pi] * yc[pi], seg) * (1.0 / N) for pi in pr]
        for pi, (cc, p) in enumerate(pairs):
            yn = yc[pi] * lax.rsqrt(var[pi] + RW_LN_EPS)
            rows = slice(cc * L, (cc + 1) * L)
            cols = slice(p * P2, (p + 1) * P2)
            out_ref[0, rows, cols] = ((yn * lng_ref[:, cols] + lnb_ref[:, cols] + bonus_ref[0, rows, cols])
                                      * g_ref[0, rows, cols]).astype(BF16)

    blk = pl.BlockSpec((1, tb, W), lambda b, c: (b, c, 0))
    vec = pl.BlockSpec((1, W), lambda b, c: (0, 0))
    return pl.pallas_call(
        body,
        grid=(bsz, t // tb),
        in_specs=[blk] * 8 + [vec, vec],
        out_specs=blk,
        out_shape=jax.ShapeDtypeStruct((bsz, t, W), BF16),
        scratch_shapes=[pltpu.VMEM((RW_HEADS // 2, 2 * N, 2 * N), F32)],
        compiler_params=_cp("parallel", "arbitrary"),
        name="rwkv_scan",
    )(r, lw, k, v, aa, bb, g_out, bonus, ln_g, ln_b)


def _mix_out_route(a, b, w_bf16, resid, g, router_split, tm=1024):
    n, wa = a.shape
    wb = b.shape[1]
    d = w_bf16.shape[1]
    e = router_split.shape[2]

    def body(a_ref, b_ref, w_ref, r_ref, g_ref, rt_ref, h_ref, xn_ref, lg_ref):
        acc = jnp.dot(a_ref[...], w_ref[0:wa, :], preferred_element_type=F32)
        acc = acc + jnp.dot(b_ref[...], w_ref[wa:wa + wb, :], preferred_element_type=F32)
        h = r_ref[...] + acc
        h_ref[...] = h
        xn = _rms_rows(h, g_ref[...])
        x_hi = xn.astype(BF16)
        xn_ref[...] = x_hi
        x_lo = (xn - x_hi.astype(F32)).astype(BF16)
        r_hi = rt_ref[0]
        r_lo = rt_ref[1]
        lg_ref[...] = (jnp.dot(x_hi, r_hi, preferred_element_type=F32)
                       + (jnp.dot(x_lo, r_hi, preferred_element_type=F32)
                          + jnp.dot(x_hi, r_lo, preferred_element_type=F32)))

    return pl.pallas_call(
        body,
        grid=(n // tm,),
        in_specs=[pl.BlockSpec((tm, wa), lambda i: (i, 0)),
                  pl.BlockSpec((tm, wb), lambda i: (i, 0)),
                  pl.BlockSpec((wa + wb, d), lambda i: (0, 0)),
                  pl.BlockSpec((tm, d), lambda i: (i, 0)),
                  pl.BlockSpec((1, d), lambda i: (0, 0)),
                  pl.BlockSpec((2, d, e), lambda i: (0, 0, 0))],
        out_specs=[pl.BlockSpec((tm, d), lambda i: (i, 0)), pl.BlockSpec((tm, d), lambda i: (i, 0)),
                   pl.BlockSpec((tm, e), lambda i: (i, 0))],
        out_shape=[jax.ShapeDtypeStruct((n, d), F32), jax.ShapeDtypeStruct((n, d), BF16),
                   jax.ShapeDtypeStruct((n, e), F32)],
        compiler_params=_cp("parallel"),
        name="mix_out_route",
    )(a, b, w_bf16, resid, g.reshape(1, d), router_split)


MOE_TM = 512
MOE_TF = 1792


def _experts(xs, row_w, item_tile, item_exp, item_lo, item_hi, wg, wu, wd):
    nrows, d = xs.shape
    tm, tf = MOE_TM, MOE_TF
    nf = D_FF // tf
    n_items = item_tile.shape[0]

    def body(it_ref, ie_ref, lo_ref, hi_ref, x_ref, w_ref, wg_ref, wu_ref, wd_ref, o_ref, acc_ref):
        i = pl.program_id(0)
        j = pl.program_id(1)
        tile = it_ref[i]
        first = jnp.logical_or(i == 0, tile != it_ref[jnp.maximum(i - 1, 0)])
        last = jnp.logical_or(i == n_items - 1, tile != it_ref[jnp.minimum(i + 1, n_items - 1)])

        @pl.when(jnp.logical_and(first, j == 0))
        def _():
            acc_ref[...] = jnp.zeros_like(acc_ref)

        lo = lo_ref[i]
        hi = hi_ref[i]

        @pl.when(lo < hi)
        def _():
            x = x_ref[...]
            gg = jnp.dot(x, wg_ref[0], preferred_element_type=F32)
            uu = jnp.dot(x, wu_ref[0], preferred_element_type=F32)
            act = (_silu(gg) * uu).astype(BF16)
            part = jnp.dot(act, wd_ref[0], preferred_element_type=F32)
            rowi = lax.broadcasted_iota(jnp.int32, (tm, 1), 0)
            mine = jnp.logical_and(rowi >= lo, rowi < hi)
            acc_ref[...] += part * jnp.where(mine, w_ref[...], 0.0)

        @pl.when(jnp.logical_and(last, j == nf - 1))
        def _():
            o_ref[...] = acc_ref[...].astype(o_ref.dtype)

    grid_spec = pltpu.PrefetchScalarGridSpec(
        num_scalar_prefetch=4,
        grid=(n_items, nf),
        in_specs=[pl.BlockSpec((tm, d), lambda i, j, it, ie, lo, hi: (it[i], 0)),
                  pl.BlockSpec((tm, 1), lambda i, j, it, ie, lo, hi: (it[i], 0)),
                  pl.BlockSpec((1, d, tf), lambda i, j, it, ie, lo, hi: (ie[i], 0, j)),
                  pl.BlockSpec((1, d, tf), lambda i, j, it, ie, lo, hi: (ie[i], 0, j)),
                  pl.BlockSpec((1, tf, d), lambda i, j, it, ie, lo, hi: (ie[i], j, 0))],
        out_specs=pl.BlockSpec((tm, d), lambda i, j, it, ie, lo, hi: (it[i], 0)),
        scratch_shapes=[pltpu.VMEM((tm, d), F32)],
    )
    return pl.pallas_call(
        body,
        grid_spec=grid_spec,
        out_shape=jax.ShapeDtypeStruct((nrows, d), BF16),
        compiler_params=_cp("arbitrary", "arbitrary"),
        name="moe_experts",
    )(item_tile, item_exp, item_lo, item_hi, xs, row_w, wg, wu, wd)


def _combine_norm(h, y0, y1, g, tm=512):
    n, d = h.shape

    def body(h_ref, a_ref, b_ref, g_ref, o_ref):
        o_ref[...] = _rms_rows(h_ref[...] + (a_ref[...].astype(F32) + b_ref[...].astype(F32)), g_ref[...])

    blk = pl.BlockSpec((tm, d), lambda i: (i, 0))
    return pl.pallas_call(
        body,
        grid=(n // tm,),
        in_specs=[blk, blk, blk, pl.BlockSpec((1, d), lambda i: (0, 0))],
        out_specs=blk,
        out_shape=jax.ShapeDtypeStruct((n, d), F32),
        compiler_params=_cp("parallel"),
        name="combine_norm",
    )(h, y0, y1, g.reshape(1, d))


def _route(logits, n):
    tm = MOE_TM
    na = n * TOP_K
    n_tiles = na // tm
    top_val, top_idx = lax.top_k(logits, TOP_K)
    top_w = jax.nn.softmax(top_val, axis=-1)
    e_flat = top_idx.reshape(-1).astype(jnp.int32)
    w_flat = top_w.reshape(-1)
    tok = jnp.arange(na, dtype=jnp.int32) // TOP_K
    _, sorted_tok, sorted_w = lax.sort((e_flat, tok, w_flat), num_keys=1, is_stable=True)
    onehot = (e_flat[:, None] == jnp.arange(N_EXPERTS, dtype=jnp.int32)[None, :]).astype(jnp.int32)
    rank = jnp.take_along_axis(jnp.cumsum(onehot, axis=0), e_flat[:, None], axis=1)[:, 0] - 1
    counts = jnp.sum(onehot, axis=0)
    ends = jnp.cumsum(counts)
    pos = (ends - counts)[e_flat] + rank
    cuts = jnp.sort(jnp.concatenate([jnp.arange(n_tiles, dtype=jnp.int32) * tm, ends[:-1].astype(jnp.int32)]))
    nxt = jnp.concatenate([cuts[1:], jnp.full((1,), na, jnp.int32)])
    item_tile = jnp.minimum(cuts // tm, n_tiles - 1)
    item_exp = jnp.minimum(jnp.searchsorted(ends, cuts, side="right"), N_EXPERTS - 1).astype(jnp.int32)
    item_lo = cuts - item_tile * tm
    item_hi = nxt - item_tile * tm
    return sorted_tok, sorted_w, (item_tile, item_exp, item_lo, item_hi), pos.reshape(n, TOP_K)


def kernel(x, e_norm1_g, e_w_in, e_ml_conv_w, e_ml_conv_b, e_ml_gate_b, e_ml_norm_g, e_gla_gate_up, e_gla_gate_b,
           e_gla_norm_g, e_w_out, e_norm2_g, e_ffn_w_gate, e_ffn_w_up, e_ffn_w_down, o_norm1_g, o_w_in,
           o_ret_norm_g, o_rw_mu, o_rw_w_up, o_rw_w0, o_rw_a_up, o_rw_a0, o_rw_g_up, o_rw_k_k, o_rw_k_a, o_rw_r_k,
           o_rw_ln_g, o_rw_ln_b, o_w_out, o_norm2_g, o_moe_router, o_moe_w_gate, o_moe_w_up, o_moe_w_down,
           final_norm_g):
    bsz, t, d = x.shape
    n = bsz * t
    h0 = x.reshape(n, d)

    w = e_w_in[0]
    row = lambda a: a.reshape(1, -1)
    x3 = x
    ng1 = row(e_norm1_g[0])
    w_if = w[:, 2048:2056]
    w_gate = jnp.zeros((d, 128), F32).at[:, :2 * ML_HEADS].set(w_if).astype(BF16)
    w_gate_t = jnp.zeros((16, d), F32).at[:2 * ML_HEADS, :].set(w_if.T).astype(BF16)
    h_ml = _mlstm(x3, ng1, w[:, :4 * ML_W].astype(BF16), w_gate, w_gate_t, row(e_ml_gate_b[0]),
                  e_ml_gate_b[0].reshape(-1, 1), e_ml_conv_w[0], row(e_ml_conv_b[0]), row(e_ml_norm_g[0]))
    gq, gk, gv, gr = 2056, 2312, 2568, 3080
    w_pairs = jnp.stack([jnp.concatenate([w[:, gq + 128 * hp:gq + 128 * (hp + 1)],
                                          w[:, gk + 128 * hp:gk + 128 * (hp + 1)],
                                          w[:, gv + 256 * hp:gv + 256 * (hp + 1)],
                                          w[:, gr + 256 * hp:gr + 256 * (hp + 1)]], axis=1)
                         for hp in range(GLA_HEADS // 2)]).astype(BF16)
    w_low = jnp.zeros((d, 128), F32).at[:, :GLA_RANK].set(w[:, 3592:3608]).astype(BF16)
    o_gla = _gla(x3, ng1, w_pairs, w_low, e_gla_gate_up[0].astype(BF16), row(e_gla_gate_b[0]),
                 row(e_gla_norm_g[0]))
    h2 = _mix_out_ffn(h_ml.reshape(n, -1), o_gla.reshape(n, -1), e_w_out[0].astype(BF16), h0, e_norm2_g[0],
                      e_ffn_w_gate[0].astype(BF16), e_ffn_w_up[0].astype(BF16), e_ffn_w_down[0].astype(BF16))

    w = o_w_in[0]
    h2_3 = h2.reshape(bsz, t, d)
    ng2 = row(o_norm1_g[0])
    y_ret = _retention(h2_3, ng2, w[:, :4 * RET_W].astype(BF16), *_retention_tables(t), row(o_ret_norm_g[0]))
    head_of = jnp.arange(RW_W) // RW_DIM
    bd = (head_of[:, None] == head_of[None, :]).astype(BF16)
    r, lw, k, v, aa, bb, g_out, bonus = _rwkv_prep(
        h2_3, ng2, w[:, 4 * RET_W:].astype(BF16), row(o_rw_mu[0]), o_rw_w_up[0].astype(BF16), row(o_rw_w0[0]),
        o_rw_a_up[0].astype(BF16), row(o_rw_a0[0]), o_rw_g_up[0].astype(BF16), row(o_rw_k_k[0]),
        row(o_rw_k_a[0]), row(o_rw_r_k[0]), bd)
    y_rw = _rwkv_scan(r, lw, k, v, aa, bb, g_out, bonus, row(o_rw_ln_g[0]), row(o_rw_ln_b[0]))
    router_pad = jnp.zeros((d, 128), F32).at[:, :N_EXPERTS].set(o_moe_router[0])
    router_hi = router_pad.astype(BF16)
    router_split = jnp.stack([router_hi, (router_pad - router_hi.astype(F32)).astype(BF16)])
    h3, xn, logits = _mix_out_route(y_ret.reshape(n, -1), y_rw.reshape(n, -1), o_w_out[0].astype(BF16), h2,
                                    o_norm2_g[0], router_split)
    sorted_tok, sorted_w, items, pos = _route(logits[:, :N_EXPERTS], n)
    xs = xn.at[sorted_tok].get(mode="promise_in_bounds")
    ys = _experts(xs, sorted_w.reshape(-1, 1), *items, o_moe_w_gate[0].astype(BF16),
                  o_moe_w_up[0].astype(BF16), o_moe_w_down[0].astype(BF16))
    y0 = ys.at[pos[:, 0]].get(mode="promise_in_bounds")
    y1 = ys.at[pos[:, 1]].get(mode="promise_in_bounds")
    out = _combine_norm(h3, y0, y1, final_norm_g)
    return out.reshape(bsz, t, d)
```

```python
import functools

import numpy as np
import jax
import jax.numpy as jnp
from jax import lax
from jax.experimental import pallas as pl
from jax.experimental.pallas import tpu as pltpu

F32 = jnp.float32
BF16 = jnp.bfloat16

D_MODEL = 1024
EPS = 1e-6
ML_HEADS, ML_DIM, ML_W, ML_CONV = 4, 128, 512, 4
GLA_HEADS, GLA_DK, GLA_DV, GLA_RANK, GLA_TAU = 4, 64, 128, 16, 16.0
RET_HEADS, RET_DIM, RET_W = 4, 128, 512
ROPE_BASE = 10000.0
RW_HEADS, RW_DIM, RW_W = 8, 64, 512
RW_COLS = 1792
RW_LN_EPS = 64e-5
D_FF = 3584
N_EXPERTS = 8
TOP_K = 2

VMEM_LIMIT = 48 * 1024 * 1024
NEG = -1e30


def _cp(*sem):
    return pltpu.CompilerParams(dimension_semantics=sem, vmem_limit_bytes=VMEM_LIMIT)


def _sigmoid(x):
    return 1.0 / (1.0 + jnp.exp(-x))


def _silu(x):
    return x * _sigmoid(x)


def _log_sigmoid(x):
    return jnp.minimum(x, 0.0) - jnp.log(1.0 + jnp.exp(-jnp.abs(x)))


def _dot(a, b):
    return jnp.dot(a.astype(BF16), b.astype(BF16), preferred_element_type=F32)


def _dot_nt(a, b):
    return lax.dot_general(a.astype(BF16), b.astype(BF16), (((1,), (1,)), ((), ())), preferred_element_type=F32)


def _dot_tn(a, b):
    return jnp.dot(a.T.astype(BF16), b.astype(BF16), preferred_element_type=F32)


def _rms_rows(x, g):
    ms = jnp.mean(x * x, axis=-1, keepdims=True)
    return x * lax.rsqrt(ms + EPS) * g


def _mix_out_ffn(a, b, w_out, resid, g, wg, wu, wd, tm=512, tf=1792):
    n, d = resid.shape
    wa = a.shape[1]
    wb = b.shape[1]
    f = wg.shape[1]
    nf = f // tf

    def body(a_ref, b_ref, wo_ref, r_ref, g_ref, wg_ref, wu_ref, wd_ref, o_ref, xn_ref, acc_ref):
        j = pl.program_id(1)

        @pl.when(j == 0)
        def _():
            h = r_ref[...] + jnp.dot(a_ref[...], wo_ref[0:wa, :], preferred_element_type=F32)
            h = h + jnp.dot(b_ref[...], wo_ref[wa:wa + wb, :], preferred_element_type=F32)
            xn_ref[...] = _rms_rows(h, g_ref[...]).astype(BF16)
            acc_ref[...] = h

        xn = xn_ref[...]
        gg = jnp.dot(xn, wg_ref[...], preferred_element_type=F32)
        uu = jnp.dot(xn, wu_ref[...], preferred_element_type=F32)
        act = (_silu(gg) * uu).astype(BF16)
        acc_ref[...] += jnp.dot(act, wd_ref[...], preferred_element_type=F32)

        @pl.when(j == nf - 1)
        def _():
            o_ref[...] = acc_ref[...]

    return pl.pallas_call(
        body,
        grid=(n // tm, nf),
        in_specs=[pl.BlockSpec((tm, wa), lambda i, j: (i, 0)),
                  pl.BlockSpec((tm, wb), lambda i, j: (i, 0)),
                  pl.BlockSpec((wa + wb, d), lambda i, j: (0, 0)),
                  pl.BlockSpec((tm, d), lambda i, j: (i, 0)),
                  pl.BlockSpec((1, d), lambda i, j: (0, 0)),
                  pl.BlockSpec((d, tf), lambda i, j: (0, j)),
                  pl.BlockSpec((d, tf), lambda i, j: (0, j)),
                  pl.BlockSpec((tf, d), lambda i, j: (j, 0))],
        out_specs=pl.BlockSpec((tm, d), lambda i, j: (i, 0)),
        out_shape=jax.ShapeDtypeStruct((n, d), F32),
        scratch_shapes=[pltpu.VMEM((tm, d), BF16), pltpu.VMEM((tm, d), F32)],
        compiler_params=_cp("parallel", "arbitrary"),
        name="mix_out_ffn",
    )(a, b, w_out, resid, g.reshape(1, d), wg, wu, wd)


ML_TB = 256
ML_L = 256


def _mlstm(x, in_g, w_in, w_gate, w_gate_t, gate_b_row, gate_b_col, conv_w, conv_b, norm_g):
    bsz, t, d = x.shape
    TB, L, D, H, W = ML_TB, ML_L, ML_DIM, ML_HEADS, ML_W

    def body(x_ref, ing_ref, w_ref, wg_ref, wgt_ref, gbr_ref, gbc_ref, cwq_ref, cwk_ref, cbq_ref, cbk_ref,
             ng_ref, out_ref, qext, kext, c_ref, n_ref, m_ref):
        c = pl.program_id(1)

        @pl.when(c == 0)
        def _():
            qext[0:8, :] = jnp.zeros((8, W), F32)
            kext[0:8, :] = jnp.zeros((8, W), F32)
            c_ref[...] = jnp.zeros_like(c_ref)
            n_ref[...] = jnp.zeros_like(n_ref)
            m_ref[...] = jnp.zeros_like(m_ref)

        xn = _rms_rows(x_ref[0], ing_ref[...]).astype(BF16)
        p = jnp.dot(xn, w_ref[...], preferred_element_type=F32)
        g_cols = jnp.dot(xn, wg_ref[...], preferred_element_type=F32)[:, 0:2 * H]
        g_rows = lax.dot_general(wgt_ref[...], xn, (((1,), (1,)), ((), ())),
                                 preferred_element_type=F32)[0:2 * H, :]
        qext[8:, :] = p[:, 0:W]
        kext[8:, :] = p[:, W:2 * W]

        def conv(ext, cw_ref, cb_ref):
            acc = cb_ref[...] + cw_ref[0:1, :] * ext[pl.ds(8 - ML_CONV + 1, TB), :]
            for kk in range(1, ML_CONV):
                acc = acc + cw_ref[kk:kk + 1, :] * ext[pl.ds(8 - ML_CONV + 1 + kk, TB), :]
            return _silu(acc)

        q_all = conv(qext, cwq_ref, cbq_ref) * (D ** -0.5)
        k_all = conv(kext, cwk_ref, cbk_ref)
        qext[0:8, :] = qext[TB:TB + 8, :]
        kext[0:8, :] = kext[TB:TB + 8, :]

        gcol = g_cols + gbr_ref[...]
        grow = g_rows + gbc_ref[...]
        fcol = _log_sigmoid(gcol[:, H:2 * H])
        frow = _log_sigmoid(grow[H:2 * H, :])
        ri = lax.broadcasted_iota(jnp.int32, (L, L), 0)
        ci = lax.broadcasted_iota(jnp.int32, (L, L), 1)
        causal = ri >= ci
        heads = range(H)
        c_st = [c_ref[h] for h in heads]
        n_st = [n_ref[h] for h in heads]
        m_st = [m_ref[h] for h in heads]
        for cc in range(TB // L):
            rows = slice(cc * L, (cc + 1) * L)
            hs = lambda z, h: z[rows, h * D:(h + 1) * D]
            w_intra, w_inter, w_state, carry, m_row = [], [], [], [], []
            for h in heads:
                f_row = frow[h:h + 1, rows]
                i_row = grow[h:h + 1, rows]
                f_col = fcol[rows, h:h + 1]
                i_col = gcol[rows, h:h + 1]
                b_col = jnp.sum(jnp.where(causal, f_row, 0.0), axis=1, keepdims=True)
                b_row = jnp.sum(jnp.where(ri <= ci, f_col, 0.0), axis=0, keepdims=True)
                g_tot = jnp.sum(f_row, axis=1, keepdims=True)
                d_intra = jnp.where(causal, b_col - b_row + i_row, NEG)
                d_inter = b_col + m_st[h]
                mr = jnp.maximum(d_inter, jnp.max(d_intra, axis=1, keepdims=True))
                w_intra.append(jnp.exp(d_intra - mr))
                w_inter.append(jnp.exp(d_inter - mr))
                m_row.append(mr)
                d_state = g_tot - b_col + i_col
                mn = jnp.maximum(g_tot + m_st[h], jnp.max(d_state, axis=0, keepdims=True))
                w_state.append(jnp.exp(d_state - mn))
                carry.append(jnp.exp(g_tot + m_st[h] - mn))
                m_st[h] = mn
            qh = [hs(q_all, h) for h in heads]
            kh = [hs(k_all, h) for h in heads]
            vh = [p[rows, 2 * W + h * D:2 * W + (h + 1) * D] for h in heads]
            s = [_dot_nt(qh[h], kh[h]) * w_intra[h] for h in heads]
            qc = [_dot(qh[h], c_st[h]) for h in heads]
            kw = [kh[h] * w_state[h] for h in heads]
            kv = [_dot_tn(kw[h], vh[h]) for h in heads]
            sv = [_dot(s[h], vh[h]) for h in heads]
            for h in heads:
                num = sv[h] + w_inter[h] * qc[h]
                den = (jnp.sum(s[h], axis=1, keepdims=True)
                       + w_inter[h] * jnp.sum(qh[h] * n_st[h], axis=1, keepdims=True))
                hval = num / jnp.maximum(jnp.abs(den), jnp.exp(-m_row[h]))
                c_st[h] = carry[h] * c_st[h] + kv[h]
                n_st[h] = carry[h] * n_st[h] + jnp.sum(kw[h], axis=0, keepdims=True)
                hg = _sigmoid(p[rows, 3 * W + h * D:3 * W + (h + 1) * D]) * hval
                hc = hg - jnp.mean(hg, axis=1, keepdims=True)
                hn = hc * lax.rsqrt(jnp.mean(hc * hc, axis=1, keepdims=True) + EPS)
                out_ref[0, rows, h * D:(h + 1) * D] = (hn * ng_ref[:, h * D:(h + 1) * D]).astype(BF16)
        for h in heads:
            c_ref[h] = c_st[h]
            n_ref[h] = n_st[h]
            m_ref[h] = m_st[h]

    fix = lambda j: (lambda b, c: (0, j))
    return pl.pallas_call(
        body,
        grid=(bsz, t // TB),
        in_specs=[pl.BlockSpec((1, TB, d), lambda b, c: (b, c, 0)),
                  pl.BlockSpec((1, d), fix(0)),
                  pl.BlockSpec((d, 4 * W), fix(0)),
                  pl.BlockSpec((d, 128), fix(0)),
                  pl.BlockSpec((16, d), fix(0)),
                  pl.BlockSpec((1, 2 * H), fix(0)),
                  pl.BlockSpec((2 * H, 1), fix(0)),
                  pl.BlockSpec((ML_CONV, W), fix(0)),
                  pl.BlockSpec((ML_CONV, W), fix(1)),
                  pl.BlockSpec((1, W), fix(0)),
                  pl.BlockSpec((1, W), fix(1)),
                  pl.BlockSpec((1, W), fix(0))],
        out_specs=pl.BlockSpec((1, TB, W), lambda b, c: (b, c, 0)),
        out_shape=jax.ShapeDtypeStruct((bsz, t, W), BF16),
        scratch_shapes=[pltpu.VMEM((TB + 8, W), F32), pltpu.VMEM((TB + 8, W), F32),
                        pltpu.VMEM((H, D, D), F32), pltpu.VMEM((H, 1, D), F32), pltpu.VMEM((H, 1, 1), F32)],
        compiler_params=_cp("parallel", "arbitrary"),
        name="mlstm",
    )(x, in_g, w_in, w_gate, w_gate_t, gate_b_row, gate_b_col, conv_w, conv_w, conv_b, conv_b, norm_g)


GLA_TC = 256
GLA_SUB = 16
GLA_GROUP = 128


def _gla(x, in_g, w_pairs, w_low, gate_up, gate_b, norm_g):
    bsz, t, d = x.shape
    tc, S, GB = GLA_TC, GLA_SUB, GLA_GROUP
    head_ones = (jnp.arange(2 * GLA_DK)[:, None] // GLA_DK == jnp.arange(2 * GB)[None, :] // GB).astype(BF16)
    nsub = tc // S
    dk, dv = GLA_DK, GLA_DV

    def body(x_ref, ing_ref, w_ref, wl_ref, gu_ref, gbias_ref, ng_ref, ones_ref, out_ref, st_ref, ksh, bsh):
        c = pl.program_id(2)

        @pl.when(c == 0)
        def _():
            st_ref[...] = jnp.zeros_like(st_ref)
            ksh[0:S, :] = jnp.zeros((S, 2 * dk), F32)
            bsh[0:S, :] = jnp.zeros((S, 2 * dk), F32)

        xn = _rms_rows(x_ref[0], ing_ref[...]).astype(BF16)
        p = jnp.dot(xn, w_ref[0], preferred_element_type=F32)
        g_low = jnp.dot(xn, wl_ref[...], preferred_element_type=F32)[:, 0:GLA_RANK]
        z = _dot(g_low, gu_ref[...]) + gbias_ref[...]
        la = _log_sigmoid(z) / GLA_TAU
        rowi = lax.broadcasted_iota(jnp.int32, (tc, 1), 0)
        rmod = rowi & (S - 1)
        bcum = la
        rsum = la
        for s in (1, 2, 4, 8):
            bcum = bcum + jnp.where(rmod >= s, pltpu.roll(bcum, s, 0), 0.0)
            rsum = rsum + jnp.where(rmod < S - s, pltpu.roll(rsum, tc - s, 0), 0.0)
        q = p[:, 0:2 * dk] * (dk ** -0.5)
        k = p[:, 2 * dk:4 * dk]
        v = p[:, 4 * dk:4 * dk + 2 * dv]
        gate = p[:, 4 * dk + 2 * dv:4 * dk + 4 * dv]
        qt = q * jnp.exp(bcum)
        kt = k * jnp.exp(rsum - la)
        eg = jnp.exp(bcum + rsum - la)

        ksh[S:, :] = k
        bsh[S:, :] = bcum
        prods = []
        for d in range(S):
            kd = k if d == 0 else ksh[pl.ds(S - d, tc), :]
            bd = bcum if d == 0 else bsh[pl.ds(S - d, tc), :]
            e = jnp.exp(jnp.where(rmod >= d, bcum - bd, 0.0))
            prods.append((q * kd * e).astype(BF16))
        ws = [jnp.dot(p, ones_ref[...], preferred_element_type=F32) for p in prods]
        coli = lax.broadcasted_iota(jnp.int32, (tc, GB), 1)
        rgrp = rowi & (GB - 1)
        att0 = jnp.zeros((tc, GB), F32)
        att1 = jnp.zeros((tc, GB), F32)
        offs = jnp.where((coli // S) == (rgrp // S), rgrp - coli, -1)
        for d in range(S):
            here = offs == d
            att0 = jnp.where(here, ws[d][:, 0:GB], att0)
            att1 = jnp.where(here, ws[d][:, GB:2 * GB], att1)

        heads = range(2)
        lk = [slice(hh * dk, (hh + 1) * dk) for hh in heads]
        lv = [slice(hh * dv, (hh + 1) * dv) for hh in heads]
        kv = [[_dot_tn(v[si * S:(si + 1) * S, lv[hh]], kt[si * S:(si + 1) * S, lk[hh]]) for hh in heads]
              for si in range(nsub)]
        st = [st_ref[hh] for hh in heads]
        inter = [[], []]
        for si in range(nsub):
            rows = slice(si * S, (si + 1) * S)
            for hh in heads:
                inter[hh].append(_dot_nt(qt[rows, lk[hh]], st[hh]))
                st[hh] = st[hh] * eg[si * S:si * S + 1, lk[hh]] + kv[si][hh]
        for hh in heads:
            st_ref[hh] = st[hh]

        for hh, att in ((0, att0), (1, att1)):
            diag = jnp.concatenate([_dot(att[g * GB:(g + 1) * GB], v[g * GB:(g + 1) * GB, lv[hh]])
                                    for g in range(tc // GB)], axis=0)
            o = diag + jnp.concatenate(inter[hh], axis=0)
            on = o * lax.rsqrt(jnp.mean(o * o, axis=1, keepdims=True) + EPS)
            out_ref[0, :, lv[hh]] = (on * ng_ref[:, lv[hh]] * _silu(gate[:, lv[hh]])).astype(BF16)

    pw = 4 * dk + 4 * dv
    return pl.pallas_call(
        body,
        grid=(bsz, GLA_HEADS // 2, t // tc),
        in_specs=[pl.BlockSpec((1, tc, d), lambda b, h, c: (b, c, 0)),
                  pl.BlockSpec((1, d), lambda b, h, c: (0, 0)),
                  pl.BlockSpec((1, d, pw), lambda b, h, c: (h, 0, 0)),
                  pl.BlockSpec((d, 128), lambda b, h, c: (0, 0)),
                  pl.BlockSpec((GLA_RANK, 2 * dk), lambda b, h, c: (0, h)),
                  pl.BlockSpec((1, 2 * dk), lambda b, h, c: (0, h)),
                  pl.BlockSpec((1, 2 * dv), lambda b, h, c: (0, h)),
                  pl.BlockSpec((2 * dk, 2 * GB), lambda b, h, c: (0, 0))],
        out_specs=pl.BlockSpec((1, tc, 2 * dv), lambda b, h, c: (b, c, h)),
        out_shape=jax.ShapeDtypeStruct((bsz, t, GLA_HEADS * dv), BF16),
        scratch_shapes=[pltpu.VMEM((2, dv, dk), F32), pltpu.VMEM((tc + S, 2 * dk), F32),
                        pltpu.VMEM((tc + S, 2 * dk), F32)],
        compiler_params=_cp("parallel", "parallel", "arbitrary"),
        name="gla",
    )(x, in_g, w_pairs, w_low, gate_up, gate_b, norm_g, head_ones)


RET_L = 256


def _retention(x, in_g, w_in, cos_t, sin_t, intra, inter, sdec, cdec, norm_g):
    bsz, t, d = x.shape
    L, D, H, W = RET_L, RET_DIM, RET_HEADS, RET_W

    def body(x_ref, ing_ref, w_ref, cos_ref, sin_ref, intra_ref, inter_ref, sdec_ref, cdec_ref, ng_ref,
             out_ref, s_ref):
        c = pl.program_id(1)

        @pl.when(c == 0)
        def _():
            s_ref[...] = jnp.zeros_like(s_ref)

        p = jnp.dot(_rms_rows(x_ref[0], ing_ref[...]).astype(BF16), w_ref[...], preferred_element_type=F32)
        cs = cos_ref[...]
        sn = sin_ref[...]

        def rot(z):
            return z * cs + pltpu.roll(z, D // 2, 1) * sn

        heads = range(H)
        q = [rot(p[:, h * D:(h + 1) * D]) * (D ** -0.5) for h in heads]
        k = [rot(p[:, W + h * D:W + (h + 1) * D]) for h in heads]
        v = [p[:, 2 * W + h * D:2 * W + (h + 1) * D] for h in heads]
        s_st = [s_ref[h] for h in heads]
        s = [_dot_nt(q[h], k[h]) * intra_ref[h] for h in heads]
        qs = [_dot(q[h], s_st[h]) for h in heads]
        kv = [_dot_tn(k[h] * sdec_ref[h], v[h]) for h in heads]
        sv = [_dot(s[h], v[h]) for h in heads]
        for h in heads:
            o = sv[h] + inter_ref[h] * qs[h]
            s_ref[h] = cdec_ref[h] * s_st[h] + kv[h]
            oc = o - jnp.mean(o, axis=1, keepdims=True)
            on = oc * lax.rsqrt(jnp.mean(oc * oc, axis=1, keepdims=True) + EPS)
            hl = slice(h * D, (h + 1) * D)
            gate = p[:, 3 * W + h * D:3 * W + (h + 1) * D]
            out_ref[0, :, hl] = (on * ng_ref[:, hl] * _silu(gate)).astype(BF16)

    fix2 = lambda b, c: (0, 0)
    fix3 = lambda b, c: (0, 0, 0)
    return pl.pallas_call(
        body,
        grid=(bsz, t // L),
        in_specs=[pl.BlockSpec((1, L, d), lambda b, c: (b, c, 0)),
                  pl.BlockSpec((1, d), fix2),
                  pl.BlockSpec((d, 4 * W), fix2),
                  pl.BlockSpec((L, D), lambda b, c: (c, 0)),
                  pl.BlockSpec((L, D), lambda b, c: (c, 0)),
                  pl.BlockSpec((H, L, L), fix3),
                  pl.BlockSpec((H, L, 1), fix3),
                  pl.BlockSpec((H, L, 1), fix3),
                  pl.BlockSpec((H, 1, 1), fix3),
                  pl.BlockSpec((1, W), lambda b, c: (0, 0))],
        out_specs=pl.BlockSpec((1, L, W), lambda b, c: (b, c, 0)),
        out_shape=jax.ShapeDtypeStruct((bsz, t, W), BF16),
        scratch_shapes=[pltpu.VMEM((H, D, D), F32)],
        compiler_params=_cp("parallel", "arbitrary"),
        name="retention",
    )(x, in_g, w_in, cos_t, sin_t, intra, inter, sdec, cdec, norm_g)


def _retention_tables(t):
    L, D = RET_L, RET_DIM
    inv = ROPE_BASE ** (-jnp.arange(0, D, 2, dtype=F32) / D)
    ang = jnp.arange(t).astype(F32)[:, None] * inv[None, :]
    cos = jnp.cos(ang)
    sin = jnp.sin(ang)
    cos_t = jnp.concatenate([cos, cos], axis=-1)
    sin_t = jnp.concatenate([-sin, sin], axis=-1)
    log_gamma = jnp.log1p(-jnp.exp2(-5.0 - jnp.arange(RET_HEADS, dtype=F32)))
    idx = jnp.arange(L, dtype=F32)
    causal = idx[:, None] >= idx[None, :]
    rel = jnp.where(causal, idx[:, None] - idx[None, :], 0.0)
    intra = jnp.where(causal, jnp.exp(log_gamma[:, None, None] * rel), 0.0)
    inter = jnp.exp(log_gamma[:, None] * (idx + 1.0))[:, :, None]
    sdec = jnp.exp(log_gamma[:, None] * (L - 1.0 - idx))[:, :, None]
    cdec = jnp.exp(log_gamma * L)[:, None, None]
    return cos_t, sin_t, intra, inter, sdec, cdec


RWP_TM = 512


def _seg_sum(x, bd):
    hi = x.astype(BF16)
    lo = (x - hi.astype(F32)).astype(BF16)
    return jnp.dot(hi, bd, preferred_element_type=F32) + jnp.dot(lo, bd, preferred_element_type=F32)


def _rwkv_prep(x, in_g, w_in, mu, w_up, w0, a_up, a0, g_up, k_k, k_a, r_k, bd):
    bsz, t, d = x.shape
    tm, W = RWP_TM, RW_W

    def body(x_ref, ing_ref, win_ref, mu_ref, wup_ref, w0_ref, aup_ref, a0_ref, gup_ref, kk_ref, ka_ref, rk_ref,
             bd_ref, r_out, lw_out, k_out, v_out, a_out, b_out, g_out, bonus_out, last_ref):
        c = pl.program_id(1)

        @pl.when(c == 0)
        def _():
            last_ref[...] = jnp.zeros_like(last_ref)

        cur = jnp.dot(_rms_rows(x_ref[0], ing_ref[...]).astype(BF16), win_ref[...], preferred_element_type=F32)
        prev = last_ref[...]
        last_ref[...] = cur[tm - 1:tm, :]
        rowi = lax.broadcasted_iota(jnp.int32, (tm, 1), 0)
        sh = jnp.where(rowi == 0, prev, pltpu.roll(cur, 1, 0))
        xm = cur + (sh - cur) * mu_ref[...]
        x_r = xm[:, 0:W]
        x_k = xm[:, W:2 * W]
        x_v = xm[:, 2 * W:3 * W]
        x_dl = xm[:, 3 * W:3 * W + 64]
        x_al = xm[:, 3 * W + 64:3 * W + 128]
        x_gl = xm[:, 3 * W + 128:3 * W + 256]
        wl = w0_ref[...] + _dot(jnp.tanh(x_dl), wup_ref[...])
        sp = jnp.maximum(-wl, 0.0) + jnp.log(1.0 + jnp.exp(-jnp.abs(wl)))
        lw_out[0] = -jnp.exp(-sp - 0.5)
        a = _sigmoid(a0_ref[...] + _dot(x_al, aup_ref[...]))
        g_out[0] = _dot(_sigmoid(x_gl), gup_ref[...]).astype(BF16)
        kk0 = x_k * kk_ref[...]
        nrm = jnp.sqrt(_seg_sum(kk0 * kk0, bd_ref[...]))
        kk = kk0 / jnp.maximum(nrm, 1e-12)
        k_h = x_k * (1.0 + (a - 1.0) * ka_ref[...])
        r_out[0] = x_r.astype(BF16)
        k_out[0] = k_h.astype(BF16)
        v_out[0] = x_v.astype(BF16)
        a_out[0] = (-kk).astype(BF16)
        b_out[0] = (kk * a).astype(BF16)
        bonus_out[0] = (_dot(x_r * k_h * rk_ref[...], bd_ref[...]) * x_v).astype(BF16)

    row = lambda b, c: (0, 0)
    blk = pl.BlockSpec((1, tm, W), lambda b, c: (b, c, 0))
    shp = [jax.ShapeDtypeStruct((bsz, t, W), F32 if i == 1 else BF16) for i in range(8)]
    return pl.pallas_call(
        body,
        grid=(bsz, t // tm),
        in_specs=[pl.BlockSpec((1, tm, d), lambda b, c: (b, c, 0)),
                  pl.BlockSpec((1, d), row),
                  pl.BlockSpec((d, RW_COLS), row),
                  pl.BlockSpec((1, RW_COLS), row),
                  pl.BlockSpec((64, W), row), pl.BlockSpec((1, W), row),
                  pl.BlockSpec((64, W), row), pl.BlockSpec((1, W), row),
                  pl.BlockSpec((128, W), row),
                  pl.BlockSpec((1, W), row), pl.BlockSpec((1, W), row), pl.BlockSpec((1, W), row),
                  pl.BlockSpec((W, W), row)],
        out_specs=[blk] * 8,
        out_shape=shp,
        scratch_shapes=[pltpu.VMEM((1, RW_COLS), F32)],
        compiler_params=_cp("parallel", "arbitrary"),
        name="rwkv_prep",
    )(x, in_g, w_in, mu, w_up, w0, a_up, a0, g_up, k_k, k_a, r_k, bd)


RW_L = 64
RW_TB = 512


def _rwkv_scan(r, lw, k, v, aa, bb, g_out, bonus, ln_g, ln_b):
    bsz, t, W = r.shape
    L, N, tb = RW_L, RW_DIM, RW_TB
    nck = tb // L

    def body(r_ref, lw_ref, k_ref, v_ref, a_ref, b_ref, g_ref, bonus_ref, lng_ref, lnb_ref, out_ref, h_ref):
        c = pl.program_id(1)

        @pl.when(c == 0)
        def _():
            h_ref[...] = jnp.zeros_like(h_ref)

        lw_all = lw_ref[0]
        rowi = lax.broadcasted_iota(jnp.int32, (tb, 1), 0) & (L - 1)
        cl = lw_all
        for s in (1, 2, 4, 8, 16, 32):
            cl = cl + jnp.where(rowi >= s, pltpu.roll(cl, s, 0), 0.0)
        cl_last = jnp.concatenate([jnp.broadcast_to(cl[(cc + 1) * L - 1:(cc + 1) * L, :], (L, W))
                                   for cc in range(nck)], axis=0)
        e_inv = jnp.exp(-cl)
        e_end = jnp.exp(cl_last - cl)
        p_end = jnp.exp(cl_last)
        at = a_ref[0] * jnp.exp(cl - lw_all)
        rt = r_ref[0] * jnp.exp(cl)
        bt = b_ref[0] * e_inv
        kt = k_ref[0] * e_inv
        b_end = b_ref[0] * e_end
        k_end = k_ref[0] * e_end
        v_all = v_ref[0]
        P2 = 2 * N
        pairs = [(cc, p) for cc in range(nck) for p in range(RW_HEADS // 2)]
        pr = range(len(pairs))

        def pb(z, pi):
            cc, p = pairs[pi]
            return z[cc * L:(cc + 1) * L, p * P2:(p + 1) * P2]

        def stack2(z):
            lane = lax.broadcasted_iota(jnp.int32, (1, z.shape[1]), 1) & (P2 - 1)
            return jnp.concatenate([jnp.where(lane < N, z, 0.0), jnp.where(lane >= N, z, 0.0)], axis=0)

        tcol = lax.broadcasted_iota(jnp.int32, (L, P2), 1) & (N - 1)
        trow = lax.broadcasted_iota(jnp.int32, (L, P2), 0)
        strict = trow > tcol
        lower = trow >= tcol
        r2 = lax.broadcasted_iota(jnp.int32, (P2, P2), 0)
        c2 = lax.broadcasted_iota(jnp.int32, (P2, P2), 1)
        same_head = (r2 // N) == (c2 // N)
        eye2 = r2 == c2
        x2 = [jnp.concatenate([pb(at, pi), pb(rt, pi)], axis=0) for pi in pr]
        m_b = [_dot_nt(x2[pi], stack2(pb(bt, pi))) for pi in pr]
        m_k = [_dot_nt(x2[pi], stack2(pb(kt, pi))) for pi in pr]
        ap = [jnp.where(strict, m_b[pi][0:L], 0.0) for pi in pr]
        a_rb = [jnp.where(lower, m_b[pi][L:2 * L], 0.0) for pi in pr]
        a_ak = [jnp.where(strict, m_k[pi][0:L], 0.0) for pi in pr]
        a_rk = [jnp.where(lower, m_k[pi][L:2 * L], 0.0) for pi in pr]
        vp = [_dot(jnp.concatenate([a_ak[pi], a_rk[pi]], axis=0), stack2(pb(v_all, pi))) for pi in pr]
        kv = [_dot(pb(k_end, pi).T, pb(v_all, pi)) for pi in pr]
        x = [jnp.concatenate([pb(at, pi), vp[pi][0:L]], axis=1) for pi in pr]
        for it in range(6):
            x = [x[pi] + _dot(ap[pi], stack2(x[pi])) for pi in pr]
            if it < 5:
                ap = [_dot(ap[pi], stack2(ap[pi])) for pi in pr]
        post1 = [_dot(a_rb[pi], stack2(x[pi])) for pi in pr]
        post2 = [_dot(pb(b_end, pi).T, x[pi]) for pi in pr]
        lhs = []
        y0s = []
        h_adds = []
        for pi in pr:
            q_hat = pb(rt, pi) + post1[pi][:, 0:P2]
            gmat = (jnp.where(same_head, post2[pi][:, 0:P2], 0.0)
                    + jnp.where(eye2, pb(p_end, pi)[0:1, :], 0.0))
            lhs.append(jnp.concatenate([q_hat, gmat], axis=0))
            y0s.append(post1[pi][:, P2:2 * P2] + vp[pi][L:2 * L])
            h_adds.append(jnp.where(same_head, post2[pi][:, P2:2 * P2] + kv[pi], 0.0))
        npair = RW_HEADS // 2
        h_st = [h_ref[p] for p in range(npair)]
        ys = [None] * len(pairs)
        for cc in range(nck):
            res = [_dot(lhs[cc * npair + p], h_st[p]) for p in range(npair)]
            for p in range(npair):
                pi = cc * npair + p
                ys[pi] = res[p][0:L] + y0s[pi]
                h_st[p] = res[p][L:L + P2] + h_adds[pi]
        for p in range(npair):
            h_ref[p] = h_st[p]
        seg = same_head.astype(BF16)
        mean = [_seg_sum(ys[pi], seg) * (1.0 / N) for pi in pr]
        yc = [ys[pi] - mean[pi] for pi in pr]
        var = [_seg_sum(yc[pi] * yc[pi], seg) * (1.0 / N) for pi in pr]
        for pi, (cc, p) in enumerate(pairs):
            yn = yc[pi] * lax.rsqrt(var[pi] + RW_LN_EPS)
            rows = slice(cc * L, (cc + 1) * L)
            cols = slice(p * P2, (p + 1) * P2)
            out_ref[0, rows, cols] = ((yn * lng_ref[:, cols] + lnb_ref[:, cols] + bonus_ref[0, rows, cols])
                                      * g_ref[0, rows, cols]).astype(BF16)

    blk = pl.BlockSpec((1, tb, W), lambda b, c: (b, c, 0))
    vec = pl.BlockSpec((1, W), lambda b, c: (0, 0))
    return pl.pallas_call(
        body,
        grid=(bsz, t // tb),
        in_specs=[blk] * 8 + [vec, vec],
        out_specs=blk,
        out_shape=jax.ShapeDtypeStruct((bsz, t, W), BF16),
        scratch_shapes=[pltpu.VMEM((RW_HEADS // 2, 2 * N, 2 * N), F32)],
        compiler_params=_cp("parallel", "arbitrary"),
        name="rwkv_scan",
    )(r, lw, k, v, aa, bb, g_out, bonus, ln_g, ln_b)


def _mix_out_route(a, b, w_bf16, resid, g, router_split, tm=1024):
    n, wa = a.shape
    wb = b.shape[1]
    d = w_bf16.shape[1]
    e = router_split.shape[2]

    def body(a_ref, b_ref, w_ref, r_ref, g_ref, rt_ref, h_ref, xn_ref, lg_ref):
        acc = jnp.dot(a_ref[...], w_ref[0:wa, :], preferred_element_type=F32)
        acc = acc + jnp.dot(b_ref[...], w_ref[wa:wa + wb, :], preferred_element_type=F32)
        h = r_ref[...] + acc
        h_ref[...] = h
        xn = _rms_rows(h, g_ref[...])
        x_hi = xn.astype(BF16)
        xn_ref[...] = x_hi
        x_lo = (xn - x_hi.astype(F32)).astype(BF16)
        r_hi = rt_ref[0]
        r_lo = rt_ref[1]
        lg_ref[...] = (jnp.dot(x_hi, r_hi, preferred_element_type=F32)
                       + (jnp.dot(x_lo, r_hi, preferred_element_type=F32)
                          + jnp.dot(x_hi, r_lo, preferred_element_type=F32)))

    return pl.pallas_call(
        body,
        grid=(n // tm,),
        in_specs=[pl.BlockSpec((tm, wa), lambda i: (i, 0)),
                  pl.BlockSpec((tm, wb), lambda i: (i, 0)),
                  pl.BlockSpec((wa + wb, d), lambda i: (0, 0)),
                  pl.BlockSpec((tm, d), lambda i: (i, 0)),
                  pl.BlockSpec((1, d), lambda i: (0, 0)),
                  pl.BlockSpec((2, d, e), lambda i: (0, 0, 0))],
        out_specs=[pl.BlockSpec((tm, d), lambda i: (i, 0)), pl.BlockSpec((tm, d), lambda i: (i, 0)),
                   pl.BlockSpec((tm, e), lambda i: (i, 0))],
        out_shape=[jax.ShapeDtypeStruct((n, d), F32), jax.ShapeDtypeStruct((n, d), BF16),
                   jax.ShapeDtypeStruct((n, e), F32)],
        compiler_params=_cp("parallel"),
        name="mix_out_route",
    )(a, b, w_bf16, resid, g.reshape(1, d), router_split)


MOE_TM = 512
MOE_TF = 1792


def _experts(xs, row_w, item_tile, item_exp, item_lo, item_hi, wg, wu, wd):
    nrows, d = xs.shape
    tm, tf = MOE_TM, MOE_TF
    nf = D_FF // tf
    n_items = item_tile.shape[0]

    def body(it_ref, ie_ref, lo_ref, hi_ref, x_ref, w_ref, wg_ref, wu_ref, wd_ref, o_ref, acc_ref):
        i = pl.program_id(0)
        j = pl.program_id(1)
        tile = it_ref[i]
        first = jnp.logical_or(i == 0, tile != it_ref[jnp.maximum(i - 1, 0)])
        last = jnp.logical_or(i == n_items - 1, tile != it_ref[jnp.minimum(i + 1, n_items - 1)])

        @pl.when(jnp.logical_and(first, j == 0))
        def _():
            acc_ref[...] = jnp.zeros_like(acc_ref)

        lo = lo_ref[i]
        hi = hi_ref[i]

        @pl.when(lo < hi)
        def _():
            x = x_ref[...]
            gg = jnp.dot(x, wg_ref[0], preferred_element_type=F32)
            uu = jnp.dot(x, wu_ref[0], preferred_element_type=F32)
            act = (_silu(gg) * uu).astype(BF16)
            part = jnp.dot(act, wd_ref[0], preferred_element_type=F32)
            rowi = lax.broadcasted_iota(jnp.int32, (tm, 1), 0)
            mine = jnp.logical_and(rowi >= lo, rowi < hi)
            acc_ref[...] += part * jnp.where(mine, w_ref[...], 0.0)

        @pl.when(jnp.logical_and(last, j == nf - 1))
        def _():
            o_ref[...] = acc_ref[...].astype(o_ref.dtype)

    grid_spec = pltpu.PrefetchScalarGridSpec(
        num_scalar_prefetch=4,
        grid=(n_items, nf),
        in_specs=[pl.BlockSpec((tm, d), lambda i, j, it, ie, lo, hi: (it[i], 0)),
                  pl.BlockSpec((tm, 1), lambda i, j, it, ie, lo, hi: (it[i], 0)),
                  pl.BlockSpec((1, d, tf), lambda i, j, it, ie, lo, hi: (ie[i], 0, j)),
                  pl.BlockSpec((1, d, tf), lambda i, j, it, ie, lo, hi: (ie[i], 0, j)),
                  pl.BlockSpec((1, tf, d), lambda i, j, it, ie, lo, hi: (ie[i], j, 0))],
        out_specs=pl.BlockSpec((tm, d), lambda i, j, it, ie, lo, hi: (it[i], 0)),
        scratch_shapes=[pltpu.VMEM((tm, d), F32)],
    )
    return pl.pallas_call(
        body,
        grid_spec=grid_spec,
        out_shape=jax.ShapeDtypeStruct((nrows, d), BF16),
        compiler_params=_cp("arbitrary", "arbitrary"),
        name="moe_experts",
    )(item_tile, item_exp, item_lo, item_hi, xs, row_w, wg, wu, wd)


def _combine_norm(h, y0, y1, g, tm=512):
    n, d = h.shape

    def body(h_ref, a_ref, b_ref, g_ref, o_ref):
        o_ref[...] = _rms_rows(h_ref[...] + (a_ref[...].astype(F32) + b_ref[...].astype(F32)), g_ref[...])

    blk = pl.BlockSpec((tm, d), lambda i: (i, 0))
    return pl.pallas_call(
        body,
        grid=(n // tm,),
        in_specs=[blk, blk, blk, pl.BlockSpec((1, d), lambda i: (0, 0))],
        out_specs=blk,
        out_shape=jax.ShapeDtypeStruct((n, d), F32),
        compiler_params=_cp("parallel"),
        name="combine_norm",
    )(h, y0, y1, g.reshape(1, d))


def _route(logits, n):
    tm = MOE_TM
    na = n * TOP_K
    n_tiles = na // tm
    top_val, top_idx = lax.top_k(logits, TOP_K)
    top_w = jax.nn.softmax(top_val, axis=-1)
    e_flat = top_idx.reshape(-1).astype(jnp.int32)
    w_flat = top_w.reshape(-1)
    tok = jnp.arange(na, dtype=jnp.int32) // TOP_K
    _, sorted_tok, sorted_w = lax.sort((e_flat, tok, w_flat), num_keys=1, is_stable=True)
    onehot = (e_flat[:, None] == jnp.arange(N_EXPERTS, dtype=jnp.int32)[None, :]).astype(jnp.int32)
    rank = jnp.take_along_axis(jnp.cumsum(onehot, axis=0), e_flat[:, None], axis=1)[:, 0] - 1
    counts = jnp.sum(onehot, axis=0)
    ends = jnp.cumsum(counts)
    pos = (ends - counts)[e_flat] + rank
    cuts = jnp.sort(jnp.concatenate([jnp.arange(n_tiles, dtype=jnp.int32) * tm, ends[:-1].astype(jnp.int32)]))
    nxt = jnp.concatenate([cuts[1:], jnp.full((1,), na, jnp.int32)])
    item_tile = jnp.minimum(cuts // tm, n_tiles - 1)
    item_exp = jnp.minimum(jnp.searchsorted(ends, cuts, side="right"), N_EXPERTS - 1).astype(jnp.int32)
    item_lo = cuts - item_tile * tm
    item_hi = nxt - item_tile * tm
    return sorted_tok, sorted_w, (item_tile, item_exp, item_lo, item_hi), pos.reshape(n, TOP_K)


def kernel(x, e_norm1_g, e_w_in, e_ml_conv_w, e_ml_conv_b, e_ml_gate_b, e_ml_norm_g, e_gla_gate_up, e_gla_gate_b,
           e_gla_norm_g, e_w_out, e_norm2_g, e_ffn_w_gate, e_ffn_w_up, e_ffn_w_down, o_norm1_g, o_w_in,
           o_ret_norm_g, o_rw_mu, o_rw_w_up, o_rw_w0, o_rw_a_up, o_rw_a0, o_rw_g_up, o_rw_k_k, o_rw_k_a, o_rw_r_k,
           o_rw_ln_g, o_rw_ln_b, o_w_out, o_norm2_g, o_moe_router, o_moe_w_gate, o_moe_w_up, o_moe_w_down,
           final_norm_g):
    bsz, t, d = x.shape
    n = bsz * t
    h0 = x.reshape(n, d)

    w = e_w_in[0]
    row = lambda a: a.reshape(1, -1)
    x3 = x
    ng1 = row(e_norm1_g[0])
    w_if = w[:, 2048:2056]
    w_gate = jnp.zeros((d, 128), F32).at[:, :2 * ML_HEADS].set(w_if).astype(BF16)
    w_gate_t = jnp.zeros((16, d), F32).at[:2 * ML_HEADS, :].set(w_if.T).astype(BF16)
    h_ml = _mlstm(x3, ng1, w[:, :4 * ML_W].astype(BF16), w_gate, w_gate_t, row(e_ml_gate_b[0]),
                  e_ml_gate_b[0].reshape(-1, 1), e_ml_conv_w[0], row(e_ml_conv_b[0]), row(e_ml_norm_g[0]))
    gq, gk, gv, gr = 2056, 2312, 2568, 3080
    w_pairs = jnp.stack([jnp.concatenate([w[:, gq + 128 * hp:gq + 128 * (hp + 1)],
                                          w[:, gk + 128 * hp:gk + 128 * (hp + 1)],
                                          w[:, gv + 256 * hp:gv + 256 * (hp + 1)],
                                          w[:, gr + 256 * hp:gr + 256 * (hp + 1)]], axis=1)
                         for hp in range(GLA_HEADS // 2)]).astype(BF16)
    w_low = jnp.zeros((d, 128), F32).at[:, :GLA_RANK].set(w[:, 3592:3608]).astype(BF16)
    o_gla = _gla(x3, ng1, w_pairs, w_low, e_gla_gate_up[0].astype(BF16), row(e_gla_gate_b[0]),
                 row(e_gla_norm_g[0]))
    h2 = _mix_out_ffn(h_ml.reshape(n, -1), o_gla.reshape(n, -1), e_w_out[0].astype(BF16), h0, e_norm2_g[0],
                      e_ffn_w_gate[0].astype(BF16), e_ffn_w_up[0].astype(BF16), e_ffn_w_down[0].astype(BF16))

    w = o_w_in[0]
    h2_3 = h2.reshape(bsz, t, d)
    ng2 = row(o_norm1_g[0])
    y_ret = _retention(h2_3, ng2, w[:, :4 * RET_W].astype(BF16), *_retention_tables(t), row(o_ret_norm_g[0]))
    head_of = jnp.arange(RW_W) // RW_DIM
    bd = (head_of[:, None] == head_of[None, :]).astype(BF16)
    r, lw, k, v, aa, bb, g_out, bonus = _rwkv_prep(
        h2_3, ng2, w[:, 4 * RET_W:].astype(BF16), row(o_rw_mu[0]), o_rw_w_up[0].astype(BF16), row(o_rw_w0[0]),
        o_rw_a_up[0].astype(BF16), row(o_rw_a0[0]), o_rw_g_up[0].astype(BF16), row(o_rw_k_k[0]),
        row(o_rw_k_a[0]), row(o_rw_r_k[0]), bd)
    y_rw = _rwkv_scan(r, lw, k, v, aa, bb, g_out, bonus, row(o_rw_ln_g[0]), row(o_rw_ln_b[0]))
    router_pad = jnp.zeros((d, 128), F32).at[:, :N_EXPERTS].set(o_moe_router[0])
    router_hi = router_pad.astype(BF16)
    router_split = jnp.stack([router_hi, (router_pad - router_hi.astype(F32)).astype(BF16)])
    h3, xn, logits = _mix_out_route(y_ret.reshape(n, -1), y_rw.reshape(n, -1), o_w_out[0].astype(BF16), h2,
                                    o_norm2_g[0], router_split)
    sorted_tok, sorted_w, items, pos = _route(logits[:, :N_EXPERTS], n)
    xs = xn.at[sorted_tok].get(mode="promise_in_bounds")
    ys = _experts(xs, sorted_w.reshape(-1, 1), *items, o_moe_w_gate[0].astype(BF16),
                  o_moe_w_up[0].astype(BF16), o_moe_w_down[0].astype(BF16))
    y0 = ys.at[pos[:, 0]].get(mode="promise_in_bounds")
    y1 = ys.at[pos[:, 1]].get(mode="promise_in_bounds")
    out = _combine_norm(h3, y0, y1, final_norm_g)
    return out.reshape(bsz, t, d)
```

```python
import functools

import numpy as np
import jax
import jax.numpy as jnp
from jax import lax
from jax.experimental import pallas as pl
from jax.experimental.pallas import tpu as pltpu

F32 = jnp.float32
BF16 = jnp.bfloat16

D_MODEL = 1024
EPS = 1e-6
ML_HEADS, ML_DIM, ML_W, ML_CONV = 4, 128, 512, 4
GLA_HEADS, GLA_DK, GLA_DV, GLA_RANK, GLA_TAU = 4, 64, 128, 16, 16.0
RET_HEADS, RET_DIM, RET_W = 4, 128, 512
ROPE_BASE = 10000.0
RW_HEADS, RW_DIM, RW_W = 8, 64, 512
RW_COLS = 1792
RW_LN_EPS = 64e-5
D_FF = 3584
N_EXPERTS = 8
TOP_K = 2

VMEM_LIMIT = 48 * 1024 * 1024
NEG = -1e30


def _cp(*sem):
    return pltpu.CompilerParams(dimension_semantics=sem, vmem_limit_bytes=VMEM_LIMIT)


def _sigmoid(x):
    return 1.0 / (1.0 + jnp.exp(-x))


def _silu(x):
    return x * _sigmoid(x)


def _log_sigmoid(x):
    return jnp.minimum(x, 0.0) - jnp.log(1.0 + jnp.exp(-jnp.abs(x)))


def _dot(a, b):
    return jnp.dot(a.astype(BF16), b.astype(BF16), preferred_element_type=F32)


def _dot_nt(a, b):
    return lax.dot_general(a.astype(BF16), b.astype(BF16), (((1,), (1,)), ((), ())), preferred_element_type=F32)


def _dot_tn(a, b):
    return jnp.dot(a.T.astype(BF16), b.astype(BF16), preferred_element_type=F32)


def _rms_rows(x, g):
    ms = jnp.mean(x * x, axis=-1, keepdims=True)
    return x * lax.rsqrt(ms + EPS) * g


def _mix_out_ffn(a, b, w_out, resid, g, wg, wu, wd, tm=512, tf=1792):
    n, d = resid.shape
    wa = a.shape[1]
    wb = b.shape[1]
    f = wg.shape[1]
    nf = f // tf

    def body(a_ref, b_ref, wo_ref, r_ref, g_ref, wg_ref, wu_ref, wd_ref, o_ref, xn_ref, acc_ref):
        j = pl.program_id(1)

        @pl.when(j == 0)
        def _():
            h = r_ref[...] + jnp.dot(a_ref[...], wo_ref[0:wa, :], preferred_element_type=F32)
            h = h + jnp.dot(b_ref[...], wo_ref[wa:wa + wb, :], preferred_element_type=F32)
            xn_ref[...] = _rms_rows(h, g_ref[...]).astype(BF16)
            acc_ref[...] = h

        xn = xn_ref[...]
        gg = jnp.dot(xn, wg_ref[...], preferred_element_type=F32)
        uu = jnp.dot(xn, wu_ref[...], preferred_element_type=F32)
        act = (_silu(gg) * uu).astype(BF16)
        acc_ref[...] += jnp.dot(act, wd_ref[...], preferred_element_type=F32)

        @pl.when(j == nf - 1)
        def _():
            o_ref[...] = acc_ref[...]

    return pl.pallas_call(
        body,
        grid=(n // tm, nf),
        in_specs=[pl.BlockSpec((tm, wa), lambda i, j: (i, 0)),
                  pl.BlockSpec((tm, wb), lambda i, j: (i, 0)),
                  pl.BlockSpec((wa + wb, d), lambda i, j: (0, 0)),
                  pl.BlockSpec((tm, d), lambda i, j: (i, 0)),
                  pl.BlockSpec((1, d), lambda i, j: (0, 0)),
                  pl.BlockSpec((d, tf), lambda i, j: (0, j)),
                  pl.BlockSpec((d, tf), lambda i, j: (0, j)),
                  pl.BlockSpec((tf, d), lambda i, j: (j, 0))],
        out_specs=pl.BlockSpec((tm, d), lambda i, j: (i, 0)),
        out_shape=jax.ShapeDtypeStruct((n, d), F32),
        scratch_shapes=[pltpu.VMEM((tm, d), BF16), pltpu.VMEM((tm, d), F32)],
        compiler_params=_cp("parallel", "arbitrary"),
        name="mix_out_ffn",
    )(a, b, w_out, resid, g.reshape(1, d), wg, wu, wd)


ML_TB = 512
ML_L = 256


def _mlstm(x, in_g, w_in, w_gate, w_gate_t, gate_b_row, gate_b_col, conv_w, conv_b, norm_g):
    bsz, t, d = x.shape
    TB, L, D, H, W = ML_TB, ML_L, ML_DIM, ML_HEADS, ML_W

    def body(x_ref, ing_ref, w_ref, wg_ref, wgt_ref, gbr_ref, gbc_ref, cwq_ref, cwk_ref, cbq_ref, cbk_ref,
             ng_ref, out_ref, qext, kext, c_ref, n_ref, m_ref):
        c = pl.program_id(1)

        @pl.when(c == 0)
        def _():
            qext[0:8, :] = jnp.zeros((8, W), F32)
            kext[0:8, :] = jnp.zeros((8, W), F32)
            c_ref[...] = jnp.zeros_like(c_ref)
            n_ref[...] = jnp.zeros_like(n_ref)
            m_ref[...] = jnp.zeros_like(m_ref)

        xn = _rms_rows(x_ref[0], ing_ref[...]).astype(BF16)
        p = jnp.dot(xn, w_ref[...], preferred_element_type=F32)
        g_cols = jnp.dot(xn, wg_ref[...], preferred_element_type=F32)[:, 0:2 * H]
        g_rows = lax.dot_general(wgt_ref[...], xn, (((1,), (1,)), ((), ())),
                                 preferred_element_type=F32)[0:2 * H, :]
        qext[8:, :] = p[:, 0:W]
        kext[8:, :] = p[:, W:2 * W]

        def conv(ext, cw_ref, cb_ref):
            acc = cb_ref[...] + cw_ref[0:1, :] * ext[pl.ds(8 - ML_CONV + 1, TB), :]
            for kk in range(1, ML_CONV):
                acc = acc + cw_ref[kk:kk + 1, :] * ext[pl.ds(8 - ML_CONV + 1 + kk, TB), :]
            return _silu(acc)

        q_all = conv(qext, cwq_ref, cbq_ref) * (D ** -0.5)
        k_all = conv(kext, cwk_ref, cbk_ref)
        qext[0:8, :] = qext[TB:TB + 8, :]
        kext[0:8, :] = kext[TB:TB + 8, :]

        gcol = g_cols + gbr_ref[...]
        grow = g_rows + gbc_ref[...]
        fcol = _log_sigmoid(gcol[:, H:2 * H])
        frow = _log_sigmoid(grow[H:2 * H, :])
        ri = lax.broadcasted_iota(jnp.int32, (L, L), 0)
        ci = lax.broadcasted_iota(jnp.int32, (L, L), 1)
        causal = ri >= ci
        heads = range(H)
        c_st = [c_ref[h] for h in heads]
        n_st = [n_ref[h] for h in heads]
        m_st = [m_ref[h] for h in heads]
        for cc in range(TB // L):
            rows = slice(cc * L, (cc + 1) * L)
            hs = lambda z, h: z[rows, h * D:(h + 1) * D]
            w_intra, w_inter, w_state, carry, m_row = [], [], [], [], []
            for h in heads:
                f_row = frow[h:h + 1, rows]
                i_row = grow[h:h + 1, rows]
                f_col = fcol[rows, h:h + 1]
                i_col = gcol[rows, h:h + 1]
                b_col = jnp.sum(jnp.where(causal, f_row, 0.0), axis=1, keepdims=True)
                b_row = jnp.sum(jnp.where(ri <= ci, f_col, 0.0), axis=0, keepdims=True)
                g_tot = jnp.sum(f_row, axis=1, keepdims=True)
                d_intra = jnp.where(causal, b_col - b_row + i_row, NEG)
                d_inter = b_col + m_st[h]
                mr = jnp.maximum(d_inter, jnp.max(d_intra, axis=1, keepdims=True))
                w_intra.append(jnp.exp(d_intra - mr))
                w_inter.append(jnp.exp(d_inter - mr))
                m_row.append(mr)
                d_state = g_tot - b_col + i_col
                mn = jnp.maximum(g_tot + m_st[h], jnp.max(d_state, axis=0, keepdims=True))
                w_state.append(jnp.exp(d_state - mn))
                carry.append(jnp.exp(g_tot + m_st[h] - mn))
                m_st[h] = mn
            qh = [hs(q_all, h) for h in heads]
            kh = [hs(k_all, h) for h in heads]
            vh = [p[rows, 2 * W + h * D:2 * W + (h + 1) * D] for h in heads]
            s = [_dot_nt(qh[h], kh[h]) * w_intra[h] for h in heads]
            qc = [_dot(qh[h], c_st[h]) for h in heads]
            kw = [kh[h] * w_state[h] for h in heads]
            kv = [_dot_tn(kw[h], vh[h]) for h in heads]
            sv = [_dot(s[h], vh[h]) for h in heads]
            for h in heads:
                num = sv[h] + w_inter[h] * qc[h]
                den = (jnp.sum(s[h], axis=1, keepdims=True)
                       + w_inter[h] * jnp.sum(qh[h] * n_st[h], axis=1, keepdims=True))
                hval = num / jnp.maximum(jnp.abs(den), jnp.exp(-m_row[h]))
                c_st[h] = carry[h] * c_st[h] + kv[h]
                n_st[h] = carry[h] * n_st[h] + jnp.sum(kw[h], axis=0, keepdims=True)
                hg = _sigmoid(p[rows, 3 * W + h * D:3 * W + (h + 1) * D]) * hval
                hc = hg - jnp.mean(hg, axis=1, keepdims=True)
                hn = hc * lax.rsqrt(jnp.mean(hc * hc, axis=1, keepdims=True) + EPS)
                out_ref[0, rows, h * D:(h + 1) * D] = (hn * ng_ref[:, h * D:(h + 1) * D]).astype(BF16)
        for h in heads:
            c_ref[h] = c_st[h]
            n_ref[h] = n_st[h]
            m_ref[h] = m_st[h]

    fix = lambda j: (lambda b, c: (0, j))
    return pl.pallas_call(
        body,
        grid=(bsz, t // TB),
        in_specs=[pl.BlockSpec((1, TB, d), lambda b, c: (b, c, 0)),
                  pl.BlockSpec((1, d), fix(0)),
                  pl.BlockSpec((d, 4 * W), fix(0)),
                  pl.BlockSpec((d, 128), fix(0)),
                  pl.BlockSpec((16, d), fix(0)),
                  pl.BlockSpec((1, 2 * H), fix(0)),
                  pl.BlockSpec((2 * H, 1), fix(0)),
                  pl.BlockSpec((ML_CONV, W), fix(0)),
                  pl.BlockSpec((ML_CONV, W), fix(1)),
                  pl.BlockSpec((1, W), fix(0)),
                  pl.BlockSpec((1, W), fix(1)),
                  pl.BlockSpec((1, W), fix(0))],
        out_specs=pl.BlockSpec((1, TB, W), lambda b, c: (b, c, 0)),
        out_shape=jax.ShapeDtypeStruct((bsz, t, W), BF16),
        scratch_shapes=[pltpu.VMEM((TB + 8, W), F32), pltpu.VMEM((TB + 8, W), F32),
                        pltpu.VMEM((H, D, D), F32), pltpu.VMEM((H, 1, D), F32), pltpu.VMEM((H, 1, 1), F32)],
        compiler_params=_cp("parallel", "arbitrary"),
        name="mlstm",
    )(x, in_g, w_in, w_gate, w_gate_t, gate_b_row, gate_b_col, conv_w, conv_w, conv_b, conv_b, norm_g)


GLA_TC = 512
GLA_SUB = 16
GLA_GROUP = 128


def _gla(x, in_g, w_pairs, w_low, gate_up, gate_b, norm_g):
    bsz, t, d = x.shape
    tc, S, GB = GLA_TC, GLA_SUB, GLA_GROUP
    head_ones = (jnp.arange(2 * GLA_DK)[:, None] // GLA_DK == jnp.arange(2 * GB)[None, :] // GB).astype(BF16)
    nsub = tc // S
    dk, dv = GLA_DK, GLA_DV

    def body(x_ref, ing_ref, w_ref, wl_ref, gu_ref, gbias_ref, ng_ref, ones_ref, out_ref, st_ref, ksh, bsh):
        c = pl.program_id(2)

        @pl.when(c == 0)
        def _():
            st_ref[...] = jnp.zeros_like(st_ref)
            ksh[0:S, :] = jnp.zeros((S, 2 * dk), F32)
            bsh[0:S, :] = jnp.zeros((S, 2 * dk), F32)

        xn = _rms_rows(x_ref[0], ing_ref[...]).astype(BF16)
        p = jnp.dot(xn, w_ref[0], preferred_element_type=F32)
        g_low = jnp.dot(xn, wl_ref[...], preferred_element_type=F32)[:, 0:GLA_RANK]
        z = _dot(g_low, gu_ref[...]) + gbias_ref[...]
        la = _log_sigmoid(z) / GLA_TAU
        rowi = lax.broadcasted_iota(jnp.int32, (tc, 1), 0)
        rmod = rowi & (S - 1)
        bcum = la
        rsum = la
        for s in (1, 2, 4, 8):
            bcum = bcum + jnp.where(rmod >= s, pltpu.roll(bcum, s, 0), 0.0)
            rsum = rsum + jnp.where(rmod < S - s, pltpu.roll(rsum, tc - s, 0), 0.0)
        q = p[:, 0:2 * dk] * (dk ** -0.5)
        k = p[:, 2 * dk:4 * dk]
        v = p[:, 4 * dk:4 * dk + 2 * dv]
        gate = p[:, 4 * dk + 2 * dv:4 * dk + 4 * dv]
        qt = q * jnp.exp(bcum)
        kt = k * jnp.exp(rsum - la)
        eg = jnp.exp(bcum + rsum - la)

        ksh[S:, :] = k
        bsh[S:, :] = bcum
        prods = []
        for d in range(S):
            kd = k if d == 0 else ksh[pl.ds(S - d, tc), :]
            bd = bcum if d == 0 else bsh[pl.ds(S - d, tc), :]
            e = jnp.exp(jnp.where(rmod >= d, bcum - bd, 0.0))
            prods.append((q * kd * e).astype(BF16))
        ws = [jnp.dot(p, ones_ref[...], preferred_element_type=F32) for p in prods]
        coli = lax.broadcasted_iota(jnp.int32, (tc, GB), 1)
        rgrp = rowi & (GB - 1)
        att0 = jnp.zeros((tc, GB), F32)
        att1 = jnp.zeros((tc, GB), F32)
        offs = jnp.where((coli // S) == (rgrp // S), rgrp - coli, -1)
        for d in range(S):
            here = offs == d
            att0 = jnp.where(here, ws[d][:, 0:GB], att0)
            att1 = jnp.where(here, ws[d][:, GB:2 * GB], att1)

        heads = range(2)
        lk = [slice(hh * dk, (hh + 1) * dk) for hh in heads]
        lv = [slice(hh * dv, (hh + 1) * dv) for hh in heads]
        kv = [[_dot_tn(v[si * S:(si + 1) * S, lv[hh]], kt[si * S:(si + 1) * S, lk[hh]]) for hh in heads]
              for si in range(nsub)]
        st = [st_ref[hh] for hh in heads]
        inter = [[], []]
        for si in range(nsub):
            rows = slice(si * S, (si + 1) * S)
            for hh in heads:
                inter[hh].append(_dot_nt(qt[rows, lk[hh]], st[hh]))
                st[hh] = st[hh] * eg[si * S:si * S + 1, lk[hh]] + kv[si][hh]
        for hh in heads:
            st_ref[hh] = st[hh]

        for hh, att in ((0, att0), (1, att1)):
            diag = jnp.concatenate([_dot(att[g * GB:(g + 1) * GB], v[g * GB:(g + 1) * GB, lv[hh]])
                                    for g in range(tc // GB)], axis=0)
            o = diag + jnp.concatenate(inter[hh], axis=0)
            on = o * lax.rsqrt(jnp.mean(o * o, axis=1, keepdims=True) + EPS)
            out_ref[0, :, lv[hh]] = (on * ng_ref[:, lv[hh]] * _silu(gate[:, lv[hh]])).astype(BF16)

    pw = 4 * dk + 4 * dv
    return pl.pallas_call(
        body,
        grid=(bsz, GLA_HEADS // 2, t // tc),
        in_specs=[pl.BlockSpec((1, tc, d), lambda b, h, c: (b, c, 0)),
                  pl.BlockSpec((1, d), lambda b, h, c: (0, 0)),
                  pl.BlockSpec((1, d, pw), lambda b, h, c: (h, 0, 0)),
                  pl.BlockSpec((d, 128), lambda b, h, c: (0, 0)),
                  pl.BlockSpec((GLA_RANK, 2 * dk), lambda b, h, c: (0, h)),
                  pl.BlockSpec((1, 2 * dk), lambda b, h, c: (0, h)),
                  pl.BlockSpec((1, 2 * dv), lambda b, h, c: (0, h)),
                  pl.BlockSpec((2 * dk, 2 * GB), lambda b, h, c: (0, 0))],
        out_specs=pl.BlockSpec((1, tc, 2 * dv), lambda b, h, c: (b, c, h)),
        out_shape=jax.ShapeDtypeStruct((bsz, t, GLA_HEADS * dv), BF16),
        scratch_shapes=[pltpu.VMEM((2, dv, dk), F32), pltpu.VMEM((tc + S, 2 * dk), F32),
                        pltpu.VMEM((tc + S, 2 * dk), F32)],
        compiler_params=_cp("parallel", "parallel", "arbitrary"),
        name="gla",
    )(x, in_g, w_pairs, w_low, gate_up, gate_b, norm_g, head_ones)


RET_TB = 512
RET_L = 256


def _retention(x, in_g, w_in, cos_t, sin_t, intra, inter, sdec, cdec, norm_g):
    bsz, t, d = x.shape
    TB, L, D, H, W = RET_TB, RET_L, RET_DIM, RET_HEADS, RET_W

    def body(x_ref, ing_ref, w_ref, cos_ref, sin_ref, intra_ref, inter_ref, sdec_ref, cdec_ref, ng_ref,
             out_ref, s_ref):
        c = pl.program_id(1)

        @pl.when(c == 0)
        def _():
            s_ref[...] = jnp.zeros_like(s_ref)

        p = jnp.dot(_rms_rows(x_ref[0], ing_ref[...]).astype(BF16), w_ref[...], preferred_element_type=F32)
        heads = range(H)
        s_st = [s_ref[h] for h in heads]
        for cc in range(TB // L):
            rows = slice(cc * L, (cc + 1) * L)
            cs = cos_ref[rows, :]
            sn = sin_ref[rows, :]

            def rot(z):
                return z * cs + pltpu.roll(z, D // 2, 1) * sn

            q = [rot(p[rows, h * D:(h + 1) * D]) * (D ** -0.5) for h in heads]
            k = [rot(p[rows, W + h * D:W + (h + 1) * D]) for h in heads]
            v = [p[rows, 2 * W + h * D:2 * W + (h + 1) * D] for h in heads]
            s = [_dot_nt(q[h], k[h]) * intra_ref[h] for h in heads]
            qs = [_dot(q[h], s_st[h]) for h in heads]
            kv = [_dot_tn(k[h] * sdec_ref[h], v[h]) for h in heads]
            sv = [_dot(s[h], v[h]) for h in heads]
            for h in heads:
                o = sv[h] + inter_ref[h] * qs[h]
                s_st[h] = cdec_ref[h] * s_st[h] + kv[h]
                oc = o - jnp.mean(o, axis=1, keepdims=True)
                on = oc * lax.rsqrt(jnp.mean(oc * oc, axis=1, keepdims=True) + EPS)
                hl = slice(h * D, (h + 1) * D)
                gate = p[rows, 3 * W + h * D:3 * W + (h + 1) * D]
                out_ref[0, rows, hl] = (on * ng_ref[:, hl] * _silu(gate)).astype(BF16)
        for h in heads:
            s_ref[h] = s_st[h]

    fix2 = lambda b, c: (0, 0)
    fix3 = lambda b, c: (0, 0, 0)
    return pl.pallas_call(
        body,
        grid=(bsz, t // TB),
        in_specs=[pl.BlockSpec((1, TB, d), lambda b, c: (b, c, 0)),
                  pl.BlockSpec((1, d), fix2),
                  pl.BlockSpec((d, 4 * W), fix2),
                  pl.BlockSpec((TB, D), lambda b, c: (c, 0)),
                  pl.BlockSpec((TB, D), lambda b, c: (c, 0)),
                  pl.BlockSpec((H, L, L), fix3),
                  pl.BlockSpec((H, L, 1), fix3),
                  pl.BlockSpec((H, L, 1), fix3),
                  pl.BlockSpec((H, 1, 1), fix3),
                  pl.BlockSpec((1, W), lambda b, c: (0, 0))],
        out_specs=pl.BlockSpec((1, TB, W), lambda b, c: (b, c, 0)),
        out_shape=jax.ShapeDtypeStruct((bsz, t, W), BF16),
        scratch_shapes=[pltpu.VMEM((H, D, D), F32)],
        compiler_params=_cp("parallel", "arbitrary"),
        name="retention",
    )(x, in_g, w_in, cos_t, sin_t, intra, inter, sdec, cdec, norm_g)


def _retention_tables(t):
    L, D = RET_L, RET_DIM
    inv = ROPE_BASE ** (-jnp.arange(0, D, 2, dtype=F32) / D)
    ang = jnp.arange(t).astype(F32)[:, None] * inv[None, :]
    cos = jnp.cos(ang)
    sin = jnp.sin(ang)
    cos_t = jnp.concatenate([cos, cos], axis=-1)
    sin_t = jnp.concatenate([-sin, sin], axis=-1)
    log_gamma = jnp.log1p(-jnp.exp2(-5.0 - jnp.arange(RET_HEADS, dtype=F32)))
    idx = jnp.arange(L, dtype=F32)
    causal = idx[:, None] >= idx[None, :]
    rel = jnp.where(causal, idx[:, None] - idx[None, :], 0.0)
    intra = jnp.where(causal, jnp.exp(log_gamma[:, None, None] * rel), 0.0)
    inter = jnp.exp(log_gamma[:, None] * (idx + 1.0))[:, :, None]
    sdec = jnp.exp(log_gamma[:, None] * (L - 1.0 - idx))[:, :, None]
    cdec = jnp.exp(log_gamma * L)[:, None, None]
    return cos_t, sin_t, intra, inter, sdec, cdec


RWP_TM = 512


def _seg_sum(x, bd):
    hi = x.astype(BF16)
    lo = (x - hi.astype(F32)).astype(BF16)
    return jnp.dot(hi, bd, preferred_element_type=F32) + jnp.dot(lo, bd, preferred_element_type=F32)


def _rwkv_prep(x, in_g, w_in, mu, w_up, w0, a_up, a0, g_up, k_k, k_a, r_k, bd):
    bsz, t, d = x.shape
    tm, W = RWP_TM, RW_W

    def body(x_ref, ing_ref, win_ref, mu_ref, wup_ref, w0_ref, aup_ref, a0_ref, gup_ref, kk_ref, ka_ref, rk_ref,
             bd_ref, r_out, lw_out, k_out, v_out, a_out, b_out, g_out, bonus_out, last_ref):
        c = pl.program_id(1)

        @pl.when(c == 0)
        def _():
            last_ref[...] = jnp.zeros_like(last_ref)

        cur = jnp.dot(_rms_rows(x_ref[0], ing_ref[...]).astype(BF16), win_ref[...], preferred_element_type=F32)
        prev = last_ref[...]
        last_ref[...] = cur[tm - 1:tm, :]
        rowi = lax.broadcasted_iota(jnp.int32, (tm, 1), 0)
        sh = jnp.where(rowi == 0, prev, pltpu.roll(cur, 1, 0))
        xm = cur + (sh - cur) * mu_ref[...]
        x_r = xm[:, 0:W]
        x_k = xm[:, W:2 * W]
        x_v = xm[:, 2 * W:3 * W]
        x_dl = xm[:, 3 * W:3 * W + 64]
        x_al = xm[:, 3 * W + 64:3 * W + 128]
        x_gl = xm[:, 3 * W + 128:3 * W + 256]
        wl = w0_ref[...] + _dot(jnp.tanh(x_dl), wup_ref[...])
        sp = jnp.maximum(-wl, 0.0) + jnp.log(1.0 + jnp.exp(-jnp.abs(wl)))
        lw_out[0] = -jnp.exp(-sp - 0.5)
        a = _sigmoid(a0_ref[...] + _dot(x_al, aup_ref[...]))
        g_out[0] = _dot(_sigmoid(x_gl), gup_ref[...]).astype(BF16)
        kk0 = x_k * kk_ref[...]
        nrm = jnp.sqrt(_seg_sum(kk0 * kk0, bd_ref[...]))
        kk = kk0 / jnp.maximum(nrm, 1e-12)
        k_h = x_k * (1.0 + (a - 1.0) * ka_ref[...])
        r_out[0] = x_r.astype(BF16)
        k_out[0] = k_h.astype(BF16)
        v_out[0] = x_v.astype(BF16)
        a_out[0] = (-kk).astype(BF16)
        b_out[0] = (kk * a).astype(BF16)
        bonus_out[0] = (_dot(x_r * k_h * rk_ref[...], bd_ref[...]) * x_v).astype(BF16)

    row = lambda b, c: (0, 0)
    blk = pl.BlockSpec((1, tm, W), lambda b, c: (b, c, 0))
    shp = [jax.ShapeDtypeStruct((bsz, t, W), F32 if i == 1 else BF16) for i in range(8)]
    return pl.pallas_call(
        body,
        grid=(bsz, t // tm),
        in_specs=[pl.BlockSpec((1, tm, d), lambda b, c: (b, c, 0)),
                  pl.BlockSpec((1, d), row),
                  pl.BlockSpec((d, RW_COLS), row),
                  pl.BlockSpec((1, RW_COLS), row),
                  pl.BlockSpec((64, W), row), pl.BlockSpec((1, W), row),
                  pl.BlockSpec((64, W), row), pl.BlockSpec((1, W), row),
                  pl.BlockSpec((128, W), row),
                  pl.BlockSpec((1, W), row), pl.BlockSpec((1, W), row), pl.BlockSpec((1, W), row),
                  pl.BlockSpec((W, W), row)],
        out_specs=[blk] * 8,
        out_shape=shp,
        scratch_shapes=[pltpu.VMEM((1, RW_COLS), F32)],
        compiler_params=_cp("parallel", "arbitrary"),
        name="rwkv_prep",
    )(x, in_g, w_in, mu, w_up, w0, a_up, a0, g_up, k_k, k_a, r_k, bd)


RW_L = 64
RW_TB = 256


def _rwkv_scan(r, lw, k, v, aa, bb, g_out, bonus, ln_g, ln_b):
    bsz, t, W = r.shape
    L, N, tb = RW_L, RW_DIM, RW_TB
    nck = tb // L

    def body(r_ref, lw_ref, k_ref, v_ref, a_ref, b_ref, g_ref, bonus_ref, lng_ref, lnb_ref, out_ref, h_ref):
        c = pl.program_id(1)

        @pl.when(c == 0)
        def _():
            h_ref[...] = jnp.zeros_like(h_ref)

        lw_all = lw_ref[0]
        rowi = lax.broadcasted_iota(jnp.int32, (tb, 1), 0) & (L - 1)
        cl = lw_all
        for s in (1, 2, 4, 8, 16, 32):
            cl = cl + jnp.where(rowi >= s, pltpu.roll(cl, s, 0), 0.0)
        cl_last = jnp.concatenate([jnp.broadcast_to(cl[(cc + 1) * L - 1:(cc + 1) * L, :], (L, W))
                                   for cc in range(nck)], axis=0)
        e_inv = jnp.exp(-cl)
        e_end = jnp.exp(cl_last - cl)
        p_end = jnp.exp(cl_last)
        at = a_ref[0] * jnp.exp(cl - lw_all)
        rt = r_ref[0] * jnp.exp(cl)
        bt = b_ref[0] * e_inv
        kt = k_ref[0] * e_inv
        b_end = b_ref[0] * e_end
        k_end = k_ref[0] * e_end
        v_all = v_ref[0]
        P2 = 2 * N
        pairs = [(cc, p) for cc in range(nck) for p in range(RW_HEADS // 2)]
        pr = range(len(pairs))

        def pb(z, pi):
            cc, p = pairs[pi]
            return z[cc * L:(cc + 1) * L, p * P2:(p + 1) * P2]

        def stack2(z):
            lane = lax.broadcasted_iota(jnp.int32, (1, z.shape[1]), 1) & (P2 - 1)
            return jnp.concatenate([jnp.where(lane < N, z, 0.0), jnp.where(lane >= N, z, 0.0)], axis=0)

        tcol = lax.broadcasted_iota(jnp.int32, (L, P2), 1) & (N - 1)
        trow = lax.broadcasted_iota(jnp.int32, (L, P2), 0)
        strict = trow > tcol
        lower = trow >= tcol
        r2 = lax.broadcasted_iota(jnp.int32, (P2, P2), 0)
        c2 = lax.broadcasted_iota(jnp.int32, (P2, P2), 1)
        same_head = (r2 // N) == (c2 // N)
        eye2 = r2 == c2
        x2 = [jnp.concatenate([pb(at, pi), pb(rt, pi)], axis=0) for pi in pr]
        m_b = [_dot_nt(x2[pi], stack2(pb(bt, pi))) for pi in pr]
        m_k = [_dot_nt(x2[pi], stack2(pb(kt, pi))) for pi in pr]
        ap = [jnp.where(strict, m_b[pi][0:L], 0.0) for pi in pr]
        a_rb = [jnp.where(lower, m_b[pi][L:2 * L], 0.0) for pi in pr]
        a_ak = [jnp.where(strict, m_k[pi][0:L], 0.0) for pi in pr]
        a_rk = [jnp.where(lower, m_k[pi][L:2 * L], 0.0) for pi in pr]
        vp = [_dot(jnp.concatenate([a_ak[pi], a_rk[pi]], axis=0), stack2(pb(v_all, pi))) for pi in pr]
        kv = [_dot(pb(k_end, pi).T, pb(v_all, pi)) for pi in pr]
        x = [jnp.concatenate([pb(at, pi), vp[pi][0:L]], axis=1) for pi in pr]
        for it in range(6):
            x = [x[pi] + _dot(ap[pi], stack2(x[pi])) for pi in pr]
            if it < 5:
                ap = [_dot(ap[pi], stack2(ap[pi])) for pi in pr]
        post1 = [_dot(a_rb[pi], stack2(x[pi])) for pi in pr]
        post2 = [_dot(pb(b_end, pi).T, x[pi]) for pi in pr]
        lhs = []
        y0s = []
        h_adds = []
        for pi in pr:
            q_hat = pb(rt, pi) + post1[pi][:, 0:P2]
            gmat = (jnp.where(same_head, post2[pi][:, 0:P2], 0.0)
                    + jnp.where(eye2, pb(p_end, pi)[0:1, :], 0.0))
            lhs.append(jnp.concatenate([q_hat, gmat], axis=0))
            y0s.append(post1[pi][:, P2:2 * P2] + vp[pi][L:2 * L])
            h_adds.append(jnp.where(same_head, post2[pi][:, P2:2 * P2] + kv[pi], 0.0))
        npair = RW_HEADS // 2
        h_st = [h_ref[p] for p in range(npair)]
        ys = [None] * len(pairs)
        for cc in range(nck):
            res = [_dot(lhs[cc * npair + p], h_st[p]) for p in range(npair)]
            for p in range(npair):
                pi = cc * npair + p
                ys[pi] = res[p][0:L] + y0s[pi]
                h_st[p] = res[p][L:L + P2] + h_adds[pi]
        for p in range(npair):
            h_ref[p] = h_st[p]
        seg = same_head.astype(BF16)
        mean = [_seg_sum(ys[pi], seg) * (1.0 / N) for pi in pr]
        yc = [ys[pi] - mean[pi] for pi in pr]
        var = [_seg_sum(yc[pi] * yc[pi], seg) * (1.0 / N) for pi in pr]
        for pi, (cc, p) in enumerate(pairs):
            yn = yc[pi] * lax.rsqrt(var[pi] + RW_LN_EPS)
            rows = slice(cc * L, (cc + 1) * L)
            cols = slice(p * P2, (p + 1) * P2)
            out_ref[0, rows, cols] = ((yn * lng_ref[:, cols] + lnb_ref[:, cols] + bonus_ref[0, rows, cols])
                                      * g_ref[0, rows, cols]).astype(BF16)

    blk = pl.BlockSpec((1, tb, W), lambda b, c: (b, c, 0))
    vec = pl.BlockSpec((1, W), lambda b, c: (0, 0))
    return pl.pallas_call(
        body,
        grid=(bsz, t // tb),
        in_specs=[blk] * 8 + [vec, vec],
        out_specs=blk,
        out_shape=jax.ShapeDtypeStruct((bsz, t, W), BF16),
        scratch_shapes=[pltpu.VMEM((RW_HEADS // 2, 2 * N, 2 * N), F32)],
        compiler_params=_cp("parallel", "arbitrary"),
        name="rwkv_scan",
    )(r, lw, k, v, aa, bb, g_out, bonus, ln_g, ln_b)


def _mix_out_route(a, b, w_bf16, resid, g, router_split, tm=1024):
    n, wa = a.shape
    wb = b.shape[1]
    d = w_bf16.shape[1]
    e = router_split.shape[2]

    def body(a_ref, b_ref, w_ref, r_ref, g_ref, rt_ref, h_ref, xn_ref, lg_ref):
        acc = jnp.dot(a_ref[...], w_ref[0:wa, :], preferred_element_type=F32)
        acc = acc + jnp.dot(b_ref[...], w_ref[wa:wa + wb, :], preferred_element_type=F32)
        h = r_ref[...] + acc
        h_ref[...] = h
        xn = _rms_rows(h, g_ref[...])
        x_hi = xn.astype(BF16)
        xn_ref[...] = x_hi
        x_lo = (xn - x_hi.astype(F32)).astype(BF16)
        r_hi = rt_ref[0]
        r_lo = rt_ref[1]
        lg_ref[...] = (jnp.dot(x_hi, r_hi, preferred_element_type=F32)
                       + (jnp.dot(x_lo, r_hi, preferred_element_type=F32)
                          + jnp.dot(x_hi, r_lo, preferred_element_type=F32)))

    return pl.pallas_call(
        body,
        grid=(n // tm,),
        in_specs=[pl.BlockSpec((tm, wa), lambda i: (i, 0)),
                  pl.BlockSpec((tm, wb), lambda i: (i, 0)),
                  pl.BlockSpec((wa + wb, d), lambda i: (0, 0)),
                  pl.BlockSpec((tm, d), lambda i: (i, 0)),
                  pl.BlockSpec((1, d), lambda i: (0, 0)),
                  pl.BlockSpec((2, d, e), lambda i: (0, 0, 0))],
        out_specs=[pl.BlockSpec((tm, d), lambda i: (i, 0)), pl.BlockSpec((tm, d), lambda i: (i, 0)),
                   pl.BlockSpec((tm, e), lambda i: (i, 0))],
        out_shape=[jax.ShapeDtypeStruct((n, d), F32), jax.ShapeDtypeStruct((n, d), BF16),
                   jax.ShapeDtypeStruct((n, e), F32)],
        compiler_params=_cp("parallel"),
        name="mix_out_route",
    )(a, b, w_bf16, resid, g.reshape(1, d), router_split)


MOE_TM = 512
MOE_TF = 1792


def _experts(xs, row_w, item_tile, item_exp, item_lo, item_hi, wg, wu, wd):
    nrows, d = xs.shape
    tm, tf = MOE_TM, MOE_TF
    nf = D_FF // tf
    n_items = item_tile.shape[0]

    def body(it_ref, ie_ref, lo_ref, hi_ref, x_ref, w_ref, wg_ref, wu_ref, wd_ref, o_ref, acc_ref):
        i = pl.program_id(0)
        j = pl.program_id(1)
        tile = it_ref[i]
        first = jnp.logical_or(i == 0, tile != it_ref[jnp.maximum(i - 1, 0)])
        last = jnp.logical_or(i == n_items - 1, tile != it_ref[jnp.minimum(i + 1, n_items - 1)])

        @pl.when(jnp.logical_and(first, j == 0))
        def _():
            acc_ref[...] = jnp.zeros_like(acc_ref)

        lo = lo_ref[i]
        hi = hi_ref[i]

        @pl.when(lo < hi)
        def _():
            x = x_ref[...]
            gg = jnp.dot(x, wg_ref[0], preferred_element_type=F32)
            uu = jnp.dot(x, wu_ref[0], preferred_element_type=F32)
            act = (_silu(gg) * uu).astype(BF16)
            part = jnp.dot(act, wd_ref[0], preferred_element_type=F32)
            rowi = lax.broadcasted_iota(jnp.int32, (tm, 1), 0)
            mine = jnp.logical_and(rowi >= lo, rowi < hi)
            acc_ref[...] += part * jnp.where(mine, w_ref[...], 0.0)

        @pl.when(jnp.logical_and(last, j == nf - 1))
        def _():
            o_ref[...] = acc_ref[...].astype(o_ref.dtype)

    grid_spec = pltpu.PrefetchScalarGridSpec(
        num_scalar_prefetch=4,
        grid=(n_items, nf),
        in_specs=[pl.BlockSpec((tm, d), lambda i, j, it, ie, lo, hi: (it[i], 0)),
                  pl.BlockSpec((tm, 1), lambda i, j, it, ie, lo, hi: (it[i], 0)),
                  pl.BlockSpec((1, d, tf), lambda i, j, it, ie, lo, hi: (ie[i], 0, j)),
                  pl.BlockSpec((1, d, tf), lambda i, j, it, ie, lo, hi: (ie[i], 0, j)),
                  pl.BlockSpec((1, tf, d), lambda i, j, it, ie, lo, hi: (ie[i], j, 0))],
        out_specs=pl.BlockSpec((tm, d), lambda i, j, it, ie, lo, hi: (it[i], 0)),
        scratch_shapes=[pltpu.VMEM((tm, d), F32)],
    )
    return pl.pallas_call(
        body,
        grid_spec=grid_spec,
        out_shape=jax.ShapeDtypeStruct((nrows, d), BF16),
        compiler_params=_cp("arbitrary", "arbitrary"),
        name="moe_experts",
    )(item_tile, item_exp, item_lo, item_hi, xs, row_w, wg, wu, wd)


def _combine_norm(h, y0, y1, g, tm=512):
    n, d = h.shape

    def body(h_ref, a_ref, b_ref, g_ref, o_ref):
        o_ref[...] = _rms_rows(h_ref[...] + (a_ref[...].astype(F32) + b_ref[...].astype(F32)), g_ref[...])

    blk = pl.BlockSpec((tm, d), lambda i: (i, 0))
    return pl.pallas_call(
        body,
        grid=(n // tm,),
        in_specs=[blk, blk, blk, pl.BlockSpec((1, d), lambda i: (0, 0))],
        out_specs=blk,
        out_shape=jax.ShapeDtypeStruct((n, d), F32),
        compiler_params=_cp("parallel"),
        name="combine_norm",
    )(h, y0, y1, g.reshape(1, d))


def _route(logits, n):
    tm = MOE_TM
    na = n * TOP_K
    n_tiles = na // tm
    top_val, top_idx = lax.top_k(logits, TOP_K)
    top_w = jax.nn.softmax(top_val, axis=-1)
    e_flat = top_idx.reshape(-1).astype(jnp.int32)
    w_flat = top_w.reshape(-1)
    tok = jnp.arange(na, dtype=jnp.int32) // TOP_K
    _, sorted_tok, sorted_w = lax.sort((e_flat, tok, w_flat), num_keys=1, is_stable=True)
    onehot = (e_flat[:, None] == jnp.arange(N_EXPERTS, dtype=jnp.int32)[None, :]).astype(jnp.int32)
    rank = jnp.take_along_axis(jnp.cumsum(onehot, axis=0), e_flat[:, None], axis=1)[:, 0] - 1
    counts = jnp.sum(onehot, axis=0)
    ends = jnp.cumsum(counts)
    pos = (ends - counts)[e_flat] + rank
    cuts = jnp.sort(jnp.concatenate([jnp.arange(n_tiles, dtype=jnp.int32) * tm, ends[:-1].astype(jnp.int32)]))
    nxt = jnp.concatenate([cuts[1:], jnp.full((1,), na, jnp.int32)])
    item_tile = jnp.minimum(cuts // tm, n_tiles - 1)
    item_exp = jnp.minimum(jnp.searchsorted(ends, cuts, side="right"), N_EXPERTS - 1).astype(jnp.int32)
    item_lo = cuts - item_tile * tm
    item_hi = nxt - item_tile * tm
    return sorted_tok, sorted_w, (item_tile, item_exp, item_lo, item_hi), pos.reshape(n, TOP_K)


def kernel(x, e_norm1_g, e_w_in, e_ml_conv_w, e_ml_conv_b, e_ml_gate_b, e_ml_norm_g, e_gla_gate_up, e_gla_gate_b,
           e_gla_norm_g, e_w_out, e_norm2_g, e_ffn_w_gate, e_ffn_w_up, e_ffn_w_down, o_norm1_g, o_w_in,
           o_ret_norm_g, o_rw_mu, o_rw_w_up, o_rw_w0, o_rw_a_up, o_rw_a0, o_rw_g_up, o_rw_k_k, o_rw_k_a, o_rw_r_k,
           o_rw_ln_g, o_rw_ln_b, o_w_out, o_norm2_g, o_moe_router, o_moe_w_gate, o_moe_w_up, o_moe_w_down,
           final_norm_g):
    bsz, t, d = x.shape
    n = bsz * t
    h0 = x.reshape(n, d)

    w = e_w_in[0]
    row = lambda a: a.reshape(1, -1)
    x3 = x
    ng1 = row(e_norm1_g[0])
    w_if = w[:, 2048:2056]
    w_gate = jnp.zeros((d, 128), F32).at[:, :2 * ML_HEADS].set(w_if).astype(BF16)
    w_gate_t = jnp.zeros((16, d), F32).at[:2 * ML_HEADS, :].set(w_if.T).astype(BF16)
    h_ml = _mlstm(x3, ng1, w[:, :4 * ML_W].astype(BF16), w_gate, w_gate_t, row(e_ml_gate_b[0]),
                  e_ml_gate_b[0].reshape(-1, 1), e_ml_conv_w[0], row(e_ml_conv_b[0]), row(e_ml_norm_g[0]))
    gq, gk, gv, gr = 2056, 2312, 2568, 3080
    w_pairs = jnp.stack([jnp.concatenate([w[:, gq + 128 * hp:gq + 128 * (hp + 1)],
                                          w[:, gk + 128 * hp:gk + 128 * (hp + 1)],
                                          w[:, gv + 256 * hp:gv + 256 * (hp + 1)],
                                          w[:, gr + 256 * hp:gr + 256 * (hp + 1)]], axis=1)
                         for hp in range(GLA_HEADS // 2)]).astype(BF16)
    w_low = jnp.zeros((d, 128), F32).at[:, :GLA_RANK].set(w[:, 3592:3608]).astype(BF16)
    o_gla = _gla(x3, ng1, w_pairs, w_low, e_gla_gate_up[0].astype(BF16), row(e_gla_gate_b[0]),
                 row(e_gla_norm_g[0]))
    h2 = _mix_out_ffn(h_ml.reshape(n, -1), o_gla.reshape(n, -1), e_w_out[0].astype(BF16), h0, e_norm2_g[0],
                      e_ffn_w_gate[0].astype(BF16), e_ffn_w_up[0].astype(BF16), e_ffn_w_down[0].astype(BF16))

    w = o_w_in[0]
    h2_3 = h2.reshape(bsz, t, d)
    ng2 = row(o_norm1_g[0])
    y_ret = _retention(h2_3, ng2, w[:, :4 * RET_W].astype(BF16), *_retention_tables(t), row(o_ret_norm_g[0]))
    head_of = jnp.arange(RW_W) // RW_DIM
    bd = (head_of[:, None] == head_of[None, :]).astype(BF16)
    r, lw, k, v, aa, bb, g_out, bonus = _rwkv_prep(
        h2_3, ng2, w[:, 4 * RET_W:].astype(BF16), row(o_rw_mu[0]), o_rw_w_up[0].astype(BF16), row(o_rw_w0[0]),
        o_rw_a_up[0].astype(BF16), row(o_rw_a0[0]), o_rw_g_up[0].astype(BF16), row(o_rw_k_k[0]),
        row(o_rw_k_a[0]), row(o_rw_r_k[0]), bd)
    y_rw = _rwkv_scan(r, lw, k, v, aa, bb, g_out, bonus, row(o_rw_ln_g[0]), row(o_rw_ln_b[0]))
    router_pad = jnp.zeros((d, 128), F32).at[:, :N_EXPERTS].set(o_moe_router[0])
    router_hi = router_pad.astype(BF16)
    router_split = jnp.stack([router_hi, (router_pad - router_hi.astype(F32)).astype(BF16)])
    h3, xn, logits = _mix_out_route(y_ret.reshape(n, -1), y_rw.reshape(n, -1), o_w_out[0].astype(BF16), h2,
                                    o_norm2_g[0], router_split)
    sorted_tok, sorted_w, items, pos = _route(logits[:, :N_EXPERTS], n)
    xs = xn.at[sorted_tok].get(mode="promise_in_bounds")
    ys = _experts(xs, sorted_w.reshape(-1, 1), *items, o_moe_w_gate[0].astype(BF16),
                  o_moe_w_up[0].astype(BF16), o_moe_w_down[0].astype(BF16))
    y0 = ys.at[pos[:, 0]].get(mode="promise_in_bounds")
    y1 = ys.at[pos[:, 1]].get(mode="promise_in_bounds")
    out = _combine_norm(h3, y0, y1, final_norm_g)
    return out.reshape(bsz, t, d)
```

```python
import functools

import numpy as np
import jax
import jax.numpy as jnp
from jax import lax
from jax.experimental import pallas as pl
from jax.experimental.pallas import tpu as pltpu

F32 = jnp.float32
BF16 = jnp.bfloat16

D_MODEL = 1024
EPS = 1e-6
ML_HEADS, ML_DIM, ML_W, ML_CONV = 4, 128, 512, 4
GLA_HEADS, GLA_DK, GLA_DV, GLA_RANK, GLA_TAU = 4, 64, 128, 16, 16.0
RET_HEADS, RET_DIM, RET_W = 4, 128, 512
ROPE_BASE = 10000.0
RW_HEADS, RW_DIM, RW_W = 8, 64, 512
RW_COLS = 1792
RW_LN_EPS = 64e-5
D_FF = 3584
N_EXPERTS = 8
TOP_K = 2

VMEM_LIMIT = 48 * 1024 * 1024
NEG = -1e30


def _cp(*sem):
    return pltpu.CompilerParams(dimension_semantics=sem, vmem_limit_bytes=VMEM_LIMIT)


def _sigmoid(x):
    return 1.0 / (1.0 + jnp.exp(-x))


def _silu(x):
    return x * _sigmoid(x)


def _log_sigmoid(x):
    return jnp.minimum(x, 0.0) - jnp.log(1.0 + jnp.exp(-jnp.abs(x)))


def _dot(a, b):
    return jnp.dot(a.astype(BF16), b.astype(BF16), preferred_element_type=F32)


def _dot_nt(a, b):
    return lax.dot_general(a.astype(BF16), b.astype(BF16), (((1,), (1,)), ((), ())), preferred_element_type=F32)


def _dot_tn(a, b):
    return jnp.dot(a.T.astype(BF16), b.astype(BF16), preferred_element_type=F32)


def _rms_rows(x, g):
    ms = jnp.mean(x * x, axis=-1, keepdims=True)
    return x * lax.rsqrt(ms + EPS) * g


def _mix_out_ffn(a, b, w_out, resid, g, wg, wu, wd, tm=512, tf=1792):
    n, d = resid.shape
    wa = a.shape[1]
    wb = b.shape[1]
    f = wg.shape[1]
    nf = f // tf

    def body(a_ref, b_ref, wo_ref, r_ref, g_ref, wg_ref, wu_ref, wd_ref, o_ref, xn_ref, acc_ref):
        j = pl.program_id(1)

        @pl.when(j == 0)
        def _():
            h = r_ref[...] + jnp.dot(a_ref[...], wo_ref[0:wa, :], preferred_element_type=F32)
            h = h + jnp.dot(b_ref[...], wo_ref[wa:wa + wb, :], preferred_element_type=F32)
            xn_ref[...] = _rms_rows(h, g_ref[...]).astype(BF16)
            acc_ref[...] = h

        xn = xn_ref[...]
        gg = jnp.dot(xn, wg_ref[...], preferred_element_type=F32)
        uu = jnp.dot(xn, wu_ref[...], preferred_element_type=F32)
        act = (_silu(gg) * uu).astype(BF16)
        acc_ref[...] += jnp.dot(act, wd_ref[...], preferred_element_type=F32)

        @pl.when(j == nf - 1)
        def _():
            o_ref[...] = acc_ref[...]

    return pl.pallas_call(
        body,
        grid=(n // tm, nf),
        in_specs=[pl.BlockSpec((tm, wa), lambda i, j: (i, 0)),
                  pl.BlockSpec((tm, wb), lambda i, j: (i, 0)),
                  pl.BlockSpec((wa + wb, d), lambda i, j: (0, 0)),
                  pl.BlockSpec((tm, d), lambda i, j: (i, 0)),
                  pl.BlockSpec((1, d), lambda i, j: (0, 0)),
                  pl.BlockSpec((d, tf), lambda i, j: (0, j)),
                  pl.BlockSpec((d, tf), lambda i, j: (0, j)),
                  pl.BlockSpec((tf, d), lambda i, j: (j, 0))],
        out_specs=pl.BlockSpec((tm, d), lambda i, j: (i, 0)),
        out_shape=jax.ShapeDtypeStruct((n, d), F32),
        scratch_shapes=[pltpu.VMEM((tm, d), BF16), pltpu.VMEM((tm, d), F32)],
        compiler_params=_cp("parallel", "arbitrary"),
        name="mix_out_ffn",
    )(a, b, w_out, resid, g.reshape(1, d), wg, wu, wd)


ML_TB = 1024
ML_L = 256


def _mlstm(x, in_g, w_in, w_gate, w_gate_t, gate_b_row, gate_b_col, conv_w, conv_b, norm_g):
    bsz, t, d = x.shape
    TB, L, D, H, W = ML_TB, ML_L, ML_DIM, ML_HEADS, ML_W

    def body(x_ref, ing_ref, w_ref, wg_ref, wgt_ref, gbr_ref, gbc_ref, cwq_ref, cwk_ref, cbq_ref, cbk_ref,
             ng_ref, out_ref, qext, kext, c_ref, n_ref, m_ref):
        c = pl.program_id(1)

        @pl.when(c == 0)
        def _():
            qext[0:8, :] = jnp.zeros((8, W), F32)
            kext[0:8, :] = jnp.zeros((8, W), F32)
            c_ref[...] = jnp.zeros_like(c_ref)
            n_ref[...] = jnp.zeros_like(n_ref)
            m_ref[...] = jnp.zeros_like(m_ref)

        xn = _rms_rows(x_ref[0], ing_ref[...]).astype(BF16)
        p = jnp.dot(xn, w_ref[...], preferred_element_type=F32)
        g_cols = jnp.dot(xn, wg_ref[...], preferred_element_type=F32)[:, 0:2 * H]
        g_rows = lax.dot_general(wgt_ref[...], xn, (((1,), (1,)), ((), ())),
                                 preferred_element_type=F32)[0:2 * H, :]
        qext[8:, :] = p[:, 0:W]
        kext[8:, :] = p[:, W:2 * W]

        def conv(ext, cw_ref, cb_ref):
            acc = cb_ref[...] + cw_ref[0:1, :] * ext[pl.ds(8 - ML_CONV + 1, TB), :]
            for kk in range(1, ML_CONV):
                acc = acc + cw_ref[kk:kk + 1, :] * ext[pl.ds(8 - ML_CONV + 1 + kk, TB), :]
            return _silu(acc)

        q_all = conv(qext, cwq_ref, cbq_ref) * (D ** -0.5)
        k_all = conv(kext, cwk_ref, cbk_ref)
        qext[0:8, :] = qext[TB:TB + 8, :]
        kext[0:8, :] = kext[TB:TB + 8, :]

        gcol = g_cols + gbr_ref[...]
        grow = g_rows + gbc_ref[...]
        fcol = _log_sigmoid(gcol[:, H:2 * H])
        frow = _log_sigmoid(grow[H:2 * H, :])
        ri = lax.broadcasted_iota(jnp.int32, (L, L), 0)
        ci = lax.broadcasted_iota(jnp.int32, (L, L), 1)
        causal = ri >= ci
        heads = range(H)
        c_st = [c_ref[h] for h in heads]
        n_st = [n_ref[h] for h in heads]
        m_st = [m_ref[h] for h in heads]
        for cc in range(TB // L):
            rows = slice(cc * L, (cc + 1) * L)
            hs = lambda z, h: z[rows, h * D:(h + 1) * D]
            w_intra, w_inter, w_state, carry, m_row = [], [], [], [], []
            for h in heads:
                f_row = frow[h:h + 1, rows]
                i_row = grow[h:h + 1, rows]
                f_col = fcol[rows, h:h + 1]
                i_col = gcol[rows, h:h + 1]
                b_col = jnp.sum(jnp.where(causal, f_row, 0.0), axis=1, keepdims=True)
                b_row = jnp.sum(jnp.where(ri <= ci, f_col, 0.0), axis=0, keepdims=True)
                g_tot = jnp.sum(f_row, axis=1, keepdims=True)
                d_intra = jnp.where(causal, b_col - b_row + i_row, NEG)
                d_inter = b_col + m_st[h]
                mr = jnp.maximum(d_inter, jnp.max(d_intra, axis=1, keepdims=True))
                w_intra.append(jnp.exp(d_intra - mr))
                w_inter.append(jnp.exp(d_inter - mr))
                m_row.append(mr)
                d_state = g_tot - b_col + i_col
                mn = jnp.maximum(g_tot + m_st[h], jnp.max(d_state, axis=0, keepdims=True))
                w_state.append(jnp.exp(d_state - mn))
                carry.append(jnp.exp(g_tot + m_st[h] - mn))
                m_st[h] = mn
            qh = [hs(q_all, h) for h in heads]
            kh = [hs(k_all, h) for h in heads]
            vh = [p[rows, 2 * W + h * D:2 * W + (h + 1) * D] for h in heads]
            s = [_dot_nt(qh[h], kh[h]) * w_intra[h] for h in heads]
            qc = [_dot(qh[h], c_st[h]) for h in heads]
            kw = [kh[h] * w_state[h] for h in heads]
            kv = [_dot_tn(kw[h], vh[h]) for h in heads]
            sv = [_dot(s[h], vh[h]) for h in heads]
            for h in heads:
                num = sv[h] + w_inter[h] * qc[h]
                den = (jnp.sum(s[h], axis=1, keepdims=True)
                       + w_inter[h] * jnp.sum(qh[h] * n_st[h], axis=1, keepdims=True))
                hval = num / jnp.maximum(jnp.abs(den), jnp.exp(-m_row[h]))
                c_st[h] = carry[h] * c_st[h] + kv[h]
                n_st[h] = carry[h] * n_st[h] + jnp.sum(kw[h], axis=0, keepdims=True)
                hg = _sigmoid(p[rows, 3 * W + h * D:3 * W + (h + 1) * D]) * hval
                hc = hg - jnp.mean(hg, axis=1, keepdims=True)
                hn = hc * lax.rsqrt(jnp.mean(hc * hc, axis=1, keepdims=True) + EPS)
                out_ref[0, rows, h * D:(h + 1) * D] = (hn * ng_ref[:, h * D:(h + 1) * D]).astype(BF16)
        for h in heads:
            c_ref[h] = c_st[h]
            n_ref[h] = n_st[h]
            m_ref[h] = m_st[h]

    fix = lambda j: (lambda b, c: (0, j))
    return pl.pallas_call(
        body,
        grid=(bsz, t // TB),
        in_specs=[pl.BlockSpec((1, TB, d), lambda b, c: (b, c, 0)),
                  pl.BlockSpec((1, d), fix(0)),
                  pl.BlockSpec((d, 4 * W), fix(0)),
                  pl.BlockSpec((d, 128), fix(0)),
                  pl.BlockSpec((16, d), fix(0)),
                  pl.BlockSpec((1, 2 * H), fix(0)),
                  pl.BlockSpec((2 * H, 1), fix(0)),
                  pl.BlockSpec((ML_CONV, W), fix(0)),
                  pl.BlockSpec((ML_CONV, W), fix(1)),
                  pl.BlockSpec((1, W), fix(0)),
                  pl.BlockSpec((1, W), fix(1)),
                  pl.BlockSpec((1, W), fix(0))],
        out_specs=pl.BlockSpec((1, TB, W), lambda b, c: (b, c, 0)),
        out_shape=jax.ShapeDtypeStruct((bsz, t, W), BF16),
        scratch_shapes=[pltpu.VMEM((TB + 8, W), F32), pltpu.VMEM((TB + 8, W), F32),
                        pltpu.VMEM((H, D, D), F32), pltpu.VMEM((H, 1, D), F32), pltpu.VMEM((H, 1, 1), F32)],
        compiler_params=_cp("parallel", "arbitrary"),
        name="mlstm",
    )(x, in_g, w_in, w_gate, w_gate_t, gate_b_row, gate_b_col, conv_w, conv_w, conv_b, conv_b, norm_g)


GLA_TC = 512
GLA_SUB = 16
GLA_GROUP = 128


def _gla(x, in_g, w_pairs, w_low, gate_up, gate_b, norm_g):
    bsz, t, d = x.shape
    tc, S, GB = GLA_TC, GLA_SUB, GLA_GROUP
    head_ones = (jnp.arange(2 * GLA_DK)[:, None] // GLA_DK == jnp.arange(2 * GB)[None, :] // GB).astype(BF16)
    nsub = tc // S
    dk, dv = GLA_DK, GLA_DV

    def body(x_ref, ing_ref, w_ref, wl_ref, gu_ref, gbias_ref, ng_ref, ones_ref, out_ref, st_ref, ksh, bsh):
        c = pl.program_id(2)

        @pl.when(c == 0)
        def _():
            st_ref[...] = jnp.zeros_like(st_ref)
            ksh[0:S, :] = jnp.zeros((S, 2 * dk), F32)
            bsh[0:S, :] = jnp.zeros((S, 2 * dk), F32)

        xn = _rms_rows(x_ref[0], ing_ref[...]).astype(BF16)
        p = jnp.dot(xn, w_ref[0], preferred_element_type=F32)
        g_low = jnp.dot(xn, wl_ref[...], preferred_element_type=F32)[:, 0:GLA_RANK]
        z = _dot(g_low, gu_ref[...]) + gbias_ref[...]
        la = _log_sigmoid(z) / GLA_TAU
        rowi = lax.broadcasted_iota(jnp.int32, (tc, 1), 0)
        rmod = rowi & (S - 1)
        bcum = la
        rsum = la
        for s in (1, 2, 4, 8):
            bcum = bcum + jnp.where(rmod >= s, pltpu.roll(bcum, s, 0), 0.0)
            rsum = rsum + jnp.where(rmod < S - s, pltpu.roll(rsum, tc - s, 0), 0.0)
        q = p[:, 0:2 * dk] * (dk ** -0.5)
        k = p[:, 2 * dk:4 * dk]
        v = p[:, 4 * dk:4 * dk + 2 * dv]
        gate = p[:, 4 * dk + 2 * dv:4 * dk + 4 * dv]
        qt = q * jnp.exp(bcum)
        kt = k * jnp.exp(rsum - la)
        eg = jnp.exp(bcum + rsum - la)

        ksh[S:, :] = k
        bsh[S:, :] = bcum
        prods = []
        for d in range(S):
            kd = k if d == 0 else ksh[pl.ds(S - d, tc), :]
            bd = bcum if d == 0 else bsh[pl.ds(S - d, tc), :]
            e = jnp.exp(jnp.where(rmod >= d, bcum - bd, 0.0))
            prods.append((q * kd * e).astype(BF16))
        ws = [jnp.dot(p, ones_ref[...], preferred_element_type=F32) for p in prods]
        coli = lax.broadcasted_iota(jnp.int32, (tc, GB), 1)
        rgrp = rowi & (GB - 1)
        att0 = jnp.zeros((tc, GB), F32)
        att1 = jnp.zeros((tc, GB), F32)
        offs = jnp.where((coli // S) == (rgrp // S), rgrp - coli, -1)
        for d in range(S):
            here = offs == d
            att0 = jnp.where(here, ws[d][:, 0:GB], att0)
            att1 = jnp.where(here, ws[d][:, GB:2 * GB], att1)

        heads = range(2)
        lk = [slice(hh * dk, (hh + 1) * dk) for hh in heads]
        lv = [slice(hh * dv, (hh + 1) * dv) for hh in heads]
        kv = [[_dot_tn(v[si * S:(si + 1) * S, lv[hh]], kt[si * S:(si + 1) * S, lk[hh]]) for hh in heads]
              for si in range(nsub)]
        st = [st_ref[hh] for hh in heads]
        inter = [[], []]
        for si in range(nsub):
            rows = slice(si * S, (si + 1) * S)
            for hh in heads:
                inter[hh].append(_dot_nt(qt[rows, lk[hh]], st[hh]))
                st[hh] = st[hh] * eg[si * S:si * S + 1, lk[hh]] + kv[si][hh]
        for hh in heads:
            st_ref[hh] = st[hh]

        for hh, att in ((0, att0), (1, att1)):
            diag = jnp.concatenate([_dot(att[g * GB:(g + 1) * GB], v[g * GB:(g + 1) * GB, lv[hh]])
                                    for g in range(tc // GB)], axis=0)
            o = diag + jnp.concatenate(inter[hh], axis=0)
            on = o * lax.rsqrt(jnp.mean(o * o, axis=1, keepdims=True) + EPS)
            out_ref[0, :, lv[hh]] = (on * ng_ref[:, lv[hh]] * _silu(gate[:, lv[hh]])).astype(BF16)

    pw = 4 * dk + 4 * dv
    return pl.pallas_call(
        body,
        grid=(bsz, GLA_HEADS // 2, t // tc),
        in_specs=[pl.BlockSpec((1, tc, d), lambda b, h, c: (b, c, 0)),
                  pl.BlockSpec((1, d), lambda b, h, c: (0, 0)),
                  pl.BlockSpec((1, d, pw), lambda b, h, c: (h, 0, 0)),
                  pl.BlockSpec((d, 128), lambda b, h, c: (0, 0)),
                  pl.BlockSpec((GLA_RANK, 2 * dk), lambda b, h, c: (0, h)),
                  pl.BlockSpec((1, 2 * dk), lambda b, h, c: (0, h)),
                  pl.BlockSpec((1, 2 * dv), lambda b, h, c: (0, h)),
                  pl.BlockSpec((2 * dk, 2 * GB), lambda b, h, c: (0, 0))],
        out_specs=pl.BlockSpec((1, tc, 2 * dv), lambda b, h, c: (b, c, h)),
        out_shape=jax.ShapeDtypeStruct((bsz, t, GLA_HEADS * dv), BF16),
        scratch_shapes=[pltpu.VMEM((2, dv, dk), F32), pltpu.VMEM((tc + S, 2 * dk), F32),
                        pltpu.VMEM((tc + S, 2 * dk), F32)],
        compiler_params=_cp("parallel", "parallel", "arbitrary"),
        name="gla",
    )(x, in_g, w_pairs, w_low, gate_up, gate_b, norm_g, head_ones)


RET_TB = 1024
RET_L = 256


def _retention(x, in_g, w_in, cos_t, sin_t, intra, inter, sdec, cdec, norm_g):
    bsz, t, d = x.shape
    TB, L, D, H, W = RET_TB, RET_L, RET_DIM, RET_HEADS, RET_W

    def body(x_ref, ing_ref, w_ref, cos_ref, sin_ref, intra_ref, inter_ref, sdec_ref, cdec_ref, ng_ref,
             out_ref, s_ref):
        c = pl.program_id(1)

        @pl.when(c == 0)
        def _():
            s_ref[...] = jnp.zeros_like(s_ref)

        p = jnp.dot(_rms_rows(x_ref[0], ing_ref[...]).astype(BF16), w_ref[...], preferred_element_type=F32)
        heads = range(H)
        s_st = [s_ref[h] for h in heads]
        for cc in range(TB // L):
            rows = slice(cc * L, (cc + 1) * L)
            cs = cos_ref[rows, :]
            sn = sin_ref[rows, :]

            def rot(z):
                return z * cs + pltpu.roll(z, D // 2, 1) * sn

            q = [rot(p[rows, h * D:(h + 1) * D]) * (D ** -0.5) for h in heads]
            k = [rot(p[rows, W + h * D:W + (h + 1) * D]) for h in heads]
            v = [p[rows, 2 * W + h * D:2 * W + (h + 1) * D] for h in heads]
            s = [_dot_nt(q[h], k[h]) * intra_ref[h] for h in heads]
            qs = [_dot(q[h], s_st[h]) for h in heads]
            kv = [_dot_tn(k[h] * sdec_ref[h], v[h]) for h in heads]
            sv = [_dot(s[h], v[h]) for h in heads]
            for h in heads:
                o = sv[h] + inter_ref[h] * qs[h]
                s_st[h] = cdec_ref[h] * s_st[h] + kv[h]
                oc = o - jnp.mean(o, axis=1, keepdims=True)
                on = oc * lax.rsqrt(jnp.mean(oc * oc, axis=1, keepdims=True) + EPS)
                hl = slice(h * D, (h + 1) * D)
                gate = p[rows, 3 * W + h * D:3 * W + (h + 1) * D]
                out_ref[0, rows, hl] = (on * ng_ref[:, hl] * _silu(gate)).astype(BF16)
        for h in heads:
            s_ref[h] = s_st[h]

    fix2 = lambda b, c: (0, 0)
    fix3 = lambda b, c: (0, 0, 0)
    return pl.pallas_call(
        body,
        grid=(bsz, t // TB),
        in_specs=[pl.BlockSpec((1, TB, d), lambda b, c: (b, c, 0)),
                  pl.BlockSpec((1, d), fix2),
                  pl.BlockSpec((d, 4 * W), fix2),
                  pl.BlockSpec((TB, D), lambda b, c: (c, 0)),
                  pl.BlockSpec((TB, D), lambda b, c: (c, 0)),
                  pl.BlockSpec((H, L, L), fix3),
                  pl.BlockSpec((H, L, 1), fix3),
                  pl.BlockSpec((H, L, 1), fix3),
                  pl.BlockSpec((H, 1, 1), fix3),
                  pl.BlockSpec((1, W), lambda b, c: (0, 0))],
        out_specs=pl.BlockSpec((1, TB, W), lambda b, c: (b, c, 0)),
        out_shape=jax.ShapeDtypeStruct((bsz, t, W), BF16),
        scratch_shapes=[pltpu.VMEM((H, D, D), F32)],
        compiler_params=_cp("parallel", "arbitrary"),
        name="retention",
    )(x, in_g, w_in, cos_t, sin_t, intra, inter, sdec, cdec, norm_g)


def _retention_tables(t):
    L, D = RET_L, RET_DIM
    inv = ROPE_BASE ** (-jnp.arange(0, D, 2, dtype=F32) / D)
    ang = jnp.arange(t).astype(F32)[:, None] * inv[None, :]
    cos = jnp.cos(ang)
    sin = jnp.sin(ang)
    cos_t = jnp.concatenate([cos, cos], axis=-1)
    sin_t = jnp.concatenate([-sin, sin], axis=-1)
    log_gamma = jnp.log1p(-jnp.exp2(-5.0 - jnp.arange(RET_HEADS, dtype=F32)))
    idx = jnp.arange(L, dtype=F32)
    causal = idx[:, None] >= idx[None, :]
    rel = jnp.where(causal, idx[:, None] - idx[None, :], 0.0)
    intra = jnp.where(causal, jnp.exp(log_gamma[:, None, None] * rel), 0.0)
    inter = jnp.exp(log_gamma[:, None] * (idx + 1.0))[:, :, None]
    sdec = jnp.exp(log_gamma[:, None] * (L - 1.0 - idx))[:, :, None]
    cdec = jnp.exp(log_gamma * L)[:, None, None]
    return cos_t, sin_t, intra, inter, sdec, cdec


RWP_TM = 512


def _seg_sum(x, bd):
    hi = x.astype(BF16)
    lo = (x - hi.astype(F32)).astype(BF16)
    return jnp.dot(hi, bd, preferred_element_type=F32) + jnp.dot(lo, bd, preferred_element_type=F32)


def _rwkv_prep(x, in_g, w_in, mu, w_up, w0, a_up, a0, g_up, k_k, k_a, r_k, bd):
    bsz, t, d = x.shape
    tm, W = RWP_TM, RW_W

    def body(x_ref, ing_ref, win_ref, mu_ref, wup_ref, w0_ref, aup_ref, a0_ref, gup_ref, kk_ref, ka_ref, rk_ref,
             bd_ref, r_out, lw_out, k_out, v_out, a_out, b_out, g_out, bonus_out, last_ref):
        c = pl.program_id(1)

        @pl.when(c == 0)
        def _():
            last_ref[...] = jnp.zeros_like(last_ref)

        cur = jnp.dot(_rms_rows(x_ref[0], ing_ref[...]).astype(BF16), win_ref[...], preferred_element_type=F32)
        prev = last_ref[...]
        last_ref[...] = cur[tm - 1:tm, :]
        rowi = lax.broadcasted_iota(jnp.int32, (tm, 1), 0)
        sh = jnp.where(rowi == 0, prev, pltpu.roll(cur, 1, 0))
        xm = cur + (sh - cur) * mu_ref[...]
        x_r = xm[:, 0:W]
        x_k = xm[:, W:2 * W]
        x_v = xm[:, 2 * W:3 * W]
        x_dl = xm[:, 3 * W:3 * W + 64]
        x_al = xm[:, 3 * W + 64:3 * W + 128]
        x_gl = xm[:, 3 * W + 128:3 * W + 256]
        wl = w0_ref[...] + _dot(jnp.tanh(x_dl), wup_ref[...])
        sp = jnp.maximum(-wl, 0.0) + jnp.log(1.0 + jnp.exp(-jnp.abs(wl)))
        lw_out[0] = -jnp.exp(-sp - 0.5)
        a = _sigmoid(a0_ref[...] + _dot(x_al, aup_ref[...]))
        g_out[0] = _dot(_sigmoid(x_gl), gup_ref[...]).astype(BF16)
        kk0 = x_k * kk_ref[...]
        nrm = jnp.sqrt(_seg_sum(kk0 * kk0, bd_ref[...]))
        kk = kk0 / jnp.maximum(nrm, 1e-12)
        k_h = x_k * (1.0 + (a - 1.0) * ka_ref[...])
        r_out[0] = x_r.astype(BF16)
        k_out[0] = k_h.astype(BF16)
        v_out[0] = x_v.astype(BF16)
        a_out[0] = (-kk).astype(BF16)
        b_out[0] = (kk * a).astype(BF16)
        bonus_out[0] = (_dot(x_r * k_h * rk_ref[...], bd_ref[...]) * x_v).astype(BF16)

    row = lambda b, c: (0, 0)
    blk = pl.BlockSpec((1, tm, W), lambda b, c: (b, c, 0))
    shp = [jax.ShapeDtypeStruct((bsz, t, W), F32 if i == 1 else BF16) for i in range(8)]
    return pl.pallas_call(
        body,
        grid=(bsz, t // tm),
        in_specs=[pl.BlockSpec((1, tm, d), lambda b, c: (b, c, 0)),
                  pl.BlockSpec((1, d), row),
                  pl.BlockSpec((d, RW_COLS), row),
                  pl.BlockSpec((1, RW_COLS), row),
                  pl.BlockSpec((64, W), row), pl.BlockSpec((1, W), row),
                  pl.BlockSpec((64, W), row), pl.BlockSpec((1, W), row),
                  pl.BlockSpec((128, W), row),
                  pl.BlockSpec((1, W), row), pl.BlockSpec((1, W), row), pl.BlockSpec((1, W), row),
                  pl.BlockSpec((W, W), row)],
        out_specs=[blk] * 8,
        out_shape=shp,
        scratch_shapes=[pltpu.VMEM((1, RW_COLS), F32)],
        compiler_params=_cp("parallel", "arbitrary"),
        name="rwkv_prep",
    )(x, in_g, w_in, mu, w_up, w0, a_up, a0, g_up, k_k, k_a, r_k, bd)


RW_L = 64
RW_TB = 256


def _rwkv_scan(r, lw, k, v, aa, bb, g_out, bonus, ln_g, ln_b):
    bsz, t, W = r.shape
    L, N, tb = RW_L, RW_DIM, RW_TB
    nck = tb // L

    def body(r_ref, lw_ref, k_ref, v_ref, a_ref, b_ref, g_ref, bonus_ref, lng_ref, lnb_ref, out_ref, h_ref):
        c = pl.program_id(1)

        @pl.when(c == 0)
        def _():
            h_ref[...] = jnp.zeros_like(h_ref)

        lw_all = lw_ref[0]
        rowi = lax.broadcasted_iota(jnp.int32, (tb, 1), 0) & (L - 1)
        cl = lw_all
        for s in (1, 2, 4, 8, 16, 32):
            cl = cl + jnp.where(rowi >= s, pltpu.roll(cl, s, 0), 0.0)
        cl_last = jnp.concatenate([jnp.broadcast_to(cl[(cc + 1) * L - 1:(cc + 1) * L, :], (L, W))
                                   for cc in range(nck)], axis=0)
        e_inv = jnp.exp(-cl)
        e_end = jnp.exp(cl_last - cl)
        p_end = jnp.exp(cl_last)
        at = a_ref[0] * jnp.exp(cl - lw_all)
        rt = r_ref[0] * jnp.exp(cl)
        bt = b_ref[0] * e_inv
        kt = k_ref[0] * e_inv
        b_end = b_ref[0] * e_end
        k_end = k_ref[0] * e_end
        v_all = v_ref[0]
        P2 = 2 * N
        pairs = [(cc, p) for cc in range(nck) for p in range(RW_HEADS // 2)]
        pr = range(len(pairs))

        def pb(z, pi):
            cc, p = pairs[pi]
            return z[cc * L:(cc + 1) * L, p * P2:(p + 1) * P2]

        def stack2(z):
            lane = lax.broadcasted_iota(jnp.int32, (1, z.shape[1]), 1) & (P2 - 1)
            return jnp.concatenate([jnp.where(lane < N, z, 0.0), jnp.where(lane >= N, z, 0.0)], axis=0)

        tcol = lax.broadcasted_iota(jnp.int32, (L, P2), 1) & (N - 1)
        trow = lax.broadcasted_iota(jnp.int32, (L, P2), 0)
        strict = trow > tcol
        lower = trow >= tcol
        r2 = lax.broadcasted_iota(jnp.int32, (P2, P2), 0)
        c2 = lax.broadcasted_iota(jnp.int32, (P2, P2), 1)
        same_head = (r2 // N) == (c2 // N)
        eye2 = r2 == c2
        x2 = [jnp.concatenate([pb(at, pi), pb(rt, pi)], axis=0) for pi in pr]
        m_b = [_dot_nt(x2[pi], stack2(pb(bt, pi))) for pi in pr]
        m_k = [_dot_nt(x2[pi], stack2(pb(kt, pi))) for pi in pr]
        ap = [jnp.where(strict, m_b[pi][0:L], 0.0) for pi in pr]
        a_rb = [jnp.where(lower, m_b[pi][L:2 * L], 0.0) for pi in pr]
        a_ak = [jnp.where(strict, m_k[pi][0:L], 0.0) for pi in pr]
        a_rk = [jnp.where(lower, m_k[pi][L:2 * L], 0.0) for pi in pr]
        vp = [_dot(jnp.concatenate([a_ak[pi], a_rk[pi]], axis=0), stack2(pb(v_all, pi))) for pi in pr]
        kv = [_dot(pb(k_end, pi).T, pb(v_all, pi)) for pi in pr]
        x = [jnp.concatenate([pb(at, pi), vp[pi][0:L]], axis=1) for pi in pr]
        for it in range(6):
            x = [x[pi] + _dot(ap[pi], stack2(x[pi])) for pi in pr]
            if it < 5:
                ap = [_dot(ap[pi], stack2(ap[pi])) for pi in pr]
        post1 = [_dot(a_rb[pi], stack2(x[pi])) for pi in pr]
        post2 = [_dot(pb(b_end, pi).T, x[pi]) for pi in pr]
        lhs = []
        y0s = []
        h_adds = []
        for pi in pr:
            q_hat = pb(rt, pi) + post1[pi][:, 0:P2]
            gmat = (jnp.where(same_head, post2[pi][:, 0:P2], 0.0)
                    + jnp.where(eye2, pb(p_end, pi)[0:1, :], 0.0))
            lhs.append(jnp.concatenate([q_hat, gmat], axis=0))
            y0s.append(post1[pi][:, P2:2 * P2] + vp[pi][L:2 * L])
            h_adds.append(jnp.where(same_head, post2[pi][:, P2:2 * P2] + kv[pi], 0.0))
        npair = RW_HEADS // 2
        h_st = [h_ref[p] for p in range(npair)]
        ys = [None] * len(pairs)
        for cc in range(nck):
            res = [_dot(lhs[cc * npair + p], h_st[p]) for p in range(npair)]
            for p in range(npair):
                pi = cc * npair + p
                ys[pi] = res[p][0:L] + y0s[pi]
                h_st[p] = res[p][L:L + P2] + h_adds[pi]
        for p in range(npair):
            h_ref[p] = h_st[p]
        seg = same_head.astype(BF16)
        mean = [_seg_sum(ys[pi], seg) * (1.0 / N) for pi in pr]
        yc = [ys[pi] - mean[pi] for pi in pr]
        var = [_seg_sum(yc[pi] * yc[pi], seg) * (1.0 / N) for pi in pr]
        for pi, (cc, p) in enumerate(pairs):
            yn = yc[pi] * lax.rsqrt(var[pi] + RW_LN_EPS)
            rows = slice(cc * L, (cc + 1) * L)
            cols = slice(p * P2, (p + 1) * P2)
            out_ref[0, rows, cols] = ((yn * lng_ref[:, cols] + lnb_ref[:, cols] + bonus_ref[0, rows, cols])
                                      * g_ref[0, rows, cols]).astype(BF16)

    blk = pl.BlockSpec((1, tb, W), lambda b, c: (b, c, 0))
    vec = pl.BlockSpec((1, W), lambda b, c: (0, 0))
    return pl.pallas_call(
        body,
        grid=(bsz, t // tb),
        in_specs=[blk] * 8 + [vec, vec],
        out_specs=blk,
        out_shape=jax.ShapeDtypeStruct((bsz, t, W), BF16),
        scratch_shapes=[pltpu.VMEM((RW_HEADS // 2, 2 * N, 2 * N), F32)],
        compiler_params=_cp("parallel", "arbitrary"),
        name="rwkv_scan",
    )(r, lw, k, v, aa, bb, g_out, bonus, ln_g, ln_b)


def _mix_out_route(a, b, w_bf16, resid, g, router_split, tm=1024):
    n, wa = a.shape
    wb = b.shape[1]
    d = w_bf16.shape[1]
    e = router_split.shape[2]

    def body(a_ref, b_ref, w_ref, r_ref, g_ref, rt_ref, h_ref, xn_ref, lg_ref):
        acc = jnp.dot(a_ref[...], w_ref[0:wa, :], preferred_element_type=F32)
        acc = acc + jnp.dot(b_ref[...], w_ref[wa:wa + wb, :], preferred_element_type=F32)
        h = r_ref[...] + acc
        h_ref[...] = h
        xn = _rms_rows(h, g_ref[...])
        x_hi = xn.astype(BF16)
        xn_ref[...] = x_hi
        x_lo = (xn - x_hi.astype(F32)).astype(BF16)
        r_hi = rt_ref[0]
        r_lo = rt_ref[1]
        lg_ref[...] = (jnp.dot(x_hi, r_hi, preferred_element_type=F32)
                       + (jnp.dot(x_lo, r_hi, preferred_element_type=F32)
                          + jnp.dot(x_hi, r_lo, preferred_element_type=F32)))

    return pl.pallas_call(
        body,
        grid=(n // tm,),
        in_specs=[pl.BlockSpec((tm, wa), lambda i: (i, 0)),
                  pl.BlockSpec((tm, wb), lambda i: (i, 0)),
                  pl.BlockSpec((wa + wb, d), lambda i: (0, 0)),
                  pl.BlockSpec((tm, d), lambda i: (i, 0)),
                  pl.BlockSpec((1, d), lambda i: (0, 0)),
                  pl.BlockSpec((2, d, e), lambda i: (0, 0, 0))],
        out_specs=[pl.BlockSpec((tm, d), lambda i: (i, 0)), pl.BlockSpec((tm, d), lambda i: (i, 0)),
                   pl.BlockSpec((tm, e), lambda i: (i, 0))],
        out_shape=[jax.ShapeDtypeStruct((n, d), F32), jax.ShapeDtypeStruct((n, d), BF16),
                   jax.ShapeDtypeStruct((n, e), F32)],
        compiler_params=_cp("parallel"),
        name="mix_out_route",
    )(a, b, w_bf16, resid, g.reshape(1, d), router_split)


MOE_TM = 512
MOE_TF = 1792


def _experts(xs, row_w, item_tile, item_exp, item_lo, item_hi, wg, wu, wd):
    nrows, d = xs.shape
    tm, tf = MOE_TM, MOE_TF
    nf = D_FF // tf
    n_items = item_tile.shape[0]

    def body(it_ref, ie_ref, lo_ref, hi_ref, x_ref, w_ref, wg_ref, wu_ref, wd_ref, o_ref, acc_ref):
        i = pl.program_id(0)
        j = pl.program_id(1)
        tile = it_ref[i]
        first = jnp.logical_or(i == 0, tile != it_ref[jnp.maximum(i - 1, 0)])
        last = jnp.logical_or(i == n_items - 1, tile != it_ref[jnp.minimum(i + 1, n_items - 1)])

        @pl.when(jnp.logical_and(first, j == 0))
        def _():
            acc_ref[...] = jnp.zeros_like(acc_ref)

        lo = lo_ref[i]
        hi = hi_ref[i]

        @pl.when(lo < hi)
        def _():
            x = x_ref[...]
            gg = jnp.dot(x, wg_ref[0], preferred_element_type=F32)
            uu = jnp.dot(x, wu_ref[0], preferred_element_type=F32)
            act = (_silu(gg) * uu).astype(BF16)
            part = jnp.dot(act, wd_ref[0], preferred_element_type=F32)
            rowi = lax.broadcasted_iota(jnp.int32, (tm, 1), 0)
            mine = jnp.logical_and(rowi >= lo, rowi < hi)
            acc_ref[...] += part * jnp.where(mine, w_ref[...], 0.0)

        @pl.when(jnp.logical_and(last, j == nf - 1))
        def _():
            o_ref[...] = acc_ref[...].astype(o_ref.dtype)

    grid_spec = pltpu.PrefetchScalarGridSpec(
        num_scalar_prefetch=4,
        grid=(n_items, nf),
        in_specs=[pl.BlockSpec((tm, d), lambda i, j, it, ie, lo, hi: (it[i], 0)),
                  pl.BlockSpec((tm, 1), lambda i, j, it, ie, lo, hi: (it[i], 0)),
                  pl.BlockSpec((1, d, tf), lambda i, j, it, ie, lo, hi: (ie[i], 0, j)),
                  pl.BlockSpec((1, d, tf), lambda i, j, it, ie, lo, hi: (ie[i], 0, j)),
                  pl.BlockSpec((1, tf, d), lambda i, j, it, ie, lo, hi: (ie[i], j, 0))],
        out_specs=pl.BlockSpec((tm, d), lambda i, j, it, ie, lo, hi: (it[i], 0)),
        scratch_shapes=[pltpu.VMEM((tm, d), F32)],
    )
    return pl.pallas_call(
        body,
        grid_spec=grid_spec,
        out_shape=jax.ShapeDtypeStruct((nrows, d), BF16),
        compiler_params=_cp("arbitrary", "arbitrary"),
        name="moe_experts",
    )(item_tile, item_exp, item_lo, item_hi, xs, row_w, wg, wu, wd)


def _combine_norm(h, y0, y1, g, tm=512):
    n, d = h.shape

    def body(h_ref, a_ref, b_ref, g_ref, o_ref):
        o_ref[...] = _rms_rows(h_ref[...] + (a_ref[...].astype(F32) + b_ref[...].astype(F32)), g_ref[...])

    blk = pl.BlockSpec((tm, d), lambda i: (i, 0))
    return pl.pallas_call(
        body,
        grid=(n // tm,),
        in_specs=[blk, blk, blk, pl.BlockSpec((1, d), lambda i: (0, 0))],
        out_specs=blk,
        out_shape=jax.ShapeDtypeStruct((n, d), F32),
        compiler_params=_cp("parallel"),
        name="combine_norm",
    )(h, y0, y1, g.reshape(1, d))


def _route(logits, n):
    tm = MOE_TM
    na = n * TOP_K
    n_tiles = na // tm
    top_val, top_idx = lax.top_k(logits, TOP_K)
    top_w = jax.nn.softmax(top_val, axis=-1)
    e_flat = top_idx.reshape(-1).astype(jnp.int32)
    w_flat = top_w.reshape(-1)
    tok = jnp.arange(na, dtype=jnp.int32) // TOP_K
    _, sorted_tok, sorted_w = lax.sort((e_flat, tok, w_flat), num_keys=1, is_stable=True)
    onehot = (e_flat[:, None] == jnp.arange(N_EXPERTS, dtype=jnp.int32)[None, :]).astype(jnp.int32)
    rank = jnp.take_along_axis(jnp.cumsum(onehot, axis=0), e_flat[:, None], axis=1)[:, 0] - 1
    counts = jnp.sum(onehot, axis=0)
    ends = jnp.cumsum(counts)
    pos = (ends - counts)[e_flat] + rank
    cuts = jnp.sort(jnp.concatenate([jnp.arange(n_tiles, dtype=jnp.int32) * tm, ends[:-1].astype(jnp.int32)]))
    nxt = jnp.concatenate([cuts[1:], jnp.full((1,), na, jnp.int32)])
    item_tile = jnp.minimum(cuts // tm, n_tiles - 1)
    item_exp = jnp.minimum(jnp.searchsorted(ends, cuts, side="right"), N_EXPERTS - 1).astype(jnp.int32)
    item_lo = cuts - item_tile * tm
    item_hi = nxt - item_tile * tm
    return sorted_tok, sorted_w, (item_tile, item_exp, item_lo, item_hi), pos.reshape(n, TOP_K)


def kernel(x, e_norm1_g, e_w_in, e_ml_conv_w, e_ml_conv_b, e_ml_gate_b, e_ml_norm_g, e_gla_gate_up, e_gla_gate_b,
           e_gla_norm_g, e_w_out, e_norm2_g, e_ffn_w_gate, e_ffn_w_up, e_ffn_w_down, o_norm1_g, o_w_in,
           o_ret_norm_g, o_rw_mu, o_rw_w_up, o_rw_w0, o_rw_a_up, o_rw_a0, o_rw_g_up, o_rw_k_k, o_rw_k_a, o_rw_r_k,
           o_rw_ln_g, o_rw_ln_b, o_w_out, o_norm2_g, o_moe_router, o_moe_w_gate, o_moe_w_up, o_moe_w_down,
           final_norm_g):
    bsz, t, d = x.shape
    n = bsz * t
    h0 = x.reshape(n, d)

    w = e_w_in[0]
    row = lambda a: a.reshape(1, -1)
    x3 = x
    ng1 = row(e_norm1_g[0])
    w_if = w[:, 2048:2056]
    w_gate = jnp.zeros((d, 128), F32).at[:, :2 * ML_HEADS].set(w_if).astype(BF16)
    w_gate_t = jnp.zeros((16, d), F32).at[:2 * ML_HEADS, :].set(w_if.T).astype(BF16)
    h_ml = _mlstm(x3, ng1, w[:, :4 * ML_W].astype(BF16), w_gate, w_gate_t, row(e_ml_gate_b[0]),
                  e_ml_gate_b[0].reshape(-1, 1), e_ml_conv_w[0], row(e_ml_conv_b[0]), row(e_ml_norm_g[0]))
    gq, gk, gv, gr = 2056, 2312, 2568, 3080
    w_pairs = jnp.stack([jnp.concatenate([w[:, gq + 128 * hp:gq + 128 * (hp + 1)],
                                          w[:, gk + 128 * hp:gk + 128 * (hp + 1)],
                                          w[:, gv + 256 * hp:gv + 256 * (hp + 1)],
                                          w[:, gr + 256 * hp:gr + 256 * (hp + 1)]], axis=1)
                         for hp in range(GLA_HEADS // 2)]).astype(BF16)
    w_low = jnp.zeros((d, 128), F32).at[:, :GLA_RANK].set(w[:, 3592:3608]).astype(BF16)
    o_gla = _gla(x3, ng1, w_pairs, w_low, e_gla_gate_up[0].astype(BF16), row(e_gla_gate_b[0]),
                 row(e_gla_norm_g[0]))
    h2 = _mix_out_ffn(h_ml.reshape(n, -1), o_gla.reshape(n, -1), e_w_out[0].astype(BF16), h0, e_norm2_g[0],
                      e_ffn_w_gate[0].astype(BF16), e_ffn_w_up[0].astype(BF16), e_ffn_w_down[0].astype(BF16))

    w = o_w_in[0]
    h2_3 = h2.reshape(bsz, t, d)
    ng2 = row(o_norm1_g[0])
    y_ret = _retention(h2_3, ng2, w[:, :4 * RET_W].astype(BF16), *_retention_tables(t), row(o_ret_norm_g[0]))
    head_of = jnp.arange(RW_W) // RW_DIM
    bd = (head_of[:, None] == head_of[None, :]).astype(BF16)
    r, lw, k, v, aa, bb, g_out, bonus = _rwkv_prep(
        h2_3, ng2, w[:, 4 * RET_W:].astype(BF16), row(o_rw_mu[0]), o_rw_w_up[0].astype(BF16), row(o_rw_w0[0]),
        o_rw_a_up[0].astype(BF16), row(o_rw_a0[0]), o_rw_g_up[0].astype(BF16), row(o_rw_k_k[0]),
        row(o_rw_k_a[0]), row(o_rw_r_k[0]), bd)
    y_rw = _rwkv_scan(r, lw, k, v, aa, bb, g_out, bonus, row(o_rw_ln_g[0]), row(o_rw_ln_b[0]))
    router_pad = jnp.zeros((d, 128), F32).at[:, :N_EXPERTS].set(o_moe_router[0])
    router_hi = router_pad.astype(BF16)
    router_split = jnp.stack([router_hi, (router_pad - router_hi.astype(F32)).astype(BF16)])
    h3, xn, logits = _mix_out_route(y_ret.reshape(n, -1), y_rw.reshape(n, -1), o_w_out[0].astype(BF16), h2,
                                    o_norm2_g[0], router_split)
    sorted_tok, sorted_w, items, pos = _route(logits[:, :N_EXPERTS], n)
    xs = xn.at[sorted_tok].get(mode="promise_in_bounds")
    ys = _experts(xs, sorted_w.reshape(-1, 1), *items, o_moe_w_gate[0].astype(BF16),
                  o_moe_w_up[0].astype(BF16), o_moe_w_down[0].astype(BF16))
    y0 = ys.at[pos[:, 0]].get(mode="promise_in_bounds")
    y1 = ys.at[pos[:, 1]].get(mode="promise_in_bounds")
    out = _combine_norm(h3, y0, y1, final_norm_g)
    return out.reshape(bsz, t, d)
```

```python
import functools

import numpy as np
import jax
import jax.numpy as jnp
from jax import lax
from jax.experimental import pallas as pl
from jax.experimental.pallas import tpu as pltpu

F32 = jnp.float32
BF16 = jnp.bfloat16

D_MODEL = 1024
EPS = 1e-6
ML_HEADS, ML_DIM, ML_W, ML_CONV = 4, 128, 512, 4
GLA_HEADS, GLA_DK, GLA_DV, GLA_RANK, GLA_TAU = 4, 64, 128, 16, 16.0
RET_HEADS, RET_DIM, RET_W = 4, 128, 512
ROPE_BASE = 10000.0
RW_HEADS, RW_DIM, RW_W = 8, 64, 512
RW_COLS = 1792
RW_LN_EPS = 64e-5
D_FF = 3584
N_EXPERTS = 8
TOP_K = 2

VMEM_LIMIT = 48 * 1024 * 1024
NEG = -1e30


def _cp(*sem):
    return pltpu.CompilerParams(dimension_semantics=sem, vmem_limit_bytes=VMEM_LIMIT)


def _sigmoid(x):
    return 1.0 / (1.0 + jnp.exp(-x))


def _silu(x):
    return x * _sigmoid(x)


def _log_sigmoid(x):
    return jnp.minimum(x, 0.0) - jnp.log(1.0 + jnp.exp(-jnp.abs(x)))


def _dot(a, b):
    return jnp.dot(a.astype(BF16), b.astype(BF16), preferred_element_type=F32)


def _dot_nt(a, b):
    return lax.dot_general(a.astype(BF16), b.astype(BF16), (((1,), (1,)), ((), ())), preferred_element_type=F32)


def _dot_tn(a, b):
    return jnp.dot(a.T.astype(BF16), b.astype(BF16), preferred_element_type=F32)


def _rms_rows(x, g):
    ms = jnp.mean(x * x, axis=-1, keepdims=True)
    return x * lax.rsqrt(ms + EPS) * g


def _mix_out_ffn(a, b, w_out, resid, g, wg, wu, wd, tm=512, tf=1792):
    n, d = resid.shape
    wa = a.shape[1]
    wb = b.shape[1]
    f = wg.shape[1]
    nf = f // tf

    def body(a_ref, b_ref, wo_ref, r_ref, g_ref, wg_ref, wu_ref, wd_ref, o_ref, xn_ref, acc_ref):
        j = pl.program_id(1)

        @pl.when(j == 0)
        def _():
            h = r_ref[...] + jnp.dot(a_ref[...], wo_ref[0:wa, :], preferred_element_type=F32)
            h = h + jnp.dot(b_ref[...], wo_ref[wa:wa + wb, :], preferred_element_type=F32)
            xn_ref[...] = _rms_rows(h, g_ref[...]).astype(BF16)
            acc_ref[...] = h

        xn = xn_ref[...]
        gg = jnp.dot(xn, wg_ref[...], preferred_element_type=F32)
        uu = jnp.dot(xn, wu_ref[...], preferred_element_type=F32)
        act = (_silu(gg) * uu).astype(BF16)
        acc_ref[...] += jnp.dot(act, wd_ref[...], preferred_element_type=F32)

        @pl.when(j == nf - 1)
        def _():
            o_ref[...] = acc_ref[...]

    return pl.pallas_call(
        body,
        grid=(n // tm, nf),
        in_specs=[pl.BlockSpec((tm, wa), lambda i, j: (i, 0)),
                  pl.BlockSpec((tm, wb), lambda i, j: (i, 0)),
                  pl.BlockSpec((wa + wb, d), lambda i, j: (0, 0)),
                  pl.BlockSpec((tm, d), lambda i, j: (i, 0)),
                  pl.BlockSpec((1, d), lambda i, j: (0, 0)),
                  pl.BlockSpec((d, tf), lambda i, j: (0, j)),
                  pl.BlockSpec((d, tf), lambda i, j: (0, j)),
                  pl.BlockSpec((tf, d), lambda i, j: (j, 0))],
        out_specs=pl.BlockSpec((tm, d), lambda i, j: (i, 0)),
        out_shape=jax.ShapeDtypeStruct((n, d), F32),
        scratch_shapes=[pltpu.VMEM((tm, d), BF16), pltpu.VMEM((tm, d), F32)],
        compiler_params=_cp("parallel", "arbitrary"),
        name="mix_out_ffn",
    )(a, b, w_out, resid, g.reshape(1, d), wg, wu, wd)


ML_TB = 1024
ML_L = 256


def _mlstm(x, in_g, w_in, w_gate, w_gate_t, gate_b_row, gate_b_col, conv_w, conv_b, norm_g):
    bsz, t, d = x.shape
    TB, L, D, H, W = ML_TB, ML_L, ML_DIM, ML_HEADS, ML_W

    def body(x_ref, ing_ref, w_ref, wg_ref, wgt_ref, gbr_ref, gbc_ref, cwq_ref, cwk_ref, cbq_ref, cbk_ref,
             ng_ref, out_ref, qext, kext, c_ref, n_ref, m_ref):
        c = pl.program_id(1)

        @pl.when(c == 0)
        def _():
            qext[0:8, :] = jnp.zeros((8, W), F32)
            kext[0:8, :] = jnp.zeros((8, W), F32)
            c_ref[...] = jnp.zeros_like(c_ref)
            n_ref[...] = jnp.zeros_like(n_ref)
            m_ref[...] = jnp.zeros_like(m_ref)

        xn = _rms_rows(x_ref[0], ing_ref[...]).astype(BF16)
        p = jnp.dot(xn, w_ref[...], preferred_element_type=F32)
        g_cols = jnp.dot(xn, wg_ref[...], preferred_element_type=F32)[:, 0:2 * H]
        g_rows = lax.dot_general(wgt_ref[...], xn, (((1,), (1,)), ((), ())),
                                 preferred_element_type=F32)[0:2 * H, :]
        qext[8:, :] = p[:, 0:W]
        kext[8:, :] = p[:, W:2 * W]

        def conv(ext, cw_ref, cb_ref):
            acc = cb_ref[...] + cw_ref[0:1, :] * ext[pl.ds(8 - ML_CONV + 1, TB), :]
            for kk in range(1, ML_CONV):
                acc = acc + cw_ref[kk:kk + 1, :] * ext[pl.ds(8 - ML_CONV + 1 + kk, TB), :]
            return _silu(acc)

        q_all = conv(qext, cwq_ref, cbq_ref) * (D ** -0.5)
        k_all = conv(kext, cwk_ref, cbk_ref)
        qext[0:8, :] = qext[TB:TB + 8, :]
        kext[0:8, :] = kext[TB:TB + 8, :]

        gcol = g_cols + gbr_ref[...]
        grow = g_rows + gbc_ref[...]
        fcol = _log_sigmoid(gcol[:, H:2 * H])
        frow = _log_sigmoid(grow[H:2 * H, :])
        ri = lax.broadcasted_iota(jnp.int32, (L, L), 0)
        ci = lax.broadcasted_iota(jnp.int32, (L, L), 1)
        causal = ri >= ci
        heads = range(H)
        c_st = [c_ref[h] for h in heads]
        n_st = [n_ref[h] for h in heads]
        m_st = [m_ref[h] for h in heads]
        for cc in range(TB // L):
            rows = slice(cc * L, (cc + 1) * L)
            hs = lambda z, h: z[rows, h * D:(h + 1) * D]
            w_intra, w_inter, w_state, carry, m_row = [], [], [], [], []
            for h in heads:
                f_row = frow[h:h + 1, rows]
                i_row = grow[h:h + 1, rows]
                f_col = fcol[rows, h:h + 1]
                i_col = gcol[rows, h:h + 1]
                b_col = jnp.sum(jnp.where(causal, f_row, 0.0), axis=1, keepdims=True)
                b_row = jnp.sum(jnp.where(ri <= ci, f_col, 0.0), axis=0, keepdims=True)
                g_tot = jnp.sum(f_row, axis=1, keepdims=True)
                d_intra = jnp.where(causal, b_col - b_row + i_row, NEG)
                d_inter = b_col + m_st[h]
                mr = jnp.maximum(d_inter, jnp.max(d_intra, axis=1, keepdims=True))
                w_intra.append(jnp.exp(d_intra - mr))
                w_inter.append(jnp.exp(d_inter - mr))
                m_row.append(mr)
                d_state = g_tot - b_col + i_col
                mn = jnp.maximum(g_tot + m_st[h], jnp.max(d_state, axis=0, keepdims=True))
                w_state.append(jnp.exp(d_state - mn))
                carry.append(jnp.exp(g_tot + m_st[h] - mn))
                m_st[h] = mn
            qh = [hs(q_all, h) for h in heads]
            kh = [hs(k_all, h) for h in heads]
            vh = [p[rows, 2 * W + h * D:2 * W + (h + 1) * D] for h in heads]
            s = [_dot_nt(qh[h], kh[h]) * w_intra[h] for h in heads]
            qc = [_dot(qh[h], c_st[h]) for h in heads]
            kw = [kh[h] * w_state[h] for h in heads]
            kv = [_dot_tn(kw[h], vh[h]) for h in heads]
            sv = [_dot(s[h], vh[h]) for h in heads]
            for h in heads:
                num = sv[h] + w_inter[h] * qc[h]
                den = (jnp.sum(s[h], axis=1, keepdims=True)
                       + w_inter[h] * jnp.sum(qh[h] * n_st[h], axis=1, keepdims=True))
                hval = num / jnp.maximum(jnp.abs(den), jnp.exp(-m_row[h]))
                c_st[h] = carry[h] * c_st[h] + kv[h]
                n_st[h] = carry[h] * n_st[h] + jnp.sum(kw[h], axis=0, keepdims=True)
                hg = _sigmoid(p[rows, 3 * W + h * D:3 * W + (h + 1) * D]) * hval
                hc = hg - jnp.mean(hg, axis=1, keepdims=True)
                hn = hc * lax.rsqrt(jnp.mean(hc * hc, axis=1, keepdims=True) + EPS)
                out_ref[0, rows, h * D:(h + 1) * D] = (hn * ng_ref[:, h * D:(h + 1) * D]).astype(BF16)
        for h in heads:
            c_ref[h] = c_st[h]
            n_ref[h] = n_st[h]
            m_ref[h] = m_st[h]

    fix = lambda j: (lambda b, c: (0, j))
    return pl.pallas_call(
        body,
        grid=(bsz, t // TB),
        in_specs=[pl.BlockSpec((1, TB, d), lambda b, c: (b, c, 0)),
                  pl.BlockSpec((1, d), fix(0)),
                  pl.BlockSpec((d, 4 * W), fix(0)),
                  pl.BlockSpec((d, 128), fix(0)),
                  pl.BlockSpec((16, d), fix(0)),
                  pl.BlockSpec((1, 2 * H), fix(0)),
                  pl.BlockSpec((2 * H, 1), fix(0)),
                  pl.BlockSpec((ML_CONV, W), fix(0)),
                  pl.BlockSpec((ML_CONV, W), fix(1)),
                  pl.BlockSpec((1, W), fix(0)),
                  pl.BlockSpec((1, W), fix(1)),
                  pl.BlockSpec((1, W), fix(0))],
        out_specs=pl.BlockSpec((1, TB, W), lambda b, c: (b, c, 0)),
        out_shape=jax.ShapeDtypeStruct((bsz, t, W), BF16),
        scratch_shapes=[pltpu.VMEM((TB + 8, W), F32), pltpu.VMEM((TB + 8, W), F32),
                        pltpu.VMEM((H, D, D), F32), pltpu.VMEM((H, 1, D), F32), pltpu.VMEM((H, 1, 1), F32)],
        compiler_params=_cp("parallel", "arbitrary"),
        name="mlstm",
    )(x, in_g, w_in, w_gate, w_gate_t, gate_b_row, gate_b_col, conv_w, conv_w, conv_b, conv_b, norm_g)


GLA_TC = 1024
GLA_SUB = 16
GLA_GROUP = 128


def _gla(x, in_g, w_pairs, w_low, gate_up, gate_b, norm_g):
    bsz, t, d = x.shape
    tc, S, GB = GLA_TC, GLA_SUB, GLA_GROUP
    head_ones = (jnp.arange(2 * GLA_DK)[:, None] // GLA_DK == jnp.arange(2 * GB)[None, :] // GB).astype(BF16)
    nsub = tc // S
    dk, dv = GLA_DK, GLA_DV

    def body(x_ref, ing_ref, w_ref, wl_ref, gu_ref, gbias_ref, ng_ref, ones_ref, out_ref, st_ref, ksh, bsh):
        c = pl.program_id(2)

        @pl.when(c == 0)
        def _():
            st_ref[...] = jnp.zeros_like(st_ref)
            ksh[0:S, :] = jnp.zeros((S, 2 * dk), F32)
            bsh[0:S, :] = jnp.zeros((S, 2 * dk), F32)

        xn = _rms_rows(x_ref[0], ing_ref[...]).astype(BF16)
        p = jnp.dot(xn, w_ref[0], preferred_element_type=F32)
        g_low = jnp.dot(xn, wl_ref[...], preferred_element_type=F32)[:, 0:GLA_RANK]
        z = _dot(g_low, gu_ref[...]) + gbias_ref[...]
        la = _log_sigmoid(z) / GLA_TAU
        rowi = lax.broadcasted_iota(jnp.int32, (tc, 1), 0)
        rmod = rowi & (S - 1)
        bcum = la
        rsum = la
        for s in (1, 2, 4, 8):
            bcum = bcum + jnp.where(rmod >= s, pltpu.roll(bcum, s, 0), 0.0)
            rsum = rsum + jnp.where(rmod < S - s, pltpu.roll(rsum, tc - s, 0), 0.0)
        q = p[:, 0:2 * dk] * (dk ** -0.5)
        k = p[:, 2 * dk:4 * dk]
        v = p[:, 4 * dk:4 * dk + 2 * dv]
        gate = p[:, 4 * dk + 2 * dv:4 * dk + 4 * dv]
        qt = q * jnp.exp(bcum)
        kt = k * jnp.exp(rsum - la)
        eg = jnp.exp(bcum + rsum - la)

        ksh[S:, :] = k
        bsh[S:, :] = bcum
        prods = []
        for d in range(S):
            kd = k if d == 0 else ksh[pl.ds(S - d, tc), :]
            bd = bcum if d == 0 else bsh[pl.ds(S - d, tc), :]
            e = jnp.exp(jnp.where(rmod >= d, bcum - bd, 0.0))
            prods.append((q * kd * e).astype(BF16))
        ws = [jnp.dot(p, ones_ref[...], preferred_element_type=F32) for p in prods]
        coli = lax.broadcasted_iota(jnp.int32, (tc, GB), 1)
        rgrp = rowi & (GB - 1)
        att0 = jnp.zeros((tc, GB), F32)
        att1 = jnp.zeros((tc, GB), F32)
        offs = jnp.where((coli // S) == (rgrp // S), rgrp - coli, -1)
        for d in range(S):
            here = offs == d
            att0 = jnp.where(here, ws[d][:, 0:GB], att0)
            att1 = jnp.where(here, ws[d][:, GB:2 * GB], att1)

        heads = range(2)
        lk = [slice(hh * dk, (hh + 1) * dk) for hh in heads]
        lv = [slice(hh * dv, (hh + 1) * dv) for hh in heads]
        kv = [[_dot_tn(v[si * S:(si + 1) * S, lv[hh]], kt[si * S:(si + 1) * S, lk[hh]]) for hh in heads]
              for si in range(nsub)]
        st = [st_ref[hh] for hh in heads]
        inter = [[], []]
        for si in range(nsub):
            rows = slice(si * S, (si + 1) * S)
            for hh in heads:
                inter[hh].append(_dot_nt(qt[rows, lk[hh]], st[hh]))
                st[hh] = st[hh] * eg[si * S:si * S + 1, lk[hh]] + kv[si][hh]
        for hh in heads:
            st_ref[hh] = st[hh]

        for hh, att in ((0, att0), (1, att1)):
            diag = jnp.concatenate([_dot(att[g * GB:(g + 1) * GB], v[g * GB:(g + 1) * GB, lv[hh]])
                                    for g in range(tc // GB)], axis=0)
            o = diag + jnp.concatenate(inter[hh], axis=0)
            on = o * lax.rsqrt(jnp.mean(o * o, axis=1, keepdims=True) + EPS)
            out_ref[0, :, lv[hh]] = (on * ng_ref[:, lv[hh]] * _silu(gate[:, lv[hh]])).astype(BF16)

    pw = 4 * dk + 4 * dv
    return pl.pallas_call(
        body,
        grid=(bsz, GLA_HEADS // 2, t // tc),
        in_specs=[pl.BlockSpec((1, tc, d), lambda b, h, c: (b, c, 0)),
                  pl.BlockSpec((1, d), lambda b, h, c: (0, 0)),
                  pl.BlockSpec((1, d, pw), lambda b, h, c: (h, 0, 0)),
                  pl.BlockSpec((d, 128), lambda b, h, c: (0, 0)),
                  pl.BlockSpec((GLA_RANK, 2 * dk), lambda b, h, c: (0, h)),
                  pl.BlockSpec((1, 2 * dk), lambda b, h, c: (0, h)),
                  pl.BlockSpec((1, 2 * dv), lambda b, h, c: (0, h)),
                  pl.BlockSpec((2 * dk, 2 * GB), lambda b, h, c: (0, 0))],
        out_specs=pl.BlockSpec((1, tc, 2 * dv), lambda b, h, c: (b, c, h)),
        out_shape=jax.ShapeDtypeStruct((bsz, t, GLA_HEADS * dv), BF16),
        scratch_shapes=[pltpu.VMEM((2, dv, dk), F32), pltpu.VMEM((tc + S, 2 * dk), F32),
                        pltpu.VMEM((tc + S, 2 * dk), F32)],
        compiler_params=_cp("parallel", "parallel", "arbitrary"),
        name="gla",
    )(x, in_g, w_pairs, w_low, gate_up, gate_b, norm_g, head_ones)


RET_TB = 1024
RET_L = 256


def _retention(x, in_g, w_in, cos_t, sin_t, intra, inter, sdec, cdec, norm_g):
    bsz, t, d = x.shape
    TB, L, D, H, W = RET_TB, RET_L, RET_DIM, RET_HEADS, RET_W

    def body(x_ref, ing_ref, w_ref, cos_ref, sin_ref, intra_ref, inter_ref, sdec_ref, cdec_ref, ng_ref,
             out_ref, s_ref):
        c = pl.program_id(1)

        @pl.when(c == 0)
        def _():
            s_ref[...] = jnp.zeros_like(s_ref)

        p = jnp.dot(_rms_rows(x_ref[0], ing_ref[...]).astype(BF16), w_ref[...], preferred_element_type=F32)
        heads = range(H)
        s_st = [s_ref[h] for h in heads]
        for cc in range(TB // L):
            rows = slice(cc * L, (cc + 1) * L)
            cs = cos_ref[rows, :]
            sn = sin_ref[rows, :]

            def rot(z):
                return z * cs + pltpu.roll(z, D // 2, 1) * sn

            q = [rot(p[rows, h * D:(h + 1) * D]) * (D ** -0.5) for h in heads]
            k = [rot(p[rows, W + h * D:W + (h + 1) * D]) for h in heads]
            v = [p[rows, 2 * W + h * D:2 * W + (h + 1) * D] for h in heads]
            s = [_dot_nt(q[h], k[h]) * intra_ref[h] for h in heads]
            qs = [_dot(q[h], s_st[h]) for h in heads]
            kv = [_dot_tn(k[h] * sdec_ref[h], v[h]) for h in heads]
            sv = [_dot(s[h], v[h]) for h in heads]
            for h in heads:
                o = sv[h] + inter_ref[h] * qs[h]
                s_st[h] = cdec_ref[h] * s_st[h] + kv[h]
                oc = o - jnp.mean(o, axis=1, keepdims=True)
                on = oc * lax.rsqrt(jnp.mean(oc * oc, axis=1, keepdims=True) + EPS)
                hl = slice(h * D, (h + 1) * D)
                gate = p[rows, 3 * W + h * D:3 * W + (h + 1) * D]
                out_ref[0, rows, hl] = (on * ng_ref[:, hl] * _silu(gate)).astype(BF16)
        for h in heads:
            s_ref[h] = s_st[h]

    fix2 = lambda b, c: (0, 0)
    fix3 = lambda b, c: (0, 0, 0)
    return pl.pallas_call(
        body,
        grid=(bsz, t // TB),
        in_specs=[pl.BlockSpec((1, TB, d), lambda b, c: (b, c, 0)),
                  pl.BlockSpec((1, d), fix2),
                  pl.BlockSpec((d, 4 * W), fix2),
                  pl.BlockSpec((TB, D), lambda b, c: (c, 0)),
                  pl.BlockSpec((TB, D), lambda b, c: (c, 0)),
                  pl.BlockSpec((H, L, L), fix3),
                  pl.BlockSpec((H, L, 1), fix3),
                  pl.BlockSpec((H, L, 1), fix3),
                  pl.BlockSpec((H, 1, 1), fix3),
                  pl.BlockSpec((1, W), lambda b, c: (0, 0))],
        out_specs=pl.BlockSpec((1, TB, W), lambda b, c: (b, c, 0)),
        out_shape=jax.ShapeDtypeStruct((bsz, t, W), BF16),
        scratch_shapes=[pltpu.VMEM((H, D, D), F32)],
        compiler_params=_cp("parallel", "arbitrary"),
        name="retention",
    )(x, in_g, w_in, cos_t, sin_t, intra, inter, sdec, cdec, norm_g)


def _retention_tables(t):
    L, D = RET_L, RET_DIM
    inv = ROPE_BASE ** (-jnp.arange(0, D, 2, dtype=F32) / D)
    ang = jnp.arange(t).astype(F32)[:, None] * inv[None, :]
    cos = jnp.cos(ang)
    sin = jnp.sin(ang)
    cos_t = jnp.concatenate([cos, cos], axis=-1)
    sin_t = jnp.concatenate([-sin, sin], axis=-1)
    log_gamma = jnp.log1p(-jnp.exp2(-5.0 - jnp.arange(RET_HEADS, dtype=F32)))
    idx = jnp.arange(L, dtype=F32)
    causal = idx[:, None] >= idx[None, :]
    rel = jnp.where(causal, idx[:, None] - idx[None, :], 0.0)
    intra = jnp.where(causal, jnp.exp(log_gamma[:, None, None] * rel), 0.0)
    inter = jnp.exp(log_gamma[:, None] * (idx + 1.0))[:, :, None]
    sdec = jnp.exp(log_gamma[:, None] * (L - 1.0 - idx))[:, :, None]
    cdec = jnp.exp(log_gamma * L)[:, None, None]
    return cos_t, sin_t, intra, inter, sdec, cdec


RWP_TM = 512


def _seg_sum(x, bd):
    hi = x.astype(BF16)
    lo = (x - hi.astype(F32)).astype(BF16)
    return jnp.dot(hi, bd, preferred_element_type=F32) + jnp.dot(lo, bd, preferred_element_type=F32)


def _rwkv_prep(x, in_g, w_in, mu, w_up, w0, a_up, a0, g_up, k_k, k_a, r_k, bd):
    bsz, t, d = x.shape
    tm, W = RWP_TM, RW_W

    def body(x_ref, ing_ref, win_ref, mu_ref, wup_ref, w0_ref, aup_ref, a0_ref, gup_ref, kk_ref, ka_ref, rk_ref,
             bd_ref, r_out, lw_out, k_out, v_out, a_out, b_out, g_out, bonus_out, last_ref):
        c = pl.program_id(1)

        @pl.when(c == 0)
        def _():
            last_ref[...] = jnp.zeros_like(last_ref)

        cur = jnp.dot(_rms_rows(x_ref[0], ing_ref[...]).astype(BF16), win_ref[...], preferred_element_type=F32)
        prev = last_ref[...]
        last_ref[...] = cur[tm - 1:tm, :]
        rowi = lax.broadcasted_iota(jnp.int32, (tm, 1), 0)
        sh = jnp.where(rowi == 0, prev, pltpu.roll(cur, 1, 0))
        xm = cur + (sh - cur) * mu_ref[...]
        x_r = xm[:, 0:W]
        x_k = xm[:, W:2 * W]
        x_v = xm[:, 2 * W:3 * W]
        x_dl = xm[:, 3 * W:3 * W + 64]
        x_al = xm[:, 3 * W + 64:3 * W + 128]
        x_gl = xm[:, 3 * W + 128:3 * W + 256]
        wl = w0_ref[...] + _dot(jnp.tanh(x_dl), wup_ref[...])
        sp = jnp.maximum(-wl, 0.0) + jnp.log(1.0 + jnp.exp(-jnp.abs(wl)))
        lw_out[0] = -jnp.exp(-sp - 0.5)
        a = _sigmoid(a0_ref[...] + _dot(x_al, aup_ref[...]))
        g_out[0] = _dot(_sigmoid(x_gl), gup_ref[...]).astype(BF16)
        kk0 = x_k * kk_ref[...]
        nrm = jnp.sqrt(_seg_sum(kk0 * kk0, bd_ref[...]))
        kk = kk0 / jnp.maximum(nrm, 1e-12)
        k_h = x_k * (1.0 + (a - 1.0) * ka_ref[...])
        r_out[0] = x_r.astype(BF16)
        k_out[0] = k_h.astype(BF16)
        v_out[0] = x_v.astype(BF16)
        a_out[0] = (-kk).astype(BF16)
        b_out[0] = (kk * a).astype(BF16)
        bonus_out[0] = (_dot(x_r * k_h * rk_ref[...], bd_ref[...]) * x_v).astype(BF16)

    row = lambda b, c: (0, 0)
    blk = pl.BlockSpec((1, tm, W), lambda b, c: (b, c, 0))
    shp = [jax.ShapeDtypeStruct((bsz, t, W), F32 if i == 1 else BF16) for i in range(8)]
    return pl.pallas_call(
        body,
        grid=(bsz, t // tm),
        in_specs=[pl.BlockSpec((1, tm, d), lambda b, c: (b, c, 0)),
                  pl.BlockSpec((1, d), row),
                  pl.BlockSpec((d, RW_COLS), row),
                  pl.BlockSpec((1, RW_COLS), row),
                  pl.BlockSpec((64, W), row), pl.BlockSpec((1, W), row),
                  pl.BlockSpec((64, W), row), pl.BlockSpec((1, W), row),
                  pl.BlockSpec((128, W), row),
                  pl.BlockSpec((1, W), row), pl.BlockSpec((1, W), row), pl.BlockSpec((1, W), row),
                  pl.BlockSpec((W, W), row)],
        out_specs=[blk] * 8,
        out_shape=shp,
        scratch_shapes=[pltpu.VMEM((1, RW_COLS), F32)],
        compiler_params=_cp("parallel", "arbitrary"),
        name="rwkv_prep",
    )(x, in_g, w_in, mu, w_up, w0, a_up, a0, g_up, k_k, k_a, r_k, bd)


RW_L = 64
RW_TB = 256


def _rwkv_scan(r, lw, k, v, aa, bb, g_out, bonus, ln_g, ln_b):
    bsz, t, W = r.shape
    L, N, tb = RW_L, RW_DIM, RW_TB
    nck = tb // L

    def body(r_ref, lw_ref, k_ref, v_ref, a_ref, b_ref, g_ref, bonus_ref, lng_ref, lnb_ref, out_ref, h_ref):
        c = pl.program_id(1)

        @pl.when(c == 0)
        def _():
            h_ref[...] = jnp.zeros_like(h_ref)

        lw_all = lw_ref[0]
        rowi = lax.broadcasted_iota(jnp.int32, (tb, 1), 0) & (L - 1)
        cl = lw_all
        for s in (1, 2, 4, 8, 16, 32):
            cl = cl + jnp.where(rowi >= s, pltpu.roll(cl, s, 0), 0.0)
        cl_last = jnp.concatenate([jnp.broadcast_to(cl[(cc + 1) * L - 1:(cc + 1) * L, :], (L, W))
                                   for cc in range(nck)], axis=0)
        e_inv = jnp.exp(-cl)
        e_end = jnp.exp(cl_last - cl)
        p_end = jnp.exp(cl_last)
        at = a_ref[0] * jnp.exp(cl - lw_all)
        rt = r_ref[0] * jnp.exp(cl)
        bt = b_ref[0] * e_inv
        kt = k_ref[0] * e_inv
        b_end = b_ref[0] * e_end
        k_end = k_ref[0] * e_end
        v_all = v_ref[0]
        P2 = 2 * N
        pairs = [(cc, p) for cc in range(nck) for p in range(RW_HEADS // 2)]
        pr = range(len(pairs))

        def pb(z, pi):
            cc, p = pairs[pi]
            return z[cc * L:(cc + 1) * L, p * P2:(p + 1) * P2]

        def stack2(z):
            lane = lax.broadcasted_iota(jnp.int32, (1, z.shape[1]), 1) & (P2 - 1)
            return jnp.concatenate([jnp.where(lane < N, z, 0.0), jnp.where(lane >= N, z, 0.0)], axis=0)

        tcol = lax.broadcasted_iota(jnp.int32, (L, P2), 1) & (N - 1)
        trow = lax.broadcasted_iota(jnp.int32, (L, P2), 0)
        strict = trow > tcol
        lower = trow >= tcol
        r2 = lax.broadcasted_iota(jnp.int32, (P2, P2), 0)
        c2 = lax.broadcasted_iota(jnp.int32, (P2, P2), 1)
        same_head = (r2 // N) == (c2 // N)
        eye2 = r2 == c2
        x2 = [jnp.concatenate([pb(at, pi), pb(rt, pi)], axis=0) for pi in pr]
        m_b = [_dot_nt(x2[pi], stack2(pb(bt, pi))) for pi in pr]
        m_k = [_dot_nt(x2[pi], stack2(pb(kt, pi))) for pi in pr]
        ap = [jnp.where(strict, m_b[pi][0:L], 0.0) for pi in pr]
        a_rb = [jnp.where(lower, m_b[pi][L:2 * L], 0.0) for pi in pr]
        a_ak = [jnp.where(strict, m_k[pi][0:L], 0.0) for pi in pr]
        a_rk = [jnp.where(lower, m_k[pi][L:2 * L], 0.0) for pi in pr]
        vp = [_dot(jnp.concatenate([a_ak[pi], a_rk[pi]], axis=0), stack2(pb(v_all, pi))) for pi in pr]
        kv = [_dot(pb(k_end, pi).T, pb(v_all, pi)) for pi in pr]
        x = [jnp.concatenate([pb(at, pi), vp[pi][0:L]], axis=1) for pi in pr]
        for it in range(6):
            x = [x[pi] + _dot(ap[pi], stack2(x[pi])) for pi in pr]
            if it < 5:
                ap = [_dot(ap[pi], stack2(ap[pi])) for pi in pr]
        post1 = [_dot(a_rb[pi], stack2(x[pi])) for pi in pr]
        post2 = [_dot(pb(b_end, pi).T, x[pi]) for pi in pr]
        lhs = []
        y0s = []
        h_adds = []
        for pi in pr:
            q_hat = pb(rt, pi) + post1[pi][:, 0:P2]
            gmat = (jnp.where(same_head, post2[pi][:, 0:P2], 0.0)
                    + jnp.where(eye2, pb(p_end, pi)[0:1, :], 0.0))
            lhs.append(jnp.concatenate([q_hat, gmat], axis=0))
            y0s.append(post1[pi][:, P2:2 * P2] + vp[pi][L:2 * L])
            h_adds.append(jnp.where(same_head, post2[pi][:, P2:2 * P2] + kv[pi], 0.0))
        npair = RW_HEADS // 2
        h_st = [h_ref[p] for p in range(npair)]
        ys = [None] * len(pairs)
        for cc in range(nck):
            res = [_dot(lhs[cc * npair + p], h_st[p]) for p in range(npair)]
            for p in range(npair):
                pi = cc * npair + p
                ys[pi] = res[p][0:L] + y0s[pi]
                h_st[p] = res[p][L:L + P2] + h_adds[pi]
        for p in range(npair):
            h_ref[p] = h_st[p]
        seg = same_head.astype(BF16)
        mean = [_seg_sum(ys[pi], seg) * (1.0 / N) for pi in pr]
        yc = [ys[pi] - mean[pi] for pi in pr]
        var = [_seg_sum(yc[pi] * yc[pi], seg) * (1.0 / N) for pi in pr]
        for pi, (cc, p) in enumerate(pairs):
            yn = yc[pi] * lax.rsqrt(var[pi] + RW_LN_EPS)
            rows = slice(cc * L, (cc + 1) * L)
            cols = slice(p * P2, (p + 1) * P2)
            out_ref[0, rows, cols] = ((yn * lng_ref[:, cols] + lnb_ref[:, cols] + bonus_ref[0, rows, cols])
                                      * g_ref[0, rows, cols]).astype(BF16)

    blk = pl.BlockSpec((1, tb, W), lambda b, c: (b, c, 0))
    vec = pl.BlockSpec((1, W), lambda b, c: (0, 0))
    return pl.pallas_call(
        body,
        grid=(bsz, t // tb),
        in_specs=[blk] * 8 + [vec, vec],
        out_specs=blk,
        out_shape=jax.ShapeDtypeStruct((bsz, t, W), BF16),
        scratch_shapes=[pltpu.VMEM((RW_HEADS // 2, 2 * N, 2 * N), F32)],
        compiler_params=_cp("parallel", "arbitrary"),
        name="rwkv_scan",
    )(r, lw, k, v, aa, bb, g_out, bonus, ln_g, ln_b)


def _mix_out_route(a, b, w_bf16, resid, g, router_split, tm=1024):
    n, wa = a.shape
    wb = b.shape[1]
    d = w_bf16.shape[1]
    e = router_split.shape[1] // 2

    def body(a_ref, b_ref, w_ref, r_ref, g_ref, rt_ref, h_ref, xn_ref, lg_ref):
        acc = jnp.dot(a_ref[...], w_ref[0:wa, :], preferred_element_type=F32)
        acc = acc + jnp.dot(b_ref[...], w_ref[wa:wa + wb, :], preferred_element_type=F32)
        h = r_ref[...] + acc
        h_ref[...] = h
        xn = _rms_rows(h, g_ref[...])
        x_hi = xn.astype(BF16)
        xn_ref[...] = x_hi
        x_lo = (xn - x_hi.astype(F32)).astype(BF16)
        hi_both = jnp.dot(x_hi, rt_ref[...], preferred_element_type=F32)
        lo_hi = jnp.dot(x_lo, rt_ref[:, 0:e], preferred_element_type=F32)
        lg_ref[...] = hi_both[:, 0:e] + (lo_hi + hi_both[:, e:2 * e])

    return pl.pallas_call(
        body,
        grid=(n // tm,),
        in_specs=[pl.BlockSpec((tm, wa), lambda i: (i, 0)),
                  pl.BlockSpec((tm, wb), lambda i: (i, 0)),
                  pl.BlockSpec((wa + wb, d), lambda i: (0, 0)),
                  pl.BlockSpec((tm, d), lambda i: (i, 0)),
                  pl.BlockSpec((1, d), lambda i: (0, 0)),
                  pl.BlockSpec((d, 2 * e), lambda i: (0, 0))],
        out_specs=[pl.BlockSpec((tm, d), lambda i: (i, 0)), pl.BlockSpec((tm, d), lambda i: (i, 0)),
                   pl.BlockSpec((tm, e), lambda i: (i, 0))],
        out_shape=[jax.ShapeDtypeStruct((n, d), F32), jax.ShapeDtypeStruct((n, d), BF16),
                   jax.ShapeDtypeStruct((n, e), F32)],
        compiler_params=_cp("parallel"),
        name="mix_out_route",
    )(a, b, w_bf16, resid, g.reshape(1, d), router_split)


MOE_TM = 512
MOE_TF = 1792


def _experts(xs, row_w, item_tile, item_exp, item_lo, item_hi, wg, wu, wd):
    nrows, d = xs.shape
    tm, tf = MOE_TM, MOE_TF
    nf = D_FF // tf
    n_items = item_tile.shape[0]

    def body(it_ref, ie_ref, lo_ref, hi_ref, x_ref, w_ref, wg_ref, wu_ref, wd_ref, o_ref, acc_ref):
        i = pl.program_id(0)
        j = pl.program_id(1)
        tile = it_ref[i]
        first = jnp.logical_or(i == 0, tile != it_ref[jnp.maximum(i - 1, 0)])
        last = jnp.logical_or(i == n_items - 1, tile != it_ref[jnp.minimum(i + 1, n_items - 1)])

        @pl.when(jnp.logical_and(first, j == 0))
        def _():
            acc_ref[...] = jnp.zeros_like(acc_ref)

        lo = lo_ref[i]
        hi = hi_ref[i]

        @pl.when(lo < hi)
        def _():
            x = x_ref[...]
            gg = jnp.dot(x, wg_ref[0], preferred_element_type=F32)
            uu = jnp.dot(x, wu_ref[0], preferred_element_type=F32)
            act = (_silu(gg) * uu).astype(BF16)
            part = jnp.dot(act, wd_ref[0], preferred_element_type=F32)
            rowi = lax.broadcasted_iota(jnp.int32, (tm, 1), 0)
            mine = jnp.logical_and(rowi >= lo, rowi < hi)
            acc_ref[...] += part * jnp.where(mine, w_ref[...], 0.0)

        @pl.when(jnp.logical_and(last, j == nf - 1))
        def _():
            o_ref[...] = acc_ref[...].astype(o_ref.dtype)

    grid_spec = pltpu.PrefetchScalarGridSpec(
        num_scalar_prefetch=4,
        grid=(n_items, nf),
        in_specs=[pl.BlockSpec((tm, d), lambda i, j, it, ie, lo, hi: (it[i], 0)),
                  pl.BlockSpec((tm, 1), lambda i, j, it, ie, lo, hi: (it[i], 0)),
                  pl.BlockSpec((1, d, tf), lambda i, j, it, ie, lo, hi: (ie[i], 0, j)),
                  pl.BlockSpec((1, d, tf), lambda i, j, it, ie, lo, hi: (ie[i], 0, j)),
                  pl.BlockSpec((1, tf, d), lambda i, j, it, ie, lo, hi: (ie[i], j, 0))],
        out_specs=pl.BlockSpec((tm, d), lambda i, j, it, ie, lo, hi: (it[i], 0)),
        scratch_shapes=[pltpu.VMEM((tm, d), F32)],
    )
    return pl.pallas_call(
        body,
        grid_spec=grid_spec,
        out_shape=jax.ShapeDtypeStruct((nrows, d), BF16),
        compiler_params=_cp("arbitrary", "arbitrary"),
        name="moe_experts",
    )(item_tile, item_exp, item_lo, item_hi, xs, row_w, wg, wu, wd)


def _combine_norm(h, y0, y1, g, tm=512):
    n, d = h.shape

    def body(h_ref, a_ref, b_ref, g_ref, o_ref):
        o_ref[...] = _rms_rows(h_ref[...] + (a_ref[...].astype(F32) + b_ref[...].astype(F32)), g_ref[...])

    blk = pl.BlockSpec((tm, d), lambda i: (i, 0))
    return pl.pallas_call(
        body,
        grid=(n // tm,),
        in_specs=[blk, blk, blk, pl.BlockSpec((1, d), lambda i: (0, 0))],
        out_specs=blk,
        out_shape=jax.ShapeDtypeStruct((n, d), F32),
        compiler_params=_cp("parallel"),
        name="combine_norm",
    )(h, y0, y1, g.reshape(1, d))


def _route(logits, n):
    tm = MOE_TM
    na = n * TOP_K
    n_tiles = na // tm
    top_val, top_idx = lax.top_k(logits, TOP_K)
    top_w = jax.nn.softmax(top_val, axis=-1)
    e_flat = top_idx.reshape(-1).astype(jnp.int32)
    w_flat = top_w.reshape(-1)
    tok = jnp.arange(na, dtype=jnp.int32) // TOP_K
    _, sorted_tok, sorted_w = lax.sort((e_flat, tok, w_flat), num_keys=1, is_stable=True)
    onehot = (e_flat[:, None] == jnp.arange(N_EXPERTS, dtype=jnp.int32)[None, :]).astype(jnp.int32)
    rank = jnp.take_along_axis(jnp.cumsum(onehot, axis=0), e_flat[:, None], axis=1)[:, 0] - 1
    counts = jnp.sum(onehot, axis=0)
    ends = jnp.cumsum(counts)
    pos = (ends - counts)[e_flat] + rank
    cuts = jnp.sort(jnp.concatenate([jnp.arange(n_tiles, dtype=jnp.int32) * tm, ends[:-1].astype(jnp.int32)]))
    nxt = jnp.concatenate([cuts[1:], jnp.full((1,), na, jnp.int32)])
    item_tile = jnp.minimum(cuts // tm, n_tiles - 1)
    item_exp = jnp.minimum(jnp.searchsorted(ends, cuts, side="right"), N_EXPERTS - 1).astype(jnp.int32)
    item_lo = cuts - item_tile * tm
    item_hi = nxt - item_tile * tm
    return sorted_tok, sorted_w, (item_tile, item_exp, item_lo, item_hi), pos.reshape(n, TOP_K)


def kernel(x, e_norm1_g, e_w_in, e_ml_conv_w, e_ml_conv_b, e_ml_gate_b, e_ml_norm_g, e_gla_gate_up, e_gla_gate_b,
           e_gla_norm_g, e_w_out, e_norm2_g, e_ffn_w_gate, e_ffn_w_up, e_ffn_w_down, o_norm1_g, o_w_in,
           o_ret_norm_g, o_rw_mu, o_rw_w_up, o_rw_w0, o_rw_a_up, o_rw_a0, o_rw_g_up, o_rw_k_k, o_rw_k_a, o_rw_r_k,
           o_rw_ln_g, o_rw_ln_b, o_w_out, o_norm2_g, o_moe_router, o_moe_w_gate, o_moe_w_up, o_moe_w_down,
           final_norm_g):
    bsz, t, d = x.shape
    n = bsz * t
    h0 = x.reshape(n, d)

    w = e_w_in[0]
    row = lambda a: a.reshape(1, -1)
    x3 = x
    ng1 = row(e_norm1_g[0])
    w_if = w[:, 2048:2056]
    w_gate = jnp.zeros((d, 128), F32).at[:, :2 * ML_HEADS].set(w_if).astype(BF16)
    w_gate_t = jnp.zeros((16, d), F32).at[:2 * ML_HEADS, :].set(w_if.T).astype(BF16)
    h_ml = _mlstm(x3, ng1, w[:, :4 * ML_W].astype(BF16), w_gate, w_gate_t, row(e_ml_gate_b[0]),
                  e_ml_gate_b[0].reshape(-1, 1), e_ml_conv_w[0], row(e_ml_conv_b[0]), row(e_ml_norm_g[0]))
    gq, gk, gv, gr = 2056, 2312, 2568, 3080
    w_pairs = jnp.stack([jnp.concatenate([w[:, gq + 128 * hp:gq + 128 * (hp + 1)],
                                          w[:, gk + 128 * hp:gk + 128 * (hp + 1)],
                                          w[:, gv + 256 * hp:gv + 256 * (hp + 1)],
                                          w[:, gr + 256 * hp:gr + 256 * (hp + 1)]], axis=1)
                         for hp in range(GLA_HEADS // 2)]).astype(BF16)
    w_low = jnp.zeros((d, 128), F32).at[:, :GLA_RANK].set(w[:, 3592:3608]).astype(BF16)
    o_gla = _gla(x3, ng1, w_pairs, w_low, e_gla_gate_up[0].astype(BF16), row(e_gla_gate_b[0]),
                 row(e_gla_norm_g[0]))
    h2 = _mix_out_ffn(h_ml.reshape(n, -1), o_gla.reshape(n, -1), e_w_out[0].astype(BF16), h0, e_norm2_g[0],
                      e_ffn_w_gate[0].astype(BF16), e_ffn_w_up[0].astype(BF16), e_ffn_w_down[0].astype(BF16))

    w = o_w_in[0]
    h2_3 = h2.reshape(bsz, t, d)
    ng2 = row(o_norm1_g[0])
    y_ret = _retention(h2_3, ng2, w[:, :4 * RET_W].astype(BF16), *_retention_tables(t), row(o_ret_norm_g[0]))
    head_of = jnp.arange(RW_W) // RW_DIM
    bd = (head_of[:, None] == head_of[None, :]).astype(BF16)
    r, lw, k, v, aa, bb, g_out, bonus = _rwkv_prep(
        h2_3, ng2, w[:, 4 * RET_W:].astype(BF16), row(o_rw_mu[0]), o_rw_w_up[0].astype(BF16), row(o_rw_w0[0]),
        o_rw_a_up[0].astype(BF16), row(o_rw_a0[0]), o_rw_g_up[0].astype(BF16), row(o_rw_k_k[0]),
        row(o_rw_k_a[0]), row(o_rw_r_k[0]), bd)
    y_rw = _rwkv_scan(r, lw, k, v, aa, bb, g_out, bonus, row(o_rw_ln_g[0]), row(o_rw_ln_b[0]))
    router_pad = jnp.zeros((d, 128), F32).at[:, :N_EXPERTS].set(o_moe_router[0])
    router_hi = router_pad.astype(BF16)
    router_split = jnp.concatenate([router_hi, (router_pad - router_hi.astype(F32)).astype(BF16)], axis=1)
    h3, xn, logits = _mix_out_route(y_ret.reshape(n, -1), y_rw.reshape(n, -1), o_w_out[0].astype(BF16), h2,
                                    o_norm2_g[0], router_split)
    sorted_tok, sorted_w, items, pos = _route(logits[:, :N_EXPERTS], n)
    xs = xn.at[sorted_tok].get(mode="promise_in_bounds")
    ys = _experts(xs, sorted_w.reshape(-1, 1), *items, o_moe_w_gate[0].astype(BF16),
                  o_moe_w_up[0].astype(BF16), o_moe_w_down[0].astype(BF16))
    y0 = ys.at[pos[:, 0]].get(mode="promise_in_bounds")
    y1 = ys.at[pos[:, 1]].get(mode="promise_in_bounds")
    out = _combine_norm(h3, y0, y1, final_norm_g)
    return out.reshape(bsz, t, d)
```

```python
import functools

import numpy as np
import jax
import jax.numpy as jnp
from jax import lax
from jax.experimental import pallas as pl
from jax.experimental.pallas import tpu as pltpu

F32 = jnp.float32
BF16 = jnp.bfloat16

D_MODEL = 1024
EPS = 1e-6
ML_HEADS, ML_DIM, ML_W, ML_CONV = 4, 128, 512, 4
GLA_HEADS, GLA_DK, GLA_DV, GLA_RANK, GLA_TAU = 4, 64, 128, 16, 16.0
RET_HEADS, RET_DIM, RET_W = 4, 128, 512
ROPE_BASE = 10000.0
RW_HEADS, RW_DIM, RW_W = 8, 64, 512
RW_COLS = 1792
RW_LN_EPS = 64e-5
D_FF = 3584
N_EXPERTS = 8
TOP_K = 2

VMEM_LIMIT = 48 * 1024 * 1024
NEG = -1e30


def _cp(*sem):
    return pltpu.CompilerParams(dimension_semantics=sem, vmem_limit_bytes=VMEM_LIMIT)


def _sigmoid(x):
    return 1.0 / (1.0 + jnp.exp(-x))


def _silu(x):
    return x * _sigmoid(x)


def _log_sigmoid(x):
    return jnp.minimum(x, 0.0) - jnp.log(1.0 + jnp.exp(-jnp.abs(x)))


def _dot(a, b):
    return jnp.dot(a.astype(BF16), b.astype(BF16), preferred_element_type=F32)


def _dot_nt(a, b):
    return lax.dot_general(a.astype(BF16), b.astype(BF16), (((1,), (1,)), ((), ())), preferred_element_type=F32)


def _dot_tn(a, b):
    return jnp.dot(a.T.astype(BF16), b.astype(BF16), preferred_element_type=F32)


def _rms_rows(x, g):
    ms = jnp.mean(x * x, axis=-1, keepdims=True)
    return x * lax.rsqrt(ms + EPS) * g


def _mix_out_ffn(a, b, w_out, resid, g, wg, wu, wd, tm=512, tf=1792):
    n, d = resid.shape
    wa = a.shape[1]
    wb = b.shape[1]
    f = wg.shape[1]
    nf = f // tf

    def body(a_ref, b_ref, wo_ref, r_ref, g_ref, wg_ref, wu_ref, wd_ref, o_ref, xn_ref, acc_ref):
        j = pl.program_id(1)

        @pl.when(j == 0)
        def _():
            h = r_ref[...] + jnp.dot(a_ref[...], wo_ref[0:wa, :], preferred_element_type=F32)
            h = h + jnp.dot(b_ref[...], wo_ref[wa:wa + wb, :], preferred_element_type=F32)
            xn_ref[...] = _rms_rows(h, g_ref[...]).astype(BF16)
            acc_ref[...] = h

        xn = xn_ref[...]
        gg = jnp.dot(xn, wg_ref[...], preferred_element_type=F32)
        uu = jnp.dot(xn, wu_ref[...], preferred_element_type=F32)
        act = (_silu(gg) * uu).astype(BF16)
        acc_ref[...] += jnp.dot(act, wd_ref[...], preferred_element_type=F32)

        @pl.when(j == nf - 1)
        def _():
            o_ref[...] = acc_ref[...]

    return pl.pallas_call(
        body,
        grid=(n // tm, nf),
        in_specs=[pl.BlockSpec((tm, wa), lambda i, j: (i, 0)),
                  pl.BlockSpec((tm, wb), lambda i, j: (i, 0)),
                  pl.BlockSpec((wa + wb, d), lambda i, j: (0, 0)),
                  pl.BlockSpec((tm, d), lambda i, j: (i, 0)),
                  pl.BlockSpec((1, d), lambda i, j: (0, 0)),
                  pl.BlockSpec((d, tf), lambda i, j: (0, j)),
                  pl.BlockSpec((d, tf), lambda i, j: (0, j)),
                  pl.BlockSpec((tf, d), lambda i, j: (j, 0))],
        out_specs=pl.BlockSpec((tm, d), lambda i, j: (i, 0)),
        out_shape=jax.ShapeDtypeStruct((n, d), F32),
        scratch_shapes=[pltpu.VMEM((tm, d), BF16), pltpu.VMEM((tm, d), F32)],
        compiler_params=_cp("parallel", "arbitrary"),
        name="mix_out_ffn",
    )(a, b, w_out, resid, g.reshape(1, d), wg, wu, wd)


ML_TB = 1024
ML_L = 256


def _mlstm(x, in_g, w_in, w_gate, w_gate_t, gate_b_row, gate_b_col, conv_w, conv_b, norm_g):
    bsz, t, d = x.shape
    TB, L, D, H, W = ML_TB, ML_L, ML_DIM, ML_HEADS, ML_W

    def body(x_ref, ing_ref, w_ref, wg_ref, wgt_ref, gbr_ref, gbc_ref, cwq_ref, cwk_ref, cbq_ref, cbk_ref,
             ng_ref, out_ref, qext, kext, c_ref, n_ref, m_ref):
        c = pl.program_id(1)

        @pl.when(c == 0)
        def _():
            qext[0:8, :] = jnp.zeros((8, W), F32)
            kext[0:8, :] = jnp.zeros((8, W), F32)
            c_ref[...] = jnp.zeros_like(c_ref)
            n_ref[...] = jnp.zeros_like(n_ref)
            m_ref[...] = jnp.zeros_like(m_ref)

        def project(cc):
            xn = _rms_rows(x_ref[0, cc * L:(cc + 1) * L, :], ing_ref[...]).astype(BF16)
            pc = jnp.dot(xn, w_ref[...], preferred_element_type=F32)
            gc = jnp.dot(xn, wg_ref[...], preferred_element_type=F32)[:, 0:2 * H]
            gr = lax.dot_general(wgt_ref[...], xn, (((1,), (1,)), ((), ())),
                                 preferred_element_type=F32)[0:2 * H, :]
            return pc, gc, gr

        def conv(ext, base, cw_ref, cb_ref):
            acc = cb_ref[...] + cw_ref[0:1, :] * ext[pl.ds(base + 8 - ML_CONV + 1, L), :]
            for kk in range(1, ML_CONV):
                acc = acc + cw_ref[kk:kk + 1, :] * ext[pl.ds(base + 8 - ML_CONV + 1 + kk, L), :]
            return _silu(acc)

        ri = lax.broadcasted_iota(jnp.int32, (L, L), 0)
        ci = lax.broadcasted_iota(jnp.int32, (L, L), 1)
        causal = ri >= ci
        heads = range(H)
        c_st = [c_ref[h] for h in heads]
        n_st = [n_ref[h] for h in heads]
        m_st = [m_ref[h] for h in heads]
        nxt = project(0)
        for cc in range(TB // L):
            rows = slice(cc * L, (cc + 1) * L)
            p, g_cols, g_rows = nxt
            if cc + 1 < TB // L:
                nxt = project(cc + 1)
            qext[8 + cc * L:8 + (cc + 1) * L, :] = p[:, 0:W]
            kext[8 + cc * L:8 + (cc + 1) * L, :] = p[:, W:2 * W]
            q_all = conv(qext, cc * L, cwq_ref, cbq_ref) * (D ** -0.5)
            k_all = conv(kext, cc * L, cwk_ref, cbk_ref)
            gcol = g_cols + gbr_ref[...]
            grow = g_rows + gbc_ref[...]
            fcol = _log_sigmoid(gcol[:, H:2 * H])
            frow = _log_sigmoid(grow[H:2 * H, :])
            hs = lambda z, h: z[:, h * D:(h + 1) * D]
            w_intra, w_inter, w_state, carry, m_row = [], [], [], [], []
            for h in heads:
                f_row = frow[h:h + 1, :]
                i_row = grow[h:h + 1, :]
                f_col = fcol[:, h:h + 1]
                i_col = gcol[:, h:h + 1]
                b_col = jnp.sum(jnp.where(causal, f_row, 0.0), axis=1, keepdims=True)
                b_row = jnp.sum(jnp.where(ri <= ci, f_col, 0.0), axis=0, keepdims=True)
                g_tot = jnp.sum(f_row, axis=1, keepdims=True)
                d_intra = jnp.where(causal, b_col - b_row + i_row, NEG)
                d_inter = b_col + m_st[h]
                mr = jnp.maximum(d_inter, jnp.max(d_intra, axis=1, keepdims=True))
                w_intra.append(jnp.exp(d_intra - mr))
                w_inter.append(jnp.exp(d_inter - mr))
                m_row.append(mr)
                d_state = g_tot - b_col + i_col
                mn = jnp.maximum(g_tot + m_st[h], jnp.max(d_state, axis=0, keepdims=True))
                w_state.append(jnp.exp(d_state - mn))
                carry.append(jnp.exp(g_tot + m_st[h] - mn))
                m_st[h] = mn
            qh = [hs(q_all, h) for h in heads]
            kh = [hs(k_all, h) for h in heads]
            vh = [p[:, 2 * W + h * D:2 * W + (h + 1) * D] for h in heads]
            s = [_dot_nt(qh[h], kh[h]) * w_intra[h] for h in heads]
            qc = [_dot(qh[h], c_st[h]) for h in heads]
            kw = [kh[h] * w_state[h] for h in heads]
            kv = [_dot_tn(kw[h], vh[h]) for h in heads]
            sv = [_dot(s[h], vh[h]) for h in heads]
            for h in heads:
                num = sv[h] + w_inter[h] * qc[h]
                den = (jnp.sum(s[h], axis=1, keepdims=True)
                       + w_inter[h] * jnp.sum(qh[h] * n_st[h], axis=1, keepdims=True))
                hval = num / jnp.maximum(jnp.abs(den), jnp.exp(-m_row[h]))
                c_st[h] = carry[h] * c_st[h] + kv[h]
                n_st[h] = carry[h] * n_st[h] + jnp.sum(kw[h], axis=0, keepdims=True)
                hg = _sigmoid(p[:, 3 * W + h * D:3 * W + (h + 1) * D]) * hval
                hc = hg - jnp.mean(hg, axis=1, keepdims=True)
                hn = hc * lax.rsqrt(jnp.mean(hc * hc, axis=1, keepdims=True) + EPS)
                out_ref[0, rows, h * D:(h + 1) * D] = (hn * ng_ref[:, h * D:(h + 1) * D]).astype(BF16)
        for h in heads:
            c_ref[h] = c_st[h]
            n_ref[h] = n_st[h]
            m_ref[h] = m_st[h]
        qext[0:8, :] = qext[TB:TB + 8, :]
        kext[0:8, :] = kext[TB:TB + 8, :]

    fix = lambda j: (lambda b, c: (0, j))
    return pl.pallas_call(
        body,
        grid=(bsz, t // TB),
        in_specs=[pl.BlockSpec((1, TB, d), lambda b, c: (b, c, 0)),
                  pl.BlockSpec((1, d), fix(0)),
                  pl.BlockSpec((d, 4 * W), fix(0)),
                  pl.BlockSpec((d, 128), fix(0)),
                  pl.BlockSpec((16, d), fix(0)),
                  pl.BlockSpec((1, 2 * H), fix(0)),
                  pl.BlockSpec((2 * H, 1), fix(0)),
                  pl.BlockSpec((ML_CONV, W), fix(0)),
                  pl.BlockSpec((ML_CONV, W), fix(1)),
                  pl.BlockSpec((1, W), fix(0)),
                  pl.BlockSpec((1, W), fix(1)),
                  pl.BlockSpec((1, W), fix(0))],
        out_specs=pl.BlockSpec((1, TB, W), lambda b, c: (b, c, 0)),
        out_shape=jax.ShapeDtypeStruct((bsz, t, W), BF16),
        scratch_shapes=[pltpu.VMEM((TB + 8, W), F32), pltpu.VMEM((TB + 8, W), F32),
                        pltpu.VMEM((H, D, D), F32), pltpu.VMEM((H, 1, D), F32), pltpu.VMEM((H, 1, 1), F32)],
        compiler_params=_cp("parallel", "arbitrary"),
        name="mlstm",
    )(x, in_g, w_in, w_gate, w_gate_t, gate_b_row, gate_b_col, conv_w, conv_w, conv_b, conv_b, norm_g)


GLA_TC = 1024
GLA_SUB = 16
GLA_GROUP = 128


def _gla(x, in_g, w_pairs, w_low, gate_up, gate_b, norm_g):
    bsz, t, d = x.shape
    tc, S, GB = GLA_TC, GLA_SUB, GLA_GROUP
    head_ones = (jnp.arange(2 * GLA_DK)[:, None] // GLA_DK == jnp.arange(2 * GB)[None, :] // GB).astype(BF16)
    nsub = tc // S
    dk, dv = GLA_DK, GLA_DV

    def body(x_ref, ing_ref, w_ref, wl_ref, gu_ref, gbias_ref, ng_ref, ones_ref, out_ref, st_ref, ksh, bsh):
        c = pl.program_id(2)

        @pl.when(c == 0)
        def _():
            st_ref[...] = jnp.zeros_like(st_ref)
            ksh[0:S, :] = jnp.zeros((S, 2 * dk), F32)
            bsh[0:S, :] = jnp.zeros((S, 2 * dk), F32)

        xn = _rms_rows(x_ref[0], ing_ref[...]).astype(BF16)
        p = jnp.dot(xn, w_ref[0], preferred_element_type=F32)
        g_low = jnp.dot(xn, wl_ref[...], preferred_element_type=F32)[:, 0:GLA_RANK]
        z = _dot(g_low, gu_ref[...]) + gbias_ref[...]
        la = _log_sigmoid(z) / GLA_TAU
        rowi = lax.broadcasted_iota(jnp.int32, (tc, 1), 0)
        rmod = rowi & (S - 1)
        bcum = la
        rsum = la
        for s in (1, 2, 4, 8):
            bcum = bcum + jnp.where(rmod >= s, pltpu.roll(bcum, s, 0), 0.0)
            rsum = rsum + jnp.where(rmod < S - s, pltpu.roll(rsum, tc - s, 0), 0.0)
        q = p[:, 0:2 * dk] * (dk ** -0.5)
        k = p[:, 2 * dk:4 * dk]
        v = p[:, 4 * dk:4 * dk + 2 * dv]
        gate = p[:, 4 * dk + 2 * dv:4 * dk + 4 * dv]
        qt = q * jnp.exp(bcum)
        kt = k * jnp.exp(rsum - la)
        eg = jnp.exp(bcum + rsum - la)

        ksh[S:, :] = k
        bsh[S:, :] = bcum
        prods = []
        for d in range(S):
            kd = k if d == 0 else ksh[pl.ds(S - d, tc), :]
            bd = bcum if d == 0 else bsh[pl.ds(S - d, tc), :]
            e = jnp.exp(jnp.where(rmod >= d, bcum - bd, 0.0))
            prods.append((q * kd * e).astype(BF16))
        ws = [jnp.dot(p, ones_ref[...], preferred_element_type=F32) for p in prods]
        coli = lax.broadcasted_iota(jnp.int32, (tc, GB), 1)
        rgrp = rowi & (GB - 1)
        att0 = jnp.zeros((tc, GB), F32)
        att1 = jnp.zeros((tc, GB), F32)
        offs = jnp.where((coli // S) == (rgrp // S), rgrp - coli, -1)
        for d in range(S):
            here = offs == d
            att0 = jnp.where(here, ws[d][:, 0:GB], att0)
            att1 = jnp.where(here, ws[d][:, GB:2 * GB], att1)

        heads = range(2)
        lk = [slice(hh * dk, (hh + 1) * dk) for hh in heads]
        lv = [slice(hh * dv, (hh + 1) * dv) for hh in heads]
        kv = [[_dot_tn(v[si * S:(si + 1) * S, lv[hh]], kt[si * S:(si + 1) * S, lk[hh]]) for hh in heads]
              for si in range(nsub)]
        st = [st_ref[hh] for hh in heads]
        inter = [[], []]
        for si in range(nsub):
            rows = slice(si * S, (si + 1) * S)
            for hh in heads:
                inter[hh].append(_dot_nt(qt[rows, lk[hh]], st[hh]))
                st[hh] = st[hh] * eg[si * S:si * S + 1, lk[hh]] + kv[si][hh]
        for hh in heads:
            st_ref[hh] = st[hh]

        for hh, att in ((0, att0), (1, att1)):
            diag = jnp.concatenate([_dot(att[g * GB:(g + 1) * GB], v[g * GB:(g + 1) * GB, lv[hh]])
                                    for g in range(tc // GB)], axis=0)
            o = diag + jnp.concatenate(inter[hh], axis=0)
            on = o * lax.rsqrt(jnp.mean(o * o, axis=1, keepdims=True) + EPS)
            out_ref[0, :, lv[hh]] = (on * ng_ref[:, lv[hh]] * _silu(gate[:, lv[hh]])).astype(BF16)

    pw = 4 * dk + 4 * dv
    return pl.pallas_call(
        body,
        grid=(bsz, GLA_HEADS // 2, t // tc),
        in_specs=[pl.BlockSpec((1, tc, d), lambda b, h, c: (b, c, 0)),
                  pl.BlockSpec((1, d), lambda b, h, c: (0, 0)),
                  pl.BlockSpec((1, d, pw), lambda b, h, c: (h, 0, 0)),
                  pl.BlockSpec((d, 128), lambda b, h, c: (0, 0)),
                  pl.BlockSpec((GLA_RANK, 2 * dk), lambda b, h, c: (0, h)),
                  pl.BlockSpec((1, 2 * dk), lambda b, h, c: (0, h)),
                  pl.BlockSpec((1, 2 * dv), lambda b, h, c: (0, h)),
                  pl.BlockSpec((2 * dk, 2 * GB), lambda b, h, c: (0, 0))],
        out_specs=pl.BlockSpec((1, tc, 2 * dv), lambda b, h, c: (b, c, h)),
        out_shape=jax.ShapeDtypeStruct((bsz, t, GLA_HEADS * dv), BF16),
        scratch_shapes=[pltpu.VMEM((2, dv, dk), F32), pltpu.VMEM((tc + S, 2 * dk), F32),
                        pltpu.VMEM((tc + S, 2 * dk), F32)],
        compiler_params=_cp("parallel", "parallel", "arbitrary"),
        name="gla",
    )(x, in_g, w_pairs, w_low, gate_up, gate_b, norm_g, head_ones)


RET_TB = 1024
RET_L = 256


def _retention(x, in_g, w_in, cos_t, sin_t, intra, inter, sdec, cdec, norm_g):
    bsz, t, d = x.shape
    TB, L, D, H, W = RET_TB, RET_L, RET_DIM, RET_HEADS, RET_W

    def body(x_ref, ing_ref, w_ref, cos_ref, sin_ref, intra_ref, inter_ref, sdec_ref, cdec_ref, ng_ref,
             out_ref, s_ref):
        c = pl.program_id(1)

        @pl.when(c == 0)
        def _():
            s_ref[...] = jnp.zeros_like(s_ref)

        p = jnp.dot(_rms_rows(x_ref[0], ing_ref[...]).astype(BF16), w_ref[...], preferred_element_type=F32)
        heads = range(H)
        s_st = [s_ref[h] for h in heads]
        for cc in range(TB // L):
            rows = slice(cc * L, (cc + 1) * L)
            cs = cos_ref[rows, :]
            sn = sin_ref[rows, :]

            def rot(z):
                return z * cs + pltpu.roll(z, D // 2, 1) * sn

            q = [rot(p[rows, h * D:(h + 1) * D]) * (D ** -0.5) for h in heads]
            k = [rot(p[rows, W + h * D:W + (h + 1) * D]) for h in heads]
            v = [p[rows, 2 * W + h * D:2 * W + (h + 1) * D] for h in heads]
            s = [_dot_nt(q[h], k[h]) * intra_ref[h] for h in heads]
            qs = [_dot(q[h], s_st[h]) for h in heads]
            kv = [_dot_tn(k[h] * sdec_ref[h], v[h]) for h in heads]
            sv = [_dot(s[h], v[h]) for h in heads]
            for h in heads:
                o = sv[h] + inter_ref[h] * qs[h]
                s_st[h] = cdec_ref[h] * s_st[h] + kv[h]
                oc = o - jnp.mean(o, axis=1, keepdims=True)
                on = oc * lax.rsqrt(jnp.mean(oc * oc, axis=1, keepdims=True) + EPS)
                hl = slice(h * D, (h + 1) * D)
                gate = p[rows, 3 * W + h * D:3 * W + (h + 1) * D]
                out_ref[0, rows, hl] = (on * ng_ref[:, hl] * _silu(gate)).astype(BF16)
        for h in heads:
            s_ref[h] = s_st[h]

    fix2 = lambda b, c: (0, 0)
    fix3 = lambda b, c: (0, 0, 0)
    return pl.pallas_call(
        body,
        grid=(bsz, t // TB),
        in_specs=[pl.BlockSpec((1, TB, d), lambda b, c: (b, c, 0)),
                  pl.BlockSpec((1, d), fix2),
                  pl.BlockSpec((d, 4 * W), fix2),
                  pl.BlockSpec((TB, D), lambda b, c: (c, 0)),
                  pl.BlockSpec((TB, D), lambda b, c: (c, 0)),
                  pl.BlockSpec((H, L, L), fix3),
                  pl.BlockSpec((H, L, 1), fix3),
                  pl.BlockSpec((H, L, 1), fix3),
                  pl.BlockSpec((H, 1, 1), fix3),
                  pl.BlockSpec((1, W), lambda b, c: (0, 0))],
        out_specs=pl.BlockSpec((1, TB, W), lambda b, c: (b, c, 0)),
        out_shape=jax.ShapeDtypeStruct((bsz, t, W), BF16),
        scratch_shapes=[pltpu.VMEM((H, D, D), F32)],
        compiler_params=_cp("parallel", "arbitrary"),
        name="retention",
    )(x, in_g, w_in, cos_t, sin_t, intra, inter, sdec, cdec, norm_g)


def _retention_tables(t):
    L, D = RET_L, RET_DIM
    inv = ROPE_BASE ** (-jnp.arange(0, D, 2, dtype=F32) / D)
    ang = jnp.arange(t).astype(F32)[:, None] * inv[None, :]
    cos = jnp.cos(ang)
    sin = jnp.sin(ang)
    cos_t = jnp.concatenate([cos, cos], axis=-1)
    sin_t = jnp.concatenate([-sin, sin], axis=-1)
    log_gamma = jnp.log1p(-jnp.exp2(-5.0 - jnp.arange(RET_HEADS, dtype=F32)))
    idx = jnp.arange(L, dtype=F32)
    causal = idx[:, None] >= idx[None, :]
    rel = jnp.where(causal, idx[:, None] - idx[None, :], 0.0)
    intra = jnp.where(causal, jnp.exp(log_gamma[:, None, None] * rel), 0.0)
    inter = jnp.exp(log_gamma[:, None] * (idx + 1.0))[:, :, None]
    sdec = jnp.exp(log_gamma[:, None] * (L - 1.0 - idx))[:, :, None]
    cdec = jnp.exp(log_gamma * L)[:, None, None]
    return cos_t, sin_t, intra, inter, sdec, cdec


RWP_TM = 512


def _seg_sum(x, bd):
    hi = x.astype(BF16)
    lo = (x - hi.astype(F32)).astype(BF16)
    return jnp.dot(hi, bd, preferred_element_type=F32) + jnp.dot(lo, bd, preferred_element_type=F32)


def _rwkv_prep(x, in_g, w_in, mu, w_up, w0, a_up, a0, g_up, k_k, k_a, r_k, bd):
    bsz, t, d = x.shape
    tm, W = RWP_TM, RW_W

    def body(x_ref, ing_ref, win_ref, mu_ref, wup_ref, w0_ref, aup_ref, a0_ref, gup_ref, kk_ref, ka_ref, rk_ref,
             bd_ref, r_out, lw_out, k_out, v_out, a_out, b_out, g_out, bonus_out, last_ref):
        c = pl.program_id(1)

        @pl.when(c == 0)
        def _():
            last_ref[...] = jnp.zeros_like(last_ref)

        cur = jnp.dot(_rms_rows(x_ref[0], ing_ref[...]).astype(BF16), win_ref[...], preferred_element_type=F32)
        prev = last_ref[...]
        last_ref[...] = cur[tm - 1:tm, :]
        rowi = lax.broadcasted_iota(jnp.int32, (tm, 1), 0)
        sh = jnp.where(rowi == 0, prev, pltpu.roll(cur, 1, 0))
        xm = cur + (sh - cur) * mu_ref[...]
        x_r = xm[:, 0:W]
        x_k = xm[:, W:2 * W]
        x_v = xm[:, 2 * W:3 * W]
        x_dl = xm[:, 3 * W:3 * W + 64]
        x_al = xm[:, 3 * W + 64:3 * W + 128]
        x_gl = xm[:, 3 * W + 128:3 * W + 256]
        wl = w0_ref[...] + _dot(jnp.tanh(x_dl), wup_ref[...])
        sp = jnp.maximum(-wl, 0.0) + jnp.log(1.0 + jnp.exp(-jnp.abs(wl)))
        lw_out[0] = -jnp.exp(-sp - 0.5)
        a = _sigmoid(a0_ref[...] + _dot(x_al, aup_ref[...]))
        g_out[0] = _dot(_sigmoid(x_gl), gup_ref[...]).astype(BF16)
        kk0 = x_k * kk_ref[...]
        nrm = jnp.sqrt(_seg_sum(kk0 * kk0, bd_ref[...]))
        kk = kk0 / jnp.maximum(nrm, 1e-12)
        k_h = x_k * (1.0 + (a - 1.0) * ka_ref[...])
        r_out[0] = x_r.astype(BF16)
        k_out[0] = k_h.astype(BF16)
        v_out[0] = x_v.astype(BF16)
        a_out[0] = (-kk).astype(BF16)
        b_out[0] = (kk * a).astype(BF16)
        bonus_out[0] = (_dot(x_r * k_h * rk_ref[...], bd_ref[...]) * x_v).astype(BF16)

    row = lambda b, c: (0, 0)
    blk = pl.BlockSpec((1, tm, W), lambda b, c: (b, c, 0))
    shp = [jax.ShapeDtypeStruct((bsz, t, W), F32 if i == 1 else BF16) for i in range(8)]
    return pl.pallas_call(
        body,
        grid=(bsz, t // tm),
        in_specs=[pl.BlockSpec((1, tm, d), lambda b, c: (b, c, 0)),
                  pl.BlockSpec((1, d), row),
                  pl.BlockSpec((d, RW_COLS), row),
                  pl.BlockSpec((1, RW_COLS), row),
                  pl.BlockSpec((64, W), row), pl.BlockSpec((1, W), row),
                  pl.BlockSpec((64, W), row), pl.BlockSpec((1, W), row),
                  pl.BlockSpec((128, W), row),
                  pl.BlockSpec((1, W), row), pl.BlockSpec((1, W), row), pl.BlockSpec((1, W), row),
                  pl.BlockSpec((W, W), row)],
        out_specs=[blk] * 8,
        out_shape=shp,
        scratch_shapes=[pltpu.VMEM((1, RW_COLS), F32)],
        compiler_params=_cp("parallel", "arbitrary"),
        name="rwkv_prep",
    )(x, in_g, w_in, mu, w_up, w0, a_up, a0, g_up, k_k, k_a, r_k, bd)


RW_L = 64
RW_TB = 256


def _rwkv_scan(r, lw, k, v, aa, bb, g_out, bonus, ln_g, ln_b):
    bsz, t, W = r.shape
    L, N, tb = RW_L, RW_DIM, RW_TB
    nck = tb // L

    def body(r_ref, lw_ref, k_ref, v_ref, a_ref, b_ref, g_ref, bonus_ref, lng_ref, lnb_ref, out_ref, h_ref):
        c = pl.program_id(1)

        @pl.when(c == 0)
        def _():
            h_ref[...] = jnp.zeros_like(h_ref)

        lw_all = lw_ref[0]
        rowi = lax.broadcasted_iota(jnp.int32, (tb, 1), 0) & (L - 1)
        cl = lw_all
        for s in (1, 2, 4, 8, 16, 32):
            cl = cl + jnp.where(rowi >= s, pltpu.roll(cl, s, 0), 0.0)
        cl_last = jnp.concatenate([jnp.broadcast_to(cl[(cc + 1) * L - 1:(cc + 1) * L, :], (L, W))
                                   for cc in range(nck)], axis=0)
        e_inv = jnp.exp(-cl)
        e_end = jnp.exp(cl_last - cl)
        p_end = jnp.exp(cl_last)
        at = a_ref[0] * jnp.exp(cl - lw_all)
        rt = r_ref[0] * jnp.exp(cl)
        bt = b_ref[0] * e_inv
        kt = k_ref[0] * e_inv
        b_end = b_ref[0] * e_end
        k_end = k_ref[0] * e_end
        v_all = v_ref[0]
        P2 = 2 * N
        pairs = [(cc, p) for cc in range(nck) for p in range(RW_HEADS // 2)]
        pr = range(len(pairs))

        def pb(z, pi):
            cc, p = pairs[pi]
            return z[cc * L:(cc + 1) * L, p * P2:(p + 1) * P2]

        def stack2(z):
            lane = lax.broadcasted_iota(jnp.int32, (1, z.shape[1]), 1) & (P2 - 1)
            return jnp.concatenate([jnp.where(lane < N, z, 0.0), jnp.where(lane >= N, z, 0.0)], axis=0)

        tcol = lax.broadcasted_iota(jnp.int32, (L, P2), 1) & (N - 1)
        trow = lax.broadcasted_iota(jnp.int32, (L, P2), 0)
        strict = trow > tcol
        lower = trow >= tcol
        r2 = lax.broadcasted_iota(jnp.int32, (P2, P2), 0)
        c2 = lax.broadcasted_iota(jnp.int32, (P2, P2), 1)
        same_head = (r2 // N) == (c2 // N)
        eye2 = r2 == c2
        x2 = [jnp.concatenate([pb(at, pi), pb(rt, pi)], axis=0) for pi in pr]
        m_b = [_dot_nt(x2[pi], stack2(pb(bt, pi))) for pi in pr]
        m_k = [_dot_nt(x2[pi], stack2(pb(kt, pi))) for pi in pr]
        ap = [jnp.where(strict, m_b[pi][0:L], 0.0) for pi in pr]
        a_rb = [jnp.where(lower, m_b[pi][L:2 * L], 0.0) for pi in pr]
        a_ak = [jnp.where(strict, m_k[pi][0:L], 0.0) for pi in pr]
        a_rk = [jnp.where(lower, m_k[pi][L:2 * L], 0.0) for pi in pr]
        vp = [_dot(jnp.concatenate([a_ak[pi], a_rk[pi]], axis=0), stack2(pb(v_all, pi))) for pi in pr]
        kv = [_dot(pb(k_end, pi).T, pb(v_all, pi)) for pi in pr]
        x = [jnp.concatenate([pb(at, pi), vp[pi][0:L]], axis=1) for pi in pr]
        for it in range(6):
            x = [x[pi] + _dot(ap[pi], stack2(x[pi])) for pi in pr]
            if it < 5:
                ap = [_dot(ap[pi], stack2(ap[pi])) for pi in pr]
        post1 = [_dot(a_rb[pi], stack2(x[pi])) for pi in pr]
        post2 = [_dot(pb(b_end, pi).T, x[pi]) for pi in pr]
        lhs = []
        y0s = []
        h_adds = []
        for pi in pr:
            q_hat = pb(rt, pi) + post1[pi][:, 0:P2]
            gmat = (jnp.where(same_head, post2[pi][:, 0:P2], 0.0)
                    + jnp.where(eye2, pb(p_end, pi)[0:1, :], 0.0))
            lhs.append(jnp.concatenate([q_hat, gmat], axis=0))
            y0s.append(post1[pi][:, P2:2 * P2] + vp[pi][L:2 * L])
            h_adds.append(jnp.where(same_head, post2[pi][:, P2:2 * P2] + kv[pi], 0.0))
        npair = RW_HEADS // 2
        h_st = [h_ref[p] for p in range(npair)]
        ys = [None] * len(pairs)
        for cc in range(nck):
            res = [_dot(lhs[cc * npair + p], h_st[p]) for p in range(npair)]
            for p in range(npair):
                pi = cc * npair + p
                ys[pi] = res[p][0:L] + y0s[pi]
                h_st[p] = res[p][L:L + P2] + h_adds[pi]
        for p in range(npair):
            h_ref[p] = h_st[p]
        seg = same_head.astype(BF16)
        mean = [_seg_sum(ys[pi], seg) * (1.0 / N) for pi in pr]
        yc = [ys[pi] - mean[pi] for pi in pr]
        var = [_seg_sum(yc[pi] * yc[pi], seg) * (1.0 / N) for pi in pr]
        for pi, (cc, p) in enumerate(pairs):
            yn = yc[pi] * lax.rsqrt(var[pi] + RW_LN_EPS)
            rows = slice(cc * L, (cc + 1) * L)
            cols = slice(p * P2, (p + 1) * P2)
            out_ref[0, rows, cols] = ((yn * lng_ref[:, cols] + lnb_ref[:, cols] + bonus_ref[0, rows, cols])
                                      * g_ref[0, rows, cols]).astype(BF16)

    blk = pl.BlockSpec((1, tb, W), lambda b, c: (b, c, 0))
    vec = pl.BlockSpec((1, W), lambda b, c: (0, 0))
    return pl.pallas_call(
        body,
        grid=(bsz, t // tb),
        in_specs=[blk] * 8 + [vec, vec],
        out_specs=blk,
        out_shape=jax.ShapeDtypeStruct((bsz, t, W), BF16),
        scratch_shapes=[pltpu.VMEM((RW_HEADS // 2, 2 * N, 2 * N), F32)],
        compiler_params=_cp("parallel", "arbitrary"),
        name="rwkv_scan",
    )(r, lw, k, v, aa, bb, g_out, bonus, ln_g, ln_b)


def _mix_out_route(a, b, w_bf16, resid, g, router_split, tm=1024):
    n, wa = a.shape
    wb = b.shape[1]
    d = w_bf16.shape[1]
    e = router_split.shape[1] // 2

    def body(a_ref, b_ref, w_ref, r_ref, g_ref, rt_ref, h_ref, xn_ref, lg_ref):
        acc = jnp.dot(a_ref[...], w_ref[0:wa, :], preferred_element_type=F32)
        acc = acc + jnp.dot(b_ref[...], w_ref[wa:wa + wb, :], preferred_element_type=F32)
        h = r_ref[...] + acc
        h_ref[...] = h
        xn = _rms_rows(h, g_ref[...])
        x_hi = xn.astype(BF16)
        xn_ref[...] = x_hi
        x_lo = (xn - x_hi.astype(F32)).astype(BF16)
        hi_both = jnp.dot(x_hi, rt_ref[...], preferred_element_type=F32)
        lo_hi = jnp.dot(x_lo, rt_ref[:, 0:e], preferred_element_type=F32)
        lg_ref[...] = hi_both[:, 0:e] + (lo_hi + hi_both[:, e:2 * e])

    return pl.pallas_call(
        body,
        grid=(n // tm,),
        in_specs=[pl.BlockSpec((tm, wa), lambda i: (i, 0)),
                  pl.BlockSpec((tm, wb), lambda i: (i, 0)),
                  pl.BlockSpec((wa + wb, d), lambda i: (0, 0)),
                  pl.BlockSpec((tm, d), lambda i: (i, 0)),
                  pl.BlockSpec((1, d), lambda i: (0, 0)),
                  pl.BlockSpec((d, 2 * e), lambda i: (0, 0))],
        out_specs=[pl.BlockSpec((tm, d), lambda i: (i, 0)), pl.BlockSpec((tm, d), lambda i: (i, 0)),
                   pl.BlockSpec((tm, e), lambda i: (i, 0))],
        out_shape=[jax.ShapeDtypeStruct((n, d), F32), jax.ShapeDtypeStruct((n, d), BF16),
                   jax.ShapeDtypeStruct((n, e), F32)],
        compiler_params=_cp("parallel"),
        name="mix_out_route",
    )(a, b, w_bf16, resid, g.reshape(1, d), router_split)


MOE_TM = 512
MOE_TF = 1792


def _experts(xs, row_w, item_tile, item_exp, item_lo, item_hi, wg, wu, wd):
    nrows, d = xs.shape
    tm, tf = MOE_TM, MOE_TF
    nf = D_FF // tf
    n_items = item_tile.shape[0]

    def body(it_ref, ie_ref, lo_ref, hi_ref, x_ref, w_ref, wg_ref, wu_ref, wd_ref, o_ref, acc_ref):
        i = pl.program_id(0)
        j = pl.program_id(1)
        tile = it_ref[i]
        first = jnp.logical_or(i == 0, tile != it_ref[jnp.maximum(i - 1, 0)])
        last = jnp.logical_or(i == n_items - 1, tile != it_ref[jnp.minimum(i + 1, n_items - 1)])

        @pl.when(jnp.logical_and(first, j == 0))
        def _():
            acc_ref[...] = jnp.zeros_like(acc_ref)

        lo = lo_ref[i]
        hi = hi_ref[i]

        @pl.when(lo < hi)
        def _():
            x = x_ref[...]
            gg = jnp.dot(x, wg_ref[0], preferred_element_type=F32)
            uu = jnp.dot(x, wu_ref[0], preferred_element_type=F32)
            act = (_silu(gg) * uu).astype(BF16)
            part = jnp.dot(act, wd_ref[0], preferred_element_type=F32)
            rowi = lax.broadcasted_iota(jnp.int32, (tm, 1), 0)
            mine = jnp.logical_and(rowi >= lo, rowi < hi)
            acc_ref[...] += part * jnp.where(mine, w_ref[...], 0.0)

        @pl.when(jnp.logical_and(last, j == nf - 1))
        def _():
            o_ref[...] = acc_ref[...].astype(o_ref.dtype)

    grid_spec = pltpu.PrefetchScalarGridSpec(
        num_scalar_prefetch=4,
        grid=(n_items, nf),
        in_specs=[pl.BlockSpec((tm, d), lambda i, j, it, ie, lo, hi: (it[i], 0)),
                  pl.BlockSpec((tm, 1), lambda i, j, it, ie, lo, hi: (it[i], 0)),
                  pl.BlockSpec((1, d, tf), lambda i, j, it, ie, lo, hi: (ie[i], 0, j)),
                  pl.BlockSpec((1, d, tf), lambda i, j, it, ie, lo, hi: (ie[i], 0, j)),
                  pl.BlockSpec((1, tf, d), lambda i, j, it, ie, lo, hi: (ie[i], j, 0))],
        out_specs=pl.BlockSpec((tm, d), lambda i, j, it, ie, lo, hi: (it[i], 0)),
        scratch_shapes=[pltpu.VMEM((tm, d), F32)],
    )
    return pl.pallas_call(
        body,
        grid_spec=grid_spec,
        out_shape=jax.ShapeDtypeStruct((nrows, d), BF16),
        compiler_params=_cp("arbitrary", "arbitrary"),
        name="moe_experts",
    )(item_tile, item_exp, item_lo, item_hi, xs, row_w, wg, wu, wd)


def _combine_norm(h, y0, y1, g, tm=512):
    n, d = h.shape

    def body(h_ref, a_ref, b_ref, g_ref, o_ref):
        o_ref[...] = _rms_rows(h_ref[...] + (a_ref[...].astype(F32) + b_ref[...].astype(F32)), g_ref[...])

    blk = pl.BlockSpec((tm, d), lambda i: (i, 0))
    return pl.pallas_call(
        body,
        grid=(n // tm,),
        in_specs=[blk, blk, blk, pl.BlockSpec((1, d), lambda i: (0, 0))],
        out_specs=blk,
        out_shape=jax.ShapeDtypeStruct((n, d), F32),
        compiler_params=_cp("parallel"),
        name="combine_norm",
    )(h, y0, y1, g.reshape(1, d))


def _route(logits, n):
    tm = MOE_TM
    na = n * TOP_K
    n_tiles = na // tm
    top_val, top_idx = lax.top_k(logits, TOP_K)
    top_w = jax.nn.softmax(top_val, axis=-1)
    e_flat = top_idx.reshape(-1).astype(jnp.int32)
    w_flat = top_w.reshape(-1)
    tok = jnp.arange(na, dtype=jnp.int32) // TOP_K
    _, sorted_tok, sorted_w = lax.sort((e_flat, tok, w_flat), num_keys=1, is_stable=True)
    onehot = (e_flat[:, None] == jnp.arange(N_EXPERTS, dtype=jnp.int32)[None, :]).astype(jnp.int32)
    rank = jnp.take_along_axis(jnp.cumsum(onehot, axis=0), e_flat[:, None], axis=1)[:, 0] - 1
    counts = jnp.sum(onehot, axis=0)
    ends = jnp.cumsum(counts)
    pos = (ends - counts)[e_flat] + rank
    cuts = jnp.sort(jnp.concatenate([jnp.arange(n_tiles, dtype=jnp.int32) * tm, ends[:-1].astype(jnp.int32)]))
    nxt = jnp.concatenate([cuts[1:], jnp.full((1,), na, jnp.int32)])
    item_tile = jnp.minimum(cuts // tm, n_tiles - 1)
    item_exp = jnp.minimum(jnp.searchsorted(ends, cuts, side="right"), N_EXPERTS - 1).astype(jnp.int32)
    item_lo = cuts - item_tile * tm
    item_hi = nxt - item_tile * tm
    return sorted_tok, sorted_w, (item_tile, item_exp, item_lo, item_hi), pos.reshape(n, TOP_K)


def kernel(x, e_norm1_g, e_w_in, e_ml_conv_w, e_ml_conv_b, e_ml_gate_b, e_ml_norm_g, e_gla_gate_up, e_gla_gate_b,
           e_gla_norm_g, e_w_out, e_norm2_g, e_ffn_w_gate, e_ffn_w_up, e_ffn_w_down, o_norm1_g, o_w_in,
           o_ret_norm_g, o_rw_mu, o_rw_w_up, o_rw_w0, o_rw_a_up, o_rw_a0, o_rw_g_up, o_rw_k_k, o_rw_k_a, o_rw_r_k,
           o_rw_ln_g, o_rw_ln_b, o_w_out, o_norm2_g, o_moe_router, o_moe_w_gate, o_moe_w_up, o_moe_w_down,
           final_norm_g):
    bsz, t, d = x.shape
    n = bsz * t
    h0 = x.reshape(n, d)

    w = e_w_in[0]
    row = lambda a: a.reshape(1, -1)
    x3 = x
    ng1 = row(e_norm1_g[0])
    w_if = w[:, 2048:2056]
    w_gate = jnp.zeros((d, 128), F32).at[:, :2 * ML_HEADS].set(w_if).astype(BF16)
    w_gate_t = jnp.zeros((16, d), F32).at[:2 * ML_HEADS, :].set(w_if.T).astype(BF16)
    h_ml = _mlstm(x3, ng1, w[:, :4 * ML_W].astype(BF16), w_gate, w_gate_t, row(e_ml_gate_b[0]),
                  e_ml_gate_b[0].reshape(-1, 1), e_ml_conv_w[0], row(e_ml_conv_b[0]), row(e_ml_norm_g[0]))
    gq, gk, gv, gr = 2056, 2312, 2568, 3080
    w_pairs = jnp.stack([jnp.concatenate([w[:, gq + 128 * hp:gq + 128 * (hp + 1)],
                                          w[:, gk + 128 * hp:gk + 128 * (hp + 1)],
                                          w[:, gv + 256 * hp:gv + 256 * (hp + 1)],
                                          w[:, gr + 256 * hp:gr + 256 * (hp + 1)]], axis=1)
                         for hp in range(GLA_HEADS // 2)]).astype(BF16)
    w_low = jnp.zeros((d, 128), F32).at[:, :GLA_RANK].set(w[:, 3592:3608]).astype(BF16)
    o_gla = _gla(x3, ng1, w_pairs, w_low, e_gla_gate_up[0].astype(BF16), row(e_gla_gate_b[0]),
                 row(e_gla_norm_g[0]))
    h2 = _mix_out_ffn(h_ml.reshape(n, -1), o_gla.reshape(n, -1), e_w_out[0].astype(BF16), h0, e_norm2_g[0],
                      e_ffn_w_gate[0].astype(BF16), e_ffn_w_up[0].astype(BF16), e_ffn_w_down[0].astype(BF16))

    w = o_w_in[0]
    h2_3 = h2.reshape(bsz, t, d)
    ng2 = row(o_norm1_g[0])
    y_ret = _retention(h2_3, ng2, w[:, :4 * RET_W].astype(BF16), *_retention_tables(t), row(o_ret_norm_g[0]))
    head_of = jnp.arange(RW_W) // RW_DIM
    bd = (head_of[:, None] == head_of[None, :]).astype(BF16)
    r, lw, k, v, aa, bb, g_out, bonus = _rwkv_prep(
        h2_3, ng2, w[:, 4 * RET_W:].astype(BF16), row(o_rw_mu[0]), o_rw_w_up[0].astype(BF16), row(o_rw_w0[0]),
        o_rw_a_up[0].astype(BF16), row(o_rw_a0[0]), o_rw_g_up[0].astype(BF16), row(o_rw_k_k[0]),
        row(o_rw_k_a[0]), row(o_rw_r_k[0]), bd)
    y_rw = _rwkv_scan(r, lw, k, v, aa, bb, g_out, bonus, row(o_rw_ln_g[0]), row(o_rw_ln_b[0]))
    router_pad = jnp.zeros((d, 128), F32).at[:, :N_EXPERTS].set(o_moe_router[0])
    router_hi = router_pad.astype(BF16)
    router_split = jnp.concatenate([router_hi, (router_pad - router_hi.astype(F32)).astype(BF16)], axis=1)
    h3, xn, logits = _mix_out_route(y_ret.reshape(n, -1), y_rw.reshape(n, -1), o_w_out[0].astype(BF16), h2,
                                    o_norm2_g[0], router_split)
    sorted_tok, sorted_w, items, pos = _route(logits[:, :N_EXPERTS], n)
    xs = xn.at[sorted_tok].get(mode="promise_in_bounds")
    ys = _experts(xs, sorted_w.reshape(-1, 1), *items, o_moe_w_gate[0].astype(BF16),
                  o_moe_w_up[0].astype(BF16), o_moe_w_down[0].astype(BF16))
    y0 = ys.at[pos[:, 0]].get(mode="promise_in_bounds")
    y1 = ys.at[pos[:, 1]].get(mode="promise_in_bounds")
    out = _combine_norm(h3, y0, y1, final_norm_g)
    return out.reshape(bsz, t, d)
```

```python
import functools

import numpy as np
import jax
import jax.numpy as jnp
from jax import lax
from jax.experimental import pallas as pl
from jax.experimental.pallas import tpu as pltpu

F32 = jnp.float32
BF16 = jnp.bfloat16

D_MODEL = 1024
EPS = 1e-6
ML_HEADS, ML_DIM, ML_W, ML_CONV = 4, 128, 512, 4
GLA_HEADS, GLA_DK, GLA_DV, GLA_RANK, GLA_TAU = 4, 64, 128, 16, 16.0
RET_HEADS, RET_DIM, RET_W = 4, 128, 512
ROPE_BASE = 10000.0
RW_HEADS, RW_DIM, RW_W = 8, 64, 512
RW_COLS = 1792
RW_LN_EPS = 64e-5
D_FF = 3584
N_EXPERTS = 8
TOP_K = 2

VMEM_LIMIT = 48 * 1024 * 1024
NEG = -1e30


def _cp(*sem):
    return pltpu.CompilerParams(dimension_semantics=sem, vmem_limit_bytes=VMEM_LIMIT)


def _sigmoid(x):
    return 1.0 / (1.0 + jnp.exp(-x))


def _silu(x):
    return x * _sigmoid(x)


def _log_sigmoid(x):
    return jnp.minimum(x, 0.0) - jnp.log(1.0 + jnp.exp(-jnp.abs(x)))


def _dot(a, b):
    return jnp.dot(a.astype(BF16), b.astype(BF16), preferred_element_type=F32)


def _dot_nt(a, b):
    return lax.dot_general(a.astype(BF16), b.astype(BF16), (((1,), (1,)), ((), ())), preferred_element_type=F32)


def _dot_tn(a, b):
    return jnp.dot(a.T.astype(BF16), b.astype(BF16), preferred_element_type=F32)


def _rms_rows(x, g):
    ms = jnp.mean(x * x, axis=-1, keepdims=True)
    return x * lax.rsqrt(ms + EPS) * g


def _mix_out_ffn(a, b, w_out, resid, g, wg, wu, wd, tm=512, tf=1792):
    n, d = resid.shape
    wa = a.shape[1]
    wb = b.shape[1]
    f = wg.shape[1]
    nf = f // tf

    def body(a_ref, b_ref, wo_ref, r_ref, g_ref, wg_ref, wu_ref, wd_ref, o_ref, xn_ref, acc_ref):
        j = pl.program_id(1)

        @pl.when(j == 0)
        def _():
            h = r_ref[...] + jnp.dot(a_ref[...], wo_ref[0:wa, :], preferred_element_type=F32)
            h = h + jnp.dot(b_ref[...], wo_ref[wa:wa + wb, :], preferred_element_type=F32)
            xn_ref[...] = _rms_rows(h, g_ref[...]).astype(BF16)
            acc_ref[...] = h

        xn = xn_ref[...]
        gg = jnp.dot(xn, wg_ref[...], preferred_element_type=F32)
        uu = jnp.dot(xn, wu_ref[...], preferred_element_type=F32)
        act = (_silu(gg) * uu).astype(BF16)
        acc_ref[...] += jnp.dot(act, wd_ref[...], preferred_element_type=F32)

        @pl.when(j == nf - 1)
        def _():
            o_ref[...] = acc_ref[...]

    return pl.pallas_call(
        body,
        grid=(n // tm, nf),
        in_specs=[pl.BlockSpec((tm, wa), lambda i, j: (i, 0)),
                  pl.BlockSpec((tm, wb), lambda i, j: (i, 0)),
                  pl.BlockSpec((wa + wb, d), lambda i, j: (0, 0)),
                  pl.BlockSpec((tm, d), lambda i, j: (i, 0)),
                  pl.BlockSpec((1, d), lambda i, j: (0, 0)),
                  pl.BlockSpec((d, tf), lambda i, j: (0, j)),
                  pl.BlockSpec((d, tf), lambda i, j: (0, j)),
                  pl.BlockSpec((tf, d), lambda i, j: (j, 0))],
        out_specs=pl.BlockSpec((tm, d), lambda i, j: (i, 0)),
        out_shape=jax.ShapeDtypeStruct((n, d), F32),
        scratch_shapes=[pltpu.VMEM((tm, d), BF16), pltpu.VMEM((tm, d), F32)],
        compiler_params=_cp("parallel", "arbitrary"),
        name="mix_out_ffn",
    )(a, b, w_out, resid, g.reshape(1, d), wg, wu, wd)


ML_TB = 1024
ML_L = 256


def _mlstm(x, in_g, w_in, w_gate, w_gate_t, gate_b_row, gate_b_col, conv_w, conv_b, norm_g):
    bsz, t, d = x.shape
    TB, L, D, H, W = ML_TB, ML_L, ML_DIM, ML_HEADS, ML_W

    def body(x_ref, ing_ref, w_ref, wg_ref, wgt_ref, gbr_ref, gbc_ref, cwq_ref, cwk_ref, cbq_ref, cbk_ref,
             ng_ref, out_ref, qext, kext, c_ref, n_ref, m_ref):
        c = pl.program_id(1)

        @pl.when(c == 0)
        def _():
            qext[0:8, :] = jnp.zeros((8, W), F32)
            kext[0:8, :] = jnp.zeros((8, W), F32)
            c_ref[...] = jnp.zeros_like(c_ref)
            n_ref[...] = jnp.zeros_like(n_ref)
            m_ref[...] = jnp.zeros_like(m_ref)

        def project(cc):
            xn = _rms_rows(x_ref[0, cc * L:(cc + 1) * L, :], ing_ref[...]).astype(BF16)
            pc = jnp.dot(xn, w_ref[...], preferred_element_type=F32)
            gc = jnp.dot(xn, wg_ref[...], preferred_element_type=F32)[:, 0:2 * H]
            gr = lax.dot_general(wgt_ref[...], xn, (((1,), (1,)), ((), ())),
                                 preferred_element_type=F32)[0:2 * H, :]
            return pc, gc, gr

        def conv(ext, base, cw_ref, cb_ref):
            acc = cb_ref[...] + cw_ref[0:1, :] * ext[pl.ds(base + 8 - ML_CONV + 1, L), :]
            for kk in range(1, ML_CONV):
                acc = acc + cw_ref[kk:kk + 1, :] * ext[pl.ds(base + 8 - ML_CONV + 1 + kk, L), :]
            return _silu(acc)

        ri = lax.broadcasted_iota(jnp.int32, (L, L), 0)
        ci = lax.broadcasted_iota(jnp.int32, (L, L), 1)
        causal = ri >= ci
        heads = range(H)
        c_st = [c_ref[h] for h in heads]
        n_st = [n_ref[h] for h in heads]
        m_st = [m_ref[h] for h in heads]
        nxt = project(0)
        for cc in range(TB // L):
            rows = slice(cc * L, (cc + 1) * L)
            p, g_cols, g_rows = nxt
            if cc + 1 < TB // L:
                nxt = project(cc + 1)
            qext[8 + cc * L:8 + (cc + 1) * L, :] = p[:, 0:W]
            kext[8 + cc * L:8 + (cc + 1) * L, :] = p[:, W:2 * W]
            q_all = conv(qext, cc * L, cwq_ref, cbq_ref) * (D ** -0.5)
            k_all = conv(kext, cc * L, cwk_ref, cbk_ref)
            gcol = g_cols + gbr_ref[...]
            grow = g_rows + gbc_ref[...]
            fcol = _log_sigmoid(gcol[:, H:2 * H])
            frow = _log_sigmoid(grow[H:2 * H, :])
            hs = lambda z, h: z[:, h * D:(h + 1) * D]
            w_intra, w_inter, w_state, carry, m_row = [], [], [], [], []
            for h in heads:
                f_row = frow[h:h + 1, :]
                i_row = grow[h:h + 1, :]
                f_col = fcol[:, h:h + 1]
                i_col = gcol[:, h:h + 1]
                b_col = jnp.sum(jnp.where(causal, f_row, 0.0), axis=1, keepdims=True)
                b_row = jnp.sum(jnp.where(ri <= ci, f_col, 0.0), axis=0, keepdims=True)
                g_tot = jnp.sum(f_row, axis=1, keepdims=True)
                d_intra = jnp.where(causal, b_col - b_row + i_row, NEG)
                d_inter = b_col + m_st[h]
                mr = jnp.maximum(d_inter, jnp.max(d_intra, axis=1, keepdims=True))
                w_intra.append(jnp.exp(d_intra - mr))
                w_inter.append(jnp.exp(d_inter - mr))
                m_row.append(mr)
                d_state = g_tot - b_col + i_col
                mn = jnp.maximum(g_tot + m_st[h], jnp.max(d_state, axis=0, keepdims=True))
                w_state.append(jnp.exp(d_state - mn))
                carry.append(jnp.exp(g_tot + m_st[h] - mn))
                m_st[h] = mn
            qh = [hs(q_all, h) for h in heads]
            kh = [hs(k_all, h) for h in heads]
            vh = [p[:, 2 * W + h * D:2 * W + (h + 1) * D] for h in heads]
            s = [_dot_nt(qh[h], kh[h]) * w_intra[h] for h in heads]
            qc = [_dot(qh[h], c_st[h]) for h in heads]
            kw = [kh[h] * w_state[h] for h in heads]
            kv = [_dot_tn(kw[h], vh[h]) for h in heads]
            sv = [_dot(s[h], vh[h]) for h in heads]
            for h in heads:
                num = sv[h] + w_inter[h] * qc[h]
                den = (jnp.sum(s[h], axis=1, keepdims=True)
                       + w_inter[h] * jnp.sum(qh[h] * n_st[h], axis=1, keepdims=True))
                hval = num / jnp.maximum(jnp.abs(den), jnp.exp(-m_row[h]))
                c_st[h] = carry[h] * c_st[h] + kv[h]
                n_st[h] = carry[h] * n_st[h] + jnp.sum(kw[h], axis=0, keepdims=True)
                hg = _sigmoid(p[:, 3 * W + h * D:3 * W + (h + 1) * D]) * hval
                hc = hg - jnp.mean(hg, axis=1, keepdims=True)
                hn = hc * lax.rsqrt(jnp.mean(hc * hc, axis=1, keepdims=True) + EPS)
                out_ref[0, rows, h * D:(h + 1) * D] = (hn * ng_ref[:, h * D:(h + 1) * D]).astype(BF16)
        for h in heads:
            c_ref[h] = c_st[h]
            n_ref[h] = n_st[h]
            m_ref[h] = m_st[h]
        qext[0:8, :] = qext[TB:TB + 8, :]
        kext[0:8, :] = kext[TB:TB + 8, :]

    fix = lambda j: (lambda b, c: (0, j))
    return pl.pallas_call(
        body,
        grid=(bsz, t // TB),
        in_specs=[pl.BlockSpec((1, TB, d), lambda b, c: (b, c, 0)),
                  pl.BlockSpec((1, d), fix(0)),
                  pl.BlockSpec((d, 4 * W), fix(0)),
                  pl.BlockSpec((d, 128), fix(0)),
                  pl.BlockSpec((16, d), fix(0)),
                  pl.BlockSpec((1, 2 * H), fix(0)),
                  pl.BlockSpec((2 * H, 1), fix(0)),
                  pl.BlockSpec((ML_CONV, W), fix(0)),
                  pl.BlockSpec((ML_CONV, W), fix(1)),
                  pl.BlockSpec((1, W), fix(0)),
                  pl.BlockSpec((1, W), fix(1)),
                  pl.BlockSpec((1, W), fix(0))],
        out_specs=pl.BlockSpec((1, TB, W), lambda b, c: (b, c, 0)),
        out_shape=jax.ShapeDtypeStruct((bsz, t, W), BF16),
        scratch_shapes=[pltpu.VMEM((TB + 8, W), F32), pltpu.VMEM((TB + 8, W), F32),
                        pltpu.VMEM((H, D, D), F32), pltpu.VMEM((H, 1, D), F32), pltpu.VMEM((H, 1, 1), F32)],
        compiler_params=_cp("parallel", "arbitrary"),
        name="mlstm",
    )(x, in_g, w_in, w_gate, w_gate_t, gate_b_row, gate_b_col, conv_w, conv_w, conv_b, conv_b, norm_g)


GLA_TC = 1024
GLA_SUB = 16
GLA_GROUP = 128


def _gla(x, in_g, w_pairs, w_low, gate_up, gate_b, norm_g):
    bsz, t, d = x.shape
    tc, S, GB = GLA_TC, GLA_SUB, GLA_GROUP
    head_ones = (jnp.arange(2 * GLA_DK)[:, None] // GLA_DK == jnp.arange(2 * GB)[None, :] // GB).astype(BF16)
    nsub = tc // S
    dk, dv = GLA_DK, GLA_DV

    def body(x_ref, ing_ref, w_ref, wl_ref, gu_ref, gbias_ref, ng_ref, ones_ref, out_ref, st_ref, ksh, bsh):
        c = pl.program_id(2)

        @pl.when(c == 0)
        def _():
            st_ref[...] = jnp.zeros_like(st_ref)
            ksh[0:S, :] = jnp.zeros((S, 2 * dk), F32)
            bsh[0:S, :] = jnp.zeros((S, 2 * dk), F32)

        xn = _rms_rows(x_ref[0], ing_ref[...]).astype(BF16)
        p = jnp.dot(xn, w_ref[0], preferred_element_type=F32)
        g_low = jnp.dot(xn, wl_ref[...], preferred_element_type=F32)[:, 0:GLA_RANK]
        z = _dot(g_low, gu_ref[...]) + gbias_ref[...]
        la = _log_sigmoid(z) / GLA_TAU
        rowi = lax.broadcasted_iota(jnp.int32, (tc, 1), 0)
        rmod = rowi & (S - 1)
        bcum = la
        rsum = la
        for s in (1, 2, 4, 8):
            bcum = bcum + jnp.where(rmod >= s, pltpu.roll(bcum, s, 0), 0.0)
            rsum = rsum + jnp.where(rmod < S - s, pltpu.roll(rsum, tc - s, 0), 0.0)
        q = p[:, 0:2 * dk] * (dk ** -0.5)
        k = p[:, 2 * dk:4 * dk]
        v = p[:, 4 * dk:4 * dk + 2 * dv]
        gate = p[:, 4 * dk + 2 * dv:4 * dk + 4 * dv]
        qt = q * jnp.exp(bcum)
        kt = k * jnp.exp(rsum - la)
        eg = jnp.exp(bcum + rsum - la)

        ksh[S:, :] = k
        bsh[S:, :] = bcum
        prods = []
        for d in range(S):
            kd = k if d == 0 else ksh[pl.ds(S - d, tc), :]
            bd = bcum if d == 0 else bsh[pl.ds(S - d, tc), :]
            e = jnp.exp(jnp.where(rmod >= d, bcum - bd, 0.0))
            prods.append((q * kd * e).astype(BF16))
        ws = [jnp.dot(p, ones_ref[...], preferred_element_type=F32) for p in prods]
        coli = lax.broadcasted_iota(jnp.int32, (tc, GB), 1)
        rgrp = rowi & (GB - 1)
        att0 = jnp.zeros((tc, GB), F32)
        att1 = jnp.zeros((tc, GB), F32)
        offs = jnp.where((coli // S) == (rgrp // S), rgrp - coli, -1)
        for d in range(S):
            here = offs == d
            att0 = jnp.where(here, ws[d][:, 0:GB], att0)
            att1 = jnp.where(here, ws[d][:, GB:2 * GB], att1)

        heads = range(2)
        lk = [slice(hh * dk, (hh + 1) * dk) for hh in heads]
        lv = [slice(hh * dv, (hh + 1) * dv) for hh in heads]
        kv = [[_dot_tn(v[si * S:(si + 1) * S, lv[hh]], kt[si * S:(si + 1) * S, lk[hh]]) for hh in heads]
              for si in range(nsub)]
        st = [st_ref[hh] for hh in heads]
        inter = [[], []]
        for si in range(nsub):
            rows = slice(si * S, (si + 1) * S)
            for hh in heads:
                inter[hh].append(_dot_nt(qt[rows, lk[hh]], st[hh]))
                st[hh] = st[hh] * eg[si * S:si * S + 1, lk[hh]] + kv[si][hh]
        for hh in heads:
            st_ref[hh] = st[hh]

        for hh, att in ((0, att0), (1, att1)):
            diag = jnp.concatenate([_dot(att[g * GB:(g + 1) * GB], v[g * GB:(g + 1) * GB, lv[hh]])
                                    for g in range(tc // GB)], axis=0)
            o = diag + jnp.concatenate(inter[hh], axis=0)
            on = o * lax.rsqrt(jnp.mean(o * o, axis=1, keepdims=True) + EPS)
            out_ref[0, :, lv[hh]] = (on * ng_ref[:, lv[hh]] * _silu(gate[:, lv[hh]])).astype(BF16)

    pw = 4 * dk + 4 * dv
    return pl.pallas_call(
        body,
        grid=(bsz, GLA_HEADS // 2, t // tc),
        in_specs=[pl.BlockSpec((1, tc, d), lambda b, h, c: (b, c, 0)),
                  pl.BlockSpec((1, d), lambda b, h, c: (0, 0)),
                  pl.BlockSpec((1, d, pw), lambda b, h, c: (h, 0, 0)),
                  pl.BlockSpec((d, 128), lambda b, h, c: (0, 0)),
                  pl.BlockSpec((GLA_RANK, 2 * dk), lambda b, h, c: (0, h)),
                  pl.BlockSpec((1, 2 * dk), lambda b, h, c: (0, h)),
                  pl.BlockSpec((1, 2 * dv), lambda b, h, c: (0, h)),
                  pl.BlockSpec((2 * dk, 2 * GB), lambda b, h, c: (0, 0))],
        out_specs=pl.BlockSpec((1, tc, 2 * dv), lambda b, h, c: (b, c, h)),
        out_shape=jax.ShapeDtypeStruct((bsz, t, GLA_HEADS * dv), BF16),
        scratch_shapes=[pltpu.VMEM((2, dv, dk), F32), pltpu.VMEM((tc + S, 2 * dk), F32),
                        pltpu.VMEM((tc + S, 2 * dk), F32)],
        compiler_params=_cp("parallel", "parallel", "arbitrary"),
        name="gla",
    )(x, in_g, w_pairs, w_low, gate_up, gate_b, norm_g, head_ones)


RET_TB = 1024
RET_L = 256


def _retention(x, in_g, w_in, cos_t, sin_t, intra, inter, sdec, cdec, norm_g):
    bsz, t, d = x.shape
    TB, L, D, H, W = RET_TB, RET_L, RET_DIM, RET_HEADS, RET_W

    def body(x_ref, ing_ref, w_ref, cos_ref, sin_ref, intra_ref, inter_ref, sdec_ref, cdec_ref, ng_ref,
             out_ref, s_ref):
        c = pl.program_id(1)

        @pl.when(c == 0)
        def _():
            s_ref[...] = jnp.zeros_like(s_ref)

        p = jnp.dot(_rms_rows(x_ref[0], ing_ref[...]).astype(BF16), w_ref[...], preferred_element_type=F32)
        heads = range(H)
        s_st = [s_ref[h] for h in heads]
        for cc in range(TB // L):
            rows = slice(cc * L, (cc + 1) * L)
            cs = cos_ref[rows, :]
            sn = sin_ref[rows, :]

            def rot(z):
                return z * cs + pltpu.roll(z, D // 2, 1) * sn

            q = [rot(p[rows, h * D:(h + 1) * D]) * (D ** -0.5) for h in heads]
            k = [rot(p[rows, W + h * D:W + (h + 1) * D]) for h in heads]
            v = [p[rows, 2 * W + h * D:2 * W + (h + 1) * D] for h in heads]
            s = [_dot_nt(q[h], k[h]) * intra_ref[h] for h in heads]
            qs = [_dot(q[h], s_st[h]) for h in heads]
            kv = [_dot_tn(k[h] * sdec_ref[h], v[h]) for h in heads]
            sv = [_dot(s[h], v[h]) for h in heads]
            for h in heads:
                o = sv[h] + inter_ref[h] * qs[h]
                s_st[h] = cdec_ref[h] * s_st[h] + kv[h]
                oc = o - jnp.mean(o, axis=1, keepdims=True)
                on = oc * lax.rsqrt(jnp.mean(oc * oc, axis=1, keepdims=True) + EPS)
                hl = slice(h * D, (h + 1) * D)
                gate = p[rows, 3 * W + h * D:3 * W + (h + 1) * D]
                out_ref[0, rows, hl] = (on * ng_ref[:, hl] * _silu(gate)).astype(BF16)
        for h in heads:
            s_ref[h] = s_st[h]

    fix2 = lambda b, c: (0, 0)
    fix3 = lambda b, c: (0, 0, 0)
    return pl.pallas_call(
        body,
        grid=(bsz, t // TB),
        in_specs=[pl.BlockSpec((1, TB, d), lambda b, c: (b, c, 0)),
                  pl.BlockSpec((1, d), fix2),
                  pl.BlockSpec((d, 4 * W), fix2),
                  pl.BlockSpec((TB, D), lambda b, c: (c, 0)),
                  pl.BlockSpec((TB, D), lambda b, c: (c, 0)),
                  pl.BlockSpec((H, L, L), fix3),
                  pl.BlockSpec((H, L, 1), fix3),
                  pl.BlockSpec((H, L, 1), fix3),
                  pl.BlockSpec((H, 1, 1), fix3),
                  pl.BlockSpec((1, W), lambda b, c: (0, 0))],
        out_specs=pl.BlockSpec((1, TB, W), lambda b, c: (b, c, 0)),
        out_shape=jax.ShapeDtypeStruct((bsz, t, W), BF16),
        scratch_shapes=[pltpu.VMEM((H, D, D), F32)],
        compiler_params=_cp("parallel", "arbitrary"),
        name="retention",
    )(x, in_g, w_in, cos_t, sin_t, intra, inter, sdec, cdec, norm_g)


def _retention_tables(t):
    L, D = RET_L, RET_DIM
    inv = ROPE_BASE ** (-jnp.arange(0, D, 2, dtype=F32) / D)
    ang = jnp.arange(t).astype(F32)[:, None] * inv[None, :]
    cos = jnp.cos(ang)
    sin = jnp.sin(ang)
    cos_t = jnp.concatenate([cos, cos], axis=-1)
    sin_t = jnp.concatenate([-sin, sin], axis=-1)
    log_gamma = jnp.log1p(-jnp.exp2(-5.0 - jnp.arange(RET_HEADS, dtype=F32)))
    idx = jnp.arange(L, dtype=F32)
    causal = idx[:, None] >= idx[None, :]
    rel = jnp.where(causal, idx[:, None] - idx[None, :], 0.0)
    intra = jnp.where(causal, jnp.exp(log_gamma[:, None, None] * rel), 0.0)
    inter = jnp.exp(log_gamma[:, None] * (idx + 1.0))[:, :, None]
    sdec = jnp.exp(log_gamma[:, None] * (L - 1.0 - idx))[:, :, None]
    cdec = jnp.exp(log_gamma * L)[:, None, None]
    return cos_t, sin_t, intra, inter, sdec, cdec


RWP_TM = 1024
RWP_SUB = 256


def _seg_sum(x, bd):
    hi = x.astype(BF16)
    lo = (x - hi.astype(F32)).astype(BF16)
    return jnp.dot(hi, bd, preferred_element_type=F32) + jnp.dot(lo, bd, preferred_element_type=F32)


def _rwkv_prep(x, in_g, w_in, mu, w_up, w0, a_up, a0, g_up, k_k, k_a, r_k, bd):
    bsz, t, d = x.shape
    tm, hm, W = RWP_TM, RWP_SUB, RW_W

    def body(x_ref, ing_ref, win_ref, mu_ref, wup_ref, w0_ref, aup_ref, a0_ref, gup_ref, kk_ref, ka_ref, rk_ref,
             bd_ref, r_out, lw_out, k_out, v_out, a_out, b_out, g_out, bonus_out, last_ref):
        c = pl.program_id(1)

        @pl.when(c == 0)
        def _():
            last_ref[...] = jnp.zeros_like(last_ref)

        def project(i):
            xs = x_ref[0, i * hm:(i + 1) * hm, :]
            return jnp.dot(_rms_rows(xs, ing_ref[...]).astype(BF16), win_ref[...], preferred_element_type=F32)

        rowi = lax.broadcasted_iota(jnp.int32, (hm, 1), 0)
        prev = last_ref[...]
        nxt = project(0)
        for i in range(tm // hm):
            rows = slice(i * hm, (i + 1) * hm)
            cur = nxt
            if i + 1 < tm // hm:
                nxt = project(i + 1)
            sh = jnp.where(rowi == 0, prev, pltpu.roll(cur, 1, 0))
            prev = cur[hm - 1:hm, :]
            xm = cur + (sh - cur) * mu_ref[...]
            x_r = xm[:, 0:W]
            x_k = xm[:, W:2 * W]
            x_v = xm[:, 2 * W:3 * W]
            x_dl = xm[:, 3 * W:3 * W + 64]
            x_al = xm[:, 3 * W + 64:3 * W + 128]
            x_gl = xm[:, 3 * W + 128:3 * W + 256]
            wl = w0_ref[...] + _dot(jnp.tanh(x_dl), wup_ref[...])
            sp = jnp.maximum(-wl, 0.0) + jnp.log(1.0 + jnp.exp(-jnp.abs(wl)))
            lw_out[0, rows, :] = -jnp.exp(-sp - 0.5)
            a = _sigmoid(a0_ref[...] + _dot(x_al, aup_ref[...]))
            g_out[0, rows, :] = _dot(_sigmoid(x_gl), gup_ref[...]).astype(BF16)
            kk0 = x_k * kk_ref[...]
            nrm = jnp.sqrt(_seg_sum(kk0 * kk0, bd_ref[...]))
            kk = kk0 / jnp.maximum(nrm, 1e-12)
            k_h = x_k * (1.0 + (a - 1.0) * ka_ref[...])
            r_out[0, rows, :] = x_r.astype(BF16)
            k_out[0, rows, :] = k_h.astype(BF16)
            v_out[0, rows, :] = x_v.astype(BF16)
            a_out[0, rows, :] = (-kk).astype(BF16)
            b_out[0, rows, :] = (kk * a).astype(BF16)
            bonus_out[0, rows, :] = (_dot(x_r * k_h * rk_ref[...], bd_ref[...]) * x_v).astype(BF16)
        last_ref[...] = prev

    row = lambda b, c: (0, 0)
    blk = pl.BlockSpec((1, tm, W), lambda b, c: (b, c, 0))
    shp = [jax.ShapeDtypeStruct((bsz, t, W), F32 if i == 1 else BF16) for i in range(8)]
    return pl.pallas_call(
        body,
        grid=(bsz, t // tm),
        in_specs=[pl.BlockSpec((1, tm, d), lambda b, c: (b, c, 0)),
                  pl.BlockSpec((1, d), row),
                  pl.BlockSpec((d, RW_COLS), row),
                  pl.BlockSpec((1, RW_COLS), row),
                  pl.BlockSpec((64, W), row), pl.BlockSpec((1, W), row),
                  pl.BlockSpec((64, W), row), pl.BlockSpec((1, W), row),
                  pl.BlockSpec((128, W), row),
                  pl.BlockSpec((1, W), row), pl.BlockSpec((1, W), row), pl.BlockSpec((1, W), row),
                  pl.BlockSpec((W, W), row)],
        out_specs=[blk] * 8,
        out_shape=shp,
        scratch_shapes=[pltpu.VMEM((1, RW_COLS), F32)],
        compiler_params=_cp("parallel", "arbitrary"),
        name="rwkv_prep",
    )(x, in_g, w_in, mu, w_up, w0, a_up, a0, g_up, k_k, k_a, r_k, bd)


RW_L = 64
RW_TB = 256


def _rwkv_scan(r, lw, k, v, aa, bb, g_out, bonus, ln_g, ln_b):
    bsz, t, W = r.shape
    L, N, tb = RW_L, RW_DIM, RW_TB
    nck = tb // L

    def body(r_ref, lw_ref, k_ref, v_ref, a_ref, b_ref, g_ref, bonus_ref, lng_ref, lnb_ref, out_ref, h_ref):
        c = pl.program_id(1)

        @pl.when(c == 0)
        def _():
            h_ref[...] = jnp.zeros_like(h_ref)

        lw_all = lw_ref[0]
        rowi = lax.broadcasted_iota(jnp.int32, (tb, 1), 0) & (L - 1)
        cl = lw_all
        for s in (1, 2, 4, 8, 16, 32):
            cl = cl + jnp.where(rowi >= s, pltpu.roll(cl, s, 0), 0.0)
        cl_last = jnp.concatenate([jnp.broadcast_to(cl[(cc + 1) * L - 1:(cc + 1) * L, :], (L, W))
                                   for cc in range(nck)], axis=0)
        e_inv = jnp.exp(-cl)
        e_end = jnp.exp(cl_last - cl)
        p_end = jnp.exp(cl_last)
        at = a_ref[0] * jnp.exp(cl - lw_all)
        rt = r_ref[0] * jnp.exp(cl)
        bt = b_ref[0] * e_inv
        kt = k_ref[0] * e_inv
        b_end = b_ref[0] * e_end
        k_end = k_ref[0] * e_end
        v_all = v_ref[0]
        P2 = 2 * N
        pairs = [(cc, p) for cc in range(nck) for p in range(RW_HEADS // 2)]
        pr = range(len(pairs))

        def pb(z, pi):
            cc, p = pairs[pi]
            return z[cc * L:(cc + 1) * L, p * P2:(p + 1) * P2]

        def stack2(z):
            lane = lax.broadcasted_iota(jnp.int32, (1, z.shape[1]), 1) & (P2 - 1)
            return jnp.concatenate([jnp.where(lane < N, z, 0.0), jnp.where(lane >= N, z, 0.0)], axis=0)

        tcol = lax.broadcasted_iota(jnp.int32, (L, P2), 1) & (N - 1)
        trow = lax.broadcasted_iota(jnp.int32, (L, P2), 0)
        strict = trow > tcol
        lower = trow >= tcol
        r2 = lax.broadcasted_iota(jnp.int32, (P2, P2), 0)
        c2 = lax.broadcasted_iota(jnp.int32, (P2, P2), 1)
        same_head = (r2 // N) == (c2 // N)
        eye2 = r2 == c2
        x2 = [jnp.concatenate([pb(at, pi), pb(rt, pi)], axis=0) for pi in pr]
        m_b = [_dot_nt(x2[pi], stack2(pb(bt, pi))) for pi in pr]
        m_k = [_dot_nt(x2[pi], stack2(pb(kt, pi))) for pi in pr]
        ap = [jnp.where(strict, m_b[pi][0:L], 0.0) for pi in pr]
        a_rb = [jnp.where(lower, m_b[pi][L:2 * L], 0.0) for pi in pr]
        a_ak = [jnp.where(strict, m_k[pi][0:L], 0.0) for pi in pr]
        a_rk = [jnp.where(lower, m_k[pi][L:2 * L], 0.0) for pi in pr]
        vp = [_dot(jnp.concatenate([a_ak[pi], a_rk[pi]], axis=0), stack2(pb(v_all, pi))) for pi in pr]
        kv = [_dot(pb(k_end, pi).T, pb(v_all, pi)) for pi in pr]
        x = [jnp.concatenate([pb(at, pi), vp[pi][0:L]], axis=1) for pi in pr]
        for it in range(6):
            x = [x[pi] + _dot(ap[pi], stack2(x[pi])) for pi in pr]
            if it < 5:
                ap = [_dot(ap[pi], stack2(ap[pi])) for pi in pr]
        post1 = [_dot(a_rb[pi], stack2(x[pi])) for pi in pr]
        post2 = [_dot(pb(b_end, pi).T, x[pi]) for pi in pr]
        lhs = []
        y0s = []
        h_adds = []
        for pi in pr:
            q_hat = pb(rt, pi) + post1[pi][:, 0:P2]
            gmat = (jnp.where(same_head, post2[pi][:, 0:P2], 0.0)
                    + jnp.where(eye2, pb(p_end, pi)[0:1, :], 0.0))
            lhs.append(jnp.concatenate([q_hat, gmat], axis=0))
            y0s.append(post1[pi][:, P2:2 * P2] + vp[pi][L:2 * L])
            h_adds.append(jnp.where(same_head, post2[pi][:, P2:2 * P2] + kv[pi], 0.0))
        npair = RW_HEADS // 2
        h_st = [h_ref[p] for p in range(npair)]
        ys = [None] * len(pairs)
        for cc in range(nck):
            res = [_dot(lhs[cc * npair + p], h_st[p]) for p in range(npair)]
            for p in range(npair):
                pi = cc * npair + p
                ys[pi] = res[p][0:L] + y0s[pi]
                h_st[p] = res[p][L:L + P2] + h_adds[pi]
        for p in range(npair):
            h_ref[p] = h_st[p]
        seg = same_head.astype(BF16)
        mean = [_seg_sum(ys[pi], seg) * (1.0 / N) for pi in pr]
        yc = [ys[pi] - mean[pi] for pi in pr]
        var = [_seg_sum(yc[pi] * yc[pi], seg) * (1.0 / N) for pi in pr]
        for pi, (cc, p) in enumerate(pairs):
            yn = yc[pi] * lax.rsqrt(var[pi] + RW_LN_EPS)
            rows = slice(cc * L, (cc + 1) * L)
            cols = slice(p * P2, (p + 1) * P2)
            out_ref[0, rows, cols] = ((yn * lng_ref[:, cols] + lnb_ref[:, cols] + bonus_ref[0, rows, cols])
                                      * g_ref[0, rows, cols]).astype(BF16)

    blk = pl.BlockSpec((1, tb, W), lambda b, c: (b, c, 0))
    vec = pl.BlockSpec((1, W), lambda b, c: (0, 0))
    return pl.pallas_call(
        body,
        grid=(bsz, t // tb),
        in_specs=[blk] * 8 + [vec, vec],
        out_specs=blk,
        out_shape=jax.ShapeDtypeStruct((bsz, t, W), BF16),
        scratch_shapes=[pltpu.VMEM((RW_HEADS // 2, 2 * N, 2 * N), F32)],
        compiler_params=_cp("parallel", "arbitrary"),
        name="rwkv_scan",
    )(r, lw, k, v, aa, bb, g_out, bonus, ln_g, ln_b)


def _mix_out_route(a, b, w_bf16, resid, g, router_split, tm=1024):
    n, wa = a.shape
    wb = b.shape[1]
    d = w_bf16.shape[1]
    e = router_split.shape[1] // 2

    def body(a_ref, b_ref, w_ref, r_ref, g_ref, rt_ref, h_ref, xn_ref, lg_ref):
        acc = jnp.dot(a_ref[...], w_ref[0:wa, :], preferred_element_type=F32)
        acc = acc + jnp.dot(b_ref[...], w_ref[wa:wa + wb, :], preferred_element_type=F32)
        h = r_ref[...] + acc
        h_ref[...] = h
        xn = _rms_rows(h, g_ref[...])
        x_hi = xn.astype(BF16)
        xn_ref[...] = x_hi
        x_lo = (xn - x_hi.astype(F32)).astype(BF16)
        hi_both = jnp.dot(x_hi, rt_ref[...], preferred_element_type=F32)
        lo_hi = jnp.dot(x_lo, rt_ref[:, 0:e], preferred_element_type=F32)
        lg_ref[...] = hi_both[:, 0:e] + (lo_hi + hi_both[:, e:2 * e])

    return pl.pallas_call(
        body,
        grid=(n // tm,),
        in_specs=[pl.BlockSpec((tm, wa), lambda i: (i, 0)),
                  pl.BlockSpec((tm, wb), lambda i: (i, 0)),
                  pl.BlockSpec((wa + wb, d), lambda i: (0, 0)),
                  pl.BlockSpec((tm, d), lambda i: (i, 0)),
                  pl.BlockSpec((1, d), lambda i: (0, 0)),
                  pl.BlockSpec((d, 2 * e), lambda i: (0, 0))],
        out_specs=[pl.BlockSpec((tm, d), lambda i: (i, 0)), pl.BlockSpec((tm, d), lambda i: (i, 0)),
                   pl.BlockSpec((tm, e), lambda i: (i, 0))],
        out_shape=[jax.ShapeDtypeStruct((n, d), F32), jax.ShapeDtypeStruct((n, d), BF16),
                   jax.ShapeDtypeStruct((n, e), F32)],
        compiler_params=_cp("parallel"),
        name="mix_out_route",
    )(a, b, w_bf16, resid, g.reshape(1, d), router_split)


MOE_TM = 512
MOE_TF = 1792


def _experts(xs, row_w, item_tile, item_exp, item_lo, item_hi, wg, wu, wd):
    nrows, d = xs.shape
    tm, tf = MOE_TM, MOE_TF
    nf = D_FF // tf
    n_items = item_tile.shape[0]

    def body(it_ref, ie_ref, lo_ref, hi_ref, x_ref, w_ref, wg_ref, wu_ref, wd_ref, o_ref, acc_ref):
        i = pl.program_id(0)
        j = pl.program_id(1)
        tile = it_ref[i]
        first = jnp.logical_or(i == 0, tile != it_ref[jnp.maximum(i - 1, 0)])
        last = jnp.logical_or(i == n_items - 1, tile != it_ref[jnp.minimum(i + 1, n_items - 1)])

        @pl.when(jnp.logical_and(first, j == 0))
        def _():
            acc_ref[...] = jnp.zeros_like(acc_ref)

        lo = lo_ref[i]
        hi = hi_ref[i]

        @pl.when(lo < hi)
        def _():
            x = x_ref[...]
            gg = jnp.dot(x, wg_ref[0], preferred_element_type=F32)
            uu = jnp.dot(x, wu_ref[0], preferred_element_type=F32)
            act = (_silu(gg) * uu).astype(BF16)
            part = jnp.dot(act, wd_ref[0], preferred_element_type=F32)
            rowi = lax.broadcasted_iota(jnp.int32, (tm, 1), 0)
            mine = jnp.logical_and(rowi >= lo, rowi < hi)
            acc_ref[...] += part * jnp.where(mine, w_ref[...], 0.0)

        @pl.when(jnp.logical_and(last, j == nf - 1))
        def _():
            o_ref[...] = acc_ref[...].astype(o_ref.dtype)

    grid_spec = pltpu.PrefetchScalarGridSpec(
        num_scalar_prefetch=4,
        grid=(n_items, nf),
        in_specs=[pl.BlockSpec((tm, d), lambda i, j, it, ie, lo, hi: (it[i], 0)),
                  pl.BlockSpec((tm, 1), lambda i, j, it, ie, lo, hi: (it[i], 0)),
                  pl.BlockSpec((1, d, tf), lambda i, j, it, ie, lo, hi: (ie[i], 0, j)),
                  pl.BlockSpec((1, d, tf), lambda i, j, it, ie, lo, hi: (ie[i], 0, j)),
                  pl.BlockSpec((1, tf, d), lambda i, j, it, ie, lo, hi: (ie[i], j, 0))],
        out_specs=pl.BlockSpec((tm, d), lambda i, j, it, ie, lo, hi: (it[i], 0)),
        scratch_shapes=[pltpu.VMEM((tm, d), F32)],
    )
    return pl.pallas_call(
        body,
        grid_spec=grid_spec,
        out_shape=jax.ShapeDtypeStruct((nrows, d), BF16),
        compiler_params=_cp("arbitrary", "arbitrary"),
        name="moe_experts",
    )(item_tile, item_exp, item_lo, item_hi, xs, row_w, wg, wu, wd)


def _combine_norm(h, y0, y1, g, tm=512):
    n, d = h.shape

    def body(h_ref, a_ref, b_ref, g_ref, o_ref):
        o_ref[...] = _rms_rows(h_ref[...] + (a_ref[...].astype(F32) + b_ref[...].astype(F32)), g_ref[...])

    blk = pl.BlockSpec((tm, d), lambda i: (i, 0))
    return pl.pallas_call(
        body,
        grid=(n // tm,),
        in_specs=[blk, blk, blk, pl.BlockSpec((1, d), lambda i: (0, 0))],
        out_specs=blk,
        out_shape=jax.ShapeDtypeStruct((n, d), F32),
        compiler_params=_cp("parallel"),
        name="combine_norm",
    )(h, y0, y1, g.reshape(1, d))


def _route(logits, n):
    tm = MOE_TM
    na = n * TOP_K
    n_tiles = na // tm
    top_val, top_idx = lax.top_k(logits, TOP_K)
    top_w = jax.nn.softmax(top_val, axis=-1)
    e_flat = top_idx.reshape(-1).astype(jnp.int32)
    w_flat = top_w.reshape(-1)
    tok = jnp.arange(na, dtype=jnp.int32) // TOP_K
    _, sorted_tok, sorted_w = lax.sort((e_flat, tok, w_flat), num_keys=1, is_stable=True)
    onehot = (e_flat[:, None] == jnp.arange(N_EXPERTS, dtype=jnp.int32)[None, :]).astype(jnp.int32)
    rank = jnp.take_along_axis(jnp.cumsum(onehot, axis=0), e_flat[:, None], axis=1)[:, 0] - 1
    counts = jnp.sum(onehot, axis=0)
    ends = jnp.cumsum(counts)
    pos = (ends - counts)[e_flat] + rank
    cuts = jnp.sort(jnp.concatenate([jnp.arange(n_tiles, dtype=jnp.int32) * tm, ends[:-1].astype(jnp.int32)]))
    nxt = jnp.concatenate([cuts[1:], jnp.full((1,), na, jnp.int32)])
    item_tile = jnp.minimum(cuts // tm, n_tiles - 1)
    item_exp = jnp.minimum(jnp.searchsorted(ends, cuts, side="right"), N_EXPERTS - 1).astype(jnp.int32)
    item_lo = cuts - item_tile * tm
    item_hi = nxt - item_tile * tm
    return sorted_tok, sorted_w, (item_tile, item_exp, item_lo, item_hi), pos.reshape(n, TOP_K)


def kernel(x, e_norm1_g, e_w_in, e_ml_conv_w, e_ml_conv_b, e_ml_gate_b, e_ml_norm_g, e_gla_gate_up, e_gla_gate_b,
           e_gla_norm_g, e_w_out, e_norm2_g, e_ffn_w_gate, e_ffn_w_up, e_ffn_w_down, o_norm1_g, o_w_in,
           o_ret_norm_g, o_rw_mu, o_rw_w_up, o_rw_w0, o_rw_a_up, o_rw_a0, o_rw_g_up, o_rw_k_k, o_rw_k_a, o_rw_r_k,
           o_rw_ln_g, o_rw_ln_b, o_w_out, o_norm2_g, o_moe_router, o_moe_w_gate, o_moe_w_up, o_moe_w_down,
           final_norm_g):
    bsz, t, d = x.shape
    n = bsz * t
    h0 = x.reshape(n, d)

    w = e_w_in[0]
    row = lambda a: a.reshape(1, -1)
    x3 = x
    ng1 = row(e_norm1_g[0])
    w_if = w[:, 2048:2056]
    w_gate = jnp.zeros((d, 128), F32).at[:, :2 * ML_HEADS].set(w_if).astype(BF16)
    w_gate_t = jnp.zeros((16, d), F32).at[:2 * ML_HEADS, :].set(w_if.T).astype(BF16)
    h_ml = _mlstm(x3, ng1, w[:, :4 * ML_W].astype(BF16), w_gate, w_gate_t, row(e_ml_gate_b[0]),
                  e_ml_gate_b[0].reshape(-1, 1), e_ml_conv_w[0], row(e_ml_conv_b[0]), row(e_ml_norm_g[0]))
    gq, gk, gv, gr = 2056, 2312, 2568, 3080
    w_pairs = jnp.stack([jnp.concatenate([w[:, gq + 128 * hp:gq + 128 * (hp + 1)],
                                          w[:, gk + 128 * hp:gk + 128 * (hp + 1)],
                                          w[:, gv + 256 * hp:gv + 256 * (hp + 1)],
                                          w[:, gr + 256 * hp:gr + 256 * (hp + 1)]], axis=1)
                         for hp in range(GLA_HEADS // 2)]).astype(BF16)
    w_low = jnp.zeros((d, 128), F32).at[:, :GLA_RANK].set(w[:, 3592:3608]).astype(BF16)
    o_gla = _gla(x3, ng1, w_pairs, w_low, e_gla_gate_up[0].astype(BF16), row(e_gla_gate_b[0]),
                 row(e_gla_norm_g[0]))
    h2 = _mix_out_ffn(h_ml.reshape(n, -1), o_gla.reshape(n, -1), e_w_out[0].astype(BF16), h0, e_norm2_g[0],
                      e_ffn_w_gate[0].astype(BF16), e_ffn_w_up[0].astype(BF16), e_ffn_w_down[0].astype(BF16))

    w = o_w_in[0]
    h2_3 = h2.reshape(bsz, t, d)
    ng2 = row(o_norm1_g[0])
    y_ret = _retention(h2_3, ng2, w[:, :4 * RET_W].astype(BF16), *_retention_tables(t), row(o_ret_norm_g[0]))
    head_of = jnp.arange(RW_W) // RW_DIM
    bd = (head_of[:, None] == head_of[None, :]).astype(BF16)
    r, lw, k, v, aa, bb, g_out, bonus = _rwkv_prep(
        h2_3, ng2, w[:, 4 * RET_W:].astype(BF16), row(o_rw_mu[0]), o_rw_w_up[0].astype(BF16), row(o_rw_w0[0]),
        o_rw_a_up[0].astype(BF16), row(o_rw_a0[0]), o_rw_g_up[0].astype(BF16), row(o_rw_k_k[0]),
        row(o_rw_k_a[0]), row(o_rw_r_k[0]), bd)
    y_rw = _rwkv_scan(r, lw, k, v, aa, bb, g_out, bonus, row(o_rw_ln_g[0]), row(o_rw_ln_b[0]))
    router_pad = jnp.zeros((d, 128), F32).at[:, :N_EXPERTS].set(o_moe_router[0])
    router_hi = router_pad.astype(BF16)
    router_split = jnp.concatenate([router_hi, (router_pad - router_hi.astype(F32)).astype(BF16)], axis=1)
    h3, xn, logits = _mix_out_route(y_ret.reshape(n, -1), y_rw.reshape(n, -1), o_w_out[0].astype(BF16), h2,
                                    o_norm2_g[0], router_split)
    sorted_tok, sorted_w, items, pos = _route(logits[:, :N_EXPERTS], n)
    xs = xn.at[sorted_tok].get(mode="promise_in_bounds")
    ys = _experts(xs, sorted_w.reshape(-1, 1), *items, o_moe_w_gate[0].astype(BF16),
                  o_moe_w_up[0].astype(BF16), o_moe_w_down[0].astype(BF16))
    y0 = ys.at[pos[:, 0]].get(mode="promise_in_bounds")
    y1 = ys.at[pos[:, 1]].get(mode="promise_in_bounds")
    out = _combine_norm(h3, y0, y1, final_norm_g)
    return out.reshape(bsz, t, d)
```

```python
import functools

import numpy as np
import jax
import jax.numpy as jnp
from jax import lax
from jax.experimental import pallas as pl
from jax.experimental.pallas import tpu as pltpu

F32 = jnp.float32
BF16 = jnp.bfloat16

D_MODEL = 1024
EPS = 1e-6
ML_HEADS, ML_DIM, ML_W, ML_CONV = 4, 128, 512, 4
GLA_HEADS, GLA_DK, GLA_DV, GLA_RANK, GLA_TAU = 4, 64, 128, 16, 16.0
RET_HEADS, RET_DIM, RET_W = 4, 128, 512
ROPE_BASE = 10000.0
RW_HEADS, RW_DIM, RW_W = 8, 64, 512
RW_COLS = 1792
RW_LN_EPS = 64e-5
D_FF = 3584
N_EXPERTS = 8
TOP_K = 2

VMEM_LIMIT = 48 * 1024 * 1024
NEG = -1e30


def _cp(*sem):
    return pltpu.CompilerParams(dimension_semantics=sem, vmem_limit_bytes=VMEM_LIMIT)


def _sigmoid(x):
    return 1.0 / (1.0 + jnp.exp(-x))


def _silu(x):
    return x * _sigmoid(x)


def _log_sigmoid(x):
    return jnp.minimum(x, 0.0) - jnp.log(1.0 + jnp.exp(-jnp.abs(x)))


def _dot(a, b):
    return jnp.dot(a.astype(BF16), b.astype(BF16), preferred_element_type=F32)


def _dot_nt(a, b):
    return lax.dot_general(a.astype(BF16), b.astype(BF16), (((1,), (1,)), ((), ())), preferred_element_type=F32)


def _dot_tn(a, b):
    return jnp.dot(a.T.astype(BF16), b.astype(BF16), preferred_element_type=F32)


def _rms_rows(x, g):
    ms = jnp.mean(x * x, axis=-1, keepdims=True)
    return x * lax.rsqrt(ms + EPS) * g


def _mix_out_ffn(a, b, w_out, resid, g, wg, wu, wd, tm=512, tf=1792):
    n, d = resid.shape
    wa = a.shape[1]
    wb = b.shape[1]
    f = wg.shape[1]
    nf = f // tf

    def body(a_ref, b_ref, wo_ref, r_ref, g_ref, wg_ref, wu_ref, wd_ref, o_ref, xn_ref, acc_ref):
        j = pl.program_id(1)

        @pl.when(j == 0)
        def _():
            h = r_ref[...] + jnp.dot(a_ref[...], wo_ref[0:wa, :], preferred_element_type=F32)
            h = h + jnp.dot(b_ref[...], wo_ref[wa:wa + wb, :], preferred_element_type=F32)
            xn_ref[...] = _rms_rows(h, g_ref[...]).astype(BF16)
            acc_ref[...] = h

        xn = xn_ref[...]
        gg = jnp.dot(xn, wg_ref[...], preferred_element_type=F32)
        uu = jnp.dot(xn, wu_ref[...], preferred_element_type=F32)
        act = (_silu(gg) * uu).astype(BF16)
        acc_ref[...] += jnp.dot(act, wd_ref[...], preferred_element_type=F32)

        @pl.when(j == nf - 1)
        def _():
            o_ref[...] = acc_ref[...]

    return pl.pallas_call(
        body,
        grid=(n // tm, nf),
        in_specs=[pl.BlockSpec((tm, wa), lambda i, j: (i, 0)),
                  pl.BlockSpec((tm, wb), lambda i, j: (i, 0)),
                  pl.BlockSpec((wa + wb, d), lambda i, j: (0, 0)),
                  pl.BlockSpec((tm, d), lambda i, j: (i, 0)),
                  pl.BlockSpec((1, d), lambda i, j: (0, 0)),
                  pl.BlockSpec((d, tf), lambda i, j: (0, j)),
                  pl.BlockSpec((d, tf), lambda i, j: (0, j)),
                  pl.BlockSpec((tf, d), lambda i, j: (j, 0))],
        out_specs=pl.BlockSpec((tm, d), lambda i, j: (i, 0)),
        out_shape=jax.ShapeDtypeStruct((n, d), F32),
        scratch_shapes=[pltpu.VMEM((tm, d), BF16), pltpu.VMEM((tm, d), F32)],
        compiler_params=_cp("parallel", "arbitrary"),
        name="mix_out_ffn",
    )(a, b, w_out, resid, g.reshape(1, d), wg, wu, wd)


ML_TB = 1024
ML_L = 256


def _mlstm(x, in_g, w_in, w_gate, w_gate_t, gate_b_row, gate_b_col, conv_w, conv_b, norm_g):
    bsz, t, d = x.shape
    TB, L, D, H, W = ML_TB, ML_L, ML_DIM, ML_HEADS, ML_W

    def body(x_ref, ing_ref, w_ref, wg_ref, wgt_ref, gbr_ref, gbc_ref, cwq_ref, cwk_ref, cbq_ref, cbk_ref,
             ng_ref, out_ref, qext, kext, c_ref, n_ref, m_ref):
        c = pl.program_id(1)

        @pl.when(c == 0)
        def _():
            qext[0:8, :] = jnp.zeros((8, W), F32)
            kext[0:8, :] = jnp.zeros((8, W), F32)
            c_ref[...] = jnp.zeros_like(c_ref)
            n_ref[...] = jnp.zeros_like(n_ref)
            m_ref[...] = jnp.zeros_like(m_ref)

        def project(cc):
            xn = _rms_rows(x_ref[0, cc * L:(cc + 1) * L, :], ing_ref[...]).astype(BF16)
            pc = jnp.dot(xn, w_ref[...], preferred_element_type=F32)
            gc = jnp.dot(xn, wg_ref[...], preferred_element_type=F32)[:, 0:2 * H]
            gr = lax.dot_general(wgt_ref[...], xn, (((1,), (1,)), ((), ())),
                                 preferred_element_type=F32)[0:2 * H, :]
            return pc, gc, gr

        def conv(ext, base, cw_ref, cb_ref):
            acc = cb_ref[...] + cw_ref[0:1, :] * ext[pl.ds(base + 8 - ML_CONV + 1, L), :]
            for kk in range(1, ML_CONV):
                acc = acc + cw_ref[kk:kk + 1, :] * ext[pl.ds(base + 8 - ML_CONV + 1 + kk, L), :]
            return _silu(acc)

        ri = lax.broadcasted_iota(jnp.int32, (L, L), 0)
        ci = lax.broadcasted_iota(jnp.int32, (L, L), 1)
        causal = ri >= ci
        heads = range(H)
        c_st = [c_ref[h] for h in heads]
        n_st = [n_ref[h] for h in heads]
        m_st = [m_ref[h] for h in heads]
        nxt = project(0)
        for cc in range(TB // L):
            rows = slice(cc * L, (cc + 1) * L)
            p, g_cols, g_rows = nxt
            if cc + 1 < TB // L:
                nxt = project(cc + 1)
            qext[8 + cc * L:8 + (cc + 1) * L, :] = p[:, 0:W]
            kext[8 + cc * L:8 + (cc + 1) * L, :] = p[:, W:2 * W]
            q_all = conv(qext, cc * L, cwq_ref, cbq_ref) * (D ** -0.5)
            k_all = conv(kext, cc * L, cwk_ref, cbk_ref)
            gcol = g_cols + gbr_ref[...]
            grow = g_rows + gbc_ref[...]
            fcol = _log_sigmoid(gcol[:, H:2 * H])
            frow = _log_sigmoid(grow[H:2 * H, :])
            hs = lambda z, h: z[:, h * D:(h + 1) * D]
            w_intra, w_inter, w_state, carry, m_row = [], [], [], [], []
            for h in heads:
                f_row = frow[h:h + 1, :]
                i_row = grow[h:h + 1, :]
                f_col = fcol[:, h:h + 1]
                i_col = gcol[:, h:h + 1]
                b_col = jnp.sum(jnp.where(causal, f_row, 0.0), axis=1, keepdims=True)
                b_row = jnp.sum(jnp.where(ri <= ci, f_col, 0.0), axis=0, keepdims=True)
                g_tot = jnp.sum(f_row, axis=1, keepdims=True)
                d_intra = jnp.where(causal, b_col - b_row + i_row, NEG)
                d_inter = b_col + m_st[h]
                mr = jnp.maximum(d_inter, jnp.max(d_intra, axis=1, keepdims=True))
                w_intra.append(jnp.exp(d_intra - mr))
                w_inter.append(jnp.exp(d_inter - mr))
                m_row.append(mr)
                d_state = g_tot - b_col + i_col
                mn = jnp.maximum(g_tot + m_st[h], jnp.max(d_state, axis=0, keepdims=True))
                w_state.append(jnp.exp(d_state - mn))
                carry.append(jnp.exp(g_tot + m_st[h] - mn))
                m_st[h] = mn
            qh = [hs(q_all, h) for h in heads]
            kh = [hs(k_all, h) for h in heads]
            vh = [p[:, 2 * W + h * D:2 * W + (h + 1) * D] for h in heads]
            s = [_dot_nt(qh[h], kh[h]) * w_intra[h] for h in heads]
            qc = [_dot(qh[h], c_st[h]) for h in heads]
            kw = [kh[h] * w_state[h] for h in heads]
            kv = [_dot_tn(kw[h], vh[h]) for h in heads]
            sv = [_dot(s[h], vh[h]) for h in heads]
            for h in heads:
                num = sv[h] + w_inter[h] * qc[h]
                den = (jnp.sum(s[h], axis=1, keepdims=True)
                       + w_inter[h] * jnp.sum(qh[h] * n_st[h], axis=1, keepdims=True))
                hval = num / jnp.maximum(jnp.abs(den), jnp.exp(-m_row[h]))
                c_st[h] = carry[h] * c_st[h] + kv[h]
                n_st[h] = carry[h] * n_st[h] + jnp.sum(kw[h], axis=0, keepdims=True)
                hg = _sigmoid(p[:, 3 * W + h * D:3 * W + (h + 1) * D]) * hval
                hc = hg - jnp.mean(hg, axis=1, keepdims=True)
                hn = hc * lax.rsqrt(jnp.mean(hc * hc, axis=1, keepdims=True) + EPS)
                out_ref[0, rows, h * D:(h + 1) * D] = (hn * ng_ref[:, h * D:(h + 1) * D]).astype(BF16)
        for h in heads:
            c_ref[h] = c_st[h]
            n_ref[h] = n_st[h]
            m_ref[h] = m_st[h]
        qext[0:8, :] = qext[TB:TB + 8, :]
        kext[0:8, :] = kext[TB:TB + 8, :]

    fix = lambda j: (lambda b, c: (0, j))
    return pl.pallas_call(
        body,
        grid=(bsz, t // TB),
        in_specs=[pl.BlockSpec((1, TB, d), lambda b, c: (b, c, 0)),
                  pl.BlockSpec((1, d), fix(0)),
                  pl.BlockSpec((d, 4 * W), fix(0)),
                  pl.BlockSpec((d, 128), fix(0)),
                  pl.BlockSpec((16, d), fix(0)),
                  pl.BlockSpec((1, 2 * H), fix(0)),
                  pl.BlockSpec((2 * H, 1), fix(0)),
                  pl.BlockSpec((ML_CONV, W), fix(0)),
                  pl.BlockSpec((ML_CONV, W), fix(1)),
                  pl.BlockSpec((1, W), fix(0)),
                  pl.BlockSpec((1, W), fix(1)),
                  pl.BlockSpec((1, W), fix(0))],
        out_specs=pl.BlockSpec((1, TB, W), lambda b, c: (b, c, 0)),
        out_shape=jax.ShapeDtypeStruct((bsz, t, W), BF16),
        scratch_shapes=[pltpu.VMEM((TB + 8, W), F32), pltpu.VMEM((TB + 8, W), F32),
                        pltpu.VMEM((H, D, D), F32), pltpu.VMEM((H, 1, D), F32), pltpu.VMEM((H, 1, 1), F32)],
        compiler_params=_cp("parallel", "arbitrary"),
        name="mlstm",
    )(x, in_g, w_in, w_gate, w_gate_t, gate_b_row, gate_b_col, conv_w, conv_w, conv_b, conv_b, norm_g)


GLA_TC = 1024
GLA_SUB = 16
GLA_GROUP = 128


def _gla(x, in_g, w_pairs, w_low, gate_up, gate_b, norm_g):
    bsz, t, d = x.shape
    tc, S, GB = GLA_TC, GLA_SUB, GLA_GROUP
    head_ones = (jnp.arange(2 * GLA_DK)[:, None] // GLA_DK == jnp.arange(2 * GB)[None, :] // GB).astype(BF16)
    nsub = tc // S
    dk, dv = GLA_DK, GLA_DV

    def body(x_ref, ing_ref, w_ref, wl_ref, gu_ref, gbias_ref, ng_ref, ones_ref, out_ref, st_ref, ksh, bsh):
        c = pl.program_id(2)

        @pl.when(c == 0)
        def _():
            st_ref[...] = jnp.zeros_like(st_ref)
            ksh[0:S, :] = jnp.zeros((S, 2 * dk), F32)
            bsh[0:S, :] = jnp.zeros((S, 2 * dk), F32)

        xn = _rms_rows(x_ref[0], ing_ref[...]).astype(BF16)
        p = jnp.dot(xn, w_ref[0], preferred_element_type=F32)
        g_low = jnp.dot(xn, wl_ref[...], preferred_element_type=F32)[:, 0:GLA_RANK]
        z = _dot(g_low, gu_ref[...]) + gbias_ref[...]
        la = _log_sigmoid(z) / GLA_TAU
        rowi = lax.broadcasted_iota(jnp.int32, (tc, 1), 0)
        rmod = rowi & (S - 1)
        bcum = la
        rsum = la
        for s in (1, 2, 4, 8):
            bcum = bcum + jnp.where(rmod >= s, pltpu.roll(bcum, s, 0), 0.0)
            rsum = rsum + jnp.where(rmod < S - s, pltpu.roll(rsum, tc - s, 0), 0.0)
        q = p[:, 0:2 * dk] * (dk ** -0.5)
        k = p[:, 2 * dk:4 * dk]
        v = p[:, 4 * dk:4 * dk + 2 * dv]
        gate = p[:, 4 * dk + 2 * dv:4 * dk + 4 * dv]
        qt = q * jnp.exp(bcum)
        kt = k * jnp.exp(rsum - la)
        eg = jnp.exp(bcum + rsum - la)

        ksh[S:, :] = k
        bsh[S:, :] = bcum
        prods = []
        for d in range(S):
            kd = k if d == 0 else ksh[pl.ds(S - d, tc), :]
            bd = bcum if d == 0 else bsh[pl.ds(S - d, tc), :]
            e = jnp.exp(jnp.where(rmod >= d, bcum - bd, 0.0))
            prods.append((q * kd * e).astype(BF16))
        ws = [jnp.dot(p, ones_ref[...], preferred_element_type=F32) for p in prods]
        coli = lax.broadcasted_iota(jnp.int32, (tc, GB), 1)
        rgrp = rowi & (GB - 1)
        att0 = jnp.zeros((tc, GB), F32)
        att1 = jnp.zeros((tc, GB), F32)
        offs = jnp.where((coli // S) == (rgrp // S), rgrp - coli, -1)
        for d in range(S):
            here = offs == d
            att0 = jnp.where(here, ws[d][:, 0:GB], att0)
            att1 = jnp.where(here, ws[d][:, GB:2 * GB], att1)

        heads = range(2)
        lk = [slice(hh * dk, (hh + 1) * dk) for hh in heads]
        lv = [slice(hh * dv, (hh + 1) * dv) for hh in heads]
        kv = [[_dot_tn(v[si * S:(si + 1) * S, lv[hh]], kt[si * S:(si + 1) * S, lk[hh]]) for hh in heads]
              for si in range(nsub)]
        st = [st_ref[hh] for hh in heads]
        inter = [[], []]
        for si in range(nsub):
            rows = slice(si * S, (si + 1) * S)
            for hh in heads:
                inter[hh].append(_dot_nt(qt[rows, lk[hh]], st[hh]))
                st[hh] = st[hh] * eg[si * S:si * S + 1, lk[hh]] + kv[si][hh]
        for hh in heads:
            st_ref[hh] = st[hh]

        for hh, att in ((0, att0), (1, att1)):
            diag = jnp.concatenate([_dot(att[g * GB:(g + 1) * GB], v[g * GB:(g + 1) * GB, lv[hh]])
                                    for g in range(tc // GB)], axis=0)
            o = diag + jnp.concatenate(inter[hh], axis=0)
            on = o * lax.rsqrt(jnp.mean(o * o, axis=1, keepdims=True) + EPS)
            out_ref[0, :, lv[hh]] = (on * ng_ref[:, lv[hh]] * _silu(gate[:, lv[hh]])).astype(BF16)

    pw = 4 * dk + 4 * dv
    return pl.pallas_call(
        body,
        grid=(bsz, GLA_HEADS // 2, t // tc),
        in_specs=[pl.BlockSpec((1, tc, d), lambda b, h, c: (b, c, 0)),
                  pl.BlockSpec((1, d), lambda b, h, c: (0, 0)),
                  pl.BlockSpec((1, d, pw), lambda b, h, c: (h, 0, 0)),
                  pl.BlockSpec((d, 128), lambda b, h, c: (0, 0)),
                  pl.BlockSpec((GLA_RANK, 2 * dk), lambda b, h, c: (0, h)),
                  pl.BlockSpec((1, 2 * dk), lambda b, h, c: (0, h)),
                  pl.BlockSpec((1, 2 * dv), lambda b, h, c: (0, h)),
                  pl.BlockSpec((2 * dk, 2 * GB), lambda b, h, c: (0, 0))],
        out_specs=pl.BlockSpec((1, tc, 2 * dv), lambda b, h, c: (b, c, h)),
        out_shape=jax.ShapeDtypeStruct((bsz, t, GLA_HEADS * dv), BF16),
        scratch_shapes=[pltpu.VMEM((2, dv, dk), F32), pltpu.VMEM((tc + S, 2 * dk), F32),
                        pltpu.VMEM((tc + S, 2 * dk), F32)],
        compiler_params=_cp("parallel", "parallel", "arbitrary"),
        name="gla",
    )(x, in_g, w_pairs, w_low, gate_up, gate_b, norm_g, head_ones)


RET_TB = 1024
RET_L = 256


def _retention(x, in_g, w_in, cos_t, sin_t, intra, inter, sdec, cdec, norm_g):
    bsz, t, d = x.shape
    TB, L, D, H, W = RET_TB, RET_L, RET_DIM, RET_HEADS, RET_W

    def body(x_ref, ing_ref, w_ref, cos_ref, sin_ref, intra_ref, inter_ref, sdec_ref, cdec_ref, ng_ref,
             out_ref, s_ref):
        c = pl.program_id(1)

        @pl.when(c == 0)
        def _():
            s_ref[...] = jnp.zeros_like(s_ref)

        p = jnp.dot(_rms_rows(x_ref[0], ing_ref[...]).astype(BF16), w_ref[...], preferred_element_type=F32)
        heads = range(H)
        s_st = [s_ref[h] for h in heads]
        for cc in range(TB // L):
            rows = slice(cc * L, (cc + 1) * L)
            cs = cos_ref[rows, :]
            sn = sin_ref[rows, :]

            def rot(z):
                return z * cs + pltpu.roll(z, D // 2, 1) * sn

            q = [rot(p[rows, h * D:(h + 1) * D]) * (D ** -0.5) for h in heads]
            k = [rot(p[rows, W + h * D:W + (h + 1) * D]) for h in heads]
            v = [p[rows, 2 * W + h * D:2 * W + (h + 1) * D] for h in heads]
            s = [_dot_nt(q[h], k[h]) * intra_ref[h] for h in heads]
            qs = [_dot(q[h], s_st[h]) for h in heads]
            kv = [_dot_tn(k[h] * sdec_ref[h], v[h]) for h in heads]
            sv = [_dot(s[h], v[h]) for h in heads]
            for h in heads:
                o = sv[h] + inter_ref[h] * qs[h]
                s_st[h] = cdec_ref[h] * s_st[h] + kv[h]
                oc = o - jnp.mean(o, axis=1, keepdims=True)
                on = oc * lax.rsqrt(jnp.mean(oc * oc, axis=1, keepdims=True) + EPS)
                hl = slice(h * D, (h + 1) * D)
                gate = p[rows, 3 * W + h * D:3 * W + (h + 1) * D]
                out_ref[0, rows, hl] = (on * ng_ref[:, hl] * _silu(gate)).astype(BF16)
        for h in heads:
            s_ref[h] = s_st[h]

    fix2 = lambda b, c: (0, 0)
    fix3 = lambda b, c: (0, 0, 0)
    return pl.pallas_call(
        body,
        grid=(bsz, t // TB),
        in_specs=[pl.BlockSpec((1, TB, d), lambda b, c: (b, c, 0)),
                  pl.BlockSpec((1, d), fix2),
                  pl.BlockSpec((d, 4 * W), fix2),
                  pl.BlockSpec((TB, D), lambda b, c: (c, 0)),
                  pl.BlockSpec((TB, D), lambda b, c: (c, 0)),
                  pl.BlockSpec((H, L, L), fix3),
                  pl.BlockSpec((H, L, 1), fix3),
                  pl.BlockSpec((H, L, 1), fix3),
                  pl.BlockSpec((H, 1, 1), fix3),
                  pl.BlockSpec((1, W), lambda b, c: (0, 0))],
        out_specs=pl.BlockSpec((1, TB, W), lambda b, c: (b, c, 0)),
        out_shape=jax.ShapeDtypeStruct((bsz, t, W), BF16),
        scratch_shapes=[pltpu.VMEM((H, D, D), F32)],
        compiler_params=_cp("parallel", "arbitrary"),
        name="retention",
    )(x, in_g, w_in, cos_t, sin_t, intra, inter, sdec, cdec, norm_g)


def _retention_tables(t):
    L, D = RET_L, RET_DIM
    inv = ROPE_BASE ** (-jnp.arange(0, D, 2, dtype=F32) / D)
    ang = jnp.arange(t).astype(F32)[:, None] * inv[None, :]
    cos = jnp.cos(ang)
    sin = jnp.sin(ang)
    cos_t = jnp.concatenate([cos, cos], axis=-1)
    sin_t = jnp.concatenate([-sin, sin], axis=-1)
    log_gamma = jnp.log1p(-jnp.exp2(-5.0 - jnp.arange(RET_HEADS, dtype=F32)))
    idx = jnp.arange(L, dtype=F32)
    causal = idx[:, None] >= idx[None, :]
    rel = jnp.where(causal, idx[:, None] - idx[None, :], 0.0)
    intra = jnp.where(causal, jnp.exp(log_gamma[:, None, None] * rel), 0.0)
    inter = jnp.exp(log_gamma[:, None] * (idx + 1.0))[:, :, None]
    sdec = jnp.exp(log_gamma[:, None] * (L - 1.0 - idx))[:, :, None]
    cdec = jnp.exp(log_gamma * L)[:, None, None]
    return cos_t, sin_t, intra, inter, sdec, cdec


RWP_TM = 1024
RWP_SUB = 256


def _seg_sum(x, bd):
    hi = x.astype(BF16)
    lo = (x - hi.astype(F32)).astype(BF16)
    return jnp.dot(hi, bd, preferred_element_type=F32) + jnp.dot(lo, bd, preferred_element_type=F32)


def _rwkv_prep(x, in_g, w_in, mu, w_up, w0, a_up, a0, g_up, k_k, k_a, r_k, bd):
    bsz, t, d = x.shape
    tm, hm, W = RWP_TM, RWP_SUB, RW_W

    def body(x_ref, ing_ref, win_ref, mu_ref, wup_ref, w0_ref, aup_ref, a0_ref, gup_ref, kk_ref, ka_ref, rk_ref,
             bd_ref, r_out, lw_out, k_out, v_out, a_out, b_out, g_out, bonus_out, last_ref):
        c = pl.program_id(1)

        @pl.when(c == 0)
        def _():
            last_ref[...] = jnp.zeros_like(last_ref)

        def project(i):
            xs = x_ref[0, i * hm:(i + 1) * hm, :]
            return jnp.dot(_rms_rows(xs, ing_ref[...]).astype(BF16), win_ref[...], preferred_element_type=F32)

        rowi = lax.broadcasted_iota(jnp.int32, (hm, 1), 0)
        prev = last_ref[...]
        nxt = project(0)
        for i in range(tm // hm):
            rows = slice(i * hm, (i + 1) * hm)
            cur = nxt
            if i + 1 < tm // hm:
                nxt = project(i + 1)
            sh = jnp.where(rowi == 0, prev, pltpu.roll(cur, 1, 0))
            prev = cur[hm - 1:hm, :]
            xm = cur + (sh - cur) * mu_ref[...]
            x_r = xm[:, 0:W]
            x_k = xm[:, W:2 * W]
            x_v = xm[:, 2 * W:3 * W]
            x_dl = xm[:, 3 * W:3 * W + 64]
            x_al = xm[:, 3 * W + 64:3 * W + 128]
            x_gl = xm[:, 3 * W + 128:3 * W + 256]
            wl = w0_ref[...] + _dot(jnp.tanh(x_dl), wup_ref[...])
            sp = jnp.maximum(-wl, 0.0) + jnp.log(1.0 + jnp.exp(-jnp.abs(wl)))
            lw_out[0, rows, :] = -jnp.exp(-sp - 0.5)
            a = _sigmoid(a0_ref[...] + _dot(x_al, aup_ref[...]))
            g_out[0, rows, :] = _dot(_sigmoid(x_gl), gup_ref[...]).astype(BF16)
            kk0 = x_k * kk_ref[...]
            nrm = jnp.sqrt(_seg_sum(kk0 * kk0, bd_ref[...]))
            kk = kk0 / jnp.maximum(nrm, 1e-12)
            k_h = x_k * (1.0 + (a - 1.0) * ka_ref[...])
            r_out[0, rows, :] = x_r.astype(BF16)
            k_out[0, rows, :] = k_h.astype(BF16)
            v_out[0, rows, :] = x_v.astype(BF16)
            a_out[0, rows, :] = (-kk).astype(BF16)
            b_out[0, rows, :] = (kk * a).astype(BF16)
            bonus_out[0, rows, :] = (_dot(x_r * k_h * rk_ref[...], bd_ref[...]) * x_v).astype(BF16)
        last_ref[...] = prev

    row = lambda b, c: (0, 0)
    blk = pl.BlockSpec((1, tm, W), lambda b, c: (b, c, 0))
    shp = [jax.ShapeDtypeStruct((bsz, t, W), F32 if i == 1 else BF16) for i in range(8)]
    return pl.pallas_call(
        body,
        grid=(bsz, t // tm),
        in_specs=[pl.BlockSpec((1, tm, d), lambda b, c: (b, c, 0)),
                  pl.BlockSpec((1, d), row),
                  pl.BlockSpec((d, RW_COLS), row),
                  pl.BlockSpec((1, RW_COLS), row),
                  pl.BlockSpec((64, W), row), pl.BlockSpec((1, W), row),
                  pl.BlockSpec((64, W), row), pl.BlockSpec((1, W), row),
                  pl.BlockSpec((128, W), row),
                  pl.BlockSpec((1, W), row), pl.BlockSpec((1, W), row), pl.BlockSpec((1, W), row),
                  pl.BlockSpec((W, W), row)],
        out_specs=[blk] * 8,
        out_shape=shp,
        scratch_shapes=[pltpu.VMEM((1, RW_COLS), F32)],
        compiler_params=_cp("parallel", "arbitrary"),
        name="rwkv_prep",
    )(x, in_g, w_in, mu, w_up, w0, a_up, a0, g_up, k_k, k_a, r_k, bd)


RW_L = 64
RW_TB = 256


def _rwkv_scan(r, lw, k, v, aa, bb, g_out, bonus, ln_g, ln_b):
    bsz, t, W = r.shape
    L, N, tb = RW_L, RW_DIM, RW_TB
    nck = tb // L

    def body(r_ref, lw_ref, k_ref, v_ref, a_ref, b_ref, g_ref, bonus_ref, lng_ref, lnb_ref, out_ref, h_ref):
        c = pl.program_id(1)

        @pl.when(c == 0)
        def _():
            h_ref[...] = jnp.zeros_like(h_ref)

        lw_all = lw_ref[0]
        rowi = lax.broadcasted_iota(jnp.int32, (tb, 1), 0) & (L - 1)
        cl = lw_all
        for s in (1, 2, 4, 8, 16, 32):
            cl = cl + jnp.where(rowi >= s, pltpu.roll(cl, s, 0), 0.0)
        cl_last = jnp.concatenate([jnp.broadcast_to(cl[(cc + 1) * L - 1:(cc + 1) * L, :], (L, W))
                                   for cc in range(nck)], axis=0)
        e_inv = jnp.exp(-cl)
        e_end = jnp.exp(cl_last - cl)
        p_end = jnp.exp(cl_last)
        at = a_ref[0] * jnp.exp(cl - lw_all)
        rt = r_ref[0] * jnp.exp(cl)
        bt = b_ref[0] * e_inv
        kt = k_ref[0] * e_inv
        b_end = b_ref[0] * e_end
        k_end = k_ref[0] * e_end
        v_all = v_ref[0]
        P2 = 2 * N
        pairs = [(cc, p) for cc in range(nck) for p in range(RW_HEADS // 2)]
        pr = range(len(pairs))

        def pb(z, pi):
            cc, p = pairs[pi]
            return z[cc * L:(cc + 1) * L, p * P2:(p + 1) * P2]

        def stack2(z):
            lane = lax.broadcasted_iota(jnp.int32, (1, z.shape[1]), 1) & (P2 - 1)
            return jnp.concatenate([jnp.where(lane < N, z, 0.0), jnp.where(lane >= N, z, 0.0)], axis=0)

        tcol = lax.broadcasted_iota(jnp.int32, (L, P2), 1) & (N - 1)
        trow = lax.broadcasted_iota(jnp.int32, (L, P2), 0)
        strict = trow > tcol
        lower = trow >= tcol
        r2 = lax.broadcasted_iota(jnp.int32, (P2, P2), 0)
        c2 = lax.broadcasted_iota(jnp.int32, (P2, P2), 1)
        same_head = (r2 // N) == (c2 // N)
        eye2 = r2 == c2
        x2 = [jnp.concatenate([pb(at, pi), pb(rt, pi)], axis=0) for pi in pr]
        m_b = [_dot_nt(x2[pi], stack2(pb(bt, pi))) for pi in pr]
        m_k = [_dot_nt(x2[pi], stack2(pb(kt, pi))) for pi in pr]
        ap = [jnp.where(strict, m_b[pi][0:L], 0.0) for pi in pr]
        a_rb = [jnp.where(lower, m_b[pi][L:2 * L], 0.0) for pi in pr]
        a_ak = [jnp.where(strict, m_k[pi][0:L], 0.0) for pi in pr]
        a_rk = [jnp.where(lower, m_k[pi][L:2 * L], 0.0) for pi in pr]
        vp = [_dot(jnp.concatenate([a_ak[pi], a_rk[pi]], axis=0), stack2(pb(v_all, pi))) for pi in pr]
        kv = [_dot(pb(k_end, pi).T, pb(v_all, pi)) for pi in pr]
        x = [jnp.concatenate([pb(at, pi), vp[pi][0:L]], axis=1) for pi in pr]
        for it in range(6):
            x = [x[pi] + _dot(ap[pi], stack2(x[pi])) for pi in pr]
            if it < 5:
                ap = [_dot(ap[pi], stack2(ap[pi])) for pi in pr]
        post1 = [_dot(a_rb[pi], stack2(x[pi])) for pi in pr]
        post2 = [_dot(pb(b_end, pi).T, x[pi]) for pi in pr]
        lhs = []
        y0s = []
        h_adds = []
        for pi in pr:
            q_hat = pb(rt, pi) + post1[pi][:, 0:P2]
            gmat = (jnp.where(same_head, post2[pi][:, 0:P2], 0.0)
                    + jnp.where(eye2, pb(p_end, pi)[0:1, :], 0.0))
            lhs.append(jnp.concatenate([q_hat, gmat], axis=0))
            y0s.append(post1[pi][:, P2:2 * P2] + vp[pi][L:2 * L])
            h_adds.append(jnp.where(same_head, post2[pi][:, P2:2 * P2] + kv[pi], 0.0))
        npair = RW_HEADS // 2
        h_st = [h_ref[p] for p in range(npair)]
        ys = [None] * len(pairs)
        for cc in range(nck):
            res = [_dot(lhs[cc * npair + p], h_st[p]) for p in range(npair)]
            for p in range(npair):
                pi = cc * npair + p
                ys[pi] = res[p][0:L] + y0s[pi]
                h_st[p] = res[p][L:L + P2] + h_adds[pi]
        for p in range(npair):
            h_ref[p] = h_st[p]
        seg = same_head.astype(BF16)
        mean = [_seg_sum(ys[pi], seg) * (1.0 / N) for pi in pr]
        yc = [ys[pi] - mean[pi] for pi in pr]
        var = [_seg_sum(yc[pi] * yc[pi], seg) * (1.0 / N) for pi in pr]
        for pi, (cc, p) in enumerate(pairs):
            yn = yc[pi] * lax.rsqrt(var[pi] + RW_LN_EPS)
            rows = slice(cc * L, (cc + 1) * L)
            cols = slice(p * P2, (p + 1) * P2)
            out_ref[0, rows, cols] = ((yn * lng_ref[:, cols] + lnb_ref[:, cols] + bonus_ref[0, rows, cols])
                                      * g_ref[0, rows, cols]).astype(BF16)

    blk = pl.BlockSpec((1, tb, W), lambda b, c: (b, c, 0))
    vec = pl.BlockSpec((1, W), lambda b, c: (0, 0))
    return pl.pallas_call(
        body,
        grid=(bsz, t // tb),
        in_specs=[blk] * 8 + [vec, vec],
        out_specs=blk,
        out_shape=jax.ShapeDtypeStruct((bsz, t, W), BF16),
        scratch_shapes=[pltpu.VMEM((RW_HEADS // 2, 2 * N, 2 * N), F32)],
        compiler_params=_cp("parallel", "arbitrary"),
        name="rwkv_scan",
    )(r, lw, k, v, aa, bb, g_out, bonus, ln_g, ln_b)


def _mix_out_route(a, b, w_bf16, resid, g, router_split, tm=1024):
    n, wa = a.shape
    wb = b.shape[1]
    d = w_bf16.shape[1]
    e = router_split.shape[1] // 2

    def body(a_ref, b_ref, w_ref, r_ref, g_ref, rt_ref, h_ref, xn_ref, lg_ref):
        acc = jnp.dot(a_ref[...], w_ref[0:wa, :], preferred_element_type=F32)
        acc = acc + jnp.dot(b_ref[...], w_ref[wa:wa + wb, :], preferred_element_type=F32)
        h = r_ref[...] + acc
        h_ref[...] = h
        xn = _rms_rows(h, g_ref[...])
        x_hi = xn.astype(BF16)
        xn_ref[...] = x_hi
        x_lo = (xn - x_hi.astype(F32)).astype(BF16)
        hi_both = jnp.dot(x_hi, rt_ref[...], preferred_element_type=F32)
        lo_hi = jnp.dot(x_lo, rt_ref[:, 0:e], preferred_element_type=F32)
        lg_ref[...] = hi_both[:, 0:e] + (lo_hi + hi_both[:, e:2 * e])

    return pl.pallas_call(
        body,
        grid=(n // tm,),
        in_specs=[pl.BlockSpec((tm, wa), lambda i: (i, 0)),
                  pl.BlockSpec((tm, wb), lambda i: (i, 0)),
                  pl.BlockSpec((wa + wb, d), lambda i: (0, 0)),
                  pl.BlockSpec((tm, d), lambda i: (i, 0)),
                  pl.BlockSpec((1, d), lambda i: (0, 0)),
                  pl.BlockSpec((d, 2 * e), lambda i: (0, 0))],
        out_specs=[pl.BlockSpec((tm, d), lambda i: (i, 0)), pl.BlockSpec((tm, d), lambda i: (i, 0)),
                   pl.BlockSpec((tm, e), lambda i: (i, 0))],
        out_shape=[jax.ShapeDtypeStruct((n, d), F32), jax.ShapeDtypeStruct((n, d), BF16),
                   jax.ShapeDtypeStruct((n, e), F32)],
        compiler_params=_cp("parallel"),
        name="mix_out_route",
    )(a, b, w_bf16, resid, g.reshape(1, d), router_split)


MOE_TM = 512
MOE_TF = 1792


def _experts(xs, row_w, item_tile, item_exp, item_lo, item_hi, wg, wu, wd):
    nrows, d = xs.shape
    tm, tf = MOE_TM, MOE_TF
    nf = D_FF // tf
    n_items = item_tile.shape[0]

    def body(it_ref, ie_ref, lo_ref, hi_ref, x_ref, w_ref, wg_ref, wu_ref, wd_ref, o_ref, acc_ref):
        i = pl.program_id(0)
        j = pl.program_id(1)
        tile = it_ref[i]
        first = jnp.logical_or(i == 0, tile != it_ref[jnp.maximum(i - 1, 0)])
        last = jnp.logical_or(i == n_items - 1, tile != it_ref[jnp.minimum(i + 1, n_items - 1)])

        @pl.when(jnp.logical_and(first, j == 0))
        def _():
            acc_ref[...] = jnp.zeros_like(acc_ref)

        lo = lo_ref[i]
        hi = hi_ref[i]

        @pl.when(lo < hi)
        def _():
            x = x_ref[...]
            gg = jnp.dot(x, wg_ref[0], preferred_element_type=F32)
            uu = jnp.dot(x, wu_ref[0], preferred_element_type=F32)
            act = (_silu(gg) * uu).astype(BF16)
            part = jnp.dot(act, wd_ref[0], preferred_element_type=F32)
            rowi = lax.broadcasted_iota(jnp.int32, (tm, 1), 0)
            mine = jnp.logical_and(rowi >= lo, rowi < hi)
            acc_ref[...] += part * jnp.where(mine, w_ref[...], 0.0)

        @pl.when(jnp.logical_and(last, j == nf - 1))
        def _():
            o_ref[...] = acc_ref[...].astype(o_ref.dtype)

    grid_spec = pltpu.PrefetchScalarGridSpec(
        num_scalar_prefetch=4,
        grid=(n_items, nf),
        in_specs=[pl.BlockSpec((tm, d), lambda i, j, it, ie, lo, hi: (it[i], 0)),
                  pl.BlockSpec((tm, 1), lambda i, j, it, ie, lo, hi: (it[i], 0)),
                  pl.BlockSpec((1, d, tf), lambda i, j, it, ie, lo, hi: (ie[i], 0, j)),
                  pl.BlockSpec((1, d, tf), lambda i, j, it, ie, lo, hi: (ie[i], 0, j)),
                  pl.BlockSpec((1, tf, d), lambda i, j, it, ie, lo, hi: (ie[i], j, 0))],
        out_specs=pl.BlockSpec((tm, d), lambda i, j, it, ie, lo, hi: (it[i], 0)),
        scratch_shapes=[pltpu.VMEM((tm, d), F32)],
    )
    return pl.pallas_call(
        body,
        grid_spec=grid_spec,
        out_shape=jax.ShapeDtypeStruct((nrows, d), BF16),
        compiler_params=_cp("arbitrary", "arbitrary"),
        name="moe_experts",
    )(item_tile, item_exp, item_lo, item_hi, xs, row_w, wg, wu, wd)


def _combine_norm(h, y0, y1, g, tm=512):
    n, d = h.shape

    def body(h_ref, a_ref, b_ref, g_ref, o_ref):
        o_ref[...] = _rms_rows(h_ref[...] + (a_ref[...].astype(F32) + b_ref[...].astype(F32)), g_ref[...])

    blk = pl.BlockSpec((tm, d), lambda i: (i, 0))
    return pl.pallas_call(
        body,
        grid=(n // tm,),
        in_specs=[blk, blk, blk, pl.BlockSpec((1, d), lambda i: (0, 0))],
        out_specs=blk,
        out_shape=jax.ShapeDtypeStruct((n, d), F32),
        compiler_params=_cp("parallel"),
        name="combine_norm",
    )(h, y0, y1, g.reshape(1, d))


def _route(logits, n):
    tm = MOE_TM
    na = n * TOP_K
    n_tiles = na // tm
    top_val, top_idx = lax.top_k(logits, TOP_K)
    top_w = jax.nn.softmax(top_val, axis=-1)
    e_flat = top_idx.reshape(-1).astype(jnp.int32)
    w_flat = top_w.reshape(-1)
    ids = jnp.arange(na, dtype=jnp.int32)
    e_sorted, sorted_a, sorted_w = lax.sort((e_flat, ids, w_flat), num_keys=1, is_stable=True)
    sorted_tok = sorted_a // TOP_K
    _, pos = lax.sort((sorted_a, ids), num_keys=1)
    ends = jnp.searchsorted(e_sorted, jnp.arange(N_EXPERTS, dtype=jnp.int32), side="right").astype(jnp.int32)
    cuts = jnp.sort(jnp.concatenate([jnp.arange(n_tiles, dtype=jnp.int32) * tm, ends[:-1].astype(jnp.int32)]))
    nxt = jnp.concatenate([cuts[1:], jnp.full((1,), na, jnp.int32)])
    item_tile = jnp.minimum(cuts // tm, n_tiles - 1)
    item_exp = jnp.minimum(jnp.searchsorted(ends, cuts, side="right"), N_EXPERTS - 1).astype(jnp.int32)
    item_lo = cuts - item_tile * tm
    item_hi = nxt - item_tile * tm
    return sorted_tok, sorted_w, (item_tile, item_exp, item_lo, item_hi), pos.reshape(n, TOP_K)


def kernel(x, e_norm1_g, e_w_in, e_ml_conv_w, e_ml_conv_b, e_ml_gate_b, e_ml_norm_g, e_gla_gate_up, e_gla_gate_b,
           e_gla_norm_g, e_w_out, e_norm2_g, e_ffn_w_gate, e_ffn_w_up, e_ffn_w_down, o_norm1_g, o_w_in,
           o_ret_norm_g, o_rw_mu, o_rw_w_up, o_rw_w0, o_rw_a_up, o_rw_a0, o_rw_g_up, o_rw_k_k, o_rw_k_a, o_rw_r_k,
           o_rw_ln_g, o_rw_ln_b, o_w_out, o_norm2_g, o_moe_router, o_moe_w_gate, o_moe_w_up, o_moe_w_down,
           final_norm_g):
    bsz, t, d = x.shape
    n = bsz * t
    h0 = x.reshape(n, d)

    w = e_w_in[0]
    row = lambda a: a.reshape(1, -1)
    x3 = x
    ng1 = row(e_norm1_g[0])
    w_if = w[:, 2048:2056]
    w_gate = jnp.zeros((d, 128), F32).at[:, :2 * ML_HEADS].set(w_if).astype(BF16)
    w_gate_t = jnp.zeros((16, d), F32).at[:2 * ML_HEADS, :].set(w_if.T).astype(BF16)
    h_ml = _mlstm(x3, ng1, w[:, :4 * ML_W].astype(BF16), w_gate, w_gate_t, row(e_ml_gate_b[0]),
                  e_ml_gate_b[0].reshape(-1, 1), e_ml_conv_w[0], row(e_ml_conv_b[0]), row(e_ml_norm_g[0]))
    gq, gk, gv, gr = 2056, 2312, 2568, 3080
    w_pairs = jnp.stack([jnp.concatenate([w[:, gq + 128 * hp:gq + 128 * (hp + 1)],
                                          w[:, gk + 128 * hp:gk + 128 * (hp + 1)],
                                          w[:, gv + 256 * hp:gv + 256 * (hp + 1)],
                                          w[:, gr + 256 * hp:gr + 256 * (hp + 1)]], axis=1)
                         for hp in range(GLA_HEADS // 2)]).astype(BF16)
    w_low = jnp.zeros((d, 128), F32).at[:, :GLA_RANK].set(w[:, 3592:3608]).astype(BF16)
    o_gla = _gla(x3, ng1, w_pairs, w_low, e_gla_gate_up[0].astype(BF16), row(e_gla_gate_b[0]),
                 row(e_gla_norm_g[0]))
    h2 = _mix_out_ffn(h_ml.reshape(n, -1), o_gla.reshape(n, -1), e_w_out[0].astype(BF16), h0, e_norm2_g[0],
                      e_ffn_w_gate[0].astype(BF16), e_ffn_w_up[0].astype(BF16), e_ffn_w_down[0].astype(BF16))

    w = o_w_in[0]
    h2_3 = h2.reshape(bsz, t, d)
    ng2 = row(o_norm1_g[0])
    y_ret = _retention(h2_3, ng2, w[:, :4 * RET_W].astype(BF16), *_retention_tables(t), row(o_ret_norm_g[0]))
    head_of = jnp.arange(RW_W) // RW_DIM
    bd = (head_of[:, None] == head_of[None, :]).astype(BF16)
    r, lw, k, v, aa, bb, g_out, bonus = _rwkv_prep(
        h2_3, ng2, w[:, 4 * RET_W:].astype(BF16), row(o_rw_mu[0]), o_rw_w_up[0].astype(BF16), row(o_rw_w0[0]),
        o_rw_a_up[0].astype(BF16), row(o_rw_a0[0]), o_rw_g_up[0].astype(BF16), row(o_rw_k_k[0]),
        row(o_rw_k_a[0]), row(o_rw_r_k[0]), bd)
    y_rw = _rwkv_scan(r, lw, k, v, aa, bb, g_out, bonus, row(o_rw_ln_g[0]), row(o_rw_ln_b[0]))
    router_pad = jnp.zeros((d, 128), F32).at[:, :N_EXPERTS].set(o_moe_router[0])
    router_hi = router_pad.astype(BF16)
    router_split = jnp.concatenate([router_hi, (router_pad - router_hi.astype(F32)).astype(BF16)], axis=1)
    h3, xn, logits = _mix_out_route(y_ret.reshape(n, -1), y_rw.reshape(n, -1), o_w_out[0].astype(BF16), h2,
                                    o_norm2_g[0], router_split)
    sorted_tok, sorted_w, items, pos = _route(logits[:, :N_EXPERTS], n)
    xs = xn.at[sorted_tok].get(mode="promise_in_bounds")
    ys = _experts(xs, sorted_w.reshape(-1, 1), *items, o_moe_w_gate[0].astype(BF16),
                  o_moe_w_up[0].astype(BF16), o_moe_w_down[0].astype(BF16))
    y0 = ys.at[pos[:, 0]].get(mode="promise_in_bounds")
    y1 = ys.at[pos[:, 1]].get(mode="promise_in_bounds")
    out = _combine_norm(h3, y0, y1, final_norm_g)
    return out.reshape(bsz, t, d)
```

```python
import functools

import numpy as np
import jax
import jax.numpy as jnp
from jax import lax
from jax.experimental import pallas as pl
from jax.experimental.pallas import tpu as pltpu

F32 = jnp.float32
BF16 = jnp.bfloat16

D_MODEL = 1024
EPS = 1e-6
ML_HEADS, ML_DIM, ML_W, ML_CONV = 4, 128, 512, 4
GLA_HEADS, GLA_DK, GLA_DV, GLA_RANK, GLA_TAU = 4, 64, 128, 16, 16.0
RET_HEADS, RET_DIM, RET_W = 4, 128, 512
ROPE_BASE = 10000.0
RW_HEADS, RW_DIM, RW_W = 8, 64, 512
RW_COLS = 1792
RW_LN_EPS = 64e-5
D_FF = 3584
N_EXPERTS = 8
TOP_K = 2

VMEM_LIMIT = 48 * 1024 * 1024
NEG = -1e30


def _cp(*sem):
    return pltpu.CompilerParams(dimension_semantics=sem, vmem_limit_bytes=VMEM_LIMIT)


def _sigmoid(x):
    return 1.0 / (1.0 + jnp.exp(-x))


def _silu(x):
    return x * _sigmoid(x)


def _log_sigmoid(x):
    return jnp.minimum(x, 0.0) - jnp.log(1.0 + jnp.exp(-jnp.abs(x)))


def _dot(a, b):
    return jnp.dot(a.astype(BF16), b.astype(BF16), preferred_element_type=F32)


def _dot_nt(a, b):
    return lax.dot_general(a.astype(BF16), b.astype(BF16), (((1,), (1,)), ((), ())), preferred_element_type=F32)


def _dot_tn(a, b):
    return jnp.dot(a.T.astype(BF16), b.astype(BF16), preferred_element_type=F32)


def _rms_rows(x, g):
    ms = jnp.mean(x * x, axis=-1, keepdims=True)
    return x * lax.rsqrt(ms + EPS) * g


def _mix_out_ffn(a, b, w_out, resid, g, wg, wu, wd, tm=512, tf=1792):
    n, d = resid.shape
    wa = a.shape[1]
    wb = b.shape[1]
    f = wg.shape[1]
    nf = f // tf

    def body(a_ref, b_ref, wo_ref, r_ref, g_ref, wg_ref, wu_ref, wd_ref, o_ref, xn_ref, acc_ref):
        j = pl.program_id(1)

        @pl.when(j == 0)
        def _():
            h = r_ref[...] + jnp.dot(a_ref[...], wo_ref[0:wa, :], preferred_element_type=F32)
            h = h + jnp.dot(b_ref[...], wo_ref[wa:wa + wb, :], preferred_element_type=F32)
            xn_ref[...] = _rms_rows(h, g_ref[...]).astype(BF16)
            acc_ref[...] = h

        xn = xn_ref[...]
        gg = jnp.dot(xn, wg_ref[...], preferred_element_type=F32)
        uu = jnp.dot(xn, wu_ref[...], preferred_element_type=F32)
        act = (_silu(gg) * uu).astype(BF16)
        acc_ref[...] += jnp.dot(act, wd_ref[...], preferred_element_type=F32)

        @pl.when(j == nf - 1)
        def _():
            o_ref[...] = acc_ref[...]

    return pl.pallas_call(
        body,
        grid=(n // tm, nf),
        in_specs=[pl.BlockSpec((tm, wa), lambda i, j: (i, 0)),
                  pl.BlockSpec((tm, wb), lambda i, j: (i, 0)),
                  pl.BlockSpec((wa + wb, d), lambda i, j: (0, 0)),
                  pl.BlockSpec((tm, d), lambda i, j: (i, 0)),
                  pl.BlockSpec((1, d), lambda i, j: (0, 0)),
                  pl.BlockSpec((d, tf), lambda i, j: (0, j)),
                  pl.BlockSpec((d, tf), lambda i, j: (0, j)),
                  pl.BlockSpec((tf, d), lambda i, j: (j, 0))],
        out_specs=pl.BlockSpec((tm, d), lambda i, j: (i, 0)),
        out_shape=jax.ShapeDtypeStruct((n, d), F32),
        scratch_shapes=[pltpu.VMEM((tm, d), BF16), pltpu.VMEM((tm, d), F32)],
        compiler_params=_cp("parallel", "arbitrary"),
        name="mix_out_ffn",
    )(a, b, w_out, resid, g.reshape(1, d), wg, wu, wd)


ML_TB = 1024
ML_L = 256


def _mlstm(x, in_g, w_in, w_gate, w_gate_t, gate_b_row, gate_b_col, conv_w, conv_b, norm_g):
    bsz, t, d = x.shape
    TB, L, D, H, W = ML_TB, ML_L, ML_DIM, ML_HEADS, ML_W

    def body(x_ref, ing_ref, w_ref, wg_ref, wgt_ref, gbr_ref, gbc_ref, cwq_ref, cwk_ref, cbq_ref, cbk_ref,
             ng_ref, out_ref, qext, kext, c_ref, n_ref, m_ref):
        c = pl.program_id(1)

        @pl.when(c == 0)
        def _():
            qext[0:8, :] = jnp.zeros((8, W), F32)
            kext[0:8, :] = jnp.zeros((8, W), F32)
            c_ref[...] = jnp.zeros_like(c_ref)
            n_ref[...] = jnp.zeros_like(n_ref)
            m_ref[...] = jnp.zeros_like(m_ref)

        def project(cc):
            xn = _rms_rows(x_ref[0, cc * L:(cc + 1) * L, :], ing_ref[...]).astype(BF16)
            pc = jnp.dot(xn, w_ref[...], preferred_element_type=F32)
            gc = jnp.dot(xn, wg_ref[...], preferred_element_type=F32)[:, 0:2 * H]
            gr = lax.dot_general(wgt_ref[...], xn, (((1,), (1,)), ((), ())),
                                 preferred_element_type=F32)[0:2 * H, :]
            return pc, gc, gr

        def conv(ext, base, cw_ref, cb_ref):
            acc = cb_ref[...] + cw_ref[0:1, :] * ext[pl.ds(base + 8 - ML_CONV + 1, L), :]
            for kk in range(1, ML_CONV):
                acc = acc + cw_ref[kk:kk + 1, :] * ext[pl.ds(base + 8 - ML_CONV + 1 + kk, L), :]
            return _silu(acc)

        ri = lax.broadcasted_iota(jnp.int32, (L, L), 0)
        ci = lax.broadcasted_iota(jnp.int32, (L, L), 1)
        causal = ri >= ci
        heads = range(H)
        c_st = [c_ref[h] for h in heads]
        n_st = [n_ref[h] for h in heads]
        m_st = [m_ref[h] for h in heads]
        nxt = project(0)
        for cc in range(TB // L):
            rows = slice(cc * L, (cc + 1) * L)
            p, g_cols, g_rows = nxt
            if cc + 1 < TB // L:
                nxt = project(cc + 1)
            qext[8 + cc * L:8 + (cc + 1) * L, :] = p[:, 0:W]
            kext[8 + cc * L:8 + (cc + 1) * L, :] = p[:, W:2 * W]
            q_all = conv(qext, cc * L, cwq_ref, cbq_ref) * (D ** -0.5)
            k_all = conv(kext, cc * L, cwk_ref, cbk_ref)
            gcol = g_cols + gbr_ref[...]
            grow = g_rows + gbc_ref[...]
            fcol = _log_sigmoid(gcol[:, H:2 * H])
            frow = _log_sigmoid(grow[H:2 * H, :])
            hs = lambda z, h: z[:, h * D:(h + 1) * D]
            w_intra, w_inter, w_state, carry, m_row = [], [], [], [], []
            for h in heads:
                f_row = frow[h:h + 1, :]
                i_row = grow[h:h + 1, :]
                f_col = fcol[:, h:h + 1]
                i_col = gcol[:, h:h + 1]
                b_col = jnp.sum(jnp.where(causal, f_row, 0.0), axis=1, keepdims=True)
                b_row = jnp.sum(jnp.where(ri <= ci, f_col, 0.0), axis=0, keepdims=True)
                g_tot = jnp.sum(f_row, axis=1, keepdims=True)
                d_intra = jnp.where(causal, b_col - b_row + i_row, NEG)
                d_inter = b_col + m_st[h]
                mr = jnp.maximum(d_inter, jnp.max(d_intra, axis=1, keepdims=True))
                w_intra.append(jnp.exp(d_intra - mr))
                w_inter.append(jnp.exp(d_inter - mr))
                m_row.append(mr)
                d_state = g_tot - b_col + i_col
                mn = jnp.maximum(g_tot + m_st[h], jnp.max(d_state, axis=0, keepdims=True))
                w_state.append(jnp.exp(d_state - mn))
                carry.append(jnp.exp(g_tot + m_st[h] - mn))
                m_st[h] = mn
            qh = [hs(q_all, h) for h in heads]
            kh = [hs(k_all, h) for h in heads]
            vh = [p[:, 2 * W + h * D:2 * W + (h + 1) * D] for h in heads]
            s = [_dot_nt(qh[h], kh[h]) * w_intra[h] for h in heads]
            qc = [_dot(qh[h], c_st[h]) for h in heads]
            kw = [kh[h] * w_state[h] for h in heads]
            kv = [_dot_tn(kw[h], vh[h]) for h in heads]
            sv = [_dot(s[h], vh[h]) for h in heads]
            for h in heads:
                num = sv[h] + w_inter[h] * qc[h]
                den = (jnp.sum(s[h], axis=1, keepdims=True)
                       + w_inter[h] * jnp.sum(qh[h] * n_st[h], axis=1, keepdims=True))
                hval = num / jnp.maximum(jnp.abs(den), jnp.exp(-m_row[h]))
                c_st[h] = carry[h] * c_st[h] + kv[h]
                n_st[h] = carry[h] * n_st[h] + jnp.sum(kw[h], axis=0, keepdims=True)
                hg = _sigmoid(p[:, 3 * W + h * D:3 * W + (h + 1) * D]) * hval
                hc = hg - jnp.mean(hg, axis=1, keepdims=True)
                hn = hc * lax.rsqrt(jnp.mean(hc * hc, axis=1, keepdims=True) + EPS)
                out_ref[0, rows, h * D:(h + 1) * D] = (hn * ng_ref[:, h * D:(h + 1) * D]).astype(BF16)
        for h in heads:
            c_ref[h] = c_st[h]
            n_ref[h] = n_st[h]
            m_ref[h] = m_st[h]
        qext[0:8, :] = qext[TB:TB + 8, :]
        kext[0:8, :] = kext[TB:TB + 8, :]

    fix = lambda j: (lambda b, c: (0, j))
    return pl.pallas_call(
        body,
        grid=(bsz, t // TB),
        in_specs=[pl.BlockSpec((1, TB, d), lambda b, c: (b, c, 0)),
                  pl.BlockSpec((1, d), fix(0)),
                  pl.BlockSpec((d, 4 * W), fix(0)),
                  pl.BlockSpec((d, 128), fix(0)),
                  pl.BlockSpec((16, d), fix(0)),
                  pl.BlockSpec((1, 2 * H), fix(0)),
                  pl.BlockSpec((2 * H, 1), fix(0)),
                  pl.BlockSpec((ML_CONV, W), fix(0)),
                  pl.BlockSpec((ML_CONV, W), fix(1)),
                  pl.BlockSpec((1, W), fix(0)),
                  pl.BlockSpec((1, W), fix(1)),
                  pl.BlockSpec((1, W), fix(0))],
        out_specs=pl.BlockSpec((1, TB, W), lambda b, c: (b, c, 0)),
        out_shape=jax.ShapeDtypeStruct((bsz, t, W), BF16),
        scratch_shapes=[pltpu.VMEM((TB + 8, W), F32), pltpu.VMEM((TB + 8, W), F32),
                        pltpu.VMEM((H, D, D), F32), pltpu.VMEM((H, 1, D), F32), pltpu.VMEM((H, 1, 1), F32)],
        compiler_params=_cp("parallel", "arbitrary"),
        name="mlstm",
    )(x, in_g, w_in, w_gate, w_gate_t, gate_b_row, gate_b_col, conv_w, conv_w, conv_b, conv_b, norm_g)


GLA_TC = 1024
GLA_SUB = 16
GLA_GROUP = 128


def _gla(x, in_g, w_pairs, w_low, gate_up, gate_b, norm_g):
    bsz, t, d = x.shape
    tc, S, GB = GLA_TC, GLA_SUB, GLA_GROUP
    head_ones = (jnp.arange(2 * GLA_DK)[:, None] // GLA_DK == jnp.arange(2 * GB)[None, :] // GB).astype(BF16)
    nsub = tc // S
    dk, dv = GLA_DK, GLA_DV

    def body(x_ref, ing_ref, w_ref, wl_ref, gu_ref, gbias_ref, ng_ref, ones_ref, out_ref, st_ref, ksh, bsh):
        c = pl.program_id(2)

        @pl.when(c == 0)
        def _():
            st_ref[...] = jnp.zeros_like(st_ref)
            ksh[0:S, :] = jnp.zeros((S, 2 * dk), F32)
            bsh[0:S, :] = jnp.zeros((S, 2 * dk), F32)

        xn = _rms_rows(x_ref[0], ing_ref[...]).astype(BF16)
        p = jnp.dot(xn, w_ref[0], preferred_element_type=F32)
        g_low = jnp.dot(xn, wl_ref[...], preferred_element_type=F32)[:, 0:GLA_RANK]
        z = _dot(g_low, gu_ref[...]) + gbias_ref[...]
        la = _log_sigmoid(z) / GLA_TAU
        rowi = lax.broadcasted_iota(jnp.int32, (tc, 1), 0)
        rmod = rowi & (S - 1)
        bcum = la
        rsum = la
        for s in (1, 2, 4, 8):
            bcum = bcum + jnp.where(rmod >= s, pltpu.roll(bcum, s, 0), 0.0)
            rsum = rsum + jnp.where(rmod < S - s, pltpu.roll(rsum, tc - s, 0), 0.0)
        q = p[:, 0:2 * dk] * (dk ** -0.5)
        k = p[:, 2 * dk:4 * dk]
        v = p[:, 4 * dk:4 * dk + 2 * dv]
        gate = p[:, 4 * dk + 2 * dv:4 * dk + 4 * dv]
        qt = q * jnp.exp(bcum)
        kt = k * jnp.exp(rsum - la)
        eg = jnp.exp(bcum + rsum - la)

        ksh[S:, :] = k
        bsh[S:, :] = bcum
        prods = []
        for d in range(S):
            kd = k if d == 0 else ksh[pl.ds(S - d, tc), :]
            bd = bcum if d == 0 else bsh[pl.ds(S - d, tc), :]
            e = jnp.exp(jnp.where(rmod >= d, bcum - bd, 0.0))
            prods.append((q * kd * e).astype(BF16))
        ws = [jnp.dot(p, ones_ref[...], preferred_element_type=F32) for p in prods]
        coli = lax.broadcasted_iota(jnp.int32, (tc, GB), 1)
        rgrp = rowi & (GB - 1)
        att0 = jnp.zeros((tc, GB), F32)
        att1 = jnp.zeros((tc, GB), F32)
        offs = jnp.where((coli // S) == (rgrp // S), rgrp - coli, -1)
        for d in range(S):
            here = offs == d
            att0 = jnp.where(here, ws[d][:, 0:GB], att0)
            att1 = jnp.where(here, ws[d][:, GB:2 * GB], att1)

        heads = range(2)
        lk = [slice(hh * dk, (hh + 1) * dk) for hh in heads]
        lv = [slice(hh * dv, (hh + 1) * dv) for hh in heads]
        kv = [[_dot_tn(v[si * S:(si + 1) * S, lv[hh]], kt[si * S:(si + 1) * S, lk[hh]]) for hh in heads]
              for si in range(nsub)]
        st = [st_ref[hh] for hh in heads]
        inter = [[], []]
        for si in range(nsub):
            rows = slice(si * S, (si + 1) * S)
            for hh in heads:
                inter[hh].append(_dot_nt(qt[rows, lk[hh]], st[hh]))
                st[hh] = st[hh] * eg[si * S:si * S + 1, lk[hh]] + kv[si][hh]
        for hh in heads:
            st_ref[hh] = st[hh]

        for hh, att in ((0, att0), (1, att1)):
            diag = jnp.concatenate([_dot(att[g * GB:(g + 1) * GB], v[g * GB:(g + 1) * GB, lv[hh]])
                                    for g in range(tc // GB)], axis=0)
            o = diag + jnp.concatenate(inter[hh], axis=0)
            on = o * lax.rsqrt(jnp.mean(o * o, axis=1, keepdims=True) + EPS)
            out_ref[0, :, lv[hh]] = (on * ng_ref[:, lv[hh]] * _silu(gate[:, lv[hh]])).astype(BF16)

    pw = 4 * dk + 4 * dv
    return pl.pallas_call(
        body,
        grid=(bsz, GLA_HEADS // 2, t // tc),
        in_specs=[pl.BlockSpec((1, tc, d), lambda b, h, c: (b, c, 0)),
                  pl.BlockSpec((1, d), lambda b, h, c: (0, 0)),
                  pl.BlockSpec((1, d, pw), lambda b, h, c: (h, 0, 0)),
                  pl.BlockSpec((d, 128), lambda b, h, c: (0, 0)),
                  pl.BlockSpec((GLA_RANK, 2 * dk), lambda b, h, c: (0, h)),
                  pl.BlockSpec((1, 2 * dk), lambda b, h, c: (0, h)),
                  pl.BlockSpec((1, 2 * dv), lambda b, h, c: (0, h)),
                  pl.BlockSpec((2 * dk, 2 * GB), lambda b, h, c: (0, 0))],
        out_specs=pl.BlockSpec((1, tc, 2 * dv), lambda b, h, c: (b, c, h)),
        out_shape=jax.ShapeDtypeStruct((bsz, t, GLA_HEADS * dv), BF16),
        scratch_shapes=[pltpu.VMEM((2, dv, dk), F32), pltpu.VMEM((tc + S, 2 * dk), F32),
                        pltpu.VMEM((tc + S, 2 * dk), F32)],
        compiler_params=_cp("parallel", "parallel", "arbitrary"),
        name="gla",
    )(x, in_g, w_pairs, w_low, gate_up, gate_b, norm_g, head_ones)


RET_TB = 1024
RET_L = 256


def _retention(x, in_g, w_in, cos_t, sin_t, intra, inter, sdec, cdec, norm_g):
    bsz, t, d = x.shape
    TB, L, D, H, W = RET_TB, RET_L, RET_DIM, RET_HEADS, RET_W

    def body(x_ref, ing_ref, w_ref, cos_ref, sin_ref, intra_ref, inter_ref, sdec_ref, cdec_ref, ng_ref,
             out_ref, s_ref):
        c = pl.program_id(1)

        @pl.when(c == 0)
        def _():
            s_ref[...] = jnp.zeros_like(s_ref)

        p = jnp.dot(_rms_rows(x_ref[0], ing_ref[...]).astype(BF16), w_ref[...], preferred_element_type=F32)
        heads = range(H)
        s_st = [s_ref[h] for h in heads]
        for cc in range(TB // L):
            rows = slice(cc * L, (cc + 1) * L)
            cs = cos_ref[rows, :]
            sn = sin_ref[rows, :]

            def rot(z):
                return z * cs + pltpu.roll(z, D // 2, 1) * sn

            q = [rot(p[rows, h * D:(h + 1) * D]) * (D ** -0.5) for h in heads]
            k = [rot(p[rows, W + h * D:W + (h + 1) * D]) for h in heads]
            v = [p[rows, 2 * W + h * D:2 * W + (h + 1) * D] for h in heads]
            s = [_dot_nt(q[h], k[h]) * intra_ref[h] for h in heads]
            qs = [_dot(q[h], s_st[h]) for h in heads]
            kv = [_dot_tn(k[h] * sdec_ref[h], v[h]) for h in heads]
            sv = [_dot(s[h], v[h]) for h in heads]
            for h in heads:
                o = sv[h] + inter_ref[h] * qs[h]
                s_st[h] = cdec_ref[h] * s_st[h] + kv[h]
                oc = o - jnp.mean(o, axis=1, keepdims=True)
                on = oc * lax.rsqrt(jnp.mean(oc * oc, axis=1, keepdims=True) + EPS)
                hl = slice(h * D, (h + 1) * D)
                gate = p[rows, 3 * W + h * D:3 * W + (h + 1) * D]
                out_ref[0, rows, hl] = (on * ng_ref[:, hl] * _silu(gate)).astype(BF16)
        for h in heads:
            s_ref[h] = s_st[h]

    fix2 = lambda b, c: (0, 0)
    fix3 = lambda b, c: (0, 0, 0)
    return pl.pallas_call(
        body,
        grid=(bsz, t // TB),
        in_specs=[pl.BlockSpec((1, TB, d), lambda b, c: (b, c, 0)),
                  pl.BlockSpec((1, d), fix2),
                  pl.BlockSpec((d, 4 * W), fix2),
                  pl.BlockSpec((TB, D), lambda b, c: (c, 0)),
                  pl.BlockSpec((TB, D), lambda b, c: (c, 0)),
                  pl.BlockSpec((H, L, L), fix3),
                  pl.BlockSpec((H, L, 1), fix3),
                  pl.BlockSpec((H, L, 1), fix3),
                  pl.BlockSpec((H, 1, 1), fix3),
                  pl.BlockSpec((1, W), lambda b, c: (0, 0))],
        out_specs=pl.BlockSpec((1, TB, W), lambda b, c: (b, c, 0)),
        out_shape=jax.ShapeDtypeStruct((bsz, t, W), BF16),
        scratch_shapes=[pltpu.VMEM((H, D, D), F32)],
        compiler_params=_cp("parallel", "arbitrary"),
        name="retention",
    )(x, in_g, w_in, cos_t, sin_t, intra, inter, sdec, cdec, norm_g)


def _retention_tables(t):
    L, D = RET_L, RET_DIM
    inv = ROPE_BASE ** (-jnp.arange(0, D, 2, dtype=F32) / D)
    ang = jnp.arange(t).astype(F32)[:, None] * inv[None, :]
    cos = jnp.cos(ang)
    sin = jnp.sin(ang)
    cos_t = jnp.concatenate([cos, cos], axis=-1)
    sin_t = jnp.concatenate([-sin, sin], axis=-1)
    log_gamma = jnp.log1p(-jnp.exp2(-5.0 - jnp.arange(RET_HEADS, dtype=F32)))
    idx = jnp.arange(L, dtype=F32)
    causal = idx[:, None] >= idx[None, :]
    rel = jnp.where(causal, idx[:, None] - idx[None, :], 0.0)
    intra = jnp.where(causal, jnp.exp(log_gamma[:, None, None] * rel), 0.0)
    inter = jnp.exp(log_gamma[:, None] * (idx + 1.0))[:, :, None]
    sdec = jnp.exp(log_gamma[:, None] * (L - 1.0 - idx))[:, :, None]
    cdec = jnp.exp(log_gamma * L)[:, None, None]
    return cos_t, sin_t, intra, inter, sdec, cdec


RWP_TM = 1024
RWP_SUB = 256


def _seg_sum(x, bd):
    hi = x.astype(BF16)
    lo = (x - hi.astype(F32)).astype(BF16)
    return jnp.dot(hi, bd, preferred_element_type=F32) + jnp.dot(lo, bd, preferred_element_type=F32)


def _rwkv_prep(x, in_g, w_in, mu, w_up, w0, a_up, a0, g_up, k_k, k_a, r_k, bd):
    bsz, t, d = x.shape
    tm, hm, W = RWP_TM, RWP_SUB, RW_W

    def body(x_ref, ing_ref, win_ref, mu_ref, wup_ref, w0_ref, aup_ref, a0_ref, gup_ref, kk_ref, ka_ref, rk_ref,
             bd_ref, r_out, lw_out, k_out, v_out, a_out, b_out, g_out, bonus_out, last_ref):
        c = pl.program_id(1)

        @pl.when(c == 0)
        def _():
            last_ref[...] = jnp.zeros_like(last_ref)

        def project(i):
            xs = x_ref[0, i * hm:(i + 1) * hm, :]
            return jnp.dot(_rms_rows(xs, ing_ref[...]).astype(BF16), win_ref[...], preferred_element_type=F32)

        rowi = lax.broadcasted_iota(jnp.int32, (hm, 1), 0)
        prev = last_ref[...]
        nxt = project(0)
        for i in range(tm // hm):
            rows = slice(i * hm, (i + 1) * hm)
            cur = nxt
            if i + 1 < tm // hm:
                nxt = project(i + 1)
            sh = jnp.where(rowi == 0, prev, pltpu.roll(cur, 1, 0))
            prev = cur[hm - 1:hm, :]
            xm = cur + (sh - cur) * mu_ref[...]
            x_r = xm[:, 0:W]
            x_k = xm[:, W:2 * W]
            x_v = xm[:, 2 * W:3 * W]
            x_dl = xm[:, 3 * W:3 * W + 64]
            x_al = xm[:, 3 * W + 64:3 * W + 128]
            x_gl = xm[:, 3 * W + 128:3 * W + 256]
            wl = w0_ref[...] + _dot(jnp.tanh(x_dl), wup_ref[...])
            sp = jnp.maximum(-wl, 0.0) + jnp.log(1.0 + jnp.exp(-jnp.abs(wl)))
            lw_out[0, rows, :] = -jnp.exp(-sp - 0.5)
            a = _sigmoid(a0_ref[...] + _dot(x_al, aup_ref[...]))
            g_out[0, rows, :] = _dot(_sigmoid(x_gl), gup_ref[...]).astype(BF16)
            kk0 = x_k * kk_ref[...]
            nrm = jnp.sqrt(_seg_sum(kk0 * kk0, bd_ref[...]))
            kk = kk0 / jnp.maximum(nrm, 1e-12)
            k_h = x_k * (1.0 + (a - 1.0) * ka_ref[...])
            r_out[0, rows, :] = x_r.astype(BF16)
            k_out[0, rows, :] = k_h.astype(BF16)
            v_out[0, rows, :] = x_v.astype(BF16)
            a_out[0, rows, :] = (-kk).astype(BF16)
            b_out[0, rows, :] = (kk * a).astype(BF16)
            bonus_out[0, rows, :] = (_dot(x_r * k_h * rk_ref[...], bd_ref[...]) * x_v).astype(BF16)
        last_ref[...] = prev

    row = lambda b, c: (0, 0)
    blk = pl.BlockSpec((1, tm, W), lambda b, c: (b, c, 0))
    shp = [jax.ShapeDtypeStruct((bsz, t, W), F32 if i == 1 else BF16) for i in range(8)]
    return pl.pallas_call(
        body,
        grid=(bsz, t // tm),
        in_specs=[pl.BlockSpec((1, tm, d), lambda b, c: (b, c, 0)),
                  pl.BlockSpec((1, d), row),
                  pl.BlockSpec((d, RW_COLS), row),
                  pl.BlockSpec((1, RW_COLS), row),
                  pl.BlockSpec((64, W), row), pl.BlockSpec((1, W), row),
                  pl.BlockSpec((64, W), row), pl.BlockSpec((1, W), row),
                  pl.BlockSpec((128, W), row),
                  pl.BlockSpec((1, W), row), pl.BlockSpec((1, W), row), pl.BlockSpec((1, W), row),
                  pl.BlockSpec((W, W), row)],
        out_specs=[blk] * 8,
        out_shape=shp,
        scratch_shapes=[pltpu.VMEM((1, RW_COLS), F32)],
        compiler_params=_cp("parallel", "arbitrary"),
        name="rwkv_prep",
    )(x, in_g, w_in, mu, w_up, w0, a_up, a0, g_up, k_k, k_a, r_k, bd)


RW_L = 64
RW_TB = 256


def _rwkv_scan(r, lw, k, v, aa, bb, g_out, bonus, ln_g, ln_b):
    bsz, t, W = r.shape
    L, N, tb = RW_L, RW_DIM, RW_TB
    nck = tb // L

    def body(r_ref, lw_ref, k_ref, v_ref, a_ref, b_ref, g_ref, bonus_ref, lng_ref, lnb_ref, out_ref, h_ref):
        c = pl.program_id(1)

        @pl.when(c == 0)
        def _():
            h_ref[...] = jnp.zeros_like(h_ref)

        lw_all = lw_ref[0]
        rowi = lax.broadcasted_iota(jnp.int32, (tb, 1), 0) & (L - 1)
        cl = lw_all
        for s in (1, 2, 4, 8, 16, 32):
            cl = cl + jnp.where(rowi >= s, pltpu.roll(cl, s, 0), 0.0)
        cl_last = jnp.concatenate([jnp.broadcast_to(cl[(cc + 1) * L - 1:(cc + 1) * L, :], (L, W))
                                   for cc in range(nck)], axis=0)
        e_inv = jnp.exp(-cl)
        e_end = jnp.exp(cl_last - cl)
        p_end = jnp.exp(cl_last)
        at = a_ref[0] * jnp.exp(cl - lw_all)
        rt = r_ref[0] * jnp.exp(cl)
        bt = b_ref[0] * e_inv
        kt = k_ref[0] * e_inv
        b_end = b_ref[0] * e_end
        k_end = k_ref[0] * e_end
        v_all = v_ref[0]
        P2 = 2 * N
        pairs = [(cc, p) for cc in range(nck) for p in range(RW_HEADS // 2)]
        pr = range(len(pairs))

        def pb(z, pi):
            cc, p = pairs[pi]
            return z[cc * L:(cc + 1) * L, p * P2:(p + 1) * P2]

        def stack2(z):
            lane = lax.broadcasted_iota(jnp.int32, (1, z.shape[1]), 1) & (P2 - 1)
            return jnp.concatenate([jnp.where(lane < N, z, 0.0), jnp.where(lane >= N, z, 0.0)], axis=0)

        tcol = lax.broadcasted_iota(jnp.int32, (L, P2), 1) & (N - 1)
        trow = lax.broadcasted_iota(jnp.int32, (L, P2), 0)
        strict = trow > tcol
        lower = trow >= tcol
        r2 = lax.broadcasted_iota(jnp.int32, (P2, P2), 0)
        c2 = lax.broadcasted_iota(jnp.int32, (P2, P2), 1)
        same_head = (r2 // N) == (c2 // N)
        eye2 = r2 == c2
        x2 = [jnp.concatenate([pb(at, pi), pb(rt, pi)], axis=0) for pi in pr]
        m_b = [_dot_nt(x2[pi], stack2(pb(bt, pi))) for pi in pr]
        m_k = [_dot_nt(x2[pi], stack2(pb(kt, pi))) for pi in pr]
        ap = [jnp.where(strict, m_b[pi][0:L], 0.0) for pi in pr]
        a_rb = [jnp.where(lower, m_b[pi][L:2 * L], 0.0) for pi in pr]
        a_ak = [jnp.where(strict, m_k[pi][0:L], 0.0) for pi in pr]
        a_rk = [jnp.where(lower, m_k[pi][L:2 * L], 0.0) for pi in pr]
        vp = [_dot(jnp.concatenate([a_ak[pi], a_rk[pi]], axis=0), stack2(pb(v_all, pi))) for pi in pr]
        kv = [_dot(pb(k_end, pi).T, pb(v_all, pi)) for pi in pr]
        x = [jnp.concatenate([pb(at, pi), vp[pi][0:L]], axis=1) for pi in pr]
        for it in range(6):
            x = [x[pi] + _dot(ap[pi], stack2(x[pi])) for pi in pr]
            if it < 5:
                ap = [_dot(ap[pi], stack2(ap[pi])) for pi in pr]
        post1 = [_dot(a_rb[pi], stack2(x[pi])) for pi in pr]
        post2 = [_dot(pb(b_end, pi).T, x[pi]) for pi in pr]
        lhs = []
        y0s = []
        h_adds = []
        for pi in pr:
            q_hat = pb(rt, pi) + post1[pi][:, 0:P2]
            gmat = (jnp.where(same_head, post2[pi][:, 0:P2], 0.0)
                    + jnp.where(eye2, pb(p_end, pi)[0:1, :], 0.0))
            lhs.append(jnp.concatenate([q_hat, gmat], axis=0))
            y0s.append(post1[pi][:, P2:2 * P2] + vp[pi][L:2 * L])
            h_adds.append(jnp.where(same_head, post2[pi][:, P2:2 * P2] + kv[pi], 0.0))
        npair = RW_HEADS // 2
        h_st = [h_ref[p] for p in range(npair)]
        ys = [None] * len(pairs)
        for cc in range(nck):
            res = [_dot(lhs[cc * npair + p], h_st[p]) for p in range(npair)]
            for p in range(npair):
                pi = cc * npair + p
                ys[pi] = res[p][0:L] + y0s[pi]
                h_st[p] = res[p][L:L + P2] + h_adds[pi]
        for p in range(npair):
            h_ref[p] = h_st[p]
        seg = same_head.astype(BF16)
        mean = [_seg_sum(ys[pi], seg) * (1.0 / N) for pi in pr]
        yc = [ys[pi] - mean[pi] for pi in pr]
        var = [_seg_sum(yc[pi] * yc[pi], seg) * (1.0 / N) for pi in pr]
        for pi, (cc, p) in enumerate(pairs):
            yn = yc[pi] * lax.rsqrt(var[pi] + RW_LN_EPS)
            rows = slice(cc * L, (cc + 1) * L)
            cols = slice(p * P2, (p + 1) * P2)
            out_ref[0, rows, cols] = ((yn * lng_ref[:, cols] + lnb_ref[:, cols] + bonus_ref[0, rows, cols])
                                      * g_ref[0, rows, cols]).astype(BF16)

    blk = pl.BlockSpec((1, tb, W), lambda b, c: (b, c, 0))
    vec = pl.BlockSpec((1, W), lambda b, c: (0, 0))
    return pl.pallas_call(
        body,
        grid=(bsz, t // tb),
        in_specs=[blk] * 8 + [vec, vec],
        out_specs=blk,
        out_shape=jax.ShapeDtypeStruct((bsz, t, W), BF16),
        scratch_shapes=[pltpu.VMEM((RW_HEADS // 2, 2 * N, 2 * N), F32)],
        compiler_params=_cp("parallel", "arbitrary"),
        name="rwkv_scan",
    )(r, lw, k, v, aa, bb, g_out, bonus, ln_g, ln_b)


def _mix_out_route(a, b, w_bf16, resid, g, router_split, tm=1024):
    n, wa = a.shape
    wb = b.shape[1]
    d = w_bf16.shape[1]
    e = router_split.shape[1] // 2

    def body(a_ref, b_ref, w_ref, r_ref, g_ref, rt_ref, h_ref, xn_ref, lg_ref):
        acc = jnp.dot(a_ref[...], w_ref[0:wa, :], preferred_element_type=F32)
        acc = acc + jnp.dot(b_ref[...], w_ref[wa:wa + wb, :], preferred_element_type=F32)
        h = r_ref[...] + acc
        h_ref[...] = h
        xn = _rms_rows(h, g_ref[...])
        x_hi = xn.astype(BF16)
        xn_ref[...] = x_hi
        x_lo = (xn - x_hi.astype(F32)).astype(BF16)
        hi_both = jnp.dot(x_hi, rt_ref[...], preferred_element_type=F32)
        lo_hi = jnp.dot(x_lo, rt_ref[:, 0:e], preferred_element_type=F32)
        lg_ref[...] = hi_both[:, 0:e] + (lo_hi + hi_both[:, e:2 * e])

    return pl.pallas_call(
        body,
        grid=(n // tm,),
        in_specs=[pl.BlockSpec((tm, wa), lambda i: (i, 0)),
                  pl.BlockSpec((tm, wb), lambda i: (i, 0)),
                  pl.BlockSpec((wa + wb, d), lambda i: (0, 0)),
                  pl.BlockSpec((tm, d), lambda i: (i, 0)),
                  pl.BlockSpec((1, d), lambda i: (0, 0)),
                  pl.BlockSpec((d, 2 * e), lambda i: (0, 0))],
        out_specs=[pl.BlockSpec((tm, d), lambda i: (i, 0)), pl.BlockSpec((tm, d), lambda i: (i, 0)),
                   pl.BlockSpec((tm, e), lambda i: (i, 0))],
        out_shape=[jax.ShapeDtypeStruct((n, d), F32), jax.ShapeDtypeStruct((n, d), BF16),
                   jax.ShapeDtypeStruct((n, e), F32)],
        compiler_params=_cp("parallel"),
        name="mix_out_route",
    )(a, b, w_bf16, resid, g.reshape(1, d), router_split)


MOE_TM = 512
MOE_TF = 1792


def _experts(xs, row_w, item_tile, item_exp, item_lo, item_hi, wg, wu, wd):
    nrows, d = xs.shape
    tm, tf = MOE_TM, MOE_TF
    nf = D_FF // tf
    n_items = item_tile.shape[0]

    def body(it_ref, ie_ref, lo_ref, hi_ref, x_ref, w_ref, wg_ref, wu_ref, wd_ref, o_ref, acc_ref):
        i = pl.program_id(0)
        j = pl.program_id(1)
        tile = it_ref[i]
        first = jnp.logical_or(i == 0, tile != it_ref[jnp.maximum(i - 1, 0)])
        last = jnp.logical_or(i == n_items - 1, tile != it_ref[jnp.minimum(i + 1, n_items - 1)])

        @pl.when(jnp.logical_and(first, j == 0))
        def _():
            acc_ref[...] = jnp.zeros_like(acc_ref)

        lo = lo_ref[i]
        hi = hi_ref[i]

        @pl.when(lo < hi)
        def _():
            x = x_ref[...]
            gg = jnp.dot(x, wg_ref[0], preferred_element_type=F32)
            uu = jnp.dot(x, wu_ref[0], preferred_element_type=F32)
            act = (_silu(gg) * uu).astype(BF16)
            part = jnp.dot(act, wd_ref[0], preferred_element_type=F32)
            rowi = lax.broadcasted_iota(jnp.int32, (tm, 1), 0)
            mine = jnp.logical_and(rowi >= lo, rowi < hi)
            acc_ref[...] += part * jnp.where(mine, w_ref[...], 0.0)

        @pl.when(jnp.logical_and(last, j == nf - 1))
        def _():
            o_ref[...] = acc_ref[...].astype(o_ref.dtype)

    grid_spec = pltpu.PrefetchScalarGridSpec(
        num_scalar_prefetch=4,
        grid=(n_items, nf),
        in_specs=[pl.BlockSpec((tm, d), lambda i, j, it, ie, lo, hi: (it[i], 0)),
                  pl.BlockSpec((tm, 1), lambda i, j, it, ie, lo, hi: (it[i], 0)),
                  pl.BlockSpec((1, d, tf), lambda i, j, it, ie, lo, hi: (ie[i], 0, j)),
                  pl.BlockSpec((1, d, tf), lambda i, j, it, ie, lo, hi: (ie[i], 0, j)),
                  pl.BlockSpec((1, tf, d), lambda i, j, it, ie, lo, hi: (ie[i], j, 0))],
        out_specs=pl.BlockSpec((tm, d), lambda i, j, it, ie, lo, hi: (it[i], 0)),
        scratch_shapes=[pltpu.VMEM((tm, d), F32)],
    )
    return pl.pallas_call(
        body,
        grid_spec=grid_spec,
        out_shape=jax.ShapeDtypeStruct((nrows, d), BF16),
        compiler_params=_cp("arbitrary", "arbitrary"),
        name="moe_experts",
    )(item_tile, item_exp, item_lo, item_hi, xs, row_w, wg, wu, wd)


def _combine_norm(h, y01, g, tm=512):
    n, d = h.shape

    def body(h_ref, y_ref, g_ref, o_ref):
        y = y_ref[:, 0:d].astype(F32) + y_ref[:, d:2 * d].astype(F32)
        o_ref[...] = _rms_rows(h_ref[...] + y, g_ref[...])

    blk = pl.BlockSpec((tm, d), lambda i: (i, 0))
    return pl.pallas_call(
        body,
        grid=(n // tm,),
        in_specs=[blk, pl.BlockSpec((tm, TOP_K * d), lambda i: (i, 0)), pl.BlockSpec((1, d), lambda i: (0, 0))],
        out_specs=blk,
        out_shape=jax.ShapeDtypeStruct((n, d), F32),
        compiler_params=_cp("parallel"),
        name="combine_norm",
    )(h, y01, g.reshape(1, d))


def _route(logits, n):
    tm = MOE_TM
    na = n * TOP_K
    n_tiles = na // tm
    top_val, top_idx = lax.top_k(logits, TOP_K)
    top_w = jax.nn.softmax(top_val, axis=-1)
    e_flat = top_idx.reshape(-1).astype(jnp.int32)
    w_flat = top_w.reshape(-1)
    ids = jnp.arange(na, dtype=jnp.int32)
    e_sorted, sorted_a, sorted_w = lax.sort((e_flat, ids, w_flat), num_keys=1, is_stable=True)
    sorted_tok = sorted_a // TOP_K
    _, pos = lax.sort((sorted_a, ids), num_keys=1)
    ends = jnp.sum((e_sorted[None, :] <= jnp.arange(N_EXPERTS, dtype=jnp.int32)[:, None]).astype(jnp.int32), axis=1)
    cuts = jnp.sort(jnp.concatenate([jnp.arange(n_tiles, dtype=jnp.int32) * tm, ends[:-1].astype(jnp.int32)]))
    nxt = jnp.concatenate([cuts[1:], jnp.full((1,), na, jnp.int32)])
    item_tile = jnp.minimum(cuts // tm, n_tiles - 1)
    item_exp = jnp.minimum(jnp.sum((ends[None, :] <= cuts[:, None]).astype(jnp.int32), axis=1), N_EXPERTS - 1)
    item_lo = cuts - item_tile * tm
    item_hi = nxt - item_tile * tm
    return sorted_tok, sorted_w, (item_tile, item_exp, item_lo, item_hi), pos.reshape(n, TOP_K)


def kernel(x, e_norm1_g, e_w_in, e_ml_conv_w, e_ml_conv_b, e_ml_gate_b, e_ml_norm_g, e_gla_gate_up, e_gla_gate_b,
           e_gla_norm_g, e_w_out, e_norm2_g, e_ffn_w_gate, e_ffn_w_up, e_ffn_w_down, o_norm1_g, o_w_in,
           o_ret_norm_g, o_rw_mu, o_rw_w_up, o_rw_w0, o_rw_a_up, o_rw_a0, o_rw_g_up, o_rw_k_k, o_rw_k_a, o_rw_r_k,
           o_rw_ln_g, o_rw_ln_b, o_w_out, o_norm2_g, o_moe_router, o_moe_w_gate, o_moe_w_up, o_moe_w_down,
           final_norm_g):
    bsz, t, d = x.shape
    n = bsz * t
    h0 = x.reshape(n, d)

    w = e_w_in[0]
    row = lambda a: a.reshape(1, -1)
    x3 = x
    ng1 = row(e_norm1_g[0])
    w_if = w[:, 2048:2056]
    w_gate = jnp.zeros((d, 128), F32).at[:, :2 * ML_HEADS].set(w_if).astype(BF16)
    w_gate_t = jnp.zeros((16, d), F32).at[:2 * ML_HEADS, :].set(w_if.T).astype(BF16)
    h_ml = _mlstm(x3, ng1, w[:, :4 * ML_W].astype(BF16), w_gate, w_gate_t, row(e_ml_gate_b[0]),
                  e_ml_gate_b[0].reshape(-1, 1), e_ml_conv_w[0], row(e_ml_conv_b[0]), row(e_ml_norm_g[0]))
    gq, gk, gv, gr = 2056, 2312, 2568, 3080
    w_pairs = jnp.stack([jnp.concatenate([w[:, gq + 128 * hp:gq + 128 * (hp + 1)],
                                          w[:, gk + 128 * hp:gk + 128 * (hp + 1)],
                                          w[:, gv + 256 * hp:gv + 256 * (hp + 1)],
                                          w[:, gr + 256 * hp:gr + 256 * (hp + 1)]], axis=1)
                         for hp in range(GLA_HEADS // 2)]).astype(BF16)
    w_low = jnp.zeros((d, 128), F32).at[:, :GLA_RANK].set(w[:, 3592:3608]).astype(BF16)
    o_gla = _gla(x3, ng1, w_pairs, w_low, e_gla_gate_up[0].astype(BF16), row(e_gla_gate_b[0]),
                 row(e_gla_norm_g[0]))
    h2 = _mix_out_ffn(h_ml.reshape(n, -1), o_gla.reshape(n, -1), e_w_out[0].astype(BF16), h0, e_norm2_g[0],
                      e_ffn_w_gate[0].astype(BF16), e_ffn_w_up[0].astype(BF16), e_ffn_w_down[0].astype(BF16))

    w = o_w_in[0]
    h2_3 = h2.reshape(bsz, t, d)
    ng2 = row(o_norm1_g[0])
    y_ret = _retention(h2_3, ng2, w[:, :4 * RET_W].astype(BF16), *_retention_tables(t), row(o_ret_norm_g[0]))
    head_of = jnp.arange(RW_W) // RW_DIM
    bd = (head_of[:, None] == head_of[None, :]).astype(BF16)
    r, lw, k, v, aa, bb, g_out, bonus = _rwkv_prep(
        h2_3, ng2, w[:, 4 * RET_W:].astype(BF16), row(o_rw_mu[0]), o_rw_w_up[0].astype(BF16), row(o_rw_w0[0]),
        o_rw_a_up[0].astype(BF16), row(o_rw_a0[0]), o_rw_g_up[0].astype(BF16), row(o_rw_k_k[0]),
        row(o_rw_k_a[0]), row(o_rw_r_k[0]), bd)
    y_rw = _rwkv_scan(r, lw, k, v, aa, bb, g_out, bonus, row(o_rw_ln_g[0]), row(o_rw_ln_b[0]))
    router_pad = jnp.zeros((d, 128), F32).at[:, :N_EXPERTS].set(o_moe_router[0])
    router_hi = router_pad.astype(BF16)
    router_split = jnp.concatenate([router_hi, (router_pad - router_hi.astype(F32)).astype(BF16)], axis=1)
    h3, xn, logits = _mix_out_route(y_ret.reshape(n, -1), y_rw.reshape(n, -1), o_w_out[0].astype(BF16), h2,
                                    o_norm2_g[0], router_split)
    sorted_tok, sorted_w, items, pos = _route(logits[:, :N_EXPERTS], n)
    xs = xn.at[sorted_tok].get(mode="promise_in_bounds")
    ys = _experts(xs, sorted_w.reshape(-1, 1), *items, o_moe_w_gate[0].astype(BF16),
                  o_moe_w_up[0].astype(BF16), o_moe_w_down[0].astype(BF16))
    y01 = ys.at[pos.reshape(-1)].get(mode="promise_in_bounds").reshape(n, TOP_K * d)
    out = _combine_norm(h3, y01, final_norm_g)
    return out.reshape(bsz, t, d)
```

```python
import functools

import numpy as np
import jax
import jax.numpy as jnp
from jax import lax
from jax.experimental import pallas as pl
from jax.experimental.pallas import tpu as pltpu

F32 = jnp.float32
BF16 = jnp.bfloat16

D_MODEL = 1024
EPS = 1e-6
ML_HEADS, ML_DIM, ML_W, ML_CONV = 4, 128, 512, 4
GLA_HEADS, GLA_DK, GLA_DV, GLA_RANK, GLA_TAU = 4, 64, 128, 16, 16.0
RET_HEADS, RET_DIM, RET_W = 4, 128, 512
ROPE_BASE = 10000.0
RW_HEADS, RW_DIM, RW_W = 8, 64, 512
RW_COLS = 1792
RW_LN_EPS = 64e-5
D_FF = 3584
N_EXPERTS = 8
TOP_K = 2

VMEM_LIMIT = 48 * 1024 * 1024
NEG = -1e30


def _cp(*sem):
    return pltpu.CompilerParams(dimension_semantics=sem, vmem_limit_bytes=VMEM_LIMIT)


def _sigmoid(x):
    return 1.0 / (1.0 + jnp.exp(-x))


def _silu(x):
    return x * _sigmoid(x)


def _log_sigmoid(x):
    return jnp.minimum(x, 0.0) - jnp.log(1.0 + jnp.exp(-jnp.abs(x)))


def _dot(a, b):
    return jnp.dot(a.astype(BF16), b.astype(BF16), preferred_element_type=F32)


def _dot_nt(a, b):
    return lax.dot_general(a.astype(BF16), b.astype(BF16), (((1,), (1,)), ((), ())), preferred_element_type=F32)


def _dot_tn(a, b):
    return jnp.dot(a.T.astype(BF16), b.astype(BF16), preferred_element_type=F32)


def _rms_rows(x, g):
    ms = jnp.mean(x * x, axis=-1, keepdims=True)
    return x * lax.rsqrt(ms + EPS) * g


def _mix_out_ffn(a, b, w_out, resid, g, wg, wu, wd, tm=512, tf=1792):
    n, d = resid.shape
    wa = a.shape[1]
    wb = b.shape[1]
    f = wg.shape[1]
    nf = f // tf

    def body(a_ref, b_ref, wo_ref, r_ref, g_ref, wg_ref, wu_ref, wd_ref, o_ref, xn_ref, acc_ref):
        j = pl.program_id(1)

        @pl.when(j == 0)
        def _():
            h = r_ref[...] + jnp.dot(a_ref[...], wo_ref[0:wa, :], preferred_element_type=F32)
            h = h + jnp.dot(b_ref[...], wo_ref[wa:wa + wb, :], preferred_element_type=F32)
            xn_ref[...] = _rms_rows(h, g_ref[...]).astype(BF16)
            acc_ref[...] = h

        xn = xn_ref[...]
        gg = jnp.dot(xn, wg_ref[...], preferred_element_type=F32)
        uu = jnp.dot(xn, wu_ref[...], preferred_element_type=F32)
        act = (_silu(gg) * uu).astype(BF16)
        acc_ref[...] += jnp.dot(act, wd_ref[...], preferred_element_type=F32)

        @pl.when(j == nf - 1)
        def _():
            o_ref[...] = acc_ref[...]

    return pl.pallas_call(
        body,
        grid=(n // tm, nf),
        in_specs=[pl.BlockSpec((tm, wa), lambda i, j: (i, 0)),
                  pl.BlockSpec((tm, wb), lambda i, j: (i, 0)),
                  pl.BlockSpec((wa + wb, d), lambda i, j: (0, 0)),
                  pl.BlockSpec((tm, d), lambda i, j: (i, 0)),
                  pl.BlockSpec((1, d), lambda i, j: (0, 0)),
                  pl.BlockSpec((d, tf), lambda i, j: (0, j)),
                  pl.BlockSpec((d, tf), lambda i, j: (0, j)),
                  pl.BlockSpec((tf, d), lambda i, j: (j, 0))],
        out_specs=pl.BlockSpec((tm, d), lambda i, j: (i, 0)),
        out_shape=jax.ShapeDtypeStruct((n, d), F32),
        scratch_shapes=[pltpu.VMEM((tm, d), BF16), pltpu.VMEM((tm, d), F32)],
        compiler_params=_cp("parallel", "arbitrary"),
        name="mix_out_ffn",
    )(a, b, w_out, resid, g.reshape(1, d), wg, wu, wd)


ML_TB = 1024
ML_L = 256


def _mlstm(x, in_g, w_in, w_gate, w_gate_t, gate_b_row, gate_b_col, conv_w, conv_b, norm_g):
    bsz, t, d = x.shape
    TB, L, D, H, W = ML_TB, ML_L, ML_DIM, ML_HEADS, ML_W

    def body(x_ref, ing_ref, w_ref, wg_ref, wgt_ref, gbr_ref, gbc_ref, cwq_ref, cwk_ref, cbq_ref, cbk_ref,
             ng_ref, out_ref, qext, kext, c_ref, n_ref, m_ref):
        c = pl.program_id(1)

        @pl.when(c == 0)
        def _():
            qext[0:8, :] = jnp.zeros((8, W), F32)
            kext[0:8, :] = jnp.zeros((8, W), F32)
            c_ref[...] = jnp.zeros_like(c_ref)
            n_ref[...] = jnp.zeros_like(n_ref)
            m_ref[...] = jnp.zeros_like(m_ref)

        def project(cc):
            xn = _rms_rows(x_ref[0, cc * L:(cc + 1) * L, :], ing_ref[...]).astype(BF16)
            pc = jnp.dot(xn, w_ref[...], preferred_element_type=F32)
            gc = jnp.dot(xn, wg_ref[...], preferred_element_type=F32)[:, 0:2 * H]
            gr = lax.dot_general(wgt_ref[...], xn, (((1,), (1,)), ((), ())),
                                 preferred_element_type=F32)[0:2 * H, :]
            return pc, gc, gr

        def conv(ext, base, cw_ref, cb_ref):
            acc = cb_ref[...] + cw_ref[0:1, :] * ext[pl.ds(base + 8 - ML_CONV + 1, L), :]
            for kk in range(1, ML_CONV):
                acc = acc + cw_ref[kk:kk + 1, :] * ext[pl.ds(base + 8 - ML_CONV + 1 + kk, L), :]
            return _silu(acc)

        ri = lax.broadcasted_iota(jnp.int32, (L, L), 0)
        ci = lax.broadcasted_iota(jnp.int32, (L, L), 1)
        causal = ri >= ci
        heads = range(H)
        c_st = [c_ref[h] for h in heads]
        n_st = [n_ref[h] for h in heads]
        m_st = [m_ref[h] for h in heads]
        nxt = project(0)
        for cc in range(TB // L):
            rows = slice(cc * L, (cc + 1) * L)
            p, g_cols, g_rows = nxt
            if cc + 1 < TB // L:
                nxt = project(cc + 1)
            qext[8 + cc * L:8 + (cc + 1) * L, :] = p[:, 0:W]
            kext[8 + cc * L:8 + (cc + 1) * L, :] = p[:, W:2 * W]
            q_all = conv(qext, cc * L, cwq_ref, cbq_ref) * (D ** -0.5)
            k_all = conv(kext, cc * L, cwk_ref, cbk_ref)
            gcol = g_cols + gbr_ref[...]
            grow = g_rows + gbc_ref[...]
            fcol = _log_sigmoid(gcol[:, H:2 * H])
            frow = _log_sigmoid(grow[H:2 * H, :])
            hs = lambda z, h: z[:, h * D:(h + 1) * D]
            w_intra, w_inter, w_state, carry, m_row = [], [], [], [], []
            for h in heads:
                f_row = frow[h:h + 1, :]
                i_row = grow[h:h + 1, :]
                f_col = fcol[:, h:h + 1]
                i_col = gcol[:, h:h + 1]
                b_col = jnp.sum(jnp.where(causal, f_row, 0.0), axis=1, keepdims=True)
                b_row = jnp.sum(jnp.where(ri <= ci, f_col, 0.0), axis=0, keepdims=True)
                g_tot = jnp.sum(f_row, axis=1, keepdims=True)
                d_intra = jnp.where(causal, b_col - b_row + i_row, NEG)
                d_inter = b_col + m_st[h]
                mr = jnp.maximum(d_inter, jnp.max(d_intra, axis=1, keepdims=True))
                w_intra.append(jnp.exp(d_intra - mr))
                w_inter.append(jnp.exp(d_inter - mr))
                m_row.append(mr)
                d_state = g_tot - b_col + i_col
                mn = jnp.maximum(g_tot + m_st[h], jnp.max(d_state, axis=0, keepdims=True))
                w_state.append(jnp.exp(d_state - mn))
                carry.append(jnp.exp(g_tot + m_st[h] - mn))
                m_st[h] = mn
            qh = [hs(q_all, h) for h in heads]
            kh = [hs(k_all, h) for h in heads]
            vh = [p[:, 2 * W + h * D:2 * W + (h + 1) * D] for h in heads]
            s = [_dot_nt(qh[h], kh[h]) * w_intra[h] for h in heads]
            qc = [_dot(qh[h], c_st[h]) for h in heads]
            kw = [kh[h] * w_state[h] for h in heads]
            kv = [_dot_tn(kw[h], vh[h]) for h in heads]
            sv = [_dot(s[h], vh[h]) for h in heads]
            for h in heads:
                num = sv[h] + w_inter[h] * qc[h]
                den = (jnp.sum(s[h], axis=1, keepdims=True)
                       + w_inter[h] * jnp.sum(qh[h] * n_st[h], axis=1, keepdims=True))
                hval = num / jnp.maximum(jnp.abs(den), jnp.exp(-m_row[h]))
                c_st[h] = carry[h] * c_st[h] + kv[h]
                n_st[h] = carry[h] * n_st[h] + jnp.sum(kw[h], axis=0, keepdims=True)
                hg = _sigmoid(p[:, 3 * W + h * D:3 * W + (h + 1) * D]) * hval
                hc = hg - jnp.mean(hg, axis=1, keepdims=True)
                hn = hc * lax.rsqrt(jnp.mean(hc * hc, axis=1, keepdims=True) + EPS)
                out_ref[0, rows, h * D:(h + 1) * D] = (hn * ng_ref[:, h * D:(h + 1) * D]).astype(BF16)
        for h in heads:
            c_ref[h] = c_st[h]
            n_ref[h] = n_st[h]
            m_ref[h] = m_st[h]
        qext[0:8, :] = qext[TB:TB + 8, :]
        kext[0:8, :] = kext[TB:TB + 8, :]

    fix = lambda j: (lambda b, c: (0, j))
    return pl.pallas_call(
        body,
        grid=(bsz, t // TB),
        in_specs=[pl.BlockSpec((1, TB, d), lambda b, c: (b, c, 0)),
                  pl.BlockSpec((1, d), fix(0)),
                  pl.BlockSpec((d, 4 * W), fix(0)),
                  pl.BlockSpec((d, 128), fix(0)),
                  pl.BlockSpec((16, d), fix(0)),
                  pl.BlockSpec((1, 2 * H), fix(0)),
                  pl.BlockSpec((2 * H, 1), fix(0)),
                  pl.BlockSpec((ML_CONV, W), fix(0)),
                  pl.BlockSpec((ML_CONV, W), fix(1)),
                  pl.BlockSpec((1, W), fix(0)),
                  pl.BlockSpec((1, W), fix(1)),
                  pl.BlockSpec((1, W), fix(0))],
        out_specs=pl.BlockSpec((1, TB, W), lambda b, c: (b, c, 0)),
        out_shape=jax.ShapeDtypeStruct((bsz, t, W), BF16),
        scratch_shapes=[pltpu.VMEM((TB + 8, W), F32), pltpu.VMEM((TB + 8, W), F32),
                        pltpu.VMEM((H, D, D), F32), pltpu.VMEM((H, 1, D), F32), pltpu.VMEM((H, 1, 1), F32)],
        compiler_params=_cp("parallel", "arbitrary"),
        name="mlstm",
    )(x, in_g, w_in, w_gate, w_gate_t, gate_b_row, gate_b_col, conv_w, conv_w, conv_b, conv_b, norm_g)


GLA_TC = 1024
GLA_SUB = 16
GLA_GROUP = 128


def _gla(x, in_g, w_pairs, w_low, gate_up, gate_b, norm_g):
    bsz, t, d = x.shape
    tc, S, GB = GLA_TC, GLA_SUB, GLA_GROUP
    head_ones = (jnp.arange(2 * GLA_DK)[:, None] // GLA_DK == jnp.arange(2 * GB)[None, :] // GB).astype(BF16)
    nsub = tc // S
    dk, dv = GLA_DK, GLA_DV

    def body(x_ref, ing_ref, w_ref, wl_ref, gu_ref, gbias_ref, ng_ref, ones_ref, out_ref, st_ref, ksh, bsh):
        c = pl.program_id(2)

        @pl.when(c == 0)
        def _():
            st_ref[...] = jnp.zeros_like(st_ref)
            ksh[0:S, :] = jnp.zeros((S, 2 * dk), F32)
            bsh[0:S, :] = jnp.zeros((S, 2 * dk), F32)

        xn = _rms_rows(x_ref[0], ing_ref[...]).astype(BF16)
        p = jnp.dot(xn, w_ref[0], preferred_element_type=F32)
        g_low = jnp.dot(xn, wl_ref[...], preferred_element_type=F32)[:, 0:GLA_RANK]
        z = _dot(g_low, gu_ref[...]) + gbias_ref[...]
        la = _log_sigmoid(z) / GLA_TAU
        rowi = lax.broadcasted_iota(jnp.int32, (tc, 1), 0)
        rmod = rowi & (S - 1)
        bcum = la
        rsum = la
        for s in (1, 2, 4, 8):
            bcum = bcum + jnp.where(rmod >= s, pltpu.roll(bcum, s, 0), 0.0)
            rsum = rsum + jnp.where(rmod < S - s, pltpu.roll(rsum, tc - s, 0), 0.0)
        q = p[:, 0:2 * dk] * (dk ** -0.5)
        k = p[:, 2 * dk:4 * dk]
        v = p[:, 4 * dk:4 * dk + 2 * dv]
        gate = p[:, 4 * dk + 2 * dv:4 * dk + 4 * dv]
        qt = q * jnp.exp(bcum)
        kt = k * jnp.exp(rsum - la)
        eg = jnp.exp(bcum + rsum - la)

        ksh[S:, :] = k
        bsh[S:, :] = bcum
        prods = []
        for d in range(S):
            kd = k if d == 0 else ksh[pl.ds(S - d, tc), :]
            bd = bcum if d == 0 else bsh[pl.ds(S - d, tc), :]
            e = jnp.exp(jnp.where(rmod >= d, bcum - bd, 0.0))
            prods.append((q * kd * e).astype(BF16))
        ws = [jnp.dot(p, ones_ref[...], preferred_element_type=F32) for p in prods]
        coli = lax.broadcasted_iota(jnp.int32, (tc, GB), 1)
        rgrp = rowi & (GB - 1)
        att0 = jnp.zeros((tc, GB), F32)
        att1 = jnp.zeros((tc, GB), F32)
        offs = jnp.where((coli // S) == (rgrp // S), rgrp - coli, -1)
        for d in range(S):
            here = offs == d
            att0 = jnp.where(here, ws[d][:, 0:GB], att0)
            att1 = jnp.where(here, ws[d][:, GB:2 * GB], att1)

        heads = range(2)
        lk = [slice(hh * dk, (hh + 1) * dk) for hh in heads]
        lv = [slice(hh * dv, (hh + 1) * dv) for hh in heads]
        kv = [[_dot_tn(v[si * S:(si + 1) * S, lv[hh]], kt[si * S:(si + 1) * S, lk[hh]]) for hh in heads]
              for si in range(nsub)]
        st = [st_ref[hh] for hh in heads]
        inter = [[], []]
        for si in range(nsub):
            rows = slice(si * S, (si + 1) * S)
            for hh in heads:
                inter[hh].append(_dot_nt(qt[rows, lk[hh]], st[hh]))
                st[hh] = st[hh] * eg[si * S:si * S + 1, lk[hh]] + kv[si][hh]
        for hh in heads:
            st_ref[hh] = st[hh]

        for hh, att in ((0, att0), (1, att1)):
            diag = jnp.concatenate([_dot(att[g * GB:(g + 1) * GB], v[g * GB:(g + 1) * GB, lv[hh]])
                                    for g in range(tc // GB)], axis=0)
            o = diag + jnp.concatenate(inter[hh], axis=0)
            on = o * lax.rsqrt(jnp.mean(o * o, axis=1, keepdims=True) + EPS)
            out_ref[0, :, lv[hh]] = (on * ng_ref[:, lv[hh]] * _silu(gate[:, lv[hh]])).astype(BF16)

    pw = 4 * dk + 4 * dv
    return pl.pallas_call(
        body,
        grid=(bsz, GLA_HEADS // 2, t // tc),
        in_specs=[pl.BlockSpec((1, tc, d), lambda b, h, c: (b, c, 0)),
                  pl.BlockSpec((1, d), lambda b, h, c: (0, 0)),
                  pl.BlockSpec((1, d, pw), lambda b, h, c: (h, 0, 0)),
                  pl.BlockSpec((d, 128), lambda b, h, c: (0, 0)),
                  pl.BlockSpec((GLA_RANK, 2 * dk), lambda b, h, c: (0, h)),
                  pl.BlockSpec((1, 2 * dk), lambda b, h, c: (0, h)),
                  pl.BlockSpec((1, 2 * dv), lambda b, h, c: (0, h)),
                  pl.BlockSpec((2 * dk, 2 * GB), lambda b, h, c: (0, 0))],
        out_specs=pl.BlockSpec((1, tc, 2 * dv), lambda b, h, c: (b, c, h)),
        out_shape=jax.ShapeDtypeStruct((bsz, t, GLA_HEADS * dv), BF16),
        scratch_shapes=[pltpu.VMEM((2, dv, dk), F32), pltpu.VMEM((tc + S, 2 * dk), F32),
                        pltpu.VMEM((tc + S, 2 * dk), F32)],
        compiler_params=_cp("parallel", "parallel", "arbitrary"),
        name="gla",
    )(x, in_g, w_pairs, w_low, gate_up, gate_b, norm_g, head_ones)


RET_TB = 1024
RET_L = 256


def _retention(x, in_g, w_in, cos_t, sin_t, intra, inter, sdec, cdec, norm_g):
    bsz, t, d = x.shape
    TB, L, D, H, W = RET_TB, RET_L, RET_DIM, RET_HEADS, RET_W

    def body(x_ref, ing_ref, w_ref, cos_ref, sin_ref, intra_ref, inter_ref, sdec_ref, cdec_ref, ng_ref,
             out_ref, s_ref):
        c = pl.program_id(1)

        @pl.when(c == 0)
        def _():
            s_ref[...] = jnp.zeros_like(s_ref)

        p = jnp.dot(_rms_rows(x_ref[0], ing_ref[...]).astype(BF16), w_ref[...], preferred_element_type=F32)
        heads = range(H)
        s_st = [s_ref[h] for h in heads]
        for cc in range(TB // L):
            rows = slice(cc * L, (cc + 1) * L)
            cs = cos_ref[rows, :]
            sn = sin_ref[rows, :]

            def rot(z):
                return z * cs + pltpu.roll(z, D // 2, 1) * sn

            q = [rot(p[rows, h * D:(h + 1) * D]) * (D ** -0.5) for h in heads]
            k = [rot(p[rows, W + h * D:W + (h + 1) * D]) for h in heads]
            v = [p[rows, 2 * W + h * D:2 * W + (h + 1) * D] for h in heads]
            s = [_dot_nt(q[h], k[h]) * intra_ref[h] for h in heads]
            qs = [_dot(q[h], s_st[h]) for h in heads]
            kv = [_dot_tn(k[h] * sdec_ref[h], v[h]) for h in heads]
            sv = [_dot(s[h], v[h]) for h in heads]
            for h in heads:
                o = sv[h] + inter_ref[h] * qs[h]
                s_st[h] = cdec_ref[h] * s_st[h] + kv[h]
                oc = o - jnp.mean(o, axis=1, keepdims=True)
                on = oc * lax.rsqrt(jnp.mean(oc * oc, axis=1, keepdims=True) + EPS)
                hl = slice(h * D, (h + 1) * D)
                gate = p[rows, 3 * W + h * D:3 * W + (h + 1) * D]
                out_ref[0, rows, hl] = (on * ng_ref[:, hl] * _silu(gate)).astype(BF16)
        for h in heads:
            s_ref[h] = s_st[h]

    fix2 = lambda b, c: (0, 0)
    fix3 = lambda b, c: (0, 0, 0)
    return pl.pallas_call(
        body,
        grid=(bsz, t // TB),
        in_specs=[pl.BlockSpec((1, TB, d), lambda b, c: (b, c, 0)),
                  pl.BlockSpec((1, d), fix2),
                  pl.BlockSpec((d, 4 * W), fix2),
                  pl.BlockSpec((TB, D), lambda b, c: (c, 0)),
                  pl.BlockSpec((TB, D), lambda b, c: (c, 0)),
                  pl.BlockSpec((H, L, L), fix3),
                  pl.BlockSpec((H, L, 1), fix3),
                  pl.BlockSpec((H, L, 1), fix3),
                  pl.BlockSpec((H, 1, 1), fix3),
                  pl.BlockSpec((1, W), lambda b, c: (0, 0))],
        out_specs=pl.BlockSpec((1, TB, W), lambda b, c: (b, c, 0)),
        out_shape=jax.ShapeDtypeStruct((bsz, t, W), BF16),
        scratch_shapes=[pltpu.VMEM((H, D, D), F32)],
        compiler_params=_cp("parallel", "arbitrary"),
        name="retention",
    )(x, in_g, w_in, cos_t, sin_t, intra, inter, sdec, cdec, norm_g)


def _retention_tables(t):
    L, D = RET_L, RET_DIM
    inv = ROPE_BASE ** (-jnp.arange(0, D, 2, dtype=F32) / D)
    ang = jnp.arange(t).astype(F32)[:, None] * inv[None, :]
    cos = jnp.cos(ang)
    sin = jnp.sin(ang)
    cos_t = jnp.concatenate([cos, cos], axis=-1)
    sin_t = jnp.concatenate([-sin, sin], axis=-1)
    log_gamma = jnp.log1p(-jnp.exp2(-5.0 - jnp.arange(RET_HEADS, dtype=F32)))
    idx = jnp.arange(L, dtype=F32)
    causal = idx[:, None] >= idx[None, :]
    rel = jnp.where(causal, idx[:, None] - idx[None, :], 0.0)
    intra = jnp.where(causal, jnp.exp(log_gamma[:, None, None] * rel), 0.0)
    inter = jnp.exp(log_gamma[:, None] * (idx + 1.0))[:, :, None]
    sdec = jnp.exp(log_gamma[:, None] * (L - 1.0 - idx))[:, :, None]
    cdec = jnp.exp(log_gamma * L)[:, None, None]
    return cos_t, sin_t, intra, inter, sdec, cdec


RWP_TM = 1024
RWP_SUB = 256


def _seg_sum(x, bd):
    hi = x.astype(BF16)
    lo = (x - hi.astype(F32)).astype(BF16)
    return jnp.dot(hi, bd, preferred_element_type=F32) + jnp.dot(lo, bd, preferred_element_type=F32)


def _rwkv_prep(x, in_g, w_in, mu, w_up, w0, a_up, a0, g_up, k_k, k_a, r_k, bd):
    bsz, t, d = x.shape
    tm, hm, W = RWP_TM, RWP_SUB, RW_W

    def body(x_ref, ing_ref, win_ref, mu_ref, wup_ref, w0_ref, aup_ref, a0_ref, gup_ref, kk_ref, ka_ref, rk_ref,
             bd_ref, r_out, lw_out, k_out, v_out, a_out, b_out, g_out, bonus_out, last_ref):
        c = pl.program_id(1)

        @pl.when(c == 0)
        def _():
            last_ref[...] = jnp.zeros_like(last_ref)

        def project(i):
            xs = x_ref[0, i * hm:(i + 1) * hm, :]
            return jnp.dot(_rms_rows(xs, ing_ref[...]).astype(BF16), win_ref[...], preferred_element_type=F32)

        rowi = lax.broadcasted_iota(jnp.int32, (hm, 1), 0)
        prev = last_ref[...]
        nxt = project(0)
        for i in range(tm // hm):
            rows = slice(i * hm, (i + 1) * hm)
            cur = nxt
            if i + 1 < tm // hm:
                nxt = project(i + 1)
            sh = jnp.where(rowi == 0, prev, pltpu.roll(cur, 1, 0))
            prev = cur[hm - 1:hm, :]
            xm = cur + (sh - cur) * mu_ref[...]
            x_r = xm[:, 0:W]
            x_k = xm[:, W:2 * W]
            x_v = xm[:, 2 * W:3 * W]
            x_dl = xm[:, 3 * W:3 * W + 64]
            x_al = xm[:, 3 * W + 64:3 * W + 128]
            x_gl = xm[:, 3 * W + 128:3 * W + 256]
            wl = w0_ref[...] + _dot(jnp.tanh(x_dl), wup_ref[...])
            sp = jnp.maximum(-wl, 0.0) + jnp.log(1.0 + jnp.exp(-jnp.abs(wl)))
            lw_out[0, rows, :] = -jnp.exp(-sp - 0.5)
            a = _sigmoid(a0_ref[...] + _dot(x_al, aup_ref[...]))
            g_out[0, rows, :] = _dot(_sigmoid(x_gl), gup_ref[...]).astype(BF16)
            kk0 = x_k * kk_ref[...]
            nrm = jnp.sqrt(_seg_sum(kk0 * kk0, bd_ref[...]))
            kk = kk0 / jnp.maximum(nrm, 1e-12)
            k_h = x_k * (1.0 + (a - 1.0) * ka_ref[...])
            r_out[0, rows, :] = x_r.astype(BF16)
            k_out[0, rows, :] = k_h.astype(BF16)
            v_out[0, rows, :] = x_v.astype(BF16)
            a_out[0, rows, :] = (-kk).astype(BF16)
            b_out[0, rows, :] = (kk * a).astype(BF16)
            bonus_out[0, rows, :] = (_dot(x_r * k_h * rk_ref[...], bd_ref[...]) * x_v).astype(BF16)
        last_ref[...] = prev

    row = lambda b, c: (0, 0)
    blk = pl.BlockSpec((1, tm, W), lambda b, c: (b, c, 0))
    shp = [jax.ShapeDtypeStruct((bsz, t, W), F32 if i == 1 else BF16) for i in range(8)]
    return pl.pallas_call(
        body,
        grid=(bsz, t // tm),
        in_specs=[pl.BlockSpec((1, tm, d), lambda b, c: (b, c, 0)),
                  pl.BlockSpec((1, d), row),
                  pl.BlockSpec((d, RW_COLS), row),
                  pl.BlockSpec((1, RW_COLS), row),
                  pl.BlockSpec((64, W), row), pl.BlockSpec((1, W), row),
                  pl.BlockSpec((64, W), row), pl.BlockSpec((1, W), row),
                  pl.BlockSpec((128, W), row),
                  pl.BlockSpec((1, W), row), pl.BlockSpec((1, W), row), pl.BlockSpec((1, W), row),
                  pl.BlockSpec((W, W), row)],
        out_specs=[blk] * 8,
        out_shape=shp,
        scratch_shapes=[pltpu.VMEM((1, RW_COLS), F32)],
        compiler_params=_cp("parallel", "arbitrary"),
        name="rwkv_prep",
    )(x, in_g, w_in, mu, w_up, w0, a_up, a0, g_up, k_k, k_a, r_k, bd)


RW_L = 64
RW_TB = 256


def _rwkv_scan(r, lw, k, v, aa, bb, g_out, bonus, ln_g, ln_b):
    bsz, t, W = r.shape
    L, N, tb = RW_L, RW_DIM, RW_TB
    nck = tb // L

    def body(r_ref, lw_ref, k_ref, v_ref, a_ref, b_ref, g_ref, bonus_ref, lng_ref, lnb_ref, out_ref, h_ref):
        c = pl.program_id(1)

        @pl.when(c == 0)
        def _():
            h_ref[...] = jnp.zeros_like(h_ref)

        lw_all = lw_ref[0]
        rowi = lax.broadcasted_iota(jnp.int32, (tb, 1), 0) & (L - 1)
        cl = lw_all
        for s in (1, 2, 4, 8, 16, 32):
            cl = cl + jnp.where(rowi >= s, pltpu.roll(cl, s, 0), 0.0)
        cl_last = jnp.concatenate([jnp.broadcast_to(cl[(cc + 1) * L - 1:(cc + 1) * L, :], (L, W))
                                   for cc in range(nck)], axis=0)
        e_inv = jnp.exp(-cl)
        e_end = jnp.exp(cl_last - cl)
        p_end = jnp.exp(cl_last)
        at = a_ref[0] * jnp.exp(cl - lw_all)
        rt = r_ref[0] * jnp.exp(cl)
        bt = b_ref[0] * e_inv
        kt = k_ref[0] * e_inv
        b_end = b_ref[0] * e_end
        k_end = k_ref[0] * e_end
        v_all = v_ref[0]
        P2 = 2 * N
        pairs = [(cc, p) for cc in range(nck) for p in range(RW_HEADS // 2)]
        pr = range(len(pairs))

        def pb(z, pi):
            cc, p = pairs[pi]
            return z[cc * L:(cc + 1) * L, p * P2:(p + 1) * P2]

        def stack2(z):
            lane = lax.broadcasted_iota(jnp.int32, (1, z.shape[1]), 1) & (P2 - 1)
            return jnp.concatenate([jnp.where(lane < N, z, 0.0), jnp.where(lane >= N, z, 0.0)], axis=0)

        tcol = lax.broadcasted_iota(jnp.int32, (L, P2), 1) & (N - 1)
        trow = lax.broadcasted_iota(jnp.int32, (L, P2), 0)
        strict = trow > tcol
        lower = trow >= tcol
        r2 = lax.broadcasted_iota(jnp.int32, (P2, P2), 0)
        c2 = lax.broadcasted_iota(jnp.int32, (P2, P2), 1)
        same_head = (r2 // N) == (c2 // N)
        eye2 = r2 == c2
        x2 = [jnp.concatenate([pb(at, pi), pb(rt, pi)], axis=0) for pi in pr]
        m_b = [_dot_nt(x2[pi], stack2(pb(bt, pi))) for pi in pr]
        m_k = [_dot_nt(x2[pi], stack2(pb(kt, pi))) for pi in pr]
        ap = [jnp.where(strict, m_b[pi][0:L], 0.0) for pi in pr]
        a_rb = [jnp.where(lower, m_b[pi][L:2 * L], 0.0) for pi in pr]
        a_ak = [jnp.where(strict, m_k[pi][0:L], 0.0) for pi in pr]
        a_rk = [jnp.where(lower, m_k[pi][L:2 * L], 0.0) for pi in pr]
        vp = [_dot(jnp.concatenate([a_ak[pi], a_rk[pi]], axis=0), stack2(pb(v_all, pi))) for pi in pr]
        kv = [_dot(pb(k_end, pi).T, pb(v_all, pi)) for pi in pr]
        x = [jnp.concatenate([pb(at, pi), vp[pi][0:L]], axis=1) for pi in pr]
        for it in range(6):
            x = [x[pi] + _dot(ap[pi], stack2(x[pi])) for pi in pr]
            if it < 5:
                ap = [_dot(ap[pi], stack2(ap[pi])) for pi in pr]
        post1 = [_dot(a_rb[pi], stack2(x[pi])) for pi in pr]
        post2 = [_dot(pb(b_end, pi).T, x[pi]) for pi in pr]
        lhs = []
        y0s = []
        h_adds = []
        for pi in pr:
            q_hat = pb(rt, pi) + post1[pi][:, 0:P2]
            gmat = (jnp.where(same_head, post2[pi][:, 0:P2], 0.0)
                    + jnp.where(eye2, pb(p_end, pi)[0:1, :], 0.0))
            lhs.append(jnp.concatenate([q_hat, gmat], axis=0))
            y0s.append(post1[pi][:, P2:2 * P2] + vp[pi][L:2 * L])
            h_adds.append(jnp.where(same_head, post2[pi][:, P2:2 * P2] + kv[pi], 0.0))
        npair = RW_HEADS // 2
        h_st = [h_ref[p] for p in range(npair)]
        ys = [None] * len(pairs)
        for cc in range(nck):
            res = [_dot(lhs[cc * npair + p], h_st[p]) for p in range(npair)]
            for p in range(npair):
                pi = cc * npair + p
                ys[pi] = res[p][0:L] + y0s[pi]
                h_st[p] = res[p][L:L + P2] + h_adds[pi]
        for p in range(npair):
            h_ref[p] = h_st[p]
        seg = same_head.astype(BF16)
        mean = [_seg_sum(ys[pi], seg) * (1.0 / N) for pi in pr]
        yc = [ys[pi] - mean[pi] for pi in pr]
        var = [_seg_sum(yc[pi] * yc[pi], seg) * (1.0 / N) for pi in pr]
        for pi, (cc, p) in enumerate(pairs):
            yn = yc[pi] * lax.rsqrt(var[pi] + RW_LN_EPS)
            rows = slice(cc * L, (cc + 1) * L)
            cols = slice(p * P2, (p + 1) * P2)
            out_ref[0, rows, cols] = ((yn * lng_ref[:, cols] + lnb_ref[:, cols] + bonus_ref[0, rows, cols])
                                      * g_ref[0, rows, cols]).astype(BF16)

    blk = pl.BlockSpec((1, tb, W), lambda b, c: (b, c, 0))
    vec = pl.BlockSpec((1, W), lambda b, c: (0, 0))
    return pl.pallas_call(
        body,
        grid=(bsz, t // tb),
        in_specs=[blk] * 8 + [vec, vec],
        out_specs=blk,
        out_shape=jax.ShapeDtypeStruct((bsz, t, W), BF16),
        scratch_shapes=[pltpu.VMEM((RW_HEADS // 2, 2 * N, 2 * N), F32)],
        compiler_params=_cp("parallel", "arbitrary"),
        name="rwkv_scan",
    )(r, lw, k, v, aa, bb, g_out, bonus, ln_g, ln_b)


def _mix_out_route(a, b, w_bf16, resid, g, router_split, tm=1024):
    n, wa = a.shape
    wb = b.shape[1]
    d = w_bf16.shape[1]
    e = router_split.shape[1] // 2

    def body(a_ref, b_ref, w_ref, r_ref, g_ref, rt_ref, h_ref, xn_ref, lg_ref):
        acc = jnp.dot(a_ref[...], w_ref[0:wa, :], preferred_element_type=F32)
        acc = acc + jnp.dot(b_ref[...], w_ref[wa:wa + wb, :], preferred_element_type=F32)
        h = r_ref[...] + acc
        h_ref[...] = h
        xn = _rms_rows(h, g_ref[...])
        x_hi = xn.astype(BF16)
        xn_ref[...] = x_hi
        x_lo = (xn - x_hi.astype(F32)).astype(BF16)
        hi_both = jnp.dot(x_hi, rt_ref[...], preferred_element_type=F32)
        lo_hi = jnp.dot(x_lo, rt_ref[:, 0:e], preferred_element_type=F32)
        lg_ref[...] = hi_both[:, 0:e] + (lo_hi + hi_both[:, e:2 * e])

    return pl.pallas_call(
        body,
        grid=(n // tm,),
        in_specs=[pl.BlockSpec((tm, wa), lambda i: (i, 0)),
                  pl.BlockSpec((tm, wb), lambda i: (i, 0)),
                  pl.BlockSpec((wa + wb, d), lambda i: (0, 0)),
                  pl.BlockSpec((tm, d), lambda i: (i, 0)),
                  pl.BlockSpec((1, d), lambda i: (0, 0)),
                  pl.BlockSpec((d, 2 * e), lambda i: (0, 0))],
        out_specs=[pl.BlockSpec((tm, d), lambda i: (i, 0)), pl.BlockSpec((tm, d), lambda i: (i, 0)),
                   pl.BlockSpec((tm, e), lambda i: (i, 0))],
        out_shape=[jax.ShapeDtypeStruct((n, d), F32), jax.ShapeDtypeStruct((n, d), BF16),
                   jax.ShapeDtypeStruct((n, e), F32)],
        compiler_params=_cp("parallel"),
        name="mix_out_route",
    )(a, b, w_bf16, resid, g.reshape(1, d), router_split)


MOE_TM = 512
MOE_TF = 1792


def _experts(xs, row_w, item_tile, item_exp, item_lo, item_hi, wg, wu, wd):
    nrows, d = xs.shape
    tm, tf = MOE_TM, MOE_TF
    nf = D_FF // tf
    n_items = item_tile.shape[0]

    def body(it_ref, ie_ref, lo_ref, hi_ref, x_ref, w_ref, wg_ref, wu_ref, wd_ref, o_ref, acc_ref):
        i = pl.program_id(0)
        j = pl.program_id(1)
        tile = it_ref[i]
        first = jnp.logical_or(i == 0, tile != it_ref[jnp.maximum(i - 1, 0)])
        last = jnp.logical_or(i == n_items - 1, tile != it_ref[jnp.minimum(i + 1, n_items - 1)])

        @pl.when(jnp.logical_and(first, j == 0))
        def _():
            acc_ref[...] = jnp.zeros_like(acc_ref)

        lo = lo_ref[i]
        hi = hi_ref[i]

        @pl.when(lo < hi)
        def _():
            x = x_ref[...]
            gg = jnp.dot(x, wg_ref[0], preferred_element_type=F32)
            uu = jnp.dot(x, wu_ref[0], preferred_element_type=F32)
            act = (_silu(gg) * uu).astype(BF16)
            part = jnp.dot(act, wd_ref[0], preferred_element_type=F32)
            rowi = lax.broadcasted_iota(jnp.int32, (tm, 1), 0)
            mine = jnp.logical_and(rowi >= lo, rowi < hi)
            acc_ref[...] += part * jnp.where(mine, w_ref[...], 0.0)

        @pl.when(jnp.logical_and(last, j == nf - 1))
        def _():
            o_ref[...] = acc_ref[...].astype(o_ref.dtype)

    grid_spec = pltpu.PrefetchScalarGridSpec(
        num_scalar_prefetch=4,
        grid=(n_items, nf),
        in_specs=[pl.BlockSpec((tm, d), lambda i, j, it, ie, lo, hi: (it[i], 0)),
                  pl.BlockSpec((tm, 1), lambda i, j, it, ie, lo, hi: (it[i], 0)),
                  pl.BlockSpec((1, d, tf), lambda i, j, it, ie, lo, hi: (ie[i], 0, j)),
                  pl.BlockSpec((1, d, tf), lambda i, j, it, ie, lo, hi: (ie[i], 0, j)),
                  pl.BlockSpec((1, tf, d), lambda i, j, it, ie, lo, hi: (ie[i], j, 0))],
        out_specs=pl.BlockSpec((tm, d), lambda i, j, it, ie, lo, hi: (it[i], 0)),
        scratch_shapes=[pltpu.VMEM((tm, d), F32)],
    )
    return pl.pallas_call(
        body,
        grid_spec=grid_spec,
        out_shape=jax.ShapeDtypeStruct((nrows, d), BF16),
        compiler_params=_cp("arbitrary", "arbitrary"),
        name="moe_experts",
    )(item_tile, item_exp, item_lo, item_hi, xs, row_w, wg, wu, wd)


def _combine_norm(h, y0, y1, g, tm=512):
    n, d = h.shape

    def body(h_ref, a_ref, b_ref, g_ref, o_ref):
        o_ref[...] = _rms_rows(h_ref[...] + (a_ref[...].astype(F32) + b_ref[...].astype(F32)), g_ref[...])

    blk = pl.BlockSpec((tm, d), lambda i: (i, 0))
    return pl.pallas_call(
        body,
        grid=(n // tm,),
        in_specs=[blk, blk, blk, pl.BlockSpec((1, d), lambda i: (0, 0))],
        out_specs=blk,
        out_shape=jax.ShapeDtypeStruct((n, d), F32),
        compiler_params=_cp("parallel"),
        name="combine_norm",
    )(h, y0, y1, g.reshape(1, d))


def _route(logits, n):
    tm = MOE_TM
    na = n * TOP_K
    n_tiles = na // tm
    top_val, top_idx = lax.top_k(logits, TOP_K)
    top_w = jax.nn.softmax(top_val, axis=-1)
    e_flat = top_idx.reshape(-1).astype(jnp.int32)
    w_flat = top_w.reshape(-1)
    ids = jnp.arange(na, dtype=jnp.int32)
    e_sorted, sorted_a, sorted_w = lax.sort((e_flat, ids, w_flat), num_keys=1, is_stable=True)
    sorted_tok = sorted_a // TOP_K
    _, pos = lax.sort((sorted_a, ids), num_keys=1)
    ends = jnp.sum((e_sorted[None, :] <= jnp.arange(N_EXPERTS, dtype=jnp.int32)[:, None]).astype(jnp.int32), axis=1)
    cuts = jnp.sort(jnp.concatenate([jnp.arange(n_tiles, dtype=jnp.int32) * tm, ends[:-1].astype(jnp.int32)]))
    nxt = jnp.concatenate([cuts[1:], jnp.full((1,), na, jnp.int32)])
    item_tile = jnp.minimum(cuts // tm, n_tiles - 1)
    item_exp = jnp.minimum(jnp.sum((ends[None, :] <= cuts[:, None]).astype(jnp.int32), axis=1), N_EXPERTS - 1)
    item_lo = cuts - item_tile * tm
    item_hi = nxt - item_tile * tm
    return sorted_tok, sorted_w, (item_tile, item_exp, item_lo, item_hi), pos.reshape(n, TOP_K)


def kernel(x, e_norm1_g, e_w_in, e_ml_conv_w, e_ml_conv_b, e_ml_gate_b, e_ml_norm_g, e_gla_gate_up, e_gla_gate_b,
           e_gla_norm_g, e_w_out, e_norm2_g, e_ffn_w_gate, e_ffn_w_up, e_ffn_w_down, o_norm1_g, o_w_in,
           o_ret_norm_g, o_rw_mu, o_rw_w_up, o_rw_w0, o_rw_a_up, o_rw_a0, o_rw_g_up, o_rw_k_k, o_rw_k_a, o_rw_r_k,
           o_rw_ln_g, o_rw_ln_b, o_w_out, o_norm2_g, o_moe_router, o_moe_w_gate, o_moe_w_up, o_moe_w_down,
           final_norm_g):
    bsz, t, d = x.shape
    n = bsz * t
    h0 = x.reshape(n, d)

    w = e_w_in[0]
    row = lambda a: a.reshape(1, -1)
    x3 = x
    ng1 = row(e_norm1_g[0])
    w_if = w[:, 2048:2056]
    w_gate = jnp.zeros((d, 128), F32).at[:, :2 * ML_HEADS].set(w_if).astype(BF16)
    w_gate_t = jnp.zeros((16, d), F32).at[:2 * ML_HEADS, :].set(w_if.T).astype(BF16)
    h_ml = _mlstm(x3, ng1, w[:, :4 * ML_W].astype(BF16), w_gate, w_gate_t, row(e_ml_gate_b[0]),
                  e_ml_gate_b[0].reshape(-1, 1), e_ml_conv_w[0], row(e_ml_conv_b[0]), row(e_ml_norm_g[0]))
    gq, gk, gv, gr = 2056, 2312, 2568, 3080
    w_pairs = jnp.stack([jnp.concatenate([w[:, gq + 128 * hp:gq + 128 * (hp + 1)],
                                          w[:, gk + 128 * hp:gk + 128 * (hp + 1)],
                                          w[:, gv + 256 * hp:gv + 256 * (hp + 1)],
                                          w[:, gr + 256 * hp:gr + 256 * (hp + 1)]], axis=1)
                         for hp in range(GLA_HEADS // 2)]).astype(BF16)
    w_low = jnp.zeros((d, 128), F32).at[:, :GLA_RANK].set(w[:, 3592:3608]).astype(BF16)
    o_gla = _gla(x3, ng1, w_pairs, w_low, e_gla_gate_up[0].astype(BF16), row(e_gla_gate_b[0]),
                 row(e_gla_norm_g[0]))
    h2 = _mix_out_ffn(h_ml.reshape(n, -1), o_gla.reshape(n, -1), e_w_out[0].astype(BF16), h0, e_norm2_g[0],
                      e_ffn_w_gate[0].astype(BF16), e_ffn_w_up[0].astype(BF16), e_ffn_w_down[0].astype(BF16))

    w = o_w_in[0]
    h2_3 = h2.reshape(bsz, t, d)
    ng2 = row(o_norm1_g[0])
    y_ret = _retention(h2_3, ng2, w[:, :4 * RET_W].astype(BF16), *_retention_tables(t), row(o_ret_norm_g[0]))
    head_of = jnp.arange(RW_W) // RW_DIM
    bd = (head_of[:, None] == head_of[None, :]).astype(BF16)
    r, lw, k, v, aa, bb, g_out, bonus = _rwkv_prep(
        h2_3, ng2, w[:, 4 * RET_W:].astype(BF16), row(o_rw_mu[0]), o_rw_w_up[0].astype(BF16), row(o_rw_w0[0]),
        o_rw_a_up[0].astype(BF16), row(o_rw_a0[0]), o_rw_g_up[0].astype(BF16), row(o_rw_k_k[0]),
        row(o_rw_k_a[0]), row(o_rw_r_k[0]), bd)
    y_rw = _rwkv_scan(r, lw, k, v, aa, bb, g_out, bonus, row(o_rw_ln_g[0]), row(o_rw_ln_b[0]))
    router_pad = jnp.zeros((d, 128), F32).at[:, :N_EXPERTS].set(o_moe_router[0])
    router_hi = router_pad.astype(BF16)
    router_split = jnp.concatenate([router_hi, (router_pad - router_hi.astype(F32)).astype(BF16)], axis=1)
    h3, xn, logits = _mix_out_route(y_ret.reshape(n, -1), y_rw.reshape(n, -1), o_w_out[0].astype(BF16), h2,
                                    o_norm2_g[0], router_split)
    sorted_tok, sorted_w, items, pos = _route(logits[:, :N_EXPERTS], n)
    xs = xn.at[sorted_tok].get(mode="promise_in_bounds")
    ys = _experts(xs, sorted_w.reshape(-1, 1), *items, o_moe_w_gate[0].astype(BF16),
                  o_moe_w_up[0].astype(BF16), o_moe_w_down[0].astype(BF16))
    y0 = ys.at[pos[:, 0]].get(mode="promise_in_bounds")
    y1 = ys.at[pos[:, 1]].get(mode="promise_in_bounds")
    out = _combine_norm(h3, y0, y1, final_norm_g)
    return out.reshape(bsz, t, d)
```

```python
import functools

import numpy as np
import jax
import jax.numpy as jnp
from jax import lax
from jax.experimental import pallas as pl
from jax.experimental.pallas import tpu as pltpu

F32 = jnp.float32
BF16 = jnp.bfloat16

D_MODEL = 1024
EPS = 1e-6
ML_HEADS, ML_DIM, ML_W, ML_CONV = 4, 128, 512, 4
GLA_HEADS, GLA_DK, GLA_DV, GLA_RANK, GLA_TAU = 4, 64, 128, 16, 16.0
RET_HEADS, RET_DIM, RET_W = 4, 128, 512
ROPE_BASE = 10000.0
RW_HEADS, RW_DIM, RW_W = 8, 64, 512
RW_COLS = 1792
RW_LN_EPS = 64e-5
D_FF = 3584
N_EXPERTS = 8
TOP_K = 2

VMEM_LIMIT = 48 * 1024 * 1024
NEG = -1e30


def _cp(*sem):
    return pltpu.CompilerParams(dimension_semantics=sem, vmem_limit_bytes=VMEM_LIMIT)


def _sigmoid(x):
    return 1.0 / (1.0 + jnp.exp(-x))


def _silu(x):
    return x * _sigmoid(x)


def _log_sigmoid(x):
    return jnp.minimum(x, 0.0) - jnp.log(1.0 + jnp.exp(-jnp.abs(x)))


def _dot(a, b):
    return jnp.dot(a.astype(BF16), b.astype(BF16), preferred_element_type=F32)


def _dot_nt(a, b):
    return lax.dot_general(a.astype(BF16), b.astype(BF16), (((1,), (1,)), ((), ())), preferred_element_type=F32)


def _dot_tn(a, b):
    return jnp.dot(a.T.astype(BF16), b.astype(BF16), preferred_element_type=F32)


def _rms_rows(x, g):
    ms = jnp.mean(x * x, axis=-1, keepdims=True)
    return x * lax.rsqrt(ms + EPS) * g


def _mix_out_ffn(a, b, w_out, resid, g, wg, wu, wd, tm=512, tf=1792):
    n, d = resid.shape
    wa = a.shape[1]
    wb = b.shape[1]
    f = wg.shape[1]
    nf = f // tf

    def body(a_ref, b_ref, wo_ref, r_ref, g_ref, wg_ref, wu_ref, wd_ref, o_ref, xn_ref, acc_ref):
        j = pl.program_id(1)

        @pl.when(j == 0)
        def _():
            h = r_ref[...] + jnp.dot(a_ref[...], wo_ref[0:wa, :], preferred_element_type=F32)
            h = h + jnp.dot(b_ref[...], wo_ref[wa:wa + wb, :], preferred_element_type=F32)
            xn_ref[...] = _rms_rows(h, g_ref[...]).astype(BF16)
            acc_ref[...] = h

        xn = xn_ref[...]
        gg = jnp.dot(xn, wg_ref[...], preferred_element_type=F32)
        uu = jnp.dot(xn, wu_ref[...], preferred_element_type=F32)
        act = (_silu(gg) * uu).astype(BF16)
        acc_ref[...] += jnp.dot(act, wd_ref[...], preferred_element_type=F32)

        @pl.when(j == nf - 1)
        def _():
            o_ref[...] = acc_ref[...]

    return pl.pallas_call(
        body,
        grid=(n // tm, nf),
        in_specs=[pl.BlockSpec((tm, wa), lambda i, j: (i, 0)),
                  pl.BlockSpec((tm, wb), lambda i, j: (i, 0)),
                  pl.BlockSpec((wa + wb, d), lambda i, j: (0, 0)),
                  pl.BlockSpec((tm, d), lambda i, j: (i, 0)),
                  pl.BlockSpec((1, d), lambda i, j: (0, 0)),
                  pl.BlockSpec((d, tf), lambda i, j: (0, j)),
                  pl.BlockSpec((d, tf), lambda i, j: (0, j)),
                  pl.BlockSpec((tf, d), lambda i, j: (j, 0))],
        out_specs=pl.BlockSpec((tm, d), lambda i, j: (i, 0)),
        out_shape=jax.ShapeDtypeStruct((n, d), F32),
        scratch_shapes=[pltpu.VMEM((tm, d), BF16), pltpu.VMEM((tm, d), F32)],
        compiler_params=_cp("parallel", "arbitrary"),
        name="mix_out_ffn",
    )(a, b, w_out, resid, g.reshape(1, d), wg, wu, wd)


ML_TB = 1024
ML_L = 256


def _mlstm(x, in_g, w_in, w_gate, w_gate_t, gate_b_row, gate_b_col, conv_w, conv_b, norm_g):
    bsz, t, d = x.shape
    TB, L, D, H, W = ML_TB, ML_L, ML_DIM, ML_HEADS, ML_W

    def body(x_ref, ing_ref, w_ref, wg_ref, wgt_ref, gbr_ref, gbc_ref, cwq_ref, cwk_ref, cbq_ref, cbk_ref,
             ng_ref, out_ref, qext, kext, c_ref, n_ref, m_ref):
        c = pl.program_id(1)

        @pl.when(c == 0)
        def _():
            qext[0:8, :] = jnp.zeros((8, W), F32)
            kext[0:8, :] = jnp.zeros((8, W), F32)
            c_ref[...] = jnp.zeros_like(c_ref)
            n_ref[...] = jnp.zeros_like(n_ref)
            m_ref[...] = jnp.zeros_like(m_ref)

        def project(cc):
            xn = _rms_rows(x_ref[0, cc * L:(cc + 1) * L, :], ing_ref[...]).astype(BF16)
            pc = jnp.dot(xn, w_ref[...], preferred_element_type=F32)
            gc = jnp.dot(xn, wg_ref[...], preferred_element_type=F32)[:, 0:2 * H]
            gr = lax.dot_general(wgt_ref[...], xn, (((1,), (1,)), ((), ())),
                                 preferred_element_type=F32)[0:2 * H, :]
            return pc, gc, gr

        def conv(ext, base, cw_ref, cb_ref):
            acc = cb_ref[...] + cw_ref[0:1, :] * ext[pl.ds(base + 8 - ML_CONV + 1, L), :]
            for kk in range(1, ML_CONV):
                acc = acc + cw_ref[kk:kk + 1, :] * ext[pl.ds(base + 8 - ML_CONV + 1 + kk, L), :]
            return _silu(acc)

        ri = lax.broadcasted_iota(jnp.int32, (L, L), 0)
        ci = lax.broadcasted_iota(jnp.int32, (L, L), 1)
        causal = ri >= ci
        heads = range(H)
        c_st = [c_ref[h] for h in heads]
        n_st = [n_ref[h] for h in heads]
        m_st = [m_ref[h] for h in heads]
        nxt = project(0)
        for cc in range(TB // L):
            rows = slice(cc * L, (cc + 1) * L)
            p, g_cols, g_rows = nxt
            if cc + 1 < TB // L:
                nxt = project(cc + 1)
            qext[8 + cc * L:8 + (cc + 1) * L, :] = p[:, 0:W]
            kext[8 + cc * L:8 + (cc + 1) * L, :] = p[:, W:2 * W]
            q_all = conv(qext, cc * L, cwq_ref, cbq_ref) * (D ** -0.5)
            k_all = conv(kext, cc * L, cwk_ref, cbk_ref)
            gcol = g_cols + gbr_ref[...]
            grow = g_rows + gbc_ref[...]
            fcol = _log_sigmoid(gcol[:, H:2 * H])
            frow = _log_sigmoid(grow[H:2 * H, :])
            hs = lambda z, h: z[:, h * D:(h + 1) * D]
            w_intra, w_inter, w_state, carry, m_row = [], [], [], [], []
            for h in heads:
                f_row = frow[h:h + 1, :]
                i_row = grow[h:h + 1, :]
                f_col = fcol[:, h:h + 1]
                i_col = gcol[:, h:h + 1]
                b_col = jnp.sum(jnp.where(causal, f_row, 0.0), axis=1, keepdims=True)
                b_row = jnp.sum(jnp.where(ri <= ci, f_col, 0.0), axis=0, keepdims=True)
                g_tot = jnp.sum(f_row, axis=1, keepdims=True)
                d_intra = jnp.where(causal, b_col - b_row + i_row, NEG)
                d_inter = b_col + m_st[h]
                mr = jnp.maximum(d_inter, jnp.max(d_intra, axis=1, keepdims=True))
                w_intra.append(jnp.exp(d_intra - mr))
                w_inter.append(jnp.exp(d_inter - mr))
                m_row.append(mr)
                d_state = g_tot - b_col + i_col
                mn = jnp.maximum(g_tot + m_st[h], jnp.max(d_state, axis=0, keepdims=True))
                w_state.append(jnp.exp(d_state - mn))
                carry.append(jnp.exp(g_tot + m_st[h] - mn))
                m_st[h] = mn
            qh = [hs(q_all, h) for h in heads]
            kh = [hs(k_all, h) for h in heads]
            vh = [p[:, 2 * W + h * D:2 * W + (h + 1) * D] for h in heads]
            s = [_dot_nt(qh[h], kh[h]) * w_intra[h] for h in heads]
            qc = [_dot(qh[h], c_st[h]) for h in heads]
            kw = [kh[h] * w_state[h] for h in heads]
            kv = [_dot_tn(kw[h], vh[h]) for h in heads]
            sv = [_dot(s[h], vh[h]) for h in heads]
            ones = jnp.ones((L, 128), BF16)
            ssum = [_dot(s[h], ones)[:, 0:1] for h in heads]
            for h in heads:
                num = sv[h] + w_inter[h] * qc[h]
                den = ssum[h] + w_inter[h] * jnp.sum(qh[h] * n_st[h], axis=1, keepdims=True)
                hval = num / jnp.maximum(jnp.abs(den), jnp.exp(-m_row[h]))
                c_st[h] = carry[h] * c_st[h] + kv[h]
                n_st[h] = carry[h] * n_st[h] + jnp.sum(kw[h], axis=0, keepdims=True)
                hg = _sigmoid(p[:, 3 * W + h * D:3 * W + (h + 1) * D]) * hval
                hc = hg - jnp.mean(hg, axis=1, keepdims=True)
                hn = hc * lax.rsqrt(jnp.mean(hc * hc, axis=1, keepdims=True) + EPS)
                out_ref[0, rows, h * D:(h + 1) * D] = (hn * ng_ref[:, h * D:(h + 1) * D]).astype(BF16)
        for h in heads:
            c_ref[h] = c_st[h]
            n_ref[h] = n_st[h]
            m_ref[h] = m_st[h]
        qext[0:8, :] = qext[TB:TB + 8, :]
        kext[0:8, :] = kext[TB:TB + 8, :]

    fix = lambda j: (lambda b, c: (0, j))
    return pl.pallas_call(
        body,
        grid=(bsz, t // TB),
        in_specs=[pl.BlockSpec((1, TB, d), lambda b, c: (b, c, 0)),
                  pl.BlockSpec((1, d), fix(0)),
                  pl.BlockSpec((d, 4 * W), fix(0)),
                  pl.BlockSpec((d, 128), fix(0)),
                  pl.BlockSpec((16, d), fix(0)),
                  pl.BlockSpec((1, 2 * H), fix(0)),
                  pl.BlockSpec((2 * H, 1), fix(0)),
                  pl.BlockSpec((ML_CONV, W), fix(0)),
                  pl.BlockSpec((ML_CONV, W), fix(1)),
                  pl.BlockSpec((1, W), fix(0)),
                  pl.BlockSpec((1, W), fix(1)),
                  pl.BlockSpec((1, W), fix(0))],
        out_specs=pl.BlockSpec((1, TB, W), lambda b, c: (b, c, 0)),
        out_shape=jax.ShapeDtypeStruct((bsz, t, W), BF16),
        scratch_shapes=[pltpu.VMEM((TB + 8, W), F32), pltpu.VMEM((TB + 8, W), F32),
                        pltpu.VMEM((H, D, D), F32), pltpu.VMEM((H, 1, D), F32), pltpu.VMEM((H, 1, 1), F32)],
        compiler_params=_cp("parallel", "arbitrary"),
        name="mlstm",
    )(x, in_g, w_in, w_gate, w_gate_t, gate_b_row, gate_b_col, conv_w, conv_w, conv_b, conv_b, norm_g)


GLA_TC = 1024
GLA_SUB = 16
GLA_GROUP = 128


def _gla(x, in_g, w_pairs, w_low, gate_up, gate_b, norm_g):
    bsz, t, d = x.shape
    tc, S, GB = GLA_TC, GLA_SUB, GLA_GROUP
    head_ones = (jnp.arange(2 * GLA_DK)[:, None] // GLA_DK == jnp.arange(2 * GB)[None, :] // GB).astype(BF16)
    nsub = tc // S
    dk, dv = GLA_DK, GLA_DV

    def body(x_ref, ing_ref, w_ref, wl_ref, gu_ref, gbias_ref, ng_ref, ones_ref, out_ref, st_ref, ksh, bsh):
        c = pl.program_id(2)

        @pl.when(c == 0)
        def _():
            st_ref[...] = jnp.zeros_like(st_ref)
            ksh[0:S, :] = jnp.zeros((S, 2 * dk), F32)
            bsh[0:S, :] = jnp.zeros((S, 2 * dk), F32)

        xn = _rms_rows(x_ref[0], ing_ref[...]).astype(BF16)
        p = jnp.dot(xn, w_ref[0], preferred_element_type=F32)
        g_low = jnp.dot(xn, wl_ref[...], preferred_element_type=F32)[:, 0:GLA_RANK]
        z = _dot(g_low, gu_ref[...]) + gbias_ref[...]
        la = _log_sigmoid(z) / GLA_TAU
        rowi = lax.broadcasted_iota(jnp.int32, (tc, 1), 0)
        rmod = rowi & (S - 1)
        bcum = la
        rsum = la
        for s in (1, 2, 4, 8):
            bcum = bcum + jnp.where(rmod >= s, pltpu.roll(bcum, s, 0), 0.0)
            rsum = rsum + jnp.where(rmod < S - s, pltpu.roll(rsum, tc - s, 0), 0.0)
        q = p[:, 0:2 * dk] * (dk ** -0.5)
        k = p[:, 2 * dk:4 * dk]
        v = p[:, 4 * dk:4 * dk + 2 * dv]
        gate = p[:, 4 * dk + 2 * dv:4 * dk + 4 * dv]
        qt = q * jnp.exp(bcum)
        kt = k * jnp.exp(rsum - la)
        eg = jnp.exp(bcum + rsum - la)

        ksh[S:, :] = k
        bsh[S:, :] = bcum
        prods = []
        for d in range(S):
            kd = k if d == 0 else ksh[pl.ds(S - d, tc), :]
            bd = bcum if d == 0 else bsh[pl.ds(S - d, tc), :]
            e = jnp.exp(jnp.where(rmod >= d, bcum - bd, 0.0))
            prods.append((q * kd * e).astype(BF16))
        ws = [jnp.dot(p, ones_ref[...], preferred_element_type=F32) for p in prods]
        coli = lax.broadcasted_iota(jnp.int32, (tc, GB), 1)
        rgrp = rowi & (GB - 1)
        att0 = jnp.zeros((tc, GB), F32)
        att1 = jnp.zeros((tc, GB), F32)
        offs = jnp.where((coli // S) == (rgrp // S), rgrp - coli, -1)
        for d in range(S):
            here = offs == d
            att0 = jnp.where(here, ws[d][:, 0:GB], att0)
            att1 = jnp.where(here, ws[d][:, GB:2 * GB], att1)

        heads = range(2)
        lk = [slice(hh * dk, (hh + 1) * dk) for hh in heads]
        lv = [slice(hh * dv, (hh + 1) * dv) for hh in heads]
        kv = [[_dot_tn(v[si * S:(si + 1) * S, lv[hh]], kt[si * S:(si + 1) * S, lk[hh]]) for hh in heads]
              for si in range(nsub)]
        st = [st_ref[hh] for hh in heads]
        inter = [[], []]
        for si in range(nsub):
            rows = slice(si * S, (si + 1) * S)
            for hh in heads:
                inter[hh].append(_dot_nt(qt[rows, lk[hh]], st[hh]))
                st[hh] = st[hh] * eg[si * S:si * S + 1, lk[hh]] + kv[si][hh]
        for hh in heads:
            st_ref[hh] = st[hh]

        for hh, att in ((0, att0), (1, att1)):
            diag = jnp.concatenate([_dot(att[g * GB:(g + 1) * GB], v[g * GB:(g + 1) * GB, lv[hh]])
                                    for g in range(tc // GB)], axis=0)
            o = diag + jnp.concatenate(inter[hh], axis=0)
            on = o * lax.rsqrt(jnp.mean(o * o, axis=1, keepdims=True) + EPS)
            out_ref[0, :, lv[hh]] = (on * ng_ref[:, lv[hh]] * _silu(gate[:, lv[hh]])).astype(BF16)

    pw = 4 * dk + 4 * dv
    return pl.pallas_call(
        body,
        grid=(bsz, GLA_HEADS // 2, t // tc),
        in_specs=[pl.BlockSpec((1, tc, d), lambda b, h, c: (b, c, 0)),
                  pl.BlockSpec((1, d), lambda b, h, c: (0, 0)),
                  pl.BlockSpec((1, d, pw), lambda b, h, c: (h, 0, 0)),
                  pl.BlockSpec((d, 128), lambda b, h, c: (0, 0)),
                  pl.BlockSpec((GLA_RANK, 2 * dk), lambda b, h, c: (0, h)),
                  pl.BlockSpec((1, 2 * dk), lambda b, h, c: (0, h)),
                  pl.BlockSpec((1, 2 * dv), lambda b, h, c: (0, h)),
                  pl.BlockSpec((2 * dk, 2 * GB), lambda b, h, c: (0, 0))],
        out_specs=pl.BlockSpec((1, tc, 2 * dv), lambda b, h, c: (b, c, h)),
        out_shape=jax.ShapeDtypeStruct((bsz, t, GLA_HEADS * dv), BF16),
        scratch_shapes=[pltpu.VMEM((2, dv, dk), F32), pltpu.VMEM((tc + S, 2 * dk), F32),
                        pltpu.VMEM((tc + S, 2 * dk), F32)],
        compiler_params=_cp("parallel", "parallel", "arbitrary"),
        name="gla",
    )(x, in_g, w_pairs, w_low, gate_up, gate_b, norm_g, head_ones)


RET_TB = 1024
RET_L = 256


def _retention(x, in_g, w_in, cos_t, sin_t, intra, inter, sdec, cdec, norm_g):
    bsz, t, d = x.shape
    TB, L, D, H, W = RET_TB, RET_L, RET_DIM, RET_HEADS, RET_W

    def body(x_ref, ing_ref, w_ref, cos_ref, sin_ref, intra_ref, inter_ref, sdec_ref, cdec_ref, ng_ref,
             out_ref, s_ref):
        c = pl.program_id(1)

        @pl.when(c == 0)
        def _():
            s_ref[...] = jnp.zeros_like(s_ref)

        p = jnp.dot(_rms_rows(x_ref[0], ing_ref[...]).astype(BF16), w_ref[...], preferred_element_type=F32)
        heads = range(H)
        s_st = [s_ref[h] for h in heads]
        for cc in range(TB // L):
            rows = slice(cc * L, (cc + 1) * L)
            cs = cos_ref[rows, :]
            sn = sin_ref[rows, :]

            def rot(z):
                return z * cs + pltpu.roll(z, D // 2, 1) * sn

            q = [rot(p[rows, h * D:(h + 1) * D]) * (D ** -0.5) for h in heads]
            k = [rot(p[rows, W + h * D:W + (h + 1) * D]) for h in heads]
            v = [p[rows, 2 * W + h * D:2 * W + (h + 1) * D] for h in heads]
            s = [_dot_nt(q[h], k[h]) * intra_ref[h] for h in heads]
            qs = [_dot(q[h], s_st[h]) for h in heads]
            kv = [_dot_tn(k[h] * sdec_ref[h], v[h]) for h in heads]
            sv = [_dot(s[h], v[h]) for h in heads]
            for h in heads:
                o = sv[h] + inter_ref[h] * qs[h]
                s_st[h] = cdec_ref[h] * s_st[h] + kv[h]
                oc = o - jnp.mean(o, axis=1, keepdims=True)
                on = oc * lax.rsqrt(jnp.mean(oc * oc, axis=1, keepdims=True) + EPS)
                hl = slice(h * D, (h + 1) * D)
                gate = p[rows, 3 * W + h * D:3 * W + (h + 1) * D]
                out_ref[0, rows, hl] = (on * ng_ref[:, hl] * _silu(gate)).astype(BF16)
        for h in heads:
            s_ref[h] = s_st[h]

    fix2 = lambda b, c: (0, 0)
    fix3 = lambda b, c: (0, 0, 0)
    return pl.pallas_call(
        body,
        grid=(bsz, t // TB),
        in_specs=[pl.BlockSpec((1, TB, d), lambda b, c: (b, c, 0)),
                  pl.BlockSpec((1, d), fix2),
                  pl.BlockSpec((d, 4 * W), fix2),
                  pl.BlockSpec((TB, D), lambda b, c: (c, 0)),
                  pl.BlockSpec((TB, D), lambda b, c: (c, 0)),
                  pl.BlockSpec((H, L, L), fix3),
                  pl.BlockSpec((H, L, 1), fix3),
                  pl.BlockSpec((H, L, 1), fix3),
                  pl.BlockSpec((H, 1, 1), fix3),
                  pl.BlockSpec((1, W), lambda b, c: (0, 0))],
        out_specs=pl.BlockSpec((1, TB, W), lambda b, c: (b, c, 0)),
        out_shape=jax.ShapeDtypeStruct((bsz, t, W), BF16),
        scratch_shapes=[pltpu.VMEM((H, D, D), F32)],
        compiler_params=_cp("parallel", "arbitrary"),
        name="retention",
    )(x, in_g, w_in, cos_t, sin_t, intra, inter, sdec, cdec, norm_g)


def _retention_tables(t):
    L, D = RET_L, RET_DIM
    inv = ROPE_BASE ** (-jnp.arange(0, D, 2, dtype=F32) / D)
    ang = jnp.arange(t).astype(F32)[:, None] * inv[None, :]
    cos = jnp.cos(ang)
    sin = jnp.sin(ang)
    cos_t = jnp.concatenate([cos, cos], axis=-1)
    sin_t = jnp.concatenate([-sin, sin], axis=-1)
    log_gamma = jnp.log1p(-jnp.exp2(-5.0 - jnp.arange(RET_HEADS, dtype=F32)))
    idx = jnp.arange(L, dtype=F32)
    causal = idx[:, None] >= idx[None, :]
    rel = jnp.where(causal, idx[:, None] - idx[None, :], 0.0)
    intra = jnp.where(causal, jnp.exp(log_gamma[:, None, None] * rel), 0.0)
    inter = jnp.exp(log_gamma[:, None] * (idx + 1.0))[:, :, None]
    sdec = jnp.exp(log_gamma[:, None] * (L - 1.0 - idx))[:, :, None]
    cdec = jnp.exp(log_gamma * L)[:, None, None]
    return cos_t, sin_t, intra, inter, sdec, cdec


RWP_TM = 1024
RWP_SUB = 256


def _seg_sum(x, bd):
    hi = x.astype(BF16)
    lo = (x - hi.astype(F32)).astype(BF16)
    return jnp.dot(hi, bd, preferred_element_type=F32) + jnp.dot(lo, bd, preferred_element_type=F32)


def _rwkv_prep(x, in_g, w_in, mu, w_up, w0, a_up, a0, g_up, k_k, k_a, r_k, bd):
    bsz, t, d = x.shape
    tm, hm, W = RWP_TM, RWP_SUB, RW_W

    def body(x_ref, ing_ref, win_ref, mu_ref, wup_ref, w0_ref, aup_ref, a0_ref, gup_ref, kk_ref, ka_ref, rk_ref,
             bd_ref, r_out, lw_out, k_out, v_out, a_out, b_out, g_out, bonus_out, last_ref):
        c = pl.program_id(1)

        @pl.when(c == 0)
        def _():
            last_ref[...] = jnp.zeros_like(last_ref)

        def project(i):
            xs = x_ref[0, i * hm:(i + 1) * hm, :]
            return jnp.dot(_rms_rows(xs, ing_ref[...]).astype(BF16), win_ref[...], preferred_element_type=F32)

        rowi = lax.broadcasted_iota(jnp.int32, (hm, 1), 0)
        prev = last_ref[...]
        nxt = project(0)
        for i in range(tm // hm):
            rows = slice(i * hm, (i + 1) * hm)
            cur = nxt
            if i + 1 < tm // hm:
                nxt = project(i + 1)
            sh = jnp.where(rowi == 0, prev, pltpu.roll(cur, 1, 0))
            prev = cur[hm - 1:hm, :]
            xm = cur + (sh - cur) * mu_ref[...]
            x_r = xm[:, 0:W]
            x_k = xm[:, W:2 * W]
            x_v = xm[:, 2 * W:3 * W]
            x_dl = xm[:, 3 * W:3 * W + 64]
            x_al = xm[:, 3 * W + 64:3 * W + 128]
            x_gl = xm[:, 3 * W + 128:3 * W + 256]
            wl = w0_ref[...] + _dot(jnp.tanh(x_dl), wup_ref[...])
            sp = jnp.maximum(-wl, 0.0) + jnp.log(1.0 + jnp.exp(-jnp.abs(wl)))
            lw_out[0, rows, :] = -jnp.exp(-sp - 0.5)
            a = _sigmoid(a0_ref[...] + _dot(x_al, aup_ref[...]))
            g_out[0, rows, :] = _dot(_sigmoid(x_gl), gup_ref[...]).astype(BF16)
            kk0 = x_k * kk_ref[...]
            nrm = jnp.sqrt(_seg_sum(kk0 * kk0, bd_ref[...]))
            kk = kk0 / jnp.maximum(nrm, 1e-12)
            k_h = x_k * (1.0 + (a - 1.0) * ka_ref[...])
            r_out[0, rows, :] = x_r.astype(BF16)
            k_out[0, rows, :] = k_h.astype(BF16)
            v_out[0, rows, :] = x_v.astype(BF16)
            a_out[0, rows, :] = (-kk).astype(BF16)
            b_out[0, rows, :] = (kk * a).astype(BF16)
            bonus_out[0, rows, :] = (_dot(x_r * k_h * rk_ref[...], bd_ref[...]) * x_v).astype(BF16)
        last_ref[...] = prev

    row = lambda b, c: (0, 0)
    blk = pl.BlockSpec((1, tm, W), lambda b, c: (b, c, 0))
    shp = [jax.ShapeDtypeStruct((bsz, t, W), F32 if i == 1 else BF16) for i in range(8)]
    return pl.pallas_call(
        body,
        grid=(bsz, t // tm),
        in_specs=[pl.BlockSpec((1, tm, d), lambda b, c: (b, c, 0)),
                  pl.BlockSpec((1, d), row),
                  pl.BlockSpec((d, RW_COLS), row),
                  pl.BlockSpec((1, RW_COLS), row),
                  pl.BlockSpec((64, W), row), pl.BlockSpec((1, W), row),
                  pl.BlockSpec((64, W), row), pl.BlockSpec((1, W), row),
                  pl.BlockSpec((128, W), row),
                  pl.BlockSpec((1, W), row), pl.BlockSpec((1, W), row), pl.BlockSpec((1, W), row),
                  pl.BlockSpec((W, W), row)],
        out_specs=[blk] * 8,
        out_shape=shp,
        scratch_shapes=[pltpu.VMEM((1, RW_COLS), F32)],
        compiler_params=_cp("parallel", "arbitrary"),
        name="rwkv_prep",
    )(x, in_g, w_in, mu, w_up, w0, a_up, a0, g_up, k_k, k_a, r_k, bd)


RW_L = 64
RW_TB = 256


def _rwkv_scan(r, lw, k, v, aa, bb, g_out, bonus, ln_g, ln_b):
    bsz, t, W = r.shape
    L, N, tb = RW_L, RW_DIM, RW_TB
    nck = tb // L

    def body(r_ref, lw_ref, k_ref, v_ref, a_ref, b_ref, g_ref, bonus_ref, lng_ref, lnb_ref, out_ref, h_ref):
        c = pl.program_id(1)

        @pl.when(c == 0)
        def _():
            h_ref[...] = jnp.zeros_like(h_ref)

        lw_all = lw_ref[0]
        rowi = lax.broadcasted_iota(jnp.int32, (tb, 1), 0) & (L - 1)
        cl = lw_all
        for s in (1, 2, 4, 8, 16, 32):
            cl = cl + jnp.where(rowi >= s, pltpu.roll(cl, s, 0), 0.0)
        cl_last = jnp.concatenate([jnp.broadcast_to(cl[(cc + 1) * L - 1:(cc + 1) * L, :], (L, W))
                                   for cc in range(nck)], axis=0)
        e_inv = jnp.exp(-cl)
        e_end = jnp.exp(cl_last - cl)
        p_end = jnp.exp(cl_last)
        at = a_ref[0] * jnp.exp(cl - lw_all)
        rt = r_ref[0] * jnp.exp(cl)
        bt = b_ref[0] * e_inv
        kt = k_ref[0] * e_inv
        b_end = b_ref[0] * e_end
        k_end = k_ref[0] * e_end
        v_all = v_ref[0]
        P2 = 2 * N
        pairs = [(cc, p) for cc in range(nck) for p in range(RW_HEADS // 2)]
        pr = range(len(pairs))

        def pb(z, pi):
            cc, p = pairs[pi]
            return z[cc * L:(cc + 1) * L, p * P2:(p + 1) * P2]

        def stack2(z):
            lane = lax.broadcasted_iota(jnp.int32, (1, z.shape[1]), 1) & (P2 - 1)
            return jnp.concatenate([jnp.where(lane < N, z, 0.0), jnp.where(lane >= N, z, 0.0)], axis=0)

        tcol = lax.broadcasted_iota(jnp.int32, (L, P2), 1) & (N - 1)
        trow = lax.broadcasted_iota(jnp.int32, (L, P2), 0)
        strict = trow > tcol
        lower = trow >= tcol
        r2 = lax.broadcasted_iota(jnp.int32, (P2, P2), 0)
        c2 = lax.broadcasted_iota(jnp.int32, (P2, P2), 1)
        same_head = (r2 // N) == (c2 // N)
        eye2 = r2 == c2
        x2 = [jnp.concatenate([pb(at, pi), pb(rt, pi)], axis=0) for pi in pr]
        m_b = [_dot_nt(x2[pi], stack2(pb(bt, pi))) for pi in pr]
        m_k = [_dot_nt(x2[pi], stack2(pb(kt, pi))) for pi in pr]
        ap = [jnp.where(strict, m_b[pi][0:L], 0.0) for pi in pr]
        a_rb = [jnp.where(lower, m_b[pi][L:2 * L], 0.0) for pi in pr]
        a_ak = [jnp.where(strict, m_k[pi][0:L], 0.0) for pi in pr]
        a_rk = [jnp.where(lower, m_k[pi][L:2 * L], 0.0) for pi in pr]
        vp = [_dot(jnp.concatenate([a_ak[pi], a_rk[pi]], axis=0), stack2(pb(v_all, pi))) for pi in pr]
        kv = [_dot(pb(k_end, pi).T, pb(v_all, pi)) for pi in pr]
        x = [jnp.concatenate([pb(at, pi), vp[pi][0:L]], axis=1) for pi in pr]
        for it in range(6):
            x = [x[pi] + _dot(ap[pi], stack2(x[pi])) for pi in pr]
            if it < 5:
                ap = [_dot(ap[pi], stack2(ap[pi])) for pi in pr]
        post1 = [_dot(a_rb[pi], stack2(x[pi])) for pi in pr]
        post2 = [_dot(pb(b_end, pi).T, x[pi]) for pi in pr]
        lhs = []
        y0s = []
        h_adds = []
        for pi in pr:
            q_hat = pb(rt, pi) + post1[pi][:, 0:P2]
            gmat = (jnp.where(same_head, post2[pi][:, 0:P2], 0.0)
                    + jnp.where(eye2, pb(p_end, pi)[0:1, :], 0.0))
            lhs.append(jnp.concatenate([q_hat, gmat], axis=0))
            y0s.append(post1[pi][:, P2:2 * P2] + vp[pi][L:2 * L])
            h_adds.append(jnp.where(same_head, post2[pi][:, P2:2 * P2] + kv[pi], 0.0))
        npair = RW_HEADS // 2
        h_st = [h_ref[p] for p in range(npair)]
        ys = [None] * len(pairs)
        for cc in range(nck):
            res = [_dot(lhs[cc * npair + p], h_st[p]) for p in range(npair)]
            for p in range(npair):
                pi = cc * npair + p
                ys[pi] = res[p][0:L] + y0s[pi]
                h_st[p] = res[p][L:L + P2] + h_adds[pi]
        for p in range(npair):
            h_ref[p] = h_st[p]
        seg = same_head.astype(BF16)
        mean = [_seg_sum(ys[pi], seg) * (1.0 / N) for pi in pr]
        yc = [ys[pi] - mean[pi] for pi in pr]
        var = [_seg_sum(yc[pi] * yc[pi], seg) * (1.0 / N) for pi in pr]
        for pi, (cc, p) in enumerate(pairs):
            yn = yc[pi] * lax.rsqrt(var[pi] + RW_LN_EPS)
            rows = slice(cc * L, (cc + 1) * L)
            cols = slice(p * P2, (p + 1) * P2)
            out_ref[0, rows, cols] = ((yn * lng_ref[:, cols] + lnb_ref[:, cols] + bonus_ref[0, rows, cols])
                                      * g_ref[0, rows, cols]).astype(BF16)

    blk = pl.BlockSpec((1, tb, W), lambda b, c: (b, c, 0))
    vec = pl.BlockSpec((1, W), lambda b, c: (0, 0))
    return pl.pallas_call(
        body,
        grid=(bsz, t // tb),
        in_specs=[blk] * 8 + [vec, vec],
        out_specs=blk,
        out_shape=jax.ShapeDtypeStruct((bsz, t, W), BF16),
        scratch_shapes=[pltpu.VMEM((RW_HEADS // 2, 2 * N, 2 * N), F32)],
        compiler_params=_cp("parallel", "arbitrary"),
        name="rwkv_scan",
    )(r, lw, k, v, aa, bb, g_out, bonus, ln_g, ln_b)


def _mix_out_route(a, b, w_bf16, resid, g, router_split, tm=1024):
    n, wa = a.shape
    wb = b.shape[1]
    d = w_bf16.shape[1]
    e = router_split.shape[1] // 2

    def body(a_ref, b_ref, w_ref, r_ref, g_ref, rt_ref, h_ref, xn_ref, lg_ref):
        acc = jnp.dot(a_ref[...], w_ref[0:wa, :], preferred_element_type=F32)
        acc = acc + jnp.dot(b_ref[...], w_ref[wa:wa + wb, :], preferred_element_type=F32)
        h = r_ref[...] + acc
        h_ref[...] = h
        xn = _rms_rows(h, g_ref[...])
        x_hi = xn.astype(BF16)
        xn_ref[...] = x_hi
        x_lo = (xn - x_hi.astype(F32)).astype(BF16)
        hi_both = jnp.dot(x_hi, rt_ref[...], preferred_element_type=F32)
        lo_hi = jnp.dot(x_lo, rt_ref[:, 0:e], preferred_element_type=F32)
        lg_ref[...] = hi_both[:, 0:e] + (lo_hi + hi_both[:, e:2 * e])

    return pl.pallas_call(
        body,
        grid=(n // tm,),
        in_specs=[pl.BlockSpec((tm, wa), lambda i: (i, 0)),
                  pl.BlockSpec((tm, wb), lambda i: (i, 0)),
                  pl.BlockSpec((wa + wb, d), lambda i: (0, 0)),
                  pl.BlockSpec((tm, d), lambda i: (i, 0)),
                  pl.BlockSpec((1, d), lambda i: (0, 0)),
                  pl.BlockSpec((d, 2 * e), lambda i: (0, 0))],
        out_specs=[pl.BlockSpec((tm, d), lambda i: (i, 0)), pl.BlockSpec((tm, d), lambda i: (i, 0)),
                   pl.BlockSpec((tm, e), lambda i: (i, 0))],
        out_shape=[jax.ShapeDtypeStruct((n, d), F32), jax.ShapeDtypeStruct((n, d), BF16),
                   jax.ShapeDtypeStruct((n, e), F32)],
        compiler_params=_cp("parallel"),
        name="mix_out_route",
    )(a, b, w_bf16, resid, g.reshape(1, d), router_split)


MOE_TM = 512
MOE_TF = 1792


def _experts(xs, row_w, item_tile, item_exp, item_lo, item_hi, wg, wu, wd):
    nrows, d = xs.shape
    tm, tf = MOE_TM, MOE_TF
    nf = D_FF // tf
    n_items = item_tile.shape[0]

    def body(it_ref, ie_ref, lo_ref, hi_ref, x_ref, w_ref, wg_ref, wu_ref, wd_ref, o_ref, acc_ref):
        i = pl.program_id(0)
        j = pl.program_id(1)
        tile = it_ref[i]
        first = jnp.logical_or(i == 0, tile != it_ref[jnp.maximum(i - 1, 0)])
        last = jnp.logical_or(i == n_items - 1, tile != it_ref[jnp.minimum(i + 1, n_items - 1)])

        @pl.when(jnp.logical_and(first, j == 0))
        def _():
            acc_ref[...] = jnp.zeros_like(acc_ref)

        lo = lo_ref[i]
        hi = hi_ref[i]

        @pl.when(lo < hi)
        def _():
            x = x_ref[...]
            gg = jnp.dot(x, wg_ref[0], preferred_element_type=F32)
            uu = jnp.dot(x, wu_ref[0], preferred_element_type=F32)
            act = (_silu(gg) * uu).astype(BF16)
            part = jnp.dot(act, wd_ref[0], preferred_element_type=F32)
            rowi = lax.broadcasted_iota(jnp.int32, (tm, 1), 0)
            mine = jnp.logical_and(rowi >= lo, rowi < hi)
            acc_ref[...] += part * jnp.where(mine, w_ref[...], 0.0)

        @pl.when(jnp.logical_and(last, j == nf - 1))
        def _():
            o_ref[...] = acc_ref[...].astype(o_ref.dtype)

    grid_spec = pltpu.PrefetchScalarGridSpec(
        num_scalar_prefetch=4,
        grid=(n_items, nf),
        in_specs=[pl.BlockSpec((tm, d), lambda i, j, it, ie, lo, hi: (it[i], 0)),
                  pl.BlockSpec((tm, 1), lambda i, j, it, ie, lo, hi: (it[i], 0)),
                  pl.BlockSpec((1, d, tf), lambda i, j, it, ie, lo, hi: (ie[i], 0, j)),
                  pl.BlockSpec((1, d, tf), lambda i, j, it, ie, lo, hi: (ie[i], 0, j)),
                  pl.BlockSpec((1, tf, d), lambda i, j, it, ie, lo, hi: (ie[i], j, 0))],
        out_specs=pl.BlockSpec((tm, d), lambda i, j, it, ie, lo, hi: (it[i], 0)),
        scratch_shapes=[pltpu.VMEM((tm, d), F32)],
    )
    return pl.pallas_call(
        body,
        grid_spec=grid_spec,
        out_shape=jax.ShapeDtypeStruct((nrows, d), BF16),
        compiler_params=_cp("arbitrary", "arbitrary"),
        name="moe_experts",
    )(item_tile, item_exp, item_lo, item_hi, xs, row_w, wg, wu, wd)


def _combine_norm(h, y0, y1, g, tm=512):
    n, d = h.shape

    def body(h_ref, a_ref, b_ref, g_ref, o_ref):
        o_ref[...] = _rms_rows(h_ref[...] + (a_ref[...].astype(F32) + b_ref[...].astype(F32)), g_ref[...])

    blk = pl.BlockSpec((tm, d), lambda i: (i, 0))
    return pl.pallas_call(
        body,
        grid=(n // tm,),
        in_specs=[blk, blk, blk, pl.BlockSpec((1, d), lambda i: (0, 0))],
        out_specs=blk,
        out_shape=jax.ShapeDtypeStruct((n, d), F32),
        compiler_params=_cp("parallel"),
        name="combine_norm",
    )(h, y0, y1, g.reshape(1, d))


def _route(logits, n):
    tm = MOE_TM
    na = n * TOP_K
    n_tiles = na // tm
    top_val, top_idx = lax.top_k(logits, TOP_K)
    top_w = jax.nn.softmax(top_val, axis=-1)
    e_flat = top_idx.reshape(-1).astype(jnp.int32)
    w_flat = top_w.reshape(-1)
    ids = jnp.arange(na, dtype=jnp.int32)
    e_sorted, sorted_a, sorted_w = lax.sort((e_flat, ids, w_flat), num_keys=1, is_stable=True)
    sorted_tok = sorted_a // TOP_K
    _, pos = lax.sort((sorted_a, ids), num_keys=1)
    ends = jnp.sum((e_sorted[None, :] <= jnp.arange(N_EXPERTS, dtype=jnp.int32)[:, None]).astype(jnp.int32), axis=1)
    cuts = jnp.sort(jnp.concatenate([jnp.arange(n_tiles, dtype=jnp.int32) * tm, ends[:-1].astype(jnp.int32)]))
    nxt = jnp.concatenate([cuts[1:], jnp.full((1,), na, jnp.int32)])
    item_tile = jnp.minimum(cuts // tm, n_tiles - 1)
    item_exp = jnp.minimum(jnp.sum((ends[None, :] <= cuts[:, None]).astype(jnp.int32), axis=1), N_EXPERTS - 1)
    item_lo = cuts - item_tile * tm
    item_hi = nxt - item_tile * tm
    return sorted_tok, sorted_w, (item_tile, item_exp, item_lo, item_hi), pos.reshape(n, TOP_K)


def kernel(x, e_norm1_g, e_w_in, e_ml_conv_w, e_ml_conv_b, e_ml_gate_b, e_ml_norm_g, e_gla_gate_up, e_gla_gate_b,
           e_gla_norm_g, e_w_out, e_norm2_g, e_ffn_w_gate, e_ffn_w_up, e_ffn_w_down, o_norm1_g, o_w_in,
           o_ret_norm_g, o_rw_mu, o_rw_w_up, o_rw_w0, o_rw_a_up, o_rw_a0, o_rw_g_up, o_rw_k_k, o_rw_k_a, o_rw_r_k,
           o_rw_ln_g, o_rw_ln_b, o_w_out, o_norm2_g, o_moe_router, o_moe_w_gate, o_moe_w_up, o_moe_w_down,
           final_norm_g):
    bsz, t, d = x.shape
    n = bsz * t
    h0 = x.reshape(n, d)

    w = e_w_in[0]
    row = lambda a: a.reshape(1, -1)
    x3 = x
    ng1 = row(e_norm1_g[0])
    w_if = w[:, 2048:2056]
    w_gate = jnp.zeros((d, 128), F32).at[:, :2 * ML_HEADS].set(w_if).astype(BF16)
    w_gate_t = jnp.zeros((16, d), F32).at[:2 * ML_HEADS, :].set(w_if.T).astype(BF16)
    h_ml = _mlstm(x3, ng1, w[:, :4 * ML_W].astype(BF16), w_gate, w_gate_t, row(e_ml_gate_b[0]),
                  e_ml_gate_b[0].reshape(-1, 1), e_ml_conv_w[0], row(e_ml_conv_b[0]), row(e_ml_norm_g[0]))
    gq, gk, gv, gr = 2056, 2312, 2568, 3080
    w_pairs = jnp.stack([jnp.concatenate([w[:, gq + 128 * hp:gq + 128 * (hp + 1)],
                                          w[:, gk + 128 * hp:gk + 128 * (hp + 1)],
                                          w[:, gv + 256 * hp:gv + 256 * (hp + 1)],
                                          w[:, gr + 256 * hp:gr + 256 * (hp + 1)]], axis=1)
                         for hp in range(GLA_HEADS // 2)]).astype(BF16)
    w_low = jnp.zeros((d, 128), F32).at[:, :GLA_RANK].set(w[:, 3592:3608]).astype(BF16)
    o_gla = _gla(x3, ng1, w_pairs, w_low, e_gla_gate_up[0].astype(BF16), row(e_gla_gate_b[0]),
                 row(e_gla_norm_g[0]))
    h2 = _mix_out_ffn(h_ml.reshape(n, -1), o_gla.reshape(n, -1), e_w_out[0].astype(BF16), h0, e_norm2_g[0],
                      e_ffn_w_gate[0].astype(BF16), e_ffn_w_up[0].astype(BF16), e_ffn_w_down[0].astype(BF16))

    w = o_w_in[0]
    h2_3 = h2.reshape(bsz, t, d)
    ng2 = row(o_norm1_g[0])
    y_ret = _retention(h2_3, ng2, w[:, :4 * RET_W].astype(BF16), *_retention_tables(t), row(o_ret_norm_g[0]))
    head_of = jnp.arange(RW_W) // RW_DIM
    bd = (head_of[:, None] == head_of[None, :]).astype(BF16)
    r, lw, k, v, aa, bb, g_out, bonus = _rwkv_prep(
        h2_3, ng2, w[:, 4 * RET_W:].astype(BF16), row(o_rw_mu[0]), o_rw_w_up[0].astype(BF16), row(o_rw_w0[0]),
        o_rw_a_up[0].astype(BF16), row(o_rw_a0[0]), o_rw_g_up[0].astype(BF16), row(o_rw_k_k[0]),
        row(o_rw_k_a[0]), row(o_rw_r_k[0]), bd)
    y_rw = _rwkv_scan(r, lw, k, v, aa, bb, g_out, bonus, row(o_rw_ln_g[0]), row(o_rw_ln_b[0]))
    router_pad = jnp.zeros((d, 128), F32).at[:, :N_EXPERTS].set(o_moe_router[0])
    router_hi = router_pad.astype(BF16)
    router_split = jnp.concatenate([router_hi, (router_pad - router_hi.astype(F32)).astype(BF16)], axis=1)
    h3, xn, logits = _mix_out_route(y_ret.reshape(n, -1), y_rw.reshape(n, -1), o_w_out[0].astype(BF16), h2,
                                    o_norm2_g[0], router_split)
    sorted_tok, sorted_w, items, pos = _route(logits[:, :N_EXPERTS], n)
    xs = xn.at[sorted_tok].get(mode="promise_in_bounds")
    ys = _experts(xs, sorted_w.reshape(-1, 1), *items, o_moe_w_gate[0].astype(BF16),
                  o_moe_w_up[0].astype(BF16), o_moe_w_down[0].astype(BF16))
    y0 = ys.at[pos[:, 0]].get(mode="promise_in_bounds")
    y1 = ys.at[pos[:, 1]].get(mode="promise_in_bounds")
    out = _combine_norm(h3, y0, y1, final_norm_g)
    return out.reshape(bsz, t, d)
```
